```python
import jax, jax.numpy as jnp
from jax import lax
import numpy as np

D_MODEL = 1024
BATCH = 8
SEQ = 4096
DEPTH = 1

SSM_HEADS = 16
SSM_HEAD_DIM = 64
SSM_WIDTH = SSM_HEADS * SSM_HEAD_DIM
SSM_GROUPS = 2
SSM_STATE = 128
CONV_WIDTH = 4
CHUNK = 128
CONV_DIM = SSM_WIDTH + 2 * SSM_GROUPS * SSM_STATE
ATT_HEADS = 16
ATT_HEAD_DIM = 64
ATT_WIDTH = ATT_HEADS * ATT_HEAD_DIM
Q_BLOCK = 128
MIX_WIDTH = SSM_WIDTH + ATT_WIDTH
D_FF = 4 * D_MODEL
SPLITS = [SSM_WIDTH,
          SSM_WIDTH + CONV_DIM,
          SSM_WIDTH + CONV_DIM + SSM_HEADS,
          SSM_WIDTH + CONV_DIM + SSM_HEADS + ATT_WIDTH,
          SSM_WIDTH + CONV_DIM + SSM_HEADS + 2 * ATT_WIDTH,
          SSM_WIDTH + CONV_DIM + SSM_HEADS + 3 * ATT_WIDTH]
IN_COLS = SPLITS[-1] + ATT_HEADS
DEEPNORM_ALPHA = (2.0 * DEPTH) ** 0.25
DEEPNORM_BETA = (8.0 * DEPTH) ** -0.25
LN_EPS = 1e-5
RMS_EPS = 1e-5

kernel_name = "hymba_ssd_fox_deepnorm_adaln"


def layer_norm(x, g, b):
    xf = x.astype(jnp.float32)
    mu = jnp.mean(xf, axis=-1, keepdims=True)
    var = jnp.mean(jnp.square(xf - mu), axis=-1, keepdims=True)
    return ((xf - mu) * lax.rsqrt(var + LN_EPS) * g + b).astype(x.dtype)


def rms_norm(x, w):
    xf = x.astype(jnp.float32)
    return xf * lax.rsqrt(jnp.mean(xf * xf, axis=-1, keepdims=True) + RMS_EPS) * w


def causal_depthwise_conv(u, w, b):
    out = lax.conv_general_dilated(
        u, w[:, None, :].astype(u.dtype), window_strides=(1,),
        padding=[(CONV_WIDTH - 1, 0)], dimension_numbers=('NWC', 'WIO', 'NWC'),
        feature_group_count=u.shape[-1])
    return out + b


def segsum(a):
    cs = jnp.cumsum(a, axis=-1)
    diff = cs[..., :, None] - cs[..., None, :]
    t = a.shape[-1]
    mask = jnp.tril(jnp.ones((t, t), dtype=bool))
    return jnp.where(mask, diff, -jnp.inf)


def ssd_chunked(xh, dt, A, Bm, Cm):
    b, s, h, p = xh.shape
    g, n = Bm.shape[-2], Bm.shape[-1]
    e = h // g
    nc = s // CHUNK
    xc = (xh.astype(jnp.float32) * dt[..., None]).reshape(b, nc, CHUNK, g, e, p)
    Bc = Bm.astype(jnp.float32).reshape(b, nc, CHUNK, g, n)
    Cc = Cm.astype(jnp.float32).reshape(b, nc, CHUNK, g, n)
    a = (dt * A).reshape(b, nc, CHUNK, g, e).transpose(0, 3, 4, 1, 2)
    a_cs = jnp.cumsum(a, axis=-1)
    decay_in = jnp.exp(segsum(a))
    cb = jnp.einsum('bclgn,bcsgn->bgcls', Cc, Bc)
    scores = cb[:, :, None] * decay_in
    y_diag = jnp.einsum('bgecls,bcsgep->bclgep', scores, xc)
    decay_to_end = jnp.exp(a_cs[..., -1:] - a_cs).transpose(0, 3, 4, 1, 2)
    states = jnp.einsum('bclgn,bclgep->bcgepn', Bc, xc * decay_to_end[..., None])
    chunk_tot = jnp.pad(a_cs[..., -1], [(0, 0), (0, 0), (0, 0), (1, 0)])
    decay_chunk = jnp.exp(segsum(chunk_tot))
    states = jnp.concatenate([jnp.zeros_like(states[:, :1]), states], axis=1)
    new_states = jnp.einsum('bgezc,bcgepn->bzgepn', decay_chunk, states)
    prev_states = new_states[:, :-1]
    decay_out = jnp.exp(a_cs).transpose(0, 3, 4, 1, 2)
    y_off = jnp.einsum('bclgn,bcgepn->bclgep', Cc, prev_states) * decay_out[..., None]
    return (y_diag + y_off).reshape(b, s, h, p)


def forgetting_attention(q, k, v, log_f):
    s, d = q.shape[1], q.shape[-1]
    scale = d ** -0.5
    cum = jnp.cumsum(log_f, axis=1).transpose(0, 2, 1)
    outs = []
    for i in range(s // Q_BLOCK):
        q0, q1 = i * Q_BLOCK, (i + 1) * Q_BLOCK
        logits = jnp.einsum('bqhd,bkhd->bhqk', q[:, q0:q1], k[:, :q1],
                            preferred_element_type=jnp.float32) * scale
        logits = logits + (cum[:, :, q0:q1, None] - cum[:, :, None, :q1])
        mask = jnp.arange(q0, q1)[:, None] >= jnp.arange(q1)[None, :]
        logits = jnp.where(mask, logits, -jnp.inf)
        probs = jax.nn.softmax(logits, axis=-1)
        outs.append(jnp.einsum('bhqk,bkhd->bqhd', probs.astype(v.dtype), v[:, :q1]))
    return jnp.concatenate(outs, axis=1)


def hybrid_mixer(h, w_in, conv_w, conv_b, dt_bias, a_log, d_skip, ssm_norm_w, f_bias,
                 attn_norm_w, w_out):
    b, s, _ = h.shape
    proj = jnp.einsum('bsd,dk->bsk', h, w_in)
    z, xbc, dt_raw, q, k, v, f_raw = jnp.split(proj, SPLITS, axis=-1)
    xbc = jax.nn.silu(causal_depthwise_conv(xbc, conv_w, conv_b))
    xs, Bm, Cm = jnp.split(xbc, [SSM_WIDTH, SSM_WIDTH + SSM_GROUPS * SSM_STATE], axis=-1)
    xs = xs.reshape(b, s, SSM_HEADS, SSM_HEAD_DIM)
    Bm = Bm.reshape(b, s, SSM_GROUPS, SSM_STATE)
    Cm = Cm.reshape(b, s, SSM_GROUPS, SSM_STATE)
    dt = jax.nn.softplus(dt_raw.astype(jnp.float32) + dt_bias)
    A = -jnp.exp(a_log.astype(jnp.float32))
    y = ssd_chunked(xs, dt, A, Bm, Cm) + d_skip[:, None] * xs
    y_ssm = rms_norm(y.reshape(b, s, SSM_WIDTH) * jax.nn.silu(z), ssm_norm_w)
    q = q.reshape(b, s, ATT_HEADS, ATT_HEAD_DIM)
    k = k.reshape(b, s, ATT_HEADS, ATT_HEAD_DIM)
    v = v.reshape(b, s, ATT_HEADS, ATT_HEAD_DIM)
    log_f = jax.nn.log_sigmoid(f_raw.astype(jnp.float32) + f_bias)
    y_att = forgetting_attention(q, k, v, log_f).reshape(b, s, ATT_WIDTH)
    y_att = rms_norm(y_att, attn_norm_w)
    y_mix = jnp.concatenate([y_ssm, y_att.astype(y_ssm.dtype)], axis=-1)
    return jnp.einsum('bsk,kd->bsd', y_mix.astype(h.dtype), w_out)


def _fwd_setup_inputs(seed: int = 0) -> dict:
    key = jax.random.key(seed)
    ks = jax.random.split(key, 24)
    f32 = jnp.float32
    nrm = lambda k, shape, s: jax.random.normal(k, shape, f32) * s
    dt0 = jnp.exp(jax.random.uniform(ks[6], (DEPTH, SSM_HEADS), f32,
                                     np.log(1e-3).astype(np.float32), np.log(1e-1).astype(np.float32)))
    return {
        "x": nrm(ks[0], (BATCH, SEQ, D_MODEL), 1.0),
        "c": nrm(ks[1], (BATCH, D_MODEL), 1.0),
        "w_ada": nrm(ks[2], (DEPTH, D_MODEL, 6 * D_MODEL), 0.5 * D_MODEL ** -0.5),
        "b_ada": nrm(ks[3], (DEPTH, 6 * D_MODEL), 0.01),
        "w_in": nrm(ks[4], (DEPTH, D_MODEL, IN_COLS), D_MODEL ** -0.5),
        "conv_w": nrm(ks[5], (DEPTH, CONV_WIDTH, CONV_DIM), CONV_WIDTH ** -0.5),
        "conv_b": nrm(ks[7], (DEPTH, CONV_DIM), 0.01),
        "dt_bias": dt0 + jnp.log(-jnp.expm1(-dt0)),
        "a_log": jnp.log(jax.random.uniform(ks[8], (DEPTH, SSM_HEADS), f32, 1.0, 16.0)),
        "d_skip": 1.0 + nrm(ks[9], (DEPTH, SSM_HEADS), 0.1),
        "ssm_norm_w": 1.0 + nrm(ks[10], (DEPTH, SSM_WIDTH), 0.05),
        "f_bias": jax.random.uniform(ks[11], (DEPTH, ATT_HEADS), f32, 1.0, 4.0),
        "attn_norm_w": 1.0 + nrm(ks[12], (DEPTH, ATT_WIDTH), 0.05),
        "w_out": nrm(ks[13], (DEPTH, MIX_WIDTH, D_MODEL), DEEPNORM_BETA * MIX_WIDTH ** -0.5),
        "ln1_g": 1.0 + nrm(ks[14], (DEPTH, D_MODEL), 0.05),
        "ln1_b": nrm(ks[15], (DEPTH, D_MODEL), 0.01),
        "w_ff_in": nrm(ks[16], (DEPTH, D_MODEL, D_FF), D_MODEL ** -0.5),
        "w_ff_out": nrm(ks[17], (DEPTH, D_FF, D_MODEL), DEEPNORM_BETA * D_FF ** -0.5),
        "ln2_g": 1.0 + nrm(ks[18], (DEPTH, D_MODEL), 0.05),
        "ln2_b": nrm(ks[19], (DEPTH, D_MODEL), 0.01),
    }


def _fwd_reference(x, c, w_ada, b_ada, w_in, conv_w, conv_b, dt_bias, a_log, d_skip, ssm_norm_w,
              f_bias, attn_norm_w, w_out, ln1_g, ln1_b, w_ff_in, w_ff_out, ln2_g, ln2_b):
    c_act = jax.nn.silu(c)
    for l in range(DEPTH):
        mod = jnp.einsum('bd,de->be', c_act, w_ada[l]) + b_ada[l]
        sh1, sc1, g1, sh2, sc2, g2 = [m[:, None, :] for m in jnp.split(mod, 6, axis=-1)]
        h = x * (1.0 + sc1) + sh1
        y = hybrid_mixer(h, w_in[l], conv_w[l], conv_b[l], dt_bias[l], a_log[l], d_skip[l],
                         ssm_norm_w[l], f_bias[l], attn_norm_w[l], w_out[l])
        x = layer_norm(DEEPNORM_ALPHA * x + (1.0 + g1) * y, ln1_g[l], ln1_b[l])
        h = x * (1.0 + sc2) + sh2
        ff = jnp.einsum('bsf,fd->bsd',
                        jnp.square(jax.nn.relu(jnp.einsum('bsd,df->bsf', h, w_ff_in[l]))),
                        w_ff_out[l])
        x = layer_norm(DEEPNORM_ALPHA * x + (1.0 + g2) * ff, ln2_g[l], ln2_b[l])
    return x


import jax as _jax
import jax.numpy as _jnp

TWIN_FORMAT = 'train_step'
FWD_PARAMS = ['x', 'c', 'w_ada', 'b_ada', 'w_in', 'conv_w', 'conv_b', 'dt_bias', 'a_log', 'd_skip', 'ssm_norm_w', 'f_bias', 'attn_norm_w', 'w_out', 'ln1_g', 'ln1_b', 'w_ff_in', 'w_ff_out', 'ln2_g', 'ln2_b']
TWIN_WEIGHTS = ['w_ada', 'b_ada', 'w_in', 'conv_w', 'conv_b', 'dt_bias', 'a_log', 'd_skip', 'ssm_norm_w', 'f_bias', 'attn_norm_w', 'w_out', 'ln1_g', 'ln1_b', 'w_ff_in', 'w_ff_out', 'ln2_g', 'ln2_b']
TWIN_DIFF_INPUT = 'x'
TWIN_INPUTS = ['x', 'c', 'w_ada', 'b_ada', 'w_in', 'conv_w', 'conv_b', 'dt_bias', 'a_log', 'd_skip', 'ssm_norm_w', 'f_bias', 'attn_norm_w', 'w_out', 'ln1_g', 'ln1_b', 'w_ff_in', 'w_ff_out', 'ln2_g', 'ln2_b', 'loss_target', 'm_w_ada', 'm_b_ada', 'm_w_in', 'm_conv_w', 'm_conv_b', 'm_dt_bias', 'm_a_log', 'm_d_skip', 'm_ssm_norm_w', 'm_f_bias', 'm_attn_norm_w', 'm_w_out', 'm_ln1_g', 'm_ln1_b', 'm_w_ff_in', 'm_w_ff_out', 'm_ln2_g', 'm_ln2_b', 'v_w_ada', 'v_b_ada', 'v_w_in', 'v_conv_w', 'v_conv_b', 'v_dt_bias', 'v_a_log', 'v_d_skip', 'v_ssm_norm_w', 'v_f_bias', 'v_attn_norm_w', 'v_w_out', 'v_ln1_g', 'v_ln1_b', 'v_w_ff_in', 'v_w_ff_out', 'v_ln2_g', 'v_ln2_b']
TWIN_OUTPUTS = ['loss', 'grad_x', 'grad_w_ada', 'grad_b_ada', 'grad_w_in', 'grad_conv_w', 'grad_conv_b', 'grad_dt_bias', 'grad_a_log', 'grad_d_skip', 'grad_ssm_norm_w', 'grad_f_bias', 'grad_attn_norm_w', 'grad_w_out', 'grad_ln1_g', 'grad_ln1_b', 'grad_w_ff_in', 'grad_w_ff_out', 'grad_ln2_g', 'grad_ln2_b', 'delta_w_ada', 'delta_b_ada', 'delta_w_in', 'delta_conv_w', 'delta_conv_b', 'delta_dt_bias', 'delta_a_log', 'delta_d_skip', 'delta_ssm_norm_w', 'delta_f_bias', 'delta_attn_norm_w', 'delta_w_out', 'delta_ln1_g', 'delta_ln1_b', 'delta_w_ff_in', 'delta_w_ff_out', 'delta_ln2_g', 'delta_ln2_b', 'new_m_w_ada', 'new_m_b_ada', 'new_m_w_in', 'new_m_conv_w', 'new_m_conv_b', 'new_m_dt_bias', 'new_m_a_log', 'new_m_d_skip', 'new_m_ssm_norm_w', 'new_m_f_bias', 'new_m_attn_norm_w', 'new_m_w_out', 'new_m_ln1_g', 'new_m_ln1_b', 'new_m_w_ff_in', 'new_m_w_ff_out', 'new_m_ln2_g', 'new_m_ln2_b', 'new_v_w_ada', 'new_v_b_ada', 'new_v_w_in', 'new_v_conv_w', 'new_v_conv_b', 'new_v_dt_bias', 'new_v_a_log', 'new_v_d_skip', 'new_v_ssm_norm_w', 'new_v_f_bias', 'new_v_attn_norm_w', 'new_v_w_out', 'new_v_ln1_g', 'new_v_ln1_b', 'new_v_w_ff_in', 'new_v_w_ff_out', 'new_v_ln2_g', 'new_v_ln2_b']
TWIN_LEAF_KINDS = {'loss': 'loss', 'grad_x': 'grad_x', 'grad_w_ada': 'grad_w', 'grad_b_ada': 'grad_w', 'grad_w_in': 'grad_w', 'grad_conv_w': 'grad_w', 'grad_conv_b': 'grad_w', 'grad_dt_bias': 'grad_w', 'grad_a_log': 'grad_w', 'grad_d_skip': 'grad_w', 'grad_ssm_norm_w': 'grad_w', 'grad_f_bias': 'grad_w', 'grad_attn_norm_w': 'grad_w', 'grad_w_out': 'grad_w', 'grad_ln1_g': 'grad_w', 'grad_ln1_b': 'grad_w', 'grad_w_ff_in': 'grad_w', 'grad_w_ff_out': 'grad_w', 'grad_ln2_g': 'grad_w', 'grad_ln2_b': 'grad_w', 'delta_w_ada': 'delta_w', 'delta_b_ada': 'delta_w', 'delta_w_in': 'delta_w', 'delta_conv_w': 'delta_w', 'delta_conv_b': 'delta_w', 'delta_dt_bias': 'delta_w', 'delta_a_log': 'delta_w', 'delta_d_skip': 'delta_w', 'delta_ssm_norm_w': 'delta_w', 'delta_f_bias': 'delta_w', 'delta_attn_norm_w': 'delta_w', 'delta_w_out': 'delta_w', 'delta_ln1_g': 'delta_w', 'delta_ln1_b': 'delta_w', 'delta_w_ff_in': 'delta_w', 'delta_w_ff_out': 'delta_w', 'delta_ln2_g': 'delta_w', 'delta_ln2_b': 'delta_w', 'new_m_w_ada': 'new_m', 'new_m_b_ada': 'new_m', 'new_m_w_in': 'new_m', 'new_m_conv_w': 'new_m', 'new_m_conv_b': 'new_m', 'new_m_dt_bias': 'new_m', 'new_m_a_log': 'new_m', 'new_m_d_skip': 'new_m', 'new_m_ssm_norm_w': 'new_m', 'new_m_f_bias': 'new_m', 'new_m_attn_norm_w': 'new_m', 'new_m_w_out': 'new_m', 'new_m_ln1_g': 'new_m', 'new_m_ln1_b': 'new_m', 'new_m_w_ff_in': 'new_m', 'new_m_w_ff_out': 'new_m', 'new_m_ln2_g': 'new_m', 'new_m_ln2_b': 'new_m', 'new_v_w_ada': 'new_v', 'new_v_b_ada': 'new_v', 'new_v_w_in': 'new_v', 'new_v_conv_w': 'new_v', 'new_v_conv_b': 'new_v', 'new_v_dt_bias': 'new_v', 'new_v_a_log': 'new_v', 'new_v_d_skip': 'new_v', 'new_v_ssm_norm_w': 'new_v', 'new_v_f_bias': 'new_v', 'new_v_attn_norm_w': 'new_v', 'new_v_w_out': 'new_v', 'new_v_ln1_g': 'new_v', 'new_v_ln1_b': 'new_v', 'new_v_w_ff_in': 'new_v', 'new_v_w_ff_out': 'new_v', 'new_v_ln2_g': 'new_v', 'new_v_ln2_b': 'new_v'}


def _forward(args):
    return _fwd_reference(*[args[k] for k in FWD_PARAMS])


def _output_shape():
    def fwd():
        inp = _fwd_setup_inputs(0)
        return _fwd_reference(*[inp[k] for k in FWD_PARAMS])
    out = _jax.eval_shape(fwd)
    return out.shape, out.dtype

N_MICROBATCH = 1
ADAM_LR = 0.001
ADAM_B1 = 0.9
ADAM_B2 = 0.999
ADAM_EPS = 1e-08
ADAM_WD = 0.01
ADAM_STEP = 10
PER_EXAMPLE_BATCH_AXIS = {'x': 0, 'c': 0, 'loss_target': 0}
SHARED_INPUTS = []
_WEIGHT_DTYPES = {'w_ada': _jnp.float32, 'b_ada': _jnp.float32, 'w_in': _jnp.float32, 'conv_w': _jnp.float32, 'conv_b': _jnp.float32, 'dt_bias': _jnp.float32, 'a_log': _jnp.float32, 'd_skip': _jnp.float32, 'ssm_norm_w': _jnp.float32, 'f_bias': _jnp.float32, 'attn_norm_w': _jnp.float32, 'w_out': _jnp.float32, 'ln1_g': _jnp.float32, 'ln1_b': _jnp.float32, 'w_ff_in': _jnp.float32, 'w_ff_out': _jnp.float32, 'ln2_g': _jnp.float32, 'ln2_b': _jnp.float32}
MOMENT_SCALE = {'w_ada': 1.411707e-01, 'b_ada': 4.623985e-01, 'w_in': 5.941402e-02, 'conv_w': 5.528200e-02, 'conv_b': 7.793404e-02, 'dt_bias': 2.127119e-01, 'a_log': 2.136355e-01, 'd_skip': 2.555924e-01, 'ssm_norm_w': 6.427579e-02, 'f_bias': 2.149598e-01, 'attn_norm_w': 8.320333e-02, 'w_out': 1.910144e-01, 'ln1_g': 1.890348e+00, 'ln1_b': 6.803751e-01, 'w_ff_in': 6.841010e-02, 'w_ff_out': 3.470412e-01, 'ln2_g': 3.251295e+01, 'ln2_b': 8.572946e+00}


def _to_microbatches(a, axis):
    t = _jnp.moveaxis(a, axis, 0)
    t = t.reshape((N_MICROBATCH, t.shape[0] // N_MICROBATCH) + t.shape[1:])
    return _jnp.moveaxis(t, 1, axis + 1)


def setup_inputs(seed: int = 0) -> dict:
    inp = _fwd_setup_inputs(seed)
    key = _jax.random.fold_in(_jax.random.key(seed), 7919)
    shape, _ = _output_shape()
    out = dict(inp)
    out["loss_target"] = _jax.random.normal(_jax.random.fold_in(key, 0), shape, _jnp.float32)
    for i, name in enumerate(TWIN_WEIGHTS):
        w = inp[name].astype(_jnp.float32)
        if MOMENT_SCALE is None:
            s = _jnp.sqrt(_jnp.mean(_jnp.square(w)) + 1e-30)
        else:
            s = MOMENT_SCALE[name]
        km, kv = _jax.random.split(_jax.random.fold_in(key, i + 1))
        out[name] = w
        out["m_" + name] = s * _jax.random.normal(km, w.shape, _jnp.float32)
        out["v_" + name] = (s * s) * _jax.random.uniform(kv, w.shape, _jnp.float32, 0.5, 1.5)
    if N_MICROBATCH > 1:
        for name, axis in PER_EXAMPLE_BATCH_AXIS.items():
            out[name] = _to_microbatches(out[name], axis)
    return {'x': out['x'], 'c': out['c'], 'w_ada': out['w_ada'], 'b_ada': out['b_ada'], 'w_in': out['w_in'], 'conv_w': out['conv_w'], 'conv_b': out['conv_b'], 'dt_bias': out['dt_bias'], 'a_log': out['a_log'], 'd_skip': out['d_skip'], 'ssm_norm_w': out['ssm_norm_w'], 'f_bias': out['f_bias'], 'attn_norm_w': out['attn_norm_w'], 'w_out': out['w_out'], 'ln1_g': out['ln1_g'], 'ln1_b': out['ln1_b'], 'w_ff_in': out['w_ff_in'], 'w_ff_out': out['w_ff_out'], 'ln2_g': out['ln2_g'], 'ln2_b': out['ln2_b'], 'loss_target': out['loss_target'], 'm_w_ada': out['m_w_ada'], 'm_b_ada': out['m_b_ada'], 'm_w_in': out['m_w_in'], 'm_conv_w': out['m_conv_w'], 'm_conv_b': out['m_conv_b'], 'm_dt_bias': out['m_dt_bias'], 'm_a_log': out['m_a_log'], 'm_d_skip': out['m_d_skip'], 'm_ssm_norm_w': out['m_ssm_norm_w'], 'm_f_bias': out['m_f_bias'], 'm_attn_norm_w': out['m_attn_norm_w'], 'm_w_out': out['m_w_out'], 'm_ln1_g': out['m_ln1_g'], 'm_ln1_b': out['m_ln1_b'], 'm_w_ff_in': out['m_w_ff_in'], 'm_w_ff_out': out['m_w_ff_out'], 'm_ln2_g': out['m_ln2_g'], 'm_ln2_b': out['m_ln2_b'], 'v_w_ada': out['v_w_ada'], 'v_b_ada': out['v_b_ada'], 'v_w_in': out['v_w_in'], 'v_conv_w': out['v_conv_w'], 'v_conv_b': out['v_conv_b'], 'v_dt_bias': out['v_dt_bias'], 'v_a_log': out['v_a_log'], 'v_d_skip': out['v_d_skip'], 'v_ssm_norm_w': out['v_ssm_norm_w'], 'v_f_bias': out['v_f_bias'], 'v_attn_norm_w': out['v_attn_norm_w'], 'v_w_out': out['v_w_out'], 'v_ln1_g': out['v_ln1_g'], 'v_ln1_b': out['v_ln1_b'], 'v_w_ff_in': out['v_w_ff_in'], 'v_w_ff_out': out['v_w_ff_out'], 'v_ln2_g': out['v_ln2_g'], 'v_ln2_b': out['v_ln2_b']}


def _loss(weights, diff, rest, loss_target):
    with _jax.named_scope("forward"):
        args = {**rest, TWIN_DIFF_INPUT: diff, **{k: w.astype(_WEIGHT_DTYPES[k]) for k, w in weights.items()}}
        y = _forward(args)
    with _jax.named_scope("loss_head"):
        err = _jnp.square(y.astype(_jnp.float32) - loss_target)
        return 0.5 * _jnp.sum(_jnp.mean(err, axis=-1)) if err.ndim else 0.5 * err


def _adamw(w, g, m, v):
    m = ADAM_B1 * m + (1.0 - ADAM_B1) * g
    v = ADAM_B2 * v + (1.0 - ADAM_B2) * _jnp.square(g)
    m_hat = m / (1.0 - ADAM_B1 ** ADAM_STEP)
    v_hat = v / (1.0 - ADAM_B2 ** ADAM_STEP)
    delta = -ADAM_LR * (m_hat / (_jnp.sqrt(v_hat) + ADAM_EPS) + ADAM_WD * w)
    return delta, m, v


def reference(x, c, w_ada, b_ada, w_in, conv_w, conv_b, dt_bias, a_log, d_skip, ssm_norm_w, f_bias, attn_norm_w, w_out, ln1_g, ln1_b, w_ff_in, w_ff_out, ln2_g, ln2_b, loss_target, m_w_ada, m_b_ada, m_w_in, m_conv_w, m_conv_b, m_dt_bias, m_a_log, m_d_skip, m_ssm_norm_w, m_f_bias, m_attn_norm_w, m_w_out, m_ln1_g, m_ln1_b, m_w_ff_in, m_w_ff_out, m_ln2_g, m_ln2_b, v_w_ada, v_b_ada, v_w_in, v_conv_w, v_conv_b, v_dt_bias, v_a_log, v_d_skip, v_ssm_norm_w, v_f_bias, v_attn_norm_w, v_w_out, v_ln1_g, v_ln1_b, v_w_ff_in, v_w_ff_out, v_ln2_g, v_ln2_b):
    given = dict(x=x, c=c, w_ada=w_ada, b_ada=b_ada, w_in=w_in, conv_w=conv_w, conv_b=conv_b, dt_bias=dt_bias, a_log=a_log, d_skip=d_skip, ssm_norm_w=ssm_norm_w, f_bias=f_bias, attn_norm_w=attn_norm_w, w_out=w_out, ln1_g=ln1_g, ln1_b=ln1_b, w_ff_in=w_ff_in, w_ff_out=w_ff_out, ln2_g=ln2_g, ln2_b=ln2_b, loss_target=loss_target, m_w_ada=m_w_ada, m_b_ada=m_b_ada, m_w_in=m_w_in, m_conv_w=m_conv_w, m_conv_b=m_conv_b, m_dt_bias=m_dt_bias, m_a_log=m_a_log, m_d_skip=m_d_skip, m_ssm_norm_w=m_ssm_norm_w, m_f_bias=m_f_bias, m_attn_norm_w=m_attn_norm_w, m_w_out=m_w_out, m_ln1_g=m_ln1_g, m_ln1_b=m_ln1_b, m_w_ff_in=m_w_ff_in, m_w_ff_out=m_w_ff_out, m_ln2_g=m_ln2_g, m_ln2_b=m_ln2_b, v_w_ada=v_w_ada, v_b_ada=v_b_ada, v_w_in=v_w_in, v_conv_w=v_conv_w, v_conv_b=v_conv_b, v_dt_bias=v_dt_bias, v_a_log=v_a_log, v_d_skip=v_d_skip, v_ssm_norm_w=v_ssm_norm_w, v_f_bias=v_f_bias, v_attn_norm_w=v_attn_norm_w, v_w_out=v_w_out, v_ln1_g=v_ln1_g, v_ln1_b=v_ln1_b, v_w_ff_in=v_w_ff_in, v_w_ff_out=v_w_ff_out, v_ln2_g=v_ln2_g, v_ln2_b=v_ln2_b)
    weights = {n: given[n] for n in TWIN_WEIGHTS}
    shared = {n: given[n] for n in SHARED_INPUTS}
    per_example = {n: given[n] for n in ['x', 'c']}
    grad_fn = _jax.value_and_grad(_loss, argnums=(0, 1))

    def one_microbatch(ex, loss_target):
        ex = dict(ex)
        diff = ex.pop(TWIN_DIFF_INPUT)
        return grad_fn(weights, diff, {**shared, **ex}, loss_target)

    if N_MICROBATCH == 1:
        loss, (grad_w, grad_x) = one_microbatch(per_example, given["loss_target"])
    else:
        def body(carry, xs):
            loss_sum, grad_sum = carry
            l_k, (gw_k, gx_k) = one_microbatch(xs[0], xs[1])
            with _jax.named_scope("update"):
                return (loss_sum + l_k, _jax.tree.map(_jnp.add, grad_sum, gw_k)), gx_k

        init = (_jnp.zeros((), _jnp.float32), _jax.tree.map(_jnp.zeros_like, weights))
        (loss, grad_w), grad_x = _jax.lax.scan(body, init, (per_example, given["loss_target"]))
    with _jax.named_scope("update"):
        delta_w, new_m, new_v = {}, {}, {}
        for n in TWIN_WEIGHTS:
            delta_w[n], new_m[n], new_v[n] = _adamw(weights[n], grad_w[n], given["m_" + n], given["v_" + n])
    return (loss, grad_x, *[grad_w[n] for n in TWIN_WEIGHTS], *[delta_w[n] for n in TWIN_WEIGHTS],
            *[new_m[n] for n in TWIN_WEIGHTS], *[new_v[n] for n in TWIN_WEIGHTS])
```

```python
import functools

import jax
import jax.numpy as jnp
from jax import lax
from jax.experimental import pallas as pl
from jax.experimental.pallas import tpu as pltpu

F32, BF16 = jnp.float32, jnp.bfloat16

D_MODEL = 1024
N_HEADS = 16
HEAD_DIM = 64
N_PAIRS = N_HEADS // 2
SSM_GROUPS = 2
SSM_STATE = 128
CHUNK = 128
CONV_W = 4
CONV_DIM = 1536
D_FF = 4096
IN_COLS = 5664
ALPHA = 2.0 ** 0.25
LN_EPS = 1e-5
RMS_EPS = 1e-5
LANES = 128
SUBLANES = 8

AL_Z, AL_XS, AL_B, AL_C, AL_Q, AL_K, AL_V, AL_DTF = 0, 1024, 2048, 2304, 2560, 3584, 4608, 5632
AL_COLS = 5760
F_LANE = 16

ADAM_LR, ADAM_B1, ADAM_B2, ADAM_EPS, ADAM_WD, ADAM_STEP = 0.001, 0.9, 0.999, 1e-08, 0.01, 10

VMEM_LIMIT = 56 * 1024 * 1024
MESH = pl.DeviceIdType.MESH


def _params(sem=None):
    return pltpu.CompilerParams(dimension_semantics=sem, vmem_limit_bytes=VMEM_LIMIT)


def _sigmoid(x):
    return 1.0 / (1.0 + jnp.exp(-x))


def _silu(x):
    return x * _sigmoid(x)


def _softplus(x):
    return jnp.maximum(x, 0.0) + jnp.log(1.0 + jnp.exp(-jnp.abs(x)))


def _split3(a):
    hi = a.astype(BF16)
    r = a - hi.astype(F32)
    mid = r.astype(BF16)
    lo = (r - mid.astype(F32)).astype(BF16)
    return hi, mid, lo


def _dot(a, b, dims=((1,), (0,))):
    return lax.dot_general(a, b, (dims, ((), ())), preferred_element_type=F32)


NN, NT, TN = ((1,), (0,)), ((1,), (1,)), ((0,), (0,))


def _dot3(t, a):
    hi, mid, lo = _split3(a)
    return _dot(t, hi) + _dot(t, mid) + _dot(t, lo)


def _matmul(name, a, b, *, dims=NN, out_dtype=F32, tm=1024, tn=1024, tk=1024):
    if dims == NN:
        (m, k), n = a.shape, b.shape[1]
    elif dims == NT:
        (m, k), n = a.shape, b.shape[0]
    else:
        (k, m), n = a.shape, b.shape[1]
    tm, tn, tk = min(tm, m), min(tn, n), min(tk, k)
    assert m % tm == 0 and n % tn == 0 and k % tk == 0, (name, m, n, k, tm, tn, tk)
    nk = k // tk
    a_spec = pl.BlockSpec((tk, tm), lambda i, j, l: (l, i)) if dims == TN else pl.BlockSpec((tm, tk), lambda i, j, l: (i, l))
    b_spec = pl.BlockSpec((tn, tk), lambda i, j, l: (j, l)) if dims == NT else pl.BlockSpec((tk, tn), lambda i, j, l: (l, j))

    def body(a_ref, b_ref, o_ref, acc_ref):
        l = pl.program_id(2)
        part = _dot(a_ref[...].astype(BF16), b_ref[...].astype(BF16), dims)

        @pl.when(l == 0)
        def _():
            acc_ref[...] = part

        @pl.when(l > 0)
        def _():
            acc_ref[...] += part

        @pl.when(l == nk - 1)
        def _():
            o_ref[...] = acc_ref[...].astype(o_ref.dtype)

    return pl.pallas_call(
        body, name=name, grid=(m // tm, n // tn, nk),
        in_specs=[a_spec, b_spec], out_specs=pl.BlockSpec((tm, tn), lambda i, j, l: (i, j)),
        out_shape=jax.ShapeDtypeStruct((m, n), out_dtype),
        scratch_shapes=[pltpu.VMEM((tm, tn), F32)],
        compiler_params=_params(("parallel", "parallel", "arbitrary")),
    )(a, b)


def _rowwise(name, fn, fulls, vecs, out_fulls, out_vecs, tr=256):
    fulls = [f if isinstance(f, tuple) else (f, f.shape[1], 0) for f in fulls]
    s = fulls[0][0].shape[0]
    tr = min(tr, s)
    nf, nv, nof, nov = len(fulls), len(vecs), len(out_fulls), len(out_vecs)
    in_specs = [pl.BlockSpec((tr, w), functools.partial(lambda i, cb: (i, cb), cb=cb)) for (_, w, cb) in fulls]
    in_specs += [pl.BlockSpec(v.shape, lambda i: (0, 0)) for v in vecs]
    out_shape = [jax.ShapeDtypeStruct((s, w), dt) for (w, dt) in out_fulls] + [jax.ShapeDtypeStruct((1, w), F32) for w in out_vecs]
    out_specs = [pl.BlockSpec((tr, w), lambda i: (i, 0)) for (w, _) in out_fulls] + [pl.BlockSpec((1, w), lambda i: (0, 0)) for w in out_vecs]

    def body(*refs):
        outs = refs[nf + nv:]
        of, ov = fn(*[r[...] for r in refs[:nf + nv]])
        for r, val in zip(outs[:nof], of):
            r[...] = val.astype(r.dtype)
        if nov:
            @pl.when(pl.program_id(0) == 0)
            def _():
                for r in outs[nof:]:
                    r[...] = jnp.zeros_like(r)
            for r, val in zip(outs[nof:], ov):
                r[...] += val

    res = pl.pallas_call(
        body, name=name, grid=(s // tr,), in_specs=in_specs, out_specs=out_specs, out_shape=out_shape,
        compiler_params=_params(("arbitrary",)),
    )(*[f[0] for f in fulls], *vecs)
    return res[:nof], res[nof:]


def _colsum(x):
    return jnp.sum(x, axis=0, keepdims=True)


def _rowmean(x):
    return jnp.mean(x, axis=-1, keepdims=True)


CONV_CB = 512
CONV_TR = 512


def _shift_down(u, halo, j):
    if j == 0:
        return u
    ru = pltpu.roll(u, j, 0)
    row8 = lax.broadcasted_iota(jnp.int32, halo.shape, 0)
    top = jnp.where(row8 < j, pltpu.roll(halo, j, 0), ru[:SUBLANES])
    return jnp.concatenate([top, ru[SUBLANES:]], axis=0)


def _shift_up(d, halo, j):
    if j == 0:
        return d
    tr = d.shape[0]
    rd = pltpu.roll(d, tr - j, 0)
    row8 = lax.broadcasted_iota(jnp.int32, halo.shape, 0)
    bot = jnp.where(row8 >= SUBLANES - j, pltpu.roll(halo, SUBLANES - j, 0), rd[tr - SUBLANES:])
    return jnp.concatenate([rd[:tr - SUBLANES], bot], axis=0)


def _conv_specs(s, tr, col0):
    cb0 = col0 // CONV_CB
    per8 = tr // SUBLANES
    blk = pl.BlockSpec((tr, CONV_CB), lambda cb, i: (i, cb0 + cb))
    prev = pl.BlockSpec((SUBLANES, CONV_CB), lambda cb, i: (jnp.maximum(i * per8 - 1, 0), cb0 + cb))
    return blk, prev


def _conv_pre(u, halo, w_ref, b_ref, first):
    halo = jnp.where(first, 0.0, halo)
    acc = b_ref[...] + w_ref[CONV_W - 1:CONV_W, :] * u
    shifted = [u]
    for j in range(1, CONV_W):
        sh = _shift_down(u, halo, j)
        shifted.append(sh)
        acc = acc + w_ref[CONV_W - 1 - j:CONV_W - j, :] * sh
    return acc, shifted


def _conv_fwd(proj, conv_w, conv_b):
    s = proj.shape[0]
    tr = min(CONV_TR, s)
    blk, prev = _conv_specs(s, tr, AL_XS)

    def body(u_ref, h_ref, w_ref, b_ref, o_ref):
        pre, _ = _conv_pre(u_ref[...], h_ref[...], w_ref, b_ref, pl.program_id(1) == 0)
        o_ref[...] = _silu(pre)

    return pl.pallas_call(
        body, name="conv_fwd", grid=(CONV_DIM // CONV_CB, s // tr),
        in_specs=[blk, prev, pl.BlockSpec((CONV_W, CONV_CB), lambda cb, i: (0, cb)), pl.BlockSpec((1, CONV_CB), lambda cb, i: (0, cb))],
        out_specs=pl.BlockSpec((tr, CONV_CB), lambda cb, i: (i, cb)),
        out_shape=jax.ShapeDtypeStruct((s, CONV_DIM), F32),
        compiler_params=_params(("parallel", "parallel")),
    )(proj, proj, conv_w, conv_b)


def _conv_bwd_pre(proj, conv_w, conv_b, dxc):
    s = proj.shape[0]
    tr = min(CONV_TR, s)
    blk, prev = _conv_specs(s, tr, AL_XS)

    def body(u_ref, h_ref, w_ref, b_ref, d_ref, dpre_ref, dw_ref, db_ref):
        i = pl.program_id(1)
        pre, shifted = _conv_pre(u_ref[...], h_ref[...], w_ref, b_ref, i == 0)
        sg = _sigmoid(pre)
        dpre = d_ref[...] * (sg * (1.0 + pre * (1.0 - sg)))
        dpre_ref[...] = dpre

        @pl.when(i == 0)
        def _():
            dw_ref[...] = jnp.zeros_like(dw_ref)
            db_ref[...] = jnp.zeros_like(db_ref)

        db_ref[...] += _colsum(dpre)
        for j in range(CONV_W):
            dw_ref[CONV_W - 1 - j:CONV_W - j, :] += _colsum(dpre * shifted[j])

    own = pl.BlockSpec((tr, CONV_CB), lambda cb, i: (i, cb))
    wspec = pl.BlockSpec((CONV_W, CONV_CB), lambda cb, i: (0, cb))
    bspec = pl.BlockSpec((1, CONV_CB), lambda cb, i: (0, cb))
    return pl.pallas_call(
        body, name="conv_bwd_pre", grid=(CONV_DIM // CONV_CB, s // tr),
        in_specs=[blk, prev, wspec, bspec, own], out_specs=[own, wspec, bspec],
        out_shape=[jax.ShapeDtypeStruct((s, CONV_DIM), F32), jax.ShapeDtypeStruct((CONV_W, CONV_DIM), F32),
                   jax.ShapeDtypeStruct((1, CONV_DIM), F32)],
        compiler_params=_params(("parallel", "arbitrary")),
    )(proj, proj, conv_w, conv_b, dxc)


def _conv_bwd_in(dpre, conv_w):
    s = dpre.shape[0]
    tr = min(CONV_TR, s)
    per8 = tr // SUBLANES
    last8 = s // SUBLANES - 1
    nb = s // tr

    def body(d_ref, n_ref, w_ref, o_ref):
        d = d_ref[...]
        halo = jnp.where(pl.program_id(1) == nb - 1, 0.0, n_ref[...])
        acc = w_ref[CONV_W - 1:CONV_W, :] * d
        for j in range(1, CONV_W):
            acc = acc + w_ref[CONV_W - 1 - j:CONV_W - j, :] * _shift_up(d, halo, j)
        o_ref[...] = acc.astype(o_ref.dtype)

    own = pl.BlockSpec((tr, CONV_CB), lambda cb, i: (i, cb))
    nxt = pl.BlockSpec((SUBLANES, CONV_CB), lambda cb, i: (jnp.minimum((i + 1) * per8, last8), cb))
    return pl.pallas_call(
        body, name="conv_bwd_in", grid=(CONV_DIM // CONV_CB, nb),
        in_specs=[own, nxt, pl.BlockSpec((CONV_W, CONV_CB), lambda cb, i: (0, cb))], out_specs=own,
        out_shape=jax.ShapeDtypeStruct((s, CONV_DIM), BF16),
        compiler_params=_params(("parallel", "parallel")),
    )(dpre, dpre, conv_w)


XC_B, XC_C = 1024, 1280


def _tile_iotas():
    row = lax.broadcasted_iota(jnp.int32, (CHUNK, LANES), 0)
    lane = lax.broadcasted_iota(jnp.int32, (CHUNK, LANES), 1)
    return row, lane


def _ssd_scalars(dtf_ref, bias_ref, alog_ref, row, lane):
    head = lane[:1] < N_HEADS
    raw = dtf_ref[...] + bias_ref[...]
    dt = _softplus(raw)
    a_neg = jnp.where(head, -jnp.exp(alog_ref[...]), 0.0)
    a = dt * a_neg
    tril = (row >= lane).astype(BF16)
    s = _dot3(tril, a)
    return raw, dt, a_neg, s


def _pair(v, j, lo):
    return jnp.where(lo, v[:, 2 * j:2 * j + 1], v[:, 2 * j + 1:2 * j + 2])


def _head_sum(x, lo, hh):
    return jnp.sum(jnp.where(lo == (hh == 0), x, 0.0), axis=1, keepdims=True)


def _decay_masks(s, h, row, lane):
    s_col = jnp.broadcast_to(s[:, h:h + 1], (CHUNK, LANES))
    s_row = s_col.T
    lm = jnp.where(row >= lane, jnp.exp(s_col - s_row), 0.0)
    lmt = jnp.where(row <= lane, jnp.exp(s_row - s_col), 0.0)
    return lm, lmt


def _ssd_fwd(xc_all, proj, dt_bias_l, a_log_l, d_exp):
    s_len = xc_all.shape[0]
    nc = s_len // CHUNK

    def body(x_ref, dtf_ref, bias_ref, alog_ref, dexp_ref, y_ref, prevs_ref, state_ref):
        @pl.when(pl.program_id(0) == 0)
        def _():
            state_ref[...] = jnp.zeros_like(state_ref)

        row, lane = _tile_iotas()
        lo = lane < HEAD_DIM
        _, dt, _, s = _ssd_scalars(dtf_ref, bias_ref, alog_ref, row, lane)
        tot = s[CHUNK - 1:CHUNK, :]
        for g in range(SSM_GROUPS):
            bg = x_ref[:, XC_B + g * SSM_STATE:XC_B + (g + 1) * SSM_STATE].astype(BF16)
            cg = x_ref[:, XC_C + g * SSM_STATE:XC_C + (g + 1) * SSM_STATE].astype(BF16)
            cb = _dot(cg, bg, NT)
            for j in range(g * 4, g * 4 + 4):
                xs_p = x_ref[:, j * LANES:(j + 1) * LANES]
                dt_p, s_p, tot_p = _pair(dt, j, lo), _pair(s, j, lo), _pair(tot, j, lo[:1])
                xc_p = xs_p * dt_p
                xc_b = xc_p.astype(BF16)
                yd = []
                for hh in range(2):
                    lm, _ = _decay_masks(s, 2 * j + hh, row, lane)
                    yd.append(_dot((cb * lm).astype(BF16), xc_b))
                prev = state_ref[j]
                prevs_ref[0, j] = prev
                yo = _dot(cg, prev.astype(BF16)) * jnp.exp(s_p)
                y_ref[:, j * LANES:(j + 1) * LANES] = jnp.where(lo, yd[0], yd[1]) + yo + dexp_ref[:, j * LANES:(j + 1) * LANES] * xs_p
                to_end = jnp.exp(tot_p - s_p)
                state_ref[j] = jnp.exp(tot_p) * prev + _dot(bg, (xc_p * to_end).astype(BF16), TN)

    vec = lambda w: pl.BlockSpec((1, w), lambda c: (0, 0))
    return pl.pallas_call(
        body, name="ssd_fwd", grid=(nc,),
        in_specs=[pl.BlockSpec((CHUNK, CONV_DIM), lambda c: (c, 0)), pl.BlockSpec((CHUNK, LANES), lambda c: (c, AL_DTF // LANES)),
                  vec(LANES), vec(LANES), vec(D_MODEL)],
        out_specs=[pl.BlockSpec((CHUNK, D_MODEL), lambda c: (c, 0)), pl.BlockSpec((1, N_PAIRS, SSM_STATE, LANES), lambda c: (c, 0, 0, 0))],
        out_shape=[jax.ShapeDtypeStruct((s_len, D_MODEL), F32), jax.ShapeDtypeStruct((nc, N_PAIRS, SSM_STATE, LANES), F32)],
        scratch_shapes=[pltpu.VMEM((N_PAIRS, SSM_STATE, LANES), F32)],
        compiler_params=_params(("arbitrary",)),
    )(xc_all, proj, dt_bias_l, a_log_l, d_exp)


def _ssd_bwd(xc_all, proj, dt_bias_l, a_log_l, d_exp, prevs, dy):
    s_len = xc_all.shape[0]
    nc = s_len // CHUNK

    def body(x_ref, dtf_ref, bias_ref, alog_ref, dexp_ref, prevs_ref, dy_ref, dx_ref, ddt_ref, da_ref, dd_ref, dbias_ref, dstate_ref):
        @pl.when(pl.program_id(0) == 0)
        def _():
            dstate_ref[...] = jnp.zeros_like(dstate_ref)
            da_ref[...] = jnp.zeros_like(da_ref)
            dd_ref[...] = jnp.zeros_like(dd_ref)
            dbias_ref[...] = jnp.zeros_like(dbias_ref)

        row, lane = _tile_iotas()
        lo = lane < HEAD_DIM
        last = row == CHUNK - 1
        raw, dt, a_neg, s = _ssd_scalars(dtf_ref, bias_ref, alog_ref, row, lane)
        tot = s[CHUNK - 1:CHUNK, :]
        ds_acc = jnp.zeros((CHUNK, LANES), F32)
        ddt_acc = jnp.zeros((CHUNK, LANES), F32)
        for g in range(SSM_GROUPS):
            bcol = slice(XC_B + g * SSM_STATE, XC_B + (g + 1) * SSM_STATE)
            ccol = slice(XC_C + g * SSM_STATE, XC_C + (g + 1) * SSM_STATE)
            bg = x_ref[:, bcol].astype(BF16)
            cg = x_ref[:, ccol].astype(BF16)
            cb = _dot(cg, bg, NT)
            cbt = _dot(bg, cg, NT)
            dcb = jnp.zeros((CHUNK, LANES), F32)
            dcbt = jnp.zeros((CHUNK, LANES), F32)
            db_acc = jnp.zeros((CHUNK, LANES), F32)
            dc_acc = jnp.zeros((CHUNK, LANES), F32)
            for j in range(g * 4, g * 4 + 4):
                cols = slice(j * LANES, (j + 1) * LANES)
                xs_p, dy_p = x_ref[:, cols], dy_ref[:, cols]
                dt_p, s_p, tot_p = _pair(dt, j, lo), _pair(s, j, lo), _pair(tot, j, lo[:1])
                xc_p = xs_p * dt_p
                xc_b, dy_b = xc_p.astype(BF16), dy_p.astype(BF16)
                e_p, f_p, etot_p = jnp.exp(s_p), jnp.exp(tot_p - s_p), jnp.exp(tot_p)
                prev, dnext = prevs_ref[0, j], dstate_ref[j]
                prev_b, dnext_b = prev.astype(BF16), dnext.astype(BF16)
                dd_ref[:, cols] += _colsum(dy_p * xs_p)
                dxs_p = dexp_ref[:, cols] * dy_p
                cp = _dot(cg, prev_b)
                gy = (dy_p * e_p).astype(BF16)
                dc_acc += _dot(gy, prev_b, NT)
                dstate_ref[j] = etot_p * dnext + _dot(cg, gy, TN)
                de = dy_p * cp * e_p
                bds = _dot(bg, dnext_b)
                db_acc += _dot((xc_p * f_p).astype(BF16), dnext_b, NT)
                dxc_p = bds * f_p
                df = bds * xc_p * f_p
                dtot_p = _colsum(dnext * prev) * etot_p + _colsum(df)
                dsl = de - df + jnp.where(last, dtot_p, 0.0)
                for hh in range(2):
                    h = 2 * j + hh
                    mine = lo == (hh == 0)
                    lm, lmt = _decay_masks(s, h, row, lane)
                    dy_h = jnp.where(mine, dy_p, 0.0).astype(BF16)
                    xc_h = jnp.where(mine, xc_p, 0.0).astype(BF16)
                    dm = _dot(dy_h, xc_b, NT)
                    dmt = _dot(xc_h, dy_b, NT)
                    mt = cbt * lmt
                    dxc_p += _dot(mt.astype(BF16), dy_h)
                    ds_h = (jnp.sum(dm * cb * lm, axis=1, keepdims=True) - jnp.sum(dmt * mt, axis=1, keepdims=True)
                            + _head_sum(dsl, lo, hh))
                    ds_acc += jnp.where(lane == h, ds_h, 0.0)
                    dcb += dm * lm
                    dcbt += dmt * lmt
                    ddt_acc += jnp.where(lane == h, _head_sum(dxc_p * xs_p, lo, hh), 0.0)
                dx_ref[:, cols] = dxs_p + dxc_p * dt_p
            dx_ref[:, ccol] = dc_acc + _dot(dcb.astype(BF16), bg)
            dx_ref[:, bcol] = db_acc + _dot(dcbt.astype(BF16), cg)
        triu = (row <= lane).astype(BF16)
        da = _dot3(triu, ds_acc)
        ddt = ddt_acc + da * a_neg
        da_ref[...] += _colsum(da * dt) * a_neg[:1]
        ddt_raw = jnp.where(lane < N_HEADS, ddt * _sigmoid(raw), 0.0)
        dbias_ref[...] += _colsum(ddt_raw)
        ddt_ref[...] = ddt_raw

    rev = lambda c: nc - 1 - c
    vec = lambda w: pl.BlockSpec((1, w), lambda c: (0, 0))
    return pl.pallas_call(
        body, name="ssd_bwd", grid=(nc,),
        in_specs=[pl.BlockSpec((CHUNK, CONV_DIM), lambda c: (rev(c), 0)), pl.BlockSpec((CHUNK, LANES), lambda c: (rev(c), AL_DTF // LANES)),
                  vec(LANES), vec(LANES), vec(D_MODEL),
                  pl.BlockSpec((1, N_PAIRS, SSM_STATE, LANES), lambda c: (rev(c), 0, 0, 0)),
                  pl.BlockSpec((CHUNK, D_MODEL), lambda c: (rev(c), 0))],
        out_specs=[pl.BlockSpec((CHUNK, CONV_DIM), lambda c: (rev(c), 0)), pl.BlockSpec((CHUNK, LANES), lambda c: (rev(c), 0)),
                   vec(LANES), vec(D_MODEL), vec(LANES)],
        out_shape=[jax.ShapeDtypeStruct((s_len, CONV_DIM), F32), jax.ShapeDtypeStruct((s_len, LANES), F32),
                   jax.ShapeDtypeStruct((1, LANES), F32), jax.ShapeDtypeStruct((1, D_MODEL), F32), jax.ShapeDtypeStruct((1, LANES), F32)],
        scratch_shapes=[pltpu.VMEM((N_PAIRS, SSM_STATE, LANES), F32)],
        compiler_params=_params(("arbitrary",)),
    )(xc_all, proj, dt_bias_l, a_log_l, d_exp, prevs, dy)


AUG_C, AUG_ONE = 64, 67
NEG = -1e30
ATT_T = 512


def _fox_cum(proj, f_bias_l):
    s_len = proj.shape[0]
    nc = s_len // CHUNK

    def body(dtf_ref, fb_ref, cum_ref):
        row, lane = _tile_iotas()
        tril = (row >= lane).astype(BF16)

        def step(c, carry):
            rows = pl.ds(pl.multiple_of(c * CHUNK, CHUNK), CHUNK)
            lf = -_softplus(-(dtf_ref[rows, :] + fb_ref[...]))
            lf = jnp.where(lane < N_HEADS, pltpu.roll(lf, LANES - F_LANE, 1), 0.0)
            cs = _dot3(tril, lf) + carry
            cum_ref[rows, :] = cs
            return cs[CHUNK - 1:CHUNK, :]

        lax.fori_loop(0, nc, step, jnp.zeros((1, LANES), F32))

    return pl.pallas_call(
        body, name="fox_cum", grid=(1,),
        in_specs=[pl.BlockSpec((s_len, LANES), lambda i: (0, AL_DTF // LANES)), pl.BlockSpec((1, LANES), lambda i: (0, 0))],
        out_specs=pl.BlockSpec((s_len, LANES), lambda i: (0, 0)),
        out_shape=jax.ShapeDtypeStruct((s_len, LANES), F32),
        compiler_params=_params(("arbitrary",)),
    )(proj, f_bias_l)


def _fox_cum_bwd(dcum, proj, f_bias_l, ddt_tile):
    s_len = proj.shape[0]
    nc = s_len // CHUNK

    def body(dcum_ref, dtf_ref, fb_ref, ddt_ref, out_ref, dfb_ref):
        row, lane = _tile_iotas()
        triu = (row <= lane).astype(BF16)
        is_f = (lane >= F_LANE) & (lane < F_LANE + N_HEADS)

        def step(t, carry):
            run, dfb = carry
            rows = pl.ds(pl.multiple_of((nc - 1 - t) * CHUNK, CHUNK), CHUNK)
            rc = _dot3(triu, dcum_ref[rows, :]) + run
            sg = _sigmoid(-(dtf_ref[rows, :] + fb_ref[...]))
            df = jnp.where(is_f, pltpu.roll(rc, F_LANE, 1) * sg, 0.0)
            out_ref[rows, :] = (df + ddt_ref[rows, :]).astype(out_ref.dtype)
            return rc[0:1, :], dfb + _colsum(df)

        _, dfb = lax.fori_loop(0, nc, step, (jnp.zeros((1, LANES), F32), jnp.zeros((1, LANES), F32)))
        dfb_ref[...] = dfb

    whole = pl.BlockSpec((s_len, LANES), lambda i: (0, 0))
    vec = pl.BlockSpec((1, LANES), lambda i: (0, 0))
    return pl.pallas_call(
        body, name="fox_cum_bwd", grid=(1,),
        in_specs=[whole, pl.BlockSpec((s_len, LANES), lambda i: (0, AL_DTF // LANES)), vec, whole],
        out_specs=[whole, vec],
        out_shape=[jax.ShapeDtypeStruct((s_len, LANES), BF16), jax.ShapeDtypeStruct((1, LANES), F32)],
        compiler_params=_params(("arbitrary",)),
    )(dcum, proj, f_bias_l, ddt_tile)


def _attn_prep(proj, cum):
    s_len = proj.shape[0]
    tr = min(512, s_len)

    def body(q_ref, k_ref, v_ref, cum_ref, qa_ref, ka_ref, vb_ref):
        p = pl.program_id(0)
        lane = lax.broadcasted_iota(jnp.int32, (tr, LANES), 1)
        lo = lane < HEAD_DIM
        c = cum_ref[...]
        c1 = c.astype(BF16).astype(F32)
        r = c - c1
        c2 = r.astype(BF16).astype(F32)
        c3 = (r - c2).astype(BF16).astype(F32)
        q, k = q_ref[...] * (HEAD_DIM ** -0.5), k_ref[...]
        for hh in range(2):
            col = lambda x: jnp.sum(jnp.where(lane == 2 * p + hh, x, 0.0), axis=1, keepdims=True)
            a1, a2, a3 = col(c1), col(c2), col(c3)
            qh = q if hh == 0 else pltpu.roll(q, HEAD_DIM, 1)
            kh = k if hh == 0 else pltpu.roll(k, HEAD_DIM, 1)
            q_aug = jnp.where(lane == AUG_C, a1, jnp.where(lane == AUG_C + 1, a2, jnp.where(lane == AUG_C + 2, a3,
                              jnp.where(lane < AUG_ONE + 3, 1.0, 0.0))))
            k_aug = jnp.where(lane < AUG_ONE, 1.0, jnp.where(lane == AUG_ONE, -a1, jnp.where(lane == AUG_ONE + 1, -a2,
                              jnp.where(lane == AUG_ONE + 2, -a3, 0.0))))
            qa_ref[hh] = jnp.where(lo, qh, q_aug).astype(BF16)
            ka_ref[hh] = jnp.where(lo, kh, k_aug).astype(BF16)
        vb_ref[...] = v_ref[...].astype(BF16)

    slab = lambda col0: pl.BlockSpec((tr, LANES), lambda p, i: (i, col0 // LANES + p))
    heads = pl.BlockSpec((2, tr, LANES), lambda p, i: (p, i, 0))
    return pl.pallas_call(
        body, name="attn_prep", grid=(N_PAIRS, s_len // tr),
        in_specs=[slab(AL_Q), slab(AL_K), slab(AL_V), pl.BlockSpec((tr, LANES), lambda p, i: (i, 0))],
        out_specs=[heads, heads, pl.BlockSpec((tr, LANES), lambda p, i: (i, p))],
        out_shape=[jax.ShapeDtypeStruct((N_HEADS, s_len, LANES), BF16), jax.ShapeDtypeStruct((N_HEADS, s_len, LANES), BF16),
                   jax.ShapeDtypeStruct((s_len, D_MODEL), BF16)],
        compiler_params=_params(("parallel", "parallel")),
    )(proj, proj, proj, cum)


def _attn_fwd(qa, ka, vb):
    s_len = vb.shape[0]
    t = min(ATT_T, s_len)

    def body(qa_ref, ka_ref, vb_ref, o_ref, lse_ref):
        i = pl.program_id(1)
        row = lax.broadcasted_iota(jnp.int32, (t, t), 0)
        col = lax.broadcasted_iota(jnp.int32, (t, t), 1)
        lo = lax.broadcasted_iota(jnp.int32, (t, LANES), 1) < HEAD_DIM
        outs, lses = [], []
        for hh in range(2):
            q = qa_ref[hh]

            def block(j, carry, masked, q=q, hh=hh):
                m, l, acc = carry
                rows = pl.ds(pl.multiple_of(j * t, t), t)
                s = _dot(q, ka_ref[hh, rows, :], NT)
                if masked:
                    s = jnp.where(row >= col, s, NEG)
                m_new = jnp.maximum(m, jnp.max(s, axis=1, keepdims=True))
                alpha = jnp.exp(m - m_new)
                p = jnp.exp(s - m_new)
                l = alpha * l + jnp.sum(p, axis=1, keepdims=True)
                acc = alpha * acc + _dot(p.astype(BF16), vb_ref[rows, :])
                return m_new, l, acc

            init = (jnp.full((t, 1), NEG, F32), jnp.zeros((t, 1), F32), jnp.zeros((t, LANES), F32))
            carry = lax.fori_loop(0, i, functools.partial(block, masked=False), init)
            m, l, acc = block(i, carry, True)
            outs.append(acc / l)
            lses.append(m + jnp.log(l))
        o_ref[...] = jnp.where(lo, outs[0], outs[1])
        lse_ref[...] = jnp.where(lo, lses[0], lses[1])

    out = pl.BlockSpec((t, LANES), lambda p, i: (i, p))
    return pl.pallas_call(
        body, name="attn_fwd", grid=(N_PAIRS, s_len // t),
        in_specs=[pl.BlockSpec((2, t, LANES), lambda p, i: (p, i, 0)), pl.BlockSpec((2, s_len, LANES), lambda p, i: (p, 0, 0)),
                  pl.BlockSpec((s_len, LANES), lambda p, i: (0, p))],
        out_specs=[out, out],
        out_shape=[jax.ShapeDtypeStruct((s_len, D_MODEL), F32), jax.ShapeDtypeStruct((s_len, D_MODEL), F32)],
        compiler_params=_params(("parallel", "parallel")),
    )(qa, ka, vb)


def _attn_bwd(qa, ka, vb, o, lse, do):
    s_len = vb.shape[0]
    t = min(ATT_T, s_len)
    nq = s_len // t

    def body(qa_ref, ka_ref, vb_ref, o_ref, lse_ref, do_ref, dqa_ref, dka_ref, dv_ref):
        j = pl.program_id(1)

        @pl.when(j == 0)
        def _():
            dqa_ref[...] = jnp.zeros_like(dqa_ref)

        row = lax.broadcasted_iota(jnp.int32, (t, t), 0)
        col = lax.broadcasted_iota(jnp.int32, (t, t), 1)
        lo = lax.broadcasted_iota(jnp.int32, (t, LANES), 1) < HEAD_DIM
        v = vb_ref[...]
        dv_tot = jnp.zeros((t, LANES), F32)
        for hh in range(2):
            k = ka_ref[hh]
            mine = lo == (hh == 0)

            def block(i, carry, masked, k=k, mine=mine, hh=hh):
                dk, dv = carry
                rows = pl.ds(pl.multiple_of(i * t, t), t)
                q = qa_ref[hh, rows, :]
                do_h = jnp.where(mine, do_ref[rows, :], 0.0)
                delta = jnp.sum(do_h * o_ref[rows, :], axis=1, keepdims=True)
                lse_h = lse_ref[rows, :][:, hh * HEAD_DIM:hh * HEAD_DIM + 1]
                s = _dot(q, k, NT)
                if masked:
                    s = jnp.where(row >= col, s, NEG)
                p = jnp.exp(s - lse_h)
                do_b = do_h.astype(BF16)
                ds = (p * (_dot(do_b, v, NT) - delta)).astype(BF16)
                dv = dv + _dot(p.astype(BF16), do_b, TN)
                dk = dk + _dot(ds, q, TN)
                dqa_ref[hh, rows, :] += _dot(ds, k)
                return dk, dv

            zero = jnp.zeros((t, LANES), F32)
            carry = block(j, (zero, zero), True)
            dk, dv = lax.fori_loop(j + 1, nq, functools.partial(block, masked=False), carry)
            dka_ref[hh] = dk
            dv_tot = dv_tot + dv
        dv_ref[...] = dv_tot.astype(dv_ref.dtype)

    whole_pair = pl.BlockSpec((2, s_len, LANES), lambda p, j: (p, 0, 0))
    blk_pair = pl.BlockSpec((2, t, LANES), lambda p, j: (p, j, 0))
    whole_cols = pl.BlockSpec((s_len, LANES), lambda p, j: (0, p))
    blk_cols = pl.BlockSpec((t, LANES), lambda p, j: (j, p))
    return pl.pallas_call(
        body, name="attn_bwd", grid=(N_PAIRS, nq),
        in_specs=[whole_pair, blk_pair, blk_cols, whole_cols, whole_cols, whole_cols],
        out_specs=[whole_pair, blk_pair, blk_cols],
        out_shape=[jax.ShapeDtypeStruct((N_HEADS, s_len, LANES), F32), jax.ShapeDtypeStruct((N_HEADS, s_len, LANES), F32),
                   jax.ShapeDtypeStruct((s_len, D_MODEL), BF16)],
        compiler_params=_params(("parallel", "arbitrary")),
    )(qa, ka, vb, o, lse, do)


def _attn_post(dqa, dka):
    s_len = dqa.shape[1]
    tr = min(256, s_len)

    def body(dqa_ref, dka_ref, dq_ref, dk_ref, dcum_ref):
        lane = lax.broadcasted_iota(jnp.int32, (tr, LANES), 1)
        lo = lane < HEAD_DIM
        dcum = jnp.zeros((tr, LANES), F32)
        for p in range(N_PAIRS):
            cols = slice(p * LANES, (p + 1) * LANES)
            a0, a1, b0, b1 = dqa_ref[2 * p], dqa_ref[2 * p + 1], dka_ref[2 * p], dka_ref[2 * p + 1]
            dq_ref[:, cols] = (jnp.where(lo, a0, pltpu.roll(a1, HEAD_DIM, 1)) * (HEAD_DIM ** -0.5)).astype(dq_ref.dtype)
            dk_ref[:, cols] = jnp.where(lo, b0, pltpu.roll(b1, HEAD_DIM, 1)).astype(dk_ref.dtype)
            for hh, (a, b) in enumerate(((a0, b0), (a1, b1))):
                dcum = dcum + jnp.where(lane == 2 * p + hh, a[:, AUG_C:AUG_C + 1] - b[:, AUG_ONE:AUG_ONE + 1], 0.0)
        dcum_ref[...] = dcum

    heads = pl.BlockSpec((N_HEADS, tr, LANES), lambda i: (0, i, 0))
    full = pl.BlockSpec((tr, D_MODEL), lambda i: (i, 0))
    return pl.pallas_call(
        body, name="attn_post", grid=(s_len // tr,),
        in_specs=[heads, heads], out_specs=[full, full, pl.BlockSpec((tr, LANES), lambda i: (i, 0))],
        out_shape=[jax.ShapeDtypeStruct((s_len, D_MODEL), BF16), jax.ShapeDtypeStruct((s_len, D_MODEL), BF16),
                   jax.ShapeDtypeStruct((s_len, LANES), F32)],
        compiler_params=_params(("parallel",)),
    )(dqa, dka)


def _ln_stats(r):
    mu = _rowmean(r)
    xc = r - mu
    rstd = lax.rsqrt(_rowmean(xc * xc) + LN_EPS)
    return xc * rstd, rstd


def _ln_bwd(dxh, xh, rstd):
    return rstd * (dxh - _rowmean(dxh) - xh * _rowmean(dxh * xh))


def _rms_bwd(dgn, g, r):
    return r * dgn - (r * r * r) * g * _rowmean(dgn * g)


def _to_aligned(w):
    pad = jnp.zeros((w.shape[0], AL_COLS - IN_COLS), w.dtype)
    return jnp.concatenate([w[:, :2560], w[:, 2576:5648], w[:, 2560:2576], w[:, 5648:5664], pad], axis=1)


def _from_aligned(g):
    return jnp.concatenate([g[:, :2560], g[:, AL_DTF:AL_DTF + 16], g[:, 2560:AL_DTF], g[:, AL_DTF + 16:AL_DTF + 32]], axis=1)


def _lanes(v, at=0):
    return jnp.pad(v, ((0, 0), (at, LANES - at - v.shape[1])))


def _local_step(x, tgt, mod, w_al, w_out, w_fi, w_fo, sp):
    d = D_MODEL
    sh1, sc1, g1, sh2, sc2, g2 = [mod[:, i * d:(i + 1) * d] for i in range(6)]
    dt_bias_l, a_log_l, f_bias_l = _lanes(sp["dt_bias"]), _lanes(sp["a_log"]), _lanes(sp["f_bias"], F_LANE)
    d_exp = jnp.repeat(sp["d_skip"], HEAD_DIM, axis=1)
    z_slab = lambda a: (a, d, AL_Z // d)

    (h1,), _ = _rowwise("mod1", lambda x, sc, sh: ([x * (1.0 + sc) + sh], []), [x], [sc1, sh1], [(d, BF16)], [])
    proj = _matmul("proj", h1, w_al, tn=1152)
    xc_all = _conv_fwd(proj, sp["conv_w"], sp["conv_b"])
    y_ssd, prevs = _ssd_fwd(xc_all, proj, dt_bias_l, a_log_l, d_exp)

    def gated_norm(y, z, w):
        g = y * _silu(z)
        return [g * lax.rsqrt(_rowmean(g * g) + RMS_EPS) * w], []

    (y_ssm,), _ = _rowwise("ssm_norm", gated_norm, [y_ssd, z_slab(proj)], [sp["ssm_norm_w"]], [(d, BF16)], [])
    cum = _fox_cum(proj, f_bias_l)
    qa, ka, vb = _attn_prep(proj, cum)
    o, lse = _attn_fwd(qa, ka, vb)
    (y_att,), _ = _rowwise("attn_norm", lambda o, w: ([o * lax.rsqrt(_rowmean(o * o) + RMS_EPS) * w], []),
                           [o], [sp["attn_norm_w"]], [(d, BF16)], [])
    y_mix = jnp.concatenate([y_ssm, y_att], axis=1)
    y = _matmul("out_proj", y_mix, w_out, tk=2048)

    def ln1_fwd(x, y, g1, sc2, sh2, lg, lb):
        r1 = ALPHA * x + (1.0 + g1) * y
        xh, _ = _ln_stats(r1)
        x1 = xh * lg + lb
        return [r1, x1 * (1.0 + sc2) + sh2], []

    (r1, h2), _ = _rowwise("ln1", ln1_fwd, [x, y], [g1, sc2, sh2, sp["ln1_g"], sp["ln1_b"]], [(d, F32), (d, BF16)], [])
    u = _matmul("ff_in", h2, w_fi)
    (act,), _ = _rowwise("relu2", lambda u: ([jnp.square(jnp.maximum(u, 0.0))], []), [u], [], [(D_FF, BF16)], [], tr=128)
    ff = _matmul("ff_out", act, w_fo, tk=2048)

    def head(r1, ff, tgt, g2, l1g, l1b, l2g, l2b):
        xh1, _ = _ln_stats(r1)
        x1 = xh1 * l1g + l1b
        xh2, rstd2 = _ln_stats(ALPHA * x1 + (1.0 + g2) * ff)
        err = xh2 * l2g + l2b - tgt
        loss = 0.5 * jnp.sum(_rowmean(err * err))
        dx2 = err * (1.0 / d)
        dr2 = _ln_bwd(dx2 * l2g, xh2, rstd2)
        return ([dr2, (1.0 + g2) * dr2],
                [_colsum(dx2 * xh2), _colsum(dx2), _colsum(dr2 * ff), jnp.full((1, LANES), loss, F32)])

    (dr2, dff), (d_ln2_g, d_ln2_b, d_g2, loss) = _rowwise(
        "loss_ln2", head, [r1, ff, tgt], [g2, sp["ln1_g"], sp["ln1_b"], sp["ln2_g"], sp["ln2_b"]],
        [(d, F32), (d, BF16)], [d, d, d, LANES])
    dact = _matmul("d_act", dff, w_fo, dims=NT)
    (du,), _ = _rowwise("relu2_bwd", lambda da, u: ([da * (2.0 * jnp.maximum(u, 0.0))], []), [dact, u], [], [(D_FF, BF16)], [], tr=128)
    dw_fo = _matmul("dw_ff_out", act, dff, dims=TN)
    dw_fi = _matmul("dw_ff_in", h2, du, dims=TN)
    dh2 = _matmul("dh2", du, w_fi, dims=NT, tk=2048)

    def ln1_bwd(r1, dr2, dh2, y, sc2, g1, lg, lb):
        xh, rstd = _ln_stats(r1)
        x1 = xh * lg + lb
        dx1 = ALPHA * dr2 + dh2 * (1.0 + sc2)
        dr1 = _ln_bwd(dx1 * lg, xh, rstd)
        return ([dr1, (1.0 + g1) * dr1],
                [_colsum(dh2 * x1), _colsum(dh2), _colsum(dx1 * xh), _colsum(dx1), _colsum(dr1 * y)])

    (dr1, dy), (d_sc2, d_sh2, d_ln1_g, d_ln1_b, d_g1) = _rowwise(
        "ln1_bwd", ln1_bwd, [r1, dr2, dh2, y], [sc2, g1, sp["ln1_g"], sp["ln1_b"]], [(d, F32), (d, BF16)], [d] * 5)
    dymix = _matmul("dy_mix", dy, w_out, dims=NT)
    dw_out = _matmul("dw_out", y_mix, dy, dims=TN)

    def attn_norm_bwd(o, dyo, w):
        r = lax.rsqrt(_rowmean(o * o) + RMS_EPS)
        return [_rms_bwd(dyo * w, o, r)], [_colsum(dyo * o * r)]

    (do,), (d_attn_w,) = _rowwise("attn_norm_bwd", attn_norm_bwd, [o, (dymix, d, 1)], [sp["attn_norm_w"]], [(d, F32)], [d])

    def gated_norm_bwd(y, z, dyo, w):
        sg = _sigmoid(z)
        sz = z * sg
        g = y * sz
        r = lax.rsqrt(_rowmean(g * g) + RMS_EPS)
        dg = _rms_bwd(dyo * w, g, r)
        return [dg * sz, dg * y * (sg * (1.0 + z * (1.0 - sg)))], [_colsum(dyo * g * r)]

    (dy_ssd, dz), (d_ssm_w,) = _rowwise("ssm_norm_bwd", gated_norm_bwd, [y_ssd, z_slab(proj), (dymix, d, 0)],
                                        [sp["ssm_norm_w"]], [(d, F32), (d, BF16)], [d])
    dqa, dka, dv = _attn_bwd(qa, ka, vb, o, lse, do)
    dq, dk, dcum = _attn_post(dqa, dka)
    dxc, ddt_tile, d_alog_l, d_dexp, d_dtb_l = _ssd_bwd(xc_all, proj, dt_bias_l, a_log_l, d_exp, prevs, dy_ssd)
    dtf, d_fb_l = _fox_cum_bwd(dcum, proj, f_bias_l, ddt_tile)
    dpre, d_conv_w, d_conv_b = _conv_bwd_pre(proj, sp["conv_w"], sp["conv_b"], dxc)
    dxbc = _conv_bwd_in(dpre, sp["conv_w"])
    dproj = jnp.concatenate([dz, dxbc, dq, dk, dv, dtf], axis=1)
    dw_al = _matmul("dw_in", h1, dproj, dims=TN, tn=1152)
    dh1 = _matmul("dh1", dproj, w_al, dims=NT, tk=1152)

    def last(x, dr1, dh1, sc1):
        return [ALPHA * dr1 + dh1 * (1.0 + sc1)], [_colsum(dh1 * x), _colsum(dh1)]

    (dx,), (d_sc1, d_sh1) = _rowwise("grad_x", last, [x, dr1, dh1], [sc1], [(d, F32)], [d, d])

    small = {
        "mod": jnp.concatenate([d_sh1, d_sc1, d_g1, d_sh2, d_sc2, d_g2], axis=1),
        "conv_w": d_conv_w, "conv_b": d_conv_b,
        "dt_bias": d_dtb_l[:, :N_HEADS], "a_log": d_alog_l[:, :N_HEADS],
        "d_skip": jnp.sum(d_dexp.reshape(N_HEADS, HEAD_DIM), axis=1)[None, :],
        "ssm_norm_w": d_ssm_w, "f_bias": d_fb_l[:, F_LANE:F_LANE + N_HEADS], "attn_norm_w": d_attn_w,
        "ln1_g": d_ln1_g, "ln1_b": d_ln1_b, "ln2_g": d_ln2_g, "ln2_b": d_ln2_b, "loss": loss,
    }
    return dx, dw_al, dw_out, dw_fi, dw_fo, small


N_DEV = 8
N_CHIPS = 4
ANY = pl.BlockSpec(memory_space=pl.ANY)
VMEM_SPEC = pl.BlockSpec(memory_space=pltpu.VMEM)


def _place():
    x, y, c = lax.axis_index("x"), lax.axis_index("y"), lax.axis_index("c")
    return x, y, c


def _other_chips(x, y):
    return [(1 - x, y, 2 * (1 - x) + y), (x, 1 - y, 2 * x + 1 - y), (1 - x, 1 - y, 2 * (1 - x) + 1 - y)]


def _allgather_small(name, v):
    r, cdim = v.shape

    def body(v_ref, out_ref, send_sems, recv_sems, local_sem):
        x, y, c = _place()
        me = 4 * x + 2 * y + c
        mine = pltpu.make_async_copy(v_ref, out_ref.at[me], local_sem)
        mine.start()
        peers = []
        for rel in range(1, N_DEV):
            px = 1 - x if rel & 4 else x
            py = 1 - y if rel & 2 else y
            pc = 1 - c if rel & 1 else c
            peers.append((px, py, pc))

        def copy(rel, slot, to):
            return pltpu.make_async_remote_copy(src_ref=v_ref, dst_ref=out_ref.at[slot], send_sem=send_sems.at[rel],
                                                recv_sem=recv_sems.at[rel], device_id=to, device_id_type=MESH)

        sends = [copy(rel, me, peer) for rel, peer in enumerate(peers)]
        for cp in sends:
            cp.start()
        for rel, (px, py, pc) in enumerate(peers):
            copy(rel, 4 * px + 2 * py + pc, (x, y, c)).wait_recv()
        for cp in sends:
            cp.wait_send()
        mine.wait()

    return pl.pallas_call(
        body, name=name, out_shape=jax.ShapeDtypeStruct((N_DEV, r, cdim), v.dtype),
        in_specs=[VMEM_SPEC], out_specs=VMEM_SPEC,
        scratch_shapes=[pltpu.SemaphoreType.DMA((N_DEV - 1,)), pltpu.SemaphoreType.DMA((N_DEV - 1,)), pltpu.SemaphoreType.DMA],
    )(v)


def _gather_weights(halves):
    n = len(halves)

    def body(*refs):
        ins, outs = refs[:n], refs[n:2 * n]
        send_sems, recv_sems, local_sems = refs[2 * n:]
        x, y, c = _place()
        k_me = 2 * x + y
        me, sibling = (x, y, c), (x, y, 1 - c)
        chips = _other_chips(x, y)

        def copy(w, idx, k, half, to, src=None):
            rh = halves[w].shape[0]
            rows = outs[w].at[k, pl.ds(pl.multiple_of(half * rh, rh), rh), :]
            return pltpu.make_async_remote_copy(src_ref=rows if src is None else src, dst_ref=rows, send_sem=send_sems.at[w, idx],
                                                recv_sem=recv_sems.at[w, idx], device_id=to, device_id_type=MESH)

        started = []
        for w in range(n):
            rh = halves[w].shape[0]
            mine = pltpu.make_async_copy(ins[w], outs[w].at[k_me, pl.ds(pl.multiple_of(c * rh, rh), rh), :], local_sems.at[w])
            mine.start()
            started.append(mine)
        sends = []
        for w in range(n):
            sends.append(copy(w, 0, k_me, c, sibling, src=ins[w]))
            for j, (cx, cy, _) in enumerate(chips):
                sends.append(copy(w, 1 + j, k_me, c, (cx, cy, c), src=ins[w]))
        for cp in sends:
            cp.start()
        for w in range(n):
            for j, (_, _, kj) in enumerate(chips):
                copy(w, 1 + j, kj, c, me).wait_recv()
                fwd = copy(w, 4 + j, kj, c, sibling)
                fwd.start()
                sends.append(fwd)
        for w in range(n):
            copy(w, 0, k_me, 1 - c, me).wait_recv()
            for j, (_, _, kj) in enumerate(chips):
                copy(w, 4 + j, kj, 1 - c, me).wait_recv()
        for cp in sends:
            cp.wait_send()
        for mine in started:
            mine.wait()

    return pl.pallas_call(
        body, name="gather_weights",
        out_shape=[jax.ShapeDtypeStruct((N_CHIPS, 2 * h.shape[0], h.shape[1]), h.dtype) for h in halves],
        in_specs=[ANY] * n, out_specs=[ANY] * n,
        scratch_shapes=[pltpu.SemaphoreType.DMA((n, 7)), pltpu.SemaphoreType.DMA((n, 7)), pltpu.SemaphoreType.DMA((n,))],
    )(*halves)


def _pair_exchange(grads):
    n = len(grads)

    def body(*refs):
        ins, outs = refs[:n], refs[n:2 * n]
        send_sems, recv_sems = refs[2 * n:]
        x, y, c = _place()
        cps = []
        for w in range(n):
            rh = grads[w].shape[1] // 2
            src = ins[w].at[:, pl.ds(pl.multiple_of((1 - c) * rh, rh), rh), :]
            cps.append(pltpu.make_async_remote_copy(src_ref=src, dst_ref=outs[w], send_sem=send_sems.at[w], recv_sem=recv_sems.at[w],
                                                    device_id=(x, y, 1 - c), device_id_type=MESH))
        for cp in cps:
            cp.start()
        for cp in cps:
            cp.wait_recv()
        for cp in cps:
            cp.wait_send()

    return pl.pallas_call(
        body, name="pair_exchange",
        out_shape=[jax.ShapeDtypeStruct((N_CHIPS, g.shape[1] // 2, g.shape[2]), g.dtype) for g in grads],
        in_specs=[ANY] * n, out_specs=[ANY] * n,
        scratch_shapes=[pltpu.SemaphoreType.DMA((n,)), pltpu.SemaphoreType.DMA((n,))],
    )(*grads)


def _pair_sum(name, g, got, c):
    _, r, cdim = g.shape
    rh = r // 2
    tr = min(256, rh)
    nb = rh // tr

    def body(c_ref, g_ref, got_ref, o_ref):
        o_ref[...] = (g_ref[...] + got_ref[...]).astype(o_ref.dtype)

    blk = pl.BlockSpec((1, tr, cdim), lambda k, i, c_ref: (k, i, 0))
    return pl.pallas_call(
        body, name=name,
        grid_spec=pltpu.PrefetchScalarGridSpec(
            num_scalar_prefetch=1, grid=(N_CHIPS, nb),
            in_specs=[pl.BlockSpec((1, tr, cdim), lambda k, i, c_ref: (k, c_ref[0] * nb + i, 0)), blk], out_specs=blk),
        out_shape=jax.ShapeDtypeStruct((N_CHIPS, rh, cdim), BF16),
        compiler_params=_params(("parallel", "parallel")),
    )(jnp.reshape(c, (1,)).astype(jnp.int32), g, got)


def _chip_scatter(parts):
    n = len(parts)

    def body(*refs):
        ins, outs = refs[:n], refs[n:2 * n]
        send_sems, recv_sems, local_sems = refs[2 * n:]
        x, y, c = _place()
        k_me = 2 * x + y
        chips = _other_chips(x, y)

        def copy(w, j, src_k, dst_k, to):
            return pltpu.make_async_remote_copy(src_ref=ins[w].at[src_k], dst_ref=outs[w].at[dst_k], send_sem=send_sems.at[w, j],
                                                recv_sem=recv_sems.at[w, j], device_id=to, device_id_type=MESH)

        local = [pltpu.make_async_copy(ins[w].at[k_me], outs[w].at[k_me], local_sems.at[w]) for w in range(n)]
        for cp in local:
            cp.start()
        sends = [copy(w, j, kj, k_me, (cx, cy, c)) for w in range(n) for j, (cx, cy, kj) in enumerate(chips)]
        for cp in sends:
            cp.start()
        for w in range(n):
            for j, (_, _, kj) in enumerate(chips):
                copy(w, j, k_me, kj, (x, y, c)).wait_recv()
        for cp in sends:
            cp.wait_send()
        for cp in local:
            cp.wait()

    return pl.pallas_call(
        body, name="chip_scatter", out_shape=[jax.ShapeDtypeStruct(p.shape, p.dtype) for p in parts],
        in_specs=[ANY] * n, out_specs=[ANY] * n,
        scratch_shapes=[pltpu.SemaphoreType.DMA((n, 3)), pltpu.SemaphoreType.DMA((n, 3)), pltpu.SemaphoreType.DMA((n,))],
    )(*parts)


def _sum_blocks(name, parts):
    k, r, cdim = parts.shape
    tr = min(256, r)

    def body(p_ref, o_ref):
        acc = p_ref[0].astype(F32)
        for i in range(1, k):
            acc = acc + p_ref[i].astype(F32)
        o_ref[...] = acc

    return pl.pallas_call(
        body, name=name, grid=(r // tr,),
        in_specs=[pl.BlockSpec((k, tr, cdim), lambda i: (0, i, 0))], out_specs=pl.BlockSpec((tr, cdim), lambda i: (i, 0)),
        out_shape=jax.ShapeDtypeStruct((r, cdim), F32), compiler_params=_params(("parallel",)),
    )(parts)


def _pair_gather(halves):
    n = len(halves)

    def body(*refs):
        ins, outs = refs[:n], refs[n:2 * n]
        send_sems, recv_sems, local_sems = refs[2 * n:]
        x, y, c = _place()

        def rows(w, half):
            rh = halves[w].shape[0]
            return outs[w].at[pl.ds(pl.multiple_of(half * rh, rh), rh), :]

        def copy(w, half, to, src=None):
            return pltpu.make_async_remote_copy(src_ref=rows(w, half) if src is None else src, dst_ref=rows(w, half),
                                                send_sem=send_sems.at[w], recv_sem=recv_sems.at[w], device_id=to, device_id_type=MESH)

        local = [pltpu.make_async_copy(ins[w], rows(w, c), local_sems.at[w]) for w in range(n)]
        for cp in local:
            cp.start()
        sends = [copy(w, c, (x, y, 1 - c), src=ins[w]) for w in range(n)]
        for cp in sends:
            cp.start()
        for w in range(n):
            copy(w, 1 - c, (x, y, c)).wait_recv()
        for cp in sends:
            cp.wait_send()
        for cp in local:
            cp.wait()

    return pl.pallas_call(
        body, name="pair_gather", out_shape=[jax.ShapeDtypeStruct((2 * h.shape[0], h.shape[1]), h.dtype) for h in halves],
        in_specs=[ANY] * n, out_specs=[ANY] * n,
        scratch_shapes=[pltpu.SemaphoreType.DMA((n,)), pltpu.SemaphoreType.DMA((n,)), pltpu.SemaphoreType.DMA((n,))],
    )(*halves)


ADA_SHARD = 6 * D_MODEL // N_CHIPS


def _mod_part(c_all, w_shard, b_shard):
    tn = 512

    def body(c_ref, w_ref, b_ref, o_ref):
        o_ref[...] = _dot(_silu(c_ref[...]).astype(BF16), w_ref[...].astype(BF16)) + b_ref[...]

    return pl.pallas_call(
        body, name="mod_part", grid=(ADA_SHARD // tn,),
        in_specs=[pl.BlockSpec((N_DEV, D_MODEL), lambda j: (0, 0)), pl.BlockSpec((D_MODEL, tn), lambda j: (0, j)),
                  pl.BlockSpec((1, tn), lambda j: (0, j))],
        out_specs=pl.BlockSpec((N_DEV, tn), lambda j: (0, j)),
        out_shape=jax.ShapeDtypeStruct((N_DEV, ADA_SHARD), F32), compiler_params=_params(("parallel",)),
    )(c_all, w_shard, b_shard)


def _w_ada_grad(c_all_t, dmod_shard):
    tm = 256

    def body(ct_ref, dm_ref, o_ref):
        act = _silu(ct_ref[...])
        acc = act[:, 0:1] * dm_ref[0:1, :]
        for dev in range(1, N_DEV):
            acc = acc + act[:, dev:dev + 1] * dm_ref[dev:dev + 1, :]
        o_ref[...] = acc

    return pl.pallas_call(
        body, name="w_ada_grad", grid=(D_MODEL // tm,),
        in_specs=[pl.BlockSpec((tm, N_DEV), lambda i: (i, 0)), pl.BlockSpec((N_DEV, ADA_SHARD), lambda i: (0, 0))],
        out_specs=pl.BlockSpec((tm, ADA_SHARD), lambda i: (i, 0)),
        out_shape=jax.ShapeDtypeStruct((D_MODEL, ADA_SHARD), F32), compiler_params=_params(("parallel",)),
    )(c_all_t, dmod_shard)


def _adamw(name, w, g, m, v):
    r, cdim = w.shape
    tr = 256 if r % 256 == 0 else r

    def body(w_ref, g_ref, m_ref, v_ref, d_ref, nm_ref, nv_ref):
        g = g_ref[...]
        nm = ADAM_B1 * m_ref[...] + (1.0 - ADAM_B1) * g
        nv = ADAM_B2 * v_ref[...] + (1.0 - ADAM_B2) * jnp.square(g)
        m_hat = nm / (1.0 - ADAM_B1 ** ADAM_STEP)
        v_hat = nv / (1.0 - ADAM_B2 ** ADAM_STEP)
        d_ref[...] = -ADAM_LR * (m_hat / (jnp.sqrt(v_hat) + ADAM_EPS) + ADAM_WD * w_ref[...])
        nm_ref[...] = nm
        nv_ref[...] = nv

    blk = pl.BlockSpec((tr, cdim), lambda i: (i, 0))
    return pl.pallas_call(
        body, name=name, grid=(r // tr,), in_specs=[blk] * 4, out_specs=[blk] * 3,
        out_shape=[jax.ShapeDtypeStruct((r, cdim), F32)] * 3, compiler_params=_params(("parallel",)),
    )(w, g, m, v)


SMALL = ["b_ada", "conv_b", "dt_bias", "a_log", "d_skip", "ssm_norm_w", "f_bias", "attn_norm_w", "ln1_g", "ln1_b", "ln2_g", "ln2_b"]


def _pad128(v):
    n = v.shape[1]
    return jnp.pad(v, ((0, 0), (0, -n % LANES)))


def _pack(vs):
    return jnp.concatenate([_pad128(v) for v in vs], axis=1)


def kernel(x, c, w_ada, b_ada, w_in, conv_w, conv_b, dt_bias, a_log, d_skip, ssm_norm_w, f_bias, attn_norm_w, w_out, ln1_g, ln1_b, w_ff_in, w_ff_out, ln2_g, ln2_b, loss_target, m_w_ada, m_b_ada, m_w_in, m_conv_w, m_conv_b, m_dt_bias, m_a_log, m_d_skip, m_ssm_norm_w, m_f_bias, m_attn_norm_w, m_w_out, m_ln1_g, m_ln1_b, m_w_ff_in, m_w_ff_out, m_ln2_g, m_ln2_b, v_w_ada, v_b_ada, v_w_in, v_conv_w, v_conv_b, v_dt_bias, v_a_log, v_d_skip, v_ssm_norm_w, v_f_bias, v_attn_norm_w, v_w_out, v_ln1_g, v_ln1_b, v_w_ff_in, v_w_ff_out, v_ln2_g, v_ln2_b):
    a = dict(b_ada=b_ada, conv_b=conv_b, dt_bias=dt_bias, a_log=a_log, d_skip=d_skip, ssm_norm_w=ssm_norm_w, f_bias=f_bias,
             attn_norm_w=attn_norm_w, ln1_g=ln1_g, ln1_b=ln1_b, ln2_g=ln2_g, ln2_b=ln2_b)
    ms = dict(b_ada=m_b_ada, conv_b=m_conv_b, dt_bias=m_dt_bias, a_log=m_a_log, d_skip=m_d_skip, ssm_norm_w=m_ssm_norm_w,
              f_bias=m_f_bias, attn_norm_w=m_attn_norm_w, ln1_g=m_ln1_g, ln1_b=m_ln1_b, ln2_g=m_ln2_g, ln2_b=m_ln2_b)
    vs = dict(b_ada=v_b_ada, conv_b=v_conv_b, dt_bias=v_dt_bias, a_log=v_a_log, d_skip=v_d_skip, ssm_norm_w=v_ssm_norm_w,
              f_bias=v_f_bias, attn_norm_w=v_attn_norm_w, ln1_g=v_ln1_g, ln1_b=v_ln1_b, ln2_g=v_ln2_g, ln2_b=v_ln2_b)
    xi, yi, ci = _place()
    chip = 2 * xi + yi
    me = 4 * xi + 2 * yi + ci
    d = D_MODEL
    conv_shard = CONV_DIM // N_CHIPS

    first = _allgather_small("gather_c", jnp.concatenate([c, conv_w[0].reshape(1, CONV_W * conv_shard)], axis=1))[:, 0]
    c_all = first[:, :d]
    conv_w_full = first[::2, d:].reshape(N_CHIPS, CONV_W, conv_shard).transpose(1, 0, 2).reshape(CONV_W, CONV_DIM)
    b_shard = lax.dynamic_slice_in_dim(b_ada, chip * ADA_SHARD, ADA_SHARD, axis=1)
    parts = _allgather_small("gather_mod", _mod_part(c_all, w_ada[0], b_shard))
    mod = lax.dynamic_index_in_dim(parts[::2], me, axis=1, keepdims=False).reshape(1, 6 * d)

    def my_half(w):
        rh = w.shape[0] // 2
        return lax.dynamic_slice_in_dim(w, ci * rh, rh, axis=0).astype(BF16)

    g_in, g_out, g_fi, g_fo = _gather_weights([my_half(w_in[0]), my_half(w_out[0]), my_half(w_ff_in[0]), my_half(w_ff_out[0])])
    w_al = _to_aligned(g_in.transpose(1, 0, 2).reshape(d, IN_COLS))
    w_out_full = g_out.reshape(2 * d, d)
    w_fi_full = g_fi.transpose(1, 0, 2).reshape(d, D_FF)
    w_fo_full = g_fo.reshape(D_FF, d)

    sp = {n: a[n] for n in SMALL[1:]}
    sp["conv_w"] = conv_w_full
    dx, dw_al, dw_out, dw_fi, dw_fo, small = _local_step(x[0], loss_target[0], mod, w_al, w_out_full, w_fi_full, w_fo_full, sp)

    names = ["mod"] + SMALL[1:]
    vec = _pack([small[n] for n in names] + [small["conv_w"].reshape(1, CONV_W * CONV_DIM), small["loss"]])
    every = _allgather_small("gather_small", vec)
    total = _sum_blocks("sum_small", jnp.broadcast_to(every, (N_DEV, SUBLANES, vec.shape[1])))[:1]
    widths = [6 * d] + [a[n].shape[1] for n in SMALL[1:]]
    offs = [0]
    for w in widths:
        offs.append(offs[-1] + w + (-w % LANES))
    g_small = {n: total[:, o:o + w] for n, o, w in zip(SMALL, offs, widths)}
    g_conv_w_full = total[:, offs[-1]:offs[-1] + CONV_W * CONV_DIM].reshape(CONV_W, CONV_DIM)
    loss = total[0, offs[-1] + CONV_W * CONV_DIM]
    dmod_shard = lax.dynamic_slice_in_dim(every[:, 0, :6 * d], chip * ADA_SHARD, ADA_SHARD, axis=1)
    g_w_ada = _w_ada_grad(c_all.T, dmod_shard)
    g_conv_w = lax.dynamic_slice_in_dim(g_conv_w_full, chip * conv_shard, conv_shard, axis=1)

    by_chip = [_from_aligned(dw_al).reshape(d, N_CHIPS, IN_COLS // N_CHIPS).transpose(1, 0, 2), dw_out.reshape(N_CHIPS, 2 * d // N_CHIPS, d),
               dw_fi.reshape(d, N_CHIPS, D_FF // N_CHIPS).transpose(1, 0, 2), dw_fo.reshape(N_CHIPS, D_FF // N_CHIPS, d)]
    got = _pair_exchange(by_chip)
    partial = [_pair_sum("pair_sum_%d" % i, g, r, ci) for i, (g, r) in enumerate(zip(by_chip, got))]
    landed = _chip_scatter(partial)
    reduced = _pair_gather([_sum_blocks("chip_sum_%d" % i, p) for i, p in enumerate(landed)])
    g_big = dict(w_ada=g_w_ada, w_in=reduced[0], conv_w=g_conv_w, w_out=reduced[1], w_ff_in=reduced[2], w_ff_out=reduced[3])

    big = dict(w_ada=(w_ada, m_w_ada, v_w_ada), w_in=(w_in, m_w_in, v_w_in), conv_w=(conv_w, m_conv_w, v_conv_w),
               w_out=(w_out, m_w_out, v_w_out), w_ff_in=(w_ff_in, m_w_ff_in, v_w_ff_in), w_ff_out=(w_ff_out, m_w_ff_out, v_w_ff_out))
    grads, deltas, new_m, new_v = {}, {}, {}, {}
    for n, (w, m, v) in big.items():
        dl, nm, nv = _adamw("adamw_" + n, w[0], g_big[n], m[0], v[0])
        grads[n], deltas[n], new_m[n], new_v[n] = g_big[n][None], dl[None], nm[None], nv[None]
    g_pack = total[:, :offs[-1]]
    dl, nm, nv = _adamw("adamw_small", _pack([a[n] for n in SMALL]), g_pack, _pack([ms[n] for n in SMALL]), _pack([vs[n] for n in SMALL]))
    for n, o, w in zip(SMALL, offs, widths):
        grads[n], deltas[n], new_m[n], new_v[n] = g_small[n], dl[:, o:o + w], nm[:, o:o + w], nv[:, o:o + w]

    order = ["w_ada", "b_ada", "w_in", "conv_w", "conv_b", "dt_bias", "a_log", "d_skip", "ssm_norm_w", "f_bias", "attn_norm_w", "w_out",
             "ln1_g", "ln1_b", "w_ff_in", "w_ff_out", "ln2_g", "ln2_b"]
    return (loss, dx[None], *[grads[n] for n in order], *[deltas[n] for n in order], *[new_m[n] for n in order], *[new_v[n] for n in order])
```

```python
import functools

import jax
import jax.numpy as jnp
from jax import lax
from jax.experimental import pallas as pl
from jax.experimental.pallas import tpu as pltpu

F32, BF16 = jnp.float32, jnp.bfloat16

D_MODEL = 1024
N_HEADS = 16
HEAD_DIM = 64
N_PAIRS = N_HEADS // 2
SSM_GROUPS = 2
SSM_STATE = 128
CHUNK = 128
CONV_W = 4
CONV_DIM = 1536
D_FF = 4096
IN_COLS = 5664
ALPHA = 2.0 ** 0.25
LN_EPS = 1e-5
RMS_EPS = 1e-5
LANES = 128
SUBLANES = 8

AL_Z, AL_XS, AL_B, AL_C, AL_Q, AL_K, AL_V, AL_DTF = 0, 1024, 2048, 2304, 2560, 3584, 4608, 5632
AL_COLS = 5760
F_LANE = 16

ADAM_LR, ADAM_B1, ADAM_B2, ADAM_EPS, ADAM_WD, ADAM_STEP = 0.001, 0.9, 0.999, 1e-08, 0.01, 10

VMEM_LIMIT = 56 * 1024 * 1024
MESH = pl.DeviceIdType.MESH


def _params(sem=None):
    return pltpu.CompilerParams(dimension_semantics=sem, vmem_limit_bytes=VMEM_LIMIT)


def _sigmoid(x):
    return 1.0 / (1.0 + jnp.exp(-x))


def _silu(x):
    return x * _sigmoid(x)


def _softplus(x):
    return jnp.maximum(x, 0.0) + jnp.log(1.0 + jnp.exp(-jnp.abs(x)))


def _split3(a):
    hi = a.astype(BF16)
    r = a - hi.astype(F32)
    mid = r.astype(BF16)
    lo = (r - mid.astype(F32)).astype(BF16)
    return hi, mid, lo


def _dot(a, b, dims=((1,), (0,))):
    return lax.dot_general(a, b, (dims, ((), ())), preferred_element_type=F32)


NN, NT, TN = ((1,), (0,)), ((1,), (1,)), ((0,), (0,))


def _dot3(t, a):
    hi, mid, lo = _split3(a)
    return _dot(t, hi) + _dot(t, mid) + _dot(t, lo)


def _matmul(name, a, b, *, dims=NN, out_dtype=F32, tm=1024, tn=1024, tk=1024, by_chip=None):
    if dims == NN:
        (m, k), n = a.shape, b.shape[1]
    elif dims == NT:
        (m, k), n = a.shape, b.shape[0]
    else:
        (k, m), n = a.shape, b.shape[1]
    if by_chip == "rows":
        tm = min(tm, m // 4)
    if by_chip == "cols":
        tn = min(tn, n // 4)
    tm, tn, tk = min(tm, m), min(tn, n), min(tk, k)
    assert m % tm == 0 and n % tn == 0 and k % tk == 0, (name, m, n, k, tm, tn, tk)
    nk = k // tk
    if by_chip == "rows":
        per = m // 4 // tm
        out_spec = pl.BlockSpec((None, tm, tn), lambda i, j, l: (i // per, i % per, j))
        out_shape = jax.ShapeDtypeStruct((4, m // 4, n), out_dtype)
    elif by_chip == "cols":
        per = n // 4 // tn
        out_spec = pl.BlockSpec((None, tm, tn), lambda i, j, l: (j // per, i, j % per))
        out_shape = jax.ShapeDtypeStruct((4, m, n // 4), out_dtype)
    else:
        out_spec = pl.BlockSpec((tm, tn), lambda i, j, l: (i, j))
        out_shape = jax.ShapeDtypeStruct((m, n), out_dtype)
    a_spec = pl.BlockSpec((tk, tm), lambda i, j, l: (l, i)) if dims == TN else pl.BlockSpec((tm, tk), lambda i, j, l: (i, l))
    b_spec = pl.BlockSpec((tn, tk), lambda i, j, l: (j, l)) if dims == NT else pl.BlockSpec((tk, tn), lambda i, j, l: (l, j))

    def body(a_ref, b_ref, o_ref, acc_ref):
        l = pl.program_id(2)
        part = _dot(a_ref[...].astype(BF16), b_ref[...].astype(BF16), dims)

        @pl.when(l == 0)
        def _():
            acc_ref[...] = part

        @pl.when(l > 0)
        def _():
            acc_ref[...] += part

        @pl.when(l == nk - 1)
        def _():
            o_ref[...] = acc_ref[...].astype(o_ref.dtype)

    return pl.pallas_call(
        body, name=name, grid=(m // tm, n // tn, nk),
        in_specs=[a_spec, b_spec], out_specs=out_spec, out_shape=out_shape,
        scratch_shapes=[pltpu.VMEM((tm, tn), F32)],
        compiler_params=_params(("parallel", "parallel", "arbitrary")),
    )(a, b)


def _rowwise(name, fn, fulls, vecs, out_fulls, out_vecs, tr=256):
    fulls = [f if isinstance(f, tuple) else (f, f.shape[1], 0) for f in fulls]
    s = fulls[0][0].shape[0]
    tr = min(tr, s)
    nf, nv, nof, nov = len(fulls), len(vecs), len(out_fulls), len(out_vecs)
    in_specs = [pl.BlockSpec((tr, w), functools.partial(lambda i, cb: (i, cb), cb=cb)) for (_, w, cb) in fulls]
    in_specs += [pl.BlockSpec(v.shape, lambda i: (0, 0)) for v in vecs]
    out_shape = [jax.ShapeDtypeStruct((s, w), dt) for (w, dt) in out_fulls] + [jax.ShapeDtypeStruct((1, w), F32) for w in out_vecs]
    out_specs = [pl.BlockSpec((tr, w), lambda i: (i, 0)) for (w, _) in out_fulls] + [pl.BlockSpec((1, w), lambda i: (0, 0)) for w in out_vecs]

    def body(*refs):
        outs = refs[nf + nv:]
        of, ov = fn(*[r[...] for r in refs[:nf + nv]])
        for r, val in zip(outs[:nof], of):
            r[...] = val.astype(r.dtype)
        if nov:
            @pl.when(pl.program_id(0) == 0)
            def _():
                for r in outs[nof:]:
                    r[...] = jnp.zeros_like(r)
            for r, val in zip(outs[nof:], ov):
                r[...] += val

    res = pl.pallas_call(
        body, name=name, grid=(s // tr,), in_specs=in_specs, out_specs=out_specs, out_shape=out_shape,
        compiler_params=_params(("arbitrary",)),
    )(*[f[0] for f in fulls], *vecs)
    return res[:nof], res[nof:]


def _colsum(x):
    return jnp.sum(x, axis=0, keepdims=True)


def _rowmean(x):
    return jnp.mean(x, axis=-1, keepdims=True)


CONV_CB = 512
CONV_TR = 512


def _shift_down(u, halo, j):
    if j == 0:
        return u
    ru = pltpu.roll(u, j, 0)
    row8 = lax.broadcasted_iota(jnp.int32, halo.shape, 0)
    top = jnp.where(row8 < j, pltpu.roll(halo, j, 0), ru[:SUBLANES])
    return jnp.concatenate([top, ru[SUBLANES:]], axis=0)


def _shift_up(d, halo, j):
    if j == 0:
        return d
    tr = d.shape[0]
    rd = pltpu.roll(d, tr - j, 0)
    row8 = lax.broadcasted_iota(jnp.int32, halo.shape, 0)
    bot = jnp.where(row8 >= SUBLANES - j, pltpu.roll(halo, SUBLANES - j, 0), rd[tr - SUBLANES:])
    return jnp.concatenate([rd[:tr - SUBLANES], bot], axis=0)


def _conv_specs(s, tr, col0):
    cb0 = col0 // CONV_CB
    per8 = tr // SUBLANES
    blk = pl.BlockSpec((tr, CONV_CB), lambda cb, i: (i, cb0 + cb))
    prev = pl.BlockSpec((SUBLANES, CONV_CB), lambda cb, i: (jnp.maximum(i * per8 - 1, 0), cb0 + cb))
    return blk, prev


def _conv_pre(u, halo, w_ref, b_ref, first):
    halo = jnp.where(first, 0.0, halo)
    acc = b_ref[...] + w_ref[CONV_W - 1:CONV_W, :] * u
    shifted = [u]
    for j in range(1, CONV_W):
        sh = _shift_down(u, halo, j)
        shifted.append(sh)
        acc = acc + w_ref[CONV_W - 1 - j:CONV_W - j, :] * sh
    return acc, shifted


def _conv_fwd(proj, conv_w, conv_b):
    s = proj.shape[0]
    tr = min(CONV_TR, s)
    blk, prev = _conv_specs(s, tr, AL_XS)

    def body(u_ref, h_ref, w_ref, b_ref, o_ref):
        pre, _ = _conv_pre(u_ref[...], h_ref[...], w_ref, b_ref, pl.program_id(1) == 0)
        o_ref[...] = _silu(pre)

    return pl.pallas_call(
        body, name="conv_fwd", grid=(CONV_DIM // CONV_CB, s // tr),
        in_specs=[blk, prev, pl.BlockSpec((CONV_W, CONV_CB), lambda cb, i: (0, cb)), pl.BlockSpec((1, CONV_CB), lambda cb, i: (0, cb))],
        out_specs=pl.BlockSpec((tr, CONV_CB), lambda cb, i: (i, cb)),
        out_shape=jax.ShapeDtypeStruct((s, CONV_DIM), F32),
        compiler_params=_params(("parallel", "parallel")),
    )(proj, proj, conv_w, conv_b)


def _conv_bwd_pre(proj, conv_w, conv_b, dxc):
    s = proj.shape[0]
    tr = min(CONV_TR, s)
    blk, prev = _conv_specs(s, tr, AL_XS)

    def body(u_ref, h_ref, w_ref, b_ref, d_ref, dpre_ref, dw_ref, db_ref):
        i = pl.program_id(1)
        pre, shifted = _conv_pre(u_ref[...], h_ref[...], w_ref, b_ref, i == 0)
        sg = _sigmoid(pre)
        dpre = d_ref[...] * (sg * (1.0 + pre * (1.0 - sg)))
        dpre_ref[...] = dpre

        @pl.when(i == 0)
        def _():
            dw_ref[...] = jnp.zeros_like(dw_ref)
            db_ref[...] = jnp.zeros_like(db_ref)

        db_ref[...] += _colsum(dpre)
        for j in range(CONV_W):
            dw_ref[CONV_W - 1 - j:CONV_W - j, :] += _colsum(dpre * shifted[j])

    own = pl.BlockSpec((tr, CONV_CB), lambda cb, i: (i, cb))
    wspec = pl.BlockSpec((CONV_W, CONV_CB), lambda cb, i: (0, cb))
    bspec = pl.BlockSpec((1, CONV_CB), lambda cb, i: (0, cb))
    return pl.pallas_call(
        body, name="conv_bwd_pre", grid=(CONV_DIM // CONV_CB, s // tr),
        in_specs=[blk, prev, wspec, bspec, own], out_specs=[own, wspec, bspec],
        out_shape=[jax.ShapeDtypeStruct((s, CONV_DIM), F32), jax.ShapeDtypeStruct((CONV_W, CONV_DIM), F32),
                   jax.ShapeDtypeStruct((1, CONV_DIM), F32)],
        compiler_params=_params(("parallel", "arbitrary")),
    )(proj, proj, conv_w, conv_b, dxc)


def _conv_bwd_in(dpre, conv_w):
    s = dpre.shape[0]
    tr = min(CONV_TR, s)
    per8 = tr // SUBLANES
    last8 = s // SUBLANES - 1
    nb = s // tr

    def body(d_ref, n_ref, w_ref, o_ref):
        d = d_ref[...]
        halo = jnp.where(pl.program_id(1) == nb - 1, 0.0, n_ref[...])
        acc = w_ref[CONV_W - 1:CONV_W, :] * d
        for j in range(1, CONV_W):
            acc = acc + w_ref[CONV_W - 1 - j:CONV_W - j, :] * _shift_up(d, halo, j)
        o_ref[...] = acc.astype(o_ref.dtype)

    own = pl.BlockSpec((tr, CONV_CB), lambda cb, i: (i, cb))
    nxt = pl.BlockSpec((SUBLANES, CONV_CB), lambda cb, i: (jnp.minimum((i + 1) * per8, last8), cb))
    return pl.pallas_call(
        body, name="conv_bwd_in", grid=(CONV_DIM // CONV_CB, nb),
        in_specs=[own, nxt, pl.BlockSpec((CONV_W, CONV_CB), lambda cb, i: (0, cb))], out_specs=own,
        out_shape=jax.ShapeDtypeStruct((s, CONV_DIM), BF16),
        compiler_params=_params(("parallel", "parallel")),
    )(dpre, dpre, conv_w)


XC_B, XC_C = 1024, 1280


def _tile_iotas():
    row = lax.broadcasted_iota(jnp.int32, (CHUNK, LANES), 0)
    lane = lax.broadcasted_iota(jnp.int32, (CHUNK, LANES), 1)
    return row, lane


def _ssd_scalars(dtf_ref, bias_ref, alog_ref, row, lane):
    head = lane[:1] < N_HEADS
    raw = dtf_ref[...] + bias_ref[...]
    dt = _softplus(raw)
    a_neg = jnp.where(head, -jnp.exp(alog_ref[...]), 0.0)
    a = dt * a_neg
    tril = (row >= lane).astype(BF16)
    s = _dot3(tril, a)
    return raw, dt, a_neg, s


def _pair(v, j, lo):
    return jnp.where(lo, v[:, 2 * j:2 * j + 1], v[:, 2 * j + 1:2 * j + 2])


def _head_sum(x, lo, hh):
    return jnp.sum(jnp.where(lo == (hh == 0), x, 0.0), axis=1, keepdims=True)


def _decay_masks(s, h, row, lane):
    s_col = jnp.broadcast_to(s[:, h:h + 1], (CHUNK, LANES))
    s_row = s_col.T
    lm = jnp.where(row >= lane, jnp.exp(s_col - s_row), 0.0)
    lmt = jnp.where(row <= lane, jnp.exp(s_row - s_col), 0.0)
    return lm, lmt


def _ssd_fwd(xc_all, proj, dt_bias_l, a_log_l, d_exp):
    s_len = xc_all.shape[0]
    nc = s_len // CHUNK

    def body(x_ref, dtf_ref, bias_ref, alog_ref, dexp_ref, y_ref, prevs_ref, state_ref):
        @pl.when(pl.program_id(0) == 0)
        def _():
            state_ref[...] = jnp.zeros_like(state_ref)

        row, lane = _tile_iotas()
        lo = lane < HEAD_DIM
        _, dt, _, s = _ssd_scalars(dtf_ref, bias_ref, alog_ref, row, lane)
        tot = s[CHUNK - 1:CHUNK, :]
        for g in range(SSM_GROUPS):
            bg = x_ref[:, XC_B + g * SSM_STATE:XC_B + (g + 1) * SSM_STATE].astype(BF16)
            cg = x_ref[:, XC_C + g * SSM_STATE:XC_C + (g + 1) * SSM_STATE].astype(BF16)
            cb = _dot(cg, bg, NT)
            for j in range(g * 4, g * 4 + 4):
                xs_p = x_ref[:, j * LANES:(j + 1) * LANES]
                dt_p, s_p, tot_p = _pair(dt, j, lo), _pair(s, j, lo), _pair(tot, j, lo[:1])
                xc_p = xs_p * dt_p
                xc_b = xc_p.astype(BF16)
                yd = []
                for hh in range(2):
                    lm, _ = _decay_masks(s, 2 * j + hh, row, lane)
                    yd.append(_dot((cb * lm).astype(BF16), xc_b))
                prev = state_ref[j]
                prevs_ref[0, j] = prev
                yo = _dot(cg, prev.astype(BF16)) * jnp.exp(s_p)
                y_ref[:, j * LANES:(j + 1) * LANES] = jnp.where(lo, yd[0], yd[1]) + yo + dexp_ref[:, j * LANES:(j + 1) * LANES] * xs_p
                to_end = jnp.exp(tot_p - s_p)
                state_ref[j] = jnp.exp(tot_p) * prev + _dot(bg, (xc_p * to_end).astype(BF16), TN)

    vec = lambda w: pl.BlockSpec((1, w), lambda c: (0, 0))
    return pl.pallas_call(
        body, name="ssd_fwd", grid=(nc,),
        in_specs=[pl.BlockSpec((CHUNK, CONV_DIM), lambda c: (c, 0)), pl.BlockSpec((CHUNK, LANES), lambda c: (c, AL_DTF // LANES)),
                  vec(LANES), vec(LANES), vec(D_MODEL)],
        out_specs=[pl.BlockSpec((CHUNK, D_MODEL), lambda c: (c, 0)), pl.BlockSpec((1, N_PAIRS, SSM_STATE, LANES), lambda c: (c, 0, 0, 0))],
        out_shape=[jax.ShapeDtypeStruct((s_len, D_MODEL), F32), jax.ShapeDtypeStruct((nc, N_PAIRS, SSM_STATE, LANES), F32)],
        scratch_shapes=[pltpu.VMEM((N_PAIRS, SSM_STATE, LANES), F32)],
        compiler_params=_params(("arbitrary",)),
    )(xc_all, proj, dt_bias_l, a_log_l, d_exp)


def _ssd_bwd(xc_all, proj, dt_bias_l, a_log_l, d_exp, prevs, dy):
    s_len = xc_all.shape[0]
    nc = s_len // CHUNK

    def body(x_ref, dtf_ref, bias_ref, alog_ref, dexp_ref, prevs_ref, dy_ref, dx_ref, ddt_ref, da_ref, dd_ref, dbias_ref, dstate_ref):
        @pl.when(pl.program_id(0) == 0)
        def _():
            dstate_ref[...] = jnp.zeros_like(dstate_ref)
            da_ref[...] = jnp.zeros_like(da_ref)
            dd_ref[...] = jnp.zeros_like(dd_ref)
            dbias_ref[...] = jnp.zeros_like(dbias_ref)

        row, lane = _tile_iotas()
        lo = lane < HEAD_DIM
        last = row == CHUNK - 1
        raw, dt, a_neg, s = _ssd_scalars(dtf_ref, bias_ref, alog_ref, row, lane)
        tot = s[CHUNK - 1:CHUNK, :]
        ds_acc = jnp.zeros((CHUNK, LANES), F32)
        ddt_acc = jnp.zeros((CHUNK, LANES), F32)
        for g in range(SSM_GROUPS):
            bcol = slice(XC_B + g * SSM_STATE, XC_B + (g + 1) * SSM_STATE)
            ccol = slice(XC_C + g * SSM_STATE, XC_C + (g + 1) * SSM_STATE)
            bg = x_ref[:, bcol].astype(BF16)
            cg = x_ref[:, ccol].astype(BF16)
            cb = _dot(cg, bg, NT)
            cbt = _dot(bg, cg, NT)
            dcb = jnp.zeros((CHUNK, LANES), F32)
            dcbt = jnp.zeros((CHUNK, LANES), F32)
            db_acc = jnp.zeros((CHUNK, LANES), F32)
            dc_acc = jnp.zeros((CHUNK, LANES), F32)
            for j in range(g * 4, g * 4 + 4):
                cols = slice(j * LANES, (j + 1) * LANES)
                xs_p, dy_p = x_ref[:, cols], dy_ref[:, cols]
                dt_p, s_p, tot_p = _pair(dt, j, lo), _pair(s, j, lo), _pair(tot, j, lo[:1])
                xc_p = xs_p * dt_p
                xc_b, dy_b = xc_p.astype(BF16), dy_p.astype(BF16)
                e_p, f_p, etot_p = jnp.exp(s_p), jnp.exp(tot_p - s_p), jnp.exp(tot_p)
                prev, dnext = prevs_ref[0, j], dstate_ref[j]
                prev_b, dnext_b = prev.astype(BF16), dnext.astype(BF16)
                dd_ref[:, cols] += _colsum(dy_p * xs_p)
                dxs_p = dexp_ref[:, cols] * dy_p
                cp = _dot(cg, prev_b)
                gy = (dy_p * e_p).astype(BF16)
                dc_acc += _dot(gy, prev_b, NT)
                dstate_ref[j] = etot_p * dnext + _dot(cg, gy, TN)
                de = dy_p * cp * e_p
                bds = _dot(bg, dnext_b)
                db_acc += _dot((xc_p * f_p).astype(BF16), dnext_b, NT)
                dxc_p = bds * f_p
                df = bds * xc_p * f_p
                dtot_p = _colsum(dnext * prev) * etot_p + _colsum(df)
                dsl = de - df + jnp.where(last, dtot_p, 0.0)
                for hh in range(2):
                    h = 2 * j + hh
                    mine = lo == (hh == 0)
                    lm, lmt = _decay_masks(s, h, row, lane)
                    dy_h = jnp.where(mine, dy_p, 0.0).astype(BF16)
                    xc_h = jnp.where(mine, xc_p, 0.0).astype(BF16)
                    dm = _dot(dy_h, xc_b, NT)
                    dmt = _dot(xc_h, dy_b, NT)
                    mt = cbt * lmt
                    dxc_p += _dot(mt.astype(BF16), dy_h)
                    ds_h = (jnp.sum(dm * cb * lm, axis=1, keepdims=True) - jnp.sum(dmt * mt, axis=1, keepdims=True)
                            + _head_sum(dsl, lo, hh))
                    ds_acc += jnp.where(lane == h, ds_h, 0.0)
                    dcb += dm * lm
                    dcbt += dmt * lmt
                    ddt_acc += jnp.where(lane == h, _head_sum(dxc_p * xs_p, lo, hh), 0.0)
                dx_ref[:, cols] = dxs_p + dxc_p * dt_p
            dx_ref[:, ccol] = dc_acc + _dot(dcb.astype(BF16), bg)
            dx_ref[:, bcol] = db_acc + _dot(dcbt.astype(BF16), cg)
        triu = (row <= lane).astype(BF16)
        da = _dot3(triu, ds_acc)
        ddt = ddt_acc + da * a_neg
        da_ref[...] += _colsum(da * dt) * a_neg[:1]
        ddt_raw = jnp.where(lane < N_HEADS, ddt * _sigmoid(raw), 0.0)
        dbias_ref[...] += _colsum(ddt_raw)
        ddt_ref[...] = ddt_raw

    rev = lambda c: nc - 1 - c
    vec = lambda w: pl.BlockSpec((1, w), lambda c: (0, 0))
    return pl.pallas_call(
        body, name="ssd_bwd", grid=(nc,),
        in_specs=[pl.BlockSpec((CHUNK, CONV_DIM), lambda c: (rev(c), 0)), pl.BlockSpec((CHUNK, LANES), lambda c: (rev(c), AL_DTF // LANES)),
                  vec(LANES), vec(LANES), vec(D_MODEL),
                  pl.BlockSpec((1, N_PAIRS, SSM_STATE, LANES), lambda c: (rev(c), 0, 0, 0)),
                  pl.BlockSpec((CHUNK, D_MODEL), lambda c: (rev(c), 0))],
        out_specs=[pl.BlockSpec((CHUNK, CONV_DIM), lambda c: (rev(c), 0)), pl.BlockSpec((CHUNK, LANES), lambda c: (rev(c), 0)),
                   vec(LANES), vec(D_MODEL), vec(LANES)],
        out_shape=[jax.ShapeDtypeStruct((s_len, CONV_DIM), F32), jax.ShapeDtypeStruct((s_len, LANES), F32),
                   jax.ShapeDtypeStruct((1, LANES), F32), jax.ShapeDtypeStruct((1, D_MODEL), F32), jax.ShapeDtypeStruct((1, LANES), F32)],
        scratch_shapes=[pltpu.VMEM((N_PAIRS, SSM_STATE, LANES), F32)],
        compiler_params=_params(("arbitrary",)),
    )(xc_all, proj, dt_bias_l, a_log_l, d_exp, prevs, dy)


AUG_C, AUG_ONE = 64, 67
NEG = -1e30
ATT_T = 512


def _fox_cum(proj, f_bias_l):
    s_len = proj.shape[0]
    nc = s_len // CHUNK

    def body(dtf_ref, fb_ref, cum_ref):
        row, lane = _tile_iotas()
        tril = (row >= lane).astype(BF16)

        def step(c, carry):
            rows = pl.ds(pl.multiple_of(c * CHUNK, CHUNK), CHUNK)
            lf = -_softplus(-(dtf_ref[rows, :] + fb_ref[...]))
            lf = jnp.where(lane < N_HEADS, pltpu.roll(lf, LANES - F_LANE, 1), 0.0)
            cs = _dot3(tril, lf) + carry
            cum_ref[rows, :] = cs
            return cs[CHUNK - 1:CHUNK, :]

        lax.fori_loop(0, nc, step, jnp.zeros((1, LANES), F32))

    return pl.pallas_call(
        body, name="fox_cum", grid=(1,),
        in_specs=[pl.BlockSpec((s_len, LANES), lambda i: (0, AL_DTF // LANES)), pl.BlockSpec((1, LANES), lambda i: (0, 0))],
        out_specs=pl.BlockSpec((s_len, LANES), lambda i: (0, 0)),
        out_shape=jax.ShapeDtypeStruct((s_len, LANES), F32),
        compiler_params=_params(("arbitrary",)),
    )(proj, f_bias_l)


def _fox_cum_bwd(dcum, proj, f_bias_l, ddt_tile):
    s_len = proj.shape[0]
    nc = s_len // CHUNK

    def body(dcum_ref, dtf_ref, fb_ref, ddt_ref, out_ref, dfb_ref):
        row, lane = _tile_iotas()
        triu = (row <= lane).astype(BF16)
        is_f = (lane >= F_LANE) & (lane < F_LANE + N_HEADS)

        def step(t, carry):
            run, dfb = carry
            rows = pl.ds(pl.multiple_of((nc - 1 - t) * CHUNK, CHUNK), CHUNK)
            rc = _dot3(triu, dcum_ref[rows, :]) + run
            sg = _sigmoid(-(dtf_ref[rows, :] + fb_ref[...]))
            df = jnp.where(is_f, pltpu.roll(rc, F_LANE, 1) * sg, 0.0)
            out_ref[rows, :] = (df + ddt_ref[rows, :]).astype(out_ref.dtype)
            return rc[0:1, :], dfb + _colsum(df)

        _, dfb = lax.fori_loop(0, nc, step, (jnp.zeros((1, LANES), F32), jnp.zeros((1, LANES), F32)))
        dfb_ref[...] = dfb

    whole = pl.BlockSpec((s_len, LANES), lambda i: (0, 0))
    vec = pl.BlockSpec((1, LANES), lambda i: (0, 0))
    return pl.pallas_call(
        body, name="fox_cum_bwd", grid=(1,),
        in_specs=[whole, pl.BlockSpec((s_len, LANES), lambda i: (0, AL_DTF // LANES)), vec, whole],
        out_specs=[whole, vec],
        out_shape=[jax.ShapeDtypeStruct((s_len, LANES), BF16), jax.ShapeDtypeStruct((1, LANES), F32)],
        compiler_params=_params(("arbitrary",)),
    )(dcum, proj, f_bias_l, ddt_tile)


def _attn_prep(proj, cum):
    s_len = proj.shape[0]
    tr = min(512, s_len)

    def body(q_ref, k_ref, v_ref, cum_ref, qa_ref, ka_ref, vb_ref):
        p = pl.program_id(0)
        lane = lax.broadcasted_iota(jnp.int32, (tr, LANES), 1)
        lo = lane < HEAD_DIM
        c = cum_ref[...]
        c1 = c.astype(BF16).astype(F32)
        r = c - c1
        c2 = r.astype(BF16).astype(F32)
        c3 = (r - c2).astype(BF16).astype(F32)
        q, k = q_ref[...] * (HEAD_DIM ** -0.5), k_ref[...]
        for hh in range(2):
            col = lambda x: jnp.sum(jnp.where(lane == 2 * p + hh, x, 0.0), axis=1, keepdims=True)
            a1, a2, a3 = col(c1), col(c2), col(c3)
            qh = q if hh == 0 else pltpu.roll(q, HEAD_DIM, 1)
            kh = k if hh == 0 else pltpu.roll(k, HEAD_DIM, 1)
            q_aug = jnp.where(lane == AUG_C, a1, jnp.where(lane == AUG_C + 1, a2, jnp.where(lane == AUG_C + 2, a3,
                              jnp.where(lane < AUG_ONE + 3, 1.0, 0.0))))
            k_aug = jnp.where(lane < AUG_ONE, 1.0, jnp.where(lane == AUG_ONE, -a1, jnp.where(lane == AUG_ONE + 1, -a2,
                              jnp.where(lane == AUG_ONE + 2, -a3, 0.0))))
            qa_ref[hh] = jnp.where(lo, qh, q_aug).astype(BF16)
            ka_ref[hh] = jnp.where(lo, kh, k_aug).astype(BF16)
        vb_ref[...] = v_ref[...].astype(BF16)

    slab = lambda col0: pl.BlockSpec((tr, LANES), lambda p, i: (i, col0 // LANES + p))
    heads = pl.BlockSpec((2, tr, LANES), lambda p, i: (p, i, 0))
    return pl.pallas_call(
        body, name="attn_prep", grid=(N_PAIRS, s_len // tr),
        in_specs=[slab(AL_Q), slab(AL_K), slab(AL_V), pl.BlockSpec((tr, LANES), lambda p, i: (i, 0))],
        out_specs=[heads, heads, pl.BlockSpec((tr, LANES), lambda p, i: (i, p))],
        out_shape=[jax.ShapeDtypeStruct((N_HEADS, s_len, LANES), BF16), jax.ShapeDtypeStruct((N_HEADS, s_len, LANES), BF16),
                   jax.ShapeDtypeStruct((s_len, D_MODEL), BF16)],
        compiler_params=_params(("parallel", "parallel")),
    )(proj, proj, proj, cum)


def _attn_fwd(qa, ka, vb, halves):
    s_len = vb.shape[0]
    t = min(ATT_T, s_len)
    nq = s_len // t
    n = len(halves)

    def body(qa_ref, ka_ref, vb_ref, *rest):
        o_ref, lse_ref = rest[n:n + 2]
        start, finish = _gather_plan(rest[:n], rest[n + 2:2 * n + 2], *rest[2 * n + 2:])
        i = pl.program_id(1)
        pl.when((pl.program_id(0) == 0) & (i == 0))(start)
        row = lax.broadcasted_iota(jnp.int32, (t, t), 0)
        col = lax.broadcasted_iota(jnp.int32, (t, t), 1)
        lo = lax.broadcasted_iota(jnp.int32, (t, LANES), 1) < HEAD_DIM
        outs, lses = [], []
        for hh in range(2):
            q = qa_ref[hh]

            def block(j, carry, masked, q=q, hh=hh):
                m, l, acc = carry
                rows = pl.ds(pl.multiple_of(j * t, t), t)
                s = _dot(q, ka_ref[hh, rows, :], NT)
                if masked:
                    s = jnp.where(row >= col, s, NEG)
                m_new = jnp.maximum(m, jnp.max(s, axis=1, keepdims=True))
                alpha = jnp.exp(m - m_new)
                p = jnp.exp(s - m_new)
                l = alpha * l + jnp.sum(p, axis=1, keepdims=True)
                acc = alpha * acc + _dot(p.astype(BF16), vb_ref[rows, :])
                return m_new, l, acc

            init = (jnp.full((t, 1), NEG, F32), jnp.zeros((t, 1), F32), jnp.zeros((t, LANES), F32))
            carry = lax.fori_loop(0, i, functools.partial(block, masked=False), init)
            m, l, acc = block(i, carry, True)
            outs.append(acc / l)
            lses.append(m + jnp.log(l))
        o_ref[...] = jnp.where(lo, outs[0], outs[1])
        lse_ref[...] = jnp.where(lo, lses[0], lses[1])
        pl.when((pl.program_id(0) == N_PAIRS - 1) & (i == nq - 1))(finish)

    out = pl.BlockSpec((t, LANES), lambda p, i: (i, p))
    res = pl.pallas_call(
        body, name="attn_fwd", grid=(N_PAIRS, nq),
        in_specs=[pl.BlockSpec((2, t, LANES), lambda p, i: (p, i, 0)), pl.BlockSpec((2, s_len, LANES), lambda p, i: (p, 0, 0)),
                  pl.BlockSpec((s_len, LANES), lambda p, i: (0, p))] + [ANY] * n,
        out_specs=[out, out] + [ANY] * n,
        out_shape=[jax.ShapeDtypeStruct((s_len, D_MODEL), F32), jax.ShapeDtypeStruct((s_len, D_MODEL), F32)]
        + [jax.ShapeDtypeStruct((N_CHIPS, 2 * h.shape[0], h.shape[1]), h.dtype) for h in halves],
        scratch_shapes=_exchange_sems(n),
        compiler_params=_params(("arbitrary", "arbitrary")),
    )(qa, ka, vb, *halves)
    return res[0], res[1], res[2:]


def _attn_bwd(qa, ka, vb, o, lse, do, parts):
    s_len = vb.shape[0]
    t = min(ATT_T, s_len)
    nq = s_len // t
    n = len(parts)

    def body(qa_ref, ka_ref, vb_ref, o_ref, lse_ref, do_ref, *rest):
        dqa_ref, dka_ref, dv_ref = rest[n:n + 3]
        start, finish = _reduce_plan(rest[:n], rest[n + 3:2 * n + 3], *rest[2 * n + 3:])
        j = pl.program_id(1)
        pl.when((pl.program_id(0) == 0) & (j == 0))(start)

        @pl.when(j == 0)
        def _():
            dqa_ref[...] = jnp.zeros_like(dqa_ref)

        row = lax.broadcasted_iota(jnp.int32, (t, t), 0)
        col = lax.broadcasted_iota(jnp.int32, (t, t), 1)
        lo = lax.broadcasted_iota(jnp.int32, (t, LANES), 1) < HEAD_DIM
        v = vb_ref[...]
        dv_tot = jnp.zeros((t, LANES), F32)
        for hh in range(2):
            k = ka_ref[hh]
            mine = lo == (hh == 0)

            def block(i, carry, masked, k=k, mine=mine, hh=hh):
                dk, dv = carry
                rows = pl.ds(pl.multiple_of(i * t, t), t)
                q = qa_ref[hh, rows, :]
                do_h = jnp.where(mine, do_ref[rows, :], 0.0)
                delta = jnp.sum(do_h * o_ref[rows, :], axis=1, keepdims=True)
                lse_h = lse_ref[rows, :][:, hh * HEAD_DIM:hh * HEAD_DIM + 1]
                s = _dot(q, k, NT)
                if masked:
                    s = jnp.where(row >= col, s, NEG)
                p = jnp.exp(s - lse_h)
                do_b = do_h.astype(BF16)
                ds = (p * (_dot(do_b, v, NT) - delta)).astype(BF16)
                dv = dv + _dot(p.astype(BF16), do_b, TN)
                dk = dk + _dot(ds, q, TN)
                dqa_ref[hh, rows, :] += _dot(ds, k)
                return dk, dv

            zero = jnp.zeros((t, LANES), F32)
            carry = block(j, (zero, zero), True)
            dk, dv = lax.fori_loop(j + 1, nq, functools.partial(block, masked=False), carry)
            dka_ref[hh] = dk
            dv_tot = dv_tot + dv
        dv_ref[...] = dv_tot.astype(dv_ref.dtype)
        pl.when((pl.program_id(0) == N_PAIRS - 1) & (j == nq - 1))(finish)

    whole_pair = pl.BlockSpec((2, s_len, LANES), lambda p, j: (p, 0, 0))
    blk_pair = pl.BlockSpec((2, t, LANES), lambda p, j: (p, j, 0))
    whole_cols = pl.BlockSpec((s_len, LANES), lambda p, j: (0, p))
    blk_cols = pl.BlockSpec((t, LANES), lambda p, j: (j, p))
    res = pl.pallas_call(
        body, name="attn_bwd", grid=(N_PAIRS, nq),
        in_specs=[whole_pair, blk_pair, blk_cols, whole_cols, whole_cols, whole_cols] + [ANY] * n,
        out_specs=[whole_pair, blk_pair, blk_cols] + [ANY] * n,
        out_shape=[jax.ShapeDtypeStruct((N_HEADS, s_len, LANES), F32), jax.ShapeDtypeStruct((N_HEADS, s_len, LANES), F32),
                   jax.ShapeDtypeStruct((s_len, D_MODEL), BF16)]
        + [jax.ShapeDtypeStruct((N_DEV, g.shape[1] // 2, g.shape[2]), g.dtype) for g in parts],
        scratch_shapes=_exchange_sems(n),
        compiler_params=_params(("arbitrary", "arbitrary")),
    )(qa, ka, vb, o, lse, do, *parts)
    return res[0], res[1], res[2], res[3:]


def _attn_post(dqa, dka):
    s_len = dqa.shape[1]
    tr = min(256, s_len)

    def body(dqa_ref, dka_ref, dq_ref, dk_ref, dcum_ref):
        lane = lax.broadcasted_iota(jnp.int32, (tr, LANES), 1)
        lo = lane < HEAD_DIM
        dcum = jnp.zeros((tr, LANES), F32)
        for p in range(N_PAIRS):
            cols = slice(p * LANES, (p + 1) * LANES)
            a0, a1, b0, b1 = dqa_ref[2 * p], dqa_ref[2 * p + 1], dka_ref[2 * p], dka_ref[2 * p + 1]
            dq_ref[:, cols] = (jnp.where(lo, a0, pltpu.roll(a1, HEAD_DIM, 1)) * (HEAD_DIM ** -0.5)).astype(dq_ref.dtype)
            dk_ref[:, cols] = jnp.where(lo, b0, pltpu.roll(b1, HEAD_DIM, 1)).astype(dk_ref.dtype)
            for hh, (a, b) in enumerate(((a0, b0), (a1, b1))):
                dcum = dcum + jnp.where(lane == 2 * p + hh, a[:, AUG_C:AUG_C + 1] - b[:, AUG_ONE:AUG_ONE + 1], 0.0)
        dcum_ref[...] = dcum

    heads = pl.BlockSpec((N_HEADS, tr, LANES), lambda i: (0, i, 0))
    full = pl.BlockSpec((tr, D_MODEL), lambda i: (i, 0))
    return pl.pallas_call(
        body, name="attn_post", grid=(s_len // tr,),
        in_specs=[heads, heads], out_specs=[full, full, pl.BlockSpec((tr, LANES), lambda i: (i, 0))],
        out_shape=[jax.ShapeDtypeStruct((s_len, D_MODEL), BF16), jax.ShapeDtypeStruct((s_len, D_MODEL), BF16),
                   jax.ShapeDtypeStruct((s_len, LANES), F32)],
        compiler_params=_params(("parallel",)),
    )(dqa, dka)


def _ln_stats(r):
    mu = _rowmean(r)
    xc = r - mu
    rstd = lax.rsqrt(_rowmean(xc * xc) + LN_EPS)
    return xc * rstd, rstd


def _ln_bwd(dxh, xh, rstd):
    return rstd * (dxh - _rowmean(dxh) - xh * _rowmean(dxh * xh))


def _rms_bwd(dgn, g, r):
    return r * dgn - (r * r * r) * g * _rowmean(dgn * g)


def _to_aligned(w):
    pad = jnp.zeros((w.shape[0], AL_COLS - IN_COLS), w.dtype)
    return jnp.concatenate([w[:, :2560], w[:, 2576:5648], w[:, 2560:2576], w[:, 5648:5664], pad], axis=1)


def _from_aligned(g):
    return jnp.concatenate([g[:, :2560], g[:, AL_DTF:AL_DTF + 16], g[:, 2560:AL_DTF], g[:, AL_DTF + 16:AL_DTF + 32]], axis=1)


def _lanes(v, at=0):
    return jnp.pad(v, ((0, 0), (at, LANES - at - v.shape[1])))


def _local_step(x, tgt, mod, w_al, halves, sp):
    d = D_MODEL
    sh1, sc1, g1, sh2, sc2, g2 = [mod[:, i * d:(i + 1) * d] for i in range(6)]
    dt_bias_l, a_log_l, f_bias_l = _lanes(sp["dt_bias"]), _lanes(sp["a_log"]), _lanes(sp["f_bias"], F_LANE)
    d_exp = jnp.repeat(sp["d_skip"], HEAD_DIM, axis=1)
    z_slab = lambda a: (a, d, AL_Z // d)

    (h1,), _ = _rowwise("mod1", lambda x, sc, sh: ([x * (1.0 + sc) + sh], []), [x], [sc1, sh1], [(d, BF16)], [])
    proj = _matmul("proj", h1, w_al, tn=1152)
    xc_all = _conv_fwd(proj, sp["conv_w"], sp["conv_b"])
    y_ssd, prevs = _ssd_fwd(xc_all, proj, dt_bias_l, a_log_l, d_exp)

    def gated_norm(y, z, w):
        g = y * _silu(z)
        return [g * lax.rsqrt(_rowmean(g * g) + RMS_EPS) * w], []

    (y_ssm,), _ = _rowwise("ssm_norm", gated_norm, [y_ssd, z_slab(proj)], [sp["ssm_norm_w"]], [(d, BF16)], [])
    cum = _fox_cum(proj, f_bias_l)
    qa, ka, vb = _attn_prep(proj, cum)
    o, lse, (g_out, g_fi, g_fo) = _attn_fwd(qa, ka, vb, halves)
    w_out = g_out.reshape(2 * d, d)
    w_fi = g_fi.transpose(1, 0, 2).reshape(d, D_FF)
    w_fo = g_fo.reshape(D_FF, d)
    (y_att,), _ = _rowwise("attn_norm", lambda o, w: ([o * lax.rsqrt(_rowmean(o * o) + RMS_EPS) * w], []),
                           [o], [sp["attn_norm_w"]], [(d, BF16)], [])
    y_mix = jnp.concatenate([y_ssm, y_att], axis=1)
    y = _matmul("out_proj", y_mix, w_out, tk=2048)

    def ln1_fwd(x, y, g1, sc2, sh2, lg, lb):
        r1 = ALPHA * x + (1.0 + g1) * y
        xh, _ = _ln_stats(r1)
        x1 = xh * lg + lb
        return [r1, x1 * (1.0 + sc2) + sh2], []

    (r1, h2), _ = _rowwise("ln1", ln1_fwd, [x, y], [g1, sc2, sh2, sp["ln1_g"], sp["ln1_b"]], [(d, F32), (d, BF16)], [])
    u = _matmul("ff_in", h2, w_fi)
    (act,), _ = _rowwise("relu2", lambda u: ([jnp.square(jnp.maximum(u, 0.0))], []), [u], [], [(D_FF, BF16)], [], tr=128)
    ff = _matmul("ff_out", act, w_fo, tk=2048)

    def head(r1, ff, tgt, g2, l1g, l1b, l2g, l2b):
        xh1, _ = _ln_stats(r1)
        x1 = xh1 * l1g + l1b
        xh2, rstd2 = _ln_stats(ALPHA * x1 + (1.0 + g2) * ff)
        err = xh2 * l2g + l2b - tgt
        loss = 0.5 * jnp.sum(_rowmean(err * err))
        dx2 = err * (1.0 / d)
        dr2 = _ln_bwd(dx2 * l2g, xh2, rstd2)
        return ([dr2, (1.0 + g2) * dr2],
                [_colsum(dx2 * xh2), _colsum(dx2), _colsum(dr2 * ff), jnp.full((1, LANES), loss, F32)])

    (dr2, dff), (d_ln2_g, d_ln2_b, d_g2, loss) = _rowwise(
        "loss_ln2", head, [r1, ff, tgt], [g2, sp["ln1_g"], sp["ln1_b"], sp["ln2_g"], sp["ln2_b"]],
        [(d, F32), (d, BF16)], [d, d, d, LANES])
    dact = _matmul("d_act", dff, w_fo, dims=NT)
    (du,), _ = _rowwise("relu2_bwd", lambda da, u: ([da * (2.0 * jnp.maximum(u, 0.0))], []), [dact, u], [], [(D_FF, BF16)], [], tr=128)
    dw_fo = _matmul("dw_ff_out", act, dff, dims=TN, out_dtype=BF16, by_chip="rows")
    dw_fi = _matmul("dw_ff_in", h2, du, dims=TN, out_dtype=BF16, by_chip="cols")
    dh2 = _matmul("dh2", du, w_fi, dims=NT, tk=2048)

    def ln1_bwd(r1, dr2, dh2, y, sc2, g1, lg, lb):
        xh, rstd = _ln_stats(r1)
        x1 = xh * lg + lb
        dx1 = ALPHA * dr2 + dh2 * (1.0 + sc2)
        dr1 = _ln_bwd(dx1 * lg, xh, rstd)
        return ([dr1, (1.0 + g1) * dr1],
                [_colsum(dh2 * x1), _colsum(dh2), _colsum(dx1 * xh), _colsum(dx1), _colsum(dr1 * y)])

    (dr1, dy), (d_sc2, d_sh2, d_ln1_g, d_ln1_b, d_g1) = _rowwise(
        "ln1_bwd", ln1_bwd, [r1, dr2, dh2, y], [sc2, g1, sp["ln1_g"], sp["ln1_b"]], [(d, F32), (d, BF16)], [d] * 5)
    dymix = _matmul("dy_mix", dy, w_out, dims=NT)
    dw_out = _matmul("dw_out", y_mix, dy, dims=TN, out_dtype=BF16, by_chip="rows")

    def attn_norm_bwd(o, dyo, w):
        r = lax.rsqrt(_rowmean(o * o) + RMS_EPS)
        return [_rms_bwd(dyo * w, o, r)], [_colsum(dyo * o * r)]

    (do,), (d_attn_w,) = _rowwise("attn_norm_bwd", attn_norm_bwd, [o, (dymix, d, 1)], [sp["attn_norm_w"]], [(d, F32)], [d])

    def gated_norm_bwd(y, z, dyo, w):
        sg = _sigmoid(z)
        sz = z * sg
        g = y * sz
        r = lax.rsqrt(_rowmean(g * g) + RMS_EPS)
        dg = _rms_bwd(dyo * w, g, r)
        return [dg * sz, dg * y * (sg * (1.0 + z * (1.0 - sg)))], [_colsum(dyo * g * r)]

    (dy_ssd, dz), (d_ssm_w,) = _rowwise("ssm_norm_bwd", gated_norm_bwd, [y_ssd, z_slab(proj), (dymix, d, 0)],
                                        [sp["ssm_norm_w"]], [(d, F32), (d, BF16)], [d])
    dqa, dka, dv, landed = _attn_bwd(qa, ka, vb, o, lse, do, [dw_out, dw_fi, dw_fo])
    dq, dk, dcum = _attn_post(dqa, dka)
    dxc, ddt_tile, d_alog_l, d_dexp, d_dtb_l = _ssd_bwd(xc_all, proj, dt_bias_l, a_log_l, d_exp, prevs, dy_ssd)
    dtf, d_fb_l = _fox_cum_bwd(dcum, proj, f_bias_l, ddt_tile)
    dpre, d_conv_w, d_conv_b = _conv_bwd_pre(proj, sp["conv_w"], sp["conv_b"], dxc)
    dxbc = _conv_bwd_in(dpre, sp["conv_w"])
    dproj = jnp.concatenate([dz, dxbc, dq, dk, dv, dtf], axis=1)
    dw_al = _matmul("dw_in", h1, dproj, dims=TN, tn=1152, out_dtype=BF16)
    dh1 = _matmul("dh1", dproj, w_al, dims=NT, tk=1152)

    def last(x, dr1, dh1, sc1):
        return [ALPHA * dr1 + dh1 * (1.0 + sc1)], [_colsum(dh1 * x), _colsum(dh1)]

    (dx,), (d_sc1, d_sh1) = _rowwise("grad_x", last, [x, dr1, dh1], [sc1], [(d, F32)], [d, d])

    small = {
        "mod": jnp.concatenate([d_sh1, d_sc1, d_g1, d_sh2, d_sc2, d_g2], axis=1),
        "conv_w": d_conv_w, "conv_b": d_conv_b,
        "dt_bias": d_dtb_l[:, :N_HEADS], "a_log": d_alog_l[:, :N_HEADS],
        "d_skip": jnp.sum(d_dexp.reshape(N_HEADS, HEAD_DIM), axis=1)[None, :],
        "ssm_norm_w": d_ssm_w, "f_bias": d_fb_l[:, F_LANE:F_LANE + N_HEADS], "attn_norm_w": d_attn_w,
        "ln1_g": d_ln1_g, "ln1_b": d_ln1_b, "ln2_g": d_ln2_g, "ln2_b": d_ln2_b, "loss": loss,
    }
    return dx, dw_al, landed, small


N_DEV = 8
N_CHIPS = 4
ANY = pl.BlockSpec(memory_space=pl.ANY)
VMEM_SPEC = pl.BlockSpec(memory_space=pltpu.VMEM)


def _place():
    x, y, c = lax.axis_index("x"), lax.axis_index("y"), lax.axis_index("c")
    return x, y, c


def _other_chips(x, y):
    return [(1 - x, y, 2 * (1 - x) + y), (x, 1 - y, 2 * x + 1 - y), (1 - x, 1 - y, 2 * (1 - x) + 1 - y)]


def _allgather_small(name, v):
    r, cdim = v.shape

    def body(v_ref, out_ref, send_sems, recv_sems, local_sem):
        x, y, c = _place()
        me = 4 * x + 2 * y + c
        mine = pltpu.make_async_copy(v_ref, out_ref.at[me], local_sem)
        mine.start()
        peers = []
        for rel in range(1, N_DEV):
            px = 1 - x if rel & 4 else x
            py = 1 - y if rel & 2 else y
            pc = 1 - c if rel & 1 else c
            peers.append((px, py, pc))

        def copy(rel, slot, to):
            return pltpu.make_async_remote_copy(src_ref=v_ref, dst_ref=out_ref.at[slot], send_sem=send_sems.at[rel],
                                                recv_sem=recv_sems.at[rel], device_id=to, device_id_type=MESH)

        sends = [copy(rel, me, peer) for rel, peer in enumerate(peers)]
        for cp in sends:
            cp.start()
        for rel, (px, py, pc) in enumerate(peers):
            copy(rel, 4 * px + 2 * py + pc, (x, y, c)).wait_recv()
        for cp in sends:
            cp.wait_send()
        mine.wait()

    return pl.pallas_call(
        body, name=name, out_shape=jax.ShapeDtypeStruct((N_DEV, r, cdim), v.dtype),
        in_specs=[VMEM_SPEC], out_specs=VMEM_SPEC,
        scratch_shapes=[pltpu.SemaphoreType.DMA((N_DEV - 1,)), pltpu.SemaphoreType.DMA((N_DEV - 1,)), pltpu.SemaphoreType.DMA],
    )(v)


def _gather_weights(halves):
    n = len(halves)

    def body(*refs):
        ins, outs = refs[:n], refs[n:2 * n]
        send_sems, recv_sems, local_sems = refs[2 * n:]
        x, y, c = _place()
        k_me = 2 * x + y
        me, sibling = (x, y, c), (x, y, 1 - c)
        chips = _other_chips(x, y)

        def copy(w, idx, k, half, to, src=None):
            rh = halves[w].shape[0]
            rows = outs[w].at[k, pl.ds(pl.multiple_of(half * rh, rh), rh), :]
            return pltpu.make_async_remote_copy(src_ref=rows if src is None else src, dst_ref=rows, send_sem=send_sems.at[w, idx],
                                                recv_sem=recv_sems.at[w, idx], device_id=to, device_id_type=MESH)

        started = []
        for w in range(n):
            rh = halves[w].shape[0]
            mine = pltpu.make_async_copy(ins[w], outs[w].at[k_me, pl.ds(pl.multiple_of(c * rh, rh), rh), :], local_sems.at[w])
            mine.start()
            started.append(mine)
        sends = []
        for w in range(n):
            sends.append(copy(w, 0, k_me, c, sibling, src=ins[w]))
            for j, (cx, cy, _) in enumerate(chips):
                sends.append(copy(w, 1 + j, k_me, c, (cx, cy, c), src=ins[w]))
        for cp in sends:
            cp.start()
        for w in range(n):
            for j, (_, _, kj) in enumerate(chips):
                copy(w, 1 + j, kj, c, me).wait_recv()
                fwd = copy(w, 4 + j, kj, c, sibling)
                fwd.start()
                sends.append(fwd)
        for w in range(n):
            copy(w, 0, k_me, 1 - c, me).wait_recv()
            for j, (_, _, kj) in enumerate(chips):
                copy(w, 4 + j, kj, 1 - c, me).wait_recv()
        for cp in sends:
            cp.wait_send()
        for mine in started:
            mine.wait()

    return pl.pallas_call(
        body, name="gather_weights",
        out_shape=[jax.ShapeDtypeStruct((N_CHIPS, 2 * h.shape[0], h.shape[1]), h.dtype) for h in halves],
        in_specs=[ANY] * n, out_specs=[ANY] * n,
        scratch_shapes=[pltpu.SemaphoreType.DMA((n, 7)), pltpu.SemaphoreType.DMA((n, 7)), pltpu.SemaphoreType.DMA((n,))],
    )(*halves)


def _peers(x, y, c):
    return [((1 - x) if rel & 4 else x, (1 - y) if rel & 2 else y, (1 - c) if rel & 1 else c) for rel in range(1, N_DEV)]


def _exchange_sems(n):
    return [pltpu.SemaphoreType.DMA((n, N_DEV - 1)), pltpu.SemaphoreType.DMA((n, N_DEV - 1)), pltpu.SemaphoreType.DMA((n,))]


def _gather_plan(ins, outs, send_sems, recv_sems, local_sems):
    x, y, c = _place()
    k_me = 2 * x + y
    peers = _peers(x, y, c)

    def copy(w, rel, k, half, to, src=None):
        rh = ins[w].shape[0]
        rows = outs[w].at[k, pl.ds(pl.multiple_of(half * rh, rh), rh), :]
        return pltpu.make_async_remote_copy(src_ref=rows if src is None else src, dst_ref=rows, send_sem=send_sems.at[w, rel],
                                            recv_sem=recv_sems.at[w, rel], device_id=to, device_id_type=MESH)

    n = len(ins)
    local = [pltpu.make_async_copy(ins[w], outs[w].at[k_me, pl.ds(pl.multiple_of(c * ins[w].shape[0], ins[w].shape[0]), ins[w].shape[0]), :],
                                   local_sems.at[w]) for w in range(n)]
    sends = [copy(w, rel, k_me, c, peer, src=ins[w]) for w in range(n) for rel, peer in enumerate(peers)]

    def start():
        for cp in local + sends:
            cp.start()

    def finish():
        for w in range(n):
            for rel, (px, py, pc) in enumerate(peers):
                copy(w, rel, 2 * px + py, pc, (x, y, c)).wait_recv()
        for cp in sends:
            cp.wait_send()
        for cp in local:
            cp.wait()

    return start, finish


def _reduce_plan(ins, outs, send_sems, recv_sems, local_sems):
    x, y, c = _place()
    me = 4 * x + 2 * y + c
    peers = _peers(x, y, c)

    def block(w, k, half):
        rh = ins[w].shape[1] // 2
        return ins[w].at[k, pl.ds(pl.multiple_of(half * rh, rh), rh), :]

    def copy(w, rel, src, slot, to):
        return pltpu.make_async_remote_copy(src_ref=src, dst_ref=outs[w].at[slot], send_sem=send_sems.at[w, rel],
                                            recv_sem=recv_sems.at[w, rel], device_id=to, device_id_type=MESH)

    n = len(ins)
    local = [pltpu.make_async_copy(block(w, 2 * x + y, c), outs[w].at[me], local_sems.at[w]) for w in range(n)]
    sends = [copy(w, rel, block(w, 2 * px + py, pc), me, (px, py, pc)) for w in range(n) for rel, (px, py, pc) in enumerate(peers)]

    def start():
        for cp in local + sends:
            cp.start()

    def finish():
        for w in range(n):
            for rel, (px, py, pc) in enumerate(peers):
                copy(w, rel, block(w, 2 * x + y, c), 4 * px + 2 * py + pc, (x, y, c)).wait_recv()
        for cp in sends:
            cp.wait_send()
        for cp in local:
            cp.wait()

    return start, finish


def _reduce_direct(parts):
    n = len(parts)

    def body(*refs):
        start, finish = _reduce_plan(refs[:n], refs[n:2 * n], *refs[2 * n:])
        start()
        finish()

    return pl.pallas_call(
        body, name="reduce_direct", out_shape=[jax.ShapeDtypeStruct((N_DEV, g.shape[1] // 2, g.shape[2]), g.dtype) for g in parts],
        in_specs=[ANY] * n, out_specs=[ANY] * n, scratch_shapes=_exchange_sems(n),
    )(*parts)


def _sum_blocks(name, parts):
    k, r, cdim = parts.shape
    tr = min(256, r)

    def body(p_ref, o_ref):
        acc = p_ref[0].astype(F32)
        for i in range(1, k):
            acc = acc + p_ref[i].astype(F32)
        o_ref[...] = acc

    return pl.pallas_call(
        body, name=name, grid=(r // tr,),
        in_specs=[pl.BlockSpec((k, tr, cdim), lambda i: (0, i, 0))], out_specs=pl.BlockSpec((tr, cdim), lambda i: (i, 0)),
        out_shape=jax.ShapeDtypeStruct((r, cdim), F32), compiler_params=_params(("parallel",)),
    )(parts)


def _pair_swap(halves):
    n = len(halves)

    def body(*refs):
        ins, outs = refs[:n], refs[n:2 * n]
        send_sems, recv_sems = refs[2 * n:]
        x, y, c = _place()
        cps = [pltpu.make_async_remote_copy(src_ref=ins[w], dst_ref=outs[w], send_sem=send_sems.at[w], recv_sem=recv_sems.at[w],
                                            device_id=(x, y, 1 - c), device_id_type=MESH) for w in range(n)]
        for cp in cps:
            cp.start()
        for cp in cps:
            cp.wait_recv()
        for cp in cps:
            cp.wait_send()

    return pl.pallas_call(
        body, name="pair_swap", out_shape=[jax.ShapeDtypeStruct(h.shape, h.dtype) for h in halves],
        in_specs=[ANY] * n, out_specs=[ANY] * n,
        scratch_shapes=[pltpu.SemaphoreType.DMA((n,)), pltpu.SemaphoreType.DMA((n,))],
    )(*halves)


ADA_SHARD = 6 * D_MODEL // N_CHIPS


def _mod_part(c_all, w_shard, b_shard):
    tn = 512

    def body(c_ref, w_ref, b_ref, o_ref):
        o_ref[...] = _dot(_silu(c_ref[...]).astype(BF16), w_ref[...].astype(BF16)) + b_ref[...]

    return pl.pallas_call(
        body, name="mod_part", grid=(ADA_SHARD // tn,),
        in_specs=[pl.BlockSpec((N_DEV, D_MODEL), lambda j: (0, 0)), pl.BlockSpec((D_MODEL, tn), lambda j: (0, j)),
                  pl.BlockSpec((1, tn), lambda j: (0, j))],
        out_specs=pl.BlockSpec((N_DEV, tn), lambda j: (0, j)),
        out_shape=jax.ShapeDtypeStruct((N_DEV, ADA_SHARD), F32), compiler_params=_params(("parallel",)),
    )(c_all, w_shard, b_shard)


def _w_ada_grad(c_all_t, dmod_shard):
    tm = 256

    def body(ct_ref, dm_ref, o_ref):
        act = _silu(ct_ref[...])
        acc = act[:, 0:1] * dm_ref[0:1, :]
        for dev in range(1, N_DEV):
            acc = acc + act[:, dev:dev + 1] * dm_ref[dev:dev + 1, :]
        o_ref[...] = acc

    return pl.pallas_call(
        body, name="w_ada_grad", grid=(D_MODEL // tm,),
        in_specs=[pl.BlockSpec((tm, N_DEV), lambda i: (i, 0)), pl.BlockSpec((N_DEV, ADA_SHARD), lambda i: (0, 0))],
        out_specs=pl.BlockSpec((tm, ADA_SHARD), lambda i: (i, 0)),
        out_shape=jax.ShapeDtypeStruct((D_MODEL, ADA_SHARD), F32), compiler_params=_params(("parallel",)),
    )(c_all_t, dmod_shard)


def _adamw_math(w, g, m, v):
    nm = ADAM_B1 * m + (1.0 - ADAM_B1) * g
    nv = ADAM_B2 * v + (1.0 - ADAM_B2) * jnp.square(g)
    m_hat = nm / (1.0 - ADAM_B1 ** ADAM_STEP)
    v_hat = nv / (1.0 - ADAM_B2 ** ADAM_STEP)
    return -ADAM_LR * (m_hat / (jnp.sqrt(v_hat) + ADAM_EPS) + ADAM_WD * w), nm, nv


def _adamw(name, w, g, m, v):
    r, cdim = w.shape
    tr = 256 if r % 256 == 0 else r

    def body(w_ref, g_ref, m_ref, v_ref, d_ref, nm_ref, nv_ref):
        d_ref[...], nm_ref[...], nv_ref[...] = _adamw_math(w_ref[...], g_ref[...], m_ref[...], v_ref[...])

    blk = pl.BlockSpec((tr, cdim), lambda i: (i, 0))
    return pl.pallas_call(
        body, name=name, grid=(r // tr,), in_specs=[blk] * 4, out_specs=[blk] * 3,
        out_shape=[jax.ShapeDtypeStruct((r, cdim), F32)] * 3, compiler_params=_params(("parallel",)),
    )(w, g, m, v)


def _adamw_pair(name, w, mine, other, m, v, c):
    r, cdim = w.shape
    rh = r // 2
    tr = min(256, rh)
    per = rh // tr

    def body(c_ref, w_ref, a_ref, b_ref, m_ref, v_ref, g_ref, d_ref, nm_ref, nv_ref):
        is_mine = (pl.program_id(0) // per) == c_ref[0]
        g = jnp.where(is_mine, a_ref[...], b_ref[...])
        g_ref[...] = g
        d_ref[...], nm_ref[...], nv_ref[...] = _adamw_math(w_ref[...], g, m_ref[...], v_ref[...])

    blk = pl.BlockSpec((tr, cdim), lambda i, c_ref: (i, 0))
    half = pl.BlockSpec((tr, cdim), lambda i, c_ref: (i % per, 0))
    return pl.pallas_call(
        body, name=name,
        grid_spec=pltpu.PrefetchScalarGridSpec(num_scalar_prefetch=1, grid=(r // tr,), in_specs=[blk, half, half, blk, blk], out_specs=[blk] * 4),
        out_shape=[jax.ShapeDtypeStruct((r, cdim), F32)] * 4, compiler_params=_params(("parallel",)),
    )(jnp.reshape(c, (1,)).astype(jnp.int32), w, mine, other, m, v)


SMALL = ["b_ada", "conv_b", "dt_bias", "a_log", "d_skip", "ssm_norm_w", "f_bias", "attn_norm_w", "ln1_g", "ln1_b", "ln2_g", "ln2_b"]


def _pad128(v):
    n = v.shape[1]
    return jnp.pad(v, ((0, 0), (0, -n % LANES)))


def _pack(vs):
    return jnp.concatenate([_pad128(v) for v in vs], axis=1)


def kernel(x, c, w_ada, b_ada, w_in, conv_w, conv_b, dt_bias, a_log, d_skip, ssm_norm_w, f_bias, attn_norm_w, w_out, ln1_g, ln1_b, w_ff_in, w_ff_out, ln2_g, ln2_b, loss_target, m_w_ada, m_b_ada, m_w_in, m_conv_w, m_conv_b, m_dt_bias, m_a_log, m_d_skip, m_ssm_norm_w, m_f_bias, m_attn_norm_w, m_w_out, m_ln1_g, m_ln1_b, m_w_ff_in, m_w_ff_out, m_ln2_g, m_ln2_b, v_w_ada, v_b_ada, v_w_in, v_conv_w, v_conv_b, v_dt_bias, v_a_log, v_d_skip, v_ssm_norm_w, v_f_bias, v_attn_norm_w, v_w_out, v_ln1_g, v_ln1_b, v_w_ff_in, v_w_ff_out, v_ln2_g, v_ln2_b):
    a = dict(b_ada=b_ada, conv_b=conv_b, dt_bias=dt_bias, a_log=a_log, d_skip=d_skip, ssm_norm_w=ssm_norm_w, f_bias=f_bias,
             attn_norm_w=attn_norm_w, ln1_g=ln1_g, ln1_b=ln1_b, ln2_g=ln2_g, ln2_b=ln2_b)
    ms = dict(b_ada=m_b_ada, conv_b=m_conv_b, dt_bias=m_dt_bias, a_log=m_a_log, d_skip=m_d_skip, ssm_norm_w=m_ssm_norm_w,
              f_bias=m_f_bias, attn_norm_w=m_attn_norm_w, ln1_g=m_ln1_g, ln1_b=m_ln1_b, ln2_g=m_ln2_g, ln2_b=m_ln2_b)
    vs = dict(b_ada=v_b_ada, conv_b=v_conv_b, dt_bias=v_dt_bias, a_log=v_a_log, d_skip=v_d_skip, ssm_norm_w=v_ssm_norm_w,
              f_bias=v_f_bias, attn_norm_w=v_attn_norm_w, ln1_g=v_ln1_g, ln1_b=v_ln1_b, ln2_g=v_ln2_g, ln2_b=v_ln2_b)
    xi, yi, ci = _place()
    chip = 2 * xi + yi
    me = 4 * xi + 2 * yi + ci
    d = D_MODEL
    conv_shard = CONV_DIM // N_CHIPS

    first = _allgather_small("gather_c", jnp.concatenate([c, conv_w[0].reshape(1, CONV_W * conv_shard)], axis=1))[:, 0]
    c_all = first[:, :d]
    conv_w_full = first[::2, d:].reshape(N_CHIPS, CONV_W, conv_shard).transpose(1, 0, 2).reshape(CONV_W, CONV_DIM)
    b_shard = lax.dynamic_slice_in_dim(b_ada, chip * ADA_SHARD, ADA_SHARD, axis=1)
    parts = _allgather_small("gather_mod", _mod_part(c_all, w_ada[0], b_shard))
    mod = lax.dynamic_index_in_dim(parts[::2], me, axis=1, keepdims=False).reshape(1, 6 * d)

    def my_half(w):
        rh = w.shape[0] // 2
        return lax.dynamic_slice_in_dim(w, ci * rh, rh, axis=0).astype(BF16)

    (g_in,) = _gather_weights([my_half(w_in[0])])
    w_al = _to_aligned(g_in.transpose(1, 0, 2).reshape(d, IN_COLS))

    sp = {n: a[n] for n in SMALL[1:]}
    sp["conv_w"] = conv_w_full
    halves = [my_half(w_out[0]), my_half(w_ff_in[0]), my_half(w_ff_out[0])]
    dx, dw_al, landed, small = _local_step(x[0], loss_target[0], mod, w_al, halves, sp)

    names = ["mod"] + SMALL[1:]
    vec = _pack([small[n] for n in names] + [small["conv_w"].reshape(1, CONV_W * CONV_DIM), small["loss"]])
    every = _allgather_small("gather_small", vec)
    total = _sum_blocks("sum_small", jnp.broadcast_to(every, (N_DEV, SUBLANES, vec.shape[1])))[:1]
    widths = [6 * d] + [a[n].shape[1] for n in SMALL[1:]]
    offs = [0]
    for w in widths:
        offs.append(offs[-1] + w + (-w % LANES))
    g_small = {n: total[:, o:o + w] for n, o, w in zip(SMALL, offs, widths)}
    g_conv_w_full = total[:, offs[-1]:offs[-1] + CONV_W * CONV_DIM].reshape(CONV_W, CONV_DIM)
    loss = total[0, offs[-1] + CONV_W * CONV_DIM]
    dmod_shard = lax.dynamic_slice_in_dim(every[:, 0, :6 * d], chip * ADA_SHARD, ADA_SHARD, axis=1)
    g_w_ada = _w_ada_grad(c_all.T, dmod_shard)
    g_conv_w = lax.dynamic_slice_in_dim(g_conv_w_full, chip * conv_shard, conv_shard, axis=1)

    (landed_in,) = _reduce_direct([_from_aligned(dw_al).reshape(d, N_CHIPS, IN_COLS // N_CHIPS).transpose(1, 0, 2)])
    mine = [_sum_blocks("dev_sum_%d" % i, p) for i, p in enumerate([landed_in, *landed])]
    other = _pair_swap(mine)

    grads, deltas, new_m, new_v = {}, {}, {}, {}
    paired = dict(w_in=(w_in, m_w_in, v_w_in), w_out=(w_out, m_w_out, v_w_out), w_ff_in=(w_ff_in, m_w_ff_in, v_w_ff_in),
                  w_ff_out=(w_ff_out, m_w_ff_out, v_w_ff_out))
    for i, (n, (w, m, v)) in enumerate(paired.items()):
        g, dl, nm, nv = _adamw_pair("adamw_" + n, w[0], mine[i], other[i], m[0], v[0], ci)
        grads[n], deltas[n], new_m[n], new_v[n] = g[None], dl[None], nm[None], nv[None]
    for n, g, (w, m, v) in (("w_ada", g_w_ada, (w_ada, m_w_ada, v_w_ada)), ("conv_w", g_conv_w, (conv_w, m_conv_w, v_conv_w))):
        dl, nm, nv = _adamw("adamw_" + n, w[0], g, m[0], v[0])
        grads[n], deltas[n], new_m[n], new_v[n] = g[None], dl[None], nm[None], nv[None]
    g_pack = total[:, :offs[-1]]
    dl, nm, nv = _adamw("adamw_small", _pack([a[n] for n in SMALL]), g_pack, _pack([ms[n] for n in SMALL]), _pack([vs[n] for n in SMALL]))
    for n, o, w in zip(SMALL, offs, widths):
        grads[n], deltas[n], new_m[n], new_v[n] = g_small[n], dl[:, o:o + w], nm[:, o:o + w], nv[:, o:o + w]

    order = ["w_ada", "b_ada", "w_in", "conv_w", "conv_b", "dt_bias", "a_log", "d_skip", "ssm_norm_w", "f_bias", "attn_norm_w", "w_out",
             "ln1_g", "ln1_b", "w_ff_in", "w_ff_out", "ln2_g", "ln2_b"]
    return (loss, dx[None], *[grads[n] for n in order], *[deltas[n] for n in order], *[new_m[n] for n in order], *[new_v[n] for n in order])
```

```python
import functools

import jax
import jax.numpy as jnp
from jax import lax
from jax.experimental import pallas as pl
from jax.experimental.pallas import tpu as pltpu

F32, BF16 = jnp.float32, jnp.bfloat16

D_MODEL = 1024
N_HEADS = 16
HEAD_DIM = 64
N_PAIRS = N_HEADS // 2
SSM_GROUPS = 2
SSM_STATE = 128
CHUNK = 128
CONV_W = 4
CONV_DIM = 1536
D_FF = 4096
IN_COLS = 5664
ALPHA = 2.0 ** 0.25
LN_EPS = 1e-5
RMS_EPS = 1e-5
LANES = 128
SUBLANES = 8

AL_Z, AL_XS, AL_B, AL_C, AL_Q, AL_K, AL_V, AL_DTF = 0, 1024, 2048, 2304, 2560, 3584, 4608, 5632
AL_COLS = 5760
F_LANE = 16

ADAM_LR, ADAM_B1, ADAM_B2, ADAM_EPS, ADAM_WD, ADAM_STEP = 0.001, 0.9, 0.999, 1e-08, 0.01, 10

VMEM_LIMIT = 56 * 1024 * 1024
MESH = pl.DeviceIdType.MESH


def _params(sem=None):
    return pltpu.CompilerParams(dimension_semantics=sem, vmem_limit_bytes=VMEM_LIMIT)


def _sigmoid(x):
    return 1.0 / (1.0 + jnp.exp(-x))


def _silu(x):
    return x * _sigmoid(x)


def _softplus(x):
    return jnp.maximum(x, 0.0) + jnp.log(1.0 + jnp.exp(-jnp.abs(x)))


def _split3(a):
    hi = a.astype(BF16)
    r = a - hi.astype(F32)
    mid = r.astype(BF16)
    lo = (r - mid.astype(F32)).astype(BF16)
    return hi, mid, lo


def _dot(a, b, dims=((1,), (0,))):
    return lax.dot_general(a, b, (dims, ((), ())), preferred_element_type=F32)


NN, NT, TN = ((1,), (0,)), ((1,), (1,)), ((0,), (0,))


def _dot3(t, a):
    hi, mid, lo = _split3(a)
    return _dot(t, hi) + _dot(t, mid) + _dot(t, lo)


def _matmul(name, a, b, *, dims=NN, out_dtype=F32, tm=1024, tn=1024, tk=1024, by_chip=None, epilogue=None, extra=None, carry=()):
    if dims == NN:
        (m, k), n = a.shape, b.shape[1]
    elif dims == NT:
        (m, k), n = a.shape, b.shape[0]
    else:
        (k, m), n = a.shape, b.shape[1]
    if by_chip == "rows":
        tm = min(tm, m // 4)
    if by_chip == "cols":
        tn = min(tn, n // 4)
    tm, tn, tk = min(tm, m), min(tn, n), min(tk, k)
    assert m % tm == 0 and n % tn == 0 and k % tk == 0, (name, m, n, k, tm, tn, tk)
    nk = k // tk
    if by_chip == "rows":
        per = m // 4 // tm
        out_spec = pl.BlockSpec((None, tm, tn), lambda i, j, l: (i // per, i % per, j))
        out_shape = jax.ShapeDtypeStruct((4, m // 4, n), out_dtype)
    elif by_chip == "cols":
        per = n // 4 // tn
        out_spec = pl.BlockSpec((None, tm, tn), lambda i, j, l: (j // per, i, j % per))
        out_shape = jax.ShapeDtypeStruct((4, m, n // 4), out_dtype)
    else:
        out_spec = pl.BlockSpec((tm, tn), lambda i, j, l: (i, j))
        out_shape = jax.ShapeDtypeStruct((m, n), out_dtype)
    a_spec = pl.BlockSpec((tk, tm), lambda i, j, l: (l, i)) if dims == TN else pl.BlockSpec((tm, tk), lambda i, j, l: (i, l))
    b_spec = pl.BlockSpec((tn, tk), lambda i, j, l: (j, l)) if dims == NT else pl.BlockSpec((tk, tn), lambda i, j, l: (l, j))

    tile = pl.BlockSpec((tm, tn), lambda i, j, l: (i, j))
    in_specs, args, out_specs, out_shape = [a_spec, b_spec], [a, b], [out_spec], [out_shape]
    if epilogue == "relu2":
        out_specs, out_shape = out_specs + [tile], out_shape + [jax.ShapeDtypeStruct((m, n), BF16)]
    elif epilogue == "relu2_bwd":
        in_specs, args = in_specs + [tile], args + [extra]
    n_in, n_out, n_c = len(args), len(out_specs), len(carry)
    scratch = [pltpu.VMEM((tm, tn) if nk > 1 else (SUBLANES, LANES), F32)]
    if n_c:
        in_specs, args = in_specs + [ANY] * n_c, args + list(carry)
        out_specs = out_specs + [ANY] * n_c
        out_shape = out_shape + [jax.ShapeDtypeStruct((N_DEV, g.shape[1] // 2, g.shape[2]), g.dtype) for g in carry]
        scratch = scratch + _exchange_sems(n_c)
    gm, gn = m // tm, n // tn

    def body(*refs):
        a_ref, b_ref = refs[:2]
        ins, outs = refs[2:n_in], refs[n_in + n_c:n_in + n_c + n_out]
        acc_ref = refs[n_in + 2 * n_c + n_out]
        i, j, l = pl.program_id(0), pl.program_id(1), pl.program_id(2)
        if n_c:
            start, wait = _reduce_plan(refs[n_in:n_in + n_c], refs[n_in + n_c + n_out:n_in + 2 * n_c + n_out], *refs[n_in + 2 * n_c + n_out + 1:])
            pl.when((i == 0) & (j == 0) & (l == 0))(start)
        part = _dot(a_ref[...].astype(BF16), b_ref[...].astype(BF16), dims)

        def finish(res):
            if epilogue == "relu2":
                outs[0][...] = res
                outs[1][...] = jnp.square(jnp.maximum(res, 0.0)).astype(BF16)
            elif epilogue == "relu2_bwd":
                outs[0][...] = (res * (2.0 * jnp.maximum(ins[0][...], 0.0))).astype(outs[0].dtype)
            else:
                outs[0][...] = res.astype(outs[0].dtype)

        if nk == 1:
            finish(part)
        else:
            @pl.when(l == 0)
            def _():
                acc_ref[...] = part

            @pl.when((l > 0) & (l < nk - 1))
            def _():
                acc_ref[...] += part

            @pl.when(l == nk - 1)
            def _():
                finish(acc_ref[...] + part)

        if n_c:
            pl.when((i == gm - 1) & (j == gn - 1) & (l == nk - 1))(wait)

    res = pl.pallas_call(
        body, name=name, grid=(gm, gn, nk),
        in_specs=in_specs, out_specs=out_specs, out_shape=out_shape, scratch_shapes=scratch,
        compiler_params=_params(("arbitrary",) * 3 if n_c else ("parallel", "parallel", "arbitrary")),
    )(*args)
    return res[0] if len(res) == 1 else res


def _rowwise(name, fn, fulls, vecs, out_fulls, out_vecs, tr=256):
    fulls = [f if isinstance(f, tuple) else (f, f.shape[1], 0) for f in fulls]
    s = fulls[0][0].shape[0]
    tr = min(tr, s)
    nf, nv, nof, nov = len(fulls), len(vecs), len(out_fulls), len(out_vecs)
    in_specs = [pl.BlockSpec((tr, w), functools.partial(lambda i, cb: (i, cb), cb=cb)) for (_, w, cb) in fulls]
    in_specs += [pl.BlockSpec(v.shape, lambda i: (0, 0)) for v in vecs]
    out_shape = [jax.ShapeDtypeStruct((s, w), dt) for (w, dt) in out_fulls] + [jax.ShapeDtypeStruct((1, w), F32) for w in out_vecs]
    out_specs = [pl.BlockSpec((tr, w), lambda i: (i, 0)) for (w, _) in out_fulls] + [pl.BlockSpec((1, w), lambda i: (0, 0)) for w in out_vecs]

    def body(*refs):
        outs = refs[nf + nv:]
        of, ov = fn(*[r[...] for r in refs[:nf + nv]])
        for r, val in zip(outs[:nof], of):
            r[...] = val.astype(r.dtype)
        if nov:
            @pl.when(pl.program_id(0) == 0)
            def _():
                for r in outs[nof:]:
                    r[...] = jnp.zeros_like(r)
            for r, val in zip(outs[nof:], ov):
                r[...] += val

    res = pl.pallas_call(
        body, name=name, grid=(s // tr,), in_specs=in_specs, out_specs=out_specs, out_shape=out_shape,
        compiler_params=_params(("arbitrary",)),
    )(*[f[0] for f in fulls], *vecs)
    return res[:nof], res[nof:]


def _colsum(x):
    return jnp.sum(x, axis=0, keepdims=True)


def _rowmean(x):
    return jnp.mean(x, axis=-1, keepdims=True)


CONV_CB = 512
CONV_TR = 512


def _shift_down(u, halo, j):
    if j == 0:
        return u
    ru = pltpu.roll(u, j, 0)
    row8 = lax.broadcasted_iota(jnp.int32, halo.shape, 0)
    top = jnp.where(row8 < j, pltpu.roll(halo, j, 0), ru[:SUBLANES])
    return jnp.concatenate([top, ru[SUBLANES:]], axis=0)


def _shift_up(d, halo, j):
    if j == 0:
        return d
    tr = d.shape[0]
    rd = pltpu.roll(d, tr - j, 0)
    row8 = lax.broadcasted_iota(jnp.int32, halo.shape, 0)
    bot = jnp.where(row8 >= SUBLANES - j, pltpu.roll(halo, SUBLANES - j, 0), rd[tr - SUBLANES:])
    return jnp.concatenate([rd[:tr - SUBLANES], bot], axis=0)


def _conv_specs(s, tr, col0):
    cb0 = col0 // CONV_CB
    per8 = tr // SUBLANES
    blk = pl.BlockSpec((tr, CONV_CB), lambda cb, i: (i, cb0 + cb))
    prev = pl.BlockSpec((SUBLANES, CONV_CB), lambda cb, i: (jnp.maximum(i * per8 - 1, 0), cb0 + cb))
    return blk, prev


def _conv_pre(u, halo, w_ref, b_ref, first):
    halo = jnp.where(first, 0.0, halo)
    acc = b_ref[...] + w_ref[CONV_W - 1:CONV_W, :] * u
    shifted = [u]
    for j in range(1, CONV_W):
        sh = _shift_down(u, halo, j)
        shifted.append(sh)
        acc = acc + w_ref[CONV_W - 1 - j:CONV_W - j, :] * sh
    return acc, shifted


def _conv_fwd(proj, conv_w, conv_b):
    s = proj.shape[0]
    tr = min(CONV_TR, s)
    blk, prev = _conv_specs(s, tr, AL_XS)

    def body(u_ref, h_ref, w_ref, b_ref, o_ref):
        pre, _ = _conv_pre(u_ref[...], h_ref[...], w_ref, b_ref, pl.program_id(1) == 0)
        o_ref[...] = _silu(pre)

    return pl.pallas_call(
        body, name="conv_fwd", grid=(CONV_DIM // CONV_CB, s // tr),
        in_specs=[blk, prev, pl.BlockSpec((CONV_W, CONV_CB), lambda cb, i: (0, cb)), pl.BlockSpec((1, CONV_CB), lambda cb, i: (0, cb))],
        out_specs=pl.BlockSpec((tr, CONV_CB), lambda cb, i: (i, cb)),
        out_shape=jax.ShapeDtypeStruct((s, CONV_DIM), F32),
        compiler_params=_params(("parallel", "parallel")),
    )(proj, proj, conv_w, conv_b)


def _conv_bwd_pre(proj, conv_w, conv_b, dxc):
    s = proj.shape[0]
    tr = min(CONV_TR, s)
    blk, prev = _conv_specs(s, tr, AL_XS)

    def body(u_ref, h_ref, w_ref, b_ref, d_ref, dpre_ref, dw_ref, db_ref):
        i = pl.program_id(1)
        pre, shifted = _conv_pre(u_ref[...], h_ref[...], w_ref, b_ref, i == 0)
        sg = _sigmoid(pre)
        dpre = d_ref[...] * (sg * (1.0 + pre * (1.0 - sg)))
        dpre_ref[...] = dpre

        @pl.when(i == 0)
        def _():
            dw_ref[...] = jnp.zeros_like(dw_ref)
            db_ref[...] = jnp.zeros_like(db_ref)

        db_ref[...] += _colsum(dpre)
        for j in range(CONV_W):
            dw_ref[CONV_W - 1 - j:CONV_W - j, :] += _colsum(dpre * shifted[j])

    own = pl.BlockSpec((tr, CONV_CB), lambda cb, i: (i, cb))
    wspec = pl.BlockSpec((CONV_W, CONV_CB), lambda cb, i: (0, cb))
    bspec = pl.BlockSpec((1, CONV_CB), lambda cb, i: (0, cb))
    return pl.pallas_call(
        body, name="conv_bwd_pre", grid=(CONV_DIM // CONV_CB, s // tr),
        in_specs=[blk, prev, wspec, bspec, own], out_specs=[own, wspec, bspec],
        out_shape=[jax.ShapeDtypeStruct((s, CONV_DIM), F32), jax.ShapeDtypeStruct((CONV_W, CONV_DIM), F32),
                   jax.ShapeDtypeStruct((1, CONV_DIM), F32)],
        compiler_params=_params(("parallel", "arbitrary")),
    )(proj, proj, conv_w, conv_b, dxc)


def _conv_bwd_in(dpre, conv_w):
    s = dpre.shape[0]
    tr = min(CONV_TR, s)
    per8 = tr // SUBLANES
    last8 = s // SUBLANES - 1
    nb = s // tr

    def body(d_ref, n_ref, w_ref, o_ref):
        d = d_ref[...]
        halo = jnp.where(pl.program_id(1) == nb - 1, 0.0, n_ref[...])
        acc = w_ref[CONV_W - 1:CONV_W, :] * d
        for j in range(1, CONV_W):
            acc = acc + w_ref[CONV_W - 1 - j:CONV_W - j, :] * _shift_up(d, halo, j)
        o_ref[...] = acc.astype(o_ref.dtype)

    own = pl.BlockSpec((tr, CONV_CB), lambda cb, i: (i, cb))
    nxt = pl.BlockSpec((SUBLANES, CONV_CB), lambda cb, i: (jnp.minimum((i + 1) * per8, last8), cb))
    return pl.pallas_call(
        body, name="conv_bwd_in", grid=(CONV_DIM // CONV_CB, nb),
        in_specs=[own, nxt, pl.BlockSpec((CONV_W, CONV_CB), lambda cb, i: (0, cb))], out_specs=own,
        out_shape=jax.ShapeDtypeStruct((s, CONV_DIM), BF16),
        compiler_params=_params(("parallel", "parallel")),
    )(dpre, dpre, conv_w)


XC_B, XC_C = 1024, 1280


def _tile_iotas():
    row = lax.broadcasted_iota(jnp.int32, (CHUNK, LANES), 0)
    lane = lax.broadcasted_iota(jnp.int32, (CHUNK, LANES), 1)
    return row, lane


def _ssd_scalars(dtf_ref, bias_ref, alog_ref, row, lane):
    head = lane[:1] < N_HEADS
    raw = dtf_ref[...] + bias_ref[...]
    dt = _softplus(raw)
    a_neg = jnp.where(head, -jnp.exp(alog_ref[...]), 0.0)
    a = dt * a_neg
    tril = (row >= lane).astype(BF16)
    s = _dot3(tril, a)
    return raw, dt, a_neg, s


def _pair(v, j, lo):
    return jnp.where(lo, v[:, 2 * j:2 * j + 1], v[:, 2 * j + 1:2 * j + 2])


def _head_sum(x, lo, hh):
    return jnp.sum(jnp.where(lo == (hh == 0), x, 0.0), axis=1, keepdims=True)


def _decay_masks(s, h, row, lane):
    s_col = jnp.broadcast_to(s[:, h:h + 1], (CHUNK, LANES))
    s_row = s_col.T
    lm = jnp.where(row >= lane, jnp.exp(s_col - s_row), 0.0)
    lmt = jnp.where(row <= lane, jnp.exp(s_row - s_col), 0.0)
    return lm, lmt


def _ssd_fwd(xc_all, proj, dt_bias_l, a_log_l, d_exp):
    s_len = xc_all.shape[0]
    nc = s_len // CHUNK

    def body(x_ref, dtf_ref, bias_ref, alog_ref, dexp_ref, y_ref, prevs_ref, state_ref):
        @pl.when(pl.program_id(0) == 0)
        def _():
            state_ref[...] = jnp.zeros_like(state_ref)

        row, lane = _tile_iotas()
        lo = lane < HEAD_DIM
        _, dt, _, s = _ssd_scalars(dtf_ref, bias_ref, alog_ref, row, lane)
        tot = s[CHUNK - 1:CHUNK, :]
        for g in range(SSM_GROUPS):
            bg = x_ref[:, XC_B + g * SSM_STATE:XC_B + (g + 1) * SSM_STATE].astype(BF16)
            cg = x_ref[:, XC_C + g * SSM_STATE:XC_C + (g + 1) * SSM_STATE].astype(BF16)
            cb = _dot(cg, bg, NT)
            for j in range(g * 4, g * 4 + 4):
                xs_p = x_ref[:, j * LANES:(j + 1) * LANES]
                dt_p, s_p, tot_p = _pair(dt, j, lo), _pair(s, j, lo), _pair(tot, j, lo[:1])
                xc_p = xs_p * dt_p
                xc_b = xc_p.astype(BF16)
                yd = []
                for hh in range(2):
                    lm, _ = _decay_masks(s, 2 * j + hh, row, lane)
                    yd.append(_dot((cb * lm).astype(BF16), xc_b))
                prev = state_ref[j]
                prevs_ref[0, j] = prev
                yo = _dot(cg, prev.astype(BF16)) * jnp.exp(s_p)
                y_ref[:, j * LANES:(j + 1) * LANES] = jnp.where(lo, yd[0], yd[1]) + yo + dexp_ref[:, j * LANES:(j + 1) * LANES] * xs_p
                to_end = jnp.exp(tot_p - s_p)
                state_ref[j] = jnp.exp(tot_p) * prev + _dot(bg, (xc_p * to_end).astype(BF16), TN)

    vec = lambda w: pl.BlockSpec((1, w), lambda c: (0, 0))
    return pl.pallas_call(
        body, name="ssd_fwd", grid=(nc,),
        in_specs=[pl.BlockSpec((CHUNK, CONV_DIM), lambda c: (c, 0)), pl.BlockSpec((CHUNK, LANES), lambda c: (c, AL_DTF // LANES)),
                  vec(LANES), vec(LANES), vec(D_MODEL)],
        out_specs=[pl.BlockSpec((CHUNK, D_MODEL), lambda c: (c, 0)), pl.BlockSpec((1, N_PAIRS, SSM_STATE, LANES), lambda c: (c, 0, 0, 0))],
        out_shape=[jax.ShapeDtypeStruct((s_len, D_MODEL), F32), jax.ShapeDtypeStruct((nc, N_PAIRS, SSM_STATE, LANES), F32)],
        scratch_shapes=[pltpu.VMEM((N_PAIRS, SSM_STATE, LANES), F32)],
        compiler_params=_params(("arbitrary",)),
    )(xc_all, proj, dt_bias_l, a_log_l, d_exp)


def _ssd_bwd(xc_all, proj, dt_bias_l, a_log_l, d_exp, prevs, dy):
    s_len = xc_all.shape[0]
    nc = s_len // CHUNK

    def body(x_ref, dtf_ref, bias_ref, alog_ref, dexp_ref, prevs_ref, dy_ref, dx_ref, ddt_ref, da_ref, dd_ref, dbias_ref, dstate_ref):
        @pl.when(pl.program_id(0) == 0)
        def _():
            dstate_ref[...] = jnp.zeros_like(dstate_ref)
            da_ref[...] = jnp.zeros_like(da_ref)
            dd_ref[...] = jnp.zeros_like(dd_ref)
            dbias_ref[...] = jnp.zeros_like(dbias_ref)

        row, lane = _tile_iotas()
        lo = lane < HEAD_DIM
        last = row == CHUNK - 1
        raw, dt, a_neg, s = _ssd_scalars(dtf_ref, bias_ref, alog_ref, row, lane)
        tot = s[CHUNK - 1:CHUNK, :]
        ds_acc = jnp.zeros((CHUNK, LANES), F32)
        ddt_acc = jnp.zeros((CHUNK, LANES), F32)
        for g in range(SSM_GROUPS):
            bcol = slice(XC_B + g * SSM_STATE, XC_B + (g + 1) * SSM_STATE)
            ccol = slice(XC_C + g * SSM_STATE, XC_C + (g + 1) * SSM_STATE)
            bg = x_ref[:, bcol].astype(BF16)
            cg = x_ref[:, ccol].astype(BF16)
            cb = _dot(cg, bg, NT)
            cbt = _dot(bg, cg, NT)
            dcb = jnp.zeros((CHUNK, LANES), F32)
            dcbt = jnp.zeros((CHUNK, LANES), F32)
            db_acc = jnp.zeros((CHUNK, LANES), F32)
            dc_acc = jnp.zeros((CHUNK, LANES), F32)
            for j in range(g * 4, g * 4 + 4):
                cols = slice(j * LANES, (j + 1) * LANES)
                xs_p, dy_p = x_ref[:, cols], dy_ref[:, cols]
                dt_p, s_p, tot_p = _pair(dt, j, lo), _pair(s, j, lo), _pair(tot, j, lo[:1])
                xc_p = xs_p * dt_p
                xc_b, dy_b = xc_p.astype(BF16), dy_p.astype(BF16)
                e_p, f_p, etot_p = jnp.exp(s_p), jnp.exp(tot_p - s_p), jnp.exp(tot_p)
                prev, dnext = prevs_ref[0, j], dstate_ref[j]
                prev_b, dnext_b = prev.astype(BF16), dnext.astype(BF16)
                dd_ref[:, cols] += _colsum(dy_p * xs_p)
                dxs_p = dexp_ref[:, cols] * dy_p
                cp = _dot(cg, prev_b)
                gy = (dy_p * e_p).astype(BF16)
                dc_acc += _dot(gy, prev_b, NT)
                dstate_ref[j] = etot_p * dnext + _dot(cg, gy, TN)
                de = dy_p * cp * e_p
                bds = _dot(bg, dnext_b)
                db_acc += _dot((xc_p * f_p).astype(BF16), dnext_b, NT)
                dxc_p = bds * f_p
                df = bds * xc_p * f_p
                dtot_p = _colsum(dnext * prev) * etot_p + _colsum(df)
                dsl = de - df + jnp.where(last, dtot_p, 0.0)
                for hh in range(2):
                    h = 2 * j + hh
                    mine = lo == (hh == 0)
                    lm, lmt = _decay_masks(s, h, row, lane)
                    dy_h = jnp.where(mine, dy_p, 0.0).astype(BF16)
                    xc_h = jnp.where(mine, xc_p, 0.0).astype(BF16)
                    dm = _dot(dy_h, xc_b, NT)
                    dmt = _dot(xc_h, dy_b, NT)
                    mt = cbt * lmt
                    dxc_p += _dot(mt.astype(BF16), dy_h)
                    ds_h = (jnp.sum(dm * cb * lm, axis=1, keepdims=True) - jnp.sum(dmt * mt, axis=1, keepdims=True)
                            + _head_sum(dsl, lo, hh))
                    ds_acc += jnp.where(lane == h, ds_h, 0.0)
                    dcb += dm * lm
                    dcbt += dmt * lmt
                    ddt_acc += jnp.where(lane == h, _head_sum(dxc_p * xs_p, lo, hh), 0.0)
                dx_ref[:, cols] = dxs_p + dxc_p * dt_p
            dx_ref[:, ccol] = dc_acc + _dot(dcb.astype(BF16), bg)
            dx_ref[:, bcol] = db_acc + _dot(dcbt.astype(BF16), cg)
        triu = (row <= lane).astype(BF16)
        da = _dot3(triu, ds_acc)
        ddt = ddt_acc + da * a_neg
        da_ref[...] += _colsum(da * dt) * a_neg[:1]
        ddt_raw = jnp.where(lane < N_HEADS, ddt * _sigmoid(raw), 0.0)
        dbias_ref[...] += _colsum(ddt_raw)
        ddt_ref[...] = ddt_raw

    rev = lambda c: nc - 1 - c
    vec = lambda w: pl.BlockSpec((1, w), lambda c: (0, 0))
    return pl.pallas_call(
        body, name="ssd_bwd", grid=(nc,),
        in_specs=[pl.BlockSpec((CHUNK, CONV_DIM), lambda c: (rev(c), 0)), pl.BlockSpec((CHUNK, LANES), lambda c: (rev(c), AL_DTF // LANES)),
                  vec(LANES), vec(LANES), vec(D_MODEL),
                  pl.BlockSpec((1, N_PAIRS, SSM_STATE, LANES), lambda c: (rev(c), 0, 0, 0)),
                  pl.BlockSpec((CHUNK, D_MODEL), lambda c: (rev(c), 0))],
        out_specs=[pl.BlockSpec((CHUNK, CONV_DIM), lambda c: (rev(c), 0)), pl.BlockSpec((CHUNK, LANES), lambda c: (rev(c), 0)),
                   vec(LANES), vec(D_MODEL), vec(LANES)],
        out_shape=[jax.ShapeDtypeStruct((s_len, CONV_DIM), F32), jax.ShapeDtypeStruct((s_len, LANES), F32),
                   jax.ShapeDtypeStruct((1, LANES), F32), jax.ShapeDtypeStruct((1, D_MODEL), F32), jax.ShapeDtypeStruct((1, LANES), F32)],
        scratch_shapes=[pltpu.VMEM((N_PAIRS, SSM_STATE, LANES), F32)],
        compiler_params=_params(("arbitrary",)),
    )(xc_all, proj, dt_bias_l, a_log_l, d_exp, prevs, dy)


AUG_C, AUG_ONE = 64, 67
NEG = -1e30
ATT_T = 512


def _fox_cum(proj, f_bias_l):
    s_len = proj.shape[0]
    nc = s_len // CHUNK

    def body(dtf_ref, fb_ref, cum_ref):
        row, lane = _tile_iotas()
        tril = (row >= lane).astype(BF16)

        def step(c, carry):
            rows = pl.ds(pl.multiple_of(c * CHUNK, CHUNK), CHUNK)
            lf = -_softplus(-(dtf_ref[rows, :] + fb_ref[...]))
            lf = jnp.where(lane < N_HEADS, pltpu.roll(lf, LANES - F_LANE, 1), 0.0)
            cs = _dot3(tril, lf) + carry
            cum_ref[rows, :] = cs
            return cs[CHUNK - 1:CHUNK, :]

        lax.fori_loop(0, nc, step, jnp.zeros((1, LANES), F32))

    return pl.pallas_call(
        body, name="fox_cum", grid=(1,),
        in_specs=[pl.BlockSpec((s_len, LANES), lambda i: (0, AL_DTF // LANES)), pl.BlockSpec((1, LANES), lambda i: (0, 0))],
        out_specs=pl.BlockSpec((s_len, LANES), lambda i: (0, 0)),
        out_shape=jax.ShapeDtypeStruct((s_len, LANES), F32),
        compiler_params=_params(("arbitrary",)),
    )(proj, f_bias_l)


def _fox_cum_bwd(dcum, proj, f_bias_l, ddt_tile):
    s_len = proj.shape[0]
    nc = s_len // CHUNK

    def body(dcum_ref, dtf_ref, fb_ref, ddt_ref, out_ref, dfb_ref):
        row, lane = _tile_iotas()
        triu = (row <= lane).astype(BF16)
        is_f = (lane >= F_LANE) & (lane < F_LANE + N_HEADS)

        def step(t, carry):
            run, dfb = carry
            rows = pl.ds(pl.multiple_of((nc - 1 - t) * CHUNK, CHUNK), CHUNK)
            rc = _dot3(triu, dcum_ref[rows, :]) + run
            sg = _sigmoid(-(dtf_ref[rows, :] + fb_ref[...]))
            df = jnp.where(is_f, pltpu.roll(rc, F_LANE, 1) * sg, 0.0)
            out_ref[rows, :] = (df + ddt_ref[rows, :]).astype(out_ref.dtype)
            return rc[0:1, :], dfb + _colsum(df)

        _, dfb = lax.fori_loop(0, nc, step, (jnp.zeros((1, LANES), F32), jnp.zeros((1, LANES), F32)))
        dfb_ref[...] = dfb

    whole = pl.BlockSpec((s_len, LANES), lambda i: (0, 0))
    vec = pl.BlockSpec((1, LANES), lambda i: (0, 0))
    return pl.pallas_call(
        body, name="fox_cum_bwd", grid=(1,),
        in_specs=[whole, pl.BlockSpec((s_len, LANES), lambda i: (0, AL_DTF // LANES)), vec, whole],
        out_specs=[whole, vec],
        out_shape=[jax.ShapeDtypeStruct((s_len, LANES), BF16), jax.ShapeDtypeStruct((1, LANES), F32)],
        compiler_params=_params(("arbitrary",)),
    )(dcum, proj, f_bias_l, ddt_tile)


def _attn_prep(proj, cum):
    s_len = proj.shape[0]
    tr = min(512, s_len)

    def body(q_ref, k_ref, v_ref, cum_ref, qa_ref, ka_ref, vb_ref):
        p = pl.program_id(0)
        lane = lax.broadcasted_iota(jnp.int32, (tr, LANES), 1)
        lo = lane < HEAD_DIM
        c = cum_ref[...]
        c1 = c.astype(BF16).astype(F32)
        r = c - c1
        c2 = r.astype(BF16).astype(F32)
        c3 = (r - c2).astype(BF16).astype(F32)
        q, k = q_ref[...] * (HEAD_DIM ** -0.5), k_ref[...]
        for hh in range(2):
            col = lambda x: jnp.sum(jnp.where(lane == 2 * p + hh, x, 0.0), axis=1, keepdims=True)
            a1, a2, a3 = col(c1), col(c2), col(c3)
            qh = q if hh == 0 else pltpu.roll(q, HEAD_DIM, 1)
            kh = k if hh == 0 else pltpu.roll(k, HEAD_DIM, 1)
            q_aug = jnp.where(lane == AUG_C, a1, jnp.where(lane == AUG_C + 1, a2, jnp.where(lane == AUG_C + 2, a3,
                              jnp.where(lane < AUG_ONE + 3, 1.0, 0.0))))
            k_aug = jnp.where(lane < AUG_ONE, 1.0, jnp.where(lane == AUG_ONE, -a1, jnp.where(lane == AUG_ONE + 1, -a2,
                              jnp.where(lane == AUG_ONE + 2, -a3, 0.0))))
            qa_ref[hh] = jnp.where(lo, qh, q_aug).astype(BF16)
            ka_ref[hh] = jnp.where(lo, kh, k_aug).astype(BF16)
        vb_ref[...] = v_ref[...].astype(BF16)

    slab = lambda col0: pl.BlockSpec((tr, LANES), lambda p, i: (i, col0 // LANES + p))
    heads = pl.BlockSpec((2, tr, LANES), lambda p, i: (p, i, 0))
    return pl.pallas_call(
        body, name="attn_prep", grid=(N_PAIRS, s_len // tr),
        in_specs=[slab(AL_Q), slab(AL_K), slab(AL_V), pl.BlockSpec((tr, LANES), lambda p, i: (i, 0))],
        out_specs=[heads, heads, pl.BlockSpec((tr, LANES), lambda p, i: (i, p))],
        out_shape=[jax.ShapeDtypeStruct((N_HEADS, s_len, LANES), BF16), jax.ShapeDtypeStruct((N_HEADS, s_len, LANES), BF16),
                   jax.ShapeDtypeStruct((s_len, D_MODEL), BF16)],
        compiler_params=_params(("parallel", "parallel")),
    )(proj, proj, proj, cum)


def _attn_fwd(qa, ka, vb, halves):
    s_len = vb.shape[0]
    t = min(ATT_T, s_len)
    nq = s_len // t
    n = len(halves)

    def body(qa_ref, ka_ref, vb_ref, *rest):
        o_ref, lse_ref = rest[n:n + 2]
        start, finish = _gather_plan(rest[:n], rest[n + 2:2 * n + 2], *rest[2 * n + 2:])
        i = pl.program_id(1)
        pl.when((pl.program_id(0) == 0) & (i == 0))(start)
        row = lax.broadcasted_iota(jnp.int32, (t, t), 0)
        col = lax.broadcasted_iota(jnp.int32, (t, t), 1)
        lo = lax.broadcasted_iota(jnp.int32, (t, LANES), 1) < HEAD_DIM
        qs = (qa_ref[0], qa_ref[1])

        def block(j, carry, masked):
            rows = pl.ds(pl.multiple_of(j * t, t), t)
            v = vb_ref[rows, :]
            new = []
            for hh in range(2):
                m, l, acc = carry[hh]
                s = _dot(qs[hh], ka_ref[hh, rows, :], NT)
                if masked:
                    s = jnp.where(row >= col, s, NEG)
                m_new = jnp.maximum(m, jnp.max(s, axis=1, keepdims=True))
                alpha = jnp.exp(m - m_new)
                p = jnp.exp(s - m_new)
                new.append((m_new, alpha * l + jnp.sum(p, axis=1, keepdims=True), alpha * acc + _dot(p.astype(BF16), v)))
            return tuple(new)

        init = (jnp.full((t, 1), NEG, F32), jnp.zeros((t, 1), F32), jnp.zeros((t, LANES), F32))
        carry = lax.fori_loop(0, i, functools.partial(block, masked=False), (init, init))
        (m0, l0, acc0), (m1, l1, acc1) = block(i, carry, True)
        o_ref[...] = jnp.where(lo, acc0 / l0, acc1 / l1)
        lse_ref[...] = jnp.where(lo, m0 + jnp.log(l0), m1 + jnp.log(l1))
        pl.when((pl.program_id(0) == N_PAIRS - 1) & (i == nq - 1))(finish)

    out = pl.BlockSpec((t, LANES), lambda p, i: (i, p))
    res = pl.pallas_call(
        body, name="attn_fwd", grid=(N_PAIRS, nq),
        in_specs=[pl.BlockSpec((2, t, LANES), lambda p, i: (p, i, 0)), pl.BlockSpec((2, s_len, LANES), lambda p, i: (p, 0, 0)),
                  pl.BlockSpec((s_len, LANES), lambda p, i: (0, p))] + [ANY] * n,
        out_specs=[out, out] + [ANY] * n,
        out_shape=[jax.ShapeDtypeStruct((s_len, D_MODEL), F32), jax.ShapeDtypeStruct((s_len, D_MODEL), F32)]
        + [jax.ShapeDtypeStruct((N_CHIPS, 2 * h.shape[0], h.shape[1]), h.dtype) for h in halves],
        scratch_shapes=_exchange_sems(n),
        compiler_params=_params(("arbitrary", "arbitrary")),
    )(qa, ka, vb, *halves)
    return res[0], res[1], res[2:]


def _attn_bwd(qa, ka, vb, o, lse, do, parts):
    s_len = vb.shape[0]
    t = min(ATT_T, s_len)
    nq = s_len // t
    n = len(parts)

    def body(qa_ref, ka_ref, vb_ref, o_ref, lse_ref, do_ref, *rest):
        dqa_ref, dka_ref, dv_ref = rest[n:n + 3]
        start, finish = _reduce_plan(rest[:n], rest[n + 3:2 * n + 3], *rest[2 * n + 3:])
        j = pl.program_id(1)
        pl.when((pl.program_id(0) == 0) & (j == 0))(start)

        @pl.when(j == 0)
        def _():
            dqa_ref[...] = jnp.zeros_like(dqa_ref)

        row = lax.broadcasted_iota(jnp.int32, (t, t), 0)
        col = lax.broadcasted_iota(jnp.int32, (t, t), 1)
        lo = lax.broadcasted_iota(jnp.int32, (t, LANES), 1) < HEAD_DIM
        v = vb_ref[...]
        ks = (ka_ref[0], ka_ref[1])

        def block(i, carry, masked):
            dk, dv = list(carry[:2]), carry[2]
            rows = pl.ds(pl.multiple_of(i * t, t), t)
            do_p, o_p, lse_p = do_ref[rows, :], o_ref[rows, :], lse_ref[rows, :]
            for hh in range(2):
                q = qa_ref[hh, rows, :]
                do_h = jnp.where(lo == (hh == 0), do_p, 0.0)
                delta = jnp.sum(do_h * o_p, axis=1, keepdims=True)
                s = _dot(q, ks[hh], NT)
                if masked:
                    s = jnp.where(row >= col, s, NEG)
                p = jnp.exp(s - lse_p[:, hh * HEAD_DIM:hh * HEAD_DIM + 1])
                do_b = do_h.astype(BF16)
                ds = (p * (_dot(do_b, v, NT) - delta)).astype(BF16)
                dv = dv + _dot(p.astype(BF16), do_b, TN)
                dk[hh] = dk[hh] + _dot(ds, q, TN)
                dqa_ref[hh, rows, :] += _dot(ds, ks[hh])
            return dk[0], dk[1], dv

        zero = jnp.zeros((t, LANES), F32)
        carry = block(j, (zero, zero, zero), True)
        dk0, dk1, dv = lax.fori_loop(j + 1, nq, functools.partial(block, masked=False), carry)
        dka_ref[0] = dk0
        dka_ref[1] = dk1
        dv_ref[...] = dv.astype(dv_ref.dtype)
        pl.when((pl.program_id(0) == N_PAIRS - 1) & (j == nq - 1))(finish)

    whole_pair = pl.BlockSpec((2, s_len, LANES), lambda p, j: (p, 0, 0))
    blk_pair = pl.BlockSpec((2, t, LANES), lambda p, j: (p, j, 0))
    whole_cols = pl.BlockSpec((s_len, LANES), lambda p, j: (0, p))
    blk_cols = pl.BlockSpec((t, LANES), lambda p, j: (j, p))
    res = pl.pallas_call(
        body, name="attn_bwd", grid=(N_PAIRS, nq),
        in_specs=[whole_pair, blk_pair, blk_cols, whole_cols, whole_cols, whole_cols] + [ANY] * n,
        out_specs=[whole_pair, blk_pair, blk_cols] + [ANY] * n,
        out_shape=[jax.ShapeDtypeStruct((N_HEADS, s_len, LANES), F32), jax.ShapeDtypeStruct((N_HEADS, s_len, LANES), F32),
                   jax.ShapeDtypeStruct((s_len, D_MODEL), BF16)]
        + [jax.ShapeDtypeStruct((N_DEV, g.shape[1] // 2, g.shape[2]), g.dtype) for g in parts],
        scratch_shapes=_exchange_sems(n),
        compiler_params=_params(("arbitrary", "arbitrary")),
    )(qa, ka, vb, o, lse, do, *parts)
    return res[0], res[1], res[2], res[3:]


def _attn_post(dqa, dka):
    s_len = dqa.shape[1]
    tr = min(256, s_len)

    def body(dqa_ref, dka_ref, dq_ref, dk_ref, dcum_ref):
        lane = lax.broadcasted_iota(jnp.int32, (tr, LANES), 1)
        lo = lane < HEAD_DIM
        dcum = jnp.zeros((tr, LANES), F32)
        for p in range(N_PAIRS):
            cols = slice(p * LANES, (p + 1) * LANES)
            a0, a1, b0, b1 = dqa_ref[2 * p], dqa_ref[2 * p + 1], dka_ref[2 * p], dka_ref[2 * p + 1]
            dq_ref[:, cols] = (jnp.where(lo, a0, pltpu.roll(a1, HEAD_DIM, 1)) * (HEAD_DIM ** -0.5)).astype(dq_ref.dtype)
            dk_ref[:, cols] = jnp.where(lo, b0, pltpu.roll(b1, HEAD_DIM, 1)).astype(dk_ref.dtype)
            for hh, (a, b) in enumerate(((a0, b0), (a1, b1))):
                dcum = dcum + jnp.where(lane == 2 * p + hh, a[:, AUG_C:AUG_C + 1] - b[:, AUG_ONE:AUG_ONE + 1], 0.0)
        dcum_ref[...] = dcum

    heads = pl.BlockSpec((N_HEADS, tr, LANES), lambda i: (0, i, 0))
    full = pl.BlockSpec((tr, D_MODEL), lambda i: (i, 0))
    return pl.pallas_call(
        body, name="attn_post", grid=(s_len // tr,),
        in_specs=[heads, heads], out_specs=[full, full, pl.BlockSpec((tr, LANES), lambda i: (i, 0))],
        out_shape=[jax.ShapeDtypeStruct((s_len, D_MODEL), BF16), jax.ShapeDtypeStruct((s_len, D_MODEL), BF16),
                   jax.ShapeDtypeStruct((s_len, LANES), F32)],
        compiler_params=_params(("parallel",)),
    )(dqa, dka)


def _ln_stats(r):
    mu = _rowmean(r)
    xc = r - mu
    rstd = lax.rsqrt(_rowmean(xc * xc) + LN_EPS)
    return xc * rstd, rstd


def _ln_bwd(dxh, xh, rstd):
    return rstd * (dxh - _rowmean(dxh) - xh * _rowmean(dxh * xh))


def _rms_bwd(dgn, g, r):
    return r * dgn - (r * r * r) * g * _rowmean(dgn * g)


def _to_aligned(w):
    pad = jnp.zeros((w.shape[0], AL_COLS - IN_COLS), w.dtype)
    return jnp.concatenate([w[:, :2560], w[:, 2576:5648], w[:, 2560:2576], w[:, 5648:5664], pad], axis=1)


def _from_aligned(g):
    return jnp.concatenate([g[:, :2560], g[:, AL_DTF:AL_DTF + 16], g[:, 2560:AL_DTF], g[:, AL_DTF + 16:AL_DTF + 32]], axis=1)


def _lanes(v, at=0):
    return jnp.pad(v, ((0, 0), (at, LANES - at - v.shape[1])))


def _local_step(x, tgt, mod, w_al, halves, sp):
    d = D_MODEL
    sh1, sc1, g1, sh2, sc2, g2 = [mod[:, i * d:(i + 1) * d] for i in range(6)]
    dt_bias_l, a_log_l, f_bias_l = _lanes(sp["dt_bias"]), _lanes(sp["a_log"]), _lanes(sp["f_bias"], F_LANE)
    d_exp = jnp.repeat(sp["d_skip"], HEAD_DIM, axis=1)
    z_slab = lambda a: (a, d, AL_Z // d)

    (h1,), _ = _rowwise("mod1", lambda x, sc, sh: ([x * (1.0 + sc) + sh], []), [x], [sc1, sh1], [(d, BF16)], [])
    proj = _matmul("proj", h1, w_al, tn=1152)
    xc_all = _conv_fwd(proj, sp["conv_w"], sp["conv_b"])
    y_ssd, prevs = _ssd_fwd(xc_all, proj, dt_bias_l, a_log_l, d_exp)

    def gated_norm(y, z, w):
        g = y * _silu(z)
        return [g * lax.rsqrt(_rowmean(g * g) + RMS_EPS) * w], []

    (y_ssm,), _ = _rowwise("ssm_norm", gated_norm, [y_ssd, z_slab(proj)], [sp["ssm_norm_w"]], [(d, BF16)], [])
    cum = _fox_cum(proj, f_bias_l)
    qa, ka, vb = _attn_prep(proj, cum)
    o, lse, (g_out, g_fi, g_fo) = _attn_fwd(qa, ka, vb, halves)
    w_out = g_out.reshape(2 * d, d)
    w_fi = g_fi.transpose(1, 0, 2).reshape(d, D_FF)
    w_fo = g_fo.reshape(D_FF, d)
    (y_att,), _ = _rowwise("attn_norm", lambda o, w: ([o * lax.rsqrt(_rowmean(o * o) + RMS_EPS) * w], []),
                           [o], [sp["attn_norm_w"]], [(d, BF16)], [])
    y_mix = jnp.concatenate([y_ssm, y_att], axis=1)
    y = _matmul("out_proj", y_mix, w_out, tk=2048)

    def ln1_fwd(x, y, g1, sc2, sh2, lg, lb):
        r1 = ALPHA * x + (1.0 + g1) * y
        xh, _ = _ln_stats(r1)
        x1 = xh * lg + lb
        return [r1, x1 * (1.0 + sc2) + sh2], []

    (r1, h2), _ = _rowwise("ln1", ln1_fwd, [x, y], [g1, sc2, sh2, sp["ln1_g"], sp["ln1_b"]], [(d, F32), (d, BF16)], [])
    u, act = _matmul("ff_in", h2, w_fi, epilogue="relu2")
    ff = _matmul("ff_out", act, w_fo, tk=2048)

    def head(r1, ff, tgt, g2, l1g, l1b, l2g, l2b):
        xh1, _ = _ln_stats(r1)
        x1 = xh1 * l1g + l1b
        xh2, rstd2 = _ln_stats(ALPHA * x1 + (1.0 + g2) * ff)
        err = xh2 * l2g + l2b - tgt
        loss = 0.5 * jnp.sum(_rowmean(err * err))
        dx2 = err * (1.0 / d)
        dr2 = _ln_bwd(dx2 * l2g, xh2, rstd2)
        return ([dr2, (1.0 + g2) * dr2],
                [_colsum(dx2 * xh2), _colsum(dx2), _colsum(dr2 * ff), jnp.full((1, LANES), loss, F32)])

    (dr2, dff), (d_ln2_g, d_ln2_b, d_g2, loss) = _rowwise(
        "loss_ln2", head, [r1, ff, tgt], [g2, sp["ln1_g"], sp["ln1_b"], sp["ln2_g"], sp["ln2_b"]],
        [(d, F32), (d, BF16)], [d, d, d, LANES])
    du = _matmul("d_act", dff, w_fo, dims=NT, out_dtype=BF16, epilogue="relu2_bwd", extra=u)
    dw_fo = _matmul("dw_ff_out", act, dff, dims=TN, out_dtype=BF16, by_chip="rows")
    dw_fi = _matmul("dw_ff_in", h2, du, dims=TN, out_dtype=BF16, by_chip="cols")
    dh2 = _matmul("dh2", du, w_fi, dims=NT, tk=2048)

    def ln1_bwd(r1, dr2, dh2, y, sc2, g1, lg, lb):
        xh, rstd = _ln_stats(r1)
        x1 = xh * lg + lb
        dx1 = ALPHA * dr2 + dh2 * (1.0 + sc2)
        dr1 = _ln_bwd(dx1 * lg, xh, rstd)
        return ([dr1, (1.0 + g1) * dr1],
                [_colsum(dh2 * x1), _colsum(dh2), _colsum(dx1 * xh), _colsum(dx1), _colsum(dr1 * y)])

    (dr1, dy), (d_sc2, d_sh2, d_ln1_g, d_ln1_b, d_g1) = _rowwise(
        "ln1_bwd", ln1_bwd, [r1, dr2, dh2, y], [sc2, g1, sp["ln1_g"], sp["ln1_b"]], [(d, F32), (d, BF16)], [d] * 5)
    dymix = _matmul("dy_mix", dy, w_out, dims=NT)
    dw_out = _matmul("dw_out", y_mix, dy, dims=TN, out_dtype=BF16, by_chip="rows")

    def attn_norm_bwd(o, dyo, w):
        r = lax.rsqrt(_rowmean(o * o) + RMS_EPS)
        return [_rms_bwd(dyo * w, o, r)], [_colsum(dyo * o * r)]

    (do,), (d_attn_w,) = _rowwise("attn_norm_bwd", attn_norm_bwd, [o, (dymix, d, 1)], [sp["attn_norm_w"]], [(d, F32)], [d])

    def gated_norm_bwd(y, z, dyo, w):
        sg = _sigmoid(z)
        sz = z * sg
        g = y * sz
        r = lax.rsqrt(_rowmean(g * g) + RMS_EPS)
        dg = _rms_bwd(dyo * w, g, r)
        return [dg * sz, dg * y * (sg * (1.0 + z * (1.0 - sg)))], [_colsum(dyo * g * r)]

    (dy_ssd, dz), (d_ssm_w,) = _rowwise("ssm_norm_bwd", gated_norm_bwd, [y_ssd, z_slab(proj), (dymix, d, 0)],
                                        [sp["ssm_norm_w"]], [(d, F32), (d, BF16)], [d])
    dqa, dka, dv, landed = _attn_bwd(qa, ka, vb, o, lse, do, [dw_out, dw_fi, dw_fo])
    dq, dk, dcum = _attn_post(dqa, dka)
    dxc, ddt_tile, d_alog_l, d_dexp, d_dtb_l = _ssd_bwd(xc_all, proj, dt_bias_l, a_log_l, d_exp, prevs, dy_ssd)
    dtf, d_fb_l = _fox_cum_bwd(dcum, proj, f_bias_l, ddt_tile)
    dpre, d_conv_w, d_conv_b = _conv_bwd_pre(proj, sp["conv_w"], sp["conv_b"], dxc)
    dxbc = _conv_bwd_in(dpre, sp["conv_w"])
    dproj = jnp.concatenate([dz, dxbc, dq, dk, dv, dtf], axis=1)
    dw_al = _matmul("dw_in", h1, dproj, dims=TN, tn=1152, out_dtype=BF16)
    part_in = _from_aligned(dw_al).reshape(d, N_CHIPS, IN_COLS // N_CHIPS).transpose(1, 0, 2)
    dh1, landed_in = _matmul("dh1", dproj, w_al, dims=NT, tk=1152, carry=[part_in])

    def last(x, dr1, dh1, sc1):
        return [ALPHA * dr1 + dh1 * (1.0 + sc1)], [_colsum(dh1 * x), _colsum(dh1)]

    (dx,), (d_sc1, d_sh1) = _rowwise("grad_x", last, [x, dr1, dh1], [sc1], [(d, F32)], [d, d])

    small = {
        "mod": jnp.concatenate([d_sh1, d_sc1, d_g1, d_sh2, d_sc2, d_g2], axis=1),
        "conv_w": d_conv_w, "conv_b": d_conv_b,
        "dt_bias": d_dtb_l[:, :N_HEADS], "a_log": d_alog_l[:, :N_HEADS],
        "d_skip": jnp.sum(d_dexp.reshape(N_HEADS, HEAD_DIM), axis=1)[None, :],
        "ssm_norm_w": d_ssm_w, "f_bias": d_fb_l[:, F_LANE:F_LANE + N_HEADS], "attn_norm_w": d_attn_w,
        "ln1_g": d_ln1_g, "ln1_b": d_ln1_b, "ln2_g": d_ln2_g, "ln2_b": d_ln2_b, "loss": loss,
    }
    return dx, [landed_in, *landed], small


N_DEV = 8
N_CHIPS = 4
ANY = pl.BlockSpec(memory_space=pl.ANY)
VMEM_SPEC = pl.BlockSpec(memory_space=pltpu.VMEM)


def _place():
    x, y, c = lax.axis_index("x"), lax.axis_index("y"), lax.axis_index("c")
    return x, y, c


def _other_chips(x, y):
    return [(1 - x, y, 2 * (1 - x) + y), (x, 1 - y, 2 * x + 1 - y), (1 - x, 1 - y, 2 * (1 - x) + 1 - y)]


def _allgather_small(name, v):
    r, cdim = v.shape

    def body(v_ref, out_ref, send_sems, recv_sems, local_sem):
        x, y, c = _place()
        me = 4 * x + 2 * y + c
        mine = pltpu.make_async_copy(v_ref, out_ref.at[me], local_sem)
        mine.start()
        peers = []
        for rel in range(1, N_DEV):
            px = 1 - x if rel & 4 else x
            py = 1 - y if rel & 2 else y
            pc = 1 - c if rel & 1 else c
            peers.append((px, py, pc))

        def copy(rel, slot, to):
            return pltpu.make_async_remote_copy(src_ref=v_ref, dst_ref=out_ref.at[slot], send_sem=send_sems.at[rel],
                                                recv_sem=recv_sems.at[rel], device_id=to, device_id_type=MESH)

        sends = [copy(rel, me, peer) for rel, peer in enumerate(peers)]
        for cp in sends:
            cp.start()
        for rel, (px, py, pc) in enumerate(peers):
            copy(rel, 4 * px + 2 * py + pc, (x, y, c)).wait_recv()
        for cp in sends:
            cp.wait_send()
        mine.wait()

    return pl.pallas_call(
        body, name=name, out_shape=jax.ShapeDtypeStruct((N_DEV, r, cdim), v.dtype),
        in_specs=[VMEM_SPEC], out_specs=VMEM_SPEC,
        scratch_shapes=[pltpu.SemaphoreType.DMA((N_DEV - 1,)), pltpu.SemaphoreType.DMA((N_DEV - 1,)), pltpu.SemaphoreType.DMA],
    )(v)


def _gather_weights(halves):
    n = len(halves)

    def body(*refs):
        ins, outs = refs[:n], refs[n:2 * n]
        send_sems, recv_sems, local_sems = refs[2 * n:]
        x, y, c = _place()
        k_me = 2 * x + y
        me, sibling = (x, y, c), (x, y, 1 - c)
        chips = _other_chips(x, y)

        def copy(w, idx, k, half, to, src=None):
            rh = halves[w].shape[0]
            rows = outs[w].at[k, pl.ds(pl.multiple_of(half * rh, rh), rh), :]
            return pltpu.make_async_remote_copy(src_ref=rows if src is None else src, dst_ref=rows, send_sem=send_sems.at[w, idx],
                                                recv_sem=recv_sems.at[w, idx], device_id=to, device_id_type=MESH)

        started = []
        for w in range(n):
            rh = halves[w].shape[0]
            mine = pltpu.make_async_copy(ins[w], outs[w].at[k_me, pl.ds(pl.multiple_of(c * rh, rh), rh), :], local_sems.at[w])
            mine.start()
            started.append(mine)
        sends = []
        for w in range(n):
            sends.append(copy(w, 0, k_me, c, sibling, src=ins[w]))
            for j, (cx, cy, _) in enumerate(chips):
                sends.append(copy(w, 1 + j, k_me, c, (cx, cy, c), src=ins[w]))
        for cp in sends:
            cp.start()
        for w in range(n):
            for j, (_, _, kj) in enumerate(chips):
                copy(w, 1 + j, kj, c, me).wait_recv()
                fwd = copy(w, 4 + j, kj, c, sibling)
                fwd.start()
                sends.append(fwd)
        for w in range(n):
            copy(w, 0, k_me, 1 - c, me).wait_recv()
            for j, (_, _, kj) in enumerate(chips):
                copy(w, 4 + j, kj, 1 - c, me).wait_recv()
        for cp in sends:
            cp.wait_send()
        for mine in started:
            mine.wait()

    return pl.pallas_call(
        body, name="gather_weights",
        out_shape=[jax.ShapeDtypeStruct((N_CHIPS, 2 * h.shape[0], h.shape[1]), h.dtype) for h in halves],
        in_specs=[ANY] * n, out_specs=[ANY] * n,
        scratch_shapes=[pltpu.SemaphoreType.DMA((n, 7)), pltpu.SemaphoreType.DMA((n, 7)), pltpu.SemaphoreType.DMA((n,))],
    )(*halves)


def _peers(x, y, c):
    return [((1 - x) if rel & 4 else x, (1 - y) if rel & 2 else y, (1 - c) if rel & 1 else c) for rel in range(1, N_DEV)]


def _exchange_sems(n):
    return [pltpu.SemaphoreType.DMA((n, N_DEV - 1)), pltpu.SemaphoreType.DMA((n, N_DEV - 1)), pltpu.SemaphoreType.DMA((n,))]


def _gather_plan(ins, outs, send_sems, recv_sems, local_sems):
    x, y, c = _place()
    k_me = 2 * x + y
    peers = _peers(x, y, c)

    def copy(w, rel, k, half, to, src=None):
        rh = ins[w].shape[0]
        rows = outs[w].at[k, pl.ds(pl.multiple_of(half * rh, rh), rh), :]
        return pltpu.make_async_remote_copy(src_ref=rows if src is None else src, dst_ref=rows, send_sem=send_sems.at[w, rel],
                                            recv_sem=recv_sems.at[w, rel], device_id=to, device_id_type=MESH)

    n = len(ins)
    local = [pltpu.make_async_copy(ins[w], outs[w].at[k_me, pl.ds(pl.multiple_of(c * ins[w].shape[0], ins[w].shape[0]), ins[w].shape[0]), :],
                                   local_sems.at[w]) for w in range(n)]
    sends = [copy(w, rel, k_me, c, peer, src=ins[w]) for w in range(n) for rel, peer in enumerate(peers)]

    def start():
        for cp in local + sends:
            cp.start()

    def finish():
        for w in range(n):
            for rel, (px, py, pc) in enumerate(peers):
                copy(w, rel, 2 * px + py, pc, (x, y, c)).wait_recv()
        for cp in sends:
            cp.wait_send()
        for cp in local:
            cp.wait()

    return start, finish


def _reduce_plan(ins, outs, send_sems, recv_sems, local_sems):
    x, y, c = _place()
    me = 4 * x + 2 * y + c
    peers = _peers(x, y, c)

    def block(w, k, half):
        rh = ins[w].shape[1] // 2
        return ins[w].at[k, pl.ds(pl.multiple_of(half * rh, rh), rh), :]

    def copy(w, rel, src, slot, to):
        return pltpu.make_async_remote_copy(src_ref=src, dst_ref=outs[w].at[slot], send_sem=send_sems.at[w, rel],
                                            recv_sem=recv_sems.at[w, rel], device_id=to, device_id_type=MESH)

    n = len(ins)
    local = [pltpu.make_async_copy(block(w, 2 * x + y, c), outs[w].at[me], local_sems.at[w]) for w in range(n)]
    sends = [copy(w, rel, block(w, 2 * px + py, pc), me, (px, py, pc)) for w in range(n) for rel, (px, py, pc) in enumerate(peers)]

    def start():
        for cp in local + sends:
            cp.start()

    def finish():
        for w in range(n):
            for rel, (px, py, pc) in enumerate(peers):
                copy(w, rel, block(w, 2 * x + y, c), 4 * px + 2 * py + pc, (x, y, c)).wait_recv()
        for cp in sends:
            cp.wait_send()
        for cp in local:
            cp.wait()

    return start, finish


def _sum_blocks(name, parts):
    k, r, cdim = parts.shape
    tr = min(256, r)

    def body(p_ref, o_ref):
        acc = p_ref[0].astype(F32)
        for i in range(1, k):
            acc = acc + p_ref[i].astype(F32)
        o_ref[...] = acc

    return pl.pallas_call(
        body, name=name, grid=(r // tr,),
        in_specs=[pl.BlockSpec((k, tr, cdim), lambda i: (0, i, 0))], out_specs=pl.BlockSpec((tr, cdim), lambda i: (i, 0)),
        out_shape=jax.ShapeDtypeStruct((r, cdim), F32), compiler_params=_params(("parallel",)),
    )(parts)


def _pair_swap(halves):
    n = len(halves)

    def body(*refs):
        ins, outs = refs[:n], refs[n:2 * n]
        send_sems, recv_sems = refs[2 * n:]
        x, y, c = _place()
        cps = [pltpu.make_async_remote_copy(src_ref=ins[w], dst_ref=outs[w], send_sem=send_sems.at[w], recv_sem=recv_sems.at[w],
                                            device_id=(x, y, 1 - c), device_id_type=MESH) for w in range(n)]
        for cp in cps:
            cp.start()
        for cp in cps:
            cp.wait_recv()
        for cp in cps:
            cp.wait_send()

    return pl.pallas_call(
        body, name="pair_swap", out_shape=[jax.ShapeDtypeStruct(h.shape, h.dtype) for h in halves],
        in_specs=[ANY] * n, out_specs=[ANY] * n,
        scratch_shapes=[pltpu.SemaphoreType.DMA((n,)), pltpu.SemaphoreType.DMA((n,))],
    )(*halves)


ADA_SHARD = 6 * D_MODEL // N_CHIPS


def _mod_part(c_all, w_shard, b_shard):
    tn = 512

    def body(c_ref, w_ref, b_ref, o_ref):
        o_ref[...] = _dot(_silu(c_ref[...]).astype(BF16), w_ref[...].astype(BF16)) + b_ref[...]

    return pl.pallas_call(
        body, name="mod_part", grid=(ADA_SHARD // tn,),
        in_specs=[pl.BlockSpec((N_DEV, D_MODEL), lambda j: (0, 0)), pl.BlockSpec((D_MODEL, tn), lambda j: (0, j)),
                  pl.BlockSpec((1, tn), lambda j: (0, j))],
        out_specs=pl.BlockSpec((N_DEV, tn), lambda j: (0, j)),
        out_shape=jax.ShapeDtypeStruct((N_DEV, ADA_SHARD), F32), compiler_params=_params(("parallel",)),
    )(c_all, w_shard, b_shard)


def _w_ada_grad(c_all_t, dmod_shard):
    tm = 256

    def body(ct_ref, dm_ref, o_ref):
        act = _silu(ct_ref[...])
        acc = act[:, 0:1] * dm_ref[0:1, :]
        for dev in range(1, N_DEV):
            acc = acc + act[:, dev:dev + 1] * dm_ref[dev:dev + 1, :]
        o_ref[...] = acc

    return pl.pallas_call(
        body, name="w_ada_grad", grid=(D_MODEL // tm,),
        in_specs=[pl.BlockSpec((tm, N_DEV), lambda i: (i, 0)), pl.BlockSpec((N_DEV, ADA_SHARD), lambda i: (0, 0))],
        out_specs=pl.BlockSpec((tm, ADA_SHARD), lambda i: (i, 0)),
        out_shape=jax.ShapeDtypeStruct((D_MODEL, ADA_SHARD), F32), compiler_params=_params(("parallel",)),
    )(c_all_t, dmod_shard)


def _adamw_math(w, g, m, v):
    nm = ADAM_B1 * m + (1.0 - ADAM_B1) * g
    nv = ADAM_B2 * v + (1.0 - ADAM_B2) * jnp.square(g)
    m_hat = nm / (1.0 - ADAM_B1 ** ADAM_STEP)
    v_hat = nv / (1.0 - ADAM_B2 ** ADAM_STEP)
    return -ADAM_LR * (m_hat / (jnp.sqrt(v_hat) + ADAM_EPS) + ADAM_WD * w), nm, nv


def _adamw(name, w, g, m, v):
    r, cdim = w.shape
    tr = 256 if r % 256 == 0 else r

    def body(w_ref, g_ref, m_ref, v_ref, d_ref, nm_ref, nv_ref):
        d_ref[...], nm_ref[...], nv_ref[...] = _adamw_math(w_ref[...], g_ref[...], m_ref[...], v_ref[...])

    blk = pl.BlockSpec((tr, cdim), lambda i: (i, 0))
    return pl.pallas_call(
        body, name=name, grid=(r // tr,), in_specs=[blk] * 4, out_specs=[blk] * 3,
        out_shape=[jax.ShapeDtypeStruct((r, cdim), F32)] * 3, compiler_params=_params(("parallel",)),
    )(w, g, m, v)


def _adamw_pair(name, w, mine, other, m, v, c):
    r, cdim = w.shape
    rh = r // 2
    tr = min(256, rh)
    per = rh // tr

    def body(c_ref, w_ref, a_ref, b_ref, m_ref, v_ref, g_ref, d_ref, nm_ref, nv_ref):
        is_mine = (pl.program_id(0) // per) == c_ref[0]
        g = jnp.where(is_mine, a_ref[...], b_ref[...])
        g_ref[...] = g
        d_ref[...], nm_ref[...], nv_ref[...] = _adamw_math(w_ref[...], g, m_ref[...], v_ref[...])

    blk = pl.BlockSpec((tr, cdim), lambda i, c_ref: (i, 0))
    half = pl.BlockSpec((tr, cdim), lambda i, c_ref: (i % per, 0))
    return pl.pallas_call(
        body, name=name,
        grid_spec=pltpu.PrefetchScalarGridSpec(num_scalar_prefetch=1, grid=(r // tr,), in_specs=[blk, half, half, blk, blk], out_specs=[blk] * 4),
        out_shape=[jax.ShapeDtypeStruct((r, cdim), F32)] * 4, compiler_params=_params(("parallel",)),
    )(jnp.reshape(c, (1,)).astype(jnp.int32), w, mine, other, m, v)


SMALL = ["b_ada", "conv_b", "dt_bias", "a_log", "d_skip", "ssm_norm_w", "f_bias", "attn_norm_w", "ln1_g", "ln1_b", "ln2_g", "ln2_b"]


def _pad128(v):
    n = v.shape[1]
    return jnp.pad(v, ((0, 0), (0, -n % LANES)))


def _pack(vs):
    return jnp.concatenate([_pad128(v) for v in vs], axis=1)


def kernel(x, c, w_ada, b_ada, w_in, conv_w, conv_b, dt_bias, a_log, d_skip, ssm_norm_w, f_bias, attn_norm_w, w_out, ln1_g, ln1_b, w_ff_in, w_ff_out, ln2_g, ln2_b, loss_target, m_w_ada, m_b_ada, m_w_in, m_conv_w, m_conv_b, m_dt_bias, m_a_log, m_d_skip, m_ssm_norm_w, m_f_bias, m_attn_norm_w, m_w_out, m_ln1_g, m_ln1_b, m_w_ff_in, m_w_ff_out, m_ln2_g, m_ln2_b, v_w_ada, v_b_ada, v_w_in, v_conv_w, v_conv_b, v_dt_bias, v_a_log, v_d_skip, v_ssm_norm_w, v_f_bias, v_attn_norm_w, v_w_out, v_ln1_g, v_ln1_b, v_w_ff_in, v_w_ff_out, v_ln2_g, v_ln2_b):
    a = dict(b_ada=b_ada, conv_b=conv_b, dt_bias=dt_bias, a_log=a_log, d_skip=d_skip, ssm_norm_w=ssm_norm_w, f_bias=f_bias,
             attn_norm_w=attn_norm_w, ln1_g=ln1_g, ln1_b=ln1_b, ln2_g=ln2_g, ln2_b=ln2_b)
    ms = dict(b_ada=m_b_ada, conv_b=m_conv_b, dt_bias=m_dt_bias, a_log=m_a_log, d_skip=m_d_skip, ssm_norm_w=m_ssm_norm_w,
              f_bias=m_f_bias, attn_norm_w=m_attn_norm_w, ln1_g=m_ln1_g, ln1_b=m_ln1_b, ln2_g=m_ln2_g, ln2_b=m_ln2_b)
    vs = dict(b_ada=v_b_ada, conv_b=v_conv_b, dt_bias=v_dt_bias, a_log=v_a_log, d_skip=v_d_skip, ssm_norm_w=v_ssm_norm_w,
              f_bias=v_f_bias, attn_norm_w=v_attn_norm_w, ln1_g=v_ln1_g, ln1_b=v_ln1_b, ln2_g=v_ln2_g, ln2_b=v_ln2_b)
    xi, yi, ci = _place()
    chip = 2 * xi + yi
    me = 4 * xi + 2 * yi + ci
    d = D_MODEL
    conv_shard = CONV_DIM // N_CHIPS

    first = _allgather_small("gather_c", jnp.concatenate([c, conv_w[0].reshape(1, CONV_W * conv_shard)], axis=1))[:, 0]
    c_all = first[:, :d]
    conv_w_full = first[::2, d:].reshape(N_CHIPS, CONV_W, conv_shard).transpose(1, 0, 2).reshape(CONV_W, CONV_DIM)
    b_shard = lax.dynamic_slice_in_dim(b_ada, chip * ADA_SHARD, ADA_SHARD, axis=1)
    parts = _allgather_small("gather_mod", _mod_part(c_all, w_ada[0], b_shard))
    mod = lax.dynamic_index_in_dim(parts[::2], me, axis=1, keepdims=False).reshape(1, 6 * d)

    def my_half(w):
        rh = w.shape[0] // 2
        return lax.dynamic_slice_in_dim(w, ci * rh, rh, axis=0).astype(BF16)

    (g_in,) = _gather_weights([my_half(w_in[0])])
    w_al = _to_aligned(g_in.transpose(1, 0, 2).reshape(d, IN_COLS))

    sp = {n: a[n] for n in SMALL[1:]}
    sp["conv_w"] = conv_w_full
    halves = [my_half(w_out[0]), my_half(w_ff_in[0]), my_half(w_ff_out[0])]
    dx, landed, small = _local_step(x[0], loss_target[0], mod, w_al, halves, sp)

    names = ["mod"] + SMALL[1:]
    vec = _pack([small[n] for n in names] + [small["conv_w"].reshape(1, CONV_W * CONV_DIM), small["loss"]])
    every = _allgather_small("gather_small", vec)
    total = _sum_blocks("sum_small", jnp.broadcast_to(every, (N_DEV, SUBLANES, vec.shape[1])))[:1]
    widths = [6 * d] + [a[n].shape[1] for n in SMALL[1:]]
    offs = [0]
    for w in widths:
        offs.append(offs[-1] + w + (-w % LANES))
    g_small = {n: total[:, o:o + w] for n, o, w in zip(SMALL, offs, widths)}
    g_conv_w_full = total[:, offs[-1]:offs[-1] + CONV_W * CONV_DIM].reshape(CONV_W, CONV_DIM)
    loss = total[0, offs[-1] + CONV_W * CONV_DIM]
    dmod_shard = lax.dynamic_slice_in_dim(every[:, 0, :6 * d], chip * ADA_SHARD, ADA_SHARD, axis=1)
    g_w_ada = _w_ada_grad(c_all.T, dmod_shard)
    g_conv_w = lax.dynamic_slice_in_dim(g_conv_w_full, chip * conv_shard, conv_shard, axis=1)

    mine = [_sum_blocks("dev_sum_%d" % i, p) for i, p in enumerate(landed)]
    other = _pair_swap(mine)

    grads, deltas, new_m, new_v = {}, {}, {}, {}
    paired = dict(w_in=(w_in, m_w_in, v_w_in), w_out=(w_out, m_w_out, v_w_out), w_ff_in=(w_ff_in, m_w_ff_in, v_w_ff_in),
                  w_ff_out=(w_ff_out, m_w_ff_out, v_w_ff_out))
    for i, (n, (w, m, v)) in enumerate(paired.items()):
        g, dl, nm, nv = _adamw_pair("adamw_" + n, w[0], mine[i], other[i], m[0], v[0], ci)
        grads[n], deltas[n], new_m[n], new_v[n] = g[None], dl[None], nm[None], nv[None]
    for n, g, (w, m, v) in (("w_ada", g_w_ada, (w_ada, m_w_ada, v_w_ada)), ("conv_w", g_conv_w, (conv_w, m_conv_w, v_conv_w))):
        dl, nm, nv = _adamw("adamw_" + n, w[0], g, m[0], v[0])
        grads[n], deltas[n], new_m[n], new_v[n] = g[None], dl[None], nm[None], nv[None]
    g_pack = total[:, :offs[-1]]
    dl, nm, nv = _adamw("adamw_small", _pack([a[n] for n in SMALL]), g_pack, _pack([ms[n] for n in SMALL]), _pack([vs[n] for n in SMALL]))
    for n, o, w in zip(SMALL, offs, widths):
        grads[n], deltas[n], new_m[n], new_v[n] = g_small[n], dl[:, o:o + w], nm[:, o:o + w], nv[:, o:o + w]

    order = ["w_ada", "b_ada", "w_in", "conv_w", "conv_b", "dt_bias", "a_log", "d_skip", "ssm_norm_w", "f_bias", "attn_norm_w", "w_out",
             "ln1_g", "ln1_b", "w_ff_in", "w_ff_out", "ln2_g", "ln2_b"]
    return (loss, dx[None], *[grads[n] for n in order], *[deltas[n] for n in order], *[new_m[n] for n in order], *[new_v[n] for n in order])
```

```python
import functools

import jax
import jax.numpy as jnp
from jax import lax
from jax.experimental import pallas as pl
from jax.experimental.pallas import tpu as pltpu

F32, BF16 = jnp.float32, jnp.bfloat16

D_MODEL = 1024
N_HEADS = 16
HEAD_DIM = 64
N_PAIRS = N_HEADS // 2
SSM_GROUPS = 2
SSM_STATE = 128
CHUNK = 128
CONV_W = 4
CONV_DIM = 1536
D_FF = 4096
IN_COLS = 5664
ALPHA = 2.0 ** 0.25
LN_EPS = 1e-5
RMS_EPS = 1e-5
LANES = 128
SUBLANES = 8

AL_Z, AL_XS, AL_B, AL_C, AL_Q, AL_K, AL_V, AL_DTF = 0, 1024, 2048, 2304, 2560, 3584, 4608, 5632
AL_COLS = 5760
F_LANE = 16

ADAM_LR, ADAM_B1, ADAM_B2, ADAM_EPS, ADAM_WD, ADAM_STEP = 0.001, 0.9, 0.999, 1e-08, 0.01, 10

VMEM_LIMIT = 56 * 1024 * 1024
MESH = pl.DeviceIdType.MESH


def _params(sem=None):
    return pltpu.CompilerParams(dimension_semantics=sem, vmem_limit_bytes=VMEM_LIMIT)


def _sigmoid(x):
    return 1.0 / (1.0 + jnp.exp(-x))


def _silu(x):
    return x * _sigmoid(x)


def _softplus(x):
    return jnp.maximum(x, 0.0) + jnp.log(1.0 + jnp.exp(-jnp.abs(x)))


def _split3(a):
    hi = a.astype(BF16)
    r = a - hi.astype(F32)
    mid = r.astype(BF16)
    lo = (r - mid.astype(F32)).astype(BF16)
    return hi, mid, lo


def _dot(a, b, dims=((1,), (0,))):
    return lax.dot_general(a, b, (dims, ((), ())), preferred_element_type=F32)


NN, NT, TN = ((1,), (0,)), ((1,), (1,)), ((0,), (0,))


def _dot3(t, a):
    hi, mid, lo = _split3(a)
    return _dot(t, hi) + _dot(t, mid) + _dot(t, lo)


def _matmul(name, a, b, *, dims=NN, out_dtype=F32, tm=1024, tn=1024, tk=1024, by_chip=None, epi=None, carry=()):
    if dims == NN:
        (m, k), n = a.shape, b.shape[1]
    elif dims == NT:
        (m, k), n = a.shape, b.shape[0]
    else:
        (k, m), n = a.shape, b.shape[1]
    if by_chip == "rows":
        tm = min(tm, m // 4)
    if by_chip == "cols":
        tn = min(tn, n // 4)
    tm, tn, tk = min(tm, m), min(tn, n), min(tk, k)
    assert m % tm == 0 and n % tn == 0 and k % tk == 0, (name, m, n, k, tm, tn, tk)
    nk = k // tk
    if by_chip == "rows":
        per = m // 4 // tm
        out_spec = pl.BlockSpec((None, tm, tn), lambda i, j, l: (i // per, i % per, j))
        out_shape = jax.ShapeDtypeStruct((4, m // 4, n), out_dtype)
    elif by_chip == "cols":
        per = n // 4 // tn
        out_spec = pl.BlockSpec((None, tm, tn), lambda i, j, l: (j // per, i, j % per))
        out_shape = jax.ShapeDtypeStruct((4, m, n // 4), out_dtype)
    else:
        out_spec = pl.BlockSpec((tm, tn), lambda i, j, l: (i, j))
        out_shape = jax.ShapeDtypeStruct((m, n), out_dtype)
    a_spec = pl.BlockSpec((tk, tm), lambda i, j, l: (l, i)) if dims == TN else pl.BlockSpec((tm, tk), lambda i, j, l: (i, l))
    b_spec = pl.BlockSpec((tn, tk), lambda i, j, l: (j, l)) if dims == NT else pl.BlockSpec((tk, tn), lambda i, j, l: (l, j))

    tile = pl.BlockSpec((tm, tn), lambda i, j, l: (i, j))
    in_specs, args, out_specs, out_shape = [a_spec, b_spec], [a, b], [out_spec], [out_shape]
    fn, n_tiles, n_sums = None, 1, 0
    if epi is not None:
        fn, fulls, vecs, outs, sums = epi
        assert by_chip is None and (not sums or n == tn), name
        in_specs = in_specs + [tile] * len(fulls) + [pl.BlockSpec((1, tn), lambda i, j, l: (0, j))] * len(vecs)
        args = args + list(fulls) + list(vecs)
        out_specs = [tile] * len(outs) + [pl.BlockSpec((1, w), lambda i, j, l: (0, 0)) for w in sums]
        out_shape = [jax.ShapeDtypeStruct((m, n), dt) for dt in outs] + [jax.ShapeDtypeStruct((1, w), F32) for w in sums]
        n_tiles, n_sums = len(outs), len(sums)
    n_in, n_out, n_c = len(args), len(out_specs), len(carry)
    scratch = [pltpu.VMEM((tm, tn) if nk > 1 else (SUBLANES, LANES), F32)]
    if n_c:
        in_specs, args = in_specs + [ANY] * n_c, args + list(carry)
        out_specs = out_specs + [ANY] * n_c
        out_shape = out_shape + [jax.ShapeDtypeStruct((N_DEV, g.shape[1] // 2, g.shape[2]), g.dtype) for g in carry]
        scratch = scratch + _exchange_sems(n_c)
    gm, gn = m // tm, n // tn

    def body(*refs):
        a_ref, b_ref = refs[:2]
        ins, outs = refs[2:n_in], refs[n_in + n_c:n_in + n_c + n_out]
        acc_ref = refs[n_in + 2 * n_c + n_out]
        i, j, l = pl.program_id(0), pl.program_id(1), pl.program_id(2)
        if n_c:
            start, wait = _reduce_plan(refs[n_in:n_in + n_c], refs[n_in + n_c + n_out:n_in + 2 * n_c + n_out], *refs[n_in + 2 * n_c + n_out + 1:])
            pl.when((i == 0) & (j == 0) & (l == 0))(start)
        part = _dot(a_ref[...].astype(BF16), b_ref[...].astype(BF16), dims)

        def finish(res):
            if fn is None:
                outs[0][...] = res.astype(outs[0].dtype)
                return
            tiles, colsums = fn(res, *[r[...] for r in ins])
            for r, val in zip(outs[:n_tiles], tiles):
                r[...] = val.astype(r.dtype)
            if n_sums:
                @pl.when(i == 0)
                def _():
                    for r in outs[n_tiles:]:
                        r[...] = jnp.zeros_like(r)
                for r, val in zip(outs[n_tiles:], colsums):
                    r[...] += val

        if nk == 1:
            finish(part)
        else:
            @pl.when(l == 0)
            def _():
                acc_ref[...] = part

            @pl.when((l > 0) & (l < nk - 1))
            def _():
                acc_ref[...] += part

            @pl.when(l == nk - 1)
            def _():
                finish(acc_ref[...] + part)

        if n_c:
            pl.when((i == gm - 1) & (j == gn - 1) & (l == nk - 1))(wait)

    res = pl.pallas_call(
        body, name=name, grid=(gm, gn, nk),
        in_specs=in_specs, out_specs=out_specs, out_shape=out_shape, scratch_shapes=scratch,
        compiler_params=_params(("arbitrary",) * 3 if n_c or n_sums else ("parallel", "parallel", "arbitrary")),
    )(*args)
    return res[0] if len(res) == 1 else res


def _rowwise(name, fn, fulls, vecs, out_fulls, out_vecs, tr=256):
    fulls = [f if isinstance(f, tuple) else (f, f.shape[1], 0) for f in fulls]
    s = fulls[0][0].shape[0]
    tr = min(tr, s)
    nf, nv, nof, nov = len(fulls), len(vecs), len(out_fulls), len(out_vecs)
    in_specs = [pl.BlockSpec((tr, w), functools.partial(lambda i, cb: (i, cb), cb=cb)) for (_, w, cb) in fulls]
    in_specs += [pl.BlockSpec(v.shape, lambda i: (0, 0)) for v in vecs]
    out_shape = [jax.ShapeDtypeStruct((s, w), dt) for (w, dt) in out_fulls] + [jax.ShapeDtypeStruct((1, w), F32) for w in out_vecs]
    out_specs = [pl.BlockSpec((tr, w), lambda i: (i, 0)) for (w, _) in out_fulls] + [pl.BlockSpec((1, w), lambda i: (0, 0)) for w in out_vecs]

    def body(*refs):
        outs = refs[nf + nv:]
        of, ov = fn(*[r[...] for r in refs[:nf + nv]])
        for r, val in zip(outs[:nof], of):
            r[...] = val.astype(r.dtype)
        if nov:
            @pl.when(pl.program_id(0) == 0)
            def _():
                for r in outs[nof:]:
                    r[...] = jnp.zeros_like(r)
            for r, val in zip(outs[nof:], ov):
                r[...] += val

    res = pl.pallas_call(
        body, name=name, grid=(s // tr,), in_specs=in_specs, out_specs=out_specs, out_shape=out_shape,
        compiler_params=_params(("arbitrary",)),
    )(*[f[0] for f in fulls], *vecs)
    return res[:nof], res[nof:]


def _colsum(x):
    return jnp.sum(x, axis=0, keepdims=True)


def _rowmean(x):
    return jnp.mean(x, axis=-1, keepdims=True)


CONV_CB = 512
CONV_TR = 512


def _shift_down(u, halo, j):
    if j == 0:
        return u
    ru = pltpu.roll(u, j, 0)
    row8 = lax.broadcasted_iota(jnp.int32, halo.shape, 0)
    top = jnp.where(row8 < j, pltpu.roll(halo, j, 0), ru[:SUBLANES])
    return jnp.concatenate([top, ru[SUBLANES:]], axis=0)


def _shift_up(d, halo, j):
    if j == 0:
        return d
    tr = d.shape[0]
    rd = pltpu.roll(d, tr - j, 0)
    row8 = lax.broadcasted_iota(jnp.int32, halo.shape, 0)
    bot = jnp.where(row8 >= SUBLANES - j, pltpu.roll(halo, SUBLANES - j, 0), rd[tr - SUBLANES:])
    return jnp.concatenate([rd[:tr - SUBLANES], bot], axis=0)


def _conv_specs(s, tr, col0):
    cb0 = col0 // CONV_CB
    per8 = tr // SUBLANES
    blk = pl.BlockSpec((tr, CONV_CB), lambda cb, i: (i, cb0 + cb))
    prev = pl.BlockSpec((SUBLANES, CONV_CB), lambda cb, i: (jnp.maximum(i * per8 - 1, 0), cb0 + cb))
    return blk, prev


def _conv_pre(u, halo, w_ref, b_ref, first):
    halo = jnp.where(first, 0.0, halo)
    acc = b_ref[...] + w_ref[CONV_W - 1:CONV_W, :] * u
    shifted = [u]
    for j in range(1, CONV_W):
        sh = _shift_down(u, halo, j)
        shifted.append(sh)
        acc = acc + w_ref[CONV_W - 1 - j:CONV_W - j, :] * sh
    return acc, shifted


def _conv_fwd(proj, conv_w, conv_b):
    s = proj.shape[0]
    tr = min(CONV_TR, s)
    blk, prev = _conv_specs(s, tr, AL_XS)

    def body(u_ref, h_ref, w_ref, b_ref, o_ref):
        pre, _ = _conv_pre(u_ref[...], h_ref[...], w_ref, b_ref, pl.program_id(1) == 0)
        o_ref[...] = _silu(pre)

    return pl.pallas_call(
        body, name="conv_fwd", grid=(CONV_DIM // CONV_CB, s // tr),
        in_specs=[blk, prev, pl.BlockSpec((CONV_W, CONV_CB), lambda cb, i: (0, cb)), pl.BlockSpec((1, CONV_CB), lambda cb, i: (0, cb))],
        out_specs=pl.BlockSpec((tr, CONV_CB), lambda cb, i: (i, cb)),
        out_shape=jax.ShapeDtypeStruct((s, CONV_DIM), F32),
        compiler_params=_params(("parallel", "parallel")),
    )(proj, proj, conv_w, conv_b)


def _conv_bwd_pre(proj, conv_w, conv_b, dxc):
    s = proj.shape[0]
    tr = min(CONV_TR, s)
    blk, prev = _conv_specs(s, tr, AL_XS)

    def body(u_ref, h_ref, w_ref, b_ref, d_ref, dpre_ref, dw_ref, db_ref):
        i = pl.program_id(1)
        pre, shifted = _conv_pre(u_ref[...], h_ref[...], w_ref, b_ref, i == 0)
        sg = _sigmoid(pre)
        dpre = d_ref[...] * (sg * (1.0 + pre * (1.0 - sg)))
        dpre_ref[...] = dpre

        @pl.when(i == 0)
        def _():
            dw_ref[...] = jnp.zeros_like(dw_ref)
            db_ref[...] = jnp.zeros_like(db_ref)

        db_ref[...] += _colsum(dpre)
        for j in range(CONV_W):
            dw_ref[CONV_W - 1 - j:CONV_W - j, :] += _colsum(dpre * shifted[j])

    own = pl.BlockSpec((tr, CONV_CB), lambda cb, i: (i, cb))
    wspec = pl.BlockSpec((CONV_W, CONV_CB), lambda cb, i: (0, cb))
    bspec = pl.BlockSpec((1, CONV_CB), lambda cb, i: (0, cb))
    return pl.pallas_call(
        body, name="conv_bwd_pre", grid=(CONV_DIM // CONV_CB, s // tr),
        in_specs=[blk, prev, wspec, bspec, own], out_specs=[own, wspec, bspec],
        out_shape=[jax.ShapeDtypeStruct((s, CONV_DIM), F32), jax.ShapeDtypeStruct((CONV_W, CONV_DIM), F32),
                   jax.ShapeDtypeStruct((1, CONV_DIM), F32)],
        compiler_params=_params(("parallel", "arbitrary")),
    )(proj, proj, conv_w, conv_b, dxc)


def _conv_bwd_in(dpre, conv_w):
    s = dpre.shape[0]
    tr = min(CONV_TR, s)
    per8 = tr // SUBLANES
    last8 = s // SUBLANES - 1
    nb = s // tr

    def body(d_ref, n_ref, w_ref, o_ref):
        d = d_ref[...]
        halo = jnp.where(pl.program_id(1) == nb - 1, 0.0, n_ref[...])
        acc = w_ref[CONV_W - 1:CONV_W, :] * d
        for j in range(1, CONV_W):
            acc = acc + w_ref[CONV_W - 1 - j:CONV_W - j, :] * _shift_up(d, halo, j)
        o_ref[...] = acc.astype(o_ref.dtype)

    own = pl.BlockSpec((tr, CONV_CB), lambda cb, i: (i, cb))
    nxt = pl.BlockSpec((SUBLANES, CONV_CB), lambda cb, i: (jnp.minimum((i + 1) * per8, last8), cb))
    return pl.pallas_call(
        body, name="conv_bwd_in", grid=(CONV_DIM // CONV_CB, nb),
        in_specs=[own, nxt, pl.BlockSpec((CONV_W, CONV_CB), lambda cb, i: (0, cb))], out_specs=own,
        out_shape=jax.ShapeDtypeStruct((s, CONV_DIM), BF16),
        compiler_params=_params(("parallel", "parallel")),
    )(dpre, dpre, conv_w)


XC_B, XC_C = 1024, 1280


def _tile_iotas():
    row = lax.broadcasted_iota(jnp.int32, (CHUNK, LANES), 0)
    lane = lax.broadcasted_iota(jnp.int32, (CHUNK, LANES), 1)
    return row, lane


def _ssd_scalars(dtf_ref, bias_ref, alog_ref, row, lane):
    head = lane[:1] < N_HEADS
    raw = dtf_ref[...] + bias_ref[...]
    dt = _softplus(raw)
    a_neg = jnp.where(head, -jnp.exp(alog_ref[...]), 0.0)
    a = dt * a_neg
    tril = (row >= lane).astype(BF16)
    s = _dot3(tril, a)
    return raw, dt, a_neg, s


def _pair(v, j, lo):
    return jnp.where(lo, v[:, 2 * j:2 * j + 1], v[:, 2 * j + 1:2 * j + 2])


def _head_sum(x, lo, hh):
    return jnp.sum(jnp.where(lo == (hh == 0), x, 0.0), axis=1, keepdims=True)


def _decay_masks(s, h, row, lane):
    s_col = jnp.broadcast_to(s[:, h:h + 1], (CHUNK, LANES))
    s_row = s_col.T
    lm = jnp.where(row >= lane, jnp.exp(s_col - s_row), 0.0)
    lmt = jnp.where(row <= lane, jnp.exp(s_row - s_col), 0.0)
    return lm, lmt


def _ssd_fwd(xc_all, proj, dt_bias_l, a_log_l, d_exp):
    s_len = xc_all.shape[0]
    nc = s_len // CHUNK

    def body(x_ref, dtf_ref, bias_ref, alog_ref, dexp_ref, y_ref, prevs_ref, state_ref):
        @pl.when(pl.program_id(0) == 0)
        def _():
            state_ref[...] = jnp.zeros_like(state_ref)

        row, lane = _tile_iotas()
        lo = lane < HEAD_DIM
        _, dt, _, s = _ssd_scalars(dtf_ref, bias_ref, alog_ref, row, lane)
        tot = s[CHUNK - 1:CHUNK, :]
        for g in range(SSM_GROUPS):
            bg = x_ref[:, XC_B + g * SSM_STATE:XC_B + (g + 1) * SSM_STATE].astype(BF16)
            cg = x_ref[:, XC_C + g * SSM_STATE:XC_C + (g + 1) * SSM_STATE].astype(BF16)
            cb = _dot(cg, bg, NT)
            for j in range(g * 4, g * 4 + 4):
                xs_p = x_ref[:, j * LANES:(j + 1) * LANES]
                dt_p, s_p, tot_p = _pair(dt, j, lo), _pair(s, j, lo), _pair(tot, j, lo[:1])
                xc_p = xs_p * dt_p
                xc_b = xc_p.astype(BF16)
                yd = []
                for hh in range(2):
                    lm, _ = _decay_masks(s, 2 * j + hh, row, lane)
                    yd.append(_dot((cb * lm).astype(BF16), xc_b))
                prev = state_ref[j]
                prevs_ref[0, j] = prev
                yo = _dot(cg, prev.astype(BF16)) * jnp.exp(s_p)
                y_ref[:, j * LANES:(j + 1) * LANES] = jnp.where(lo, yd[0], yd[1]) + yo + dexp_ref[:, j * LANES:(j + 1) * LANES] * xs_p
                to_end = jnp.exp(tot_p - s_p)
                state_ref[j] = jnp.exp(tot_p) * prev + _dot(bg, (xc_p * to_end).astype(BF16), TN)

    vec = lambda w: pl.BlockSpec((1, w), lambda c: (0, 0))
    return pl.pallas_call(
        body, name="ssd_fwd", grid=(nc,),
        in_specs=[pl.BlockSpec((CHUNK, CONV_DIM), lambda c: (c, 0)), pl.BlockSpec((CHUNK, LANES), lambda c: (c, AL_DTF // LANES)),
                  vec(LANES), vec(LANES), vec(D_MODEL)],
        out_specs=[pl.BlockSpec((CHUNK, D_MODEL), lambda c: (c, 0)), pl.BlockSpec((1, N_PAIRS, SSM_STATE, LANES), lambda c: (c, 0, 0, 0))],
        out_shape=[jax.ShapeDtypeStruct((s_len, D_MODEL), F32), jax.ShapeDtypeStruct((nc, N_PAIRS, SSM_STATE, LANES), F32)],
        scratch_shapes=[pltpu.VMEM((N_PAIRS, SSM_STATE, LANES), F32)],
        compiler_params=_params(("arbitrary",)),
    )(xc_all, proj, dt_bias_l, a_log_l, d_exp)


def _ssd_bwd(xc_all, proj, dt_bias_l, a_log_l, d_exp, prevs, dy):
    s_len = xc_all.shape[0]
    nc = s_len // CHUNK

    def body(x_ref, dtf_ref, bias_ref, alog_ref, dexp_ref, prevs_ref, dy_ref, dx_ref, ddt_ref, da_ref, dd_ref, dbias_ref, dstate_ref):
        @pl.when(pl.program_id(0) == 0)
        def _():
            dstate_ref[...] = jnp.zeros_like(dstate_ref)
            da_ref[...] = jnp.zeros_like(da_ref)
            dd_ref[...] = jnp.zeros_like(dd_ref)
            dbias_ref[...] = jnp.zeros_like(dbias_ref)

        row, lane = _tile_iotas()
        lo = lane < HEAD_DIM
        last = row == CHUNK - 1
        raw, dt, a_neg, s = _ssd_scalars(dtf_ref, bias_ref, alog_ref, row, lane)
        tot = s[CHUNK - 1:CHUNK, :]
        ds_acc = jnp.zeros((CHUNK, LANES), F32)
        ddt_acc = jnp.zeros((CHUNK, LANES), F32)
        for g in range(SSM_GROUPS):
            bcol = slice(XC_B + g * SSM_STATE, XC_B + (g + 1) * SSM_STATE)
            ccol = slice(XC_C + g * SSM_STATE, XC_C + (g + 1) * SSM_STATE)
            bg = x_ref[:, bcol].astype(BF16)
            cg = x_ref[:, ccol].astype(BF16)
            cb = _dot(cg, bg, NT)
            cbt = _dot(bg, cg, NT)
            dcb = jnp.zeros((CHUNK, LANES), F32)
            dcbt = jnp.zeros((CHUNK, LANES), F32)
            db_acc = jnp.zeros((CHUNK, LANES), F32)
            dc_acc = jnp.zeros((CHUNK, LANES), F32)
            for j in range(g * 4, g * 4 + 4):
                cols = slice(j * LANES, (j + 1) * LANES)
                xs_p, dy_p = x_ref[:, cols], dy_ref[:, cols]
                dt_p, s_p, tot_p = _pair(dt, j, lo), _pair(s, j, lo), _pair(tot, j, lo[:1])
                xc_p = xs_p * dt_p
                xc_b, dy_b = xc_p.astype(BF16), dy_p.astype(BF16)
                e_p, f_p, etot_p = jnp.exp(s_p), jnp.exp(tot_p - s_p), jnp.exp(tot_p)
                prev, dnext = prevs_ref[0, j], dstate_ref[j]
                prev_b, dnext_b = prev.astype(BF16), dnext.astype(BF16)
                dd_ref[:, cols] += _colsum(dy_p * xs_p)
                dxs_p = dexp_ref[:, cols] * dy_p
                cp = _dot(cg, prev_b)
                gy = (dy_p * e_p).astype(BF16)
                dc_acc += _dot(gy, prev_b, NT)
                dstate_ref[j] = etot_p * dnext + _dot(cg, gy, TN)
                de = dy_p * cp * e_p
                bds = _dot(bg, dnext_b)
                db_acc += _dot((xc_p * f_p).astype(BF16), dnext_b, NT)
                dxc_p = bds * f_p
                df = bds * xc_p * f_p
                dtot_p = _colsum(dnext * prev) * etot_p + _colsum(df)
                dsl = de - df + jnp.where(last, dtot_p, 0.0)
                for hh in range(2):
                    h = 2 * j + hh
                    mine = lo == (hh == 0)
                    lm, lmt = _decay_masks(s, h, row, lane)
                    dy_h = jnp.where(mine, dy_p, 0.0).astype(BF16)
                    xc_h = jnp.where(mine, xc_p, 0.0).astype(BF16)
                    dm = _dot(dy_h, xc_b, NT)
                    dmt = _dot(xc_h, dy_b, NT)
                    mt = cbt * lmt
                    dxc_p += _dot(mt.astype(BF16), dy_h)
                    ds_h = (jnp.sum(dm * cb * lm, axis=1, keepdims=True) - jnp.sum(dmt * mt, axis=1, keepdims=True)
                            + _head_sum(dsl, lo, hh))
                    ds_acc += jnp.where(lane == h, ds_h, 0.0)
                    dcb += dm * lm
                    dcbt += dmt * lmt
                    ddt_acc += jnp.where(lane == h, _head_sum(dxc_p * xs_p, lo, hh), 0.0)
                dx_ref[:, cols] = dxs_p + dxc_p * dt_p
            dx_ref[:, ccol] = dc_acc + _dot(dcb.astype(BF16), bg)
            dx_ref[:, bcol] = db_acc + _dot(dcbt.astype(BF16), cg)
        triu = (row <= lane).astype(BF16)
        da = _dot3(triu, ds_acc)
        ddt = ddt_acc + da * a_neg
        da_ref[...] += _colsum(da * dt) * a_neg[:1]
        ddt_raw = jnp.where(lane < N_HEADS, ddt * _sigmoid(raw), 0.0)
        dbias_ref[...] += _colsum(ddt_raw)
        ddt_ref[...] = ddt_raw

    rev = lambda c: nc - 1 - c
    vec = lambda w: pl.BlockSpec((1, w), lambda c: (0, 0))
    return pl.pallas_call(
        body, name="ssd_bwd", grid=(nc,),
        in_specs=[pl.BlockSpec((CHUNK, CONV_DIM), lambda c: (rev(c), 0)), pl.BlockSpec((CHUNK, LANES), lambda c: (rev(c), AL_DTF // LANES)),
                  vec(LANES), vec(LANES), vec(D_MODEL),
                  pl.BlockSpec((1, N_PAIRS, SSM_STATE, LANES), lambda c: (rev(c), 0, 0, 0)),
                  pl.BlockSpec((CHUNK, D_MODEL), lambda c: (rev(c), 0))],
        out_specs=[pl.BlockSpec((CHUNK, CONV_DIM), lambda c: (rev(c), 0)), pl.BlockSpec((CHUNK, LANES), lambda c: (rev(c), 0)),
                   vec(LANES), vec(D_MODEL), vec(LANES)],
        out_shape=[jax.ShapeDtypeStruct((s_len, CONV_DIM), F32), jax.ShapeDtypeStruct((s_len, LANES), F32),
                   jax.ShapeDtypeStruct((1, LANES), F32), jax.ShapeDtypeStruct((1, D_MODEL), F32), jax.ShapeDtypeStruct((1, LANES), F32)],
        scratch_shapes=[pltpu.VMEM((N_PAIRS, SSM_STATE, LANES), F32)],
        compiler_params=_params(("arbitrary",)),
    )(xc_all, proj, dt_bias_l, a_log_l, d_exp, prevs, dy)


AUG_C, AUG_ONE = 64, 67
NEG = -1e30
ATT_T = 512


def _fox_cum(proj, f_bias_l):
    s_len = proj.shape[0]
    nc = s_len // CHUNK

    def body(dtf_ref, fb_ref, cum_ref):
        row, lane = _tile_iotas()
        tril = (row >= lane).astype(BF16)

        def step(c, carry):
            rows = pl.ds(pl.multiple_of(c * CHUNK, CHUNK), CHUNK)
            lf = -_softplus(-(dtf_ref[rows, :] + fb_ref[...]))
            lf = jnp.where(lane < N_HEADS, pltpu.roll(lf, LANES - F_LANE, 1), 0.0)
            cs = _dot3(tril, lf) + carry
            cum_ref[rows, :] = cs
            return cs[CHUNK - 1:CHUNK, :]

        lax.fori_loop(0, nc, step, jnp.zeros((1, LANES), F32))

    return pl.pallas_call(
        body, name="fox_cum", grid=(1,),
        in_specs=[pl.BlockSpec((s_len, LANES), lambda i: (0, AL_DTF // LANES)), pl.BlockSpec((1, LANES), lambda i: (0, 0))],
        out_specs=pl.BlockSpec((s_len, LANES), lambda i: (0, 0)),
        out_shape=jax.ShapeDtypeStruct((s_len, LANES), F32),
        compiler_params=_params(("arbitrary",)),
    )(proj, f_bias_l)


def _fox_cum_bwd(dcum, proj, f_bias_l, ddt_tile):
    s_len = proj.shape[0]
    nc = s_len // CHUNK

    def body(dcum_ref, dtf_ref, fb_ref, ddt_ref, out_ref, dfb_ref):
        row, lane = _tile_iotas()
        triu = (row <= lane).astype(BF16)
        is_f = (lane >= F_LANE) & (lane < F_LANE + N_HEADS)

        def step(t, carry):
            run, dfb = carry
            rows = pl.ds(pl.multiple_of((nc - 1 - t) * CHUNK, CHUNK), CHUNK)
            rc = _dot3(triu, dcum_ref[rows, :]) + run
            sg = _sigmoid(-(dtf_ref[rows, :] + fb_ref[...]))
            df = jnp.where(is_f, pltpu.roll(rc, F_LANE, 1) * sg, 0.0)
            out_ref[rows, :] = (df + ddt_ref[rows, :]).astype(out_ref.dtype)
            return rc[0:1, :], dfb + _colsum(df)

        _, dfb = lax.fori_loop(0, nc, step, (jnp.zeros((1, LANES), F32), jnp.zeros((1, LANES), F32)))
        dfb_ref[...] = dfb

    whole = pl.BlockSpec((s_len, LANES), lambda i: (0, 0))
    vec = pl.BlockSpec((1, LANES), lambda i: (0, 0))
    return pl.pallas_call(
        body, name="fox_cum_bwd", grid=(1,),
        in_specs=[whole, pl.BlockSpec((s_len, LANES), lambda i: (0, AL_DTF // LANES)), vec, whole],
        out_specs=[whole, vec],
        out_shape=[jax.ShapeDtypeStruct((s_len, LANES), BF16), jax.ShapeDtypeStruct((1, LANES), F32)],
        compiler_params=_params(("arbitrary",)),
    )(dcum, proj, f_bias_l, ddt_tile)


def _attn_prep(proj, cum):
    s_len = proj.shape[0]
    tr = min(512, s_len)

    def body(q_ref, k_ref, v_ref, cum_ref, qa_ref, ka_ref, vb_ref):
        p = pl.program_id(0)
        lane = lax.broadcasted_iota(jnp.int32, (tr, LANES), 1)
        lo = lane < HEAD_DIM
        c = cum_ref[...]
        c1 = c.astype(BF16).astype(F32)
        r = c - c1
        c2 = r.astype(BF16).astype(F32)
        c3 = (r - c2).astype(BF16).astype(F32)
        q, k = q_ref[...] * (HEAD_DIM ** -0.5), k_ref[...]
        for hh in range(2):
            col = lambda x: jnp.sum(jnp.where(lane == 2 * p + hh, x, 0.0), axis=1, keepdims=True)
            a1, a2, a3 = col(c1), col(c2), col(c3)
            qh = q if hh == 0 else pltpu.roll(q, HEAD_DIM, 1)
            kh = k if hh == 0 else pltpu.roll(k, HEAD_DIM, 1)
            q_aug = jnp.where(lane == AUG_C, a1, jnp.where(lane == AUG_C + 1, a2, jnp.where(lane == AUG_C + 2, a3,
                              jnp.where(lane < AUG_ONE + 3, 1.0, 0.0))))
            k_aug = jnp.where(lane < AUG_ONE, 1.0, jnp.where(lane == AUG_ONE, -a1, jnp.where(lane == AUG_ONE + 1, -a2,
                              jnp.where(lane == AUG_ONE + 2, -a3, 0.0))))
            qa_ref[hh] = jnp.where(lo, qh, q_aug).astype(BF16)
            ka_ref[hh] = jnp.where(lo, kh, k_aug).astype(BF16)
        vb_ref[...] = v_ref[...].astype(BF16)

    slab = lambda col0: pl.BlockSpec((tr, LANES), lambda p, i: (i, col0 // LANES + p))
    heads = pl.BlockSpec((2, tr, LANES), lambda p, i: (p, i, 0))
    return pl.pallas_call(
        body, name="attn_prep", grid=(N_PAIRS, s_len // tr),
        in_specs=[slab(AL_Q), slab(AL_K), slab(AL_V), pl.BlockSpec((tr, LANES), lambda p, i: (i, 0))],
        out_specs=[heads, heads, pl.BlockSpec((tr, LANES), lambda p, i: (i, p))],
        out_shape=[jax.ShapeDtypeStruct((N_HEADS, s_len, LANES), BF16), jax.ShapeDtypeStruct((N_HEADS, s_len, LANES), BF16),
                   jax.ShapeDtypeStruct((s_len, D_MODEL), BF16)],
        compiler_params=_params(("parallel", "parallel")),
    )(proj, proj, proj, cum)


def _attn_fwd(qa, ka, vb, halves):
    s_len = vb.shape[0]
    t = min(ATT_T, s_len)
    nq = s_len // t
    n = len(halves)

    def body(qa_ref, ka_ref, vb_ref, *rest):
        o_ref, lse_ref = rest[n:n + 2]
        start, finish = _gather_plan(rest[:n], rest[n + 2:2 * n + 2], *rest[2 * n + 2:])
        i = pl.program_id(1)
        pl.when((pl.program_id(0) == 0) & (i == 0))(start)
        row = lax.broadcasted_iota(jnp.int32, (t, t), 0)
        col = lax.broadcasted_iota(jnp.int32, (t, t), 1)
        lo = lax.broadcasted_iota(jnp.int32, (t, LANES), 1) < HEAD_DIM
        qs = (qa_ref[0], qa_ref[1])

        def block(j, carry, masked):
            rows = pl.ds(pl.multiple_of(j * t, t), t)
            v = vb_ref[rows, :]
            new = []
            for hh in range(2):
                m, l, acc = carry[hh]
                s = _dot(qs[hh], ka_ref[hh, rows, :], NT)
                if masked:
                    s = jnp.where(row >= col, s, NEG)
                m_new = jnp.maximum(m, jnp.max(s, axis=1, keepdims=True))
                alpha = jnp.exp(m - m_new)
                p = jnp.exp(s - m_new)
                new.append((m_new, alpha * l + jnp.sum(p, axis=1, keepdims=True), alpha * acc + _dot(p.astype(BF16), v)))
            return tuple(new)

        init = (jnp.full((t, 1), NEG, F32), jnp.zeros((t, 1), F32), jnp.zeros((t, LANES), F32))
        carry = lax.fori_loop(0, i, functools.partial(block, masked=False), (init, init))
        (m0, l0, acc0), (m1, l1, acc1) = block(i, carry, True)
        o_ref[...] = jnp.where(lo, acc0 / l0, acc1 / l1)
        lse_ref[...] = jnp.where(lo, m0 + jnp.log(l0), m1 + jnp.log(l1))
        pl.when((pl.program_id(0) == N_PAIRS - 1) & (i == nq - 1))(finish)

    out = pl.BlockSpec((t, LANES), lambda p, i: (i, p))
    res = pl.pallas_call(
        body, name="attn_fwd", grid=(N_PAIRS, nq),
        in_specs=[pl.BlockSpec((2, t, LANES), lambda p, i: (p, i, 0)), pl.BlockSpec((2, s_len, LANES), lambda p, i: (p, 0, 0)),
                  pl.BlockSpec((s_len, LANES), lambda p, i: (0, p))] + [ANY] * n,
        out_specs=[out, out] + [ANY] * n,
        out_shape=[jax.ShapeDtypeStruct((s_len, D_MODEL), F32), jax.ShapeDtypeStruct((s_len, D_MODEL), F32)]
        + [jax.ShapeDtypeStruct((N_CHIPS, *h.shape), h.dtype) for h in halves],
        scratch_shapes=_exchange_sems(n),
        compiler_params=_params(("arbitrary", "arbitrary")),
    )(qa, ka, vb, *halves)
    return res[0], res[1], res[2:]


def _attn_bwd(qa, ka, vb, o, lse, do, parts):
    s_len = vb.shape[0]
    t = min(ATT_T, s_len)
    nq = s_len // t
    n = len(parts)

    def body(qa_ref, ka_ref, vb_ref, o_ref, lse_ref, do_ref, *rest):
        dqa_ref, dka_ref, dv_ref = rest[n:n + 3]
        start, finish = _reduce_plan(rest[:n], rest[n + 3:2 * n + 3], *rest[2 * n + 3:])
        j = pl.program_id(1)
        pl.when((pl.program_id(0) == 0) & (j == 0))(start)

        @pl.when(j == 0)
        def _():
            dqa_ref[...] = jnp.zeros_like(dqa_ref)

        row = lax.broadcasted_iota(jnp.int32, (t, t), 0)
        col = lax.broadcasted_iota(jnp.int32, (t, t), 1)
        lo = lax.broadcasted_iota(jnp.int32, (t, LANES), 1) < HEAD_DIM
        v = vb_ref[...]
        ks = (ka_ref[0], ka_ref[1])

        def block(i, carry, masked):
            dk, dv = list(carry[:2]), carry[2]
            rows = pl.ds(pl.multiple_of(i * t, t), t)
            do_p, o_p, lse_p = do_ref[rows, :], o_ref[rows, :], lse_ref[rows, :]
            for hh in range(2):
                q = qa_ref[hh, rows, :]
                do_h = jnp.where(lo == (hh == 0), do_p, 0.0)
                delta = jnp.sum(do_h * o_p, axis=1, keepdims=True)
                s = _dot(q, ks[hh], NT)
                if masked:
                    s = jnp.where(row >= col, s, NEG)
                p = jnp.exp(s - lse_p[:, hh * HEAD_DIM:hh * HEAD_DIM + 1])
                do_b = do_h.astype(BF16)
                ds = (p * (_dot(do_b, v, NT) - delta)).astype(BF16)
                dv = dv + _dot(p.astype(BF16), do_b, TN)
                dk[hh] = dk[hh] + _dot(ds, q, TN)
                dqa_ref[hh, rows, :] += _dot(ds, ks[hh])
            return dk[0], dk[1], dv

        zero = jnp.zeros((t, LANES), F32)
        carry = block(j, (zero, zero, zero), True)
        dk0, dk1, dv = lax.fori_loop(j + 1, nq, functools.partial(block, masked=False), carry)
        dka_ref[0] = dk0
        dka_ref[1] = dk1
        dv_ref[...] = dv.astype(dv_ref.dtype)
        pl.when((pl.program_id(0) == N_PAIRS - 1) & (j == nq - 1))(finish)

    whole_pair = pl.BlockSpec((2, s_len, LANES), lambda p, j: (p, 0, 0))
    blk_pair = pl.BlockSpec((2, t, LANES), lambda p, j: (p, j, 0))
    whole_cols = pl.BlockSpec((s_len, LANES), lambda p, j: (0, p))
    blk_cols = pl.BlockSpec((t, LANES), lambda p, j: (j, p))
    res = pl.pallas_call(
        body, name="attn_bwd", grid=(N_PAIRS, nq),
        in_specs=[whole_pair, blk_pair, blk_cols, whole_cols, whole_cols, whole_cols] + [ANY] * n,
        out_specs=[whole_pair, blk_pair, blk_cols] + [ANY] * n,
        out_shape=[jax.ShapeDtypeStruct((N_HEADS, s_len, LANES), F32), jax.ShapeDtypeStruct((N_HEADS, s_len, LANES), F32),
                   jax.ShapeDtypeStruct((s_len, D_MODEL), BF16)]
        + [jax.ShapeDtypeStruct((N_DEV, g.shape[1] // 2, g.shape[2]), g.dtype) for g in parts],
        scratch_shapes=_exchange_sems(n),
        compiler_params=_params(("arbitrary", "arbitrary")),
    )(qa, ka, vb, o, lse, do, *parts)
    return res[0], res[1], res[2], res[3:]


def _attn_post(dqa, dka):
    s_len = dqa.shape[1]
    tr = min(256, s_len)

    def body(dqa_ref, dka_ref, dq_ref, dk_ref, dcum_ref):
        lane = lax.broadcasted_iota(jnp.int32, (tr, LANES), 1)
        lo = lane < HEAD_DIM
        dcum = jnp.zeros((tr, LANES), F32)
        for p in range(N_PAIRS):
            cols = slice(p * LANES, (p + 1) * LANES)
            a0, a1, b0, b1 = dqa_ref[2 * p], dqa_ref[2 * p + 1], dka_ref[2 * p], dka_ref[2 * p + 1]
            dq_ref[:, cols] = (jnp.where(lo, a0, pltpu.roll(a1, HEAD_DIM, 1)) * (HEAD_DIM ** -0.5)).astype(dq_ref.dtype)
            dk_ref[:, cols] = jnp.where(lo, b0, pltpu.roll(b1, HEAD_DIM, 1)).astype(dk_ref.dtype)
            for hh, (a, b) in enumerate(((a0, b0), (a1, b1))):
                dcum = dcum + jnp.where(lane == 2 * p + hh, a[:, AUG_C:AUG_C + 1] - b[:, AUG_ONE:AUG_ONE + 1], 0.0)
        dcum_ref[...] = dcum

    heads = pl.BlockSpec((N_HEADS, tr, LANES), lambda i: (0, i, 0))
    full = pl.BlockSpec((tr, D_MODEL), lambda i: (i, 0))
    return pl.pallas_call(
        body, name="attn_post", grid=(s_len // tr,),
        in_specs=[heads, heads], out_specs=[full, full, pl.BlockSpec((tr, LANES), lambda i: (i, 0))],
        out_shape=[jax.ShapeDtypeStruct((s_len, D_MODEL), BF16), jax.ShapeDtypeStruct((s_len, D_MODEL), BF16),
                   jax.ShapeDtypeStruct((s_len, LANES), F32)],
        compiler_params=_params(("parallel",)),
    )(dqa, dka)


def _ln_stats(r):
    mu = _rowmean(r)
    xc = r - mu
    rstd = lax.rsqrt(_rowmean(xc * xc) + LN_EPS)
    return xc * rstd, rstd


def _ln_bwd(dxh, xh, rstd):
    return rstd * (dxh - _rowmean(dxh) - xh * _rowmean(dxh * xh))


def _rms_bwd(dgn, g, r):
    return r * dgn - (r * r * r) * g * _rowmean(dgn * g)


def _to_aligned(w):
    pad = jnp.zeros((w.shape[0], AL_COLS - IN_COLS), w.dtype)
    return jnp.concatenate([w[:, :2560], w[:, 2576:5648], w[:, 2560:2576], w[:, 5648:5664], pad], axis=1)


def _from_aligned(g):
    return jnp.concatenate([g[:, :2560], g[:, AL_DTF:AL_DTF + 16], g[:, 2560:AL_DTF], g[:, AL_DTF + 16:AL_DTF + 32]], axis=1)


def _lanes(v, at=0):
    return jnp.pad(v, ((0, 0), (at, LANES - at - v.shape[1])))


def _local_step(x, tgt, mod, w_al, halves, sp):
    d = D_MODEL
    sh1, sc1, g1, sh2, sc2, g2 = [mod[:, i * d:(i + 1) * d] for i in range(6)]
    dt_bias_l, a_log_l, f_bias_l = _lanes(sp["dt_bias"]), _lanes(sp["a_log"]), _lanes(sp["f_bias"], F_LANE)
    d_exp = jnp.repeat(sp["d_skip"], HEAD_DIM, axis=1)
    z_slab = lambda a: (a, d, AL_Z // d)

    (h1,), _ = _rowwise("mod1", lambda x, sc, sh: ([x * (1.0 + sc) + sh], []), [x], [sc1, sh1], [(d, BF16)], [])
    proj = _matmul("proj", h1, w_al, tn=1152)
    xc_all = _conv_fwd(proj, sp["conv_w"], sp["conv_b"])
    y_ssd, prevs = _ssd_fwd(xc_all, proj, dt_bias_l, a_log_l, d_exp)

    def gated_norm(y, z, w):
        g = y * _silu(z)
        return [g * lax.rsqrt(_rowmean(g * g) + RMS_EPS) * w], []

    (y_ssm,), _ = _rowwise("ssm_norm", gated_norm, [y_ssd, z_slab(proj)], [sp["ssm_norm_w"]], [(d, BF16)], [])
    cum = _fox_cum(proj, f_bias_l)
    qa, ka, vb = _attn_prep(proj, cum)
    o, lse, (g_out, g_fi, g_fo) = _attn_fwd(qa, ka, vb, halves)
    w_out = g_out.reshape(2 * d, d)
    w_fi = g_fi.transpose(1, 0, 2).reshape(d, D_FF)
    w_fo = g_fo.reshape(D_FF, d)
    (y_att,), _ = _rowwise("attn_norm", lambda o, w: ([o * lax.rsqrt(_rowmean(o * o) + RMS_EPS) * w], []),
                           [o], [sp["attn_norm_w"]], [(d, BF16)], [])
    y_mix = jnp.concatenate([y_ssm, y_att], axis=1)
    def ln1_fwd(y, x, g1, sc2, sh2, lg, lb):
        r1 = ALPHA * x + (1.0 + g1) * y
        xh, _ = _ln_stats(r1)
        x1 = xh * lg + lb
        return [y, r1, x1 * (1.0 + sc2) + sh2], []

    y, r1, h2 = _matmul("out_proj", y_mix, w_out, tm=512, tk=2048,
                        epi=(ln1_fwd, [x], [g1, sc2, sh2, sp["ln1_g"], sp["ln1_b"]], [F32, F32, BF16], []))
    u, act = _matmul("ff_in", h2, w_fi, epi=(lambda u: ([u, jnp.square(jnp.maximum(u, 0.0))], []), [], [], [F32, BF16], []))

    def head(ff, r1, tgt, g2, l1g, l1b, l2g, l2b):
        xh1, _ = _ln_stats(r1)
        x1 = xh1 * l1g + l1b
        xh2, rstd2 = _ln_stats(ALPHA * x1 + (1.0 + g2) * ff)
        err = xh2 * l2g + l2b - tgt
        loss = 0.5 * jnp.sum(_rowmean(err * err))
        dx2 = err * (1.0 / d)
        dr2 = _ln_bwd(dx2 * l2g, xh2, rstd2)
        return ([dr2, (1.0 + g2) * dr2],
                [_colsum(dx2 * xh2), _colsum(dx2), _colsum(dr2 * ff), jnp.full((1, LANES), loss, F32)])

    dr2, dff, d_ln2_g, d_ln2_b, d_g2, loss = _matmul(
        "ff_out", act, w_fo, tm=512, tk=2048,
        epi=(head, [r1, tgt], [g2, sp["ln1_g"], sp["ln1_b"], sp["ln2_g"], sp["ln2_b"]], [F32, BF16], [d, d, d, LANES]))
    du = _matmul("d_act", dff, w_fo, dims=NT, epi=(lambda da, u: ([da * (2.0 * jnp.maximum(u, 0.0))], []), [u], [], [BF16], []))
    dw_fo = _matmul("dw_ff_out", act, dff, dims=TN, out_dtype=BF16, by_chip="rows")
    dw_fi = _matmul("dw_ff_in", h2, du, dims=TN, out_dtype=BF16, by_chip="cols")

    def ln1_bwd(dh2, r1, dr2, y, sc2, g1, lg, lb):
        xh, rstd = _ln_stats(r1)
        x1 = xh * lg + lb
        dx1 = ALPHA * dr2 + dh2 * (1.0 + sc2)
        dr1 = _ln_bwd(dx1 * lg, xh, rstd)
        return ([dr1, (1.0 + g1) * dr1],
                [_colsum(dh2 * x1), _colsum(dh2), _colsum(dx1 * xh), _colsum(dx1), _colsum(dr1 * y)])

    dr1, dy, d_sc2, d_sh2, d_ln1_g, d_ln1_b, d_g1 = _matmul(
        "dh2", du, w_fi, dims=NT, tm=512, tk=2048,
        epi=(ln1_bwd, [r1, dr2, y], [sc2, g1, sp["ln1_g"], sp["ln1_b"]], [F32, BF16], [d] * 5))
    dymix = _matmul("dy_mix", dy, w_out, dims=NT)
    dw_out = _matmul("dw_out", y_mix, dy, dims=TN, out_dtype=BF16, by_chip="rows")

    def attn_norm_bwd(o, dyo, w):
        r = lax.rsqrt(_rowmean(o * o) + RMS_EPS)
        return [_rms_bwd(dyo * w, o, r)], [_colsum(dyo * o * r)]

    (do,), (d_attn_w,) = _rowwise("attn_norm_bwd", attn_norm_bwd, [o, (dymix, d, 1)], [sp["attn_norm_w"]], [(d, F32)], [d])

    def gated_norm_bwd(y, z, dyo, w):
        sg = _sigmoid(z)
        sz = z * sg
        g = y * sz
        r = lax.rsqrt(_rowmean(g * g) + RMS_EPS)
        dg = _rms_bwd(dyo * w, g, r)
        return [dg * sz, dg * y * (sg * (1.0 + z * (1.0 - sg)))], [_colsum(dyo * g * r)]

    (dy_ssd, dz), (d_ssm_w,) = _rowwise("ssm_norm_bwd", gated_norm_bwd, [y_ssd, z_slab(proj), (dymix, d, 0)],
                                        [sp["ssm_norm_w"]], [(d, F32), (d, BF16)], [d])
    dqa, dka, dv, landed = _attn_bwd(qa, ka, vb, o, lse, do, [dw_out, dw_fi, dw_fo])
    dq, dk, dcum = _attn_post(dqa, dka)
    dxc, ddt_tile, d_alog_l, d_dexp, d_dtb_l = _ssd_bwd(xc_all, proj, dt_bias_l, a_log_l, d_exp, prevs, dy_ssd)
    dtf, d_fb_l = _fox_cum_bwd(dcum, proj, f_bias_l, ddt_tile)
    dpre, d_conv_w, d_conv_b = _conv_bwd_pre(proj, sp["conv_w"], sp["conv_b"], dxc)
    dxbc = _conv_bwd_in(dpre, sp["conv_w"])
    dproj = jnp.concatenate([dz, dxbc, dq, dk, dv, dtf], axis=1)
    dw_al = _matmul("dw_in", h1, dproj, dims=TN, tn=1152, out_dtype=BF16)
    part_in = _from_aligned(dw_al).reshape(d, N_CHIPS, IN_COLS // N_CHIPS).transpose(1, 0, 2)
    def last(dh1, x, dr1, sc1):
        return [ALPHA * dr1 + dh1 * (1.0 + sc1)], [_colsum(dh1 * x), _colsum(dh1)]

    dx, d_sc1, d_sh1, landed_in = _matmul("dh1", dproj, w_al, dims=NT, tm=512, tk=1152, carry=[part_in],
                                          epi=(last, [x, dr1], [sc1], [F32], [d, d]))

    small = {
        "mod": jnp.concatenate([d_sh1, d_sc1, d_g1, d_sh2, d_sc2, d_g2], axis=1),
        "conv_w": d_conv_w, "conv_b": d_conv_b,
        "dt_bias": d_dtb_l[:, :N_HEADS], "a_log": d_alog_l[:, :N_HEADS],
        "d_skip": jnp.sum(d_dexp.reshape(N_HEADS, HEAD_DIM), axis=1)[None, :],
        "ssm_norm_w": d_ssm_w, "f_bias": d_fb_l[:, F_LANE:F_LANE + N_HEADS], "attn_norm_w": d_attn_w,
        "ln1_g": d_ln1_g, "ln1_b": d_ln1_b, "ln2_g": d_ln2_g, "ln2_b": d_ln2_b, "loss": loss,
    }
    return dx, [landed_in, *landed], small


N_DEV = 8
N_CHIPS = 4
ANY = pl.BlockSpec(memory_space=pl.ANY)
VMEM_SPEC = pl.BlockSpec(memory_space=pltpu.VMEM)


def _place():
    x, y, c = lax.axis_index("x"), lax.axis_index("y"), lax.axis_index("c")
    return x, y, c


def _other_chips(x, y):
    return [(1 - x, y, 2 * (1 - x) + y), (x, 1 - y, 2 * x + 1 - y), (1 - x, 1 - y, 2 * (1 - x) + 1 - y)]


def _allgather_small(name, v):
    r, cdim = v.shape

    def body(v_ref, out_ref, send_sems, recv_sems, local_sem):
        x, y, c = _place()
        me = 4 * x + 2 * y + c
        mine = pltpu.make_async_copy(v_ref, out_ref.at[me], local_sem)
        mine.start()
        peers = []
        for rel in range(1, N_DEV):
            px = 1 - x if rel & 4 else x
            py = 1 - y if rel & 2 else y
            pc = 1 - c if rel & 1 else c
            peers.append((px, py, pc))

        def copy(rel, slot, to):
            return pltpu.make_async_remote_copy(src_ref=v_ref, dst_ref=out_ref.at[slot], send_sem=send_sems.at[rel],
                                                recv_sem=recv_sems.at[rel], device_id=to, device_id_type=MESH)

        sends = [copy(rel, me, peer) for rel, peer in enumerate(peers)]
        for cp in sends:
            cp.start()
        for rel, (px, py, pc) in enumerate(peers):
            copy(rel, 4 * px + 2 * py + pc, (x, y, c)).wait_recv()
        for cp in sends:
            cp.wait_send()
        mine.wait()

    return pl.pallas_call(
        body, name=name, out_shape=jax.ShapeDtypeStruct((N_DEV, r, cdim), v.dtype),
        in_specs=[VMEM_SPEC], out_specs=VMEM_SPEC,
        scratch_shapes=[pltpu.SemaphoreType.DMA((N_DEV - 1,)), pltpu.SemaphoreType.DMA((N_DEV - 1,)), pltpu.SemaphoreType.DMA],
    )(v)


def _gather_shards(shard):
    r, cdim = shard.shape
    rh = r // 2

    def body(in_ref, out_ref, stage, send_sems, recv_sems, local_sems):
        x, y, c = _place()
        k_me = 2 * x + y
        me, sibling = (x, y, c), (x, y, 1 - c)
        chips = _other_chips(x, y)

        def copy(idx, k, half, to, src=None):
            rows = out_ref.at[k, pl.ds(pl.multiple_of(half * rh, rh), rh), :]
            return pltpu.make_async_remote_copy(src_ref=rows if src is None else src, dst_ref=rows, send_sem=send_sems.at[idx],
                                                recv_sem=recv_sems.at[idx], device_id=to, device_id_type=MESH)

        mine = in_ref.at[pl.ds(pl.multiple_of(c * rh, rh), rh), :]
        sends = [copy(j, k_me, c, (cx, cy, c), src=mine) for j, (cx, cy, _) in enumerate(chips)]
        for cp in sends:
            cp.start()
        load = pltpu.make_async_copy(in_ref, stage, local_sems.at[0])
        load.start()
        load.wait()
        store = pltpu.make_async_copy(stage, out_ref.at[k_me], local_sems.at[1])
        store.start()
        for j, (_, _, kj) in enumerate(chips):
            copy(j, kj, c, me).wait_recv()
            fwd = copy(3 + j, kj, c, sibling)
            fwd.start()
            sends.append(fwd)
        for j, (_, _, kj) in enumerate(chips):
            copy(3 + j, kj, 1 - c, me).wait_recv()
        for cp in sends:
            cp.wait_send()
        store.wait()

    return pl.pallas_call(
        body, name="gather_w_in", out_shape=jax.ShapeDtypeStruct((N_CHIPS, r, cdim), shard.dtype),
        in_specs=[ANY], out_specs=ANY,
        scratch_shapes=[pltpu.VMEM((r, cdim), shard.dtype), pltpu.SemaphoreType.DMA((6,)), pltpu.SemaphoreType.DMA((6,)),
                        pltpu.SemaphoreType.DMA((2,))],
        compiler_params=_params(),
    )(shard)


def _peers(x, y, c):
    return [((1 - x) if rel & 4 else x, (1 - y) if rel & 2 else y, (1 - c) if rel & 1 else c) for rel in range(1, N_DEV)]


def _exchange_sems(n):
    return [pltpu.SemaphoreType.DMA((n, N_DEV - 1)), pltpu.SemaphoreType.DMA((n, N_DEV - 1)), pltpu.SemaphoreType.DMA((n,))]


def _gather_plan(ins, outs, send_sems, recv_sems, local_sems):
    x, y, c = _place()
    k_me = 2 * x + y
    peers = [(rel, p) for rel, p in enumerate(_peers(x, y, c)) if (rel + 1) & 6]

    def copy(w, rel, k, half, to, src=None):
        rh = ins[w].shape[0] // 2
        rows = outs[w].at[k, pl.ds(pl.multiple_of(half * rh, rh), rh), :]
        return pltpu.make_async_remote_copy(src_ref=rows if src is None else src, dst_ref=rows, send_sem=send_sems.at[w, rel],
                                            recv_sem=recv_sems.at[w, rel], device_id=to, device_id_type=MESH)

    def mine(w):
        rh = ins[w].shape[0] // 2
        return ins[w].at[pl.ds(pl.multiple_of(c * rh, rh), rh), :]

    n = len(ins)
    local = [pltpu.make_async_copy(ins[w], outs[w].at[k_me], local_sems.at[w]) for w in range(n)]
    sends = [copy(w, rel, k_me, c, peer, src=mine(w)) for w in range(n) for rel, peer in peers]

    def start():
        for cp in local + sends:
            cp.start()

    def finish():
        for w in range(n):
            for rel, (px, py, pc) in peers:
                copy(w, rel, 2 * px + py, pc, (x, y, c)).wait_recv()
        for cp in sends:
            cp.wait_send()
        for cp in local:
            cp.wait()

    return start, finish


def _reduce_plan(ins, outs, send_sems, recv_sems, local_sems):
    x, y, c = _place()
    me = 4 * x + 2 * y + c
    peers = _peers(x, y, c)

    def block(w, k, half):
        rh = ins[w].shape[1] // 2
        return ins[w].at[k, pl.ds(pl.multiple_of(half * rh, rh), rh), :]

    def copy(w, rel, src, slot, to):
        return pltpu.make_async_remote_copy(src_ref=src, dst_ref=outs[w].at[slot], send_sem=send_sems.at[w, rel],
                                            recv_sem=recv_sems.at[w, rel], device_id=to, device_id_type=MESH)

    n = len(ins)
    local = [pltpu.make_async_copy(block(w, 2 * x + y, c), outs[w].at[me], local_sems.at[w]) for w in range(n)]
    sends = [copy(w, rel, block(w, 2 * px + py, pc), me, (px, py, pc)) for w in range(n) for rel, (px, py, pc) in enumerate(peers)]

    def start():
        for cp in local + sends:
            cp.start()

    def finish():
        for w in range(n):
            for rel, (px, py, pc) in enumerate(peers):
                copy(w, rel, block(w, 2 * x + y, c), 4 * px + 2 * py + pc, (x, y, c)).wait_recv()
        for cp in sends:
            cp.wait_send()
        for cp in local:
            cp.wait()

    return start, finish


def _sum_blocks(name, parts):
    k, r, cdim = parts.shape
    tr = min(256, r)

    def body(p_ref, o_ref):
        acc = p_ref[0].astype(F32)
        for i in range(1, k):
            acc = acc + p_ref[i].astype(F32)
        o_ref[...] = acc

    return pl.pallas_call(
        body, name=name, grid=(r // tr,),
        in_specs=[pl.BlockSpec((k, tr, cdim), lambda i: (0, i, 0))], out_specs=pl.BlockSpec((tr, cdim), lambda i: (i, 0)),
        out_shape=jax.ShapeDtypeStruct((r, cdim), F32), compiler_params=_params(("parallel",)),
    )(parts)


def _pair_swap(halves):
    n = len(halves)

    def body(*refs):
        ins, outs = refs[:n], refs[n:2 * n]
        send_sems, recv_sems = refs[2 * n:]
        x, y, c = _place()
        cps = [pltpu.make_async_remote_copy(src_ref=ins[w], dst_ref=outs[w], send_sem=send_sems.at[w], recv_sem=recv_sems.at[w],
                                            device_id=(x, y, 1 - c), device_id_type=MESH) for w in range(n)]
        for cp in cps:
            cp.start()
        for cp in cps:
            cp.wait_recv()
        for cp in cps:
            cp.wait_send()

    return pl.pallas_call(
        body, name="pair_swap", out_shape=[jax.ShapeDtypeStruct(h.shape, h.dtype) for h in halves],
        in_specs=[ANY] * n, out_specs=[ANY] * n,
        scratch_shapes=[pltpu.SemaphoreType.DMA((n,)), pltpu.SemaphoreType.DMA((n,))],
    )(*halves)


ADA_SHARD = 6 * D_MODEL // N_CHIPS


def _mod_part(c_all, w_shard, b_shard):
    tn = 512

    def body(c_ref, w_ref, b_ref, o_ref):
        o_ref[...] = _dot(_silu(c_ref[...]).astype(BF16), w_ref[...].astype(BF16)) + b_ref[...]

    return pl.pallas_call(
        body, name="mod_part", grid=(ADA_SHARD // tn,),
        in_specs=[pl.BlockSpec((N_DEV, D_MODEL), lambda j: (0, 0)), pl.BlockSpec((D_MODEL, tn), lambda j: (0, j)),
                  pl.BlockSpec((1, tn), lambda j: (0, j))],
        out_specs=pl.BlockSpec((N_DEV, tn), lambda j: (0, j)),
        out_shape=jax.ShapeDtypeStruct((N_DEV, ADA_SHARD), F32), compiler_params=_params(("parallel",)),
    )(c_all, w_shard, b_shard)


def _w_ada_grad(c_all_t, dmod_shard):
    tm = 256

    def body(ct_ref, dm_ref, o_ref):
        act = _silu(ct_ref[...])
        acc = act[:, 0:1] * dm_ref[0:1, :]
        for dev in range(1, N_DEV):
            acc = acc + act[:, dev:dev + 1] * dm_ref[dev:dev + 1, :]
        o_ref[...] = acc

    return pl.pallas_call(
        body, name="w_ada_grad", grid=(D_MODEL // tm,),
        in_specs=[pl.BlockSpec((tm, N_DEV), lambda i: (i, 0)), pl.BlockSpec((N_DEV, ADA_SHARD), lambda i: (0, 0))],
        out_specs=pl.BlockSpec((tm, ADA_SHARD), lambda i: (i, 0)),
        out_shape=jax.ShapeDtypeStruct((D_MODEL, ADA_SHARD), F32), compiler_params=_params(("parallel",)),
    )(c_all_t, dmod_shard)


def _adamw_math(w, g, m, v):
    nm = ADAM_B1 * m + (1.0 - ADAM_B1) * g
    nv = ADAM_B2 * v + (1.0 - ADAM_B2) * jnp.square(g)
    m_hat = nm / (1.0 - ADAM_B1 ** ADAM_STEP)
    v_hat = nv / (1.0 - ADAM_B2 ** ADAM_STEP)
    return -ADAM_LR * (m_hat / (jnp.sqrt(v_hat) + ADAM_EPS) + ADAM_WD * w), nm, nv


def _adamw(name, w, g, m, v):
    r, cdim = w.shape
    tr = 256 if r % 256 == 0 else r

    def body(w_ref, g_ref, m_ref, v_ref, d_ref, nm_ref, nv_ref):
        d_ref[...], nm_ref[...], nv_ref[...] = _adamw_math(w_ref[...], g_ref[...], m_ref[...], v_ref[...])

    blk = pl.BlockSpec((tr, cdim), lambda i: (i, 0))
    return pl.pallas_call(
        body, name=name, grid=(r // tr,), in_specs=[blk] * 4, out_specs=[blk] * 3,
        out_shape=[jax.ShapeDtypeStruct((r, cdim), F32)] * 3, compiler_params=_params(("parallel",)),
    )(w, g, m, v)


def _adamw_pair(name, w, mine, other, m, v, c):
    r, cdim = w.shape
    rh = r // 2
    tr = min(256, rh)
    per = rh // tr

    def body(c_ref, w_ref, a_ref, b_ref, m_ref, v_ref, g_ref, d_ref, nm_ref, nv_ref):
        is_mine = (pl.program_id(0) // per) == c_ref[0]
        g = jnp.where(is_mine, a_ref[...], b_ref[...])
        g_ref[...] = g
        d_ref[...], nm_ref[...], nv_ref[...] = _adamw_math(w_ref[...], g, m_ref[...], v_ref[...])

    blk = pl.BlockSpec((tr, cdim), lambda i, c_ref: (i, 0))
    half = pl.BlockSpec((tr, cdim), lambda i, c_ref: (i % per, 0))
    return pl.pallas_call(
        body, name=name,
        grid_spec=pltpu.PrefetchScalarGridSpec(num_scalar_prefetch=1, grid=(r // tr,), in_specs=[blk, half, half, blk, blk], out_specs=[blk] * 4),
        out_shape=[jax.ShapeDtypeStruct((r, cdim), F32)] * 4, compiler_params=_params(("parallel",)),
    )(jnp.reshape(c, (1,)).astype(jnp.int32), w, mine, other, m, v)


SMALL = ["b_ada", "conv_b", "dt_bias", "a_log", "d_skip", "ssm_norm_w", "f_bias", "attn_norm_w", "ln1_g", "ln1_b", "ln2_g", "ln2_b"]


def _pack(vs):
    pieces = []
    for v in vs:
        pieces.append(v)
        if v.shape[1] % LANES:
            pieces.append(jnp.zeros((1, -v.shape[1] % LANES), v.dtype))
    return jnp.concatenate(pieces, axis=1)


def _adamw_small(total, offs, ws, ms, vs):
    n = len(ws)

    def body(*refs):
        t_ref, outs = refs[0], refs[1 + 3 * n:]
        for i in range(n):
            g = t_ref[:, offs[i]:offs[i] + ws[i].shape[1]]
            dl, nm, nv = _adamw_math(refs[1 + i][...], g, refs[1 + n + i][...], refs[1 + 2 * n + i][...])
            outs[4 * i][...], outs[4 * i + 1][...], outs[4 * i + 2][...], outs[4 * i + 3][...] = g, dl, nm, nv

    res = pl.pallas_call(
        body, name="adamw_small", in_specs=[VMEM_SPEC] * (1 + 3 * n), out_specs=[VMEM_SPEC] * (4 * n),
        out_shape=[jax.ShapeDtypeStruct(w.shape, F32) for w in ws for _ in range(4)],
    )(total, *ws, *ms, *vs)
    return [res[4 * i:4 * i + 4] for i in range(n)]


def kernel(x, c, w_ada, b_ada, w_in, conv_w, conv_b, dt_bias, a_log, d_skip, ssm_norm_w, f_bias, attn_norm_w, w_out, ln1_g, ln1_b, w_ff_in, w_ff_out, ln2_g, ln2_b, loss_target, m_w_ada, m_b_ada, m_w_in, m_conv_w, m_conv_b, m_dt_bias, m_a_log, m_d_skip, m_ssm_norm_w, m_f_bias, m_attn_norm_w, m_w_out, m_ln1_g, m_ln1_b, m_w_ff_in, m_w_ff_out, m_ln2_g, m_ln2_b, v_w_ada, v_b_ada, v_w_in, v_conv_w, v_conv_b, v_dt_bias, v_a_log, v_d_skip, v_ssm_norm_w, v_f_bias, v_attn_norm_w, v_w_out, v_ln1_g, v_ln1_b, v_w_ff_in, v_w_ff_out, v_ln2_g, v_ln2_b):
    a = dict(b_ada=b_ada, conv_b=conv_b, dt_bias=dt_bias, a_log=a_log, d_skip=d_skip, ssm_norm_w=ssm_norm_w, f_bias=f_bias,
             attn_norm_w=attn_norm_w, ln1_g=ln1_g, ln1_b=ln1_b, ln2_g=ln2_g, ln2_b=ln2_b)
    ms = dict(b_ada=m_b_ada, conv_b=m_conv_b, dt_bias=m_dt_bias, a_log=m_a_log, d_skip=m_d_skip, ssm_norm_w=m_ssm_norm_w,
              f_bias=m_f_bias, attn_norm_w=m_attn_norm_w, ln1_g=m_ln1_g, ln1_b=m_ln1_b, ln2_g=m_ln2_g, ln2_b=m_ln2_b)
    vs = dict(b_ada=v_b_ada, conv_b=v_conv_b, dt_bias=v_dt_bias, a_log=v_a_log, d_skip=v_d_skip, ssm_norm_w=v_ssm_norm_w,
              f_bias=v_f_bias, attn_norm_w=v_attn_norm_w, ln1_g=v_ln1_g, ln1_b=v_ln1_b, ln2_g=v_ln2_g, ln2_b=v_ln2_b)
    xi, yi, ci = _place()
    chip = 2 * xi + yi
    me = 4 * xi + 2 * yi + ci
    d = D_MODEL
    conv_shard = CONV_DIM // N_CHIPS

    first = _allgather_small("gather_c", jnp.concatenate([c, conv_w[0].reshape(1, CONV_W * conv_shard)], axis=1))[:, 0]
    c_all = first[:, :d]
    conv_w_full = first[::2, d:].reshape(N_CHIPS, CONV_W, conv_shard).transpose(1, 0, 2).reshape(CONV_W, CONV_DIM)
    b_shard = lax.dynamic_slice_in_dim(b_ada, chip * ADA_SHARD, ADA_SHARD, axis=1)
    parts = _allgather_small("gather_mod", _mod_part(c_all, w_ada[0], b_shard))
    mod = lax.dynamic_index_in_dim(parts[::2], me, axis=1, keepdims=False).reshape(1, 6 * d)

    g_in = _gather_shards(w_in[0].astype(BF16))
    w_al = _to_aligned(g_in.transpose(1, 0, 2).reshape(d, IN_COLS))

    sp = {n: a[n] for n in SMALL[1:]}
    sp["conv_w"] = conv_w_full
    shards = [w_out[0].astype(BF16), w_ff_in[0].astype(BF16), w_ff_out[0].astype(BF16)]
    dx, landed, small = _local_step(x[0], loss_target[0], mod, w_al, shards, sp)

    names = ["mod"] + SMALL[1:]
    vec = _pack([small[n] for n in names] + [small["conv_w"].reshape(1, CONV_W * CONV_DIM), small["loss"]])
    every = _allgather_small("gather_small", vec)
    total = _sum_blocks("sum_small", jnp.broadcast_to(every, (N_DEV, SUBLANES, vec.shape[1])))[:1]
    widths = [6 * d] + [a[n].shape[1] for n in SMALL[1:]]
    offs = [0]
    for w in widths:
        offs.append(offs[-1] + w + (-w % LANES))
    g_conv_w_full = total[:, offs[-1]:offs[-1] + CONV_W * CONV_DIM].reshape(CONV_W, CONV_DIM)
    loss = total[0, offs[-1] + CONV_W * CONV_DIM]
    dmod_shard = lax.dynamic_slice_in_dim(every[:, 0, :6 * d], chip * ADA_SHARD, ADA_SHARD, axis=1)
    g_w_ada = _w_ada_grad(c_all.T, dmod_shard)
    g_conv_w = lax.dynamic_slice_in_dim(g_conv_w_full, chip * conv_shard, conv_shard, axis=1)

    mine = [_sum_blocks("dev_sum_%d" % i, p) for i, p in enumerate(landed)]
    other = _pair_swap(mine)

    grads, deltas, new_m, new_v = {}, {}, {}, {}
    paired = dict(w_in=(w_in, m_w_in, v_w_in), w_out=(w_out, m_w_out, v_w_out), w_ff_in=(w_ff_in, m_w_ff_in, v_w_ff_in),
                  w_ff_out=(w_ff_out, m_w_ff_out, v_w_ff_out))
    for i, (n, (w, m, v)) in enumerate(paired.items()):
        g, dl, nm, nv = _adamw_pair("adamw_" + n, w[0], mine[i], other[i], m[0], v[0], ci)
        grads[n], deltas[n], new_m[n], new_v[n] = g[None], dl[None], nm[None], nv[None]
    for n, g, (w, m, v) in (("w_ada", g_w_ada, (w_ada, m_w_ada, v_w_ada)), ("conv_w", g_conv_w, (conv_w, m_conv_w, v_conv_w))):
        dl, nm, nv = _adamw("adamw_" + n, w[0], g, m[0], v[0])
        grads[n], deltas[n], new_m[n], new_v[n] = g[None], dl[None], nm[None], nv[None]
    for n, res in zip(SMALL, _adamw_small(total, offs, [a[n] for n in SMALL], [ms[n] for n in SMALL], [vs[n] for n in SMALL])):
        grads[n], deltas[n], new_m[n], new_v[n] = res

    order = ["w_ada", "b_ada", "w_in", "conv_w", "conv_b", "dt_bias", "a_log", "d_skip", "ssm_norm_w", "f_bias", "attn_norm_w", "w_out",
             "ln1_g", "ln1_b", "w_ff_in", "w_ff_out", "ln2_g", "ln2_b"]
    return (loss, dx[None], *[grads[n] for n in order], *[deltas[n] for n in order], *[new_m[n] for n in order], *[new_v[n] for n in order])
```

```python
import functools

import jax
import jax.numpy as jnp
from jax import lax
from jax.experimental import pallas as pl
from jax.experimental.pallas import tpu as pltpu

F32, BF16 = jnp.float32, jnp.bfloat16

D_MODEL = 1024
N_HEADS = 16
HEAD_DIM = 64
N_PAIRS = N_HEADS // 2
SSM_GROUPS = 2
SSM_STATE = 128
CHUNK = 128
CONV_W = 4
CONV_DIM = 1536
D_FF = 4096
IN_COLS = 5664
ALPHA = 2.0 ** 0.25
LN_EPS = 1e-5
RMS_EPS = 1e-5
LANES = 128
SUBLANES = 8

AL_Z, AL_XS, AL_Q, AL_K, AL_V, AL_B, AL_C, AL_DTF = 0, 1024, 2048, 3072, 4096, 5120, 5376, 5632
AL_COLS = 5760
F_LANE = 16

ADAM_LR, ADAM_B1, ADAM_B2, ADAM_EPS, ADAM_WD, ADAM_STEP = 0.001, 0.9, 0.999, 1e-08, 0.01, 10

VMEM_LIMIT = 56 * 1024 * 1024
MESH = pl.DeviceIdType.MESH


def _params(sem=None):
    return pltpu.CompilerParams(dimension_semantics=sem, vmem_limit_bytes=VMEM_LIMIT)


def _sigmoid(x):
    return 1.0 / (1.0 + jnp.exp(-x))


def _silu(x):
    return x * _sigmoid(x)


def _softplus(x):
    return jnp.maximum(x, 0.0) + jnp.log(1.0 + jnp.exp(-jnp.abs(x)))


def _split3(a):
    hi = a.astype(BF16)
    r = a - hi.astype(F32)
    mid = r.astype(BF16)
    lo = (r - mid.astype(F32)).astype(BF16)
    return hi, mid, lo


def _dot(a, b, dims=((1,), (0,))):
    return lax.dot_general(a, b, (dims, ((), ())), preferred_element_type=F32)


NN, NT, TN = ((1,), (0,)), ((1,), (1,)), ((0,), (0,))


def _dot3(t, a):
    hi, mid, lo = _split3(a)
    return _dot(t, hi) + _dot(t, mid) + _dot(t, lo)


def _matmul(name, a, b, *, dims=NN, out_dtype=F32, tm=1024, tn=1024, tk=1024, by_chip=None, epi=None, carry=()):
    if dims == NN:
        (m, k), n = a.shape, b.shape[1]
    elif dims == NT:
        (m, k), n = a.shape, b.shape[0]
    else:
        (k, m), n = a.shape, b.shape[1]
    if by_chip == "rows":
        tm = min(tm, m // 4)
    if by_chip == "cols":
        tn = min(tn, n // 4)
    tm, tn, tk = min(tm, m), min(tn, n), min(tk, k)
    assert m % tm == 0 and n % tn == 0 and k % tk == 0, (name, m, n, k, tm, tn, tk)
    nk = k // tk
    if by_chip == "rows":
        per = m // 4 // tm
        out_spec = pl.BlockSpec((None, tm, tn), lambda i, j, l: (i // per, i % per, j))
        out_shape = jax.ShapeDtypeStruct((4, m // 4, n), out_dtype)
    elif by_chip == "cols":
        per = n // 4 // tn
        out_spec = pl.BlockSpec((None, tm, tn), lambda i, j, l: (j // per, i, j % per))
        out_shape = jax.ShapeDtypeStruct((4, m, n // 4), out_dtype)
    else:
        out_spec = pl.BlockSpec((tm, tn), lambda i, j, l: (i, j))
        out_shape = jax.ShapeDtypeStruct((m, n), out_dtype)
    a_spec = pl.BlockSpec((tk, tm), lambda i, j, l: (l, i)) if dims == TN else pl.BlockSpec((tm, tk), lambda i, j, l: (i, l))
    b_spec = pl.BlockSpec((tn, tk), lambda i, j, l: (j, l)) if dims == NT else pl.BlockSpec((tk, tn), lambda i, j, l: (l, j))

    tile = pl.BlockSpec((tm, tn), lambda i, j, l: (i, j))
    in_specs, args, out_specs, out_shape = [a_spec, b_spec], [a, b], [out_spec], [out_shape]
    fn, n_tiles, n_sums = None, 1, 0
    if epi is not None:
        fn, fulls, vecs, outs, sums = epi
        assert by_chip is None and (not sums or n == tn), name
        in_specs = in_specs + [tile] * len(fulls) + [pl.BlockSpec((1, tn), lambda i, j, l: (0, j))] * len(vecs)
        args = args + list(fulls) + list(vecs)
        out_specs = [tile] * len(outs) + [pl.BlockSpec((1, w), lambda i, j, l: (0, 0)) for w in sums]
        out_shape = [jax.ShapeDtypeStruct((m, n), dt) for dt in outs] + [jax.ShapeDtypeStruct((1, w), F32) for w in sums]
        n_tiles, n_sums = len(outs), len(sums)
    n_in, n_out, n_c = len(args), len(out_specs), len(carry)
    scratch = [pltpu.VMEM((tm, tn) if nk > 1 else (SUBLANES, LANES), F32)]
    if n_c:
        in_specs, args = in_specs + [ANY] * n_c, args + list(carry)
        out_specs = out_specs + [ANY] * n_c
        out_shape = out_shape + [jax.ShapeDtypeStruct((N_DEV, g.shape[1] // 2, g.shape[2]), g.dtype) for g in carry]
        scratch = scratch + _exchange_sems(n_c)
    gm, gn = m // tm, n // tn

    def body(*refs):
        a_ref, b_ref = refs[:2]
        ins, outs = refs[2:n_in], refs[n_in + n_c:n_in + n_c + n_out]
        acc_ref = refs[n_in + 2 * n_c + n_out]
        i, j, l = pl.program_id(0), pl.program_id(1), pl.program_id(2)
        if n_c:
            start, wait = _reduce_plan(refs[n_in:n_in + n_c], refs[n_in + n_c + n_out:n_in + 2 * n_c + n_out], *refs[n_in + 2 * n_c + n_out + 1:])
            pl.when((i == 0) & (j == 0) & (l == 0))(start)
        part = _dot(a_ref[...].astype(BF16), b_ref[...].astype(BF16), dims)

        def finish(res):
            if fn is None:
                outs[0][...] = res.astype(outs[0].dtype)
                return
            tiles, colsums = fn(res, *[r[...] for r in ins])
            for r, val in zip(outs[:n_tiles], tiles):
                r[...] = val.astype(r.dtype)
            if n_sums:
                @pl.when(i == 0)
                def _():
                    for r in outs[n_tiles:]:
                        r[...] = jnp.zeros_like(r)
                for r, val in zip(outs[n_tiles:], colsums):
                    r[...] += val

        if nk == 1:
            finish(part)
        else:
            @pl.when(l == 0)
            def _():
                acc_ref[...] = part

            @pl.when((l > 0) & (l < nk - 1))
            def _():
                acc_ref[...] += part

            @pl.when(l == nk - 1)
            def _():
                finish(acc_ref[...] + part)

        if n_c:
            pl.when((i == gm - 1) & (j == gn - 1) & (l == nk - 1))(wait)

    res = pl.pallas_call(
        body, name=name, grid=(gm, gn, nk),
        in_specs=in_specs, out_specs=out_specs, out_shape=out_shape, scratch_shapes=scratch,
        compiler_params=_params(("arbitrary",) * 3 if n_c or n_sums else ("parallel", "parallel", "arbitrary")),
    )(*args)
    return res[0] if len(res) == 1 else res


def _rowwise(name, fn, fulls, vecs, out_fulls, out_vecs, tr=256):
    fulls = [f if isinstance(f, tuple) else (f, f.shape[1], 0) for f in fulls]
    s = fulls[0][0].shape[0]
    tr = min(tr, s)
    out_fulls = [o if len(o) == 3 else (*o, (o[0], 0, None)) for o in out_fulls]
    into = [(k, slab[2]) for k, (_, _, slab) in enumerate(out_fulls) if slab[2] is not None]
    nf, nv, nof, nov = len(fulls), len(vecs), len(out_fulls), len(out_vecs)
    in_specs = [pl.BlockSpec((tr, w), functools.partial(lambda i, cb: (i, cb), cb=cb)) for (_, w, cb) in fulls]
    in_specs += [pl.BlockSpec(v.shape, lambda i: (0, 0)) for v in vecs] + [ANY] * len(into)
    out_shape = [jax.ShapeDtypeStruct((s, slab[0]), dt) for (_, dt, slab) in out_fulls] + [jax.ShapeDtypeStruct((1, w), F32) for w in out_vecs]
    out_specs = [pl.BlockSpec((tr, w), functools.partial(lambda i, cb: (i, cb), cb=slab[1])) for (w, _, slab) in out_fulls]
    out_specs += [pl.BlockSpec((1, w), lambda i: (0, 0)) for w in out_vecs]

    def body(*refs):
        outs = refs[nf + nv + len(into):]
        of, ov = fn(*[r[...] for r in refs[:nf + nv]])
        for r, val in zip(outs[:nof], of):
            r[...] = val.astype(r.dtype)
        if nov:
            @pl.when(pl.program_id(0) == 0)
            def _():
                for r in outs[nof:]:
                    r[...] = jnp.zeros_like(r)
            for r, val in zip(outs[nof:], ov):
                r[...] += val

    res = pl.pallas_call(
        body, name=name, grid=(s // tr,), in_specs=in_specs, out_specs=out_specs, out_shape=out_shape,
        input_output_aliases={nf + nv + pos: k for pos, (k, _) in enumerate(into)},
        compiler_params=_params(("arbitrary",)),
    )(*[f[0] for f in fulls], *vecs, *[buf for _, buf in into])
    return res[:nof], res[nof:]


def _colsum(x):
    return jnp.sum(x, axis=0, keepdims=True)


def _rowmean(x):
    return jnp.mean(x, axis=-1, keepdims=True)


CONV_CB = 512
CONV_TR = 512


def _shift_down(u, halo, j):
    if j == 0:
        return u
    ru = pltpu.roll(u, j, 0)
    row8 = lax.broadcasted_iota(jnp.int32, halo.shape, 0)
    top = jnp.where(row8 < j, pltpu.roll(halo, j, 0), ru[:SUBLANES])
    return jnp.concatenate([top, ru[SUBLANES:]], axis=0)


def _shift_up(d, halo, j):
    if j == 0:
        return d
    tr = d.shape[0]
    rd = pltpu.roll(d, tr - j, 0)
    row8 = lax.broadcasted_iota(jnp.int32, halo.shape, 0)
    bot = jnp.where(row8 >= SUBLANES - j, pltpu.roll(halo, SUBLANES - j, 0), rd[tr - SUBLANES:])
    return jnp.concatenate([rd[:tr - SUBLANES], bot], axis=0)


def _conv_col(cb):
    return jnp.where(cb < 2, AL_XS // CONV_CB + cb, AL_B // CONV_CB)


def _conv_specs(s, tr):
    per8 = tr // SUBLANES
    blk = pl.BlockSpec((tr, CONV_CB), lambda cb, i: (i, _conv_col(cb)))
    prev = pl.BlockSpec((SUBLANES, CONV_CB), lambda cb, i: (jnp.maximum(i * per8 - 1, 0), _conv_col(cb)))
    return blk, prev


def _conv_pre(u, halo, w_ref, b_ref, first):
    halo = jnp.where(first, 0.0, halo)
    acc = b_ref[...] + w_ref[CONV_W - 1:CONV_W, :] * u
    shifted = [u]
    for j in range(1, CONV_W):
        sh = _shift_down(u, halo, j)
        shifted.append(sh)
        acc = acc + w_ref[CONV_W - 1 - j:CONV_W - j, :] * sh
    return acc, shifted


def _conv_fwd(proj, conv_w, conv_b):
    s = proj.shape[0]
    tr = min(CONV_TR, s)
    blk, prev = _conv_specs(s, tr)

    def body(u_ref, h_ref, w_ref, b_ref, o_ref):
        pre, _ = _conv_pre(u_ref[...], h_ref[...], w_ref, b_ref, pl.program_id(1) == 0)
        o_ref[...] = _silu(pre)

    return pl.pallas_call(
        body, name="conv_fwd", grid=(CONV_DIM // CONV_CB, s // tr),
        in_specs=[blk, prev, pl.BlockSpec((CONV_W, CONV_CB), lambda cb, i: (0, cb)), pl.BlockSpec((1, CONV_CB), lambda cb, i: (0, cb))],
        out_specs=pl.BlockSpec((tr, CONV_CB), lambda cb, i: (i, cb)),
        out_shape=jax.ShapeDtypeStruct((s, CONV_DIM), F32),
        compiler_params=_params(("parallel", "parallel")),
    )(proj, proj, conv_w, conv_b)


def _conv_bwd_pre(proj, conv_w, conv_b, dxc):
    s = proj.shape[0]
    tr = min(CONV_TR, s)
    blk, prev = _conv_specs(s, tr)

    def body(u_ref, h_ref, w_ref, b_ref, d_ref, dpre_ref, dw_ref, db_ref):
        i = pl.program_id(1)
        pre, shifted = _conv_pre(u_ref[...], h_ref[...], w_ref, b_ref, i == 0)
        sg = _sigmoid(pre)
        dpre = d_ref[...] * (sg * (1.0 + pre * (1.0 - sg)))
        dpre_ref[...] = dpre

        @pl.when(i == 0)
        def _():
            dw_ref[...] = jnp.zeros_like(dw_ref)
            db_ref[...] = jnp.zeros_like(db_ref)

        db_ref[...] += _colsum(dpre)
        for j in range(CONV_W):
            dw_ref[CONV_W - 1 - j:CONV_W - j, :] += _colsum(dpre * shifted[j])

    own = pl.BlockSpec((tr, CONV_CB), lambda cb, i: (i, cb))
    wspec = pl.BlockSpec((CONV_W, CONV_CB), lambda cb, i: (0, cb))
    bspec = pl.BlockSpec((1, CONV_CB), lambda cb, i: (0, cb))
    return pl.pallas_call(
        body, name="conv_bwd_pre", grid=(CONV_DIM // CONV_CB, s // tr),
        in_specs=[blk, prev, wspec, bspec, own], out_specs=[own, wspec, bspec],
        out_shape=[jax.ShapeDtypeStruct((s, CONV_DIM), F32), jax.ShapeDtypeStruct((CONV_W, CONV_DIM), F32),
                   jax.ShapeDtypeStruct((1, CONV_DIM), F32)],
        compiler_params=_params(("parallel", "arbitrary")),
    )(proj, proj, conv_w, conv_b, dxc)


def _conv_bwd_in(dpre, conv_w, dproj):
    s = dpre.shape[0]
    tr = min(CONV_TR, s)
    per8 = tr // SUBLANES
    last8 = s // SUBLANES - 1
    nb = s // tr

    def body(d_ref, n_ref, w_ref, _, o_ref):
        d = d_ref[...]
        halo = jnp.where(pl.program_id(1) == nb - 1, 0.0, n_ref[...])
        acc = w_ref[CONV_W - 1:CONV_W, :] * d
        for j in range(1, CONV_W):
            acc = acc + w_ref[CONV_W - 1 - j:CONV_W - j, :] * _shift_up(d, halo, j)
        o_ref[...] = acc.astype(o_ref.dtype)

    own = pl.BlockSpec((tr, CONV_CB), lambda cb, i: (i, cb))
    nxt = pl.BlockSpec((SUBLANES, CONV_CB), lambda cb, i: (jnp.minimum((i + 1) * per8, last8), cb))
    return pl.pallas_call(
        body, name="conv_bwd_in", grid=(CONV_DIM // CONV_CB, nb),
        in_specs=[own, nxt, pl.BlockSpec((CONV_W, CONV_CB), lambda cb, i: (0, cb)), ANY],
        out_specs=pl.BlockSpec((tr, CONV_CB), lambda cb, i: (i, _conv_col(cb))),
        out_shape=jax.ShapeDtypeStruct(dproj.shape, dproj.dtype), input_output_aliases={3: 0},
        compiler_params=_params(("parallel", "parallel")),
    )(dpre, dpre, conv_w, dproj)


XC_B, XC_C = 1024, 1280


def _tile_iotas():
    row = lax.broadcasted_iota(jnp.int32, (CHUNK, LANES), 0)
    lane = lax.broadcasted_iota(jnp.int32, (CHUNK, LANES), 1)
    return row, lane


def _ssd_scalars(dtf_ref, bias_ref, alog_ref, row, lane):
    head = lane[:1] < N_HEADS
    raw = dtf_ref[...] + bias_ref[...]
    dt = _softplus(raw)
    a_neg = jnp.where(head, -jnp.exp(alog_ref[...]), 0.0)
    a = dt * a_neg
    tril = (row >= lane).astype(BF16)
    s = _dot3(tril, a)
    return raw, dt, a_neg, s


def _pair(v, j, lo):
    return jnp.where(lo, v[:, 2 * j:2 * j + 1], v[:, 2 * j + 1:2 * j + 2])


def _head_sum(x, lo, hh):
    return jnp.sum(jnp.where(lo == (hh == 0), x, 0.0), axis=1, keepdims=True)


def _decay_masks(s, h, row, lane):
    s_col = jnp.broadcast_to(s[:, h:h + 1], (CHUNK, LANES))
    s_row = s_col.T
    lm = jnp.where(row >= lane, jnp.exp(s_col - s_row), 0.0)
    lmt = jnp.where(row <= lane, jnp.exp(s_row - s_col), 0.0)
    return lm, lmt


def _ssd_fwd(xc_all, proj, dt_bias_l, a_log_l, d_exp):
    s_len = xc_all.shape[0]
    nc = s_len // CHUNK

    def body(x_ref, dtf_ref, bias_ref, alog_ref, dexp_ref, y_ref, prevs_ref, state_ref):
        @pl.when(pl.program_id(0) == 0)
        def _():
            state_ref[...] = jnp.zeros_like(state_ref)

        row, lane = _tile_iotas()
        lo = lane < HEAD_DIM
        _, dt, _, s = _ssd_scalars(dtf_ref, bias_ref, alog_ref, row, lane)
        tot = s[CHUNK - 1:CHUNK, :]
        for g in range(SSM_GROUPS):
            bg = x_ref[:, XC_B + g * SSM_STATE:XC_B + (g + 1) * SSM_STATE].astype(BF16)
            cg = x_ref[:, XC_C + g * SSM_STATE:XC_C + (g + 1) * SSM_STATE].astype(BF16)
            cb = _dot(cg, bg, NT)
            for j in range(g * 4, g * 4 + 4):
                xs_p = x_ref[:, j * LANES:(j + 1) * LANES]
                dt_p, s_p, tot_p = _pair(dt, j, lo), _pair(s, j, lo), _pair(tot, j, lo[:1])
                xc_p = xs_p * dt_p
                xc_b = xc_p.astype(BF16)
                yd = []
                for hh in range(2):
                    lm, _ = _decay_masks(s, 2 * j + hh, row, lane)
                    yd.append(_dot((cb * lm).astype(BF16), xc_b))
                prev = state_ref[j]
                prevs_ref[0, j] = prev
                yo = _dot(cg, prev.astype(BF16)) * jnp.exp(s_p)
                y_ref[:, j * LANES:(j + 1) * LANES] = jnp.where(lo, yd[0], yd[1]) + yo + dexp_ref[:, j * LANES:(j + 1) * LANES] * xs_p
                to_end = jnp.exp(tot_p - s_p)
                state_ref[j] = jnp.exp(tot_p) * prev + _dot(bg, (xc_p * to_end).astype(BF16), TN)

    vec = lambda w: pl.BlockSpec((1, w), lambda c: (0, 0))
    return pl.pallas_call(
        body, name="ssd_fwd", grid=(nc,),
        in_specs=[pl.BlockSpec((CHUNK, CONV_DIM), lambda c: (c, 0)), pl.BlockSpec((CHUNK, LANES), lambda c: (c, AL_DTF // LANES)),
                  vec(LANES), vec(LANES), vec(D_MODEL)],
        out_specs=[pl.BlockSpec((CHUNK, D_MODEL), lambda c: (c, 0)), pl.BlockSpec((1, N_PAIRS, SSM_STATE, LANES), lambda c: (c, 0, 0, 0))],
        out_shape=[jax.ShapeDtypeStruct((s_len, D_MODEL), F32), jax.ShapeDtypeStruct((nc, N_PAIRS, SSM_STATE, LANES), F32)],
        scratch_shapes=[pltpu.VMEM((N_PAIRS, SSM_STATE, LANES), F32)],
        compiler_params=_params(("arbitrary",)),
    )(xc_all, proj, dt_bias_l, a_log_l, d_exp)


def _ssd_bwd(xc_all, proj, dt_bias_l, a_log_l, d_exp, prevs, dy):
    s_len = xc_all.shape[0]
    nc = s_len // CHUNK

    def body(x_ref, dtf_ref, bias_ref, alog_ref, dexp_ref, prevs_ref, dy_ref, dx_ref, ddt_ref, da_ref, dd_ref, dbias_ref, dstate_ref):
        @pl.when(pl.program_id(0) == 0)
        def _():
            dstate_ref[...] = jnp.zeros_like(dstate_ref)
            da_ref[...] = jnp.zeros_like(da_ref)
            dd_ref[...] = jnp.zeros_like(dd_ref)
            dbias_ref[...] = jnp.zeros_like(dbias_ref)

        row, lane = _tile_iotas()
        lo = lane < HEAD_DIM
        last = row == CHUNK - 1
        raw, dt, a_neg, s = _ssd_scalars(dtf_ref, bias_ref, alog_ref, row, lane)
        tot = s[CHUNK - 1:CHUNK, :]
        ds_acc = jnp.zeros((CHUNK, LANES), F32)
        ddt_acc = jnp.zeros((CHUNK, LANES), F32)
        for g in range(SSM_GROUPS):
            bcol = slice(XC_B + g * SSM_STATE, XC_B + (g + 1) * SSM_STATE)
            ccol = slice(XC_C + g * SSM_STATE, XC_C + (g + 1) * SSM_STATE)
            bg = x_ref[:, bcol].astype(BF16)
            cg = x_ref[:, ccol].astype(BF16)
            cb = _dot(cg, bg, NT)
            cbt = _dot(bg, cg, NT)
            dcb = jnp.zeros((CHUNK, LANES), F32)
            dcbt = jnp.zeros((CHUNK, LANES), F32)
            db_acc = jnp.zeros((CHUNK, LANES), F32)
            dc_acc = jnp.zeros((CHUNK, LANES), F32)
            for j in range(g * 4, g * 4 + 4):
                cols = slice(j * LANES, (j + 1) * LANES)
                xs_p, dy_p = x_ref[:, cols], dy_ref[:, cols]
                dt_p, s_p, tot_p = _pair(dt, j, lo), _pair(s, j, lo), _pair(tot, j, lo[:1])
                xc_p = xs_p * dt_p
                xc_b, dy_b = xc_p.astype(BF16), dy_p.astype(BF16)
                e_p, f_p, etot_p = jnp.exp(s_p), jnp.exp(tot_p - s_p), jnp.exp(tot_p)
                prev, dnext = prevs_ref[0, j], dstate_ref[j]
                prev_b, dnext_b = prev.astype(BF16), dnext.astype(BF16)
                dd_ref[:, cols] += _colsum(dy_p * xs_p)
                dxs_p = dexp_ref[:, cols] * dy_p
                cp = _dot(cg, prev_b)
                gy = (dy_p * e_p).astype(BF16)
                dc_acc += _dot(gy, prev_b, NT)
                dstate_ref[j] = etot_p * dnext + _dot(cg, gy, TN)
                de = dy_p * cp * e_p
                bds = _dot(bg, dnext_b)
                db_acc += _dot((xc_p * f_p).astype(BF16), dnext_b, NT)
                dxc_p = bds * f_p
                df = bds * xc_p * f_p
                dtot_p = _colsum(dnext * prev) * etot_p + _colsum(df)
                dsl = de - df + jnp.where(last, dtot_p, 0.0)
                for hh in range(2):
                    h = 2 * j + hh
                    mine = lo == (hh == 0)
                    lm, lmt = _decay_masks(s, h, row, lane)
                    dy_h = jnp.where(mine, dy_p, 0.0).astype(BF16)
                    xc_h = jnp.where(mine, xc_p, 0.0).astype(BF16)
                    dm = _dot(dy_h, xc_b, NT)
                    dmt = _dot(xc_h, dy_b, NT)
                    mt = cbt * lmt
                    dxc_p += _dot(mt.astype(BF16), dy_h)
                    ds_h = (jnp.sum(dm * cb * lm, axis=1, keepdims=True) - jnp.sum(dmt * mt, axis=1, keepdims=True)
                            + _head_sum(dsl, lo, hh))
                    ds_acc += jnp.where(lane == h, ds_h, 0.0)
                    dcb += dm * lm
                    dcbt += dmt * lmt
                    ddt_acc += jnp.where(lane == h, _head_sum(dxc_p * xs_p, lo, hh), 0.0)
                dx_ref[:, cols] = dxs_p + dxc_p * dt_p
            dx_ref[:, ccol] = dc_acc + _dot(dcb.astype(BF16), bg)
            dx_ref[:, bcol] = db_acc + _dot(dcbt.astype(BF16), cg)
        triu = (row <= lane).astype(BF16)
        da = _dot3(triu, ds_acc)
        ddt = ddt_acc + da * a_neg
        da_ref[...] += _colsum(da * dt) * a_neg[:1]
        ddt_raw = jnp.where(lane < N_HEADS, ddt * _sigmoid(raw), 0.0)
        dbias_ref[...] += _colsum(ddt_raw)
        ddt_ref[...] = ddt_raw

    rev = lambda c: nc - 1 - c
    vec = lambda w: pl.BlockSpec((1, w), lambda c: (0, 0))
    return pl.pallas_call(
        body, name="ssd_bwd", grid=(nc,),
        in_specs=[pl.BlockSpec((CHUNK, CONV_DIM), lambda c: (rev(c), 0)), pl.BlockSpec((CHUNK, LANES), lambda c: (rev(c), AL_DTF // LANES)),
                  vec(LANES), vec(LANES), vec(D_MODEL),
                  pl.BlockSpec((1, N_PAIRS, SSM_STATE, LANES), lambda c: (rev(c), 0, 0, 0)),
                  pl.BlockSpec((CHUNK, D_MODEL), lambda c: (rev(c), 0))],
        out_specs=[pl.BlockSpec((CHUNK, CONV_DIM), lambda c: (rev(c), 0)), pl.BlockSpec((CHUNK, LANES), lambda c: (rev(c), 0)),
                   vec(LANES), vec(D_MODEL), vec(LANES)],
        out_shape=[jax.ShapeDtypeStruct((s_len, CONV_DIM), F32), jax.ShapeDtypeStruct((s_len, LANES), F32),
                   jax.ShapeDtypeStruct((1, LANES), F32), jax.ShapeDtypeStruct((1, D_MODEL), F32), jax.ShapeDtypeStruct((1, LANES), F32)],
        scratch_shapes=[pltpu.VMEM((N_PAIRS, SSM_STATE, LANES), F32)],
        compiler_params=_params(("arbitrary",)),
    )(xc_all, proj, dt_bias_l, a_log_l, d_exp, prevs, dy)


AUG_C, AUG_ONE = 64, 67
NEG = -1e30
ATT_T = 512


def _fox_cum(proj, f_bias_l):
    s_len = proj.shape[0]
    nc = s_len // CHUNK

    def body(dtf_ref, fb_ref, cum_ref):
        row, lane = _tile_iotas()
        tril = (row >= lane).astype(BF16)

        def step(c, carry):
            rows = pl.ds(pl.multiple_of(c * CHUNK, CHUNK), CHUNK)
            lf = -_softplus(-(dtf_ref[rows, :] + fb_ref[...]))
            lf = jnp.where(lane < N_HEADS, pltpu.roll(lf, LANES - F_LANE, 1), 0.0)
            cs = _dot3(tril, lf) + carry
            cum_ref[rows, :] = cs
            return cs[CHUNK - 1:CHUNK, :]

        lax.fori_loop(0, nc, step, jnp.zeros((1, LANES), F32))

    return pl.pallas_call(
        body, name="fox_cum", grid=(1,),
        in_specs=[pl.BlockSpec((s_len, LANES), lambda i: (0, AL_DTF // LANES)), pl.BlockSpec((1, LANES), lambda i: (0, 0))],
        out_specs=pl.BlockSpec((s_len, LANES), lambda i: (0, 0)),
        out_shape=jax.ShapeDtypeStruct((s_len, LANES), F32),
        compiler_params=_params(("arbitrary",)),
    )(proj, f_bias_l)


def _fox_cum_bwd(dcum, proj, f_bias_l, ddt_tile, dproj):
    s_len = proj.shape[0]
    nc = s_len // CHUNK

    def body(dcum_ref, dtf_ref, fb_ref, ddt_ref, _, out_ref, dfb_ref):
        row, lane = _tile_iotas()
        triu = (row <= lane).astype(BF16)
        is_f = (lane >= F_LANE) & (lane < F_LANE + N_HEADS)

        def step(t, carry):
            run, dfb = carry
            rows = pl.ds(pl.multiple_of((nc - 1 - t) * CHUNK, CHUNK), CHUNK)
            rc = _dot3(triu, dcum_ref[rows, :]) + run
            sg = _sigmoid(-(dtf_ref[rows, :] + fb_ref[...]))
            df = jnp.where(is_f, pltpu.roll(rc, F_LANE, 1) * sg, 0.0)
            out_ref[rows, :] = (df + ddt_ref[rows, :]).astype(out_ref.dtype)
            return rc[0:1, :], dfb + _colsum(df)

        _, dfb = lax.fori_loop(0, nc, step, (jnp.zeros((1, LANES), F32), jnp.zeros((1, LANES), F32)))
        dfb_ref[...] = dfb

    whole = pl.BlockSpec((s_len, LANES), lambda i: (0, 0))
    dtf_cols = pl.BlockSpec((s_len, LANES), lambda i: (0, AL_DTF // LANES))
    vec = pl.BlockSpec((1, LANES), lambda i: (0, 0))
    return pl.pallas_call(
        body, name="fox_cum_bwd", grid=(1,),
        in_specs=[whole, dtf_cols, vec, whole, ANY], out_specs=[dtf_cols, vec],
        out_shape=[jax.ShapeDtypeStruct(dproj.shape, dproj.dtype), jax.ShapeDtypeStruct((1, LANES), F32)],
        input_output_aliases={4: 0}, compiler_params=_params(("arbitrary",)),
    )(dcum, proj, f_bias_l, ddt_tile, dproj)


def _attn_prep(proj, cum):
    s_len = proj.shape[0]
    tr = min(512, s_len)

    def body(q_ref, k_ref, v_ref, cum_ref, qa_ref, ka_ref, vb_ref):
        p = pl.program_id(0)
        lane = lax.broadcasted_iota(jnp.int32, (tr, LANES), 1)
        lo = lane < HEAD_DIM
        c = cum_ref[...]
        c1 = c.astype(BF16).astype(F32)
        r = c - c1
        c2 = r.astype(BF16).astype(F32)
        c3 = (r - c2).astype(BF16).astype(F32)
        q, k = q_ref[...] * (HEAD_DIM ** -0.5), k_ref[...]
        for hh in range(2):
            col = lambda x: jnp.sum(jnp.where(lane == 2 * p + hh, x, 0.0), axis=1, keepdims=True)
            a1, a2, a3 = col(c1), col(c2), col(c3)
            qh = q if hh == 0 else pltpu.roll(q, HEAD_DIM, 1)
            kh = k if hh == 0 else pltpu.roll(k, HEAD_DIM, 1)
            q_aug = jnp.where(lane == AUG_C, a1, jnp.where(lane == AUG_C + 1, a2, jnp.where(lane == AUG_C + 2, a3,
                              jnp.where(lane < AUG_ONE + 3, 1.0, 0.0))))
            k_aug = jnp.where(lane < AUG_ONE, 1.0, jnp.where(lane == AUG_ONE, -a1, jnp.where(lane == AUG_ONE + 1, -a2,
                              jnp.where(lane == AUG_ONE + 2, -a3, 0.0))))
            qa_ref[hh] = jnp.where(lo, qh, q_aug).astype(BF16)
            ka_ref[hh] = jnp.where(lo, kh, k_aug).astype(BF16)
        vb_ref[...] = v_ref[...].astype(BF16)

    slab = lambda col0: pl.BlockSpec((tr, LANES), lambda p, i: (i, col0 // LANES + p))
    heads = pl.BlockSpec((2, tr, LANES), lambda p, i: (p, i, 0))
    return pl.pallas_call(
        body, name="attn_prep", grid=(N_PAIRS, s_len // tr),
        in_specs=[slab(AL_Q), slab(AL_K), slab(AL_V), pl.BlockSpec((tr, LANES), lambda p, i: (i, 0))],
        out_specs=[heads, heads, pl.BlockSpec((tr, LANES), lambda p, i: (i, p))],
        out_shape=[jax.ShapeDtypeStruct((N_HEADS, s_len, LANES), BF16), jax.ShapeDtypeStruct((N_HEADS, s_len, LANES), BF16),
                   jax.ShapeDtypeStruct((s_len, D_MODEL), BF16)],
        compiler_params=_params(("parallel", "parallel")),
    )(proj, proj, proj, cum)


def _attn_fwd(qa, ka, vb, halves):
    s_len = vb.shape[0]
    t = min(ATT_T, s_len)
    nq = s_len // t
    n = len(halves)

    def body(qa_ref, ka_ref, vb_ref, *rest):
        o_ref, lse_ref = rest[n:n + 2]
        start, finish = _gather_plan(rest[:n], rest[n + 2:2 * n + 2], *rest[2 * n + 2:])
        i = pl.program_id(1)
        pl.when((pl.program_id(0) == 0) & (i == 0))(start)
        row = lax.broadcasted_iota(jnp.int32, (t, t), 0)
        col = lax.broadcasted_iota(jnp.int32, (t, t), 1)
        lo = lax.broadcasted_iota(jnp.int32, (t, LANES), 1) < HEAD_DIM
        qs = (qa_ref[0], qa_ref[1])

        def block(j, carry, masked):
            rows = pl.ds(pl.multiple_of(j * t, t), t)
            v = vb_ref[rows, :]
            new = []
            for hh in range(2):
                m, l, acc = carry[hh]
                s = _dot(qs[hh], ka_ref[hh, rows, :], NT)
                if masked:
                    s = jnp.where(row >= col, s, NEG)
                m_new = jnp.maximum(m, jnp.max(s, axis=1, keepdims=True))
                alpha = jnp.exp(m - m_new)
                p = jnp.exp(s - m_new)
                new.append((m_new, alpha * l + jnp.sum(p, axis=1, keepdims=True), alpha * acc + _dot(p.astype(BF16), v)))
            return tuple(new)

        init = (jnp.full((t, 1), NEG, F32), jnp.zeros((t, 1), F32), jnp.zeros((t, LANES), F32))
        carry = lax.fori_loop(0, i, functools.partial(block, masked=False), (init, init))
        (m0, l0, acc0), (m1, l1, acc1) = block(i, carry, True)
        o_ref[...] = jnp.where(lo, acc0 / l0, acc1 / l1)
        lse_ref[...] = jnp.where(lo, m0 + jnp.log(l0), m1 + jnp.log(l1))
        pl.when((pl.program_id(0) == N_PAIRS - 1) & (i == nq - 1))(finish)

    out = pl.BlockSpec((t, LANES), lambda p, i: (i, p))
    res = pl.pallas_call(
        body, name="attn_fwd", grid=(N_PAIRS, nq),
        in_specs=[pl.BlockSpec((2, t, LANES), lambda p, i: (p, i, 0)), pl.BlockSpec((2, s_len, LANES), lambda p, i: (p, 0, 0)),
                  pl.BlockSpec((s_len, LANES), lambda p, i: (0, p))] + [ANY] * n,
        out_specs=[out, out] + [ANY] * n,
        out_shape=[jax.ShapeDtypeStruct((s_len, D_MODEL), F32), jax.ShapeDtypeStruct((s_len, D_MODEL), F32)]
        + [jax.ShapeDtypeStruct((N_CHIPS, *h.shape), h.dtype) for h in halves],
        scratch_shapes=_exchange_sems(n),
        compiler_params=_params(("arbitrary", "arbitrary")),
    )(qa, ka, vb, *halves)
    return res[0], res[1], res[2:]


def _attn_bwd(qa, ka, vb, o, lse, do, parts, dproj):
    s_len = vb.shape[0]
    t = min(ATT_T, s_len)
    nq = s_len // t
    n = len(parts)

    def body(qa_ref, ka_ref, vb_ref, o_ref, lse_ref, do_ref, *rest):
        dqa_ref, dka_ref, dv_ref = rest[n + 1:n + 4]
        start, finish = _reduce_plan(rest[:n], rest[n + 4:2 * n + 4], *rest[2 * n + 4:])
        j = pl.program_id(1)
        pl.when((pl.program_id(0) == 0) & (j == 0))(start)

        @pl.when(j == 0)
        def _():
            dqa_ref[...] = jnp.zeros_like(dqa_ref)

        row = lax.broadcasted_iota(jnp.int32, (t, t), 0)
        col = lax.broadcasted_iota(jnp.int32, (t, t), 1)
        lo = lax.broadcasted_iota(jnp.int32, (t, LANES), 1) < HEAD_DIM
        v = vb_ref[...]
        ks = (ka_ref[0], ka_ref[1])

        def block(i, carry, masked):
            dk, dv = list(carry[:2]), carry[2]
            rows = pl.ds(pl.multiple_of(i * t, t), t)
            do_p, o_p, lse_p = do_ref[rows, :], o_ref[rows, :], lse_ref[rows, :]
            for hh in range(2):
                q = qa_ref[hh, rows, :]
                do_h = jnp.where(lo == (hh == 0), do_p, 0.0)
                delta = jnp.sum(do_h * o_p, axis=1, keepdims=True)
                s = _dot(q, ks[hh], NT)
                if masked:
                    s = jnp.where(row >= col, s, NEG)
                p = jnp.exp(s - lse_p[:, hh * HEAD_DIM:hh * HEAD_DIM + 1])
                do_b = do_h.astype(BF16)
                ds = (p * (_dot(do_b, v, NT) - delta)).astype(BF16)
                dv = dv + _dot(p.astype(BF16), do_b, TN)
                dk[hh] = dk[hh] + _dot(ds, q, TN)
                dqa_ref[hh, rows, :] += _dot(ds, ks[hh])
            return dk[0], dk[1], dv

        zero = jnp.zeros((t, LANES), F32)
        carry = block(j, (zero, zero, zero), True)
        dk0, dk1, dv = lax.fori_loop(j + 1, nq, functools.partial(block, masked=False), carry)
        dka_ref[0] = dk0
        dka_ref[1] = dk1
        dv_ref[...] = dv.astype(dv_ref.dtype)
        pl.when((pl.program_id(0) == N_PAIRS - 1) & (j == nq - 1))(finish)

    whole_pair = pl.BlockSpec((2, s_len, LANES), lambda p, j: (p, 0, 0))
    blk_pair = pl.BlockSpec((2, t, LANES), lambda p, j: (p, j, 0))
    whole_cols = pl.BlockSpec((s_len, LANES), lambda p, j: (0, p))
    blk_cols = pl.BlockSpec((t, LANES), lambda p, j: (j, p))
    res = pl.pallas_call(
        body, name="attn_bwd", grid=(N_PAIRS, nq),
        in_specs=[whole_pair, blk_pair, blk_cols, whole_cols, whole_cols, whole_cols] + [ANY] * (n + 1),
        out_specs=[whole_pair, blk_pair, pl.BlockSpec((t, LANES), lambda p, j: (j, AL_V // LANES + p))] + [ANY] * n,
        out_shape=[jax.ShapeDtypeStruct((N_HEADS, s_len, LANES), F32), jax.ShapeDtypeStruct((N_HEADS, s_len, LANES), F32),
                   jax.ShapeDtypeStruct(dproj.shape, dproj.dtype)]
        + [jax.ShapeDtypeStruct((N_DEV, g.shape[1] // 2, g.shape[2]), g.dtype) for g in parts],
        scratch_shapes=_exchange_sems(n), input_output_aliases={6 + n: 2},
        compiler_params=_params(("arbitrary", "arbitrary")),
    )(qa, ka, vb, o, lse, do, *parts, dproj)
    return res[0], res[1], res[2], res[3:]


def _attn_post(dqa, dka, dproj):
    s_len = dqa.shape[1]
    tr = min(256, s_len)
    assert AL_K == AL_Q + D_MODEL and AL_Q % (2 * D_MODEL) == 0

    def body(dqa_ref, dka_ref, _, dqk_ref, dcum_ref):
        lane = lax.broadcasted_iota(jnp.int32, (tr, LANES), 1)
        lo = lane < HEAD_DIM
        dcum = jnp.zeros((tr, LANES), F32)
        for p in range(N_PAIRS):
            a0, a1, b0, b1 = dqa_ref[2 * p], dqa_ref[2 * p + 1], dka_ref[2 * p], dka_ref[2 * p + 1]
            dq = jnp.where(lo, a0, pltpu.roll(a1, HEAD_DIM, 1)) * (HEAD_DIM ** -0.5)
            dqk_ref[:, p * LANES:(p + 1) * LANES] = dq.astype(dqk_ref.dtype)
            dqk_ref[:, D_MODEL + p * LANES:D_MODEL + (p + 1) * LANES] = jnp.where(lo, b0, pltpu.roll(b1, HEAD_DIM, 1)).astype(dqk_ref.dtype)
            for hh, (a, b) in enumerate(((a0, b0), (a1, b1))):
                dcum = dcum + jnp.where(lane == 2 * p + hh, a[:, AUG_C:AUG_C + 1] - b[:, AUG_ONE:AUG_ONE + 1], 0.0)
        dcum_ref[...] = dcum

    heads = pl.BlockSpec((N_HEADS, tr, LANES), lambda i: (0, i, 0))
    return pl.pallas_call(
        body, name="attn_post", grid=(s_len // tr,),
        in_specs=[heads, heads, ANY],
        out_specs=[pl.BlockSpec((tr, 2 * D_MODEL), lambda i: (i, AL_Q // (2 * D_MODEL))), pl.BlockSpec((tr, LANES), lambda i: (i, 0))],
        out_shape=[jax.ShapeDtypeStruct(dproj.shape, dproj.dtype), jax.ShapeDtypeStruct((s_len, LANES), F32)],
        input_output_aliases={2: 0}, compiler_params=_params(("parallel",)),
    )(dqa, dka, dproj)


def _ln_stats(r):
    mu = _rowmean(r)
    xc = r - mu
    rstd = lax.rsqrt(_rowmean(xc * xc) + LN_EPS)
    return xc * rstd, rstd


def _ln_bwd(dxh, xh, rstd):
    return rstd * (dxh - _rowmean(dxh) - xh * _rowmean(dxh * xh))


def _rms_bwd(dgn, g, r):
    return r * dgn - (r * r * r) * g * _rowmean(dgn * g)


def _to_aligned(w):
    pad = jnp.zeros((w.shape[0], AL_COLS - IN_COLS), w.dtype)
    return jnp.concatenate([w[:, :2048], w[:, 2576:5648], w[:, 2048:2560], w[:, 2560:2576], w[:, 5648:5664], pad], axis=1)


def _from_aligned(g):
    return jnp.concatenate([g[:, :AL_Q], g[:, AL_B:AL_DTF], g[:, AL_DTF:AL_DTF + 16], g[:, AL_Q:AL_B], g[:, AL_DTF + 16:AL_DTF + 32]], axis=1)


def _lanes(v, at=0):
    return jnp.pad(v, ((0, 0), (at, LANES - at - v.shape[1])))


def _local_step(x, tgt, mod, w_al, halves, sp):
    d = D_MODEL
    sh1, sc1, g1, sh2, sc2, g2 = [mod[:, i * d:(i + 1) * d] for i in range(6)]
    dt_bias_l, a_log_l, f_bias_l = _lanes(sp["dt_bias"]), _lanes(sp["a_log"]), _lanes(sp["f_bias"], F_LANE)
    d_exp = jnp.repeat(sp["d_skip"], HEAD_DIM, axis=1)
    z_slab = lambda a: (a, d, AL_Z // d)

    (h1,), _ = _rowwise("mod1", lambda x, sc, sh: ([x * (1.0 + sc) + sh], []), [x], [sc1, sh1], [(d, BF16)], [])
    proj = _matmul("proj", h1, w_al, tn=1152)
    xc_all = _conv_fwd(proj, sp["conv_w"], sp["conv_b"])
    y_ssd, prevs = _ssd_fwd(xc_all, proj, dt_bias_l, a_log_l, d_exp)

    def gated_norm(y, z, w):
        g = y * _silu(z)
        return [g * lax.rsqrt(_rowmean(g * g) + RMS_EPS) * w], []

    (y_mix,), _ = _rowwise("ssm_norm", gated_norm, [y_ssd, z_slab(proj)], [sp["ssm_norm_w"]], [(d, BF16, (2 * d, 0, None))], [])
    cum = _fox_cum(proj, f_bias_l)
    qa, ka, vb = _attn_prep(proj, cum)
    o, lse, (g_out, g_fi, g_fo) = _attn_fwd(qa, ka, vb, halves)
    w_out = g_out.reshape(2 * d, d)
    w_fi = g_fi.transpose(1, 0, 2).reshape(d, D_FF)
    w_fo = g_fo.reshape(D_FF, d)
    (y_mix,), _ = _rowwise("attn_norm", lambda o, w: ([o * lax.rsqrt(_rowmean(o * o) + RMS_EPS) * w], []),
                           [o], [sp["attn_norm_w"]], [(d, BF16, (2 * d, 1, y_mix))], [])
    def ln1_fwd(y, x, g1, sc2, sh2, lg, lb):
        r1 = ALPHA * x + (1.0 + g1) * y
        xh, _ = _ln_stats(r1)
        x1 = xh * lg + lb
        return [y, r1, x1 * (1.0 + sc2) + sh2], []

    y, r1, h2 = _matmul("out_proj", y_mix, w_out, tm=512, tk=2048,
                        epi=(ln1_fwd, [x], [g1, sc2, sh2, sp["ln1_g"], sp["ln1_b"]], [F32, F32, BF16], []))
    u, act = _matmul("ff_in", h2, w_fi, epi=(lambda u: ([u, jnp.square(jnp.maximum(u, 0.0))], []), [], [], [F32, BF16], []))

    def head(ff, r1, tgt, g2, l1g, l1b, l2g, l2b):
        xh1, _ = _ln_stats(r1)
        x1 = xh1 * l1g + l1b
        xh2, rstd2 = _ln_stats(ALPHA * x1 + (1.0 + g2) * ff)
        err = xh2 * l2g + l2b - tgt
        loss = 0.5 * jnp.sum(_rowmean(err * err))
        dx2 = err * (1.0 / d)
        dr2 = _ln_bwd(dx2 * l2g, xh2, rstd2)
        return ([dr2, (1.0 + g2) * dr2],
                [_colsum(dx2 * xh2), _colsum(dx2), _colsum(dr2 * ff), jnp.full((1, LANES), loss, F32)])

    dr2, dff, d_ln2_g, d_ln2_b, d_g2, loss = _matmul(
        "ff_out", act, w_fo, tm=512, tk=2048,
        epi=(head, [r1, tgt], [g2, sp["ln1_g"], sp["ln1_b"], sp["ln2_g"], sp["ln2_b"]], [F32, BF16], [d, d, d, LANES]))
    du = _matmul("d_act", dff, w_fo, dims=NT, epi=(lambda da, u: ([da * (2.0 * jnp.maximum(u, 0.0))], []), [u], [], [BF16], []))
    dw_fo = _matmul("dw_ff_out", act, dff, dims=TN, out_dtype=BF16, by_chip="rows")
    dw_fi = _matmul("dw_ff_in", h2, du, dims=TN, out_dtype=BF16, by_chip="cols")

    def ln1_bwd(dh2, r1, dr2, y, sc2, g1, lg, lb):
        xh, rstd = _ln_stats(r1)
        x1 = xh * lg + lb
        dx1 = ALPHA * dr2 + dh2 * (1.0 + sc2)
        dr1 = _ln_bwd(dx1 * lg, xh, rstd)
        return ([dr1, (1.0 + g1) * dr1],
                [_colsum(dh2 * x1), _colsum(dh2), _colsum(dx1 * xh), _colsum(dx1), _colsum(dr1 * y)])

    dr1, dy, d_sc2, d_sh2, d_ln1_g, d_ln1_b, d_g1 = _matmul(
        "dh2", du, w_fi, dims=NT, tm=512, tk=2048,
        epi=(ln1_bwd, [r1, dr2, y], [sc2, g1, sp["ln1_g"], sp["ln1_b"]], [F32, BF16], [d] * 5))
    dymix = _matmul("dy_mix", dy, w_out, dims=NT)
    dw_out = _matmul("dw_out", y_mix, dy, dims=TN, out_dtype=BF16, by_chip="rows")

    def attn_norm_bwd(o, dyo, w):
        r = lax.rsqrt(_rowmean(o * o) + RMS_EPS)
        return [_rms_bwd(dyo * w, o, r)], [_colsum(dyo * o * r)]

    (do,), (d_attn_w,) = _rowwise("attn_norm_bwd", attn_norm_bwd, [o, (dymix, d, 1)], [sp["attn_norm_w"]], [(d, F32)], [d])

    def gated_norm_bwd(y, z, dyo, w):
        sg = _sigmoid(z)
        sz = z * sg
        g = y * sz
        r = lax.rsqrt(_rowmean(g * g) + RMS_EPS)
        dg = _rms_bwd(dyo * w, g, r)
        return [dg * sz, dg * y * (sg * (1.0 + z * (1.0 - sg)))], [_colsum(dyo * g * r)]

    (dy_ssd, dproj), (d_ssm_w,) = _rowwise("ssm_norm_bwd", gated_norm_bwd, [y_ssd, z_slab(proj), (dymix, d, 0)],
                                           [sp["ssm_norm_w"]], [(d, F32), (d, BF16, (AL_COLS, AL_Z // d, None))], [d])
    dqa, dka, dproj, landed = _attn_bwd(qa, ka, vb, o, lse, do, [dw_out, dw_fi, dw_fo], dproj)
    dproj, dcum = _attn_post(dqa, dka, dproj)
    dxc, ddt_tile, d_alog_l, d_dexp, d_dtb_l = _ssd_bwd(xc_all, proj, dt_bias_l, a_log_l, d_exp, prevs, dy_ssd)
    dproj, d_fb_l = _fox_cum_bwd(dcum, proj, f_bias_l, ddt_tile, dproj)
    dpre, d_conv_w, d_conv_b = _conv_bwd_pre(proj, sp["conv_w"], sp["conv_b"], dxc)
    dproj = _conv_bwd_in(dpre, sp["conv_w"], dproj)
    dw_al = _matmul("dw_in", h1, dproj, dims=TN, tn=1152, out_dtype=BF16)
    part_in = _from_aligned(dw_al).reshape(d, N_CHIPS, IN_COLS // N_CHIPS).transpose(1, 0, 2)
    def last(dh1, x, dr1, sc1):
        return [ALPHA * dr1 + dh1 * (1.0 + sc1)], [_colsum(dh1 * x), _colsum(dh1)]

    dx, d_sc1, d_sh1, landed_in = _matmul("dh1", dproj, w_al, dims=NT, tm=512, tk=1152, carry=[part_in],
                                          epi=(last, [x, dr1], [sc1], [F32], [d, d]))

    small = {
        "mod": jnp.concatenate([d_sh1, d_sc1, d_g1, d_sh2, d_sc2, d_g2], axis=1),
        "conv_w": d_conv_w, "conv_b": d_conv_b,
        "dt_bias": d_dtb_l[:, :N_HEADS], "a_log": d_alog_l[:, :N_HEADS],
        "d_skip": jnp.sum(d_dexp.reshape(N_HEADS, HEAD_DIM), axis=1)[None, :],
        "ssm_norm_w": d_ssm_w, "f_bias": d_fb_l[:, F_LANE:F_LANE + N_HEADS], "attn_norm_w": d_attn_w,
        "ln1_g": d_ln1_g, "ln1_b": d_ln1_b, "ln2_g": d_ln2_g, "ln2_b": d_ln2_b, "loss": loss,
    }
    return dx, [landed_in, *landed], small


N_DEV = 8
N_CHIPS = 4
ANY = pl.BlockSpec(memory_space=pl.ANY)
VMEM_SPEC = pl.BlockSpec(memory_space=pltpu.VMEM)


def _place():
    x, y, c = lax.axis_index("x"), lax.axis_index("y"), lax.axis_index("c")
    return x, y, c


def _other_chips(x, y):
    return [(1 - x, y, 2 * (1 - x) + y), (x, 1 - y, 2 * x + 1 - y), (1 - x, 1 - y, 2 * (1 - x) + 1 - y)]


def _allgather_small(name, v):
    r, cdim = v.shape

    def body(v_ref, out_ref, send_sems, recv_sems, local_sem):
        x, y, c = _place()
        me = 4 * x + 2 * y + c
        mine = pltpu.make_async_copy(v_ref, out_ref.at[me], local_sem)
        mine.start()
        peers = []
        for rel in range(1, N_DEV):
            px = 1 - x if rel & 4 else x
            py = 1 - y if rel & 2 else y
            pc = 1 - c if rel & 1 else c
            peers.append((px, py, pc))

        def copy(rel, slot, to):
            return pltpu.make_async_remote_copy(src_ref=v_ref, dst_ref=out_ref.at[slot], send_sem=send_sems.at[rel],
                                                recv_sem=recv_sems.at[rel], device_id=to, device_id_type=MESH)

        sends = [copy(rel, me, peer) for rel, peer in enumerate(peers)]
        for cp in sends:
            cp.start()
        for rel, (px, py, pc) in enumerate(peers):
            copy(rel, 4 * px + 2 * py + pc, (x, y, c)).wait_recv()
        for cp in sends:
            cp.wait_send()
        mine.wait()

    return pl.pallas_call(
        body, name=name, out_shape=jax.ShapeDtypeStruct((N_DEV, r, cdim), v.dtype),
        in_specs=[VMEM_SPEC], out_specs=VMEM_SPEC,
        scratch_shapes=[pltpu.SemaphoreType.DMA((N_DEV - 1,)), pltpu.SemaphoreType.DMA((N_DEV - 1,)), pltpu.SemaphoreType.DMA],
    )(v)


def _gather_shards(shard):
    r, cdim = shard.shape
    rh = r // 2

    def body(in_ref, out_ref, stage, send_sems, recv_sems, local_sems):
        x, y, c = _place()
        k_me = 2 * x + y
        me, sibling = (x, y, c), (x, y, 1 - c)
        chips = _other_chips(x, y)

        def copy(idx, k, half, to, src=None):
            rows = out_ref.at[k, pl.ds(pl.multiple_of(half * rh, rh), rh), :]
            return pltpu.make_async_remote_copy(src_ref=rows if src is None else src, dst_ref=rows, send_sem=send_sems.at[idx],
                                                recv_sem=recv_sems.at[idx], device_id=to, device_id_type=MESH)

        mine = in_ref.at[pl.ds(pl.multiple_of(c * rh, rh), rh), :]
        sends = [copy(j, k_me, c, (cx, cy, c), src=mine) for j, (cx, cy, _) in enumerate(chips)]
        for cp in sends:
            cp.start()
        load = pltpu.make_async_copy(in_ref, stage, local_sems.at[0])
        load.start()
        load.wait()
        store = pltpu.make_async_copy(stage, out_ref.at[k_me], local_sems.at[1])
        store.start()
        for j, (_, _, kj) in enumerate(chips):
            copy(j, kj, c, me).wait_recv()
            fwd = copy(3 + j, kj, c, sibling)
            fwd.start()
            sends.append(fwd)
        for j, (_, _, kj) in enumerate(chips):
            copy(3 + j, kj, 1 - c, me).wait_recv()
        for cp in sends:
            cp.wait_send()
        store.wait()

    return pl.pallas_call(
        body, name="gather_w_in", out_shape=jax.ShapeDtypeStruct((N_CHIPS, r, cdim), shard.dtype),
        in_specs=[ANY], out_specs=ANY,
        scratch_shapes=[pltpu.VMEM((r, cdim), shard.dtype), pltpu.SemaphoreType.DMA((6,)), pltpu.SemaphoreType.DMA((6,)),
                        pltpu.SemaphoreType.DMA((2,))],
        compiler_params=_params(),
    )(shard)


def _peers(x, y, c):
    return [((1 - x) if rel & 4 else x, (1 - y) if rel & 2 else y, (1 - c) if rel & 1 else c) for rel in range(1, N_DEV)]


def _exchange_sems(n):
    return [pltpu.SemaphoreType.DMA((n, N_DEV - 1)), pltpu.SemaphoreType.DMA((n, N_DEV - 1)), pltpu.SemaphoreType.DMA((n,))]


def _gather_plan(ins, outs, send_sems, recv_sems, local_sems):
    x, y, c = _place()
    k_me = 2 * x + y
    peers = [(rel, p) for rel, p in enumerate(_peers(x, y, c)) if (rel + 1) & 6]

    def copy(w, rel, k, half, to, src=None):
        rh = ins[w].shape[0] // 2
        rows = outs[w].at[k, pl.ds(pl.multiple_of(half * rh, rh), rh), :]
        return pltpu.make_async_remote_copy(src_ref=rows if src is None else src, dst_ref=rows, send_sem=send_sems.at[w, rel],
                                            recv_sem=recv_sems.at[w, rel], device_id=to, device_id_type=MESH)

    def mine(w):
        rh = ins[w].shape[0] // 2
        return ins[w].at[pl.ds(pl.multiple_of(c * rh, rh), rh), :]

    n = len(ins)
    local = [pltpu.make_async_copy(ins[w], outs[w].at[k_me], local_sems.at[w]) for w in range(n)]
    sends = [copy(w, rel, k_me, c, peer, src=mine(w)) for w in range(n) for rel, peer in peers]

    def start():
        for cp in local + sends:
            cp.start()

    def finish():
        for w in range(n):
            for rel, (px, py, pc) in peers:
                copy(w, rel, 2 * px + py, pc, (x, y, c)).wait_recv()
        for cp in sends:
            cp.wait_send()
        for cp in local:
            cp.wait()

    return start, finish


def _reduce_plan(ins, outs, send_sems, recv_sems, local_sems):
    x, y, c = _place()
    me = 4 * x + 2 * y + c
    peers = _peers(x, y, c)

    def block(w, k, half):
        rh = ins[w].shape[1] // 2
        return ins[w].at[k, pl.ds(pl.multiple_of(half * rh, rh), rh), :]

    def copy(w, rel, src, slot, to):
        return pltpu.make_async_remote_copy(src_ref=src, dst_ref=outs[w].at[slot], send_sem=send_sems.at[w, rel],
                                            recv_sem=recv_sems.at[w, rel], device_id=to, device_id_type=MESH)

    n = len(ins)
    local = [pltpu.make_async_copy(block(w, 2 * x + y, c), outs[w].at[me], local_sems.at[w]) for w in range(n)]
    sends = [copy(w, rel, block(w, 2 * px + py, pc), me, (px, py, pc)) for w in range(n) for rel, (px, py, pc) in enumerate(peers)]

    def start():
        for cp in local + sends:
            cp.start()

    def finish():
        for w in range(n):
            for rel, (px, py, pc) in enumerate(peers):
                copy(w, rel, block(w, 2 * x + y, c), 4 * px + 2 * py + pc, (x, y, c)).wait_recv()
        for cp in sends:
            cp.wait_send()
        for cp in local:
            cp.wait()

    return start, finish


def _sum_blocks(name, parts):
    k, r, cdim = parts.shape
    tr = min(256, r)

    def body(p_ref, o_ref):
        acc = p_ref[0].astype(F32)
        for i in range(1, k):
            acc = acc + p_ref[i].astype(F32)
        o_ref[...] = acc

    return pl.pallas_call(
        body, name=name, grid=(r // tr,),
        in_specs=[pl.BlockSpec((k, tr, cdim), lambda i: (0, i, 0))], out_specs=pl.BlockSpec((tr, cdim), lambda i: (i, 0)),
        out_shape=jax.ShapeDtypeStruct((r, cdim), F32), compiler_params=_params(("parallel",)),
    )(parts)


def _pair_swap(halves):
    n = len(halves)

    def body(*refs):
        ins, outs = refs[:n], refs[n:2 * n]
        send_sems, recv_sems = refs[2 * n:]
        x, y, c = _place()
        cps = [pltpu.make_async_remote_copy(src_ref=ins[w], dst_ref=outs[w], send_sem=send_sems.at[w], recv_sem=recv_sems.at[w],
                                            device_id=(x, y, 1 - c), device_id_type=MESH) for w in range(n)]
        for cp in cps:
            cp.start()
        for cp in cps:
            cp.wait_recv()
        for cp in cps:
            cp.wait_send()

    return pl.pallas_call(
        body, name="pair_swap", out_shape=[jax.ShapeDtypeStruct(h.shape, h.dtype) for h in halves],
        in_specs=[ANY] * n, out_specs=[ANY] * n,
        scratch_shapes=[pltpu.SemaphoreType.DMA((n,)), pltpu.SemaphoreType.DMA((n,))],
    )(*halves)


ADA_SHARD = 6 * D_MODEL // N_CHIPS


def _mod_part(c_all, w_shard, b_shard):
    tn = 512

    def body(c_ref, w_ref, b_ref, o_ref):
        o_ref[...] = _dot(_silu(c_ref[...]).astype(BF16), w_ref[...].astype(BF16)) + b_ref[...]

    return pl.pallas_call(
        body, name="mod_part", grid=(ADA_SHARD // tn,),
        in_specs=[pl.BlockSpec((N_DEV, D_MODEL), lambda j: (0, 0)), pl.BlockSpec((D_MODEL, tn), lambda j: (0, j)),
                  pl.BlockSpec((1, tn), lambda j: (0, j))],
        out_specs=pl.BlockSpec((N_DEV, tn), lambda j: (0, j)),
        out_shape=jax.ShapeDtypeStruct((N_DEV, ADA_SHARD), F32), compiler_params=_params(("parallel",)),
    )(c_all, w_shard, b_shard)


def _w_ada_grad(c_all_t, dmod_shard):
    tm = 256

    def body(ct_ref, dm_ref, o_ref):
        act = _silu(ct_ref[...])
        acc = act[:, 0:1] * dm_ref[0:1, :]
        for dev in range(1, N_DEV):
            acc = acc + act[:, dev:dev + 1] * dm_ref[dev:dev + 1, :]
        o_ref[...] = acc

    return pl.pallas_call(
        body, name="w_ada_grad", grid=(D_MODEL // tm,),
        in_specs=[pl.BlockSpec((tm, N_DEV), lambda i: (i, 0)), pl.BlockSpec((N_DEV, ADA_SHARD), lambda i: (0, 0))],
        out_specs=pl.BlockSpec((tm, ADA_SHARD), lambda i: (i, 0)),
        out_shape=jax.ShapeDtypeStruct((D_MODEL, ADA_SHARD), F32), compiler_params=_params(("parallel",)),
    )(c_all_t, dmod_shard)


def _adamw_math(w, g, m, v):
    nm = ADAM_B1 * m + (1.0 - ADAM_B1) * g
    nv = ADAM_B2 * v + (1.0 - ADAM_B2) * jnp.square(g)
    m_hat = nm / (1.0 - ADAM_B1 ** ADAM_STEP)
    v_hat = nv / (1.0 - ADAM_B2 ** ADAM_STEP)
    return -ADAM_LR * (m_hat / (jnp.sqrt(v_hat) + ADAM_EPS) + ADAM_WD * w), nm, nv


def _adamw(name, w, g, m, v):
    _, r, cdim = w.shape
    tr = 256 if r % 256 == 0 else r

    def body(w_ref, g_ref, m_ref, v_ref, go_ref, d_ref, nm_ref, nv_ref):
        go_ref[...] = g_ref[...]
        d_ref[...], nm_ref[...], nv_ref[...] = _adamw_math(w_ref[...], g_ref[...], m_ref[...], v_ref[...])

    blk = pl.BlockSpec((None, tr, cdim), lambda i: (0, i, 0))
    return pl.pallas_call(
        body, name=name, grid=(r // tr,), in_specs=[blk, pl.BlockSpec((tr, cdim), lambda i: (i, 0)), blk, blk], out_specs=[blk] * 4,
        out_shape=[jax.ShapeDtypeStruct((1, r, cdim), F32)] * 4, compiler_params=_params(("parallel",)),
    )(w, g, m, v)


def _adamw_pair(name, w, mine, other, m, v, c):
    _, r, cdim = w.shape
    rh = r // 2
    tr = min(256, rh)
    per = rh // tr

    def body(c_ref, w_ref, a_ref, b_ref, m_ref, v_ref, g_ref, d_ref, nm_ref, nv_ref):
        is_mine = (pl.program_id(0) // per) == c_ref[0]
        g = jnp.where(is_mine, a_ref[...], b_ref[...])
        g_ref[...] = g
        d_ref[...], nm_ref[...], nv_ref[...] = _adamw_math(w_ref[...], g, m_ref[...], v_ref[...])

    blk = pl.BlockSpec((None, tr, cdim), lambda i, c_ref: (0, i, 0))
    half = pl.BlockSpec((tr, cdim), lambda i, c_ref: (i % per, 0))
    return pl.pallas_call(
        body, name=name,
        grid_spec=pltpu.PrefetchScalarGridSpec(num_scalar_prefetch=1, grid=(r // tr,), in_specs=[blk, half, half, blk, blk], out_specs=[blk] * 4),
        out_shape=[jax.ShapeDtypeStruct((1, r, cdim), F32)] * 4, compiler_params=_params(("parallel",)),
    )(jnp.reshape(c, (1,)).astype(jnp.int32), w, mine, other, m, v)


SMALL = ["b_ada", "conv_b", "dt_bias", "a_log", "d_skip", "ssm_norm_w", "f_bias", "attn_norm_w", "ln1_g", "ln1_b", "ln2_g", "ln2_b"]


def _pack(vs):
    pieces = []
    for v in vs:
        pieces.append(v)
        if v.shape[1] % LANES:
            pieces.append(jnp.zeros((1, -v.shape[1] % LANES), v.dtype))
    return jnp.concatenate(pieces, axis=1)


def _adamw_small(total, offs, ws, ms, vs):
    n = len(ws)

    def body(*refs):
        t_ref, outs = refs[0], refs[1 + 3 * n:]
        for i in range(n):
            g = t_ref[:, offs[i]:offs[i] + ws[i].shape[1]]
            dl, nm, nv = _adamw_math(refs[1 + i][...], g, refs[1 + n + i][...], refs[1 + 2 * n + i][...])
            outs[4 * i][...], outs[4 * i + 1][...], outs[4 * i + 2][...], outs[4 * i + 3][...] = g, dl, nm, nv

    res = pl.pallas_call(
        body, name="adamw_small", in_specs=[VMEM_SPEC] * (1 + 3 * n), out_specs=[VMEM_SPEC] * (4 * n),
        out_shape=[jax.ShapeDtypeStruct(w.shape, F32) for w in ws for _ in range(4)],
    )(total, *ws, *ms, *vs)
    return [res[4 * i:4 * i + 4] for i in range(n)]


def kernel(x, c, w_ada, b_ada, w_in, conv_w, conv_b, dt_bias, a_log, d_skip, ssm_norm_w, f_bias, attn_norm_w, w_out, ln1_g, ln1_b, w_ff_in, w_ff_out, ln2_g, ln2_b, loss_target, m_w_ada, m_b_ada, m_w_in, m_conv_w, m_conv_b, m_dt_bias, m_a_log, m_d_skip, m_ssm_norm_w, m_f_bias, m_attn_norm_w, m_w_out, m_ln1_g, m_ln1_b, m_w_ff_in, m_w_ff_out, m_ln2_g, m_ln2_b, v_w_ada, v_b_ada, v_w_in, v_conv_w, v_conv_b, v_dt_bias, v_a_log, v_d_skip, v_ssm_norm_w, v_f_bias, v_attn_norm_w, v_w_out, v_ln1_g, v_ln1_b, v_w_ff_in, v_w_ff_out, v_ln2_g, v_ln2_b):
    a = dict(b_ada=b_ada, conv_b=conv_b, dt_bias=dt_bias, a_log=a_log, d_skip=d_skip, ssm_norm_w=ssm_norm_w, f_bias=f_bias,
             attn_norm_w=attn_norm_w, ln1_g=ln1_g, ln1_b=ln1_b, ln2_g=ln2_g, ln2_b=ln2_b)
    ms = dict(b_ada=m_b_ada, conv_b=m_conv_b, dt_bias=m_dt_bias, a_log=m_a_log, d_skip=m_d_skip, ssm_norm_w=m_ssm_norm_w,
              f_bias=m_f_bias, attn_norm_w=m_attn_norm_w, ln1_g=m_ln1_g, ln1_b=m_ln1_b, ln2_g=m_ln2_g, ln2_b=m_ln2_b)
    vs = dict(b_ada=v_b_ada, conv_b=v_conv_b, dt_bias=v_dt_bias, a_log=v_a_log, d_skip=v_d_skip, ssm_norm_w=v_ssm_norm_w,
              f_bias=v_f_bias, attn_norm_w=v_attn_norm_w, ln1_g=v_ln1_g, ln1_b=v_ln1_b, ln2_g=v_ln2_g, ln2_b=v_ln2_b)
    xi, yi, ci = _place()
    chip = 2 * xi + yi
    me = 4 * xi + 2 * yi + ci
    d = D_MODEL
    conv_shard = CONV_DIM // N_CHIPS

    first = _allgather_small("gather_c", jnp.concatenate([c, conv_w[0].reshape(1, CONV_W * conv_shard)], axis=1))[:, 0]
    c_all = first[:, :d]
    conv_w_full = first[::2, d:].reshape(N_CHIPS, CONV_W, conv_shard).transpose(1, 0, 2).reshape(CONV_W, CONV_DIM)
    b_shard = lax.dynamic_slice_in_dim(b_ada, chip * ADA_SHARD, ADA_SHARD, axis=1)
    parts = _allgather_small("gather_mod", _mod_part(c_all, w_ada[0], b_shard))
    mod = lax.dynamic_index_in_dim(parts[::2], me, axis=1, keepdims=False).reshape(1, 6 * d)

    g_in = _gather_shards(w_in[0].astype(BF16))
    w_al = _to_aligned(g_in.transpose(1, 0, 2).reshape(d, IN_COLS))

    sp = {n: a[n] for n in SMALL[1:]}
    sp["conv_w"] = conv_w_full
    shards = [w_out[0].astype(BF16), w_ff_in[0].astype(BF16), w_ff_out[0].astype(BF16)]
    dx, landed, small = _local_step(x[0], loss_target[0], mod, w_al, shards, sp)

    names = ["mod"] + SMALL[1:]
    vec = _pack([small[n] for n in names] + [small["conv_w"].reshape(1, CONV_W * CONV_DIM), small["loss"]])
    every = _allgather_small("gather_small", vec)
    total = _sum_blocks("sum_small", jnp.broadcast_to(every, (N_DEV, SUBLANES, vec.shape[1])))[:1]
    widths = [6 * d] + [a[n].shape[1] for n in SMALL[1:]]
    offs = [0]
    for w in widths:
        offs.append(offs[-1] + w + (-w % LANES))
    g_conv_w_full = total[:, offs[-1]:offs[-1] + CONV_W * CONV_DIM].reshape(CONV_W, CONV_DIM)
    loss = total[0, offs[-1] + CONV_W * CONV_DIM]
    dmod_shard = lax.dynamic_slice_in_dim(every[:, 0, :6 * d], chip * ADA_SHARD, ADA_SHARD, axis=1)
    g_w_ada = _w_ada_grad(c_all.T, dmod_shard)
    g_conv_w = lax.dynamic_slice_in_dim(g_conv_w_full, chip * conv_shard, conv_shard, axis=1)

    mine = [_sum_blocks("dev_sum_%d" % i, p) for i, p in enumerate(landed)]
    other = _pair_swap(mine)

    grads, deltas, new_m, new_v = {}, {}, {}, {}
    paired = dict(w_in=(w_in, m_w_in, v_w_in), w_out=(w_out, m_w_out, v_w_out), w_ff_in=(w_ff_in, m_w_ff_in, v_w_ff_in),
                  w_ff_out=(w_ff_out, m_w_ff_out, v_w_ff_out))
    for i, (n, (w, m, v)) in enumerate(paired.items()):
        grads[n], deltas[n], new_m[n], new_v[n] = _adamw_pair("adamw_" + n, w, mine[i], other[i], m, v, ci)
    for n, g, (w, m, v) in (("w_ada", g_w_ada, (w_ada, m_w_ada, v_w_ada)), ("conv_w", g_conv_w, (conv_w, m_conv_w, v_conv_w))):
        grads[n], deltas[n], new_m[n], new_v[n] = _adamw("adamw_" + n, w, g, m, v)
    for n, res in zip(SMALL, _adamw_small(total, offs, [a[n] for n in SMALL], [ms[n] for n in SMALL], [vs[n] for n in SMALL])):
        grads[n], deltas[n], new_m[n], new_v[n] = res

    order = ["w_ada", "b_ada", "w_in", "conv_w", "conv_b", "dt_bias", "a_log", "d_skip", "ssm_norm_w", "f_bias", "attn_norm_w", "w_out",
             "ln1_g", "ln1_b", "w_ff_in", "w_ff_out", "ln2_g", "ln2_b"]
    return (loss, dx[None], *[grads[n] for n in order], *[deltas[n] for n in order], *[new_m[n] for n in order], *[new_v[n] for n in order])
```

```python
import functools

import jax
import jax.numpy as jnp
from jax import lax
from jax.experimental import pallas as pl
from jax.experimental.pallas import tpu as pltpu

F32, BF16 = jnp.float32, jnp.bfloat16

D_MODEL = 1024
N_HEADS = 16
HEAD_DIM = 64
N_PAIRS = N_HEADS // 2
SSM_GROUPS = 2
SSM_STATE = 128
CHUNK = 128
CONV_W = 4
CONV_DIM = 1536
D_FF = 4096
IN_COLS = 5664
ALPHA = 2.0 ** 0.25
LN_EPS = 1e-5
RMS_EPS = 1e-5
LANES = 128
SUBLANES = 8

AL_Z, AL_XS, AL_Q, AL_K, AL_V, AL_B, AL_C, AL_DTF = 0, 1024, 2048, 3072, 4096, 5120, 5376, 5632
AL_COLS = 5760
F_LANE = 16

ADAM_LR, ADAM_B1, ADAM_B2, ADAM_EPS, ADAM_WD, ADAM_STEP = 0.001, 0.9, 0.999, 1e-08, 0.01, 10

VMEM_LIMIT = 56 * 1024 * 1024
MESH = pl.DeviceIdType.MESH


def _params(sem=None):
    return pltpu.CompilerParams(dimension_semantics=sem, vmem_limit_bytes=VMEM_LIMIT)


def _sigmoid(x):
    return 1.0 / (1.0 + jnp.exp(-x))


def _silu(x):
    return x * _sigmoid(x)


def _softplus(x):
    return jnp.maximum(x, 0.0) + jnp.log(1.0 + jnp.exp(-jnp.abs(x)))


def _split3(a):
    hi = a.astype(BF16)
    r = a - hi.astype(F32)
    mid = r.astype(BF16)
    lo = (r - mid.astype(F32)).astype(BF16)
    return hi, mid, lo


def _dot(a, b, dims=((1,), (0,))):
    return lax.dot_general(a, b, (dims, ((), ())), preferred_element_type=F32)


NN, NT, TN = ((1,), (0,)), ((1,), (1,)), ((0,), (0,))


def _dot3(t, a):
    hi, mid, lo = _split3(a)
    return _dot(t, hi) + _dot(t, mid) + _dot(t, lo)


def _matmul(name, a, b, *, dims=NN, out_dtype=F32, tm=1024, tn=1024, tk=1024, by_chip=None, epi=None, carry=()):
    if dims == NN:
        (m, k), n = a.shape, b.shape[1]
    elif dims == NT:
        (m, k), n = a.shape, b.shape[0]
    else:
        (k, m), n = a.shape, b.shape[1]
    if by_chip == "rows":
        tm = min(tm, m // 4)
    if by_chip == "cols":
        tn = min(tn, n // 4)
    tm, tn, tk = min(tm, m), min(tn, n), min(tk, k)
    assert m % tm == 0 and n % tn == 0 and k % tk == 0, (name, m, n, k, tm, tn, tk)
    nk = k // tk
    if by_chip == "rows":
        per = m // 4 // tm
        out_spec = pl.BlockSpec((None, tm, tn), lambda i, j, l: (i // per, i % per, j))
        out_shape = jax.ShapeDtypeStruct((4, m // 4, n), out_dtype)
    elif by_chip == "cols":
        per = n // 4 // tn
        out_spec = pl.BlockSpec((None, tm, tn), lambda i, j, l: (j // per, i, j % per))
        out_shape = jax.ShapeDtypeStruct((4, m, n // 4), out_dtype)
    else:
        out_spec = pl.BlockSpec((tm, tn), lambda i, j, l: (i, j))
        out_shape = jax.ShapeDtypeStruct((m, n), out_dtype)
    a_spec = pl.BlockSpec((tk, tm), lambda i, j, l: (l, i)) if dims == TN else pl.BlockSpec((tm, tk), lambda i, j, l: (i, l))
    b_spec = pl.BlockSpec((tn, tk), lambda i, j, l: (j, l)) if dims == NT else pl.BlockSpec((tk, tn), lambda i, j, l: (l, j))

    tile = pl.BlockSpec((tm, tn), lambda i, j, l: (i, j))
    in_specs, args, out_specs, out_shape = [a_spec, b_spec], [a, b], [out_spec], [out_shape]
    fn, n_tiles, n_sums = None, 1, 0
    if epi is not None:
        fn, fulls, vecs, outs, sums = epi
        assert by_chip is None and (not sums or n == tn), name
        in_specs = in_specs + [tile] * len(fulls) + [pl.BlockSpec((1, tn), lambda i, j, l: (0, j))] * len(vecs)
        args = args + list(fulls) + list(vecs)
        out_specs = [tile] * len(outs) + [pl.BlockSpec((1, w), lambda i, j, l: (0, 0)) for w in sums]
        out_shape = [jax.ShapeDtypeStruct((m, n), dt) for dt in outs] + [jax.ShapeDtypeStruct((1, w), F32) for w in sums]
        n_tiles, n_sums = len(outs), len(sums)
    n_in, n_out, n_c = len(args), len(out_specs), len(carry)
    scratch = [pltpu.VMEM((tm, tn) if nk > 1 else (SUBLANES, LANES), F32)]
    if n_c:
        in_specs, args = in_specs + [ANY] * n_c, args + list(carry)
        out_specs = out_specs + [ANY] * n_c
        out_shape = out_shape + [jax.ShapeDtypeStruct(g.shape, g.dtype) for g in carry]
        scratch = scratch + _exchange_sems(n_c)
    gm, gn = m // tm, n // tn

    def body(*refs):
        a_ref, b_ref = refs[:2]
        ins, outs = refs[2:n_in], refs[n_in + n_c:n_in + n_c + n_out]
        acc_ref = refs[n_in + 2 * n_c + n_out]
        i, j, l = pl.program_id(0), pl.program_id(1), pl.program_id(2)
        if n_c:
            start, wait = _scatter_plan(refs[n_in:n_in + n_c], refs[n_in + n_c + n_out:n_in + 2 * n_c + n_out], *refs[n_in + 2 * n_c + n_out + 1:])
            pl.when((i == 0) & (j == 0) & (l == 0))(start)
        part = _dot(a_ref[...].astype(BF16), b_ref[...].astype(BF16), dims)

        def finish(res):
            if fn is None:
                outs[0][...] = res.astype(outs[0].dtype)
                return
            tiles, colsums = fn(res, *[r[...] for r in ins])
            for r, val in zip(outs[:n_tiles], tiles):
                r[...] = val.astype(r.dtype)
            if n_sums:
                @pl.when(i == 0)
                def _():
                    for r in outs[n_tiles:]:
                        r[...] = jnp.zeros_like(r)
                for r, val in zip(outs[n_tiles:], colsums):
                    r[...] += val

        if nk == 1:
            finish(part)
        else:
            @pl.when(l == 0)
            def _():
                acc_ref[...] = part

            @pl.when((l > 0) & (l < nk - 1))
            def _():
                acc_ref[...] += part

            @pl.when(l == nk - 1)
            def _():
                finish(acc_ref[...] + part)

        if n_c:
            pl.when((i == gm - 1) & (j == gn - 1) & (l == nk - 1))(wait)

    res = pl.pallas_call(
        body, name=name, grid=(gm, gn, nk),
        in_specs=in_specs, out_specs=out_specs, out_shape=out_shape, scratch_shapes=scratch,
        compiler_params=_params(("arbitrary",) * 3 if n_c or n_sums else ("parallel", "parallel", "arbitrary")),
    )(*args)
    return res[0] if len(res) == 1 else res


def _rowwise(name, fn, fulls, vecs, out_fulls, out_vecs, tr=256):
    fulls = [f if isinstance(f, tuple) else (f, f.shape[1], 0) for f in fulls]
    s = fulls[0][0].shape[0]
    tr = min(tr, s)
    out_fulls = [o if len(o) == 3 else (*o, (o[0], 0, None)) for o in out_fulls]
    into = [(k, slab[2]) for k, (_, _, slab) in enumerate(out_fulls) if slab[2] is not None]
    nf, nv, nof, nov = len(fulls), len(vecs), len(out_fulls), len(out_vecs)
    in_specs = [pl.BlockSpec((tr, w), functools.partial(lambda i, cb: (i, cb), cb=cb)) for (_, w, cb) in fulls]
    in_specs += [pl.BlockSpec(v.shape, lambda i: (0, 0)) for v in vecs] + [ANY] * len(into)
    out_shape = [jax.ShapeDtypeStruct((s, slab[0]), dt) for (_, dt, slab) in out_fulls] + [jax.ShapeDtypeStruct((1, w), F32) for w in out_vecs]
    out_specs = [pl.BlockSpec((tr, w), functools.partial(lambda i, cb: (i, cb), cb=slab[1])) for (w, _, slab) in out_fulls]
    out_specs += [pl.BlockSpec((1, w), lambda i: (0, 0)) for w in out_vecs]

    def body(*refs):
        outs = refs[nf + nv + len(into):]
        of, ov = fn(*[r[...] for r in refs[:nf + nv]])
        for r, val in zip(outs[:nof], of):
            r[...] = val.astype(r.dtype)
        if nov:
            @pl.when(pl.program_id(0) == 0)
            def _():
                for r in outs[nof:]:
                    r[...] = jnp.zeros_like(r)
            for r, val in zip(outs[nof:], ov):
                r[...] += val

    res = pl.pallas_call(
        body, name=name, grid=(s // tr,), in_specs=in_specs, out_specs=out_specs, out_shape=out_shape,
        input_output_aliases={nf + nv + pos: k for pos, (k, _) in enumerate(into)},
        compiler_params=_params(("arbitrary",)),
    )(*[f[0] for f in fulls], *vecs, *[buf for _, buf in into])
    return res[:nof], res[nof:]


def _colsum(x):
    return jnp.sum(x, axis=0, keepdims=True)


def _rowmean(x):
    return jnp.mean(x, axis=-1, keepdims=True)


CONV_CB = 512
CONV_TR = 512


def _shift_down(u, halo, j):
    if j == 0:
        return u
    ru = pltpu.roll(u, j, 0)
    row8 = lax.broadcasted_iota(jnp.int32, halo.shape, 0)
    top = jnp.where(row8 < j, pltpu.roll(halo, j, 0), ru[:SUBLANES])
    return jnp.concatenate([top, ru[SUBLANES:]], axis=0)


def _shift_up(d, halo, j):
    if j == 0:
        return d
    tr = d.shape[0]
    rd = pltpu.roll(d, tr - j, 0)
    row8 = lax.broadcasted_iota(jnp.int32, halo.shape, 0)
    bot = jnp.where(row8 >= SUBLANES - j, pltpu.roll(halo, SUBLANES - j, 0), rd[tr - SUBLANES:])
    return jnp.concatenate([rd[:tr - SUBLANES], bot], axis=0)


def _conv_col(cb):
    return jnp.where(cb < 2, AL_XS // CONV_CB + cb, AL_B // CONV_CB)


def _conv_specs(s, tr):
    per8 = tr // SUBLANES
    blk = pl.BlockSpec((tr, CONV_CB), lambda cb, i: (i, _conv_col(cb)))
    prev = pl.BlockSpec((SUBLANES, CONV_CB), lambda cb, i: (jnp.maximum(i * per8 - 1, 0), _conv_col(cb)))
    return blk, prev


def _conv_pre(u, halo, w_ref, b_ref, first):
    halo = jnp.where(first, 0.0, halo)
    acc = b_ref[...] + w_ref[CONV_W - 1:CONV_W, :] * u
    shifted = [u]
    for j in range(1, CONV_W):
        sh = _shift_down(u, halo, j)
        shifted.append(sh)
        acc = acc + w_ref[CONV_W - 1 - j:CONV_W - j, :] * sh
    return acc, shifted


def _conv_fwd(proj, conv_w, conv_b):
    s = proj.shape[0]
    tr = min(CONV_TR, s)
    blk, prev = _conv_specs(s, tr)

    def body(u_ref, h_ref, w_ref, b_ref, o_ref):
        pre, _ = _conv_pre(u_ref[...], h_ref[...], w_ref, b_ref, pl.program_id(1) == 0)
        o_ref[...] = _silu(pre)

    return pl.pallas_call(
        body, name="conv_fwd", grid=(CONV_DIM // CONV_CB, s // tr),
        in_specs=[blk, prev, pl.BlockSpec((CONV_W, CONV_CB), lambda cb, i: (0, cb)), pl.BlockSpec((1, CONV_CB), lambda cb, i: (0, cb))],
        out_specs=pl.BlockSpec((tr, CONV_CB), lambda cb, i: (i, cb)),
        out_shape=jax.ShapeDtypeStruct((s, CONV_DIM), F32),
        compiler_params=_params(("parallel", "parallel")),
    )(proj, proj, conv_w, conv_b)


def _conv_bwd_pre(proj, conv_w, conv_b, dxc):
    s = proj.shape[0]
    tr = min(CONV_TR, s)
    blk, prev = _conv_specs(s, tr)

    def body(u_ref, h_ref, w_ref, b_ref, d_ref, dpre_ref, dw_ref, db_ref):
        i = pl.program_id(1)
        pre, shifted = _conv_pre(u_ref[...], h_ref[...], w_ref, b_ref, i == 0)
        sg = _sigmoid(pre)
        dpre = d_ref[...] * (sg * (1.0 + pre * (1.0 - sg)))
        dpre_ref[...] = dpre

        @pl.when(i == 0)
        def _():
            dw_ref[...] = jnp.zeros_like(dw_ref)
            db_ref[...] = jnp.zeros_like(db_ref)

        db_ref[...] += _colsum(dpre)
        for j in range(CONV_W):
            dw_ref[CONV_W - 1 - j:CONV_W - j, :] += _colsum(dpre * shifted[j])

    own = pl.BlockSpec((tr, CONV_CB), lambda cb, i: (i, cb))
    wspec = pl.BlockSpec((CONV_W, CONV_CB), lambda cb, i: (0, cb))
    bspec = pl.BlockSpec((1, CONV_CB), lambda cb, i: (0, cb))
    return pl.pallas_call(
        body, name="conv_bwd_pre", grid=(CONV_DIM // CONV_CB, s // tr),
        in_specs=[blk, prev, wspec, bspec, own], out_specs=[own, wspec, bspec],
        out_shape=[jax.ShapeDtypeStruct((s, CONV_DIM), F32), jax.ShapeDtypeStruct((CONV_W, CONV_DIM), F32),
                   jax.ShapeDtypeStruct((1, CONV_DIM), F32)],
        compiler_params=_params(("parallel", "arbitrary")),
    )(proj, proj, conv_w, conv_b, dxc)


def _conv_bwd_in(dpre, conv_w, dproj):
    s = dpre.shape[0]
    tr = min(CONV_TR, s)
    per8 = tr // SUBLANES
    last8 = s // SUBLANES - 1
    nb = s // tr

    def body(d_ref, n_ref, w_ref, _, o_ref):
        d = d_ref[...]
        halo = jnp.where(pl.program_id(1) == nb - 1, 0.0, n_ref[...])
        acc = w_ref[CONV_W - 1:CONV_W, :] * d
        for j in range(1, CONV_W):
            acc = acc + w_ref[CONV_W - 1 - j:CONV_W - j, :] * _shift_up(d, halo, j)
        o_ref[...] = acc.astype(o_ref.dtype)

    own = pl.BlockSpec((tr, CONV_CB), lambda cb, i: (i, cb))
    nxt = pl.BlockSpec((SUBLANES, CONV_CB), lambda cb, i: (jnp.minimum((i + 1) * per8, last8), cb))
    return pl.pallas_call(
        body, name="conv_bwd_in", grid=(CONV_DIM // CONV_CB, nb),
        in_specs=[own, nxt, pl.BlockSpec((CONV_W, CONV_CB), lambda cb, i: (0, cb)), ANY],
        out_specs=pl.BlockSpec((tr, CONV_CB), lambda cb, i: (i, _conv_col(cb))),
        out_shape=jax.ShapeDtypeStruct(dproj.shape, dproj.dtype), input_output_aliases={3: 0},
        compiler_params=_params(("parallel", "parallel")),
    )(dpre, dpre, conv_w, dproj)


XC_B, XC_C = 1024, 1280


def _tile_iotas():
    row = lax.broadcasted_iota(jnp.int32, (CHUNK, LANES), 0)
    lane = lax.broadcasted_iota(jnp.int32, (CHUNK, LANES), 1)
    return row, lane


def _ssd_scalars(dtf_ref, bias_ref, alog_ref, row, lane):
    head = lane[:1] < N_HEADS
    raw = dtf_ref[...] + bias_ref[...]
    dt = _softplus(raw)
    a_neg = jnp.where(head, -jnp.exp(alog_ref[...]), 0.0)
    a = dt * a_neg
    tril = (row >= lane).astype(BF16)
    s = _dot3(tril, a)
    return raw, dt, a_neg, s


def _pair(v, j, lo):
    return jnp.where(lo, v[:, 2 * j:2 * j + 1], v[:, 2 * j + 1:2 * j + 2])


def _head_sum(x, lo, hh):
    return jnp.sum(jnp.where(lo == (hh == 0), x, 0.0), axis=1, keepdims=True)


def _decay_masks(s, h, row, lane):
    s_col = jnp.broadcast_to(s[:, h:h + 1], (CHUNK, LANES))
    s_row = s_col.T
    lm = jnp.where(row >= lane, jnp.exp(s_col - s_row), 0.0)
    lmt = jnp.where(row <= lane, jnp.exp(s_row - s_col), 0.0)
    return lm, lmt


def _ssd_fwd(xc_all, proj, dt_bias_l, a_log_l, d_exp):
    s_len = xc_all.shape[0]
    nc = s_len // CHUNK

    def body(x_ref, dtf_ref, bias_ref, alog_ref, dexp_ref, y_ref, prevs_ref, state_ref):
        @pl.when(pl.program_id(0) == 0)
        def _():
            state_ref[...] = jnp.zeros_like(state_ref)

        row, lane = _tile_iotas()
        lo = lane < HEAD_DIM
        _, dt, _, s = _ssd_scalars(dtf_ref, bias_ref, alog_ref, row, lane)
        tot = s[CHUNK - 1:CHUNK, :]
        for g in range(SSM_GROUPS):
            bg = x_ref[:, XC_B + g * SSM_STATE:XC_B + (g + 1) * SSM_STATE].astype(BF16)
            cg = x_ref[:, XC_C + g * SSM_STATE:XC_C + (g + 1) * SSM_STATE].astype(BF16)
            cb = _dot(cg, bg, NT)
            for j in range(g * 4, g * 4 + 4):
                xs_p = x_ref[:, j * LANES:(j + 1) * LANES]
                dt_p, s_p, tot_p = _pair(dt, j, lo), _pair(s, j, lo), _pair(tot, j, lo[:1])
                xc_p = xs_p * dt_p
                xc_b = xc_p.astype(BF16)
                yd = []
                for hh in range(2):
                    lm, _ = _decay_masks(s, 2 * j + hh, row, lane)
                    yd.append(_dot((cb * lm).astype(BF16), xc_b))
                prev = state_ref[j]
                prevs_ref[0, j] = prev
                yo = _dot(cg, prev.astype(BF16)) * jnp.exp(s_p)
                y_ref[:, j * LANES:(j + 1) * LANES] = jnp.where(lo, yd[0], yd[1]) + yo + dexp_ref[:, j * LANES:(j + 1) * LANES] * xs_p
                to_end = jnp.exp(tot_p - s_p)
                state_ref[j] = jnp.exp(tot_p) * prev + _dot(bg, (xc_p * to_end).astype(BF16), TN)

    vec = lambda w: pl.BlockSpec((1, w), lambda c: (0, 0))
    return pl.pallas_call(
        body, name="ssd_fwd", grid=(nc,),
        in_specs=[pl.BlockSpec((CHUNK, CONV_DIM), lambda c: (c, 0)), pl.BlockSpec((CHUNK, LANES), lambda c: (c, AL_DTF // LANES)),
                  vec(LANES), vec(LANES), vec(D_MODEL)],
        out_specs=[pl.BlockSpec((CHUNK, D_MODEL), lambda c: (c, 0)), pl.BlockSpec((1, N_PAIRS, SSM_STATE, LANES), lambda c: (c, 0, 0, 0))],
        out_shape=[jax.ShapeDtypeStruct((s_len, D_MODEL), F32), jax.ShapeDtypeStruct((nc, N_PAIRS, SSM_STATE, LANES), F32)],
        scratch_shapes=[pltpu.VMEM((N_PAIRS, SSM_STATE, LANES), F32)],
        compiler_params=_params(("arbitrary",)),
    )(xc_all, proj, dt_bias_l, a_log_l, d_exp)


def _ssd_bwd(xc_all, proj, dt_bias_l, a_log_l, d_exp, prevs, dy):
    s_len = xc_all.shape[0]
    nc = s_len // CHUNK

    def body(x_ref, dtf_ref, bias_ref, alog_ref, dexp_ref, prevs_ref, dy_ref, dx_ref, ddt_ref, da_ref, dd_ref, dbias_ref, dstate_ref):
        @pl.when(pl.program_id(0) == 0)
        def _():
            dstate_ref[...] = jnp.zeros_like(dstate_ref)
            da_ref[...] = jnp.zeros_like(da_ref)
            dd_ref[...] = jnp.zeros_like(dd_ref)
            dbias_ref[...] = jnp.zeros_like(dbias_ref)

        row, lane = _tile_iotas()
        lo = lane < HEAD_DIM
        last = row == CHUNK - 1
        raw, dt, a_neg, s = _ssd_scalars(dtf_ref, bias_ref, alog_ref, row, lane)
        tot = s[CHUNK - 1:CHUNK, :]
        ds_acc = jnp.zeros((CHUNK, LANES), F32)
        ddt_acc = jnp.zeros((CHUNK, LANES), F32)
        for g in range(SSM_GROUPS):
            bcol = slice(XC_B + g * SSM_STATE, XC_B + (g + 1) * SSM_STATE)
            ccol = slice(XC_C + g * SSM_STATE, XC_C + (g + 1) * SSM_STATE)
            bg = x_ref[:, bcol].astype(BF16)
            cg = x_ref[:, ccol].astype(BF16)
            cb = _dot(cg, bg, NT)
            cbt = _dot(bg, cg, NT)
            dcb = jnp.zeros((CHUNK, LANES), F32)
            dcbt = jnp.zeros((CHUNK, LANES), F32)
            db_acc = jnp.zeros((CHUNK, LANES), F32)
            dc_acc = jnp.zeros((CHUNK, LANES), F32)
            for j in range(g * 4, g * 4 + 4):
                cols = slice(j * LANES, (j + 1) * LANES)
                xs_p, dy_p = x_ref[:, cols], dy_ref[:, cols]
                dt_p, s_p, tot_p = _pair(dt, j, lo), _pair(s, j, lo), _pair(tot, j, lo[:1])
                xc_p = xs_p * dt_p
                xc_b, dy_b = xc_p.astype(BF16), dy_p.astype(BF16)
                e_p, f_p, etot_p = jnp.exp(s_p), jnp.exp(tot_p - s_p), jnp.exp(tot_p)
                prev, dnext = prevs_ref[0, j], dstate_ref[j]
                prev_b, dnext_b = prev.astype(BF16), dnext.astype(BF16)
                dd_ref[:, cols] += _colsum(dy_p * xs_p)
                dxs_p = dexp_ref[:, cols] * dy_p
                cp = _dot(cg, prev_b)
                gy = (dy_p * e_p).astype(BF16)
                dc_acc += _dot(gy, prev_b, NT)
                dstate_ref[j] = etot_p * dnext + _dot(cg, gy, TN)
                de = dy_p * cp * e_p
                bds = _dot(bg, dnext_b)
                db_acc += _dot((xc_p * f_p).astype(BF16), dnext_b, NT)
                dxc_p = bds * f_p
                df = bds * xc_p * f_p
                dtot_p = _colsum(dnext * prev) * etot_p + _colsum(df)
                dsl = de - df + jnp.where(last, dtot_p, 0.0)
                for hh in range(2):
                    h = 2 * j + hh
                    mine = lo == (hh == 0)
                    lm, lmt = _decay_masks(s, h, row, lane)
                    dy_h = jnp.where(mine, dy_p, 0.0).astype(BF16)
                    xc_h = jnp.where(mine, xc_p, 0.0).astype(BF16)
                    dm = _dot(dy_h, xc_b, NT)
                    dmt = _dot(xc_h, dy_b, NT)
                    mt = cbt * lmt
                    dxc_p += _dot(mt.astype(BF16), dy_h)
                    ds_h = (jnp.sum(dm * cb * lm, axis=1, keepdims=True) - jnp.sum(dmt * mt, axis=1, keepdims=True)
                            + _head_sum(dsl, lo, hh))
                    ds_acc += jnp.where(lane == h, ds_h, 0.0)
                    dcb += dm * lm
                    dcbt += dmt * lmt
                    ddt_acc += jnp.where(lane == h, _head_sum(dxc_p * xs_p, lo, hh), 0.0)
                dx_ref[:, cols] = dxs_p + dxc_p * dt_p
            dx_ref[:, ccol] = dc_acc + _dot(dcb.astype(BF16), bg)
            dx_ref[:, bcol] = db_acc + _dot(dcbt.astype(BF16), cg)
        triu = (row <= lane).astype(BF16)
        da = _dot3(triu, ds_acc)
        ddt = ddt_acc + da * a_neg
        da_ref[...] += _colsum(da * dt) * a_neg[:1]
        ddt_raw = jnp.where(lane < N_HEADS, ddt * _sigmoid(raw), 0.0)
        dbias_ref[...] += _colsum(ddt_raw)
        ddt_ref[...] = ddt_raw

    rev = lambda c: nc - 1 - c
    vec = lambda w: pl.BlockSpec((1, w), lambda c: (0, 0))
    return pl.pallas_call(
        body, name="ssd_bwd", grid=(nc,),
        in_specs=[pl.BlockSpec((CHUNK, CONV_DIM), lambda c: (rev(c), 0)), pl.BlockSpec((CHUNK, LANES), lambda c: (rev(c), AL_DTF // LANES)),
                  vec(LANES), vec(LANES), vec(D_MODEL),
                  pl.BlockSpec((1, N_PAIRS, SSM_STATE, LANES), lambda c: (rev(c), 0, 0, 0)),
                  pl.BlockSpec((CHUNK, D_MODEL), lambda c: (rev(c), 0))],
        out_specs=[pl.BlockSpec((CHUNK, CONV_DIM), lambda c: (rev(c), 0)), pl.BlockSpec((CHUNK, LANES), lambda c: (rev(c), 0)),
                   vec(LANES), vec(D_MODEL), vec(LANES)],
        out_shape=[jax.ShapeDtypeStruct((s_len, CONV_DIM), F32), jax.ShapeDtypeStruct((s_len, LANES), F32),
                   jax.ShapeDtypeStruct((1, LANES), F32), jax.ShapeDtypeStruct((1, D_MODEL), F32), jax.ShapeDtypeStruct((1, LANES), F32)],
        scratch_shapes=[pltpu.VMEM((N_PAIRS, SSM_STATE, LANES), F32)],
        compiler_params=_params(("arbitrary",)),
    )(xc_all, proj, dt_bias_l, a_log_l, d_exp, prevs, dy)


AUG_C, AUG_ONE = 64, 67
NEG = -1e30
ATT_T = 512


def _fox_cum(proj, f_bias_l):
    s_len = proj.shape[0]
    nc = s_len // CHUNK

    def body(dtf_ref, fb_ref, cum_ref):
        row, lane = _tile_iotas()
        tril = (row >= lane).astype(BF16)

        def step(c, carry):
            rows = pl.ds(pl.multiple_of(c * CHUNK, CHUNK), CHUNK)
            lf = -_softplus(-(dtf_ref[rows, :] + fb_ref[...]))
            lf = jnp.where(lane < N_HEADS, pltpu.roll(lf, LANES - F_LANE, 1), 0.0)
            cs = _dot3(tril, lf) + carry
            cum_ref[rows, :] = cs
            return cs[CHUNK - 1:CHUNK, :]

        lax.fori_loop(0, nc, step, jnp.zeros((1, LANES), F32))

    return pl.pallas_call(
        body, name="fox_cum", grid=(1,),
        in_specs=[pl.BlockSpec((s_len, LANES), lambda i: (0, AL_DTF // LANES)), pl.BlockSpec((1, LANES), lambda i: (0, 0))],
        out_specs=pl.BlockSpec((s_len, LANES), lambda i: (0, 0)),
        out_shape=jax.ShapeDtypeStruct((s_len, LANES), F32),
        compiler_params=_params(("arbitrary",)),
    )(proj, f_bias_l)


def _fox_cum_bwd(dcum, proj, f_bias_l, ddt_tile, dproj):
    s_len = proj.shape[0]
    nc = s_len // CHUNK

    def body(dcum_ref, dtf_ref, fb_ref, ddt_ref, _, out_ref, dfb_ref):
        row, lane = _tile_iotas()
        triu = (row <= lane).astype(BF16)
        is_f = (lane >= F_LANE) & (lane < F_LANE + N_HEADS)

        def step(t, carry):
            run, dfb = carry
            rows = pl.ds(pl.multiple_of((nc - 1 - t) * CHUNK, CHUNK), CHUNK)
            rc = _dot3(triu, dcum_ref[rows, :]) + run
            sg = _sigmoid(-(dtf_ref[rows, :] + fb_ref[...]))
            df = jnp.where(is_f, pltpu.roll(rc, F_LANE, 1) * sg, 0.0)
            out_ref[rows, :] = (df + ddt_ref[rows, :]).astype(out_ref.dtype)
            return rc[0:1, :], dfb + _colsum(df)

        _, dfb = lax.fori_loop(0, nc, step, (jnp.zeros((1, LANES), F32), jnp.zeros((1, LANES), F32)))
        dfb_ref[...] = dfb

    whole = pl.BlockSpec((s_len, LANES), lambda i: (0, 0))
    dtf_cols = pl.BlockSpec((s_len, LANES), lambda i: (0, AL_DTF // LANES))
    vec = pl.BlockSpec((1, LANES), lambda i: (0, 0))
    return pl.pallas_call(
        body, name="fox_cum_bwd", grid=(1,),
        in_specs=[whole, dtf_cols, vec, whole, ANY], out_specs=[dtf_cols, vec],
        out_shape=[jax.ShapeDtypeStruct(dproj.shape, dproj.dtype), jax.ShapeDtypeStruct((1, LANES), F32)],
        input_output_aliases={4: 0}, compiler_params=_params(("arbitrary",)),
    )(dcum, proj, f_bias_l, ddt_tile, dproj)


def _attn_prep(proj, cum):
    s_len = proj.shape[0]
    tr = min(512, s_len)

    def body(q_ref, k_ref, v_ref, cum_ref, qa_ref, ka_ref, vb_ref):
        p = pl.program_id(0)
        lane = lax.broadcasted_iota(jnp.int32, (tr, LANES), 1)
        lo = lane < HEAD_DIM
        c = cum_ref[...]
        c1 = c.astype(BF16).astype(F32)
        r = c - c1
        c2 = r.astype(BF16).astype(F32)
        c3 = (r - c2).astype(BF16).astype(F32)
        q, k = q_ref[...] * (HEAD_DIM ** -0.5), k_ref[...]
        for hh in range(2):
            col = lambda x: jnp.sum(jnp.where(lane == 2 * p + hh, x, 0.0), axis=1, keepdims=True)
            a1, a2, a3 = col(c1), col(c2), col(c3)
            qh = q if hh == 0 else pltpu.roll(q, HEAD_DIM, 1)
            kh = k if hh == 0 else pltpu.roll(k, HEAD_DIM, 1)
            q_aug = jnp.where(lane == AUG_C, a1, jnp.where(lane == AUG_C + 1, a2, jnp.where(lane == AUG_C + 2, a3,
                              jnp.where(lane < AUG_ONE + 3, 1.0, 0.0))))
            k_aug = jnp.where(lane < AUG_ONE, 1.0, jnp.where(lane == AUG_ONE, -a1, jnp.where(lane == AUG_ONE + 1, -a2,
                              jnp.where(lane == AUG_ONE + 2, -a3, 0.0))))
            qa_ref[hh] = jnp.where(lo, qh, q_aug).astype(BF16)
            ka_ref[hh] = jnp.where(lo, kh, k_aug).astype(BF16)
        vb_ref[...] = v_ref[...].astype(BF16)

    slab = lambda col0: pl.BlockSpec((tr, LANES), lambda p, i: (i, col0 // LANES + p))
    heads = pl.BlockSpec((2, tr, LANES), lambda p, i: (p, i, 0))
    return pl.pallas_call(
        body, name="attn_prep", grid=(N_PAIRS, s_len // tr),
        in_specs=[slab(AL_Q), slab(AL_K), slab(AL_V), pl.BlockSpec((tr, LANES), lambda p, i: (i, 0))],
        out_specs=[heads, heads, pl.BlockSpec((tr, LANES), lambda p, i: (i, p))],
        out_shape=[jax.ShapeDtypeStruct((N_HEADS, s_len, LANES), BF16), jax.ShapeDtypeStruct((N_HEADS, s_len, LANES), BF16),
                   jax.ShapeDtypeStruct((s_len, D_MODEL), BF16)],
        compiler_params=_params(("parallel", "parallel")),
    )(proj, proj, proj, cum)


def _attn_fwd(qa, ka, vb, halves):
    s_len = vb.shape[0]
    t = min(ATT_T, s_len)
    nq = s_len // t
    n = len(halves)

    def body(qa_ref, ka_ref, vb_ref, *rest):
        o_ref, lse_ref = rest[n:n + 2]
        start, finish = _gather_plan(rest[:n], rest[n + 2:2 * n + 2], *rest[2 * n + 2:])
        i = pl.program_id(1)
        pl.when((pl.program_id(0) == 0) & (i == 0))(start)
        row = lax.broadcasted_iota(jnp.int32, (t, t), 0)
        col = lax.broadcasted_iota(jnp.int32, (t, t), 1)
        lo = lax.broadcasted_iota(jnp.int32, (t, LANES), 1) < HEAD_DIM
        qs = (qa_ref[0], qa_ref[1])

        def block(j, carry, masked):
            rows = pl.ds(pl.multiple_of(j * t, t), t)
            v = vb_ref[rows, :]
            new = []
            for hh in range(2):
                m, l, acc = carry[hh]
                s = _dot(qs[hh], ka_ref[hh, rows, :], NT)
                if masked:
                    s = jnp.where(row >= col, s, NEG)
                m_new = jnp.maximum(m, jnp.max(s, axis=1, keepdims=True))
                alpha = jnp.exp(m - m_new)
                p = jnp.exp(s - m_new)
                new.append((m_new, alpha * l + jnp.sum(p, axis=1, keepdims=True), alpha * acc + _dot(p.astype(BF16), v)))
            return tuple(new)

        init = (jnp.full((t, 1), NEG, F32), jnp.zeros((t, 1), F32), jnp.zeros((t, LANES), F32))
        carry = lax.fori_loop(0, i, functools.partial(block, masked=False), (init, init))
        (m0, l0, acc0), (m1, l1, acc1) = block(i, carry, True)
        o_ref[...] = jnp.where(lo, acc0 / l0, acc1 / l1)
        lse_ref[...] = jnp.where(lo, m0 + jnp.log(l0), m1 + jnp.log(l1))
        pl.when((pl.program_id(0) == N_PAIRS - 1) & (i == nq - 1))(finish)

    out = pl.BlockSpec((t, LANES), lambda p, i: (i, p))
    res = pl.pallas_call(
        body, name="attn_fwd", grid=(N_PAIRS, nq),
        in_specs=[pl.BlockSpec((2, t, LANES), lambda p, i: (p, i, 0)), pl.BlockSpec((2, s_len, LANES), lambda p, i: (p, 0, 0)),
                  pl.BlockSpec((s_len, LANES), lambda p, i: (0, p))] + [ANY] * n,
        out_specs=[out, out] + [ANY] * n,
        out_shape=[jax.ShapeDtypeStruct((s_len, D_MODEL), F32), jax.ShapeDtypeStruct((s_len, D_MODEL), F32)]
        + [jax.ShapeDtypeStruct((N_CHIPS, *h.shape), h.dtype) for h in halves],
        scratch_shapes=_exchange_sems(n),
        compiler_params=_params(("arbitrary", "arbitrary")),
    )(qa, ka, vb, *halves)
    return res[0], res[1], res[2:]


def _attn_bwd(qa, ka, vb, o, lse, do, parts, dproj):
    s_len = vb.shape[0]
    t = min(ATT_T, s_len)
    nq = s_len // t
    n = len(parts)

    def body(qa_ref, ka_ref, vb_ref, o_ref, lse_ref, do_ref, *rest):
        dqa_ref, dka_ref, dv_ref = rest[n + 1:n + 4]
        start, finish = _reduce_plan(rest[:n], rest[n + 4:2 * n + 4], *rest[2 * n + 4:])
        j = pl.program_id(1)
        pl.when((pl.program_id(0) == 0) & (j == 0))(start)

        @pl.when(j == 0)
        def _():
            dqa_ref[...] = jnp.zeros_like(dqa_ref)

        row = lax.broadcasted_iota(jnp.int32, (t, t), 0)
        col = lax.broadcasted_iota(jnp.int32, (t, t), 1)
        lo = lax.broadcasted_iota(jnp.int32, (t, LANES), 1) < HEAD_DIM
        v = vb_ref[...]
        ks = (ka_ref[0], ka_ref[1])

        def block(i, carry, masked):
            dk, dv = list(carry[:2]), carry[2]
            rows = pl.ds(pl.multiple_of(i * t, t), t)
            do_p, o_p, lse_p = do_ref[rows, :], o_ref[rows, :], lse_ref[rows, :]
            for hh in range(2):
                q = qa_ref[hh, rows, :]
                do_h = jnp.where(lo == (hh == 0), do_p, 0.0)
                delta = jnp.sum(do_h * o_p, axis=1, keepdims=True)
                s = _dot(q, ks[hh], NT)
                if masked:
                    s = jnp.where(row >= col, s, NEG)
                p = jnp.exp(s - lse_p[:, hh * HEAD_DIM:hh * HEAD_DIM + 1])
                do_b = do_h.astype(BF16)
                ds = (p * (_dot(do_b, v, NT) - delta)).astype(BF16)
                dv = dv + _dot(p.astype(BF16), do_b, TN)
                dk[hh] = dk[hh] + _dot(ds, q, TN)
                dqa_ref[hh, rows, :] += _dot(ds, ks[hh])
            return dk[0], dk[1], dv

        zero = jnp.zeros((t, LANES), F32)
        carry = block(j, (zero, zero, zero), True)
        dk0, dk1, dv = lax.fori_loop(j + 1, nq, functools.partial(block, masked=False), carry)
        dka_ref[0] = dk0
        dka_ref[1] = dk1
        dv_ref[...] = dv.astype(dv_ref.dtype)
        pl.when((pl.program_id(0) == N_PAIRS - 1) & (j == nq - 1))(finish)

    whole_pair = pl.BlockSpec((2, s_len, LANES), lambda p, j: (p, 0, 0))
    blk_pair = pl.BlockSpec((2, t, LANES), lambda p, j: (p, j, 0))
    whole_cols = pl.BlockSpec((s_len, LANES), lambda p, j: (0, p))
    blk_cols = pl.BlockSpec((t, LANES), lambda p, j: (j, p))
    res = pl.pallas_call(
        body, name="attn_bwd", grid=(N_PAIRS, nq),
        in_specs=[whole_pair, blk_pair, blk_cols, whole_cols, whole_cols, whole_cols] + [ANY] * (n + 1),
        out_specs=[whole_pair, blk_pair, pl.BlockSpec((t, LANES), lambda p, j: (j, AL_V // LANES + p))] + [ANY] * n,
        out_shape=[jax.ShapeDtypeStruct((N_HEADS, s_len, LANES), F32), jax.ShapeDtypeStruct((N_HEADS, s_len, LANES), F32),
                   jax.ShapeDtypeStruct(dproj.shape, dproj.dtype)]
        + [jax.ShapeDtypeStruct((N_DEV, g.shape[1] // 2, g.shape[2]), g.dtype) for g in parts],
        scratch_shapes=_exchange_sems(n), input_output_aliases={6 + n: 2},
        compiler_params=_params(("arbitrary", "arbitrary")),
    )(qa, ka, vb, o, lse, do, *parts, dproj)
    return res[0], res[1], res[2], res[3:]


def _attn_post(dqa, dka, dproj):
    s_len = dqa.shape[1]
    tr = min(256, s_len)
    assert AL_K == AL_Q + D_MODEL and AL_Q % (2 * D_MODEL) == 0

    def body(dqa_ref, dka_ref, _, dqk_ref, dcum_ref):
        lane = lax.broadcasted_iota(jnp.int32, (tr, LANES), 1)
        lo = lane < HEAD_DIM
        dcum = jnp.zeros((tr, LANES), F32)
        for p in range(N_PAIRS):
            a0, a1, b0, b1 = dqa_ref[2 * p], dqa_ref[2 * p + 1], dka_ref[2 * p], dka_ref[2 * p + 1]
            dq = jnp.where(lo, a0, pltpu.roll(a1, HEAD_DIM, 1)) * (HEAD_DIM ** -0.5)
            dqk_ref[:, p * LANES:(p + 1) * LANES] = dq.astype(dqk_ref.dtype)
            dqk_ref[:, D_MODEL + p * LANES:D_MODEL + (p + 1) * LANES] = jnp.where(lo, b0, pltpu.roll(b1, HEAD_DIM, 1)).astype(dqk_ref.dtype)
            for hh, (a, b) in enumerate(((a0, b0), (a1, b1))):
                dcum = dcum + jnp.where(lane == 2 * p + hh, a[:, AUG_C:AUG_C + 1] - b[:, AUG_ONE:AUG_ONE + 1], 0.0)
        dcum_ref[...] = dcum

    heads = pl.BlockSpec((N_HEADS, tr, LANES), lambda i: (0, i, 0))
    return pl.pallas_call(
        body, name="attn_post", grid=(s_len // tr,),
        in_specs=[heads, heads, ANY],
        out_specs=[pl.BlockSpec((tr, 2 * D_MODEL), lambda i: (i, AL_Q // (2 * D_MODEL))), pl.BlockSpec((tr, LANES), lambda i: (i, 0))],
        out_shape=[jax.ShapeDtypeStruct(dproj.shape, dproj.dtype), jax.ShapeDtypeStruct((s_len, LANES), F32)],
        input_output_aliases={2: 0}, compiler_params=_params(("parallel",)),
    )(dqa, dka, dproj)


def _ln_stats(r):
    mu = _rowmean(r)
    xc = r - mu
    rstd = lax.rsqrt(_rowmean(xc * xc) + LN_EPS)
    return xc * rstd, rstd


def _ln_bwd(dxh, xh, rstd):
    return rstd * (dxh - _rowmean(dxh) - xh * _rowmean(dxh * xh))


def _rms_bwd(dgn, g, r):
    return r * dgn - (r * r * r) * g * _rowmean(dgn * g)


def _to_aligned(w):
    pad = jnp.zeros((w.shape[0], AL_COLS - IN_COLS), w.dtype)
    return jnp.concatenate([w[:, :2048], w[:, 2576:5648], w[:, 2048:2560], w[:, 2560:2576], w[:, 5648:5664], pad], axis=1)


def _from_aligned(g):
    return jnp.concatenate([g[:, :AL_Q], g[:, AL_B:AL_DTF], g[:, AL_DTF:AL_DTF + 16], g[:, AL_Q:AL_B], g[:, AL_DTF + 16:AL_DTF + 32]], axis=1)


def _lanes(v, at=0):
    return jnp.pad(v, ((0, 0), (at, LANES - at - v.shape[1])))


def _local_step(x, tgt, mod, w_al, halves, sp):
    d = D_MODEL
    sh1, sc1, g1, sh2, sc2, g2 = [mod[:, i * d:(i + 1) * d] for i in range(6)]
    dt_bias_l, a_log_l, f_bias_l = _lanes(sp["dt_bias"]), _lanes(sp["a_log"]), _lanes(sp["f_bias"], F_LANE)
    d_exp = jnp.repeat(sp["d_skip"], HEAD_DIM, axis=1)
    z_slab = lambda a: (a, d, AL_Z // d)

    (h1,), _ = _rowwise("mod1", lambda x, sc, sh: ([x * (1.0 + sc) + sh], []), [x], [sc1, sh1], [(d, BF16)], [])
    proj = _matmul("proj", h1, w_al, tn=1152)
    xc_all = _conv_fwd(proj, sp["conv_w"], sp["conv_b"])
    y_ssd, prevs = _ssd_fwd(xc_all, proj, dt_bias_l, a_log_l, d_exp)

    def gated_norm(y, z, w):
        g = y * _silu(z)
        return [g * lax.rsqrt(_rowmean(g * g) + RMS_EPS) * w], []

    (y_mix,), _ = _rowwise("ssm_norm", gated_norm, [y_ssd, z_slab(proj)], [sp["ssm_norm_w"]], [(d, BF16, (2 * d, 0, None))], [])
    cum = _fox_cum(proj, f_bias_l)
    qa, ka, vb = _attn_prep(proj, cum)
    o, lse, (g_out, g_fi, g_fo) = _attn_fwd(qa, ka, vb, halves)
    w_out = g_out.reshape(2 * d, d)
    w_fi = g_fi.transpose(1, 0, 2).reshape(d, D_FF)
    w_fo = g_fo.reshape(D_FF, d)
    (y_mix,), _ = _rowwise("attn_norm", lambda o, w: ([o * lax.rsqrt(_rowmean(o * o) + RMS_EPS) * w], []),
                           [o], [sp["attn_norm_w"]], [(d, BF16, (2 * d, 1, y_mix))], [])
    def ln1_fwd(y, x, g1, sc2, sh2, lg, lb):
        r1 = ALPHA * x + (1.0 + g1) * y
        xh, _ = _ln_stats(r1)
        x1 = xh * lg + lb
        return [y, r1, x1 * (1.0 + sc2) + sh2], []

    y, r1, h2 = _matmul("out_proj", y_mix, w_out, tm=512, tk=2048,
                        epi=(ln1_fwd, [x], [g1, sc2, sh2, sp["ln1_g"], sp["ln1_b"]], [F32, F32, BF16], []))
    u, act = _matmul("ff_in", h2, w_fi, epi=(lambda u: ([u, jnp.square(jnp.maximum(u, 0.0))], []), [], [], [F32, BF16], []))

    def head(ff, r1, tgt, g2, l1g, l1b, l2g, l2b):
        xh1, _ = _ln_stats(r1)
        x1 = xh1 * l1g + l1b
        xh2, rstd2 = _ln_stats(ALPHA * x1 + (1.0 + g2) * ff)
        err = xh2 * l2g + l2b - tgt
        loss = 0.5 * jnp.sum(_rowmean(err * err))
        dx2 = err * (1.0 / d)
        dr2 = _ln_bwd(dx2 * l2g, xh2, rstd2)
        return ([dr2, (1.0 + g2) * dr2],
                [_colsum(dx2 * xh2), _colsum(dx2), _colsum(dr2 * ff), jnp.full((1, LANES), loss, F32)])

    dr2, dff, d_ln2_g, d_ln2_b, d_g2, loss = _matmul(
        "ff_out", act, w_fo, tm=512, tk=2048,
        epi=(head, [r1, tgt], [g2, sp["ln1_g"], sp["ln1_b"], sp["ln2_g"], sp["ln2_b"]], [F32, BF16], [d, d, d, LANES]))
    du = _matmul("d_act", dff, w_fo, dims=NT, epi=(lambda da, u: ([da * (2.0 * jnp.maximum(u, 0.0))], []), [u], [], [BF16], []))
    dw_fo = _matmul("dw_ff_out", act, dff, dims=TN, out_dtype=BF16, by_chip="rows")
    dw_fi = _matmul("dw_ff_in", h2, du, dims=TN, out_dtype=BF16, by_chip="cols")

    def ln1_bwd(dh2, r1, dr2, y, sc2, g1, lg, lb):
        xh, rstd = _ln_stats(r1)
        x1 = xh * lg + lb
        dx1 = ALPHA * dr2 + dh2 * (1.0 + sc2)
        dr1 = _ln_bwd(dx1 * lg, xh, rstd)
        return ([dr1, (1.0 + g1) * dr1],
                [_colsum(dh2 * x1), _colsum(dh2), _colsum(dx1 * xh), _colsum(dx1), _colsum(dr1 * y)])

    dr1, dy, d_sc2, d_sh2, d_ln1_g, d_ln1_b, d_g1 = _matmul(
        "dh2", du, w_fi, dims=NT, tm=512, tk=2048,
        epi=(ln1_bwd, [r1, dr2, y], [sc2, g1, sp["ln1_g"], sp["ln1_b"]], [F32, BF16], [d] * 5))
    dymix = _matmul("dy_mix", dy, w_out, dims=NT)
    dw_out = _matmul("dw_out", y_mix, dy, dims=TN, out_dtype=BF16, by_chip="rows")

    def attn_norm_bwd(o, dyo, w):
        r = lax.rsqrt(_rowmean(o * o) + RMS_EPS)
        return [_rms_bwd(dyo * w, o, r)], [_colsum(dyo * o * r)]

    (do,), (d_attn_w,) = _rowwise("attn_norm_bwd", attn_norm_bwd, [o, (dymix, d, 1)], [sp["attn_norm_w"]], [(d, F32)], [d])

    def gated_norm_bwd(y, z, dyo, w):
        sg = _sigmoid(z)
        sz = z * sg
        g = y * sz
        r = lax.rsqrt(_rowmean(g * g) + RMS_EPS)
        dg = _rms_bwd(dyo * w, g, r)
        return [dg * sz, dg * y * (sg * (1.0 + z * (1.0 - sg)))], [_colsum(dyo * g * r)]

    (dy_ssd, dproj), (d_ssm_w,) = _rowwise("ssm_norm_bwd", gated_norm_bwd, [y_ssd, z_slab(proj), (dymix, d, 0)],
                                           [sp["ssm_norm_w"]], [(d, F32), (d, BF16, (AL_COLS, AL_Z // d, None))], [d])
    dqa, dka, dproj, landed = _attn_bwd(qa, ka, vb, o, lse, do, [dw_out, dw_fi, dw_fo], dproj)
    dproj, dcum = _attn_post(dqa, dka, dproj)
    dxc, ddt_tile, d_alog_l, d_dexp, d_dtb_l = _ssd_bwd(xc_all, proj, dt_bias_l, a_log_l, d_exp, prevs, dy_ssd)
    dproj, d_fb_l = _fox_cum_bwd(dcum, proj, f_bias_l, ddt_tile, dproj)
    dpre, d_conv_w, d_conv_b = _conv_bwd_pre(proj, sp["conv_w"], sp["conv_b"], dxc)
    dproj = _conv_bwd_in(dpre, sp["conv_w"], dproj)
    dw_al = _matmul("dw_in", h1, dproj, dims=TN, tn=1152, out_dtype=BF16)
    part_in = _from_aligned(dw_al).reshape(d, N_CHIPS, IN_COLS // N_CHIPS).transpose(1, 0, 2)
    def last(dh1, x, dr1, sc1):
        return [ALPHA * dr1 + dh1 * (1.0 + sc1)], [_colsum(dh1 * x), _colsum(dh1)]

    chip_in = _pair_sum(part_in, _pair_exchange(part_in), lax.axis_index("c"))
    dx, d_sc1, d_sh1, landed_in = _matmul("dh1", dproj, w_al, dims=NT, tm=512, tk=1152, carry=[chip_in],
                                          epi=(last, [x, dr1], [sc1], [F32], [d, d]))

    small = {
        "mod": jnp.concatenate([d_sh1, d_sc1, d_g1, d_sh2, d_sc2, d_g2], axis=1),
        "conv_w": d_conv_w, "conv_b": d_conv_b,
        "dt_bias": d_dtb_l[:, :N_HEADS], "a_log": d_alog_l[:, :N_HEADS],
        "d_skip": jnp.sum(d_dexp.reshape(N_HEADS, HEAD_DIM), axis=1)[None, :],
        "ssm_norm_w": d_ssm_w, "f_bias": d_fb_l[:, F_LANE:F_LANE + N_HEADS], "attn_norm_w": d_attn_w,
        "ln1_g": d_ln1_g, "ln1_b": d_ln1_b, "ln2_g": d_ln2_g, "ln2_b": d_ln2_b, "loss": loss,
    }
    return dx, [landed_in, *landed], small


N_DEV = 8
N_CHIPS = 4
ANY = pl.BlockSpec(memory_space=pl.ANY)
VMEM_SPEC = pl.BlockSpec(memory_space=pltpu.VMEM)


def _place():
    x, y, c = lax.axis_index("x"), lax.axis_index("y"), lax.axis_index("c")
    return x, y, c


def _other_chips(x, y):
    return [(1 - x, y, 2 * (1 - x) + y), (x, 1 - y, 2 * x + 1 - y), (1 - x, 1 - y, 2 * (1 - x) + 1 - y)]


def _allgather_small(name, v):
    r, cdim = v.shape

    def body(v_ref, out_ref, send_sems, recv_sems, local_sem):
        x, y, c = _place()
        me = 4 * x + 2 * y + c
        mine = pltpu.make_async_copy(v_ref, out_ref.at[me], local_sem)
        mine.start()
        peers = []
        for rel in range(1, N_DEV):
            px = 1 - x if rel & 4 else x
            py = 1 - y if rel & 2 else y
            pc = 1 - c if rel & 1 else c
            peers.append((px, py, pc))

        def copy(rel, slot, to):
            return pltpu.make_async_remote_copy(src_ref=v_ref, dst_ref=out_ref.at[slot], send_sem=send_sems.at[rel],
                                                recv_sem=recv_sems.at[rel], device_id=to, device_id_type=MESH)

        sends = [copy(rel, me, peer) for rel, peer in enumerate(peers)]
        for cp in sends:
            cp.start()
        for rel, (px, py, pc) in enumerate(peers):
            copy(rel, 4 * px + 2 * py + pc, (x, y, c)).wait_recv()
        for cp in sends:
            cp.wait_send()
        mine.wait()

    return pl.pallas_call(
        body, name=name, out_shape=jax.ShapeDtypeStruct((N_DEV, r, cdim), v.dtype),
        in_specs=[VMEM_SPEC], out_specs=VMEM_SPEC,
        scratch_shapes=[pltpu.SemaphoreType.DMA((N_DEV - 1,)), pltpu.SemaphoreType.DMA((N_DEV - 1,)), pltpu.SemaphoreType.DMA],
    )(v)


def _gather_shards(shard):
    r, cdim = shard.shape
    rh = r // 2

    def body(in_ref, out_ref, stage, send_sems, recv_sems, local_sems):
        x, y, c = _place()
        k_me = 2 * x + y
        me, sibling = (x, y, c), (x, y, 1 - c)
        chips = _other_chips(x, y)

        def copy(idx, k, half, to, src=None):
            rows = out_ref.at[k, pl.ds(pl.multiple_of(half * rh, rh), rh), :]
            return pltpu.make_async_remote_copy(src_ref=rows if src is None else src, dst_ref=rows, send_sem=send_sems.at[idx],
                                                recv_sem=recv_sems.at[idx], device_id=to, device_id_type=MESH)

        mine = in_ref.at[pl.ds(pl.multiple_of(c * rh, rh), rh), :]
        sends = [copy(j, k_me, c, (cx, cy, c), src=mine) for j, (cx, cy, _) in enumerate(chips)]
        for cp in sends:
            cp.start()
        load = pltpu.make_async_copy(in_ref, stage, local_sems.at[0])
        load.start()
        load.wait()
        store = pltpu.make_async_copy(stage, out_ref.at[k_me], local_sems.at[1])
        store.start()
        for j, (_, _, kj) in enumerate(chips):
            copy(j, kj, c, me).wait_recv()
            fwd = copy(3 + j, kj, c, sibling)
            fwd.start()
            sends.append(fwd)
        for j, (_, _, kj) in enumerate(chips):
            copy(3 + j, kj, 1 - c, me).wait_recv()
        for cp in sends:
            cp.wait_send()
        store.wait()

    return pl.pallas_call(
        body, name="gather_w_in", out_shape=jax.ShapeDtypeStruct((N_CHIPS, r, cdim), shard.dtype),
        in_specs=[ANY], out_specs=ANY,
        scratch_shapes=[pltpu.VMEM((r, cdim), shard.dtype), pltpu.SemaphoreType.DMA((6,)), pltpu.SemaphoreType.DMA((6,)),
                        pltpu.SemaphoreType.DMA((2,))],
        compiler_params=_params(),
    )(shard)


def _peers(x, y, c):
    return [((1 - x) if rel & 4 else x, (1 - y) if rel & 2 else y, (1 - c) if rel & 1 else c) for rel in range(1, N_DEV)]


def _exchange_sems(n):
    return [pltpu.SemaphoreType.DMA((n, N_DEV - 1)), pltpu.SemaphoreType.DMA((n, N_DEV - 1)), pltpu.SemaphoreType.DMA((n,))]


def _gather_plan(ins, outs, send_sems, recv_sems, local_sems):
    x, y, c = _place()
    k_me = 2 * x + y
    peers = [(rel, p) for rel, p in enumerate(_peers(x, y, c)) if (rel + 1) & 6]

    def copy(w, rel, k, half, to, src=None):
        rh = ins[w].shape[0] // 2
        rows = outs[w].at[k, pl.ds(pl.multiple_of(half * rh, rh), rh), :]
        return pltpu.make_async_remote_copy(src_ref=rows if src is None else src, dst_ref=rows, send_sem=send_sems.at[w, rel],
                                            recv_sem=recv_sems.at[w, rel], device_id=to, device_id_type=MESH)

    def mine(w):
        rh = ins[w].shape[0] // 2
        return ins[w].at[pl.ds(pl.multiple_of(c * rh, rh), rh), :]

    n = len(ins)
    local = [pltpu.make_async_copy(ins[w], outs[w].at[k_me], local_sems.at[w]) for w in range(n)]
    sends = [copy(w, rel, k_me, c, peer, src=mine(w)) for w in range(n) for rel, peer in peers]

    def start():
        for cp in local + sends:
            cp.start()

    def finish():
        for w in range(n):
            for rel, (px, py, pc) in peers:
                copy(w, rel, 2 * px + py, pc, (x, y, c)).wait_recv()
        for cp in sends:
            cp.wait_send()
        for cp in local:
            cp.wait()

    return start, finish


def _reduce_plan(ins, outs, send_sems, recv_sems, local_sems):
    x, y, c = _place()
    me = 4 * x + 2 * y + c
    peers = _peers(x, y, c)

    def block(w, k, half):
        rh = ins[w].shape[1] // 2
        return ins[w].at[k, pl.ds(pl.multiple_of(half * rh, rh), rh), :]

    def copy(w, rel, src, slot, to):
        return pltpu.make_async_remote_copy(src_ref=src, dst_ref=outs[w].at[slot], send_sem=send_sems.at[w, rel],
                                            recv_sem=recv_sems.at[w, rel], device_id=to, device_id_type=MESH)

    n = len(ins)
    local = [pltpu.make_async_copy(block(w, 2 * x + y, c), outs[w].at[me], local_sems.at[w]) for w in range(n)]
    sends = [copy(w, rel, block(w, 2 * px + py, pc), me, (px, py, pc)) for w in range(n) for rel, (px, py, pc) in enumerate(peers)]

    def start():
        for cp in local + sends:
            cp.start()

    def finish():
        for w in range(n):
            for rel, (px, py, pc) in enumerate(peers):
                copy(w, rel, block(w, 2 * x + y, c), 4 * px + 2 * py + pc, (x, y, c)).wait_recv()
        for cp in sends:
            cp.wait_send()
        for cp in local:
            cp.wait()

    return start, finish


def _scatter_plan(ins, outs, send_sems, recv_sems, local_sems):
    x, y, c = _place()
    k_me = 2 * x + y
    chips = _other_chips(x, y)

    def copy(w, j, src_k, dst_k, to):
        return pltpu.make_async_remote_copy(src_ref=ins[w].at[src_k], dst_ref=outs[w].at[dst_k], send_sem=send_sems.at[w, j],
                                            recv_sem=recv_sems.at[w, j], device_id=to, device_id_type=MESH)

    n = len(ins)
    local = [pltpu.make_async_copy(ins[w].at[k_me], outs[w].at[k_me], local_sems.at[w]) for w in range(n)]
    sends = [copy(w, j, kj, k_me, (cx, cy, c)) for w in range(n) for j, (cx, cy, kj) in enumerate(chips)]

    def start():
        for cp in local + sends:
            cp.start()

    def finish():
        for w in range(n):
            for j, (_, _, kj) in enumerate(chips):
                copy(w, j, k_me, kj, (x, y, c)).wait_recv()
        for cp in sends:
            cp.wait_send()
        for cp in local:
            cp.wait()

    return start, finish


def _pair_exchange(g):
    _, r, cdim = g.shape
    rh = r // 2

    def body(g_ref, got_ref, send_sem, recv_sem):
        x, y, c = _place()
        cp = pltpu.make_async_remote_copy(src_ref=g_ref.at[:, pl.ds(pl.multiple_of((1 - c) * rh, rh), rh), :], dst_ref=got_ref,
                                          send_sem=send_sem, recv_sem=recv_sem, device_id=(x, y, 1 - c), device_id_type=MESH)
        cp.start()
        cp.wait_recv()
        cp.wait_send()

    return pl.pallas_call(
        body, name="pair_exchange", out_shape=jax.ShapeDtypeStruct((N_CHIPS, rh, cdim), g.dtype),
        in_specs=[ANY], out_specs=ANY, scratch_shapes=[pltpu.SemaphoreType.DMA, pltpu.SemaphoreType.DMA],
    )(g)


def _pair_sum(g, got, c):
    _, r, cdim = g.shape
    rh = r // 2
    tr = min(256, rh)
    nb = rh // tr

    def body(c_ref, g_ref, got_ref, o_ref):
        o_ref[...] = (g_ref[...].astype(F32) + got_ref[...].astype(F32)).astype(o_ref.dtype)

    blk = pl.BlockSpec((1, tr, cdim), lambda k, i, c_ref: (k, i, 0))
    return pl.pallas_call(
        body, name="pair_sum",
        grid_spec=pltpu.PrefetchScalarGridSpec(
            num_scalar_prefetch=1, grid=(N_CHIPS, nb),
            in_specs=[pl.BlockSpec((1, tr, cdim), lambda k, i, c_ref: (k, c_ref[0] * nb + i, 0)), blk], out_specs=blk),
        out_shape=jax.ShapeDtypeStruct((N_CHIPS, rh, cdim), BF16),
        compiler_params=_params(("parallel", "parallel")),
    )(jnp.reshape(c, (1,)).astype(jnp.int32), g, got)


def _sum_blocks(name, parts):
    k, r, cdim = parts.shape
    tr = min(256, r)

    def body(p_ref, o_ref):
        acc = p_ref[0].astype(F32)
        for i in range(1, k):
            acc = acc + p_ref[i].astype(F32)
        o_ref[...] = acc

    return pl.pallas_call(
        body, name=name, grid=(r // tr,),
        in_specs=[pl.BlockSpec((k, tr, cdim), lambda i: (0, i, 0))], out_specs=pl.BlockSpec((tr, cdim), lambda i: (i, 0)),
        out_shape=jax.ShapeDtypeStruct((r, cdim), F32), compiler_params=_params(("parallel",)),
    )(parts)


def _pair_swap(halves):
    n = len(halves)

    def body(*refs):
        ins, outs = refs[:n], refs[n:2 * n]
        send_sems, recv_sems = refs[2 * n:]
        x, y, c = _place()
        cps = [pltpu.make_async_remote_copy(src_ref=ins[w], dst_ref=outs[w], send_sem=send_sems.at[w], recv_sem=recv_sems.at[w],
                                            device_id=(x, y, 1 - c), device_id_type=MESH) for w in range(n)]
        for cp in cps:
            cp.start()
        for cp in cps:
            cp.wait_recv()
        for cp in cps:
            cp.wait_send()

    return pl.pallas_call(
        body, name="pair_swap", out_shape=[jax.ShapeDtypeStruct(h.shape, h.dtype) for h in halves],
        in_specs=[ANY] * n, out_specs=[ANY] * n,
        scratch_shapes=[pltpu.SemaphoreType.DMA((n,)), pltpu.SemaphoreType.DMA((n,))],
    )(*halves)


ADA_SHARD = 6 * D_MODEL // N_CHIPS


def _mod_part(c_all, w_shard, b_shard):
    tn = 512

    def body(c_ref, w_ref, b_ref, o_ref):
        o_ref[...] = _dot(_silu(c_ref[...]).astype(BF16), w_ref[...].astype(BF16)) + b_ref[...]

    return pl.pallas_call(
        body, name="mod_part", grid=(ADA_SHARD // tn,),
        in_specs=[pl.BlockSpec((N_DEV, D_MODEL), lambda j: (0, 0)), pl.BlockSpec((D_MODEL, tn), lambda j: (0, j)),
                  pl.BlockSpec((1, tn), lambda j: (0, j))],
        out_specs=pl.BlockSpec((N_DEV, tn), lambda j: (0, j)),
        out_shape=jax.ShapeDtypeStruct((N_DEV, ADA_SHARD), F32), compiler_params=_params(("parallel",)),
    )(c_all, w_shard, b_shard)


def _w_ada_grad(c_all_t, dmod_shard):
    tm = 256

    def body(ct_ref, dm_ref, o_ref):
        act = _silu(ct_ref[...])
        acc = act[:, 0:1] * dm_ref[0:1, :]
        for dev in range(1, N_DEV):
            acc = acc + act[:, dev:dev + 1] * dm_ref[dev:dev + 1, :]
        o_ref[...] = acc

    return pl.pallas_call(
        body, name="w_ada_grad", grid=(D_MODEL // tm,),
        in_specs=[pl.BlockSpec((tm, N_DEV), lambda i: (i, 0)), pl.BlockSpec((N_DEV, ADA_SHARD), lambda i: (0, 0))],
        out_specs=pl.BlockSpec((tm, ADA_SHARD), lambda i: (i, 0)),
        out_shape=jax.ShapeDtypeStruct((D_MODEL, ADA_SHARD), F32), compiler_params=_params(("parallel",)),
    )(c_all_t, dmod_shard)


def _adamw_math(w, g, m, v):
    nm = ADAM_B1 * m + (1.0 - ADAM_B1) * g
    nv = ADAM_B2 * v + (1.0 - ADAM_B2) * jnp.square(g)
    m_hat = nm / (1.0 - ADAM_B1 ** ADAM_STEP)
    v_hat = nv / (1.0 - ADAM_B2 ** ADAM_STEP)
    return -ADAM_LR * (m_hat / (jnp.sqrt(v_hat) + ADAM_EPS) + ADAM_WD * w), nm, nv


def _adamw(name, w, g, m, v):
    _, r, cdim = w.shape
    tr = 256 if r % 256 == 0 else r

    def body(w_ref, g_ref, m_ref, v_ref, go_ref, d_ref, nm_ref, nv_ref):
        go_ref[...] = g_ref[...]
        d_ref[...], nm_ref[...], nv_ref[...] = _adamw_math(w_ref[...], g_ref[...], m_ref[...], v_ref[...])

    blk = pl.BlockSpec((None, tr, cdim), lambda i: (0, i, 0))
    return pl.pallas_call(
        body, name=name, grid=(r // tr,), in_specs=[blk, pl.BlockSpec((tr, cdim), lambda i: (i, 0)), blk, blk], out_specs=[blk] * 4,
        out_shape=[jax.ShapeDtypeStruct((1, r, cdim), F32)] * 4, compiler_params=_params(("parallel",)),
    )(w, g, m, v)


def _adamw_pair(name, w, mine, other, m, v, c):
    _, r, cdim = w.shape
    rh = r // 2
    tr = min(256, rh)
    per = rh // tr

    def body(c_ref, w_ref, a_ref, b_ref, m_ref, v_ref, g_ref, d_ref, nm_ref, nv_ref):
        is_mine = (pl.program_id(0) // per) == c_ref[0]
        g = jnp.where(is_mine, a_ref[...], b_ref[...])
        g_ref[...] = g
        d_ref[...], nm_ref[...], nv_ref[...] = _adamw_math(w_ref[...], g, m_ref[...], v_ref[...])

    blk = pl.BlockSpec((None, tr, cdim), lambda i, c_ref: (0, i, 0))
    half = pl.BlockSpec((tr, cdim), lambda i, c_ref: (i % per, 0))
    return pl.pallas_call(
        body, name=name,
        grid_spec=pltpu.PrefetchScalarGridSpec(num_scalar_prefetch=1, grid=(r // tr,), in_specs=[blk, half, half, blk, blk], out_specs=[blk] * 4),
        out_shape=[jax.ShapeDtypeStruct((1, r, cdim), F32)] * 4, compiler_params=_params(("parallel",)),
    )(jnp.reshape(c, (1,)).astype(jnp.int32), w, mine, other, m, v)


SMALL = ["b_ada", "conv_b", "dt_bias", "a_log", "d_skip", "ssm_norm_w", "f_bias", "attn_norm_w", "ln1_g", "ln1_b", "ln2_g", "ln2_b"]


def _pack(vs):
    pieces = []
    for v in vs:
        pieces.append(v)
        if v.shape[1] % LANES:
            pieces.append(jnp.zeros((1, -v.shape[1] % LANES), v.dtype))
    return jnp.concatenate(pieces, axis=1)


def _adamw_small(total, offs, ws, ms, vs):
    n = len(ws)

    def body(*refs):
        t_ref, outs = refs[0], refs[1 + 3 * n:]
        for i in range(n):
            g = t_ref[:, offs[i]:offs[i] + ws[i].shape[1]]
            dl, nm, nv = _adamw_math(refs[1 + i][...], g, refs[1 + n + i][...], refs[1 + 2 * n + i][...])
            outs[4 * i][...], outs[4 * i + 1][...], outs[4 * i + 2][...], outs[4 * i + 3][...] = g, dl, nm, nv

    res = pl.pallas_call(
        body, name="adamw_small", in_specs=[VMEM_SPEC] * (1 + 3 * n), out_specs=[VMEM_SPEC] * (4 * n),
        out_shape=[jax.ShapeDtypeStruct(w.shape, F32) for w in ws for _ in range(4)],
    )(total, *ws, *ms, *vs)
    return [res[4 * i:4 * i + 4] for i in range(n)]


def kernel(x, c, w_ada, b_ada, w_in, conv_w, conv_b, dt_bias, a_log, d_skip, ssm_norm_w, f_bias, attn_norm_w, w_out, ln1_g, ln1_b, w_ff_in, w_ff_out, ln2_g, ln2_b, loss_target, m_w_ada, m_b_ada, m_w_in, m_conv_w, m_conv_b, m_dt_bias, m_a_log, m_d_skip, m_ssm_norm_w, m_f_bias, m_attn_norm_w, m_w_out, m_ln1_g, m_ln1_b, m_w_ff_in, m_w_ff_out, m_ln2_g, m_ln2_b, v_w_ada, v_b_ada, v_w_in, v_conv_w, v_conv_b, v_dt_bias, v_a_log, v_d_skip, v_ssm_norm_w, v_f_bias, v_attn_norm_w, v_w_out, v_ln1_g, v_ln1_b, v_w_ff_in, v_w_ff_out, v_ln2_g, v_ln2_b):
    a = dict(b_ada=b_ada, conv_b=conv_b, dt_bias=dt_bias, a_log=a_log, d_skip=d_skip, ssm_norm_w=ssm_norm_w, f_bias=f_bias,
             attn_norm_w=attn_norm_w, ln1_g=ln1_g, ln1_b=ln1_b, ln2_g=ln2_g, ln2_b=ln2_b)
    ms = dict(b_ada=m_b_ada, conv_b=m_conv_b, dt_bias=m_dt_bias, a_log=m_a_log, d_skip=m_d_skip, ssm_norm_w=m_ssm_norm_w,
              f_bias=m_f_bias, attn_norm_w=m_attn_norm_w, ln1_g=m_ln1_g, ln1_b=m_ln1_b, ln2_g=m_ln2_g, ln2_b=m_ln2_b)
    vs = dict(b_ada=v_b_ada, conv_b=v_conv_b, dt_bias=v_dt_bias, a_log=v_a_log, d_skip=v_d_skip, ssm_norm_w=v_ssm_norm_w,
              f_bias=v_f_bias, attn_norm_w=v_attn_norm_w, ln1_g=v_ln1_g, ln1_b=v_ln1_b, ln2_g=v_ln2_g, ln2_b=v_ln2_b)
    xi, yi, ci = _place()
    chip = 2 * xi + yi
    me = 4 * xi + 2 * yi + ci
    d = D_MODEL
    conv_shard = CONV_DIM // N_CHIPS

    first = _allgather_small("gather_c", jnp.concatenate([c, conv_w[0].reshape(1, CONV_W * conv_shard)], axis=1))[:, 0]
    c_all = first[:, :d]
    conv_w_full = first[::2, d:].reshape(N_CHIPS, CONV_W, conv_shard).transpose(1, 0, 2).reshape(CONV_W, CONV_DIM)
    b_shard = lax.dynamic_slice_in_dim(b_ada, chip * ADA_SHARD, ADA_SHARD, axis=1)
    parts = _allgather_small("gather_mod", _mod_part(c_all, w_ada[0], b_shard))
    mod = lax.dynamic_index_in_dim(parts[::2], me, axis=1, keepdims=False).reshape(1, 6 * d)

    g_in = _gather_shards(w_in[0].astype(BF16))
    w_al = _to_aligned(g_in.transpose(1, 0, 2).reshape(d, IN_COLS))

    sp = {n: a[n] for n in SMALL[1:]}
    sp["conv_w"] = conv_w_full
    shards = [w_out[0].astype(BF16), w_ff_in[0].astype(BF16), w_ff_out[0].astype(BF16)]
    dx, landed, small = _local_step(x[0], loss_target[0], mod, w_al, shards, sp)

    names = ["mod"] + SMALL[1:]
    vec = _pack([small[n] for n in names] + [small["conv_w"].reshape(1, CONV_W * CONV_DIM), small["loss"]])
    every = _allgather_small("gather_small", vec)
    total = _sum_blocks("sum_small", jnp.broadcast_to(every, (N_DEV, SUBLANES, vec.shape[1])))[:1]
    widths = [6 * d] + [a[n].shape[1] for n in SMALL[1:]]
    offs = [0]
    for w in widths:
        offs.append(offs[-1] + w + (-w % LANES))
    g_conv_w_full = total[:, offs[-1]:offs[-1] + CONV_W * CONV_DIM].reshape(CONV_W, CONV_DIM)
    loss = total[0, offs[-1] + CONV_W * CONV_DIM]
    dmod_shard = lax.dynamic_slice_in_dim(every[:, 0, :6 * d], chip * ADA_SHARD, ADA_SHARD, axis=1)
    g_w_ada = _w_ada_grad(c_all.T, dmod_shard)
    g_conv_w = lax.dynamic_slice_in_dim(g_conv_w_full, chip * conv_shard, conv_shard, axis=1)

    mine = [_sum_blocks("dev_sum_%d" % i, p) for i, p in enumerate(landed)]
    other = _pair_swap(mine)

    grads, deltas, new_m, new_v = {}, {}, {}, {}
    paired = dict(w_in=(w_in, m_w_in, v_w_in), w_out=(w_out, m_w_out, v_w_out), w_ff_in=(w_ff_in, m_w_ff_in, v_w_ff_in),
                  w_ff_out=(w_ff_out, m_w_ff_out, v_w_ff_out))
    for i, (n, (w, m, v)) in enumerate(paired.items()):
        grads[n], deltas[n], new_m[n], new_v[n] = _adamw_pair("adamw_" + n, w, mine[i], other[i], m, v, ci)
    for n, g, (w, m, v) in (("w_ada", g_w_ada, (w_ada, m_w_ada, v_w_ada)), ("conv_w", g_conv_w, (conv_w, m_conv_w, v_conv_w))):
        grads[n], deltas[n], new_m[n], new_v[n] = _adamw("adamw_" + n, w, g, m, v)
    for n, res in zip(SMALL, _adamw_small(total, offs, [a[n] for n in SMALL], [ms[n] for n in SMALL], [vs[n] for n in SMALL])):
        grads[n], deltas[n], new_m[n], new_v[n] = res

    order = ["w_ada", "b_ada", "w_in", "conv_w", "conv_b", "dt_bias", "a_log", "d_skip", "ssm_norm_w", "f_bias", "attn_norm_w", "w_out",
             "ln1_g", "ln1_b", "w_ff_in", "w_ff_out", "ln2_g", "ln2_b"]
    return (loss, dx[None], *[grads[n] for n in order], *[deltas[n] for n in order], *[new_m[n] for n in order], *[new_v[n] for n in order])
```

```python
import functools

import jax
import jax.numpy as jnp
from jax import lax
from jax.experimental import pallas as pl
from jax.experimental.pallas import tpu as pltpu

F32, BF16 = jnp.float32, jnp.bfloat16

D_MODEL = 1024
N_HEADS = 16
HEAD_DIM = 64
N_PAIRS = N_HEADS // 2
SSM_GROUPS = 2
SSM_STATE = 128
CHUNK = 128
CONV_W = 4
CONV_DIM = 1536
D_FF = 4096
IN_COLS = 5664
ALPHA = 2.0 ** 0.25
LN_EPS = 1e-5
RMS_EPS = 1e-5
LANES = 128
SUBLANES = 8

AL_Z, AL_XS, AL_Q, AL_K, AL_V, AL_B, AL_C, AL_DTF = 0, 1024, 2048, 3072, 4096, 5120, 5376, 5632
AL_COLS = 5760
F_LANE = 16

ADAM_LR, ADAM_B1, ADAM_B2, ADAM_EPS, ADAM_WD, ADAM_STEP = 0.001, 0.9, 0.999, 1e-08, 0.01, 10

VMEM_LIMIT = 56 * 1024 * 1024
MESH = pl.DeviceIdType.MESH


def _params(sem=None):
    return pltpu.CompilerParams(dimension_semantics=sem, vmem_limit_bytes=VMEM_LIMIT)


def _sigmoid(x):
    return 1.0 / (1.0 + jnp.exp(-x))


def _silu(x):
    return x * _sigmoid(x)


def _softplus(x):
    return jnp.maximum(x, 0.0) + jnp.log(1.0 + jnp.exp(-jnp.abs(x)))


def _split3(a):
    hi = a.astype(BF16)
    r = a - hi.astype(F32)
    mid = r.astype(BF16)
    lo = (r - mid.astype(F32)).astype(BF16)
    return hi, mid, lo


def _dot(a, b, dims=((1,), (0,))):
    return lax.dot_general(a, b, (dims, ((), ())), preferred_element_type=F32)


NN, NT, TN = ((1,), (0,)), ((1,), (1,)), ((0,), (0,))


def _dot3(t, a):
    hi, mid, lo = _split3(a)
    return _dot(t, hi) + _dot(t, mid) + _dot(t, lo)


def _matmul(name, a, b, *, dims=NN, out_dtype=F32, tm=1024, tn=1024, tk=1024, by_chip=None, epi=None, carry=()):
    if dims == NN:
        (m, k), n = a.shape, b.shape[1]
    elif dims == NT:
        (m, k), n = a.shape, b.shape[0]
    else:
        (k, m), n = a.shape, b.shape[1]
    if by_chip == "rows":
        tm = min(tm, m // 4)
    if by_chip == "cols":
        tn = min(tn, n // 4)
    tm, tn, tk = min(tm, m), min(tn, n), min(tk, k)
    assert m % tm == 0 and n % tn == 0 and k % tk == 0, (name, m, n, k, tm, tn, tk)
    nk = k // tk
    if by_chip == "rows":
        per = m // 4 // tm
        out_spec = pl.BlockSpec((None, tm, tn), lambda i, j, l: (i // per, i % per, j))
        out_shape = jax.ShapeDtypeStruct((4, m // 4, n), out_dtype)
    elif by_chip == "cols":
        per = n // 4 // tn
        out_spec = pl.BlockSpec((None, tm, tn), lambda i, j, l: (j // per, i, j % per))
        out_shape = jax.ShapeDtypeStruct((4, m, n // 4), out_dtype)
    else:
        out_spec = pl.BlockSpec((tm, tn), lambda i, j, l: (i, j))
        out_shape = jax.ShapeDtypeStruct((m, n), out_dtype)
    a_spec = pl.BlockSpec((tk, tm), lambda i, j, l: (l, i)) if dims == TN else pl.BlockSpec((tm, tk), lambda i, j, l: (i, l))
    b_spec = pl.BlockSpec((tn, tk), lambda i, j, l: (j, l)) if dims == NT else pl.BlockSpec((tk, tn), lambda i, j, l: (l, j))

    tile = pl.BlockSpec((tm, tn), lambda i, j, l: (i, j))
    in_specs, args, out_specs, out_shape = [a_spec, b_spec], [a, b], [out_spec], [out_shape]
    fn, n_tiles, n_sums = None, 1, 0
    if epi is not None:
        fn, fulls, vecs, outs, sums = epi
        assert by_chip is None and (not sums or n == tn), name
        in_specs = in_specs + [tile] * len(fulls) + [pl.BlockSpec((1, tn), lambda i, j, l: (0, j))] * len(vecs)
        args = args + list(fulls) + list(vecs)
        out_specs = [tile] * len(outs) + [pl.BlockSpec((1, w), lambda i, j, l: (0, 0)) for w in sums]
        out_shape = [jax.ShapeDtypeStruct((m, n), dt) for dt in outs] + [jax.ShapeDtypeStruct((1, w), F32) for w in sums]
        n_tiles, n_sums = len(outs), len(sums)
    n_in, n_out, n_c = len(args), len(out_specs), len(carry)
    scratch = [pltpu.VMEM((tm, tn) if nk > 1 else (SUBLANES, LANES), F32)]
    if n_c:
        in_specs, args = in_specs + [ANY] * n_c, args + list(carry)
        out_specs = out_specs + [ANY] * n_c
        out_shape = out_shape + [jax.ShapeDtypeStruct(g.shape, g.dtype) for g in carry]
        scratch = scratch + _exchange_sems(n_c)
    gm, gn = m // tm, n // tn

    def body(*refs):
        a_ref, b_ref = refs[:2]
        ins, outs = refs[2:n_in], refs[n_in + n_c:n_in + n_c + n_out]
        acc_ref = refs[n_in + 2 * n_c + n_out]
        i, j, l = pl.program_id(0), pl.program_id(1), pl.program_id(2)
        if n_c:
            start, wait = _scatter_plan(refs[n_in:n_in + n_c], refs[n_in + n_c + n_out:n_in + 2 * n_c + n_out], *refs[n_in + 2 * n_c + n_out + 1:])
            pl.when((i == 0) & (j == 0) & (l == 0))(start)
        part = _dot(a_ref[...].astype(BF16), b_ref[...].astype(BF16), dims)

        def finish(res):
            if fn is None:
                outs[0][...] = res.astype(outs[0].dtype)
                return
            tiles, colsums = fn(res, *[r[...] for r in ins])
            for r, val in zip(outs[:n_tiles], tiles):
                r[...] = val.astype(r.dtype)
            if n_sums:
                @pl.when(i == 0)
                def _():
                    for r in outs[n_tiles:]:
                        r[...] = jnp.zeros_like(r)
                for r, val in zip(outs[n_tiles:], colsums):
                    r[...] += val

        if nk == 1:
            finish(part)
        else:
            @pl.when(l == 0)
            def _():
                acc_ref[...] = part

            @pl.when((l > 0) & (l < nk - 1))
            def _():
                acc_ref[...] += part

            @pl.when(l == nk - 1)
            def _():
                finish(acc_ref[...] + part)

        if n_c:
            pl.when((i == gm - 1) & (j == gn - 1) & (l == nk - 1))(wait)

    res = pl.pallas_call(
        body, name=name, grid=(gm, gn, nk),
        in_specs=in_specs, out_specs=out_specs, out_shape=out_shape, scratch_shapes=scratch,
        compiler_params=_params(("arbitrary",) * 3 if n_c or n_sums else ("parallel", "parallel", "arbitrary")),
    )(*args)
    return res[0] if len(res) == 1 else res


def _rowwise(name, fn, fulls, vecs, out_fulls, out_vecs, tr=256):
    fulls = [f if isinstance(f, tuple) else (f, f.shape[1], 0) for f in fulls]
    s = fulls[0][0].shape[0]
    tr = min(tr, s)
    out_fulls = [o if len(o) == 3 else (*o, (o[0], 0, None)) for o in out_fulls]
    into = [(k, slab[2]) for k, (_, _, slab) in enumerate(out_fulls) if slab[2] is not None]
    nf, nv, nof, nov = len(fulls), len(vecs), len(out_fulls), len(out_vecs)
    in_specs = [pl.BlockSpec((tr, w), functools.partial(lambda i, cb: (i, cb), cb=cb)) for (_, w, cb) in fulls]
    in_specs += [pl.BlockSpec(v.shape, lambda i: (0, 0)) for v in vecs] + [ANY] * len(into)
    out_shape = [jax.ShapeDtypeStruct((s, slab[0]), dt) for (_, dt, slab) in out_fulls] + [jax.ShapeDtypeStruct((1, w), F32) for w in out_vecs]
    out_specs = [pl.BlockSpec((tr, w), functools.partial(lambda i, cb: (i, cb), cb=slab[1])) for (w, _, slab) in out_fulls]
    out_specs += [pl.BlockSpec((1, w), lambda i: (0, 0)) for w in out_vecs]

    def body(*refs):
        outs = refs[nf + nv + len(into):]
        of, ov = fn(*[r[...] for r in refs[:nf + nv]])
        for r, val in zip(outs[:nof], of):
            r[...] = val.astype(r.dtype)
        if nov:
            @pl.when(pl.program_id(0) == 0)
            def _():
                for r in outs[nof:]:
                    r[...] = jnp.zeros_like(r)
            for r, val in zip(outs[nof:], ov):
                r[...] += val

    res = pl.pallas_call(
        body, name=name, grid=(s // tr,), in_specs=in_specs, out_specs=out_specs, out_shape=out_shape,
        input_output_aliases={nf + nv + pos: k for pos, (k, _) in enumerate(into)},
        compiler_params=_params(("arbitrary",)),
    )(*[f[0] for f in fulls], *vecs, *[buf for _, buf in into])
    return res[:nof], res[nof:]


def _colsum(x):
    return jnp.sum(x, axis=0, keepdims=True)


def _rowmean(x):
    return jnp.mean(x, axis=-1, keepdims=True)


CONV_CB = 512
CONV_TR = 512


def _shift_down(u, halo, j):
    if j == 0:
        return u
    ru = pltpu.roll(u, j, 0)
    row8 = lax.broadcasted_iota(jnp.int32, halo.shape, 0)
    top = jnp.where(row8 < j, pltpu.roll(halo, j, 0), ru[:SUBLANES])
    return jnp.concatenate([top, ru[SUBLANES:]], axis=0)


def _shift_up(d, halo, j):
    if j == 0:
        return d
    tr = d.shape[0]
    rd = pltpu.roll(d, tr - j, 0)
    row8 = lax.broadcasted_iota(jnp.int32, halo.shape, 0)
    bot = jnp.where(row8 >= SUBLANES - j, pltpu.roll(halo, SUBLANES - j, 0), rd[tr - SUBLANES:])
    return jnp.concatenate([rd[:tr - SUBLANES], bot], axis=0)


def _conv_col(cb):
    return jnp.where(cb < 2, AL_XS // CONV_CB + cb, AL_B // CONV_CB)


def _conv_specs(s, tr):
    per8 = tr // SUBLANES
    blk = pl.BlockSpec((tr, CONV_CB), lambda cb, i: (i, _conv_col(cb)))
    prev = pl.BlockSpec((SUBLANES, CONV_CB), lambda cb, i: (jnp.maximum(i * per8 - 1, 0), _conv_col(cb)))
    return blk, prev


def _conv_pre(u, halo, w_ref, b_ref, first):
    halo = jnp.where(first, 0.0, halo)
    acc = b_ref[...] + w_ref[CONV_W - 1:CONV_W, :] * u
    shifted = [u]
    for j in range(1, CONV_W):
        sh = _shift_down(u, halo, j)
        shifted.append(sh)
        acc = acc + w_ref[CONV_W - 1 - j:CONV_W - j, :] * sh
    return acc, shifted


def _conv_fwd(proj, conv_w, conv_b):
    s = proj.shape[0]
    tr = min(CONV_TR, s)
    blk, prev = _conv_specs(s, tr)

    def body(u_ref, h_ref, w_ref, b_ref, o_ref):
        pre, _ = _conv_pre(u_ref[...], h_ref[...], w_ref, b_ref, pl.program_id(1) == 0)
        o_ref[...] = _silu(pre)

    return pl.pallas_call(
        body, name="conv_fwd", grid=(CONV_DIM // CONV_CB, s // tr),
        in_specs=[blk, prev, pl.BlockSpec((CONV_W, CONV_CB), lambda cb, i: (0, cb)), pl.BlockSpec((1, CONV_CB), lambda cb, i: (0, cb))],
        out_specs=pl.BlockSpec((tr, CONV_CB), lambda cb, i: (i, cb)),
        out_shape=jax.ShapeDtypeStruct((s, CONV_DIM), F32),
        compiler_params=_params(("parallel", "parallel")),
    )(proj, proj, conv_w, conv_b)


def _conv_bwd_pre(proj, conv_w, conv_b, dxc):
    s = proj.shape[0]
    tr = min(CONV_TR, s)
    blk, prev = _conv_specs(s, tr)

    def body(u_ref, h_ref, w_ref, b_ref, d_ref, dpre_ref, dw_ref, db_ref):
        i = pl.program_id(1)
        pre, shifted = _conv_pre(u_ref[...], h_ref[...], w_ref, b_ref, i == 0)
        sg = _sigmoid(pre)
        dpre = d_ref[...] * (sg * (1.0 + pre * (1.0 - sg)))
        dpre_ref[...] = dpre

        @pl.when(i == 0)
        def _():
            dw_ref[...] = jnp.zeros_like(dw_ref)
            db_ref[...] = jnp.zeros_like(db_ref)

        db_ref[...] += _colsum(dpre)
        for j in range(CONV_W):
            dw_ref[CONV_W - 1 - j:CONV_W - j, :] += _colsum(dpre * shifted[j])

    own = pl.BlockSpec((tr, CONV_CB), lambda cb, i: (i, cb))
    wspec = pl.BlockSpec((CONV_W, CONV_CB), lambda cb, i: (0, cb))
    bspec = pl.BlockSpec((1, CONV_CB), lambda cb, i: (0, cb))
    return pl.pallas_call(
        body, name="conv_bwd_pre", grid=(CONV_DIM // CONV_CB, s // tr),
        in_specs=[blk, prev, wspec, bspec, own], out_specs=[own, wspec, bspec],
        out_shape=[jax.ShapeDtypeStruct((s, CONV_DIM), F32), jax.ShapeDtypeStruct((CONV_W, CONV_DIM), F32),
                   jax.ShapeDtypeStruct((1, CONV_DIM), F32)],
        compiler_params=_params(("parallel", "arbitrary")),
    )(proj, proj, conv_w, conv_b, dxc)


def _conv_bwd_in(dpre, conv_w, dproj):
    s = dpre.shape[0]
    tr = min(CONV_TR, s)
    per8 = tr // SUBLANES
    last8 = s // SUBLANES - 1
    nb = s // tr

    def body(d_ref, n_ref, w_ref, _, o_ref):
        d = d_ref[...]
        halo = jnp.where(pl.program_id(1) == nb - 1, 0.0, n_ref[...])
        acc = w_ref[CONV_W - 1:CONV_W, :] * d
        for j in range(1, CONV_W):
            acc = acc + w_ref[CONV_W - 1 - j:CONV_W - j, :] * _shift_up(d, halo, j)
        o_ref[...] = acc.astype(o_ref.dtype)

    own = pl.BlockSpec((tr, CONV_CB), lambda cb, i: (i, cb))
    nxt = pl.BlockSpec((SUBLANES, CONV_CB), lambda cb, i: (jnp.minimum((i + 1) * per8, last8), cb))
    return pl.pallas_call(
        body, name="conv_bwd_in", grid=(CONV_DIM // CONV_CB, nb),
        in_specs=[own, nxt, pl.BlockSpec((CONV_W, CONV_CB), lambda cb, i: (0, cb)), ANY],
        out_specs=pl.BlockSpec((tr, CONV_CB), lambda cb, i: (i, _conv_col(cb))),
        out_shape=jax.ShapeDtypeStruct(dproj.shape, dproj.dtype), input_output_aliases={3: 0},
        compiler_params=_params(("parallel", "parallel")),
    )(dpre, dpre, conv_w, dproj)


XC_B, XC_C = 1024, 1280


def _tile_iotas():
    row = lax.broadcasted_iota(jnp.int32, (CHUNK, LANES), 0)
    lane = lax.broadcasted_iota(jnp.int32, (CHUNK, LANES), 1)
    return row, lane


def _ssd_scalars(dtf_ref, bias_ref, alog_ref, row, lane):
    head = lane[:1] < N_HEADS
    raw = dtf_ref[...] + bias_ref[...]
    dt = _softplus(raw)
    a_neg = jnp.where(head, -jnp.exp(alog_ref[...]), 0.0)
    a = dt * a_neg
    tril = (row >= lane).astype(BF16)
    s = _dot3(tril, a)
    return raw, dt, a_neg, s


def _pair(v, j, lo):
    return jnp.where(lo, v[:, 2 * j:2 * j + 1], v[:, 2 * j + 1:2 * j + 2])


def _head_sum(x, lo, hh):
    return jnp.sum(jnp.where(lo == (hh == 0), x, 0.0), axis=1, keepdims=True)


def _decay_masks(s, h, row, lane):
    s_col = jnp.broadcast_to(s[:, h:h + 1], (CHUNK, LANES))
    s_row = s_col.T
    lm = jnp.where(row >= lane, jnp.exp(s_col - s_row), 0.0)
    lmt = jnp.where(row <= lane, jnp.exp(s_row - s_col), 0.0)
    return lm, lmt


def _ssd_fwd(xc_all, proj, dt_bias_l, a_log_l, d_exp):
    s_len = xc_all.shape[0]
    nc = s_len // CHUNK

    def body(x_ref, dtf_ref, bias_ref, alog_ref, dexp_ref, y_ref, prevs_ref, state_ref):
        @pl.when(pl.program_id(0) == 0)
        def _():
            state_ref[...] = jnp.zeros_like(state_ref)

        row, lane = _tile_iotas()
        lo = lane < HEAD_DIM
        _, dt, _, s = _ssd_scalars(dtf_ref, bias_ref, alog_ref, row, lane)
        tot = s[CHUNK - 1:CHUNK, :]
        for g in range(SSM_GROUPS):
            bg = x_ref[:, XC_B + g * SSM_STATE:XC_B + (g + 1) * SSM_STATE].astype(BF16)
            cg = x_ref[:, XC_C + g * SSM_STATE:XC_C + (g + 1) * SSM_STATE].astype(BF16)
            cb = _dot(cg, bg, NT)
            for j in range(g * 4, g * 4 + 4):
                xs_p = x_ref[:, j * LANES:(j + 1) * LANES]
                dt_p, s_p, tot_p = _pair(dt, j, lo), _pair(s, j, lo), _pair(tot, j, lo[:1])
                xc_p = xs_p * dt_p
                xc_b = xc_p.astype(BF16)
                yd = []
                for hh in range(2):
                    lm, _ = _decay_masks(s, 2 * j + hh, row, lane)
                    yd.append(_dot((cb * lm).astype(BF16), xc_b))
                prev = state_ref[j]
                prevs_ref[0, j] = prev
                yo = _dot(cg, prev.astype(BF16)) * jnp.exp(s_p)
                y_ref[:, j * LANES:(j + 1) * LANES] = jnp.where(lo, yd[0], yd[1]) + yo + dexp_ref[:, j * LANES:(j + 1) * LANES] * xs_p
                to_end = jnp.exp(tot_p - s_p)
                state_ref[j] = jnp.exp(tot_p) * prev + _dot(bg, (xc_p * to_end).astype(BF16), TN)

    vec = lambda w: pl.BlockSpec((1, w), lambda c: (0, 0))
    return pl.pallas_call(
        body, name="ssd_fwd", grid=(nc,),
        in_specs=[pl.BlockSpec((CHUNK, CONV_DIM), lambda c: (c, 0)), pl.BlockSpec((CHUNK, LANES), lambda c: (c, AL_DTF // LANES)),
                  vec(LANES), vec(LANES), vec(D_MODEL)],
        out_specs=[pl.BlockSpec((CHUNK, D_MODEL), lambda c: (c, 0)), pl.BlockSpec((1, N_PAIRS, SSM_STATE, LANES), lambda c: (c, 0, 0, 0))],
        out_shape=[jax.ShapeDtypeStruct((s_len, D_MODEL), F32), jax.ShapeDtypeStruct((nc, N_PAIRS, SSM_STATE, LANES), F32)],
        scratch_shapes=[pltpu.VMEM((N_PAIRS, SSM_STATE, LANES), F32)],
        compiler_params=_params(("arbitrary",)),
    )(xc_all, proj, dt_bias_l, a_log_l, d_exp)


def _ssd_bwd(xc_all, proj, dt_bias_l, a_log_l, d_exp, prevs, dy):
    s_len = xc_all.shape[0]
    nc = s_len // CHUNK

    def body(x_ref, dtf_ref, bias_ref, alog_ref, dexp_ref, prevs_ref, dy_ref, dx_ref, ddt_ref, da_ref, dd_ref, dbias_ref, dstate_ref):
        @pl.when(pl.program_id(0) == 0)
        def _():
            dstate_ref[...] = jnp.zeros_like(dstate_ref)
            da_ref[...] = jnp.zeros_like(da_ref)
            dd_ref[...] = jnp.zeros_like(dd_ref)
            dbias_ref[...] = jnp.zeros_like(dbias_ref)

        row, lane = _tile_iotas()
        lo = lane < HEAD_DIM
        last = row == CHUNK - 1
        raw, dt, a_neg, s = _ssd_scalars(dtf_ref, bias_ref, alog_ref, row, lane)
        tot = s[CHUNK - 1:CHUNK, :]
        ds_acc = jnp.zeros((CHUNK, LANES), F32)
        ddt_acc = jnp.zeros((CHUNK, LANES), F32)
        for g in range(SSM_GROUPS):
            bcol = slice(XC_B + g * SSM_STATE, XC_B + (g + 1) * SSM_STATE)
            ccol = slice(XC_C + g * SSM_STATE, XC_C + (g + 1) * SSM_STATE)
            bg = x_ref[:, bcol].astype(BF16)
            cg = x_ref[:, ccol].astype(BF16)
            cb = _dot(cg, bg, NT)
            cbt = _dot(bg, cg, NT)
            dcb = jnp.zeros((CHUNK, LANES), F32)
            dcbt = jnp.zeros((CHUNK, LANES), F32)
            db_acc = jnp.zeros((CHUNK, LANES), F32)
            dc_acc = jnp.zeros((CHUNK, LANES), F32)
            for j in range(g * 4, g * 4 + 4):
                cols = slice(j * LANES, (j + 1) * LANES)
                xs_p, dy_p = x_ref[:, cols], dy_ref[:, cols]
                dt_p, s_p, tot_p = _pair(dt, j, lo), _pair(s, j, lo), _pair(tot, j, lo[:1])
                xc_p = xs_p * dt_p
                xc_b, dy_b = xc_p.astype(BF16), dy_p.astype(BF16)
                e_p, f_p, etot_p = jnp.exp(s_p), jnp.exp(tot_p - s_p), jnp.exp(tot_p)
                prev, dnext = prevs_ref[0, j], dstate_ref[j]
                prev_b, dnext_b = prev.astype(BF16), dnext.astype(BF16)
                dd_ref[:, cols] += _colsum(dy_p * xs_p)
                dxs_p = dexp_ref[:, cols] * dy_p
                cp = _dot(cg, prev_b)
                gy = (dy_p * e_p).astype(BF16)
                dc_acc += _dot(gy, prev_b, NT)
                dstate_ref[j] = etot_p * dnext + _dot(cg, gy, TN)
                de = dy_p * cp * e_p
                bds = _dot(bg, dnext_b)
                db_acc += _dot((xc_p * f_p).astype(BF16), dnext_b, NT)
                dxc_p = bds * f_p
                df = bds * xc_p * f_p
                dtot_p = _colsum(dnext * prev) * etot_p + _colsum(df)
                dsl = de - df + jnp.where(last, dtot_p, 0.0)
                for hh in range(2):
                    h = 2 * j + hh
                    mine = lo == (hh == 0)
                    lm, lmt = _decay_masks(s, h, row, lane)
                    dy_h = jnp.where(mine, dy_p, 0.0).astype(BF16)
                    xc_h = jnp.where(mine, xc_p, 0.0).astype(BF16)
                    dm = _dot(dy_h, xc_b, NT)
                    dmt = _dot(xc_h, dy_b, NT)
                    mt = cbt * lmt
                    dxc_p += _dot(mt.astype(BF16), dy_h)
                    ds_h = (jnp.sum(dm * cb * lm, axis=1, keepdims=True) - jnp.sum(dmt * mt, axis=1, keepdims=True)
                            + _head_sum(dsl, lo, hh))
                    ds_acc += jnp.where(lane == h, ds_h, 0.0)
                    dcb += dm * lm
                    dcbt += dmt * lmt
                    ddt_acc += jnp.where(lane == h, _head_sum(dxc_p * xs_p, lo, hh), 0.0)
                dx_ref[:, cols] = dxs_p + dxc_p * dt_p
            dx_ref[:, ccol] = dc_acc + _dot(dcb.astype(BF16), bg)
            dx_ref[:, bcol] = db_acc + _dot(dcbt.astype(BF16), cg)
        triu = (row <= lane).astype(BF16)
        da = _dot3(triu, ds_acc)
        ddt = ddt_acc + da * a_neg
        da_ref[...] += _colsum(da * dt) * a_neg[:1]
        ddt_raw = jnp.where(lane < N_HEADS, ddt * _sigmoid(raw), 0.0)
        dbias_ref[...] += _colsum(ddt_raw)
        ddt_ref[...] = ddt_raw

    rev = lambda c: nc - 1 - c
    vec = lambda w: pl.BlockSpec((1, w), lambda c: (0, 0))
    return pl.pallas_call(
        body, name="ssd_bwd", grid=(nc,),
        in_specs=[pl.BlockSpec((CHUNK, CONV_DIM), lambda c: (rev(c), 0)), pl.BlockSpec((CHUNK, LANES), lambda c: (rev(c), AL_DTF // LANES)),
                  vec(LANES), vec(LANES), vec(D_MODEL),
                  pl.BlockSpec((1, N_PAIRS, SSM_STATE, LANES), lambda c: (rev(c), 0, 0, 0)),
                  pl.BlockSpec((CHUNK, D_MODEL), lambda c: (rev(c), 0))],
        out_specs=[pl.BlockSpec((CHUNK, CONV_DIM), lambda c: (rev(c), 0)), pl.BlockSpec((CHUNK, LANES), lambda c: (rev(c), 0)),
                   vec(LANES), vec(D_MODEL), vec(LANES)],
        out_shape=[jax.ShapeDtypeStruct((s_len, CONV_DIM), F32), jax.ShapeDtypeStruct((s_len, LANES), F32),
                   jax.ShapeDtypeStruct((1, LANES), F32), jax.ShapeDtypeStruct((1, D_MODEL), F32), jax.ShapeDtypeStruct((1, LANES), F32)],
        scratch_shapes=[pltpu.VMEM((N_PAIRS, SSM_STATE, LANES), F32)],
        compiler_params=_params(("arbitrary",)),
    )(xc_all, proj, dt_bias_l, a_log_l, d_exp, prevs, dy)


AUG_C, AUG_ONE = 64, 67
NEG = -1e30
ATT_T = 512


def _fox_cum(proj, f_bias_l):
    s_len = proj.shape[0]
    nc = s_len // CHUNK

    def body(dtf_ref, fb_ref, cum_ref):
        row, lane = _tile_iotas()
        tril = (row >= lane).astype(BF16)

        def step(c, carry):
            rows = pl.ds(pl.multiple_of(c * CHUNK, CHUNK), CHUNK)
            lf = -_softplus(-(dtf_ref[rows, :] + fb_ref[...]))
            lf = jnp.where(lane < N_HEADS, pltpu.roll(lf, LANES - F_LANE, 1), 0.0)
            cs = _dot3(tril, lf) + carry
            cum_ref[rows, :] = cs
            return cs[CHUNK - 1:CHUNK, :]

        lax.fori_loop(0, nc, step, jnp.zeros((1, LANES), F32))

    return pl.pallas_call(
        body, name="fox_cum", grid=(1,),
        in_specs=[pl.BlockSpec((s_len, LANES), lambda i: (0, AL_DTF // LANES)), pl.BlockSpec((1, LANES), lambda i: (0, 0))],
        out_specs=pl.BlockSpec((s_len, LANES), lambda i: (0, 0)),
        out_shape=jax.ShapeDtypeStruct((s_len, LANES), F32),
        compiler_params=_params(("arbitrary",)),
    )(proj, f_bias_l)


def _fox_cum_bwd(dcum, proj, f_bias_l, ddt_tile, dproj):
    s_len = proj.shape[0]
    nc = s_len // CHUNK

    def body(dcum_ref, dtf_ref, fb_ref, ddt_ref, _, out_ref, dfb_ref):
        row, lane = _tile_iotas()
        triu = (row <= lane).astype(BF16)
        is_f = (lane >= F_LANE) & (lane < F_LANE + N_HEADS)

        def step(t, carry):
            run, dfb = carry
            rows = pl.ds(pl.multiple_of((nc - 1 - t) * CHUNK, CHUNK), CHUNK)
            rc = _dot3(triu, dcum_ref[rows, :]) + run
            sg = _sigmoid(-(dtf_ref[rows, :] + fb_ref[...]))
            df = jnp.where(is_f, pltpu.roll(rc, F_LANE, 1) * sg, 0.0)
            out_ref[rows, :] = (df + ddt_ref[rows, :]).astype(out_ref.dtype)
            return rc[0:1, :], dfb + _colsum(df)

        _, dfb = lax.fori_loop(0, nc, step, (jnp.zeros((1, LANES), F32), jnp.zeros((1, LANES), F32)))
        dfb_ref[...] = dfb

    whole = pl.BlockSpec((s_len, LANES), lambda i: (0, 0))
    dtf_cols = pl.BlockSpec((s_len, LANES), lambda i: (0, AL_DTF // LANES))
    vec = pl.BlockSpec((1, LANES), lambda i: (0, 0))
    return pl.pallas_call(
        body, name="fox_cum_bwd", grid=(1,),
        in_specs=[whole, dtf_cols, vec, whole, ANY], out_specs=[dtf_cols, vec],
        out_shape=[jax.ShapeDtypeStruct(dproj.shape, dproj.dtype), jax.ShapeDtypeStruct((1, LANES), F32)],
        input_output_aliases={4: 0}, compiler_params=_params(("arbitrary",)),
    )(dcum, proj, f_bias_l, ddt_tile, dproj)


def _attn_prep(proj, cum):
    s_len = proj.shape[0]
    tr = min(512, s_len)

    def body(q_ref, k_ref, v_ref, cum_ref, qa_ref, ka_ref, vb_ref):
        p = pl.program_id(0)
        lane = lax.broadcasted_iota(jnp.int32, (tr, LANES), 1)
        lo = lane < HEAD_DIM
        c = cum_ref[...]
        c1 = c.astype(BF16).astype(F32)
        r = c - c1
        c2 = r.astype(BF16).astype(F32)
        c3 = (r - c2).astype(BF16).astype(F32)
        q, k = q_ref[...] * (HEAD_DIM ** -0.5), k_ref[...]
        for hh in range(2):
            col = lambda x: jnp.sum(jnp.where(lane == 2 * p + hh, x, 0.0), axis=1, keepdims=True)
            a1, a2, a3 = col(c1), col(c2), col(c3)
            qh = q if hh == 0 else pltpu.roll(q, HEAD_DIM, 1)
            kh = k if hh == 0 else pltpu.roll(k, HEAD_DIM, 1)
            q_aug = jnp.where(lane == AUG_C, a1, jnp.where(lane == AUG_C + 1, a2, jnp.where(lane == AUG_C + 2, a3,
                              jnp.where(lane < AUG_ONE + 3, 1.0, 0.0))))
            k_aug = jnp.where(lane < AUG_ONE, 1.0, jnp.where(lane == AUG_ONE, -a1, jnp.where(lane == AUG_ONE + 1, -a2,
                              jnp.where(lane == AUG_ONE + 2, -a3, 0.0))))
            qa_ref[hh] = jnp.where(lo, qh, q_aug).astype(BF16)
            ka_ref[hh] = jnp.where(lo, kh, k_aug).astype(BF16)
        vb_ref[...] = v_ref[...].astype(BF16)

    slab = lambda col0: pl.BlockSpec((tr, LANES), lambda p, i: (i, col0 // LANES + p))
    heads = pl.BlockSpec((2, tr, LANES), lambda p, i: (p, i, 0))
    return pl.pallas_call(
        body, name="attn_prep", grid=(N_PAIRS, s_len // tr),
        in_specs=[slab(AL_Q), slab(AL_K), slab(AL_V), pl.BlockSpec((tr, LANES), lambda p, i: (i, 0))],
        out_specs=[heads, heads, pl.BlockSpec((tr, LANES), lambda p, i: (i, p))],
        out_shape=[jax.ShapeDtypeStruct((N_HEADS, s_len, LANES), BF16), jax.ShapeDtypeStruct((N_HEADS, s_len, LANES), BF16),
                   jax.ShapeDtypeStruct((s_len, D_MODEL), BF16)],
        compiler_params=_params(("parallel", "parallel")),
    )(proj, proj, proj, cum)


def _attn_fwd(qa, ka, vb, halves):
    s_len = vb.shape[0]
    t = min(ATT_T, s_len)
    nq = s_len // t
    n = len(halves)

    def body(qa_ref, ka_ref, vb_ref, *rest):
        o_ref, lse_ref = rest[n:n + 2]
        start, finish = _gather_plan(rest[:n], rest[n + 2:2 * n + 2], *rest[2 * n + 2:])
        i = pl.program_id(1)
        pl.when((pl.program_id(0) == 0) & (i == 0))(start)
        row = lax.broadcasted_iota(jnp.int32, (t, t), 0)
        col = lax.broadcasted_iota(jnp.int32, (t, t), 1)
        lo = lax.broadcasted_iota(jnp.int32, (t, LANES), 1) < HEAD_DIM
        qs = (qa_ref[0], qa_ref[1])

        def block(j, carry, masked):
            rows = pl.ds(pl.multiple_of(j * t, t), t)
            v = vb_ref[rows, :]
            new = []
            for hh in range(2):
                m, l, acc = carry[hh]
                s = _dot(qs[hh], ka_ref[hh, rows, :], NT)
                if masked:
                    s = jnp.where(row >= col, s, NEG)
                m_new = jnp.maximum(m, jnp.max(s, axis=1, keepdims=True))
                alpha = jnp.exp(m - m_new)
                p = jnp.exp(s - m_new)
                new.append((m_new, alpha * l + jnp.sum(p, axis=1, keepdims=True), alpha * acc + _dot(p.astype(BF16), v)))
            return tuple(new)

        init = (jnp.full((t, 1), NEG, F32), jnp.zeros((t, 1), F32), jnp.zeros((t, LANES), F32))
        carry = lax.fori_loop(0, i, functools.partial(block, masked=False), (init, init))
        (m0, l0, acc0), (m1, l1, acc1) = block(i, carry, True)
        o_ref[...] = jnp.where(lo, acc0 / l0, acc1 / l1)
        lse_ref[...] = jnp.where(lo, m0 + jnp.log(l0), m1 + jnp.log(l1))
        pl.when((pl.program_id(0) == N_PAIRS - 1) & (i == nq - 1))(finish)

    out = pl.BlockSpec((t, LANES), lambda p, i: (i, p))
    res = pl.pallas_call(
        body, name="attn_fwd", grid=(N_PAIRS, nq),
        in_specs=[pl.BlockSpec((2, t, LANES), lambda p, i: (p, i, 0)), pl.BlockSpec((2, s_len, LANES), lambda p, i: (p, 0, 0)),
                  pl.BlockSpec((s_len, LANES), lambda p, i: (0, p))] + [ANY] * n,
        out_specs=[out, out] + [ANY] * n,
        out_shape=[jax.ShapeDtypeStruct((s_len, D_MODEL), F32), jax.ShapeDtypeStruct((s_len, D_MODEL), F32)]
        + [jax.ShapeDtypeStruct((N_CHIPS, *h.shape), h.dtype) for h in halves],
        scratch_shapes=_exchange_sems(n),
        compiler_params=_params(("arbitrary", "arbitrary")),
    )(qa, ka, vb, *halves)
    return res[0], res[1], res[2:]


def _attn_bwd(qa, ka, vb, o, lse, do, parts, dproj):
    s_len = vb.shape[0]
    t = min(ATT_T, s_len)
    nq = s_len // t
    n = len(parts)

    def body(qa_ref, ka_ref, vb_ref, o_ref, lse_ref, do_ref, *rest):
        dqa_ref, dka_ref, dv_ref = rest[n + 1:n + 4]
        start, finish = _reduce_plan(rest[:n], rest[n + 4:2 * n + 4], *rest[2 * n + 4:])
        j = pl.program_id(1)
        pl.when((pl.program_id(0) == 0) & (j == 0))(start)

        @pl.when(j == 0)
        def _():
            dqa_ref[...] = jnp.zeros_like(dqa_ref)

        row = lax.broadcasted_iota(jnp.int32, (t, t), 0)
        col = lax.broadcasted_iota(jnp.int32, (t, t), 1)
        lo = lax.broadcasted_iota(jnp.int32, (t, LANES), 1) < HEAD_DIM
        v = vb_ref[...]
        ks = (ka_ref[0], ka_ref[1])

        def block(i, carry, masked):
            dk, dv = list(carry[:2]), carry[2]
            rows = pl.ds(pl.multiple_of(i * t, t), t)
            do_p, o_p, lse_p = do_ref[rows, :], o_ref[rows, :], lse_ref[rows, :]
            for hh in range(2):
                q = qa_ref[hh, rows, :]
                do_h = jnp.where(lo == (hh == 0), do_p, 0.0)
                delta = jnp.sum(do_h * o_p, axis=1, keepdims=True)
                s = _dot(q, ks[hh], NT)
                if masked:
                    s = jnp.where(row >= col, s, NEG)
                p = jnp.exp(s - lse_p[:, hh * HEAD_DIM:hh * HEAD_DIM + 1])
                do_b = do_h.astype(BF16)
                ds = (p * (_dot(do_b, v, NT) - delta)).astype(BF16)
                dv = dv + _dot(p.astype(BF16), do_b, TN)
                dk[hh] = dk[hh] + _dot(ds, q, TN)
                dqa_ref[hh, rows, :] += _dot(ds, ks[hh])
            return dk[0], dk[1], dv

        zero = jnp.zeros((t, LANES), F32)
        carry = block(j, (zero, zero, zero), True)
        dk0, dk1, dv = lax.fori_loop(j + 1, nq, functools.partial(block, masked=False), carry)
        dka_ref[0] = dk0
        dka_ref[1] = dk1
        dv_ref[...] = dv.astype(dv_ref.dtype)
        pl.when((pl.program_id(0) == N_PAIRS - 1) & (j == nq - 1))(finish)

    whole_pair = pl.BlockSpec((2, s_len, LANES), lambda p, j: (p, 0, 0))
    blk_pair = pl.BlockSpec((2, t, LANES), lambda p, j: (p, j, 0))
    whole_cols = pl.BlockSpec((s_len, LANES), lambda p, j: (0, p))
    blk_cols = pl.BlockSpec((t, LANES), lambda p, j: (j, p))
    res = pl.pallas_call(
        body, name="attn_bwd", grid=(N_PAIRS, nq),
        in_specs=[whole_pair, blk_pair, blk_cols, whole_cols, whole_cols, whole_cols] + [ANY] * (n + 1),
        out_specs=[whole_pair, blk_pair, pl.BlockSpec((t, LANES), lambda p, j: (j, AL_V // LANES + p))] + [ANY] * n,
        out_shape=[jax.ShapeDtypeStruct((N_HEADS, s_len, LANES), F32), jax.ShapeDtypeStruct((N_HEADS, s_len, LANES), F32),
                   jax.ShapeDtypeStruct(dproj.shape, dproj.dtype)]
        + [jax.ShapeDtypeStruct((N_DEV, g.shape[1] // 2, g.shape[2]), g.dtype) for g in parts],
        scratch_shapes=_exchange_sems(n), input_output_aliases={6 + n: 2},
        compiler_params=_params(("arbitrary", "arbitrary")),
    )(qa, ka, vb, o, lse, do, *parts, dproj)
    return res[0], res[1], res[2], res[3:]


def _attn_post(dqa, dka, dproj):
    s_len = dqa.shape[1]
    tr = min(256, s_len)
    assert AL_K == AL_Q + D_MODEL and AL_Q % (2 * D_MODEL) == 0

    def body(dqa_ref, dka_ref, _, dqk_ref, dcum_ref):
        lane = lax.broadcasted_iota(jnp.int32, (tr, LANES), 1)
        lo = lane < HEAD_DIM
        dcum = jnp.zeros((tr, LANES), F32)
        for p in range(N_PAIRS):
            a0, a1, b0, b1 = dqa_ref[2 * p], dqa_ref[2 * p + 1], dka_ref[2 * p], dka_ref[2 * p + 1]
            dq = jnp.where(lo, a0, pltpu.roll(a1, HEAD_DIM, 1)) * (HEAD_DIM ** -0.5)
            dqk_ref[:, p * LANES:(p + 1) * LANES] = dq.astype(dqk_ref.dtype)
            dqk_ref[:, D_MODEL + p * LANES:D_MODEL + (p + 1) * LANES] = jnp.where(lo, b0, pltpu.roll(b1, HEAD_DIM, 1)).astype(dqk_ref.dtype)
            for hh, (a, b) in enumerate(((a0, b0), (a1, b1))):
                dcum = dcum + jnp.where(lane == 2 * p + hh, a[:, AUG_C:AUG_C + 1] - b[:, AUG_ONE:AUG_ONE + 1], 0.0)
        dcum_ref[...] = dcum

    heads = pl.BlockSpec((N_HEADS, tr, LANES), lambda i: (0, i, 0))
    return pl.pallas_call(
        body, name="attn_post", grid=(s_len // tr,),
        in_specs=[heads, heads, ANY],
        out_specs=[pl.BlockSpec((tr, 2 * D_MODEL), lambda i: (i, AL_Q // (2 * D_MODEL))), pl.BlockSpec((tr, LANES), lambda i: (i, 0))],
        out_shape=[jax.ShapeDtypeStruct(dproj.shape, dproj.dtype), jax.ShapeDtypeStruct((s_len, LANES), F32)],
        input_output_aliases={2: 0}, compiler_params=_params(("parallel",)),
    )(dqa, dka, dproj)


def _ln_stats(r):
    mu = _rowmean(r)
    xc = r - mu
    rstd = lax.rsqrt(_rowmean(xc * xc) + LN_EPS)
    return xc * rstd, rstd


def _ln_bwd(dxh, xh, rstd):
    return rstd * (dxh - _rowmean(dxh) - xh * _rowmean(dxh * xh))


def _rms_bwd(dgn, g, r):
    return r * dgn - (r * r * r) * g * _rowmean(dgn * g)


def _to_aligned(wt):
    pad = jnp.zeros((AL_COLS - IN_COLS, wt.shape[1]), wt.dtype)
    return jnp.concatenate([wt[:2048], wt[2576:5648], wt[2048:2560], wt[2560:2576], wt[5648:5664], pad], axis=0)


def _from_aligned(gt):
    return jnp.concatenate([gt[:AL_Q], gt[AL_B:AL_DTF], gt[AL_DTF:AL_DTF + 16], gt[AL_Q:AL_B], gt[AL_DTF + 16:AL_DTF + 32]], axis=0)


def _lanes(v, at=0):
    return jnp.pad(v, ((0, 0), (at, LANES - at - v.shape[1])))


def _local_step(x, tgt, mod, w_alt, halves, sp):
    d = D_MODEL
    sh1, sc1, g1, sh2, sc2, g2 = [mod[:, i * d:(i + 1) * d] for i in range(6)]
    dt_bias_l, a_log_l, f_bias_l = _lanes(sp["dt_bias"]), _lanes(sp["a_log"]), _lanes(sp["f_bias"], F_LANE)
    d_exp = jnp.repeat(sp["d_skip"], HEAD_DIM, axis=1)
    z_slab = lambda a: (a, d, AL_Z // d)

    (h1,), _ = _rowwise("mod1", lambda x, sc, sh: ([x * (1.0 + sc) + sh], []), [x], [sc1, sh1], [(d, BF16)], [])
    proj = _matmul("proj", h1, w_alt, dims=NT, tn=1152)
    xc_all = _conv_fwd(proj, sp["conv_w"], sp["conv_b"])
    y_ssd, prevs = _ssd_fwd(xc_all, proj, dt_bias_l, a_log_l, d_exp)

    def gated_norm(y, z, w):
        g = y * _silu(z)
        return [g * lax.rsqrt(_rowmean(g * g) + RMS_EPS) * w], []

    (y_mix,), _ = _rowwise("ssm_norm", gated_norm, [y_ssd, z_slab(proj)], [sp["ssm_norm_w"]], [(d, BF16, (2 * d, 0, None))], [])
    cum = _fox_cum(proj, f_bias_l)
    qa, ka, vb = _attn_prep(proj, cum)
    o, lse, (g_out, g_fi, g_fo) = _attn_fwd(qa, ka, vb, halves)
    w_out = g_out.reshape(2 * d, d)
    w_fi = g_fi.transpose(1, 0, 2).reshape(d, D_FF)
    w_fo = g_fo.reshape(D_FF, d)
    (y_mix,), _ = _rowwise("attn_norm", lambda o, w: ([o * lax.rsqrt(_rowmean(o * o) + RMS_EPS) * w], []),
                           [o], [sp["attn_norm_w"]], [(d, BF16, (2 * d, 1, y_mix))], [])
    def ln1_fwd(y, x, g1, sc2, sh2, lg, lb):
        r1 = ALPHA * x + (1.0 + g1) * y
        xh, _ = _ln_stats(r1)
        x1 = xh * lg + lb
        return [y, r1, x1 * (1.0 + sc2) + sh2], []

    y, r1, h2 = _matmul("out_proj", y_mix, w_out, tm=512, tk=2048,
                        epi=(ln1_fwd, [x], [g1, sc2, sh2, sp["ln1_g"], sp["ln1_b"]], [F32, F32, BF16], []))
    u, act = _matmul("ff_in", h2, w_fi, epi=(lambda u: ([u, jnp.square(jnp.maximum(u, 0.0))], []), [], [], [F32, BF16], []))

    def head(ff, r1, tgt, g2, l1g, l1b, l2g, l2b):
        xh1, _ = _ln_stats(r1)
        x1 = xh1 * l1g + l1b
        xh2, rstd2 = _ln_stats(ALPHA * x1 + (1.0 + g2) * ff)
        err = xh2 * l2g + l2b - tgt
        loss = 0.5 * jnp.sum(_rowmean(err * err))
        dx2 = err * (1.0 / d)
        dr2 = _ln_bwd(dx2 * l2g, xh2, rstd2)
        return ([dr2, (1.0 + g2) * dr2],
                [_colsum(dx2 * xh2), _colsum(dx2), _colsum(dr2 * ff), jnp.full((1, LANES), loss, F32)])

    dr2, dff, d_ln2_g, d_ln2_b, d_g2, loss = _matmul(
        "ff_out", act, w_fo, tm=512, tk=2048,
        epi=(head, [r1, tgt], [g2, sp["ln1_g"], sp["ln1_b"], sp["ln2_g"], sp["ln2_b"]], [F32, BF16], [d, d, d, LANES]))
    du = _matmul("d_act", dff, w_fo, dims=NT, epi=(lambda da, u: ([da * (2.0 * jnp.maximum(u, 0.0))], []), [u], [], [BF16], []))
    dw_fo = _matmul("dw_ff_out", act, dff, dims=TN, out_dtype=BF16, by_chip="rows")
    dw_fi = _matmul("dw_ff_in", h2, du, dims=TN, out_dtype=BF16, by_chip="cols")

    def ln1_bwd(dh2, r1, dr2, y, sc2, g1, lg, lb):
        xh, rstd = _ln_stats(r1)
        x1 = xh * lg + lb
        dx1 = ALPHA * dr2 + dh2 * (1.0 + sc2)
        dr1 = _ln_bwd(dx1 * lg, xh, rstd)
        return ([dr1, (1.0 + g1) * dr1],
                [_colsum(dh2 * x1), _colsum(dh2), _colsum(dx1 * xh), _colsum(dx1), _colsum(dr1 * y)])

    dr1, dy, d_sc2, d_sh2, d_ln1_g, d_ln1_b, d_g1 = _matmul(
        "dh2", du, w_fi, dims=NT, tm=512, tk=2048,
        epi=(ln1_bwd, [r1, dr2, y], [sc2, g1, sp["ln1_g"], sp["ln1_b"]], [F32, BF16], [d] * 5))
    dymix = _matmul("dy_mix", dy, w_out, dims=NT)
    dw_out = _matmul("dw_out", y_mix, dy, dims=TN, out_dtype=BF16, by_chip="rows")

    def attn_norm_bwd(o, dyo, w):
        r = lax.rsqrt(_rowmean(o * o) + RMS_EPS)
        return [_rms_bwd(dyo * w, o, r)], [_colsum(dyo * o * r)]

    (do,), (d_attn_w,) = _rowwise("attn_norm_bwd", attn_norm_bwd, [o, (dymix, d, 1)], [sp["attn_norm_w"]], [(d, F32)], [d])

    def gated_norm_bwd(y, z, dyo, w):
        sg = _sigmoid(z)
        sz = z * sg
        g = y * sz
        r = lax.rsqrt(_rowmean(g * g) + RMS_EPS)
        dg = _rms_bwd(dyo * w, g, r)
        return [dg * sz, dg * y * (sg * (1.0 + z * (1.0 - sg)))], [_colsum(dyo * g * r)]

    (dy_ssd, dproj), (d_ssm_w,) = _rowwise("ssm_norm_bwd", gated_norm_bwd, [y_ssd, z_slab(proj), (dymix, d, 0)],
                                           [sp["ssm_norm_w"]], [(d, F32), (d, BF16, (AL_COLS, AL_Z // d, None))], [d])
    dqa, dka, dproj, landed = _attn_bwd(qa, ka, vb, o, lse, do, [dw_out, dw_fi, dw_fo], dproj)
    dproj, dcum = _attn_post(dqa, dka, dproj)
    dxc, ddt_tile, d_alog_l, d_dexp, d_dtb_l = _ssd_bwd(xc_all, proj, dt_bias_l, a_log_l, d_exp, prevs, dy_ssd)
    dproj, d_fb_l = _fox_cum_bwd(dcum, proj, f_bias_l, ddt_tile, dproj)
    dpre, d_conv_w, d_conv_b = _conv_bwd_pre(proj, sp["conv_w"], sp["conv_b"], dxc)
    dproj = _conv_bwd_in(dpre, sp["conv_w"], dproj)
    dw_alt = _matmul("dw_in", dproj, h1, dims=TN, tm=1152, out_dtype=BF16)
    part_in = _from_aligned(dw_alt).reshape(N_CHIPS, IN_COLS // N_CHIPS, d)

    def last(dh1, x, dr1, sc1):
        return [ALPHA * dr1 + dh1 * (1.0 + sc1)], [_colsum(dh1 * x), _colsum(dh1)]

    chip_in = _pair_sum(part_in, _pair_exchange(part_in), lax.axis_index("c"))
    dx, d_sc1, d_sh1, landed_in = _matmul("dh1", dproj, w_alt, tm=512, tk=1152, carry=[chip_in],
                                          epi=(last, [x, dr1], [sc1], [F32], [d, d]))

    small = {
        "mod": jnp.concatenate([d_sh1, d_sc1, d_g1, d_sh2, d_sc2, d_g2], axis=1),
        "conv_w": d_conv_w, "conv_b": d_conv_b,
        "dt_bias": d_dtb_l[:, :N_HEADS], "a_log": d_alog_l[:, :N_HEADS],
        "d_skip": jnp.sum(d_dexp.reshape(N_HEADS, HEAD_DIM), axis=1)[None, :],
        "ssm_norm_w": d_ssm_w, "f_bias": d_fb_l[:, F_LANE:F_LANE + N_HEADS], "attn_norm_w": d_attn_w,
        "ln1_g": d_ln1_g, "ln1_b": d_ln1_b, "ln2_g": d_ln2_g, "ln2_b": d_ln2_b, "loss": loss,
    }
    return dx, [landed_in, *landed], small


N_DEV = 8
N_CHIPS = 4
ANY = pl.BlockSpec(memory_space=pl.ANY)
VMEM_SPEC = pl.BlockSpec(memory_space=pltpu.VMEM)


def _place():
    x, y, c = lax.axis_index("x"), lax.axis_index("y"), lax.axis_index("c")
    return x, y, c


def _other_chips(x, y):
    return [(1 - x, y, 2 * (1 - x) + y), (x, 1 - y, 2 * x + 1 - y), (1 - x, 1 - y, 2 * (1 - x) + 1 - y)]


def _allgather_small(name, v):
    r, cdim = v.shape

    def body(v_ref, out_ref, send_sems, recv_sems, local_sem):
        x, y, c = _place()
        me = 4 * x + 2 * y + c
        mine = pltpu.make_async_copy(v_ref, out_ref.at[me], local_sem)
        mine.start()
        peers = []
        for rel in range(1, N_DEV):
            px = 1 - x if rel & 4 else x
            py = 1 - y if rel & 2 else y
            pc = 1 - c if rel & 1 else c
            peers.append((px, py, pc))

        def copy(rel, slot, to):
            return pltpu.make_async_remote_copy(src_ref=v_ref, dst_ref=out_ref.at[slot], send_sem=send_sems.at[rel],
                                                recv_sem=recv_sems.at[rel], device_id=to, device_id_type=MESH)

        sends = [copy(rel, me, peer) for rel, peer in enumerate(peers)]
        for cp in sends:
            cp.start()
        for rel, (px, py, pc) in enumerate(peers):
            copy(rel, 4 * px + 2 * py + pc, (x, y, c)).wait_recv()
        for cp in sends:
            cp.wait_send()
        mine.wait()

    return pl.pallas_call(
        body, name=name, out_shape=jax.ShapeDtypeStruct((N_DEV, r, cdim), v.dtype),
        in_specs=[VMEM_SPEC], out_specs=VMEM_SPEC,
        scratch_shapes=[pltpu.SemaphoreType.DMA((N_DEV - 1,)), pltpu.SemaphoreType.DMA((N_DEV - 1,)), pltpu.SemaphoreType.DMA],
    )(v)


def _gather_shards(shard):
    r, cdim = shard.shape
    ch = cdim // 2

    def body(in_ref, out_ref, stage, send_sems, recv_sems, local_sems):
        x, y, c = _place()
        k_me = 2 * x + y
        me, sibling = (x, y, c), (x, y, 1 - c)
        chips = _other_chips(x, y)

        def copy(idx, k, half, to, src=None):
            rows = out_ref.at[k, :, pl.ds(pl.multiple_of(half * ch, ch), ch)]
            return pltpu.make_async_remote_copy(src_ref=rows if src is None else src, dst_ref=rows, send_sem=send_sems.at[idx],
                                                recv_sem=recv_sems.at[idx], device_id=to, device_id_type=MESH)

        mine = in_ref.at[:, pl.ds(pl.multiple_of(c * ch, ch), ch)]
        sends = [copy(j, k_me, c, (cx, cy, c), src=mine) for j, (cx, cy, _) in enumerate(chips)]
        for cp in sends:
            cp.start()
        load = pltpu.make_async_copy(in_ref, stage, local_sems.at[0])
        load.start()
        load.wait()
        store = pltpu.make_async_copy(stage, out_ref.at[k_me], local_sems.at[1])
        store.start()
        for j, (_, _, kj) in enumerate(chips):
            copy(j, kj, c, me).wait_recv()
            fwd = copy(3 + j, kj, c, sibling)
            fwd.start()
            sends.append(fwd)
        for j, (_, _, kj) in enumerate(chips):
            copy(3 + j, kj, 1 - c, me).wait_recv()
        for cp in sends:
            cp.wait_send()
        store.wait()

    return pl.pallas_call(
        body, name="gather_w_in", out_shape=jax.ShapeDtypeStruct((N_CHIPS, r, cdim), shard.dtype),
        in_specs=[ANY], out_specs=ANY,
        scratch_shapes=[pltpu.VMEM((r, cdim), shard.dtype), pltpu.SemaphoreType.DMA((6,)), pltpu.SemaphoreType.DMA((6,)),
                        pltpu.SemaphoreType.DMA((2,))],
        compiler_params=_params(),
    )(shard)


def _peers(x, y, c):
    return [((1 - x) if rel & 4 else x, (1 - y) if rel & 2 else y, (1 - c) if rel & 1 else c) for rel in range(1, N_DEV)]


def _exchange_sems(n):
    return [pltpu.SemaphoreType.DMA((n, N_DEV - 1)), pltpu.SemaphoreType.DMA((n, N_DEV - 1)), pltpu.SemaphoreType.DMA((n,))]


def _gather_plan(ins, outs, send_sems, recv_sems, local_sems):
    x, y, c = _place()
    k_me = 2 * x + y
    peers = [(rel, p) for rel, p in enumerate(_peers(x, y, c)) if (rel + 1) & 6]

    def copy(w, rel, k, half, to, src=None):
        rh = ins[w].shape[0] // 2
        rows = outs[w].at[k, pl.ds(pl.multiple_of(half * rh, rh), rh), :]
        return pltpu.make_async_remote_copy(src_ref=rows if src is None else src, dst_ref=rows, send_sem=send_sems.at[w, rel],
                                            recv_sem=recv_sems.at[w, rel], device_id=to, device_id_type=MESH)

    def mine(w):
        rh = ins[w].shape[0] // 2
        return ins[w].at[pl.ds(pl.multiple_of(c * rh, rh), rh), :]

    n = len(ins)
    local = [pltpu.make_async_copy(ins[w], outs[w].at[k_me], local_sems.at[w]) for w in range(n)]
    sends = [copy(w, rel, k_me, c, peer, src=mine(w)) for w in range(n) for rel, peer in peers]

    def start():
        for cp in local + sends:
            cp.start()

    def finish():
        for w in range(n):
            for rel, (px, py, pc) in peers:
                copy(w, rel, 2 * px + py, pc, (x, y, c)).wait_recv()
        for cp in sends:
            cp.wait_send()
        for cp in local:
            cp.wait()

    return start, finish


def _reduce_plan(ins, outs, send_sems, recv_sems, local_sems):
    x, y, c = _place()
    me = 4 * x + 2 * y + c
    peers = _peers(x, y, c)

    def block(w, k, half):
        rh = ins[w].shape[1] // 2
        return ins[w].at[k, pl.ds(pl.multiple_of(half * rh, rh), rh), :]

    def copy(w, rel, src, slot, to):
        return pltpu.make_async_remote_copy(src_ref=src, dst_ref=outs[w].at[slot], send_sem=send_sems.at[w, rel],
                                            recv_sem=recv_sems.at[w, rel], device_id=to, device_id_type=MESH)

    n = len(ins)
    local = [pltpu.make_async_copy(block(w, 2 * x + y, c), outs[w].at[me], local_sems.at[w]) for w in range(n)]
    sends = [copy(w, rel, block(w, 2 * px + py, pc), me, (px, py, pc)) for w in range(n) for rel, (px, py, pc) in enumerate(peers)]

    def start():
        for cp in local + sends:
            cp.start()

    def finish():
        for w in range(n):
            for rel, (px, py, pc) in enumerate(peers):
                copy(w, rel, block(w, 2 * x + y, c), 4 * px + 2 * py + pc, (x, y, c)).wait_recv()
        for cp in sends:
            cp.wait_send()
        for cp in local:
            cp.wait()

    return start, finish


def _scatter_plan(ins, outs, send_sems, recv_sems, local_sems):
    x, y, c = _place()
    k_me = 2 * x + y
    chips = _other_chips(x, y)

    def copy(w, j, src_k, dst_k, to):
        return pltpu.make_async_remote_copy(src_ref=ins[w].at[src_k], dst_ref=outs[w].at[dst_k], send_sem=send_sems.at[w, j],
                                            recv_sem=recv_sems.at[w, j], device_id=to, device_id_type=MESH)

    n = len(ins)
    local = [pltpu.make_async_copy(ins[w].at[k_me], outs[w].at[k_me], local_sems.at[w]) for w in range(n)]
    sends = [copy(w, j, kj, k_me, (cx, cy, c)) for w in range(n) for j, (cx, cy, kj) in enumerate(chips)]

    def start():
        for cp in local + sends:
            cp.start()

    def finish():
        for w in range(n):
            for j, (_, _, kj) in enumerate(chips):
                copy(w, j, k_me, kj, (x, y, c)).wait_recv()
        for cp in sends:
            cp.wait_send()
        for cp in local:
            cp.wait()

    return start, finish


def _row_tile(r, mult=2 * SUBLANES):
    if r % 256 == 0:
        return 256
    return max([t for t in range(mult, 513, mult) if r % t == 0], default=r)


def _pair_exchange(g):
    _, r, cdim = g.shape
    ch = cdim // 2

    def body(g_ref, got_ref, send_sem, recv_sem):
        x, y, c = _place()
        cp = pltpu.make_async_remote_copy(src_ref=g_ref.at[:, :, pl.ds(pl.multiple_of((1 - c) * ch, ch), ch)], dst_ref=got_ref,
                                          send_sem=send_sem, recv_sem=recv_sem, device_id=(x, y, 1 - c), device_id_type=MESH)
        cp.start()
        cp.wait_recv()
        cp.wait_send()

    return pl.pallas_call(
        body, name="pair_exchange", out_shape=jax.ShapeDtypeStruct((N_CHIPS, r, ch), g.dtype),
        in_specs=[ANY], out_specs=ANY, scratch_shapes=[pltpu.SemaphoreType.DMA, pltpu.SemaphoreType.DMA],
    )(g)


def _pair_sum(g, got, c):
    _, r, cdim = g.shape
    ch = cdim // 2
    tr = _row_tile(r)

    def body(c_ref, g_ref, got_ref, o_ref):
        o_ref[...] = (g_ref[...].astype(F32) + got_ref[...].astype(F32)).astype(o_ref.dtype)

    blk = pl.BlockSpec((1, tr, ch), lambda k, i, c_ref: (k, i, 0))
    return pl.pallas_call(
        body, name="pair_sum",
        grid_spec=pltpu.PrefetchScalarGridSpec(
            num_scalar_prefetch=1, grid=(N_CHIPS, r // tr),
            in_specs=[pl.BlockSpec((1, tr, ch), lambda k, i, c_ref: (k, i, c_ref[0])), blk], out_specs=blk),
        out_shape=jax.ShapeDtypeStruct((N_CHIPS, r, ch), BF16),
        compiler_params=_params(("parallel", "parallel")),
    )(jnp.reshape(c, (1,)).astype(jnp.int32), g, got)


def _sum_blocks(name, parts):
    k, r, cdim = parts.shape
    tr = _row_tile(r)

    def body(p_ref, o_ref):
        acc = p_ref[0].astype(F32)
        for i in range(1, k):
            acc = acc + p_ref[i].astype(F32)
        o_ref[...] = acc

    return pl.pallas_call(
        body, name=name, grid=(r // tr,),
        in_specs=[pl.BlockSpec((k, tr, cdim), lambda i: (0, i, 0))], out_specs=pl.BlockSpec((tr, cdim), lambda i: (i, 0)),
        out_shape=jax.ShapeDtypeStruct((r, cdim), F32), compiler_params=_params(("parallel",)),
    )(parts)


def _pair_swap(halves):
    n = len(halves)

    def body(*refs):
        ins, outs = refs[:n], refs[n:2 * n]
        send_sems, recv_sems = refs[2 * n:]
        x, y, c = _place()
        cps = [pltpu.make_async_remote_copy(src_ref=ins[w], dst_ref=outs[w], send_sem=send_sems.at[w], recv_sem=recv_sems.at[w],
                                            device_id=(x, y, 1 - c), device_id_type=MESH) for w in range(n)]
        for cp in cps:
            cp.start()
        for cp in cps:
            cp.wait_recv()
        for cp in cps:
            cp.wait_send()

    return pl.pallas_call(
        body, name="pair_swap", out_shape=[jax.ShapeDtypeStruct(h.shape, h.dtype) for h in halves],
        in_specs=[ANY] * n, out_specs=[ANY] * n,
        scratch_shapes=[pltpu.SemaphoreType.DMA((n,)), pltpu.SemaphoreType.DMA((n,))],
    )(*halves)


ADA_SHARD = 6 * D_MODEL // N_CHIPS


def _mod_part(c_all, w_shard, b_shard):
    tn = 512

    def body(c_ref, w_ref, b_ref, o_ref):
        o_ref[...] = _dot(_silu(c_ref[...]).astype(BF16), w_ref[...].astype(BF16)) + b_ref[...]

    return pl.pallas_call(
        body, name="mod_part", grid=(ADA_SHARD // tn,),
        in_specs=[pl.BlockSpec((N_DEV, D_MODEL), lambda j: (0, 0)), pl.BlockSpec((D_MODEL, tn), lambda j: (0, j)),
                  pl.BlockSpec((1, tn), lambda j: (0, j))],
        out_specs=pl.BlockSpec((N_DEV, tn), lambda j: (0, j)),
        out_shape=jax.ShapeDtypeStruct((N_DEV, ADA_SHARD), F32), compiler_params=_params(("parallel",)),
    )(c_all, w_shard, b_shard)


def _w_ada_grad(c_all_t, dmod_shard):
    tm = 256

    def body(ct_ref, dm_ref, o_ref):
        act = _silu(ct_ref[...])
        acc = act[:, 0:1] * dm_ref[0:1, :]
        for dev in range(1, N_DEV):
            acc = acc + act[:, dev:dev + 1] * dm_ref[dev:dev + 1, :]
        o_ref[...] = acc

    return pl.pallas_call(
        body, name="w_ada_grad", grid=(D_MODEL // tm,),
        in_specs=[pl.BlockSpec((tm, N_DEV), lambda i: (i, 0)), pl.BlockSpec((N_DEV, ADA_SHARD), lambda i: (0, 0))],
        out_specs=pl.BlockSpec((tm, ADA_SHARD), lambda i: (i, 0)),
        out_shape=jax.ShapeDtypeStruct((D_MODEL, ADA_SHARD), F32), compiler_params=_params(("parallel",)),
    )(c_all_t, dmod_shard)


def _adamw_math(w, g, m, v):
    nm = ADAM_B1 * m + (1.0 - ADAM_B1) * g
    nv = ADAM_B2 * v + (1.0 - ADAM_B2) * jnp.square(g)
    m_hat = nm / (1.0 - ADAM_B1 ** ADAM_STEP)
    v_hat = nv / (1.0 - ADAM_B2 ** ADAM_STEP)
    return -ADAM_LR * (m_hat / (jnp.sqrt(v_hat) + ADAM_EPS) + ADAM_WD * w), nm, nv


def _adamw(name, w, g, m, v):
    _, r, cdim = w.shape
    tr = 256 if r % 256 == 0 else r

    def body(w_ref, g_ref, m_ref, v_ref, go_ref, d_ref, nm_ref, nv_ref):
        go_ref[...] = g_ref[...]
        d_ref[...], nm_ref[...], nv_ref[...] = _adamw_math(w_ref[...], g_ref[...], m_ref[...], v_ref[...])

    blk = pl.BlockSpec((None, tr, cdim), lambda i: (0, i, 0))
    return pl.pallas_call(
        body, name=name, grid=(r // tr,), in_specs=[blk, pl.BlockSpec((tr, cdim), lambda i: (i, 0)), blk, blk], out_specs=[blk] * 4,
        out_shape=[jax.ShapeDtypeStruct((1, r, cdim), F32)] * 4, compiler_params=_params(("parallel",)),
    )(w, g, m, v)


def _adamw_pair(name, w, mine, other, m, v, c, by_cols=False):
    _, r, cdim = w.shape
    hr, hc = mine.shape
    tr = _row_tile(hr, SUBLANES)
    per = hr // tr

    def body(c_ref, w_ref, a_ref, b_ref, m_ref, v_ref, g_ref, d_ref, nm_ref, nv_ref):
        half = pl.program_id(1) if by_cols else pl.program_id(0) // per
        g = jnp.where(half == c_ref[0], a_ref[...], b_ref[...])
        g_ref[...] = g
        d_ref[...], nm_ref[...], nv_ref[...] = _adamw_math(w_ref[...], g, m_ref[...], v_ref[...])

    blk = pl.BlockSpec((None, tr, hc), lambda i, j, c_ref: (0, i, j))
    half = pl.BlockSpec((tr, hc), lambda i, j, c_ref: (i % per, 0))
    return pl.pallas_call(
        body, name=name,
        grid_spec=pltpu.PrefetchScalarGridSpec(num_scalar_prefetch=1, grid=(r // tr, cdim // hc),
                                               in_specs=[blk, half, half, blk, blk], out_specs=[blk] * 4),
        out_shape=[jax.ShapeDtypeStruct((1, r, cdim), F32)] * 4, compiler_params=_params(("parallel", "parallel")),
    )(jnp.reshape(c, (1,)).astype(jnp.int32), w, mine, other, m, v)


SMALL = ["b_ada", "conv_b", "dt_bias", "a_log", "d_skip", "ssm_norm_w", "f_bias", "attn_norm_w", "ln1_g", "ln1_b", "ln2_g", "ln2_b"]


def _pack(vs):
    pieces = []
    for v in vs:
        pieces.append(v)
        if v.shape[1] % LANES:
            pieces.append(jnp.zeros((1, -v.shape[1] % LANES), v.dtype))
    return jnp.concatenate(pieces, axis=1)


def _adamw_small(total, offs, ws, ms, vs):
    n = len(ws)

    def body(*refs):
        t_ref, outs = refs[0], refs[1 + 3 * n:]
        for i in range(n):
            g = t_ref[:, offs[i]:offs[i] + ws[i].shape[1]]
            dl, nm, nv = _adamw_math(refs[1 + i][...], g, refs[1 + n + i][...], refs[1 + 2 * n + i][...])
            outs[4 * i][...], outs[4 * i + 1][...], outs[4 * i + 2][...], outs[4 * i + 3][...] = g, dl, nm, nv

    res = pl.pallas_call(
        body, name="adamw_small", in_specs=[VMEM_SPEC] * (1 + 3 * n), out_specs=[VMEM_SPEC] * (4 * n),
        out_shape=[jax.ShapeDtypeStruct(w.shape, F32) for w in ws for _ in range(4)],
    )(total, *ws, *ms, *vs)
    return [res[4 * i:4 * i + 4] for i in range(n)]


def kernel(x, c, w_ada, b_ada, w_in, conv_w, conv_b, dt_bias, a_log, d_skip, ssm_norm_w, f_bias, attn_norm_w, w_out, ln1_g, ln1_b, w_ff_in, w_ff_out, ln2_g, ln2_b, loss_target, m_w_ada, m_b_ada, m_w_in, m_conv_w, m_conv_b, m_dt_bias, m_a_log, m_d_skip, m_ssm_norm_w, m_f_bias, m_attn_norm_w, m_w_out, m_ln1_g, m_ln1_b, m_w_ff_in, m_w_ff_out, m_ln2_g, m_ln2_b, v_w_ada, v_b_ada, v_w_in, v_conv_w, v_conv_b, v_dt_bias, v_a_log, v_d_skip, v_ssm_norm_w, v_f_bias, v_attn_norm_w, v_w_out, v_ln1_g, v_ln1_b, v_w_ff_in, v_w_ff_out, v_ln2_g, v_ln2_b):
    a = dict(b_ada=b_ada, conv_b=conv_b, dt_bias=dt_bias, a_log=a_log, d_skip=d_skip, ssm_norm_w=ssm_norm_w, f_bias=f_bias,
             attn_norm_w=attn_norm_w, ln1_g=ln1_g, ln1_b=ln1_b, ln2_g=ln2_g, ln2_b=ln2_b)
    ms = dict(b_ada=m_b_ada, conv_b=m_conv_b, dt_bias=m_dt_bias, a_log=m_a_log, d_skip=m_d_skip, ssm_norm_w=m_ssm_norm_w,
              f_bias=m_f_bias, attn_norm_w=m_attn_norm_w, ln1_g=m_ln1_g, ln1_b=m_ln1_b, ln2_g=m_ln2_g, ln2_b=m_ln2_b)
    vs = dict(b_ada=v_b_ada, conv_b=v_conv_b, dt_bias=v_dt_bias, a_log=v_a_log, d_skip=v_d_skip, ssm_norm_w=v_ssm_norm_w,
              f_bias=v_f_bias, attn_norm_w=v_attn_norm_w, ln1_g=v_ln1_g, ln1_b=v_ln1_b, ln2_g=v_ln2_g, ln2_b=v_ln2_b)
    xi, yi, ci = _place()
    chip = 2 * xi + yi
    me = 4 * xi + 2 * yi + ci
    d = D_MODEL
    conv_shard = CONV_DIM // N_CHIPS

    first = _allgather_small("gather_c", jnp.concatenate([c, conv_w[0].reshape(1, CONV_W * conv_shard)], axis=1))[:, 0]
    c_all = first[:, :d]
    conv_w_full = first[::2, d:].reshape(N_CHIPS, CONV_W, conv_shard).transpose(1, 0, 2).reshape(CONV_W, CONV_DIM)
    b_shard = lax.dynamic_slice_in_dim(b_ada, chip * ADA_SHARD, ADA_SHARD, axis=1)
    parts = _allgather_small("gather_mod", _mod_part(c_all, w_ada[0], b_shard))
    mod = lax.dynamic_index_in_dim(parts[::2], me, axis=1, keepdims=False).reshape(1, 6 * d)

    w_in_t, m_w_in_t, v_w_in_t = [jnp.transpose(t, (0, 2, 1)) for t in (w_in, m_w_in, v_w_in)]
    w_alt = _to_aligned(_gather_shards(w_in_t[0].astype(BF16)).reshape(IN_COLS, d))

    sp = {n: a[n] for n in SMALL[1:]}
    sp["conv_w"] = conv_w_full
    shards = [w_out[0].astype(BF16), w_ff_in[0].astype(BF16), w_ff_out[0].astype(BF16)]
    dx, landed, small = _local_step(x[0], loss_target[0], mod, w_alt, shards, sp)

    names = ["mod"] + SMALL[1:]
    vec = _pack([small[n] for n in names] + [small["conv_w"].reshape(1, CONV_W * CONV_DIM), small["loss"]])
    every = _allgather_small("gather_small", vec)
    total = _sum_blocks("sum_small", jnp.broadcast_to(every, (N_DEV, SUBLANES, vec.shape[1])))[:1]
    widths = [6 * d] + [a[n].shape[1] for n in SMALL[1:]]
    offs = [0]
    for w in widths:
        offs.append(offs[-1] + w + (-w % LANES))
    g_conv_w_full = total[:, offs[-1]:offs[-1] + CONV_W * CONV_DIM].reshape(CONV_W, CONV_DIM)
    loss = total[0, offs[-1] + CONV_W * CONV_DIM]
    dmod_shard = lax.dynamic_slice_in_dim(every[:, 0, :6 * d], chip * ADA_SHARD, ADA_SHARD, axis=1)
    g_w_ada = _w_ada_grad(c_all.T, dmod_shard)
    g_conv_w = lax.dynamic_slice_in_dim(g_conv_w_full, chip * conv_shard, conv_shard, axis=1)

    mine = [_sum_blocks("dev_sum_%d" % i, p) for i, p in enumerate(landed)]
    other = _pair_swap(mine)

    grads, deltas, new_m, new_v = {}, {}, {}, {}
    paired = dict(w_in=(w_in_t, m_w_in_t, v_w_in_t), w_out=(w_out, m_w_out, v_w_out), w_ff_in=(w_ff_in, m_w_ff_in, v_w_ff_in),
                  w_ff_out=(w_ff_out, m_w_ff_out, v_w_ff_out))
    for i, (n, (w, m, v)) in enumerate(paired.items()):
        res = _adamw_pair("adamw_" + n, w, mine[i], other[i], m, v, ci, by_cols=n == "w_in")
        grads[n], deltas[n], new_m[n], new_v[n] = [jnp.transpose(t, (0, 2, 1)) for t in res] if n == "w_in" else res
    for n, g, (w, m, v) in (("w_ada", g_w_ada, (w_ada, m_w_ada, v_w_ada)), ("conv_w", g_conv_w, (conv_w, m_conv_w, v_conv_w))):
        grads[n], deltas[n], new_m[n], new_v[n] = _adamw("adamw_" + n, w, g, m, v)
    for n, res in zip(SMALL, _adamw_small(total, offs, [a[n] for n in SMALL], [ms[n] for n in SMALL], [vs[n] for n in SMALL])):
        grads[n], deltas[n], new_m[n], new_v[n] = res

    order = ["w_ada", "b_ada", "w_in", "conv_w", "conv_b", "dt_bias", "a_log", "d_skip", "ssm_norm_w", "f_bias", "attn_norm_w", "w_out",
             "ln1_g", "ln1_b", "w_ff_in", "w_ff_out", "ln2_g", "ln2_b"]
    return (loss, dx[None], *[grads[n] for n in order], *[deltas[n] for n in order], *[new_m[n] for n in order], *[new_v[n] for n in order])
```

```python
import functools

import jax
import jax.numpy as jnp
from jax import lax
from jax.experimental import pallas as pl
from jax.experimental.pallas import tpu as pltpu

F32, BF16 = jnp.float32, jnp.bfloat16

D_MODEL = 1024
N_HEADS = 16
HEAD_DIM = 64
N_PAIRS = N_HEADS // 2
SSM_GROUPS = 2
SSM_STATE = 128
CHUNK = 128
CONV_W = 4
CONV_DIM = 1536
D_FF = 4096
IN_COLS = 5664
ALPHA = 2.0 ** 0.25
LN_EPS = 1e-5
RMS_EPS = 1e-5
LANES = 128
SUBLANES = 8

AL_Z, AL_XS, AL_Q, AL_K, AL_V, AL_B, AL_C, AL_DTF = 0, 1024, 2048, 3072, 4096, 5120, 5376, 5632
AL_COLS = 5760
F_LANE = 16

ADAM_LR, ADAM_B1, ADAM_B2, ADAM_EPS, ADAM_WD, ADAM_STEP = 0.001, 0.9, 0.999, 1e-08, 0.01, 10

VMEM_LIMIT = 56 * 1024 * 1024
MESH = pl.DeviceIdType.MESH


def _params(sem=None):
    return pltpu.CompilerParams(dimension_semantics=sem, vmem_limit_bytes=VMEM_LIMIT)


def _sigmoid(x):
    return 1.0 / (1.0 + jnp.exp(-x))


def _silu(x):
    return x * _sigmoid(x)


def _softplus(x):
    return jnp.maximum(x, 0.0) + jnp.log(1.0 + jnp.exp(-jnp.abs(x)))


def _split3(a):
    hi = a.astype(BF16)
    r = a - hi.astype(F32)
    mid = r.astype(BF16)
    lo = (r - mid.astype(F32)).astype(BF16)
    return hi, mid, lo


def _dot(a, b, dims=((1,), (0,))):
    return lax.dot_general(a, b, (dims, ((), ())), preferred_element_type=F32)


NN, NT, TN = ((1,), (0,)), ((1,), (1,)), ((0,), (0,))


def _dot3(t, a):
    hi, mid, lo = _split3(a)
    return _dot(t, hi) + _dot(t, mid) + _dot(t, lo)


def _matmul(name, a, b, *, dims=NN, out_dtype=F32, tm=1024, tn=1024, tk=1024, by_chip=None, epi=None, carry=()):
    if dims == NN:
        (m, k), n = a.shape, b.shape[1]
    elif dims == NT:
        (m, k), n = a.shape, b.shape[0]
    else:
        (k, m), n = a.shape, b.shape[1]
    if by_chip == "rows":
        tm = min(tm, m // 4)
    if by_chip == "cols":
        tn = min(tn, n // 4)
    tm, tn, tk = min(tm, m), min(tn, n), min(tk, k)
    assert m % tm == 0 and n % tn == 0 and k % tk == 0, (name, m, n, k, tm, tn, tk)
    nk = k // tk
    if by_chip == "rows":
        per = m // 4 // tm
        out_spec = pl.BlockSpec((None, tm, tn), lambda i, j, l: (i // per, i % per, j))
        out_shape = jax.ShapeDtypeStruct((4, m // 4, n), out_dtype)
    elif by_chip == "cols":
        per = n // 4 // tn
        out_spec = pl.BlockSpec((None, tm, tn), lambda i, j, l: (j // per, i, j % per))
        out_shape = jax.ShapeDtypeStruct((4, m, n // 4), out_dtype)
    else:
        out_spec = pl.BlockSpec((tm, tn), lambda i, j, l: (i, j))
        out_shape = jax.ShapeDtypeStruct((m, n), out_dtype)
    a_spec = pl.BlockSpec((tk, tm), lambda i, j, l: (l, i)) if dims == TN else pl.BlockSpec((tm, tk), lambda i, j, l: (i, l))
    b_spec = pl.BlockSpec((tn, tk), lambda i, j, l: (j, l)) if dims == NT else pl.BlockSpec((tk, tn), lambda i, j, l: (l, j))

    tile = pl.BlockSpec((tm, tn), lambda i, j, l: (i, j))
    in_specs, args, out_specs, out_shape = [a_spec, b_spec], [a, b], [out_spec], [out_shape]
    fn, n_tiles, n_sums = None, 1, 0
    if epi is not None:
        fn, fulls, vecs, outs, sums = epi
        assert by_chip is None and (not sums or n == tn), name
        in_specs = in_specs + [tile] * len(fulls) + [pl.BlockSpec((1, tn), lambda i, j, l: (0, j))] * len(vecs)
        args = args + list(fulls) + list(vecs)
        out_specs = [tile] * len(outs) + [pl.BlockSpec((1, w), lambda i, j, l: (0, 0)) for w in sums]
        out_shape = [jax.ShapeDtypeStruct((m, n), dt) for dt in outs] + [jax.ShapeDtypeStruct((1, w), F32) for w in sums]
        n_tiles, n_sums = len(outs), len(sums)
    n_in, n_out, n_c = len(args), len(out_specs), len(carry)
    scratch = [pltpu.VMEM((tm, tn) if nk > 1 else (SUBLANES, LANES), F32)]
    if n_c:
        in_specs, args = in_specs + [ANY] * n_c, args + list(carry)
        out_specs = out_specs + [ANY] * n_c
        out_shape = out_shape + [jax.ShapeDtypeStruct(g.shape, g.dtype) for g in carry]
        scratch = scratch + _exchange_sems(n_c)
    gm, gn = m // tm, n // tn

    def body(*refs):
        a_ref, b_ref = refs[:2]
        ins, outs = refs[2:n_in], refs[n_in + n_c:n_in + n_c + n_out]
        acc_ref = refs[n_in + 2 * n_c + n_out]
        i, j, l = pl.program_id(0), pl.program_id(1), pl.program_id(2)
        if n_c:
            start, wait = _scatter_plan(refs[n_in:n_in + n_c], refs[n_in + n_c + n_out:n_in + 2 * n_c + n_out], *refs[n_in + 2 * n_c + n_out + 1:])
            pl.when((i == 0) & (j == 0) & (l == 0))(start)
        part = _dot(a_ref[...].astype(BF16), b_ref[...].astype(BF16), dims)

        def finish(res):
            if fn is None:
                outs[0][...] = res.astype(outs[0].dtype)
                return
            tiles, colsums = fn(res, *[r[...] for r in ins])
            for r, val in zip(outs[:n_tiles], tiles):
                r[...] = val.astype(r.dtype)
            if n_sums:
                @pl.when(i == 0)
                def _():
                    for r in outs[n_tiles:]:
                        r[...] = jnp.zeros_like(r)
                for r, val in zip(outs[n_tiles:], colsums):
                    r[...] += val

        if nk == 1:
            finish(part)
        else:
            @pl.when(l == 0)
            def _():
                acc_ref[...] = part

            @pl.when((l > 0) & (l < nk - 1))
            def _():
                acc_ref[...] += part

            @pl.when(l == nk - 1)
            def _():
                finish(acc_ref[...] + part)

        if n_c:
            pl.when((i == gm - 1) & (j == gn - 1) & (l == nk - 1))(wait)

    res = pl.pallas_call(
        body, name=name, grid=(gm, gn, nk),
        in_specs=in_specs, out_specs=out_specs, out_shape=out_shape, scratch_shapes=scratch,
        compiler_params=_params(("arbitrary",) * 3 if n_c or n_sums else ("parallel", "parallel", "arbitrary")),
    )(*args)
    return res[0] if len(res) == 1 else res


def _rowwise(name, fn, fulls, vecs, out_fulls, out_vecs, tr=256):
    fulls = [f if isinstance(f, tuple) else (f, f.shape[1], 0) for f in fulls]
    s = fulls[0][0].shape[0]
    tr = min(tr, s)
    out_fulls = [o if len(o) == 3 else (*o, (o[0], 0, None)) for o in out_fulls]
    into = [(k, slab[2]) for k, (_, _, slab) in enumerate(out_fulls) if slab[2] is not None]
    nf, nv, nof, nov = len(fulls), len(vecs), len(out_fulls), len(out_vecs)
    in_specs = [pl.BlockSpec((tr, w), functools.partial(lambda i, cb: (i, cb), cb=cb)) for (_, w, cb) in fulls]
    in_specs += [pl.BlockSpec(v.shape, lambda i: (0, 0)) for v in vecs] + [ANY] * len(into)
    out_shape = [jax.ShapeDtypeStruct((s, slab[0]), dt) for (_, dt, slab) in out_fulls] + [jax.ShapeDtypeStruct((1, w), F32) for w in out_vecs]
    out_specs = [pl.BlockSpec((tr, w), functools.partial(lambda i, cb: (i, cb), cb=slab[1])) for (w, _, slab) in out_fulls]
    out_specs += [pl.BlockSpec((1, w), lambda i: (0, 0)) for w in out_vecs]

    def body(*refs):
        outs = refs[nf + nv + len(into):]
        of, ov = fn(*[r[...] for r in refs[:nf + nv]])
        for r, val in zip(outs[:nof], of):
            r[...] = val.astype(r.dtype)
        if nov:
            @pl.when(pl.program_id(0) == 0)
            def _():
                for r in outs[nof:]:
                    r[...] = jnp.zeros_like(r)
            for r, val in zip(outs[nof:], ov):
                r[...] += val

    res = pl.pallas_call(
        body, name=name, grid=(s // tr,), in_specs=in_specs, out_specs=out_specs, out_shape=out_shape,
        input_output_aliases={nf + nv + pos: k for pos, (k, _) in enumerate(into)},
        compiler_params=_params(("arbitrary",)),
    )(*[f[0] for f in fulls], *vecs, *[buf for _, buf in into])
    return res[:nof], res[nof:]


def _colsum(x):
    return jnp.sum(x, axis=0, keepdims=True)


def _rowmean(x):
    return jnp.mean(x, axis=-1, keepdims=True)


CONV_CB = 512
CONV_TR = 512


def _shift_down(u, halo, j):
    if j == 0:
        return u
    ru = pltpu.roll(u, j, 0)
    row8 = lax.broadcasted_iota(jnp.int32, halo.shape, 0)
    top = jnp.where(row8 < j, pltpu.roll(halo, j, 0), ru[:SUBLANES])
    return jnp.concatenate([top, ru[SUBLANES:]], axis=0)


def _shift_up(d, halo, j):
    if j == 0:
        return d
    tr = d.shape[0]
    rd = pltpu.roll(d, tr - j, 0)
    row8 = lax.broadcasted_iota(jnp.int32, halo.shape, 0)
    bot = jnp.where(row8 >= SUBLANES - j, pltpu.roll(halo, SUBLANES - j, 0), rd[tr - SUBLANES:])
    return jnp.concatenate([rd[:tr - SUBLANES], bot], axis=0)


def _conv_col(cb):
    return jnp.where(cb < 2, AL_XS // CONV_CB + cb, AL_B // CONV_CB)


def _conv_specs(s, tr):
    per8 = tr // SUBLANES
    blk = pl.BlockSpec((tr, CONV_CB), lambda cb, i: (i, _conv_col(cb)))
    prev = pl.BlockSpec((SUBLANES, CONV_CB), lambda cb, i: (jnp.maximum(i * per8 - 1, 0), _conv_col(cb)))
    return blk, prev


def _conv_pre(u, halo, w_ref, b_ref, first):
    halo = jnp.where(first, 0.0, halo)
    acc = b_ref[...] + w_ref[CONV_W - 1:CONV_W, :] * u
    shifted = [u]
    for j in range(1, CONV_W):
        sh = _shift_down(u, halo, j)
        shifted.append(sh)
        acc = acc + w_ref[CONV_W - 1 - j:CONV_W - j, :] * sh
    return acc, shifted


def _conv_fwd(proj, conv_w, conv_b):
    s = proj.shape[0]
    tr = min(CONV_TR, s)
    blk, prev = _conv_specs(s, tr)

    def body(u_ref, h_ref, w_ref, b_ref, o_ref):
        pre, _ = _conv_pre(u_ref[...], h_ref[...], w_ref, b_ref, pl.program_id(1) == 0)
        o_ref[...] = _silu(pre)

    return pl.pallas_call(
        body, name="conv_fwd", grid=(CONV_DIM // CONV_CB, s // tr),
        in_specs=[blk, prev, pl.BlockSpec((CONV_W, CONV_CB), lambda cb, i: (0, cb)), pl.BlockSpec((1, CONV_CB), lambda cb, i: (0, cb))],
        out_specs=pl.BlockSpec((tr, CONV_CB), lambda cb, i: (i, cb)),
        out_shape=jax.ShapeDtypeStruct((s, CONV_DIM), F32),
        compiler_params=_params(("parallel", "parallel")),
    )(proj, proj, conv_w, conv_b)


def _conv_bwd_pre(proj, conv_w, conv_b, dxc):
    s = proj.shape[0]
    tr = min(CONV_TR, s)
    blk, prev = _conv_specs(s, tr)

    def body(u_ref, h_ref, w_ref, b_ref, d_ref, dpre_ref, dw_ref, db_ref):
        i = pl.program_id(1)
        pre, shifted = _conv_pre(u_ref[...], h_ref[...], w_ref, b_ref, i == 0)
        sg = _sigmoid(pre)
        dpre = d_ref[...] * (sg * (1.0 + pre * (1.0 - sg)))
        dpre_ref[...] = dpre

        @pl.when(i == 0)
        def _():
            dw_ref[...] = jnp.zeros_like(dw_ref)
            db_ref[...] = jnp.zeros_like(db_ref)

        db_ref[...] += _colsum(dpre)
        for j in range(CONV_W):
            dw_ref[CONV_W - 1 - j:CONV_W - j, :] += _colsum(dpre * shifted[j])

    own = pl.BlockSpec((tr, CONV_CB), lambda cb, i: (i, cb))
    wspec = pl.BlockSpec((CONV_W, CONV_CB), lambda cb, i: (0, cb))
    bspec = pl.BlockSpec((1, CONV_CB), lambda cb, i: (0, cb))
    return pl.pallas_call(
        body, name="conv_bwd_pre", grid=(CONV_DIM // CONV_CB, s // tr),
        in_specs=[blk, prev, wspec, bspec, own], out_specs=[own, wspec, bspec],
        out_shape=[jax.ShapeDtypeStruct((s, CONV_DIM), F32), jax.ShapeDtypeStruct((CONV_W, CONV_DIM), F32),
                   jax.ShapeDtypeStruct((1, CONV_DIM), F32)],
        compiler_params=_params(("parallel", "arbitrary")),
    )(proj, proj, conv_w, conv_b, dxc)


def _conv_bwd_in(dpre, conv_w, dproj):
    s = dpre.shape[0]
    tr = min(CONV_TR, s)
    per8 = tr // SUBLANES
    last8 = s // SUBLANES - 1
    nb = s // tr

    def body(d_ref, n_ref, w_ref, _, o_ref):
        d = d_ref[...]
        halo = jnp.where(pl.program_id(1) == nb - 1, 0.0, n_ref[...])
        acc = w_ref[CONV_W - 1:CONV_W, :] * d
        for j in range(1, CONV_W):
            acc = acc + w_ref[CONV_W - 1 - j:CONV_W - j, :] * _shift_up(d, halo, j)
        o_ref[...] = acc.astype(o_ref.dtype)

    own = pl.BlockSpec((tr, CONV_CB), lambda cb, i: (i, cb))
    nxt = pl.BlockSpec((SUBLANES, CONV_CB), lambda cb, i: (jnp.minimum((i + 1) * per8, last8), cb))
    return pl.pallas_call(
        body, name="conv_bwd_in", grid=(CONV_DIM // CONV_CB, nb),
        in_specs=[own, nxt, pl.BlockSpec((CONV_W, CONV_CB), lambda cb, i: (0, cb)), ANY],
        out_specs=pl.BlockSpec((tr, CONV_CB), lambda cb, i: (i, _conv_col(cb))),
        out_shape=jax.ShapeDtypeStruct(dproj.shape, dproj.dtype), input_output_aliases={3: 0},
        compiler_params=_params(("parallel", "parallel")),
    )(dpre, dpre, conv_w, dproj)


XC_B, XC_C = 1024, 1280


def _tile_iotas():
    row = lax.broadcasted_iota(jnp.int32, (CHUNK, LANES), 0)
    lane = lax.broadcasted_iota(jnp.int32, (CHUNK, LANES), 1)
    return row, lane


def _ssd_scalars(dtf_ref, bias_ref, alog_ref, row, lane):
    head = lane[:1] < N_HEADS
    raw = dtf_ref[...] + bias_ref[...]
    dt = _softplus(raw)
    a_neg = jnp.where(head, -jnp.exp(alog_ref[...]), 0.0)
    a = dt * a_neg
    tril = (row >= lane).astype(BF16)
    s = _dot3(tril, a)
    return raw, dt, a_neg, s


def _pair(v, j, lo):
    return jnp.where(lo, v[:, 2 * j:2 * j + 1], v[:, 2 * j + 1:2 * j + 2])


def _head_sum(x, lo, hh):
    return jnp.sum(jnp.where(lo == (hh == 0), x, 0.0), axis=1, keepdims=True)


def _decay_masks(s, h, row, lane):
    s_col = jnp.broadcast_to(s[:, h:h + 1], (CHUNK, LANES))
    s_row = s_col.T
    lm = jnp.where(row >= lane, jnp.exp(s_col - s_row), 0.0)
    lmt = jnp.where(row <= lane, jnp.exp(s_row - s_col), 0.0)
    return lm, lmt


def _ssd_fwd(xc_all, proj, dt_bias_l, a_log_l, d_exp):
    s_len = xc_all.shape[0]
    nc = s_len // CHUNK

    def body(x_ref, dtf_ref, bias_ref, alog_ref, dexp_ref, y_ref, prevs_ref, state_ref):
        @pl.when(pl.program_id(0) == 0)
        def _():
            state_ref[...] = jnp.zeros_like(state_ref)

        row, lane = _tile_iotas()
        lo = lane < HEAD_DIM
        _, dt, _, s = _ssd_scalars(dtf_ref, bias_ref, alog_ref, row, lane)
        tot = s[CHUNK - 1:CHUNK, :]
        for g in range(SSM_GROUPS):
            bg = x_ref[:, XC_B + g * SSM_STATE:XC_B + (g + 1) * SSM_STATE].astype(BF16)
            cg = x_ref[:, XC_C + g * SSM_STATE:XC_C + (g + 1) * SSM_STATE].astype(BF16)
            cb = _dot(cg, bg, NT)
            for j in range(g * 4, g * 4 + 4):
                xs_p = x_ref[:, j * LANES:(j + 1) * LANES]
                dt_p, s_p, tot_p = _pair(dt, j, lo), _pair(s, j, lo), _pair(tot, j, lo[:1])
                xc_p = xs_p * dt_p
                xc_b = xc_p.astype(BF16)
                yd = []
                for hh in range(2):
                    lm, _ = _decay_masks(s, 2 * j + hh, row, lane)
                    yd.append(_dot((cb * lm).astype(BF16), xc_b))
                prev = state_ref[j]
                prevs_ref[0, j] = prev
                yo = _dot(cg, prev.astype(BF16)) * jnp.exp(s_p)
                y_ref[:, j * LANES:(j + 1) * LANES] = jnp.where(lo, yd[0], yd[1]) + yo + dexp_ref[:, j * LANES:(j + 1) * LANES] * xs_p
                to_end = jnp.exp(tot_p - s_p)
                state_ref[j] = jnp.exp(tot_p) * prev + _dot(bg, (xc_p * to_end).astype(BF16), TN)

    vec = lambda w: pl.BlockSpec((1, w), lambda c: (0, 0))
    return pl.pallas_call(
        body, name="ssd_fwd", grid=(nc,),
        in_specs=[pl.BlockSpec((CHUNK, CONV_DIM), lambda c: (c, 0)), pl.BlockSpec((CHUNK, LANES), lambda c: (c, AL_DTF // LANES)),
                  vec(LANES), vec(LANES), vec(D_MODEL)],
        out_specs=[pl.BlockSpec((CHUNK, D_MODEL), lambda c: (c, 0)), pl.BlockSpec((1, N_PAIRS, SSM_STATE, LANES), lambda c: (c, 0, 0, 0))],
        out_shape=[jax.ShapeDtypeStruct((s_len, D_MODEL), F32), jax.ShapeDtypeStruct((nc, N_PAIRS, SSM_STATE, LANES), F32)],
        scratch_shapes=[pltpu.VMEM((N_PAIRS, SSM_STATE, LANES), F32)],
        compiler_params=_params(("arbitrary",)),
    )(xc_all, proj, dt_bias_l, a_log_l, d_exp)


def _ssd_bwd(xc_all, proj, dt_bias_l, a_log_l, d_exp, prevs, dy):
    s_len = xc_all.shape[0]
    nc = s_len // CHUNK

    def body(x_ref, dtf_ref, bias_ref, alog_ref, dexp_ref, prevs_ref, dy_ref, dx_ref, ddt_ref, da_ref, dd_ref, dbias_ref, dstate_ref):
        @pl.when(pl.program_id(0) == 0)
        def _():
            dstate_ref[...] = jnp.zeros_like(dstate_ref)
            da_ref[...] = jnp.zeros_like(da_ref)
            dd_ref[...] = jnp.zeros_like(dd_ref)
            dbias_ref[...] = jnp.zeros_like(dbias_ref)

        row, lane = _tile_iotas()
        lo = lane < HEAD_DIM
        last = row == CHUNK - 1
        raw, dt, a_neg, s = _ssd_scalars(dtf_ref, bias_ref, alog_ref, row, lane)
        tot = s[CHUNK - 1:CHUNK, :]
        ds_acc = jnp.zeros((CHUNK, LANES), F32)
        ddt_acc = jnp.zeros((CHUNK, LANES), F32)
        for g in range(SSM_GROUPS):
            bcol = slice(XC_B + g * SSM_STATE, XC_B + (g + 1) * SSM_STATE)
            ccol = slice(XC_C + g * SSM_STATE, XC_C + (g + 1) * SSM_STATE)
            bg = x_ref[:, bcol].astype(BF16)
            cg = x_ref[:, ccol].astype(BF16)
            cb = _dot(cg, bg, NT)
            cbt = _dot(bg, cg, NT)
            dcb = jnp.zeros((CHUNK, LANES), F32)
            dcbt = jnp.zeros((CHUNK, LANES), F32)
            db_acc = jnp.zeros((CHUNK, LANES), F32)
            dc_acc = jnp.zeros((CHUNK, LANES), F32)
            for j in range(g * 4, g * 4 + 4):
                cols = slice(j * LANES, (j + 1) * LANES)
                xs_p, dy_p = x_ref[:, cols], dy_ref[:, cols]
                dt_p, s_p, tot_p = _pair(dt, j, lo), _pair(s, j, lo), _pair(tot, j, lo[:1])
                xc_p = xs_p * dt_p
                xc_b, dy_b = xc_p.astype(BF16), dy_p.astype(BF16)
                e_p, f_p, etot_p = jnp.exp(s_p), jnp.exp(tot_p - s_p), jnp.exp(tot_p)
                prev, dnext = prevs_ref[0, j], dstate_ref[j]
                prev_b, dnext_b = prev.astype(BF16), dnext.astype(BF16)
                dd_ref[:, cols] += _colsum(dy_p * xs_p)
                dxs_p = dexp_ref[:, cols] * dy_p
                cp = _dot(cg, prev_b)
                gy = (dy_p * e_p).astype(BF16)
                dc_acc += _dot(gy, prev_b, NT)
                dstate_ref[j] = etot_p * dnext + _dot(cg, gy, TN)
                de = dy_p * cp * e_p
                bds = _dot(bg, dnext_b)
                db_acc += _dot((xc_p * f_p).astype(BF16), dnext_b, NT)
                dxc_p = bds * f_p
                df = bds * xc_p * f_p
                dtot_p = _colsum(dnext * prev) * etot_p + _colsum(df)
                dsl = de - df + jnp.where(last, dtot_p, 0.0)
                for hh in range(2):
                    h = 2 * j + hh
                    mine = lo == (hh == 0)
                    lm, lmt = _decay_masks(s, h, row, lane)
                    dy_h = jnp.where(mine, dy_p, 0.0).astype(BF16)
                    xc_h = jnp.where(mine, xc_p, 0.0).astype(BF16)
                    dm = _dot(dy_h, xc_b, NT)
                    dmt = _dot(xc_h, dy_b, NT)
                    mt = cbt * lmt
                    dxc_p += _dot(mt.astype(BF16), dy_h)
                    ds_h = (jnp.sum(dm * cb * lm, axis=1, keepdims=True) - jnp.sum(dmt * mt, axis=1, keepdims=True)
                            + _head_sum(dsl, lo, hh))
                    ds_acc += jnp.where(lane == h, ds_h, 0.0)
                    dcb += dm * lm
                    dcbt += dmt * lmt
                    ddt_acc += jnp.where(lane == h, _head_sum(dxc_p * xs_p, lo, hh), 0.0)
                dx_ref[:, cols] = dxs_p + dxc_p * dt_p
            dx_ref[:, ccol] = dc_acc + _dot(dcb.astype(BF16), bg)
            dx_ref[:, bcol] = db_acc + _dot(dcbt.astype(BF16), cg)
        triu = (row <= lane).astype(BF16)
        da = _dot3(triu, ds_acc)
        ddt = ddt_acc + da * a_neg
        da_ref[...] += _colsum(da * dt) * a_neg[:1]
        ddt_raw = jnp.where(lane < N_HEADS, ddt * _sigmoid(raw), 0.0)
        dbias_ref[...] += _colsum(ddt_raw)
        ddt_ref[...] = ddt_raw

    rev = lambda c: nc - 1 - c
    vec = lambda w: pl.BlockSpec((1, w), lambda c: (0, 0))
    return pl.pallas_call(
        body, name="ssd_bwd", grid=(nc,),
        in_specs=[pl.BlockSpec((CHUNK, CONV_DIM), lambda c: (rev(c), 0)), pl.BlockSpec((CHUNK, LANES), lambda c: (rev(c), AL_DTF // LANES)),
                  vec(LANES), vec(LANES), vec(D_MODEL),
                  pl.BlockSpec((1, N_PAIRS, SSM_STATE, LANES), lambda c: (rev(c), 0, 0, 0)),
                  pl.BlockSpec((CHUNK, D_MODEL), lambda c: (rev(c), 0))],
        out_specs=[pl.BlockSpec((CHUNK, CONV_DIM), lambda c: (rev(c), 0)), pl.BlockSpec((CHUNK, LANES), lambda c: (rev(c), 0)),
                   vec(LANES), vec(D_MODEL), vec(LANES)],
        out_shape=[jax.ShapeDtypeStruct((s_len, CONV_DIM), F32), jax.ShapeDtypeStruct((s_len, LANES), F32),
                   jax.ShapeDtypeStruct((1, LANES), F32), jax.ShapeDtypeStruct((1, D_MODEL), F32), jax.ShapeDtypeStruct((1, LANES), F32)],
        scratch_shapes=[pltpu.VMEM((N_PAIRS, SSM_STATE, LANES), F32)],
        compiler_params=_params(("arbitrary",)),
    )(xc_all, proj, dt_bias_l, a_log_l, d_exp, prevs, dy)


AUG_C, AUG_ONE = 64, 67
NEG = -1e30
ATT_T = 512


def _fox_cum(proj, f_bias_l):
    s_len = proj.shape[0]
    nc = s_len // CHUNK

    def body(dtf_ref, fb_ref, cum_ref):
        row, lane = _tile_iotas()
        tril = (row >= lane).astype(BF16)

        def step(c, carry):
            rows = pl.ds(pl.multiple_of(c * CHUNK, CHUNK), CHUNK)
            lf = -_softplus(-(dtf_ref[rows, :] + fb_ref[...]))
            lf = jnp.where(lane < N_HEADS, pltpu.roll(lf, LANES - F_LANE, 1), 0.0)
            cs = _dot3(tril, lf) + carry
            cum_ref[rows, :] = cs
            return cs[CHUNK - 1:CHUNK, :]

        lax.fori_loop(0, nc, step, jnp.zeros((1, LANES), F32))

    return pl.pallas_call(
        body, name="fox_cum", grid=(1,),
        in_specs=[pl.BlockSpec((s_len, LANES), lambda i: (0, AL_DTF // LANES)), pl.BlockSpec((1, LANES), lambda i: (0, 0))],
        out_specs=pl.BlockSpec((s_len, LANES), lambda i: (0, 0)),
        out_shape=jax.ShapeDtypeStruct((s_len, LANES), F32),
        compiler_params=_params(("arbitrary",)),
    )(proj, f_bias_l)


def _fox_cum_bwd(dcum, proj, f_bias_l, ddt_tile, dproj):
    s_len = proj.shape[0]
    nc = s_len // CHUNK

    def body(dcum_ref, dtf_ref, fb_ref, ddt_ref, _, out_ref, dfb_ref):
        row, lane = _tile_iotas()
        triu = (row <= lane).astype(BF16)
        is_f = (lane >= F_LANE) & (lane < F_LANE + N_HEADS)

        def step(t, carry):
            run, dfb = carry
            rows = pl.ds(pl.multiple_of((nc - 1 - t) * CHUNK, CHUNK), CHUNK)
            rc = _dot3(triu, dcum_ref[rows, :]) + run
            sg = _sigmoid(-(dtf_ref[rows, :] + fb_ref[...]))
            df = jnp.where(is_f, pltpu.roll(rc, F_LANE, 1) * sg, 0.0)
            out_ref[rows, :] = (df + ddt_ref[rows, :]).astype(out_ref.dtype)
            return rc[0:1, :], dfb + _colsum(df)

        _, dfb = lax.fori_loop(0, nc, step, (jnp.zeros((1, LANES), F32), jnp.zeros((1, LANES), F32)))
        dfb_ref[...] = dfb

    whole = pl.BlockSpec((s_len, LANES), lambda i: (0, 0))
    dtf_cols = pl.BlockSpec((s_len, LANES), lambda i: (0, AL_DTF // LANES))
    vec = pl.BlockSpec((1, LANES), lambda i: (0, 0))
    return pl.pallas_call(
        body, name="fox_cum_bwd", grid=(1,),
        in_specs=[whole, dtf_cols, vec, whole, ANY], out_specs=[dtf_cols, vec],
        out_shape=[jax.ShapeDtypeStruct(dproj.shape, dproj.dtype), jax.ShapeDtypeStruct((1, LANES), F32)],
        input_output_aliases={4: 0}, compiler_params=_params(("arbitrary",)),
    )(dcum, proj, f_bias_l, ddt_tile, dproj)


def _attn_prep(proj, cum):
    s_len = proj.shape[0]
    tr = min(512, s_len)

    def body(q_ref, k_ref, v_ref, cum_ref, qa_ref, ka_ref, vb_ref):
        p = pl.program_id(0)
        lane = lax.broadcasted_iota(jnp.int32, (tr, LANES), 1)
        lo = lane < HEAD_DIM
        c = cum_ref[...]
        c1 = c.astype(BF16).astype(F32)
        r = c - c1
        c2 = r.astype(BF16).astype(F32)
        c3 = (r - c2).astype(BF16).astype(F32)
        q, k = q_ref[...] * (HEAD_DIM ** -0.5), k_ref[...]
        for hh in range(2):
            col = lambda x: jnp.sum(jnp.where(lane == 2 * p + hh, x, 0.0), axis=1, keepdims=True)
            a1, a2, a3 = col(c1), col(c2), col(c3)
            qh = q if hh == 0 else pltpu.roll(q, HEAD_DIM, 1)
            kh = k if hh == 0 else pltpu.roll(k, HEAD_DIM, 1)
            q_aug = jnp.where(lane == AUG_C, a1, jnp.where(lane == AUG_C + 1, a2, jnp.where(lane == AUG_C + 2, a3,
                              jnp.where(lane < AUG_ONE + 3, 1.0, 0.0))))
            k_aug = jnp.where(lane < AUG_ONE, 1.0, jnp.where(lane == AUG_ONE, -a1, jnp.where(lane == AUG_ONE + 1, -a2,
                              jnp.where(lane == AUG_ONE + 2, -a3, 0.0))))
            qa_ref[hh] = jnp.where(lo, qh, q_aug).astype(BF16)
            ka_ref[hh] = jnp.where(lo, kh, k_aug).astype(BF16)
        vb_ref[...] = v_ref[...].astype(BF16)

    slab = lambda col0: pl.BlockSpec((tr, LANES), lambda p, i: (i, col0 // LANES + p))
    heads = pl.BlockSpec((2, tr, LANES), lambda p, i: (p, i, 0))
    return pl.pallas_call(
        body, name="attn_prep", grid=(N_PAIRS, s_len // tr),
        in_specs=[slab(AL_Q), slab(AL_K), slab(AL_V), pl.BlockSpec((tr, LANES), lambda p, i: (i, 0))],
        out_specs=[heads, heads, pl.BlockSpec((tr, LANES), lambda p, i: (i, p))],
        out_shape=[jax.ShapeDtypeStruct((N_HEADS, s_len, LANES), BF16), jax.ShapeDtypeStruct((N_HEADS, s_len, LANES), BF16),
                   jax.ShapeDtypeStruct((s_len, D_MODEL), BF16)],
        compiler_params=_params(("parallel", "parallel")),
    )(proj, proj, proj, cum)


def _attn_fwd(qa, ka, vb, halves):
    s_len = vb.shape[0]
    t = min(ATT_T, s_len)
    nq = s_len // t
    n = len(halves)

    def body(qa_ref, ka_ref, vb_ref, *rest):
        o_ref, lse_ref = rest[n:n + 2]
        start, finish = _gather_plan(rest[:n], rest[n + 2:2 * n + 2], *rest[2 * n + 2:])
        i = pl.program_id(1)
        pl.when((pl.program_id(0) == 0) & (i == 0))(start)
        row = lax.broadcasted_iota(jnp.int32, (t, t), 0)
        col = lax.broadcasted_iota(jnp.int32, (t, t), 1)
        lo = lax.broadcasted_iota(jnp.int32, (t, LANES), 1) < HEAD_DIM
        qs = (qa_ref[0], qa_ref[1])

        def block(j, carry, masked):
            rows = pl.ds(pl.multiple_of(j * t, t), t)
            v = vb_ref[rows, :]
            new = []
            for hh in range(2):
                m, l, acc = carry[hh]
                s = _dot(qs[hh], ka_ref[hh, rows, :], NT)
                if masked:
                    s = jnp.where(row >= col, s, NEG)
                m_new = jnp.maximum(m, jnp.max(s, axis=1, keepdims=True))
                alpha = jnp.exp(m - m_new)
                p = jnp.exp(s - m_new)
                new.append((m_new, alpha * l + jnp.sum(p, axis=1, keepdims=True), alpha * acc + _dot(p.astype(BF16), v)))
            return tuple(new)

        init = (jnp.full((t, 1), NEG, F32), jnp.zeros((t, 1), F32), jnp.zeros((t, LANES), F32))
        carry = lax.fori_loop(0, i, functools.partial(block, masked=False), (init, init))
        (m0, l0, acc0), (m1, l1, acc1) = block(i, carry, True)
        o_ref[...] = jnp.where(lo, acc0 / l0, acc1 / l1)
        lse_ref[...] = jnp.where(lo, m0 + jnp.log(l0), m1 + jnp.log(l1))
        pl.when((pl.program_id(0) == N_PAIRS - 1) & (i == nq - 1))(finish)

    out = pl.BlockSpec((t, LANES), lambda p, i: (i, p))
    res = pl.pallas_call(
        body, name="attn_fwd", grid=(N_PAIRS, nq),
        in_specs=[pl.BlockSpec((2, t, LANES), lambda p, i: (p, i, 0)), pl.BlockSpec((2, s_len, LANES), lambda p, i: (p, 0, 0)),
                  pl.BlockSpec((s_len, LANES), lambda p, i: (0, p))] + [ANY] * n,
        out_specs=[out, out] + [ANY] * n,
        out_shape=[jax.ShapeDtypeStruct((s_len, D_MODEL), F32), jax.ShapeDtypeStruct((s_len, D_MODEL), F32)]
        + [jax.ShapeDtypeStruct((N_CHIPS, *h.shape), h.dtype) for h in halves],
        scratch_shapes=_exchange_sems(n),
        compiler_params=_params(("arbitrary", "arbitrary")),
    )(qa, ka, vb, *halves)
    return res[0], res[1], res[2:]


def _attn_bwd(qa, ka, vb, o, lse, do, parts, dproj):
    s_len = vb.shape[0]
    t = min(ATT_T, s_len)
    nq = s_len // t
    n = len(parts)

    def body(qa_ref, ka_ref, vb_ref, o_ref, lse_ref, do_ref, *rest):
        dqa_ref, dka_ref, dv_ref = rest[n + 1:n + 4]
        start, finish = _reduce_plan(rest[:n], rest[n + 4:2 * n + 4], *rest[2 * n + 4:])
        j = pl.program_id(1)
        pl.when((pl.program_id(0) == 0) & (j == 0))(start)

        @pl.when(j == 0)
        def _():
            dqa_ref[...] = jnp.zeros_like(dqa_ref)

        row = lax.broadcasted_iota(jnp.int32, (t, t), 0)
        col = lax.broadcasted_iota(jnp.int32, (t, t), 1)
        lo = lax.broadcasted_iota(jnp.int32, (t, LANES), 1) < HEAD_DIM
        v = vb_ref[...]
        ks = (ka_ref[0], ka_ref[1])

        def block(i, carry, masked):
            dk, dv = list(carry[:2]), carry[2]
            rows = pl.ds(pl.multiple_of(i * t, t), t)
            do_p, o_p, lse_p = do_ref[rows, :], o_ref[rows, :], lse_ref[rows, :]
            for hh in range(2):
                q = qa_ref[hh, rows, :]
                do_h = jnp.where(lo == (hh == 0), do_p, 0.0)
                delta = jnp.sum(do_h * o_p, axis=1, keepdims=True)
                s = _dot(q, ks[hh], NT)
                if masked:
                    s = jnp.where(row >= col, s, NEG)
                p = jnp.exp(s - lse_p[:, hh * HEAD_DIM:hh * HEAD_DIM + 1])
                do_b = do_h.astype(BF16)
                ds = (p * (_dot(do_b, v, NT) - delta)).astype(BF16)
                dv = dv + _dot(p.astype(BF16), do_b, TN)
                dk[hh] = dk[hh] + _dot(ds, q, TN)
                dqa_ref[hh, rows, :] += _dot(ds, ks[hh])
            return dk[0], dk[1], dv

        zero = jnp.zeros((t, LANES), F32)
        carry = block(j, (zero, zero, zero), True)
        dk0, dk1, dv = lax.fori_loop(j + 1, nq, functools.partial(block, masked=False), carry)
        dka_ref[0] = dk0
        dka_ref[1] = dk1
        dv_ref[...] = dv.astype(dv_ref.dtype)
        pl.when((pl.program_id(0) == N_PAIRS - 1) & (j == nq - 1))(finish)

    whole_pair = pl.BlockSpec((2, s_len, LANES), lambda p, j: (p, 0, 0))
    blk_pair = pl.BlockSpec((2, t, LANES), lambda p, j: (p, j, 0))
    whole_cols = pl.BlockSpec((s_len, LANES), lambda p, j: (0, p))
    blk_cols = pl.BlockSpec((t, LANES), lambda p, j: (j, p))
    res = pl.pallas_call(
        body, name="attn_bwd", grid=(N_PAIRS, nq),
        in_specs=[whole_pair, blk_pair, blk_cols, whole_cols, whole_cols, whole_cols] + [ANY] * (n + 1),
        out_specs=[whole_pair, blk_pair, pl.BlockSpec((t, LANES), lambda p, j: (j, AL_V // LANES + p))] + [ANY] * n,
        out_shape=[jax.ShapeDtypeStruct((N_HEADS, s_len, LANES), F32), jax.ShapeDtypeStruct((N_HEADS, s_len, LANES), F32),
                   jax.ShapeDtypeStruct(dproj.shape, dproj.dtype)]
        + [jax.ShapeDtypeStruct((N_DEV, g.shape[1] // 2, g.shape[2]), g.dtype) for g in parts],
        scratch_shapes=_exchange_sems(n), input_output_aliases={6 + n: 2},
        compiler_params=_params(("arbitrary", "arbitrary")),
    )(qa, ka, vb, o, lse, do, *parts, dproj)
    return res[0], res[1], res[2], res[3:]


def _attn_post(dqa, dka, dproj):
    s_len = dqa.shape[1]
    tr = min(256, s_len)
    assert AL_K == AL_Q + D_MODEL and AL_Q % (2 * D_MODEL) == 0

    def body(dqa_ref, dka_ref, _, dqk_ref, dcum_ref):
        lane = lax.broadcasted_iota(jnp.int32, (tr, LANES), 1)
        lo = lane < HEAD_DIM
        dcum = jnp.zeros((tr, LANES), F32)
        for p in range(N_PAIRS):
            a0, a1, b0, b1 = dqa_ref[2 * p], dqa_ref[2 * p + 1], dka_ref[2 * p], dka_ref[2 * p + 1]
            dq = jnp.where(lo, a0, pltpu.roll(a1, HEAD_DIM, 1)) * (HEAD_DIM ** -0.5)
            dqk_ref[:, p * LANES:(p + 1) * LANES] = dq.astype(dqk_ref.dtype)
            dqk_ref[:, D_MODEL + p * LANES:D_MODEL + (p + 1) * LANES] = jnp.where(lo, b0, pltpu.roll(b1, HEAD_DIM, 1)).astype(dqk_ref.dtype)
            for hh, (a, b) in enumerate(((a0, b0), (a1, b1))):
                dcum = dcum + jnp.where(lane == 2 * p + hh, a[:, AUG_C:AUG_C + 1] - b[:, AUG_ONE:AUG_ONE + 1], 0.0)
        dcum_ref[...] = dcum

    heads = pl.BlockSpec((N_HEADS, tr, LANES), lambda i: (0, i, 0))
    return pl.pallas_call(
        body, name="attn_post", grid=(s_len // tr,),
        in_specs=[heads, heads, ANY],
        out_specs=[pl.BlockSpec((tr, 2 * D_MODEL), lambda i: (i, AL_Q // (2 * D_MODEL))), pl.BlockSpec((tr, LANES), lambda i: (i, 0))],
        out_shape=[jax.ShapeDtypeStruct(dproj.shape, dproj.dtype), jax.ShapeDtypeStruct((s_len, LANES), F32)],
        input_output_aliases={2: 0}, compiler_params=_params(("parallel",)),
    )(dqa, dka, dproj)


def _ln_stats(r):
    mu = _rowmean(r)
    xc = r - mu
    rstd = lax.rsqrt(_rowmean(xc * xc) + LN_EPS)
    return xc * rstd, rstd


def _ln_bwd(dxh, xh, rstd):
    return rstd * (dxh - _rowmean(dxh) - xh * _rowmean(dxh * xh))


def _rms_bwd(dgn, g, r):
    return r * dgn - (r * r * r) * g * _rowmean(dgn * g)


def _to_aligned(wt):
    out = jnp.zeros((AL_COLS, wt.shape[1]), wt.dtype)
    for dst, (lo, hi) in ((0, (0, 2048)), (AL_Q, (2576, 5648)), (AL_B, (2048, 2560)), (AL_DTF, (2560, 2576)), (AL_DTF + 16, (5648, 5664))):
        out = lax.dynamic_update_slice_in_dim(out, wt[lo:hi], dst, axis=0)
    return out


def _from_aligned(gt):
    return jnp.concatenate([gt[:AL_Q], gt[AL_B:AL_DTF], gt[AL_DTF:AL_DTF + 16], gt[AL_Q:AL_B], gt[AL_DTF + 16:AL_DTF + 32]], axis=0)


def _lanes(v, at=0):
    return jnp.pad(v, ((0, 0), (at, LANES - at - v.shape[1])))


def _local_step(x, tgt, mod, w_alt, halves, sp):
    d = D_MODEL
    sh1, sc1, g1, sh2, sc2, g2 = [mod[:, i * d:(i + 1) * d] for i in range(6)]
    dt_bias_l, a_log_l, f_bias_l = _lanes(sp["dt_bias"]), _lanes(sp["a_log"]), _lanes(sp["f_bias"], F_LANE)
    d_exp = jnp.repeat(sp["d_skip"], HEAD_DIM, axis=1)
    z_slab = lambda a: (a, d, AL_Z // d)

    (h1,), _ = _rowwise("mod1", lambda x, sc, sh: ([x * (1.0 + sc) + sh], []), [x], [sc1, sh1], [(d, BF16)], [])
    proj = _matmul("proj", h1, w_alt, dims=NT, tn=1152)
    xc_all = _conv_fwd(proj, sp["conv_w"], sp["conv_b"])
    y_ssd, prevs = _ssd_fwd(xc_all, proj, dt_bias_l, a_log_l, d_exp)

    def gated_norm(y, z, w):
        g = y * _silu(z)
        return [g * lax.rsqrt(_rowmean(g * g) + RMS_EPS) * w], []

    (y_mix,), _ = _rowwise("ssm_norm", gated_norm, [y_ssd, z_slab(proj)], [sp["ssm_norm_w"]], [(d, BF16, (2 * d, 0, None))], [])
    cum = _fox_cum(proj, f_bias_l)
    qa, ka, vb = _attn_prep(proj, cum)
    o, lse, (g_out, g_fi, g_fo) = _attn_fwd(qa, ka, vb, halves)
    w_out = g_out.reshape(2 * d, d)
    w_fi = g_fi.transpose(1, 0, 2).reshape(d, D_FF)
    w_fo = g_fo.reshape(D_FF, d)
    (y_mix,), _ = _rowwise("attn_norm", lambda o, w: ([o * lax.rsqrt(_rowmean(o * o) + RMS_EPS) * w], []),
                           [o], [sp["attn_norm_w"]], [(d, BF16, (2 * d, 1, y_mix))], [])
    def ln1_fwd(y, x, g1, sc2, sh2, lg, lb):
        r1 = ALPHA * x + (1.0 + g1) * y
        xh, _ = _ln_stats(r1)
        x1 = xh * lg + lb
        return [y, r1, x1 * (1.0 + sc2) + sh2], []

    y, r1, h2 = _matmul("out_proj", y_mix, w_out, tm=512, tk=2048,
                        epi=(ln1_fwd, [x], [g1, sc2, sh2, sp["ln1_g"], sp["ln1_b"]], [F32, F32, BF16], []))
    act = _matmul("ff_in", h2, w_fi, epi=(lambda u: ([jnp.square(jnp.maximum(u, 0.0))], []), [], [], [BF16], []))

    def head(ff, r1, tgt, g2, l1g, l1b, l2g, l2b):
        xh1, _ = _ln_stats(r1)
        x1 = xh1 * l1g + l1b
        xh2, rstd2 = _ln_stats(ALPHA * x1 + (1.0 + g2) * ff)
        err = xh2 * l2g + l2b - tgt
        loss = 0.5 * jnp.sum(_rowmean(err * err))
        dx2 = err * (1.0 / d)
        dr2 = _ln_bwd(dx2 * l2g, xh2, rstd2)
        return ([dr2, (1.0 + g2) * dr2],
                [_colsum(dx2 * xh2), _colsum(dx2), _colsum(dr2 * ff), jnp.full((1, LANES), loss, F32)])

    dr2, dff, d_ln2_g, d_ln2_b, d_g2, loss = _matmul(
        "ff_out", act, w_fo, tm=512, tk=2048,
        epi=(head, [r1, tgt], [g2, sp["ln1_g"], sp["ln1_b"], sp["ln2_g"], sp["ln2_b"]], [F32, BF16], [d, d, d, LANES]))
    du = _matmul("d_act", dff, w_fo, dims=NT, epi=(lambda da, act: ([da * (2.0 * jnp.sqrt(act.astype(F32)))], []), [act], [], [BF16], []))
    dw_fo = _matmul("dw_ff_out", act, dff, dims=TN, out_dtype=BF16, by_chip="rows")
    dw_fi = _matmul("dw_ff_in", h2, du, dims=TN, out_dtype=BF16, by_chip="cols")

    def ln1_bwd(dh2, r1, dr2, y, sc2, g1, lg, lb):
        xh, rstd = _ln_stats(r1)
        x1 = xh * lg + lb
        dx1 = ALPHA * dr2 + dh2 * (1.0 + sc2)
        dr1 = _ln_bwd(dx1 * lg, xh, rstd)
        return ([dr1, (1.0 + g1) * dr1],
                [_colsum(dh2 * x1), _colsum(dh2), _colsum(dx1 * xh), _colsum(dx1), _colsum(dr1 * y)])

    dr1, dy, d_sc2, d_sh2, d_ln1_g, d_ln1_b, d_g1 = _matmul(
        "dh2", du, w_fi, dims=NT, tm=512, tk=2048,
        epi=(ln1_bwd, [r1, dr2, y], [sc2, g1, sp["ln1_g"], sp["ln1_b"]], [F32, BF16], [d] * 5))
    dymix = _matmul("dy_mix", dy, w_out, dims=NT)
    dw_out = _matmul("dw_out", y_mix, dy, dims=TN, out_dtype=BF16, by_chip="rows")

    def attn_norm_bwd(o, dyo, w):
        r = lax.rsqrt(_rowmean(o * o) + RMS_EPS)
        return [_rms_bwd(dyo * w, o, r)], [_colsum(dyo * o * r)]

    (do,), (d_attn_w,) = _rowwise("attn_norm_bwd", attn_norm_bwd, [o, (dymix, d, 1)], [sp["attn_norm_w"]], [(d, F32)], [d])

    def gated_norm_bwd(y, z, dyo, w):
        sg = _sigmoid(z)
        sz = z * sg
        g = y * sz
        r = lax.rsqrt(_rowmean(g * g) + RMS_EPS)
        dg = _rms_bwd(dyo * w, g, r)
        return [dg * sz, dg * y * (sg * (1.0 + z * (1.0 - sg)))], [_colsum(dyo * g * r)]

    (dy_ssd, dproj), (d_ssm_w,) = _rowwise("ssm_norm_bwd", gated_norm_bwd, [y_ssd, z_slab(proj), (dymix, d, 0)],
                                           [sp["ssm_norm_w"]], [(d, F32), (d, BF16, (AL_COLS, AL_Z // d, None))], [d])
    dqa, dka, dproj, landed = _attn_bwd(qa, ka, vb, o, lse, do, [dw_out, dw_fi, dw_fo], dproj)
    dproj, dcum = _attn_post(dqa, dka, dproj)
    dxc, ddt_tile, d_alog_l, d_dexp, d_dtb_l = _ssd_bwd(xc_all, proj, dt_bias_l, a_log_l, d_exp, prevs, dy_ssd)
    dproj, d_fb_l = _fox_cum_bwd(dcum, proj, f_bias_l, ddt_tile, dproj)
    dpre, d_conv_w, d_conv_b = _conv_bwd_pre(proj, sp["conv_w"], sp["conv_b"], dxc)
    dproj = _conv_bwd_in(dpre, sp["conv_w"], dproj)
    dw_alt = _matmul("dw_in", dproj, h1, dims=TN, tm=1152, out_dtype=BF16)
    part_in = _from_aligned(dw_alt).reshape(N_CHIPS, IN_COLS // N_CHIPS, d)

    def last(dh1, x, dr1, sc1):
        return [ALPHA * dr1 + dh1 * (1.0 + sc1)], [_colsum(dh1 * x), _colsum(dh1)]

    chip_in = _pair_sum(part_in, _pair_exchange(part_in), lax.axis_index("c"))
    dx, d_sc1, d_sh1, landed_in = _matmul("dh1", dproj, w_alt, tm=512, tk=1152, carry=[chip_in],
                                          epi=(last, [x, dr1], [sc1], [F32], [d, d]))

    small = {
        "mod": jnp.concatenate([d_sh1, d_sc1, d_g1, d_sh2, d_sc2, d_g2], axis=1),
        "conv_w": d_conv_w, "conv_b": d_conv_b,
        "dt_bias": d_dtb_l[:, :N_HEADS], "a_log": d_alog_l[:, :N_HEADS],
        "d_skip": jnp.sum(d_dexp.reshape(N_HEADS, HEAD_DIM), axis=1)[None, :],
        "ssm_norm_w": d_ssm_w, "f_bias": d_fb_l[:, F_LANE:F_LANE + N_HEADS], "attn_norm_w": d_attn_w,
        "ln1_g": d_ln1_g, "ln1_b": d_ln1_b, "ln2_g": d_ln2_g, "ln2_b": d_ln2_b, "loss": loss,
    }
    return dx, [landed_in, *landed], small


N_DEV = 8
N_CHIPS = 4
ANY = pl.BlockSpec(memory_space=pl.ANY)
VMEM_SPEC = pl.BlockSpec(memory_space=pltpu.VMEM)


def _place():
    x, y, c = lax.axis_index("x"), lax.axis_index("y"), lax.axis_index("c")
    return x, y, c


def _other_chips(x, y):
    return [(1 - x, y, 2 * (1 - x) + y), (x, 1 - y, 2 * x + 1 - y), (1 - x, 1 - y, 2 * (1 - x) + 1 - y)]


def _small_gather(v_ref, out_ref, send_sems, recv_sems, local_sem):
    x, y, c = _place()
    me = 4 * x + 2 * y + c
    mine = pltpu.make_async_copy(v_ref, out_ref.at[me], local_sem)
    mine.start()
    peers = _peers(x, y, c)

    def copy(rel, slot, to):
        return pltpu.make_async_remote_copy(src_ref=v_ref, dst_ref=out_ref.at[slot], send_sem=send_sems.at[rel],
                                            recv_sem=recv_sems.at[rel], device_id=to, device_id_type=MESH)

    sends = [copy(rel, me, peer) for rel, peer in enumerate(peers)]
    for cp in sends:
        cp.start()
    for rel, (px, py, pc) in enumerate(peers):
        copy(rel, 4 * px + 2 * py + pc, (x, y, c)).wait_recv()
    for cp in sends:
        cp.wait_send()
    mine.wait()


SMALL_GATHER_SEMS = [pltpu.SemaphoreType.DMA((N_DEV - 1,)), pltpu.SemaphoreType.DMA((N_DEV - 1,)), pltpu.SemaphoreType.DMA]


def _allgather_small(name, v):
    return pl.pallas_call(
        _small_gather_body(), name=name, out_shape=jax.ShapeDtypeStruct((N_DEV, *v.shape), v.dtype),
        in_specs=[VMEM_SPEC], out_specs=VMEM_SPEC, scratch_shapes=SMALL_GATHER_SEMS,
    )(v)


def _small_gather_body():
    def body(v_ref, out_ref, send_sems, recv_sems, local_sem):
        _small_gather(v_ref, out_ref, send_sems, recv_sems, local_sem)
    return body


def _prologue(v8, w_ada_shard, b_shard, w_in_shard):
    r, cdim = w_in_shard.shape

    def body(v_ref, wada_ref, b_ref, win_ref, first_ref, parts_ref, gin_ref, stage, part_ref, *sems):
        w_start, w_finish = _shard_gather_plan(win_ref, gin_ref, stage, *sems[6:])
        w_start()
        _small_gather(v_ref, first_ref, *sems[:3])
        row = lax.broadcasted_iota(jnp.int32, (N_DEV, D_MODEL), 0)
        c_all = jnp.zeros((N_DEV, D_MODEL), F32)
        for dev in range(N_DEV):
            c_all = c_all + jnp.where(row == dev, first_ref[dev][:, :D_MODEL], 0.0)
        part_ref[...] = _dot(_silu(c_all).astype(BF16), wada_ref[...].astype(BF16)) + b_ref[...]
        _small_gather(part_ref, parts_ref, *sems[3:6])
        w_finish()

    return pl.pallas_call(
        body, name="prologue",
        out_shape=[jax.ShapeDtypeStruct((N_DEV, *v8.shape), F32), jax.ShapeDtypeStruct((N_DEV, N_DEV, ADA_SHARD), F32),
                   jax.ShapeDtypeStruct((N_CHIPS, r, cdim), w_in_shard.dtype)],
        in_specs=[VMEM_SPEC, VMEM_SPEC, VMEM_SPEC, ANY], out_specs=[VMEM_SPEC, VMEM_SPEC, ANY],
        scratch_shapes=[pltpu.VMEM((r, cdim), w_in_shard.dtype), pltpu.VMEM((N_DEV, ADA_SHARD), F32)] + SMALL_GATHER_SEMS * 2
        + [pltpu.SemaphoreType.DMA((6,)), pltpu.SemaphoreType.DMA((6,)), pltpu.SemaphoreType.DMA((2,))],
        compiler_params=_params(),
    )(v8, w_ada_shard, b_shard, w_in_shard)


def _shard_gather_plan(in_ref, out_ref, stage, send_sems, recv_sems, local_sems):
    ch = in_ref.shape[1] // 2
    x, y, c = _place()
    k_me = 2 * x + y
    me, sibling = (x, y, c), (x, y, 1 - c)
    chips = _other_chips(x, y)

    def copy(idx, k, half, to, src=None):
        cols = out_ref.at[k, :, pl.ds(pl.multiple_of(half * ch, ch), ch)]
        return pltpu.make_async_remote_copy(src_ref=cols if src is None else src, dst_ref=cols, send_sem=send_sems.at[idx],
                                            recv_sem=recv_sems.at[idx], device_id=to, device_id_type=MESH)

    mine = in_ref.at[:, pl.ds(pl.multiple_of(c * ch, ch), ch)]
    sends = [copy(j, k_me, c, (cx, cy, c), src=mine) for j, (cx, cy, _) in enumerate(chips)]
    load = pltpu.make_async_copy(in_ref, stage, local_sems.at[0])
    store = pltpu.make_async_copy(stage, out_ref.at[k_me], local_sems.at[1])

    def start():
        for cp in sends:
            cp.start()
        load.start()

    def finish():
        load.wait()
        store.start()
        forwards = []
        for j, (_, _, kj) in enumerate(chips):
            copy(j, kj, c, me).wait_recv()
            forwards.append(copy(3 + j, kj, c, sibling))
            forwards[-1].start()
        for j, (_, _, kj) in enumerate(chips):
            copy(3 + j, kj, 1 - c, me).wait_recv()
        for cp in sends + forwards:
            cp.wait_send()
        store.wait()

    return start, finish


def _peers(x, y, c):
    return [((1 - x) if rel & 4 else x, (1 - y) if rel & 2 else y, (1 - c) if rel & 1 else c) for rel in range(1, N_DEV)]


def _exchange_sems(n):
    return [pltpu.SemaphoreType.DMA((n, N_DEV - 1)), pltpu.SemaphoreType.DMA((n, N_DEV - 1)), pltpu.SemaphoreType.DMA((n,))]


def _gather_plan(ins, outs, send_sems, recv_sems, local_sems):
    x, y, c = _place()
    k_me = 2 * x + y
    peers = [(rel, p) for rel, p in enumerate(_peers(x, y, c)) if (rel + 1) & 6]

    def copy(w, rel, k, half, to, src=None):
        rh = ins[w].shape[0] // 2
        rows = outs[w].at[k, pl.ds(pl.multiple_of(half * rh, rh), rh), :]
        return pltpu.make_async_remote_copy(src_ref=rows if src is None else src, dst_ref=rows, send_sem=send_sems.at[w, rel],
                                            recv_sem=recv_sems.at[w, rel], device_id=to, device_id_type=MESH)

    def mine(w):
        rh = ins[w].shape[0] // 2
        return ins[w].at[pl.ds(pl.multiple_of(c * rh, rh), rh), :]

    n = len(ins)
    local = [pltpu.make_async_copy(ins[w], outs[w].at[k_me], local_sems.at[w]) for w in range(n)]
    sends = [copy(w, rel, k_me, c, peer, src=mine(w)) for w in range(n) for rel, peer in peers]

    def start():
        for cp in local + sends:
            cp.start()

    def finish():
        for w in range(n):
            for rel, (px, py, pc) in peers:
                copy(w, rel, 2 * px + py, pc, (x, y, c)).wait_recv()
        for cp in sends:
            cp.wait_send()
        for cp in local:
            cp.wait()

    return start, finish


def _reduce_plan(ins, outs, send_sems, recv_sems, local_sems):
    x, y, c = _place()
    me = 4 * x + 2 * y + c
    peers = _peers(x, y, c)

    def block(w, k, half):
        rh = ins[w].shape[1] // 2
        return ins[w].at[k, pl.ds(pl.multiple_of(half * rh, rh), rh), :]

    def copy(w, rel, src, slot, to):
        return pltpu.make_async_remote_copy(src_ref=src, dst_ref=outs[w].at[slot], send_sem=send_sems.at[w, rel],
                                            recv_sem=recv_sems.at[w, rel], device_id=to, device_id_type=MESH)

    n = len(ins)
    local = [pltpu.make_async_copy(block(w, 2 * x + y, c), outs[w].at[me], local_sems.at[w]) for w in range(n)]
    sends = [copy(w, rel, block(w, 2 * px + py, pc), me, (px, py, pc)) for w in range(n) for rel, (px, py, pc) in enumerate(peers)]

    def start():
        for cp in local + sends:
            cp.start()

    def finish():
        for w in range(n):
            for rel, (px, py, pc) in enumerate(peers):
                copy(w, rel, block(w, 2 * x + y, c), 4 * px + 2 * py + pc, (x, y, c)).wait_recv()
        for cp in sends:
            cp.wait_send()
        for cp in local:
            cp.wait()

    return start, finish


def _scatter_plan(ins, outs, send_sems, recv_sems, local_sems):
    x, y, c = _place()
    k_me = 2 * x + y
    chips = _other_chips(x, y)

    def copy(w, j, src_k, dst_k, to):
        return pltpu.make_async_remote_copy(src_ref=ins[w].at[src_k], dst_ref=outs[w].at[dst_k], send_sem=send_sems.at[w, j],
                                            recv_sem=recv_sems.at[w, j], device_id=to, device_id_type=MESH)

    n = len(ins)
    local = [pltpu.make_async_copy(ins[w].at[k_me], outs[w].at[k_me], local_sems.at[w]) for w in range(n)]
    sends = [copy(w, j, kj, k_me, (cx, cy, c)) for w in range(n) for j, (cx, cy, kj) in enumerate(chips)]

    def start():
        for cp in local + sends:
            cp.start()

    def finish():
        for w in range(n):
            for j, (_, _, kj) in enumerate(chips):
                copy(w, j, k_me, kj, (x, y, c)).wait_recv()
        for cp in sends:
            cp.wait_send()
        for cp in local:
            cp.wait()

    return start, finish


def _row_tile(r, mult=2 * SUBLANES):
    if r % 256 == 0:
        return 256
    return max([t for t in range(mult, 513, mult) if r % t == 0], default=r)


def _pair_exchange(g):
    _, r, cdim = g.shape
    ch = cdim // 2

    def body(g_ref, got_ref, send_sem, recv_sem):
        x, y, c = _place()
        cp = pltpu.make_async_remote_copy(src_ref=g_ref.at[:, :, pl.ds(pl.multiple_of((1 - c) * ch, ch), ch)], dst_ref=got_ref,
                                          send_sem=send_sem, recv_sem=recv_sem, device_id=(x, y, 1 - c), device_id_type=MESH)
        cp.start()
        cp.wait_recv()
        cp.wait_send()

    return pl.pallas_call(
        body, name="pair_exchange", out_shape=jax.ShapeDtypeStruct((N_CHIPS, r, ch), g.dtype),
        in_specs=[ANY], out_specs=ANY, scratch_shapes=[pltpu.SemaphoreType.DMA, pltpu.SemaphoreType.DMA],
    )(g)


def _pair_sum(g, got, c):
    _, r, cdim = g.shape
    ch = cdim // 2
    tr = _row_tile(r)

    def body(c_ref, g_ref, got_ref, o_ref):
        o_ref[...] = (g_ref[...].astype(F32) + got_ref[...].astype(F32)).astype(o_ref.dtype)

    blk = pl.BlockSpec((1, tr, ch), lambda k, i, c_ref: (k, i, 0))
    return pl.pallas_call(
        body, name="pair_sum",
        grid_spec=pltpu.PrefetchScalarGridSpec(
            num_scalar_prefetch=1, grid=(N_CHIPS, r // tr),
            in_specs=[pl.BlockSpec((1, tr, ch), lambda k, i, c_ref: (k, i, c_ref[0])), blk], out_specs=blk),
        out_shape=jax.ShapeDtypeStruct((N_CHIPS, r, ch), BF16),
        compiler_params=_params(("parallel", "parallel")),
    )(jnp.reshape(c, (1,)).astype(jnp.int32), g, got)


def _sum_blocks(name, parts):
    k, r, cdim = parts.shape
    tr = _row_tile(r)

    def body(p_ref, o_ref):
        acc = p_ref[0].astype(F32)
        for i in range(1, k):
            acc = acc + p_ref[i].astype(F32)
        o_ref[...] = acc

    return pl.pallas_call(
        body, name=name, grid=(r // tr,),
        in_specs=[pl.BlockSpec((k, tr, cdim), lambda i: (0, i, 0))], out_specs=pl.BlockSpec((tr, cdim), lambda i: (i, 0)),
        out_shape=jax.ShapeDtypeStruct((r, cdim), F32), compiler_params=_params(("parallel",)),
    )(parts)


def _pair_swap(halves):
    n = len(halves)

    def body(*refs):
        ins, outs = refs[:n], refs[n:2 * n]
        send_sems, recv_sems = refs[2 * n:]
        x, y, c = _place()
        cps = [pltpu.make_async_remote_copy(src_ref=ins[w], dst_ref=outs[w], send_sem=send_sems.at[w], recv_sem=recv_sems.at[w],
                                            device_id=(x, y, 1 - c), device_id_type=MESH) for w in range(n)]
        for cp in cps:
            cp.start()
        for cp in cps:
            cp.wait_recv()
        for cp in cps:
            cp.wait_send()

    return pl.pallas_call(
        body, name="pair_swap", out_shape=[jax.ShapeDtypeStruct(h.shape, h.dtype) for h in halves],
        in_specs=[ANY] * n, out_specs=[ANY] * n,
        scratch_shapes=[pltpu.SemaphoreType.DMA((n,)), pltpu.SemaphoreType.DMA((n,))],
    )(*halves)


ADA_SHARD = 6 * D_MODEL // N_CHIPS


def _w_ada_grad(c_all_t, dmod_shard):
    tm = 256

    def body(ct_ref, dm_ref, o_ref):
        act = _silu(ct_ref[...])
        acc = act[:, 0:1] * dm_ref[0:1, :]
        for dev in range(1, N_DEV):
            acc = acc + act[:, dev:dev + 1] * dm_ref[dev:dev + 1, :]
        o_ref[...] = acc

    return pl.pallas_call(
        body, name="w_ada_grad", grid=(D_MODEL // tm,),
        in_specs=[pl.BlockSpec((tm, N_DEV), lambda i: (i, 0)), pl.BlockSpec((N_DEV, ADA_SHARD), lambda i: (0, 0))],
        out_specs=pl.BlockSpec((tm, ADA_SHARD), lambda i: (i, 0)),
        out_shape=jax.ShapeDtypeStruct((D_MODEL, ADA_SHARD), F32), compiler_params=_params(("parallel",)),
    )(c_all_t, dmod_shard)


def _adamw_math(w, g, m, v):
    nm = ADAM_B1 * m + (1.0 - ADAM_B1) * g
    nv = ADAM_B2 * v + (1.0 - ADAM_B2) * jnp.square(g)
    m_hat = nm / (1.0 - ADAM_B1 ** ADAM_STEP)
    v_hat = nv / (1.0 - ADAM_B2 ** ADAM_STEP)
    return -ADAM_LR * (m_hat / (jnp.sqrt(v_hat) + ADAM_EPS) + ADAM_WD * w), nm, nv


def _adamw(name, w, g, m, v):
    _, r, cdim = w.shape
    tr = 256 if r % 256 == 0 else r

    def body(w_ref, g_ref, m_ref, v_ref, go_ref, d_ref, nm_ref, nv_ref):
        go_ref[...] = g_ref[...]
        d_ref[...], nm_ref[...], nv_ref[...] = _adamw_math(w_ref[...], g_ref[...], m_ref[...], v_ref[...])

    blk = pl.BlockSpec((None, tr, cdim), lambda i: (0, i, 0))
    return pl.pallas_call(
        body, name=name, grid=(r // tr,), in_specs=[blk, pl.BlockSpec((tr, cdim), lambda i: (i, 0)), blk, blk], out_specs=[blk] * 4,
        out_shape=[jax.ShapeDtypeStruct((1, r, cdim), F32)] * 4, compiler_params=_params(("parallel",)),
    )(w, g, m, v)


def _adamw_pair(name, w, mine, other, m, v, c, by_cols=False):
    _, r, cdim = w.shape
    hr, hc = mine.shape
    tr = _row_tile(hr, SUBLANES)
    per = hr // tr

    def body(c_ref, w_ref, a_ref, b_ref, m_ref, v_ref, g_ref, d_ref, nm_ref, nv_ref):
        half = pl.program_id(1) if by_cols else pl.program_id(0) // per
        g = jnp.where(half == c_ref[0], a_ref[...], b_ref[...])
        g_ref[...] = g
        d_ref[...], nm_ref[...], nv_ref[...] = _adamw_math(w_ref[...], g, m_ref[...], v_ref[...])

    blk = pl.BlockSpec((None, tr, hc), lambda i, j, c_ref: (0, i, j))
    half = pl.BlockSpec((tr, hc), lambda i, j, c_ref: (i % per, 0))
    return pl.pallas_call(
        body, name=name,
        grid_spec=pltpu.PrefetchScalarGridSpec(num_scalar_prefetch=1, grid=(r // tr, cdim // hc),
                                               in_specs=[blk, half, half, blk, blk], out_specs=[blk] * 4),
        out_shape=[jax.ShapeDtypeStruct((1, r, cdim), F32)] * 4, compiler_params=_params(("parallel", "parallel")),
    )(jnp.reshape(c, (1,)).astype(jnp.int32), w, mine, other, m, v)


SMALL = ["b_ada", "conv_b", "dt_bias", "a_log", "d_skip", "ssm_norm_w", "f_bias", "attn_norm_w", "ln1_g", "ln1_b", "ln2_g", "ln2_b"]


def _pack(vs):
    pieces = []
    for v in vs:
        pieces.append(v)
        if v.shape[1] % LANES:
            pieces.append(jnp.zeros((1, -v.shape[1] % LANES), v.dtype))
    return jnp.concatenate(pieces, axis=1)


def _adamw_small(total, offs, ws, ms, vs):
    n = len(ws)

    def body(*refs):
        t_ref, outs = refs[0], refs[1 + 3 * n:]
        for i in range(n):
            g = t_ref[:, offs[i]:offs[i] + ws[i].shape[1]]
            dl, nm, nv = _adamw_math(refs[1 + i][...], g, refs[1 + n + i][...], refs[1 + 2 * n + i][...])
            outs[4 * i][...], outs[4 * i + 1][...], outs[4 * i + 2][...], outs[4 * i + 3][...] = g, dl, nm, nv

    res = pl.pallas_call(
        body, name="adamw_small", in_specs=[VMEM_SPEC] * (1 + 3 * n), out_specs=[VMEM_SPEC] * (4 * n),
        out_shape=[jax.ShapeDtypeStruct(w.shape, F32) for w in ws for _ in range(4)],
    )(total, *ws, *ms, *vs)
    return [res[4 * i:4 * i + 4] for i in range(n)]


def kernel(x, c, w_ada, b_ada, w_in, conv_w, conv_b, dt_bias, a_log, d_skip, ssm_norm_w, f_bias, attn_norm_w, w_out, ln1_g, ln1_b, w_ff_in, w_ff_out, ln2_g, ln2_b, loss_target, m_w_ada, m_b_ada, m_w_in, m_conv_w, m_conv_b, m_dt_bias, m_a_log, m_d_skip, m_ssm_norm_w, m_f_bias, m_attn_norm_w, m_w_out, m_ln1_g, m_ln1_b, m_w_ff_in, m_w_ff_out, m_ln2_g, m_ln2_b, v_w_ada, v_b_ada, v_w_in, v_conv_w, v_conv_b, v_dt_bias, v_a_log, v_d_skip, v_ssm_norm_w, v_f_bias, v_attn_norm_w, v_w_out, v_ln1_g, v_ln1_b, v_w_ff_in, v_w_ff_out, v_ln2_g, v_ln2_b):
    a = dict(b_ada=b_ada, conv_b=conv_b, dt_bias=dt_bias, a_log=a_log, d_skip=d_skip, ssm_norm_w=ssm_norm_w, f_bias=f_bias,
             attn_norm_w=attn_norm_w, ln1_g=ln1_g, ln1_b=ln1_b, ln2_g=ln2_g, ln2_b=ln2_b)
    ms = dict(b_ada=m_b_ada, conv_b=m_conv_b, dt_bias=m_dt_bias, a_log=m_a_log, d_skip=m_d_skip, ssm_norm_w=m_ssm_norm_w,
              f_bias=m_f_bias, attn_norm_w=m_attn_norm_w, ln1_g=m_ln1_g, ln1_b=m_ln1_b, ln2_g=m_ln2_g, ln2_b=m_ln2_b)
    vs = dict(b_ada=v_b_ada, conv_b=v_conv_b, dt_bias=v_dt_bias, a_log=v_a_log, d_skip=v_d_skip, ssm_norm_w=v_ssm_norm_w,
              f_bias=v_f_bias, attn_norm_w=v_attn_norm_w, ln1_g=v_ln1_g, ln1_b=v_ln1_b, ln2_g=v_ln2_g, ln2_b=v_ln2_b)
    xi, yi, ci = _place()
    chip = 2 * xi + yi
    me = 4 * xi + 2 * yi + ci
    d = D_MODEL
    conv_shard = CONV_DIM // N_CHIPS

    w_in_t, m_w_in_t, v_w_in_t = [jnp.transpose(t, (0, 2, 1)) for t in (w_in, m_w_in, v_w_in)]
    v8 = jnp.broadcast_to(jnp.concatenate([c, conv_w[0].reshape(1, CONV_W * conv_shard)], axis=1), (SUBLANES, d + CONV_W * conv_shard))
    b_shard = lax.dynamic_slice_in_dim(b_ada, chip * ADA_SHARD, ADA_SHARD, axis=1)
    first, parts, g_in = _prologue(v8, w_ada[0], b_shard, w_in_t[0].astype(BF16))
    first = first[:, 0]
    c_all = first[:, :d]
    conv_w_full = first[::2, d:].reshape(N_CHIPS, CONV_W, conv_shard).transpose(1, 0, 2).reshape(CONV_W, CONV_DIM)
    mod = lax.dynamic_index_in_dim(parts[::2], me, axis=1, keepdims=False).reshape(1, 6 * d)
    w_alt = _to_aligned(g_in.reshape(IN_COLS, d))

    sp = {n: a[n] for n in SMALL[1:]}
    sp["conv_w"] = conv_w_full
    shards = [w_out[0].astype(BF16), w_ff_in[0].astype(BF16), w_ff_out[0].astype(BF16)]
    dx, landed, small = _local_step(x[0], loss_target[0], mod, w_alt, shards, sp)

    names = ["mod"] + SMALL[1:]
    vec = _pack([small[n] for n in names] + [small["conv_w"].reshape(1, CONV_W * CONV_DIM), small["loss"]])
    every = _allgather_small("gather_small", vec)
    total = _sum_blocks("sum_small", jnp.broadcast_to(every, (N_DEV, SUBLANES, vec.shape[1])))[:1]
    widths = [6 * d] + [a[n].shape[1] for n in SMALL[1:]]
    offs = [0]
    for w in widths:
        offs.append(offs[-1] + w + (-w % LANES))
    g_conv_w_full = total[:, offs[-1]:offs[-1] + CONV_W * CONV_DIM].reshape(CONV_W, CONV_DIM)
    loss = total[0, offs[-1] + CONV_W * CONV_DIM]
    dmod_shard = lax.dynamic_slice_in_dim(every[:, 0, :6 * d], chip * ADA_SHARD, ADA_SHARD, axis=1)
    g_w_ada = _w_ada_grad(c_all.T, dmod_shard)
    g_conv_w = lax.dynamic_slice_in_dim(g_conv_w_full, chip * conv_shard, conv_shard, axis=1)

    mine = [_sum_blocks("dev_sum_%d" % i, p) for i, p in enumerate(landed)]
    other = _pair_swap(mine)

    grads, deltas, new_m, new_v = {}, {}, {}, {}
    paired = dict(w_in=(w_in_t, m_w_in_t, v_w_in_t), w_out=(w_out, m_w_out, v_w_out), w_ff_in=(w_ff_in, m_w_ff_in, v_w_ff_in),
                  w_ff_out=(w_ff_out, m_w_ff_out, v_w_ff_out))
    for i, (n, (w, m, v)) in enumerate(paired.items()):
        res = _adamw_pair("adamw_" + n, w, mine[i], other[i], m, v, ci, by_cols=n == "w_in")
        grads[n], deltas[n], new_m[n], new_v[n] = [jnp.transpose(t, (0, 2, 1)) for t in res] if n == "w_in" else res
    for n, g, (w, m, v) in (("w_ada", g_w_ada, (w_ada, m_w_ada, v_w_ada)), ("conv_w", g_conv_w, (conv_w, m_conv_w, v_conv_w))):
        grads[n], deltas[n], new_m[n], new_v[n] = _adamw("adamw_" + n, w, g, m, v)
    for n, res in zip(SMALL, _adamw_small(total, offs, [a[n] for n in SMALL], [ms[n] for n in SMALL], [vs[n] for n in SMALL])):
        grads[n], deltas[n], new_m[n], new_v[n] = res

    order = ["w_ada", "b_ada", "w_in", "conv_w", "conv_b", "dt_bias", "a_log", "d_skip", "ssm_norm_w", "f_bias", "attn_norm_w", "w_out",
             "ln1_g", "ln1_b", "w_ff_in", "w_ff_out", "ln2_g", "ln2_b"]
    return (loss, dx[None], *[grads[n] for n in order], *[deltas[n] for n in order], *[new_m[n] for n in order], *[new_v[n] for n in order])
```

```python
import functools

import jax
import jax.numpy as jnp
from jax import lax
from jax.experimental import pallas as pl
from jax.experimental.pallas import tpu as pltpu

F32, BF16 = jnp.float32, jnp.bfloat16

D_MODEL = 1024
N_HEADS = 16
HEAD_DIM = 64
N_PAIRS = N_HEADS // 2
SSM_GROUPS = 2
SSM_STATE = 128
CHUNK = 128
CONV_W = 4
CONV_DIM = 1536
D_FF = 4096
IN_COLS = 5664
ALPHA = 2.0 ** 0.25
LN_EPS = 1e-5
RMS_EPS = 1e-5
LANES = 128
SUBLANES = 8

AL_Z, AL_XS, AL_Q, AL_K, AL_V, AL_B, AL_C, AL_DTF = 0, 1024, 2048, 3072, 4096, 5120, 5376, 5632
AL_COLS = 5760
F_LANE = 16

ADAM_LR, ADAM_B1, ADAM_B2, ADAM_EPS, ADAM_WD, ADAM_STEP = 0.001, 0.9, 0.999, 1e-08, 0.01, 10

VMEM_LIMIT = 56 * 1024 * 1024
MESH = pl.DeviceIdType.MESH


def _params(sem=None):
    return pltpu.CompilerParams(dimension_semantics=sem, vmem_limit_bytes=VMEM_LIMIT)


def _sigmoid(x):
    return 1.0 / (1.0 + jnp.exp(-x))


def _silu(x):
    return x * _sigmoid(x)


def _softplus(x):
    return jnp.maximum(x, 0.0) + jnp.log(1.0 + jnp.exp(-jnp.abs(x)))


def _split3(a):
    hi = a.astype(BF16)
    r = a - hi.astype(F32)
    mid = r.astype(BF16)
    lo = (r - mid.astype(F32)).astype(BF16)
    return hi, mid, lo


def _dot(a, b, dims=((1,), (0,))):
    return lax.dot_general(a, b, (dims, ((), ())), preferred_element_type=F32)


NN, NT, TN = ((1,), (0,)), ((1,), (1,)), ((0,), (0,))


def _dot3(t, a):
    hi, mid, lo = _split3(a)
    return _dot(t, hi) + _dot(t, mid) + _dot(t, lo)


def _matmul(name, a, b, *, dims=NN, out_dtype=F32, tm=1024, tn=1024, tk=1024, by_chip=None, epi=None, carry=()):
    if dims == NN:
        (m, k), n = a.shape, b.shape[1]
    elif dims == NT:
        (m, k), n = a.shape, b.shape[0]
    else:
        (k, m), n = a.shape, b.shape[1]
    if by_chip == "rows":
        tm = min(tm, m // 4)
    if by_chip == "cols":
        tn = min(tn, n // 4)
    tm, tn, tk = min(tm, m), min(tn, n), min(tk, k)
    assert m % tm == 0 and n % tn == 0 and k % tk == 0, (name, m, n, k, tm, tn, tk)
    nk = k // tk
    if by_chip == "rows":
        per = m // 4 // tm
        out_spec = pl.BlockSpec((None, tm, tn), lambda i, j, l: (i // per, i % per, j))
        out_shape = jax.ShapeDtypeStruct((4, m // 4, n), out_dtype)
    elif by_chip == "cols":
        per = n // 4 // tn
        out_spec = pl.BlockSpec((None, tm, tn), lambda i, j, l: (j // per, i, j % per))
        out_shape = jax.ShapeDtypeStruct((4, m, n // 4), out_dtype)
    else:
        out_spec = pl.BlockSpec((tm, tn), lambda i, j, l: (i, j))
        out_shape = jax.ShapeDtypeStruct((m, n), out_dtype)
    a_spec = pl.BlockSpec((tk, tm), lambda i, j, l: (l, i)) if dims == TN else pl.BlockSpec((tm, tk), lambda i, j, l: (i, l))
    b_spec = pl.BlockSpec((tn, tk), lambda i, j, l: (j, l)) if dims == NT else pl.BlockSpec((tk, tn), lambda i, j, l: (l, j))

    tile = pl.BlockSpec((tm, tn), lambda i, j, l: (i, j))
    in_specs, args, out_specs, out_shape = [a_spec, b_spec], [a, b], [out_spec], [out_shape]
    fn, n_tiles, n_sums = None, 1, 0
    if epi is not None:
        fn, fulls, vecs, outs, sums = epi
        assert by_chip is None and (not sums or n == tn), name
        in_specs = in_specs + [tile] * len(fulls) + [pl.BlockSpec((1, tn), lambda i, j, l: (0, j))] * len(vecs)
        args = args + list(fulls) + list(vecs)
        out_specs = [tile] * len(outs) + [pl.BlockSpec((1, w), lambda i, j, l: (0, 0)) for w in sums]
        out_shape = [jax.ShapeDtypeStruct((m, n), dt) for dt in outs] + [jax.ShapeDtypeStruct((1, w), F32) for w in sums]
        n_tiles, n_sums = len(outs), len(sums)
    n_in, n_out, n_c = len(args), len(out_specs), len(carry)
    scratch = [pltpu.VMEM((tm, tn) if nk > 1 else (SUBLANES, LANES), F32)]
    if n_c:
        in_specs, args = in_specs + [ANY] * n_c, args + list(carry)
        out_specs = out_specs + [ANY] * n_c
        out_shape = out_shape + [jax.ShapeDtypeStruct(g.shape, g.dtype) for g in carry]
        scratch = scratch + _exchange_sems(n_c)
    gm, gn = m // tm, n // tn

    def body(*refs):
        a_ref, b_ref = refs[:2]
        ins, outs = refs[2:n_in], refs[n_in + n_c:n_in + n_c + n_out]
        acc_ref = refs[n_in + 2 * n_c + n_out]
        i, j, l = pl.program_id(0), pl.program_id(1), pl.program_id(2)
        if n_c:
            start, wait = _scatter_plan(refs[n_in:n_in + n_c], refs[n_in + n_c + n_out:n_in + 2 * n_c + n_out], *refs[n_in + 2 * n_c + n_out + 1:])
            pl.when((i == 0) & (j == 0) & (l == 0))(start)
        part = _dot(a_ref[...].astype(BF16), b_ref[...].astype(BF16), dims)

        def finish(res):
            if fn is None:
                outs[0][...] = res.astype(outs[0].dtype)
                return
            tiles, colsums = fn(res, *[r[...] for r in ins])
            for r, val in zip(outs[:n_tiles], tiles):
                r[...] = val.astype(r.dtype)
            if n_sums:
                @pl.when(i == 0)
                def _():
                    for r in outs[n_tiles:]:
                        r[...] = jnp.zeros_like(r)
                for r, val in zip(outs[n_tiles:], colsums):
                    r[...] += val

        if nk == 1:
            finish(part)
        else:
            @pl.when(l == 0)
            def _():
                acc_ref[...] = part

            @pl.when((l > 0) & (l < nk - 1))
            def _():
                acc_ref[...] += part

            @pl.when(l == nk - 1)
            def _():
                finish(acc_ref[...] + part)

        if n_c:
            pl.when((i == gm - 1) & (j == gn - 1) & (l == nk - 1))(wait)

    res = pl.pallas_call(
        body, name=name, grid=(gm, gn, nk),
        in_specs=in_specs, out_specs=out_specs, out_shape=out_shape, scratch_shapes=scratch,
        compiler_params=_params(("arbitrary",) * 3 if n_c or n_sums else ("parallel", "parallel", "arbitrary")),
    )(*args)
    return res[0] if len(res) == 1 else res


def _rowwise(name, fn, fulls, vecs, out_fulls, out_vecs, tr=256):
    fulls = [f if isinstance(f, tuple) else (f, f.shape[1], 0) for f in fulls]
    s = fulls[0][0].shape[0]
    tr = min(tr, s)
    out_fulls = [o if len(o) == 3 else (*o, (o[0], 0, None)) for o in out_fulls]
    into = [(k, slab[2]) for k, (_, _, slab) in enumerate(out_fulls) if slab[2] is not None]
    nf, nv, nof, nov = len(fulls), len(vecs), len(out_fulls), len(out_vecs)
    in_specs = [pl.BlockSpec((tr, w), functools.partial(lambda i, cb: (i, cb), cb=cb)) for (_, w, cb) in fulls]
    in_specs += [pl.BlockSpec(v.shape, lambda i: (0, 0)) for v in vecs] + [ANY] * len(into)
    out_shape = [jax.ShapeDtypeStruct((s, slab[0]), dt) for (_, dt, slab) in out_fulls] + [jax.ShapeDtypeStruct((1, w), F32) for w in out_vecs]
    out_specs = [pl.BlockSpec((tr, w), functools.partial(lambda i, cb: (i, cb), cb=slab[1])) for (w, _, slab) in out_fulls]
    out_specs += [pl.BlockSpec((1, w), lambda i: (0, 0)) for w in out_vecs]

    def body(*refs):
        outs = refs[nf + nv + len(into):]
        of, ov = fn(*[r[...] for r in refs[:nf + nv]])
        for r, val in zip(outs[:nof], of):
            r[...] = val.astype(r.dtype)
        if nov:
            @pl.when(pl.program_id(0) == 0)
            def _():
                for r in outs[nof:]:
                    r[...] = jnp.zeros_like(r)
            for r, val in zip(outs[nof:], ov):
                r[...] += val

    res = pl.pallas_call(
        body, name=name, grid=(s // tr,), in_specs=in_specs, out_specs=out_specs, out_shape=out_shape,
        input_output_aliases={nf + nv + pos: k for pos, (k, _) in enumerate(into)},
        compiler_params=_params(("arbitrary",)),
    )(*[f[0] for f in fulls], *vecs, *[buf for _, buf in into])
    return res[:nof], res[nof:]


def _colsum(x):
    return jnp.sum(x, axis=0, keepdims=True)


def _rowmean(x):
    return jnp.mean(x, axis=-1, keepdims=True)


CONV_CB = 512
CONV_TR = 512


def _shift_down(u, halo, j):
    if j == 0:
        return u
    ru = pltpu.roll(u, j, 0)
    row8 = lax.broadcasted_iota(jnp.int32, halo.shape, 0)
    top = jnp.where(row8 < j, pltpu.roll(halo, j, 0), ru[:SUBLANES])
    return jnp.concatenate([top, ru[SUBLANES:]], axis=0)


def _shift_up(d, halo, j):
    if j == 0:
        return d
    tr = d.shape[0]
    rd = pltpu.roll(d, tr - j, 0)
    row8 = lax.broadcasted_iota(jnp.int32, halo.shape, 0)
    bot = jnp.where(row8 >= SUBLANES - j, pltpu.roll(halo, SUBLANES - j, 0), rd[tr - SUBLANES:])
    return jnp.concatenate([rd[:tr - SUBLANES], bot], axis=0)


def _conv_col(cb):
    return jnp.where(cb < 2, AL_XS // CONV_CB + cb, AL_B // CONV_CB)


def _conv_specs(s, tr):
    per8 = tr // SUBLANES
    blk = pl.BlockSpec((tr, CONV_CB), lambda cb, i: (i, _conv_col(cb)))
    prev = pl.BlockSpec((SUBLANES, CONV_CB), lambda cb, i: (jnp.maximum(i * per8 - 1, 0), _conv_col(cb)))
    return blk, prev


def _conv_pre(u, halo, w_ref, b_ref, first):
    halo = jnp.where(first, 0.0, halo)
    acc = b_ref[...] + w_ref[CONV_W - 1:CONV_W, :] * u
    shifted = [u]
    for j in range(1, CONV_W):
        sh = _shift_down(u, halo, j)
        shifted.append(sh)
        acc = acc + w_ref[CONV_W - 1 - j:CONV_W - j, :] * sh
    return acc, shifted


def _conv_fwd(proj, conv_w, conv_b):
    s = proj.shape[0]
    tr = min(CONV_TR, s)
    blk, prev = _conv_specs(s, tr)

    def body(u_ref, h_ref, w_ref, b_ref, o_ref):
        pre, _ = _conv_pre(u_ref[...], h_ref[...], w_ref, b_ref, pl.program_id(1) == 0)
        o_ref[...] = _silu(pre)

    return pl.pallas_call(
        body, name="conv_fwd", grid=(CONV_DIM // CONV_CB, s // tr),
        in_specs=[blk, prev, pl.BlockSpec((CONV_W, CONV_CB), lambda cb, i: (0, cb)), pl.BlockSpec((1, CONV_CB), lambda cb, i: (0, cb))],
        out_specs=pl.BlockSpec((tr, CONV_CB), lambda cb, i: (i, cb)),
        out_shape=jax.ShapeDtypeStruct((s, CONV_DIM), F32),
        compiler_params=_params(("parallel", "parallel")),
    )(proj, proj, conv_w, conv_b)


def _conv_bwd_pre(proj, conv_w, conv_b, dxc):
    s = proj.shape[0]
    tr = min(CONV_TR, s)
    blk, prev = _conv_specs(s, tr)

    def body(u_ref, h_ref, w_ref, b_ref, d_ref, dpre_ref, dw_ref, db_ref):
        i = pl.program_id(1)
        pre, shifted = _conv_pre(u_ref[...], h_ref[...], w_ref, b_ref, i == 0)
        sg = _sigmoid(pre)
        dpre = d_ref[...] * (sg * (1.0 + pre * (1.0 - sg)))
        dpre_ref[...] = dpre

        @pl.when(i == 0)
        def _():
            dw_ref[...] = jnp.zeros_like(dw_ref)
            db_ref[...] = jnp.zeros_like(db_ref)

        db_ref[...] += _colsum(dpre)
        for j in range(CONV_W):
            dw_ref[CONV_W - 1 - j:CONV_W - j, :] += _colsum(dpre * shifted[j])

    own = pl.BlockSpec((tr, CONV_CB), lambda cb, i: (i, cb))
    wspec = pl.BlockSpec((CONV_W, CONV_CB), lambda cb, i: (0, cb))
    bspec = pl.BlockSpec((1, CONV_CB), lambda cb, i: (0, cb))
    return pl.pallas_call(
        body, name="conv_bwd_pre", grid=(CONV_DIM // CONV_CB, s // tr),
        in_specs=[blk, prev, wspec, bspec, own], out_specs=[own, wspec, bspec],
        out_shape=[jax.ShapeDtypeStruct((s, CONV_DIM), F32), jax.ShapeDtypeStruct((CONV_W, CONV_DIM), F32),
                   jax.ShapeDtypeStruct((1, CONV_DIM), F32)],
        compiler_params=_params(("parallel", "arbitrary")),
    )(proj, proj, conv_w, conv_b, dxc)


def _conv_bwd_in(dpre, conv_w, dproj):
    s = dpre.shape[0]
    tr = min(CONV_TR, s)
    per8 = tr // SUBLANES
    last8 = s // SUBLANES - 1
    nb = s // tr

    def body(d_ref, n_ref, w_ref, _, o_ref):
        d = d_ref[...]
        halo = jnp.where(pl.program_id(1) == nb - 1, 0.0, n_ref[...])
        acc = w_ref[CONV_W - 1:CONV_W, :] * d
        for j in range(1, CONV_W):
            acc = acc + w_ref[CONV_W - 1 - j:CONV_W - j, :] * _shift_up(d, halo, j)
        o_ref[...] = acc.astype(o_ref.dtype)

    own = pl.BlockSpec((tr, CONV_CB), lambda cb, i: (i, cb))
    nxt = pl.BlockSpec((SUBLANES, CONV_CB), lambda cb, i: (jnp.minimum((i + 1) * per8, last8), cb))
    return pl.pallas_call(
        body, name="conv_bwd_in", grid=(CONV_DIM // CONV_CB, nb),
        in_specs=[own, nxt, pl.BlockSpec((CONV_W, CONV_CB), lambda cb, i: (0, cb)), ANY],
        out_specs=pl.BlockSpec((tr, CONV_CB), lambda cb, i: (i, _conv_col(cb))),
        out_shape=jax.ShapeDtypeStruct(dproj.shape, dproj.dtype), input_output_aliases={3: 0},
        compiler_params=_params(("parallel", "parallel")),
    )(dpre, dpre, conv_w, dproj)


XC_B, XC_C = 1024, 1280


def _tile_iotas():
    row = lax.broadcasted_iota(jnp.int32, (CHUNK, LANES), 0)
    lane = lax.broadcasted_iota(jnp.int32, (CHUNK, LANES), 1)
    return row, lane


def _ssd_scalars(dtf_ref, bias_ref, alog_ref, row, lane):
    head = lane[:1] < N_HEADS
    raw = dtf_ref[...] + bias_ref[...]
    dt = _softplus(raw)
    a_neg = jnp.where(head, -jnp.exp(alog_ref[...]), 0.0)
    a = dt * a_neg
    tril = (row >= lane).astype(BF16)
    s = _dot3(tril, a)
    return raw, dt, a_neg, s


def _pair(v, j, lo):
    return jnp.where(lo, v[:, 2 * j:2 * j + 1], v[:, 2 * j + 1:2 * j + 2])


def _head_sum(x, lo, hh):
    return jnp.sum(jnp.where(lo == (hh == 0), x, 0.0), axis=1, keepdims=True)


def _decay_masks(s, st, h, row, lane):
    s_col = jnp.broadcast_to(s[:, h:h + 1], (CHUNK, LANES))
    s_row = jnp.broadcast_to(st[h:h + 1, :], (CHUNK, LANES))
    lm = jnp.where(row >= lane, jnp.exp(s_col - s_row), 0.0)
    lmt = jnp.where(row <= lane, jnp.exp(s_row - s_col), 0.0)
    return lm, lmt


def _ssd_fwd(xc_all, proj, dt_bias_l, a_log_l, d_exp):
    s_len = xc_all.shape[0]
    nc = s_len // CHUNK

    def body(x_ref, dtf_ref, bias_ref, alog_ref, dexp_ref, y_ref, prevs_ref, state_ref):
        @pl.when(pl.program_id(0) == 0)
        def _():
            state_ref[...] = jnp.zeros_like(state_ref)

        row, lane = _tile_iotas()
        lo = lane < HEAD_DIM
        _, dt, _, s = _ssd_scalars(dtf_ref, bias_ref, alog_ref, row, lane)
        tot = s[CHUNK - 1:CHUNK, :]
        st = s.T
        for g in range(SSM_GROUPS):
            bg = x_ref[:, XC_B + g * SSM_STATE:XC_B + (g + 1) * SSM_STATE].astype(BF16)
            cg = x_ref[:, XC_C + g * SSM_STATE:XC_C + (g + 1) * SSM_STATE].astype(BF16)
            cb = _dot(cg, bg, NT)
            for j in range(g * 4, g * 4 + 4):
                xs_p = x_ref[:, j * LANES:(j + 1) * LANES]
                dt_p, s_p, tot_p = _pair(dt, j, lo), _pair(s, j, lo), _pair(tot, j, lo[:1])
                xc_p = xs_p * dt_p
                xc_b = xc_p.astype(BF16)
                yd = []
                for hh in range(2):
                    lm, _ = _decay_masks(s, st, 2 * j + hh, row, lane)
                    yd.append(_dot((cb * lm).astype(BF16), xc_b))
                prev = state_ref[j]
                prevs_ref[0, j] = prev
                yo = _dot(cg, prev.astype(BF16)) * jnp.exp(s_p)
                y_ref[:, j * LANES:(j + 1) * LANES] = jnp.where(lo, yd[0], yd[1]) + yo + dexp_ref[:, j * LANES:(j + 1) * LANES] * xs_p
                to_end = jnp.exp(tot_p - s_p)
                state_ref[j] = jnp.exp(tot_p) * prev + _dot(bg, (xc_p * to_end).astype(BF16), TN)

    vec = lambda w: pl.BlockSpec((1, w), lambda c: (0, 0))
    return pl.pallas_call(
        body, name="ssd_fwd", grid=(nc,),
        in_specs=[pl.BlockSpec((CHUNK, CONV_DIM), lambda c: (c, 0)), pl.BlockSpec((CHUNK, LANES), lambda c: (c, AL_DTF // LANES)),
                  vec(LANES), vec(LANES), vec(D_MODEL)],
        out_specs=[pl.BlockSpec((CHUNK, D_MODEL), lambda c: (c, 0)), pl.BlockSpec((1, N_PAIRS, SSM_STATE, LANES), lambda c: (c, 0, 0, 0))],
        out_shape=[jax.ShapeDtypeStruct((s_len, D_MODEL), F32), jax.ShapeDtypeStruct((nc, N_PAIRS, SSM_STATE, LANES), F32)],
        scratch_shapes=[pltpu.VMEM((N_PAIRS, SSM_STATE, LANES), F32)],
        compiler_params=_params(("arbitrary",)),
    )(xc_all, proj, dt_bias_l, a_log_l, d_exp)


def _ssd_bwd(xc_all, proj, dt_bias_l, a_log_l, d_exp, prevs, dy):
    s_len = xc_all.shape[0]
    nc = s_len // CHUNK

    def body(x_ref, dtf_ref, bias_ref, alog_ref, dexp_ref, prevs_ref, dy_ref, dx_ref, ddt_ref, da_ref, dd_ref, dbias_ref, dstate_ref):
        @pl.when(pl.program_id(0) == 0)
        def _():
            dstate_ref[...] = jnp.zeros_like(dstate_ref)
            da_ref[...] = jnp.zeros_like(da_ref)
            dd_ref[...] = jnp.zeros_like(dd_ref)
            dbias_ref[...] = jnp.zeros_like(dbias_ref)

        row, lane = _tile_iotas()
        lo = lane < HEAD_DIM
        last = row == CHUNK - 1
        raw, dt, a_neg, s = _ssd_scalars(dtf_ref, bias_ref, alog_ref, row, lane)
        tot = s[CHUNK - 1:CHUNK, :]
        st = s.T
        ds_acc = jnp.zeros((CHUNK, LANES), F32)
        ddt_acc = jnp.zeros((CHUNK, LANES), F32)
        for g in range(SSM_GROUPS):
            bcol = slice(XC_B + g * SSM_STATE, XC_B + (g + 1) * SSM_STATE)
            ccol = slice(XC_C + g * SSM_STATE, XC_C + (g + 1) * SSM_STATE)
            bg = x_ref[:, bcol].astype(BF16)
            cg = x_ref[:, ccol].astype(BF16)
            cb = _dot(cg, bg, NT)
            cbt = _dot(bg, cg, NT)
            dcb = jnp.zeros((CHUNK, LANES), F32)
            dcbt = jnp.zeros((CHUNK, LANES), F32)
            db_acc = jnp.zeros((CHUNK, LANES), F32)
            dc_acc = jnp.zeros((CHUNK, LANES), F32)
            for j in range(g * 4, g * 4 + 4):
                cols = slice(j * LANES, (j + 1) * LANES)
                xs_p, dy_p = x_ref[:, cols], dy_ref[:, cols]
                dt_p, s_p, tot_p = _pair(dt, j, lo), _pair(s, j, lo), _pair(tot, j, lo[:1])
                xc_p = xs_p * dt_p
                xc_b, dy_b = xc_p.astype(BF16), dy_p.astype(BF16)
                e_p, f_p, etot_p = jnp.exp(s_p), jnp.exp(tot_p - s_p), jnp.exp(tot_p)
                prev, dnext = prevs_ref[0, j], dstate_ref[j]
                prev_b, dnext_b = prev.astype(BF16), dnext.astype(BF16)
                dd_ref[:, cols] += _colsum(dy_p * xs_p)
                dxs_p = dexp_ref[:, cols] * dy_p
                cp = _dot(cg, prev_b)
                gy = (dy_p * e_p).astype(BF16)
                dc_acc += _dot(gy, prev_b, NT)
                dstate_ref[j] = etot_p * dnext + _dot(cg, gy, TN)
                de = dy_p * cp * e_p
                bds = _dot(bg, dnext_b)
                db_acc += _dot((xc_p * f_p).astype(BF16), dnext_b, NT)
                dxc_p = bds * f_p
                df = bds * xc_p * f_p
                dtot_p = _colsum(dnext * prev) * etot_p + _colsum(df)
                dsl = de - df + jnp.where(last, dtot_p, 0.0)
                for hh in range(2):
                    h = 2 * j + hh
                    mine = lo == (hh == 0)
                    lm, lmt = _decay_masks(s, st, h, row, lane)
                    dy_h = jnp.where(mine, dy_p, 0.0).astype(BF16)
                    xc_h = jnp.where(mine, xc_p, 0.0).astype(BF16)
                    dm = _dot(dy_h, xc_b, NT)
                    dmt = _dot(xc_h, dy_b, NT)
                    mt = cbt * lmt
                    dxc_p += _dot(mt.astype(BF16), dy_h)
                    dml, dmtl = dm * lm, dmt * lmt
                    ds_h = jnp.sum(dml * cb - dmtl * cbt + jnp.where(mine, dsl, 0.0), axis=1, keepdims=True)
                    ds_acc += jnp.where(lane == h, ds_h, 0.0)
                    dcb += dml
                    dcbt += dmtl
                    ddt_acc += jnp.where(lane == h, _head_sum(dxc_p * xs_p, lo, hh), 0.0)
                dx_ref[:, cols] = dxs_p + dxc_p * dt_p
            dx_ref[:, ccol] = dc_acc + _dot(dcb.astype(BF16), bg)
            dx_ref[:, bcol] = db_acc + _dot(dcbt.astype(BF16), cg)
        triu = (row <= lane).astype(BF16)
        da = _dot3(triu, ds_acc)
        ddt = ddt_acc + da * a_neg
        da_ref[...] += _colsum(da * dt) * a_neg[:1]
        ddt_raw = jnp.where(lane < N_HEADS, ddt * _sigmoid(raw), 0.0)
        dbias_ref[...] += _colsum(ddt_raw)
        ddt_ref[...] = ddt_raw

    rev = lambda c: nc - 1 - c
    vec = lambda w: pl.BlockSpec((1, w), lambda c: (0, 0))
    return pl.pallas_call(
        body, name="ssd_bwd", grid=(nc,),
        in_specs=[pl.BlockSpec((CHUNK, CONV_DIM), lambda c: (rev(c), 0)), pl.BlockSpec((CHUNK, LANES), lambda c: (rev(c), AL_DTF // LANES)),
                  vec(LANES), vec(LANES), vec(D_MODEL),
                  pl.BlockSpec((1, N_PAIRS, SSM_STATE, LANES), lambda c: (rev(c), 0, 0, 0)),
                  pl.BlockSpec((CHUNK, D_MODEL), lambda c: (rev(c), 0))],
        out_specs=[pl.BlockSpec((CHUNK, CONV_DIM), lambda c: (rev(c), 0)), pl.BlockSpec((CHUNK, LANES), lambda c: (rev(c), 0)),
                   vec(LANES), vec(D_MODEL), vec(LANES)],
        out_shape=[jax.ShapeDtypeStruct((s_len, CONV_DIM), F32), jax.ShapeDtypeStruct((s_len, LANES), F32),
                   jax.ShapeDtypeStruct((1, LANES), F32), jax.ShapeDtypeStruct((1, D_MODEL), F32), jax.ShapeDtypeStruct((1, LANES), F32)],
        scratch_shapes=[pltpu.VMEM((N_PAIRS, SSM_STATE, LANES), F32)],
        compiler_params=_params(("arbitrary",)),
    )(xc_all, proj, dt_bias_l, a_log_l, d_exp, prevs, dy)


AUG_C, AUG_ONE = 64, 67
NEG = -1e30
ATT_T = 512


def _fox_cum(proj, f_bias_l):
    s_len = proj.shape[0]
    nc = s_len // CHUNK

    def body(dtf_ref, fb_ref, cum_ref):
        row, lane = _tile_iotas()
        tril = (row >= lane).astype(BF16)

        def step(c, carry):
            rows = pl.ds(pl.multiple_of(c * CHUNK, CHUNK), CHUNK)
            lf = -_softplus(-(dtf_ref[rows, :] + fb_ref[...]))
            lf = jnp.where(lane < N_HEADS, pltpu.roll(lf, LANES - F_LANE, 1), 0.0)
            cs = _dot3(tril, lf) + carry
            cum_ref[rows, :] = cs
            return cs[CHUNK - 1:CHUNK, :]

        lax.fori_loop(0, nc, step, jnp.zeros((1, LANES), F32))

    return pl.pallas_call(
        body, name="fox_cum", grid=(1,),
        in_specs=[pl.BlockSpec((s_len, LANES), lambda i: (0, AL_DTF // LANES)), pl.BlockSpec((1, LANES), lambda i: (0, 0))],
        out_specs=pl.BlockSpec((s_len, LANES), lambda i: (0, 0)),
        out_shape=jax.ShapeDtypeStruct((s_len, LANES), F32),
        compiler_params=_params(("arbitrary",)),
    )(proj, f_bias_l)


def _fox_cum_bwd(dcum, proj, f_bias_l, ddt_tile, dproj):
    s_len = proj.shape[0]
    nc = s_len // CHUNK

    def body(dcum_ref, dtf_ref, fb_ref, ddt_ref, _, out_ref, dfb_ref):
        row, lane = _tile_iotas()
        triu = (row <= lane).astype(BF16)
        is_f = (lane >= F_LANE) & (lane < F_LANE + N_HEADS)

        def step(t, carry):
            run, dfb = carry
            rows = pl.ds(pl.multiple_of((nc - 1 - t) * CHUNK, CHUNK), CHUNK)
            rc = _dot3(triu, dcum_ref[rows, :]) + run
            sg = _sigmoid(-(dtf_ref[rows, :] + fb_ref[...]))
            df = jnp.where(is_f, pltpu.roll(rc, F_LANE, 1) * sg, 0.0)
            out_ref[rows, :] = (df + ddt_ref[rows, :]).astype(out_ref.dtype)
            return rc[0:1, :], dfb + _colsum(df)

        _, dfb = lax.fori_loop(0, nc, step, (jnp.zeros((1, LANES), F32), jnp.zeros((1, LANES), F32)))
        dfb_ref[...] = dfb

    whole = pl.BlockSpec((s_len, LANES), lambda i: (0, 0))
    dtf_cols = pl.BlockSpec((s_len, LANES), lambda i: (0, AL_DTF // LANES))
    vec = pl.BlockSpec((1, LANES), lambda i: (0, 0))
    return pl.pallas_call(
        body, name="fox_cum_bwd", grid=(1,),
        in_specs=[whole, dtf_cols, vec, whole, ANY], out_specs=[dtf_cols, vec],
        out_shape=[jax.ShapeDtypeStruct(dproj.shape, dproj.dtype), jax.ShapeDtypeStruct((1, LANES), F32)],
        input_output_aliases={4: 0}, compiler_params=_params(("arbitrary",)),
    )(dcum, proj, f_bias_l, ddt_tile, dproj)


def _attn_prep(proj, cum):
    s_len = proj.shape[0]
    tr = min(512, s_len)

    def body(q_ref, k_ref, v_ref, cum_ref, qa_ref, ka_ref, vb_ref):
        p = pl.program_id(0)
        lane = lax.broadcasted_iota(jnp.int32, (tr, LANES), 1)
        lo = lane < HEAD_DIM
        c = cum_ref[...]
        c1 = c.astype(BF16).astype(F32)
        r = c - c1
        c2 = r.astype(BF16).astype(F32)
        c3 = (r - c2).astype(BF16).astype(F32)
        q, k = q_ref[...] * (HEAD_DIM ** -0.5), k_ref[...]
        for hh in range(2):
            col = lambda x: jnp.sum(jnp.where(lane == 2 * p + hh, x, 0.0), axis=1, keepdims=True)
            a1, a2, a3 = col(c1), col(c2), col(c3)
            qh = q if hh == 0 else pltpu.roll(q, HEAD_DIM, 1)
            kh = k if hh == 0 else pltpu.roll(k, HEAD_DIM, 1)
            q_aug = jnp.where(lane == AUG_C, a1, jnp.where(lane == AUG_C + 1, a2, jnp.where(lane == AUG_C + 2, a3,
                              jnp.where(lane < AUG_ONE + 3, 1.0, 0.0))))
            k_aug = jnp.where(lane < AUG_ONE, 1.0, jnp.where(lane == AUG_ONE, -a1, jnp.where(lane == AUG_ONE + 1, -a2,
                              jnp.where(lane == AUG_ONE + 2, -a3, 0.0))))
            qa_ref[hh] = jnp.where(lo, qh, q_aug).astype(BF16)
            ka_ref[hh] = jnp.where(lo, kh, k_aug).astype(BF16)
        vb_ref[...] = v_ref[...].astype(BF16)

    slab = lambda col0: pl.BlockSpec((tr, LANES), lambda p, i: (i, col0 // LANES + p))
    heads = pl.BlockSpec((2, tr, LANES), lambda p, i: (p, i, 0))
    return pl.pallas_call(
        body, name="attn_prep", grid=(N_PAIRS, s_len // tr),
        in_specs=[slab(AL_Q), slab(AL_K), slab(AL_V), pl.BlockSpec((tr, LANES), lambda p, i: (i, 0))],
        out_specs=[heads, heads, pl.BlockSpec((tr, LANES), lambda p, i: (i, p))],
        out_shape=[jax.ShapeDtypeStruct((N_HEADS, s_len, LANES), BF16), jax.ShapeDtypeStruct((N_HEADS, s_len, LANES), BF16),
                   jax.ShapeDtypeStruct((s_len, D_MODEL), BF16)],
        compiler_params=_params(("parallel", "parallel")),
    )(proj, proj, proj, cum)


def _attn_fwd(qa, ka, vb, halves):
    s_len = vb.shape[0]
    t = min(ATT_T, s_len)
    nq = s_len // t
    n = len(halves)

    def body(qa_ref, ka_ref, vb_ref, *rest):
        o_ref, lse_ref = rest[n:n + 2]
        start, finish = _gather_plan(rest[:n], rest[n + 2:2 * n + 2], *rest[2 * n + 2:])
        i = pl.program_id(1)
        pl.when((pl.program_id(0) == 0) & (i == 0))(start)
        row = lax.broadcasted_iota(jnp.int32, (t, t), 0)
        col = lax.broadcasted_iota(jnp.int32, (t, t), 1)
        lo = lax.broadcasted_iota(jnp.int32, (t, LANES), 1) < HEAD_DIM
        qs = (qa_ref[0], qa_ref[1])

        def block(j, carry, masked):
            rows = pl.ds(pl.multiple_of(j * t, t), t)
            v = vb_ref[rows, :]
            new = []
            for hh in range(2):
                m, l, acc = carry[hh]
                s = _dot(qs[hh], ka_ref[hh, rows, :], NT)
                if masked:
                    s = jnp.where(row >= col, s, NEG)
                m_new = jnp.maximum(m, jnp.max(s, axis=1, keepdims=True))
                alpha = jnp.exp(m - m_new)
                p = jnp.exp(s - m_new)
                new.append((m_new, alpha * l + jnp.sum(p, axis=1, keepdims=True), alpha * acc + _dot(p.astype(BF16), v)))
            return tuple(new)

        init = (jnp.full((t, 1), NEG, F32), jnp.zeros((t, 1), F32), jnp.zeros((t, LANES), F32))
        carry = lax.fori_loop(0, i, functools.partial(block, masked=False), (init, init))
        (m0, l0, acc0), (m1, l1, acc1) = block(i, carry, True)
        o_ref[...] = jnp.where(lo, acc0 / l0, acc1 / l1)
        lse_ref[...] = jnp.where(lo, m0 + jnp.log(l0), m1 + jnp.log(l1))
        pl.when((pl.program_id(0) == N_PAIRS - 1) & (i == nq - 1))(finish)

    out = pl.BlockSpec((t, LANES), lambda p, i: (i, p))
    res = pl.pallas_call(
        body, name="attn_fwd", grid=(N_PAIRS, nq),
        in_specs=[pl.BlockSpec((2, t, LANES), lambda p, i: (p, i, 0)), pl.BlockSpec((2, s_len, LANES), lambda p, i: (p, 0, 0)),
                  pl.BlockSpec((s_len, LANES), lambda p, i: (0, p))] + [ANY] * n,
        out_specs=[out, out] + [ANY] * n,
        out_shape=[jax.ShapeDtypeStruct((s_len, D_MODEL), F32), jax.ShapeDtypeStruct((s_len, D_MODEL), F32)]
        + [jax.ShapeDtypeStruct((N_CHIPS, *h.shape), h.dtype) for h in halves],
        scratch_shapes=_exchange_sems(n),
        compiler_params=_params(("arbitrary", "arbitrary")),
    )(qa, ka, vb, *halves)
    return res[0], res[1], res[2:]


def _attn_bwd(qa, ka, vb, o, lse, do, parts, dproj):
    s_len = vb.shape[0]
    t = min(ATT_T, s_len)
    nq = s_len // t
    n = len(parts)

    def body(qa_ref, ka_ref, vb_ref, o_ref, lse_ref, do_ref, *rest):
        dqa_ref, dka_ref, dv_ref = rest[n + 1:n + 4]
        start, finish = _reduce_plan(rest[:n], rest[n + 4:2 * n + 4], *rest[2 * n + 4:])
        j = pl.program_id(1)
        pl.when((pl.program_id(0) == 0) & (j == 0))(start)

        @pl.when(j == 0)
        def _():
            dqa_ref[...] = jnp.zeros_like(dqa_ref)

        row = lax.broadcasted_iota(jnp.int32, (t, t), 0)
        col = lax.broadcasted_iota(jnp.int32, (t, t), 1)
        lo = lax.broadcasted_iota(jnp.int32, (t, LANES), 1) < HEAD_DIM
        v = vb_ref[...]
        ks = (ka_ref[0], ka_ref[1])

        def block(i, carry, masked):
            dk, dv = list(carry[:2]), carry[2]
            rows = pl.ds(pl.multiple_of(i * t, t), t)
            do_p, o_p, lse_p = do_ref[rows, :], o_ref[rows, :], lse_ref[rows, :]
            for hh in range(2):
                q = qa_ref[hh, rows, :]
                do_h = jnp.where(lo == (hh == 0), do_p, 0.0)
                delta = jnp.sum(do_h * o_p, axis=1, keepdims=True)
                s = _dot(q, ks[hh], NT)
                if masked:
                    s = jnp.where(row >= col, s, NEG)
                p = jnp.exp(s - lse_p[:, hh * HEAD_DIM:hh * HEAD_DIM + 1])
                do_b = do_h.astype(BF16)
                ds = (p * (_dot(do_b, v, NT) - delta)).astype(BF16)
                dv = dv + _dot(p.astype(BF16), do_b, TN)
                dk[hh] = dk[hh] + _dot(ds, q, TN)
                dqa_ref[hh, rows, :] += _dot(ds, ks[hh])
            return dk[0], dk[1], dv

        zero = jnp.zeros((t, LANES), F32)
        carry = block(j, (zero, zero, zero), True)
        dk0, dk1, dv = lax.fori_loop(j + 1, nq, functools.partial(block, masked=False), carry)
        dka_ref[0] = dk0
        dka_ref[1] = dk1
        dv_ref[...] = dv.astype(dv_ref.dtype)
        pl.when((pl.program_id(0) == N_PAIRS - 1) & (j == nq - 1))(finish)

    whole_pair = pl.BlockSpec((2, s_len, LANES), lambda p, j: (p, 0, 0))
    blk_pair = pl.BlockSpec((2, t, LANES), lambda p, j: (p, j, 0))
    whole_cols = pl.BlockSpec((s_len, LANES), lambda p, j: (0, p))
    blk_cols = pl.BlockSpec((t, LANES), lambda p, j: (j, p))
    res = pl.pallas_call(
        body, name="attn_bwd", grid=(N_PAIRS, nq),
        in_specs=[whole_pair, blk_pair, blk_cols, whole_cols, whole_cols, whole_cols] + [ANY] * (n + 1),
        out_specs=[whole_pair, blk_pair, pl.BlockSpec((t, LANES), lambda p, j: (j, AL_V // LANES + p))] + [ANY] * n,
        out_shape=[jax.ShapeDtypeStruct((N_HEADS, s_len, LANES), F32), jax.ShapeDtypeStruct((N_HEADS, s_len, LANES), F32),
                   jax.ShapeDtypeStruct(dproj.shape, dproj.dtype)]
        + [jax.ShapeDtypeStruct((N_DEV, g.shape[1] // 2, g.shape[2]), g.dtype) for g in parts],
        scratch_shapes=_exchange_sems(n), input_output_aliases={6 + n: 2},
        compiler_params=_params(("arbitrary", "arbitrary")),
    )(qa, ka, vb, o, lse, do, *parts, dproj)
    return res[0], res[1], res[2], res[3:]


def _attn_post(dqa, dka, dproj):
    s_len = dqa.shape[1]
    tr = min(256, s_len)
    assert AL_K == AL_Q + D_MODEL and AL_Q % (2 * D_MODEL) == 0

    def body(dqa_ref, dka_ref, _, dqk_ref, dcum_ref):
        lane = lax.broadcasted_iota(jnp.int32, (tr, LANES), 1)
        lo = lane < HEAD_DIM
        dcum = jnp.zeros((tr, LANES), F32)
        for p in range(N_PAIRS):
            a0, a1, b0, b1 = dqa_ref[2 * p], dqa_ref[2 * p + 1], dka_ref[2 * p], dka_ref[2 * p + 1]
            dq = jnp.where(lo, a0, pltpu.roll(a1, HEAD_DIM, 1)) * (HEAD_DIM ** -0.5)
            dqk_ref[:, p * LANES:(p + 1) * LANES] = dq.astype(dqk_ref.dtype)
            dqk_ref[:, D_MODEL + p * LANES:D_MODEL + (p + 1) * LANES] = jnp.where(lo, b0, pltpu.roll(b1, HEAD_DIM, 1)).astype(dqk_ref.dtype)
            for hh, (a, b) in enumerate(((a0, b0), (a1, b1))):
                dcum = dcum + jnp.where(lane == 2 * p + hh, a[:, AUG_C:AUG_C + 1] - b[:, AUG_ONE:AUG_ONE + 1], 0.0)
        dcum_ref[...] = dcum

    heads = pl.BlockSpec((N_HEADS, tr, LANES), lambda i: (0, i, 0))
    return pl.pallas_call(
        body, name="attn_post", grid=(s_len // tr,),
        in_specs=[heads, heads, ANY],
        out_specs=[pl.BlockSpec((tr, 2 * D_MODEL), lambda i: (i, AL_Q // (2 * D_MODEL))), pl.BlockSpec((tr, LANES), lambda i: (i, 0))],
        out_shape=[jax.ShapeDtypeStruct(dproj.shape, dproj.dtype), jax.ShapeDtypeStruct((s_len, LANES), F32)],
        input_output_aliases={2: 0}, compiler_params=_params(("parallel",)),
    )(dqa, dka, dproj)


def _ln_stats(r):
    mu = _rowmean(r)
    xc = r - mu
    rstd = lax.rsqrt(_rowmean(xc * xc) + LN_EPS)
    return xc * rstd, rstd


def _ln_bwd(dxh, xh, rstd):
    return rstd * (dxh - _rowmean(dxh) - xh * _rowmean(dxh * xh))


def _rms_bwd(dgn, g, r):
    return r * dgn - (r * r * r) * g * _rowmean(dgn * g)


def _to_aligned(wt):
    out = jnp.zeros((AL_COLS, wt.shape[1]), wt.dtype)
    for dst, (lo, hi) in ((0, (0, 2048)), (AL_Q, (2576, 5648)), (AL_B, (2048, 2560)), (AL_DTF, (2560, 2576)), (AL_DTF + 16, (5648, 5664))):
        out = lax.dynamic_update_slice_in_dim(out, wt[lo:hi], dst, axis=0)
    return out


def _from_aligned(gt):
    out = jnp.zeros((IN_COLS, gt.shape[1]), gt.dtype)
    for dst, (lo, hi) in ((0, (0, AL_Q)), (2048, (AL_B, AL_DTF)), (2560, (AL_DTF, AL_DTF + 16)), (2576, (AL_Q, AL_B)),
                          (5648, (AL_DTF + 16, AL_DTF + 32))):
        out = lax.dynamic_update_slice_in_dim(out, gt[lo:hi], dst, axis=0)
    return out


def _lanes(v, at=0):
    return jnp.pad(v, ((0, 0), (at, LANES - at - v.shape[1])))


def _local_step(x, tgt, mod, w_alt, halves, sp):
    d = D_MODEL
    sh1, sc1, g1, sh2, sc2, g2 = [mod[:, i * d:(i + 1) * d] for i in range(6)]
    dt_bias_l, a_log_l, f_bias_l = _lanes(sp["dt_bias"]), _lanes(sp["a_log"]), _lanes(sp["f_bias"], F_LANE)
    d_exp = jnp.repeat(sp["d_skip"], HEAD_DIM, axis=1)
    z_slab = lambda a: (a, d, AL_Z // d)

    (h1,), _ = _rowwise("mod1", lambda x, sc, sh: ([x * (1.0 + sc) + sh], []), [x], [sc1, sh1], [(d, BF16)], [])
    proj = _matmul("proj", h1, w_alt, dims=NT, tn=1152)
    xc_all = _conv_fwd(proj, sp["conv_w"], sp["conv_b"])
    y_ssd, prevs = _ssd_fwd(xc_all, proj, dt_bias_l, a_log_l, d_exp)

    def gated_norm(y, z, w):
        g = y * _silu(z)
        return [g * lax.rsqrt(_rowmean(g * g) + RMS_EPS) * w], []

    (y_mix,), _ = _rowwise("ssm_norm", gated_norm, [y_ssd, z_slab(proj)], [sp["ssm_norm_w"]], [(d, BF16, (2 * d, 0, None))], [])
    cum = _fox_cum(proj, f_bias_l)
    qa, ka, vb = _attn_prep(proj, cum)
    o, lse, (g_out, g_fi, g_fo) = _attn_fwd(qa, ka, vb, halves)
    w_out = g_out.reshape(2 * d, d)
    w_fi = g_fi.transpose(1, 0, 2).reshape(d, D_FF)
    w_fo = g_fo.reshape(D_FF, d)
    (y_mix,), _ = _rowwise("attn_norm", lambda o, w: ([o * lax.rsqrt(_rowmean(o * o) + RMS_EPS) * w], []),
                           [o], [sp["attn_norm_w"]], [(d, BF16, (2 * d, 1, y_mix))], [])
    def ln1_fwd(y, x, g1, sc2, sh2, lg, lb):
        r1 = ALPHA * x + (1.0 + g1) * y
        xh, _ = _ln_stats(r1)
        x1 = xh * lg + lb
        return [y, r1, x1 * (1.0 + sc2) + sh2], []

    y, r1, h2 = _matmul("out_proj", y_mix, w_out, tm=512, tk=2048,
                        epi=(ln1_fwd, [x], [g1, sc2, sh2, sp["ln1_g"], sp["ln1_b"]], [F32, F32, BF16], []))
    act = _matmul("ff_in", h2, w_fi, epi=(lambda u: ([jnp.square(jnp.maximum(u, 0.0))], []), [], [], [BF16], []))

    def head(ff, r1, tgt, g2, l1g, l1b, l2g, l2b):
        xh1, _ = _ln_stats(r1)
        x1 = xh1 * l1g + l1b
        xh2, rstd2 = _ln_stats(ALPHA * x1 + (1.0 + g2) * ff)
        err = xh2 * l2g + l2b - tgt
        loss = 0.5 * jnp.sum(_rowmean(err * err))
        dx2 = err * (1.0 / d)
        dr2 = _ln_bwd(dx2 * l2g, xh2, rstd2)
        return ([dr2, (1.0 + g2) * dr2],
                [_colsum(dx2 * xh2), _colsum(dx2), _colsum(dr2 * ff), jnp.full((1, LANES), loss, F32)])

    dr2, dff, d_ln2_g, d_ln2_b, d_g2, loss = _matmul(
        "ff_out", act, w_fo, tm=512, tk=2048,
        epi=(head, [r1, tgt], [g2, sp["ln1_g"], sp["ln1_b"], sp["ln2_g"], sp["ln2_b"]], [F32, BF16], [d, d, d, LANES]))
    du = _matmul("d_act", dff, w_fo, dims=NT, epi=(lambda da, act: ([da * (2.0 * jnp.sqrt(act.astype(F32)))], []), [act], [], [BF16], []))
    dw_fo = _matmul("dw_ff_out", act, dff, dims=TN, out_dtype=BF16, by_chip="rows")
    dw_fi = _matmul("dw_ff_in", h2, du, dims=TN, out_dtype=BF16, by_chip="cols")

    def ln1_bwd(dh2, r1, dr2, y, sc2, g1, lg, lb):
        xh, rstd = _ln_stats(r1)
        x1 = xh * lg + lb
        dx1 = ALPHA * dr2 + dh2 * (1.0 + sc2)
        dr1 = _ln_bwd(dx1 * lg, xh, rstd)
        return ([dr1, (1.0 + g1) * dr1],
                [_colsum(dh2 * x1), _colsum(dh2), _colsum(dx1 * xh), _colsum(dx1), _colsum(dr1 * y)])

    dr1, dy, d_sc2, d_sh2, d_ln1_g, d_ln1_b, d_g1 = _matmul(
        "dh2", du, w_fi, dims=NT, tm=512, tk=2048,
        epi=(ln1_bwd, [r1, dr2, y], [sc2, g1, sp["ln1_g"], sp["ln1_b"]], [F32, BF16], [d] * 5))
    dymix = _matmul("dy_mix", dy, w_out, dims=NT)
    dw_out = _matmul("dw_out", y_mix, dy, dims=TN, out_dtype=BF16, by_chip="rows")

    def attn_norm_bwd(o, dyo, w):
        r = lax.rsqrt(_rowmean(o * o) + RMS_EPS)
        return [_rms_bwd(dyo * w, o, r)], [_colsum(dyo * o * r)]

    (do,), (d_attn_w,) = _rowwise("attn_norm_bwd", attn_norm_bwd, [o, (dymix, d, 1)], [sp["attn_norm_w"]], [(d, F32)], [d])

    def gated_norm_bwd(y, z, dyo, w):
        sg = _sigmoid(z)
        sz = z * sg
        g = y * sz
        r = lax.rsqrt(_rowmean(g * g) + RMS_EPS)
        dg = _rms_bwd(dyo * w, g, r)
        return [dg * sz, dg * y * (sg * (1.0 + z * (1.0 - sg)))], [_colsum(dyo * g * r)]

    (dy_ssd, dproj), (d_ssm_w,) = _rowwise("ssm_norm_bwd", gated_norm_bwd, [y_ssd, z_slab(proj), (dymix, d, 0)],
                                           [sp["ssm_norm_w"]], [(d, F32), (d, BF16, (AL_COLS, AL_Z // d, None))], [d])
    dqa, dka, dproj, landed = _attn_bwd(qa, ka, vb, o, lse, do, [dw_out, dw_fi, dw_fo], dproj)
    dproj, dcum = _attn_post(dqa, dka, dproj)
    dxc, ddt_tile, d_alog_l, d_dexp, d_dtb_l = _ssd_bwd(xc_all, proj, dt_bias_l, a_log_l, d_exp, prevs, dy_ssd)
    dproj, d_fb_l = _fox_cum_bwd(dcum, proj, f_bias_l, ddt_tile, dproj)
    dpre, d_conv_w, d_conv_b = _conv_bwd_pre(proj, sp["conv_w"], sp["conv_b"], dxc)
    dproj = _conv_bwd_in(dpre, sp["conv_w"], dproj)
    dw_alt = _matmul("dw_in", dproj, h1, dims=TN, tm=1152, out_dtype=BF16)
    part_in = _from_aligned(dw_alt).reshape(N_CHIPS, IN_COLS // N_CHIPS, d)

    def last(dh1, x, dr1, sc1):
        return [ALPHA * dr1 + dh1 * (1.0 + sc1)], [_colsum(dh1 * x), _colsum(dh1)]

    chip_in = _pair_sum(part_in, _pair_exchange(part_in), lax.axis_index("c"))
    dx, d_sc1, d_sh1, landed_in = _matmul("dh1", dproj, w_alt, tm=512, tk=1152, carry=[chip_in],
                                          epi=(last, [x, dr1], [sc1], [F32], [d, d]))

    small = {
        "mod": jnp.concatenate([d_sh1, d_sc1, d_g1, d_sh2, d_sc2, d_g2], axis=1),
        "conv_w": d_conv_w, "conv_b": d_conv_b,
        "dt_bias": d_dtb_l[:, :N_HEADS], "a_log": d_alog_l[:, :N_HEADS],
        "d_skip": jnp.sum(d_dexp.reshape(N_HEADS, HEAD_DIM), axis=1)[None, :],
        "ssm_norm_w": d_ssm_w, "f_bias": d_fb_l[:, F_LANE:F_LANE + N_HEADS], "attn_norm_w": d_attn_w,
        "ln1_g": d_ln1_g, "ln1_b": d_ln1_b, "ln2_g": d_ln2_g, "ln2_b": d_ln2_b, "loss": loss,
    }
    return dx, [landed_in, *landed], small


N_DEV = 8
N_CHIPS = 4
ANY = pl.BlockSpec(memory_space=pl.ANY)
VMEM_SPEC = pl.BlockSpec(memory_space=pltpu.VMEM)


def _place():
    x, y, c = lax.axis_index("x"), lax.axis_index("y"), lax.axis_index("c")
    return x, y, c


def _other_chips(x, y):
    return [(1 - x, y, 2 * (1 - x) + y), (x, 1 - y, 2 * x + 1 - y), (1 - x, 1 - y, 2 * (1 - x) + 1 - y)]


def _small_gather(v_ref, out_ref, send_sems, recv_sems, local_sem):
    x, y, c = _place()
    me = 4 * x + 2 * y + c
    mine = pltpu.make_async_copy(v_ref, out_ref.at[me], local_sem)
    mine.start()
    peers = _peers(x, y, c)

    def copy(rel, slot, to):
        return pltpu.make_async_remote_copy(src_ref=v_ref, dst_ref=out_ref.at[slot], send_sem=send_sems.at[rel],
                                            recv_sem=recv_sems.at[rel], device_id=to, device_id_type=MESH)

    sends = [copy(rel, me, peer) for rel, peer in enumerate(peers)]
    for cp in sends:
        cp.start()
    for rel, (px, py, pc) in enumerate(peers):
        copy(rel, 4 * px + 2 * py + pc, (x, y, c)).wait_recv()
    for cp in sends:
        cp.wait_send()
    mine.wait()


SMALL_GATHER_SEMS = [pltpu.SemaphoreType.DMA((N_DEV - 1,)), pltpu.SemaphoreType.DMA((N_DEV - 1,)), pltpu.SemaphoreType.DMA]


def _allgather_small(name, v, with_sum=False):
    def body(v_ref, out_ref, *rest):
        _small_gather(v_ref, out_ref, *rest[-3:])
        if with_sum:
            acc = out_ref[0]
            for dev in range(1, N_DEV):
                acc = acc + out_ref[dev]
            rest[0][...] = acc

    every = jax.ShapeDtypeStruct((N_DEV, *v.shape), v.dtype)
    return pl.pallas_call(
        body, name=name, out_shape=[every, jax.ShapeDtypeStruct(v.shape, v.dtype)] if with_sum else every,
        in_specs=[VMEM_SPEC], out_specs=[VMEM_SPEC] * 2 if with_sum else VMEM_SPEC, scratch_shapes=SMALL_GATHER_SEMS,
    )(v)


def _gather_shards(shard):
    def body(in_ref, out_ref, stage, send_sems, recv_sems, local_sems):
        start, finish = _shard_gather_plan(in_ref, out_ref, stage, send_sems, recv_sems, local_sems)
        start()
        finish()

    return pl.pallas_call(
        body, name="gather_w_in", out_shape=jax.ShapeDtypeStruct((N_CHIPS, *shard.shape), shard.dtype),
        in_specs=[ANY], out_specs=ANY,
        scratch_shapes=[pltpu.VMEM(shard.shape, shard.dtype), pltpu.SemaphoreType.DMA((6,)), pltpu.SemaphoreType.DMA((6,)),
                        pltpu.SemaphoreType.DMA((2,))],
        compiler_params=_params(),
    )(shard)


def _shard_gather_plan(in_ref, out_ref, stage, send_sems, recv_sems, local_sems):
    ch = in_ref.shape[1] // 2
    x, y, c = _place()
    k_me = 2 * x + y
    me, sibling = (x, y, c), (x, y, 1 - c)
    chips = _other_chips(x, y)

    def copy(idx, k, half, to, src=None):
        cols = out_ref.at[k, :, pl.ds(pl.multiple_of(half * ch, ch), ch)]
        return pltpu.make_async_remote_copy(src_ref=cols if src is None else src, dst_ref=cols, send_sem=send_sems.at[idx],
                                            recv_sem=recv_sems.at[idx], device_id=to, device_id_type=MESH)

    mine = in_ref.at[:, pl.ds(pl.multiple_of(c * ch, ch), ch)]
    sends = [copy(j, k_me, c, (cx, cy, c), src=mine) for j, (cx, cy, _) in enumerate(chips)]
    load = pltpu.make_async_copy(in_ref, stage, local_sems.at[0])
    store = pltpu.make_async_copy(stage, out_ref.at[k_me], local_sems.at[1])

    def start():
        for cp in sends:
            cp.start()
        load.start()

    def finish():
        load.wait()
        store.start()
        forwards = []
        for j, (_, _, kj) in enumerate(chips):
            copy(j, kj, c, me).wait_recv()
            forwards.append(copy(3 + j, kj, c, sibling))
            forwards[-1].start()
        for j, (_, _, kj) in enumerate(chips):
            copy(3 + j, kj, 1 - c, me).wait_recv()
        for cp in sends + forwards:
            cp.wait_send()
        store.wait()

    return start, finish


def _peers(x, y, c):
    return [((1 - x) if rel & 4 else x, (1 - y) if rel & 2 else y, (1 - c) if rel & 1 else c) for rel in range(1, N_DEV)]


def _exchange_sems(n):
    return [pltpu.SemaphoreType.DMA((n, N_DEV - 1)), pltpu.SemaphoreType.DMA((n, N_DEV - 1)), pltpu.SemaphoreType.DMA((n,))]


def _gather_plan(ins, outs, send_sems, recv_sems, local_sems):
    x, y, c = _place()
    k_me = 2 * x + y
    peers = [(rel, p) for rel, p in enumerate(_peers(x, y, c)) if (rel + 1) & 6]

    def copy(w, rel, k, half, to, src=None):
        rh = ins[w].shape[0] // 2
        rows = outs[w].at[k, pl.ds(pl.multiple_of(half * rh, rh), rh), :]
        return pltpu.make_async_remote_copy(src_ref=rows if src is None else src, dst_ref=rows, send_sem=send_sems.at[w, rel],
                                            recv_sem=recv_sems.at[w, rel], device_id=to, device_id_type=MESH)

    def mine(w):
        rh = ins[w].shape[0] // 2
        return ins[w].at[pl.ds(pl.multiple_of(c * rh, rh), rh), :]

    n = len(ins)
    local = [pltpu.make_async_copy(ins[w], outs[w].at[k_me], local_sems.at[w]) for w in range(n)]
    sends = [copy(w, rel, k_me, c, peer, src=mine(w)) for w in range(n) for rel, peer in peers]

    def start():
        for cp in local + sends:
            cp.start()

    def finish():
        for w in range(n):
            for rel, (px, py, pc) in peers:
                copy(w, rel, 2 * px + py, pc, (x, y, c)).wait_recv()
        for cp in sends:
            cp.wait_send()
        for cp in local:
            cp.wait()

    return start, finish


def _reduce_plan(ins, outs, send_sems, recv_sems, local_sems):
    x, y, c = _place()
    me = 4 * x + 2 * y + c
    peers = _peers(x, y, c)

    def block(w, k, half):
        rh = ins[w].shape[1] // 2
        return ins[w].at[k, pl.ds(pl.multiple_of(half * rh, rh), rh), :]

    def copy(w, rel, src, slot, to):
        return pltpu.make_async_remote_copy(src_ref=src, dst_ref=outs[w].at[slot], send_sem=send_sems.at[w, rel],
                                            recv_sem=recv_sems.at[w, rel], device_id=to, device_id_type=MESH)

    n = len(ins)
    local = [pltpu.make_async_copy(block(w, 2 * x + y, c), outs[w].at[me], local_sems.at[w]) for w in range(n)]
    sends = [copy(w, rel, block(w, 2 * px + py, pc), me, (px, py, pc)) for w in range(n) for rel, (px, py, pc) in enumerate(peers)]

    def start():
        for cp in local + sends:
            cp.start()

    def finish():
        for w in range(n):
            for rel, (px, py, pc) in enumerate(peers):
                copy(w, rel, block(w, 2 * x + y, c), 4 * px + 2 * py + pc, (x, y, c)).wait_recv()
        for cp in sends:
            cp.wait_send()
        for cp in local:
            cp.wait()

    return start, finish


def _scatter_plan(ins, outs, send_sems, recv_sems, local_sems):
    x, y, c = _place()
    k_me = 2 * x + y
    chips = _other_chips(x, y)

    def copy(w, j, src_k, dst_k, to):
        return pltpu.make_async_remote_copy(src_ref=ins[w].at[src_k], dst_ref=outs[w].at[dst_k], send_sem=send_sems.at[w, j],
                                            recv_sem=recv_sems.at[w, j], device_id=to, device_id_type=MESH)

    n = len(ins)
    local = [pltpu.make_async_copy(ins[w].at[k_me], outs[w].at[k_me], local_sems.at[w]) for w in range(n)]
    sends = [copy(w, j, kj, k_me, (cx, cy, c)) for w in range(n) for j, (cx, cy, kj) in enumerate(chips)]

    def start():
        for cp in local + sends:
            cp.start()

    def finish():
        for w in range(n):
            for j, (_, _, kj) in enumerate(chips):
                copy(w, j, k_me, kj, (x, y, c)).wait_recv()
        for cp in sends:
            cp.wait_send()
        for cp in local:
            cp.wait()

    return start, finish


def _row_tile(r, mult=2 * SUBLANES):
    if r % 256 == 0:
        return 256
    return max([t for t in range(mult, 513, mult) if r % t == 0], default=r)


def _pair_exchange(g):
    _, r, cdim = g.shape
    ch = cdim // 2

    def body(g_ref, got_ref, send_sem, recv_sem):
        x, y, c = _place()
        cp = pltpu.make_async_remote_copy(src_ref=g_ref.at[:, :, pl.ds(pl.multiple_of((1 - c) * ch, ch), ch)], dst_ref=got_ref,
                                          send_sem=send_sem, recv_sem=recv_sem, device_id=(x, y, 1 - c), device_id_type=MESH)
        cp.start()
        cp.wait_recv()
        cp.wait_send()

    return pl.pallas_call(
        body, name="pair_exchange", out_shape=jax.ShapeDtypeStruct((N_CHIPS, r, ch), g.dtype),
        in_specs=[ANY], out_specs=ANY, scratch_shapes=[pltpu.SemaphoreType.DMA, pltpu.SemaphoreType.DMA],
    )(g)


def _pair_sum(g, got, c):
    _, r, cdim = g.shape
    ch = cdim // 2
    tr = _row_tile(r)

    def body(c_ref, g_ref, got_ref, o_ref):
        o_ref[...] = (g_ref[...].astype(F32) + got_ref[...].astype(F32)).astype(o_ref.dtype)

    blk = pl.BlockSpec((1, tr, ch), lambda k, i, c_ref: (k, i, 0))
    return pl.pallas_call(
        body, name="pair_sum",
        grid_spec=pltpu.PrefetchScalarGridSpec(
            num_scalar_prefetch=1, grid=(N_CHIPS, r // tr),
            in_specs=[pl.BlockSpec((1, tr, ch), lambda k, i, c_ref: (k, i, c_ref[0])), blk], out_specs=blk),
        out_shape=jax.ShapeDtypeStruct((N_CHIPS, r, ch), BF16),
        compiler_params=_params(("parallel", "parallel")),
    )(jnp.reshape(c, (1,)).astype(jnp.int32), g, got)


def _sum_blocks(name, parts):
    k, r, cdim = parts.shape
    tr = _row_tile(r)

    def body(p_ref, o_ref):
        acc = p_ref[0].astype(F32)
        for i in range(1, k):
            acc = acc + p_ref[i].astype(F32)
        o_ref[...] = acc

    return pl.pallas_call(
        body, name=name, grid=(r // tr,),
        in_specs=[pl.BlockSpec((k, tr, cdim), lambda i: (0, i, 0))], out_specs=pl.BlockSpec((tr, cdim), lambda i: (i, 0)),
        out_shape=jax.ShapeDtypeStruct((r, cdim), F32), compiler_params=_params(("parallel",)),
    )(parts)


def _pair_swap(halves):
    n = len(halves)

    def body(*refs):
        ins, outs = refs[:n], refs[n:2 * n]
        send_sems, recv_sems = refs[2 * n:]
        x, y, c = _place()
        cps = [pltpu.make_async_remote_copy(src_ref=ins[w], dst_ref=outs[w], send_sem=send_sems.at[w], recv_sem=recv_sems.at[w],
                                            device_id=(x, y, 1 - c), device_id_type=MESH) for w in range(n)]
        for cp in cps:
            cp.start()
        for cp in cps:
            cp.wait_recv()
        for cp in cps:
            cp.wait_send()

    return pl.pallas_call(
        body, name="pair_swap", out_shape=[jax.ShapeDtypeStruct(h.shape, h.dtype) for h in halves],
        in_specs=[ANY] * n, out_specs=[ANY] * n,
        scratch_shapes=[pltpu.SemaphoreType.DMA((n,)), pltpu.SemaphoreType.DMA((n,))],
    )(*halves)


ADA_SHARD = 6 * D_MODEL // N_CHIPS


def _mod_part(c_all, w_shard, b_shard):
    tn = 512

    def body(c_ref, w_ref, b_ref, o_ref):
        o_ref[...] = _dot(_silu(c_ref[...]).astype(BF16), w_ref[...].astype(BF16)) + b_ref[...]

    return pl.pallas_call(
        body, name="mod_part", grid=(ADA_SHARD // tn,),
        in_specs=[pl.BlockSpec((N_DEV, D_MODEL), lambda j: (0, 0)), pl.BlockSpec((D_MODEL, tn), lambda j: (0, j)),
                  pl.BlockSpec((1, tn), lambda j: (0, j))],
        out_specs=pl.BlockSpec((N_DEV, tn), lambda j: (0, j)),
        out_shape=jax.ShapeDtypeStruct((N_DEV, ADA_SHARD), F32), compiler_params=_params(("parallel",)),
    )(c_all, w_shard, b_shard)


def _w_ada_grad(c_all_t, dmod_shard):
    tm = 256

    def body(ct_ref, dm_ref, o_ref):
        act = _silu(ct_ref[...])
        acc = act[:, 0:1] * dm_ref[0:1, :]
        for dev in range(1, N_DEV):
            acc = acc + act[:, dev:dev + 1] * dm_ref[dev:dev + 1, :]
        o_ref[...] = acc

    return pl.pallas_call(
        body, name="w_ada_grad", grid=(D_MODEL // tm,),
        in_specs=[pl.BlockSpec((tm, N_DEV), lambda i: (i, 0)), pl.BlockSpec((N_DEV, ADA_SHARD), lambda i: (0, 0))],
        out_specs=pl.BlockSpec((tm, ADA_SHARD), lambda i: (i, 0)),
        out_shape=jax.ShapeDtypeStruct((D_MODEL, ADA_SHARD), F32), compiler_params=_params(("parallel",)),
    )(c_all_t, dmod_shard)


def _adamw_math(w, g, m, v):
    nm = ADAM_B1 * m + (1.0 - ADAM_B1) * g
    nv = ADAM_B2 * v + (1.0 - ADAM_B2) * jnp.square(g)
    m_hat = nm / (1.0 - ADAM_B1 ** ADAM_STEP)
    v_hat = nv / (1.0 - ADAM_B2 ** ADAM_STEP)
    return -ADAM_LR * (m_hat / (jnp.sqrt(v_hat) + ADAM_EPS) + ADAM_WD * w), nm, nv


def _adamw(name, w, g, m, v):
    _, r, cdim = w.shape
    tr = 256 if r % 256 == 0 else r

    def body(w_ref, g_ref, m_ref, v_ref, go_ref, d_ref, nm_ref, nv_ref):
        go_ref[...] = g_ref[...]
        d_ref[...], nm_ref[...], nv_ref[...] = _adamw_math(w_ref[...], g_ref[...], m_ref[...], v_ref[...])

    blk = pl.BlockSpec((None, tr, cdim), lambda i: (0, i, 0))
    return pl.pallas_call(
        body, name=name, grid=(r // tr,), in_specs=[blk, pl.BlockSpec((tr, cdim), lambda i: (i, 0)), blk, blk], out_specs=[blk] * 4,
        out_shape=[jax.ShapeDtypeStruct((1, r, cdim), F32)] * 4, compiler_params=_params(("parallel",)),
    )(w, g, m, v)


def _adamw_pair(name, w, mine, other, m, v, c, by_cols=False):
    _, r, cdim = w.shape
    hr, hc = mine.shape
    tr = _row_tile(hr, SUBLANES)
    per = hr // tr

    def body(c_ref, w_ref, a_ref, b_ref, m_ref, v_ref, g_ref, d_ref, nm_ref, nv_ref):
        half = pl.program_id(1) if by_cols else pl.program_id(0) // per
        g = jnp.where(half == c_ref[0], a_ref[...], b_ref[...])
        g_ref[...] = g
        d_ref[...], nm_ref[...], nv_ref[...] = _adamw_math(w_ref[...], g, m_ref[...], v_ref[...])

    blk = pl.BlockSpec((None, tr, hc), lambda i, j, c_ref: (0, i, j))
    half = pl.BlockSpec((tr, hc), lambda i, j, c_ref: (i % per, 0))
    return pl.pallas_call(
        body, name=name,
        grid_spec=pltpu.PrefetchScalarGridSpec(num_scalar_prefetch=1, grid=(r // tr, cdim // hc),
                                               in_specs=[blk, half, half, blk, blk], out_specs=[blk] * 4),
        out_shape=[jax.ShapeDtypeStruct((1, r, cdim), F32)] * 4, compiler_params=_params(("parallel", "parallel")),
    )(jnp.reshape(c, (1,)).astype(jnp.int32), w, mine, other, m, v)


SMALL = ["b_ada", "conv_b", "dt_bias", "a_log", "d_skip", "ssm_norm_w", "f_bias", "attn_norm_w", "ln1_g", "ln1_b", "ln2_g", "ln2_b"]


def _pack(vs):
    pieces = []
    for v in vs:
        pieces.append(v)
        if v.shape[1] % LANES:
            pieces.append(jnp.zeros((1, -v.shape[1] % LANES), v.dtype))
    return jnp.concatenate(pieces, axis=1)


def _adamw_small(total, offs, ws, ms, vs):
    n = len(ws)

    def body(*refs):
        t_ref, outs = refs[0], refs[1 + 3 * n:]
        for i in range(n):
            g = t_ref[:, offs[i]:offs[i] + ws[i].shape[1]]
            dl, nm, nv = _adamw_math(refs[1 + i][...], g, refs[1 + n + i][...], refs[1 + 2 * n + i][...])
            outs[4 * i][...], outs[4 * i + 1][...], outs[4 * i + 2][...], outs[4 * i + 3][...] = g, dl, nm, nv

    res = pl.pallas_call(
        body, name="adamw_small", in_specs=[VMEM_SPEC] * (1 + 3 * n), out_specs=[VMEM_SPEC] * (4 * n),
        out_shape=[jax.ShapeDtypeStruct(w.shape, F32) for w in ws for _ in range(4)],
    )(total, *ws, *ms, *vs)
    return [res[4 * i:4 * i + 4] for i in range(n)]


def kernel(x, c, w_ada, b_ada, w_in, conv_w, conv_b, dt_bias, a_log, d_skip, ssm_norm_w, f_bias, attn_norm_w, w_out, ln1_g, ln1_b, w_ff_in, w_ff_out, ln2_g, ln2_b, loss_target, m_w_ada, m_b_ada, m_w_in, m_conv_w, m_conv_b, m_dt_bias, m_a_log, m_d_skip, m_ssm_norm_w, m_f_bias, m_attn_norm_w, m_w_out, m_ln1_g, m_ln1_b, m_w_ff_in, m_w_ff_out, m_ln2_g, m_ln2_b, v_w_ada, v_b_ada, v_w_in, v_conv_w, v_conv_b, v_dt_bias, v_a_log, v_d_skip, v_ssm_norm_w, v_f_bias, v_attn_norm_w, v_w_out, v_ln1_g, v_ln1_b, v_w_ff_in, v_w_ff_out, v_ln2_g, v_ln2_b):
    a = dict(b_ada=b_ada, conv_b=conv_b, dt_bias=dt_bias, a_log=a_log, d_skip=d_skip, ssm_norm_w=ssm_norm_w, f_bias=f_bias,
             attn_norm_w=attn_norm_w, ln1_g=ln1_g, ln1_b=ln1_b, ln2_g=ln2_g, ln2_b=ln2_b)
    ms = dict(b_ada=m_b_ada, conv_b=m_conv_b, dt_bias=m_dt_bias, a_log=m_a_log, d_skip=m_d_skip, ssm_norm_w=m_ssm_norm_w,
              f_bias=m_f_bias, attn_norm_w=m_attn_norm_w, ln1_g=m_ln1_g, ln1_b=m_ln1_b, ln2_g=m_ln2_g, ln2_b=m_ln2_b)
    vs = dict(b_ada=v_b_ada, conv_b=v_conv_b, dt_bias=v_dt_bias, a_log=v_a_log, d_skip=v_d_skip, ssm_norm_w=v_ssm_norm_w,
              f_bias=v_f_bias, attn_norm_w=v_attn_norm_w, ln1_g=v_ln1_g, ln1_b=v_ln1_b, ln2_g=v_ln2_g, ln2_b=v_ln2_b)
    xi, yi, ci = _place()
    chip = 2 * xi + yi
    me = 4 * xi + 2 * yi + ci
    d = D_MODEL
    conv_shard = CONV_DIM // N_CHIPS

    first = _allgather_small("gather_c", jnp.concatenate([c, conv_w[0].reshape(1, CONV_W * conv_shard)], axis=1))[:, 0]
    c_all = first[:, :d]
    conv_w_full = first[::2, d:].reshape(N_CHIPS, CONV_W, conv_shard).transpose(1, 0, 2).reshape(CONV_W, CONV_DIM)
    b_shard = lax.dynamic_slice_in_dim(b_ada, chip * ADA_SHARD, ADA_SHARD, axis=1)
    parts = _allgather_small("gather_mod", _mod_part(c_all, w_ada[0], b_shard))
    mod = lax.dynamic_index_in_dim(parts[::2], me, axis=1, keepdims=False).reshape(1, 6 * d)

    w_in_t, m_w_in_t, v_w_in_t = [jnp.transpose(t, (0, 2, 1)) for t in (w_in, m_w_in, v_w_in)]
    w_alt = _to_aligned(_gather_shards(w_in_t[0].astype(BF16)).reshape(IN_COLS, d))

    sp = {n: a[n] for n in SMALL[1:]}
    sp["conv_w"] = conv_w_full
    shards = [w_out[0].astype(BF16), w_ff_in[0].astype(BF16), w_ff_out[0].astype(BF16)]
    dx, landed, small = _local_step(x[0], loss_target[0], mod, w_alt, shards, sp)

    names = ["mod"] + SMALL[1:]
    vec = _pack([small[n] for n in names] + [small["conv_w"].reshape(1, CONV_W * CONV_DIM), small["loss"]])
    every, total = _allgather_small("gather_small", vec, with_sum=True)
    widths = [6 * d] + [a[n].shape[1] for n in SMALL[1:]]
    offs = [0]
    for w in widths:
        offs.append(offs[-1] + w + (-w % LANES))
    g_conv_w_full = total[:, offs[-1]:offs[-1] + CONV_W * CONV_DIM].reshape(CONV_W, CONV_DIM)
    loss = total[0, offs[-1] + CONV_W * CONV_DIM]
    dmod_shard = lax.dynamic_slice_in_dim(every[:, 0, :6 * d], chip * ADA_SHARD, ADA_SHARD, axis=1)
    g_w_ada = _w_ada_grad(c_all.T, dmod_shard)
    g_conv_w = lax.dynamic_slice_in_dim(g_conv_w_full, chip * conv_shard, conv_shard, axis=1)

    mine = [_sum_blocks("dev_sum_%d" % i, p) for i, p in enumerate(landed)]
    other = _pair_swap(mine)

    grads, deltas, new_m, new_v = {}, {}, {}, {}
    paired = dict(w_in=(w_in_t, m_w_in_t, v_w_in_t), w_out=(w_out, m_w_out, v_w_out), w_ff_in=(w_ff_in, m_w_ff_in, v_w_ff_in),
                  w_ff_out=(w_ff_out, m_w_ff_out, v_w_ff_out))
    for i, (n, (w, m, v)) in enumerate(paired.items()):
        res = _adamw_pair("adamw_" + n, w, mine[i], other[i], m, v, ci, by_cols=n == "w_in")
        grads[n], deltas[n], new_m[n], new_v[n] = [jnp.transpose(t, (0, 2, 1)) for t in res] if n == "w_in" else res
    for n, g, (w, m, v) in (("w_ada", g_w_ada, (w_ada, m_w_ada, v_w_ada)), ("conv_w", g_conv_w, (conv_w, m_conv_w, v_conv_w))):
        grads[n], deltas[n], new_m[n], new_v[n] = _adamw("adamw_" + n, w, g, m, v)
    for n, res in zip(SMALL, _adamw_small(total, offs, [a[n] for n in SMALL], [ms[n] for n in SMALL], [vs[n] for n in SMALL])):
        grads[n], deltas[n], new_m[n], new_v[n] = res

    order = ["w_ada", "b_ada", "w_in", "conv_w", "conv_b", "dt_bias", "a_log", "d_skip", "ssm_norm_w", "f_bias", "attn_norm_w", "w_out",
             "ln1_g", "ln1_b", "w_ff_in", "w_ff_out", "ln2_g", "ln2_b"]
    return (loss, dx[None], *[grads[n] for n in order], *[deltas[n] for n in order], *[new_m[n] for n in order], *[new_v[n] for n in order])
```

```python
import functools

import jax
import jax.numpy as jnp
from jax import lax
from jax.experimental import pallas as pl
from jax.experimental.pallas import tpu as pltpu

F32, BF16 = jnp.float32, jnp.bfloat16

D_MODEL = 1024
N_HEADS = 16
HEAD_DIM = 64
N_PAIRS = N_HEADS // 2
SSM_GROUPS = 2
SSM_STATE = 128
CHUNK = 128
CONV_W = 4
CONV_DIM = 1536
D_FF = 4096
IN_COLS = 5664
ALPHA = 2.0 ** 0.25
LN_EPS = 1e-5
RMS_EPS = 1e-5
LANES = 128
SUBLANES = 8

AL_Z, AL_XS, AL_Q, AL_K, AL_V, AL_B, AL_C, AL_DTF = 0, 1024, 2048, 3072, 4096, 5120, 5376, 5632
AL_COLS = 5760
F_LANE = 16

ADAM_LR, ADAM_B1, ADAM_B2, ADAM_EPS, ADAM_WD, ADAM_STEP = 0.001, 0.9, 0.999, 1e-08, 0.01, 10

VMEM_LIMIT = 56 * 1024 * 1024
MESH = pl.DeviceIdType.MESH


def _params(sem=None):
    return pltpu.CompilerParams(dimension_semantics=sem, vmem_limit_bytes=VMEM_LIMIT)


def _sigmoid(x):
    return 1.0 / (1.0 + jnp.exp(-x))


def _silu(x):
    return x * _sigmoid(x)


def _softplus(x):
    return jnp.maximum(x, 0.0) + jnp.log(1.0 + jnp.exp(-jnp.abs(x)))


def _split3(a):
    hi = a.astype(BF16)
    r = a - hi.astype(F32)
    mid = r.astype(BF16)
    lo = (r - mid.astype(F32)).astype(BF16)
    return hi, mid, lo


def _dot(a, b, dims=((1,), (0,))):
    return lax.dot_general(a, b, (dims, ((), ())), preferred_element_type=F32)


NN, NT, TN = ((1,), (0,)), ((1,), (1,)), ((0,), (0,))


def _dot3(t, a):
    hi, mid, lo = _split3(a)
    return _dot(t, hi) + _dot(t, mid) + _dot(t, lo)


def _matmul(name, a, b, *, dims=NN, out_dtype=F32, tm=1024, tn=1024, tk=1024, by_chip=None, epi=None, carry=()):
    if dims == NN:
        (m, k), n = a.shape, b.shape[1]
    elif dims == NT:
        (m, k), n = a.shape, b.shape[0]
    else:
        (k, m), n = a.shape, b.shape[1]
    if by_chip == "rows":
        tm = min(tm, m // 4)
    if by_chip == "cols":
        tn = min(tn, n // 4)
    tm, tn, tk = min(tm, m), min(tn, n), min(tk, k)
    assert m % tm == 0 and n % tn == 0 and k % tk == 0, (name, m, n, k, tm, tn, tk)
    nk = k // tk
    if by_chip == "rows":
        per = m // 4 // tm
        out_spec = pl.BlockSpec((None, tm, tn), lambda i, j, l: (i // per, i % per, j))
        out_shape = jax.ShapeDtypeStruct((4, m // 4, n), out_dtype)
    elif by_chip == "cols":
        per = n // 4 // tn
        out_spec = pl.BlockSpec((None, tm, tn), lambda i, j, l: (j // per, i, j % per))
        out_shape = jax.ShapeDtypeStruct((4, m, n // 4), out_dtype)
    else:
        out_spec = pl.BlockSpec((tm, tn), lambda i, j, l: (i, j))
        out_shape = jax.ShapeDtypeStruct((m, n), out_dtype)
    a_spec = pl.BlockSpec((tk, tm), lambda i, j, l: (l, i)) if dims == TN else pl.BlockSpec((tm, tk), lambda i, j, l: (i, l))
    b_spec = pl.BlockSpec((tn, tk), lambda i, j, l: (j, l)) if dims == NT else pl.BlockSpec((tk, tn), lambda i, j, l: (l, j))

    tile = pl.BlockSpec((tm, tn), lambda i, j, l: (i, j))
    in_specs, args, out_specs, out_shape = [a_spec, b_spec], [a, b], [out_spec], [out_shape]
    fn, n_tiles, n_sums = None, 1, 0
    if epi is not None:
        fn, fulls, vecs, outs, sums = epi
        assert by_chip is None and (not sums or n == tn), name
        in_specs = in_specs + [tile] * len(fulls) + [pl.BlockSpec((1, tn), lambda i, j, l: (0, j))] * len(vecs)
        args = args + list(fulls) + list(vecs)
        out_specs = [tile] * len(outs) + [pl.BlockSpec((1, w), lambda i, j, l: (0, 0)) for w in sums]
        out_shape = [jax.ShapeDtypeStruct((m, n), dt) for dt in outs] + [jax.ShapeDtypeStruct((1, w), F32) for w in sums]
        n_tiles, n_sums = len(outs), len(sums)
    n_in, n_out, n_c = len(args), len(out_specs), len(carry)
    scratch = [pltpu.VMEM((tm, tn) if nk > 1 else (SUBLANES, LANES), F32)]
    if n_c:
        in_specs, args = in_specs + [ANY] * n_c, args + list(carry)
        out_specs = out_specs + [ANY] * n_c
        out_shape = out_shape + [jax.ShapeDtypeStruct(g.shape, g.dtype) for g in carry]
        scratch = scratch + _exchange_sems(n_c)
    gm, gn = m // tm, n // tn

    def body(*refs):
        a_ref, b_ref = refs[:2]
        ins, outs = refs[2:n_in], refs[n_in + n_c:n_in + n_c + n_out]
        acc_ref = refs[n_in + 2 * n_c + n_out]
        i, j, l = pl.program_id(0), pl.program_id(1), pl.program_id(2)
        if n_c:
            start, wait = _scatter_plan(refs[n_in:n_in + n_c], refs[n_in + n_c + n_out:n_in + 2 * n_c + n_out], *refs[n_in + 2 * n_c + n_out + 1:])
            pl.when((i == 0) & (j == 0) & (l == 0))(start)
        part = _dot(a_ref[...].astype(BF16), b_ref[...].astype(BF16), dims)

        def finish(res):
            if fn is None:
                outs[0][...] = res.astype(outs[0].dtype)
                return
            tiles, colsums = fn(res, *[r[...] for r in ins])
            for r, val in zip(outs[:n_tiles], tiles):
                r[...] = val.astype(r.dtype)
            if n_sums:
                @pl.when(i == 0)
                def _():
                    for r in outs[n_tiles:]:
                        r[...] = jnp.zeros_like(r)
                for r, val in zip(outs[n_tiles:], colsums):
                    r[...] += val

        if nk == 1:
            finish(part)
        else:
            @pl.when(l == 0)
            def _():
                acc_ref[...] = part

            @pl.when((l > 0) & (l < nk - 1))
            def _():
                acc_ref[...] += part

            @pl.when(l == nk - 1)
            def _():
                finish(acc_ref[...] + part)

        if n_c:
            pl.when((i == gm - 1) & (j == gn - 1) & (l == nk - 1))(wait)

    res = pl.pallas_call(
        body, name=name, grid=(gm, gn, nk),
        in_specs=in_specs, out_specs=out_specs, out_shape=out_shape, scratch_shapes=scratch,
        compiler_params=_params(("arbitrary",) * 3 if n_c or n_sums else ("parallel", "parallel", "arbitrary")),
    )(*args)
    return res[0] if len(res) == 1 else res


def _rowwise(name, fn, fulls, vecs, out_fulls, out_vecs, tr=256):
    fulls = [f if isinstance(f, tuple) else (f, f.shape[1], 0) for f in fulls]
    s = fulls[0][0].shape[0]
    tr = min(tr, s)
    out_fulls = [o if len(o) == 3 else (*o, (o[0], 0, None)) for o in out_fulls]
    into = [(k, slab[2]) for k, (_, _, slab) in enumerate(out_fulls) if slab[2] is not None]
    nf, nv, nof, nov = len(fulls), len(vecs), len(out_fulls), len(out_vecs)
    in_specs = [pl.BlockSpec((tr, w), functools.partial(lambda i, cb: (i, cb), cb=cb)) for (_, w, cb) in fulls]
    in_specs += [pl.BlockSpec(v.shape, lambda i: (0, 0)) for v in vecs] + [ANY] * len(into)
    out_shape = [jax.ShapeDtypeStruct((s, slab[0]), dt) for (_, dt, slab) in out_fulls] + [jax.ShapeDtypeStruct((1, w), F32) for w in out_vecs]
    out_specs = [pl.BlockSpec((tr, w), functools.partial(lambda i, cb: (i, cb), cb=slab[1])) for (w, _, slab) in out_fulls]
    out_specs += [pl.BlockSpec((1, w), lambda i: (0, 0)) for w in out_vecs]

    def body(*refs):
        outs = refs[nf + nv + len(into):]
        of, ov = fn(*[r[...] for r in refs[:nf + nv]])
        for r, val in zip(outs[:nof], of):
            r[...] = val.astype(r.dtype)
        if nov:
            @pl.when(pl.program_id(0) == 0)
            def _():
                for r in outs[nof:]:
                    r[...] = jnp.zeros_like(r)
            for r, val in zip(outs[nof:], ov):
                r[...] += val

    res = pl.pallas_call(
        body, name=name, grid=(s // tr,), in_specs=in_specs, out_specs=out_specs, out_shape=out_shape,
        input_output_aliases={nf + nv + pos: k for pos, (k, _) in enumerate(into)},
        compiler_params=_params(("arbitrary",)),
    )(*[f[0] for f in fulls], *vecs, *[buf for _, buf in into])
    return res[:nof], res[nof:]


def _colsum(x):
    return jnp.sum(x, axis=0, keepdims=True)


def _rowmean(x):
    return jnp.mean(x, axis=-1, keepdims=True)


CONV_CB = 512
CONV_TR = 512


def _shift_down(u, halo, j):
    if j == 0:
        return u
    ru = pltpu.roll(u, j, 0)
    row8 = lax.broadcasted_iota(jnp.int32, halo.shape, 0)
    top = jnp.where(row8 < j, pltpu.roll(halo, j, 0), ru[:SUBLANES])
    return jnp.concatenate([top, ru[SUBLANES:]], axis=0)


def _shift_up(d, halo, j):
    if j == 0:
        return d
    tr = d.shape[0]
    rd = pltpu.roll(d, tr - j, 0)
    row8 = lax.broadcasted_iota(jnp.int32, halo.shape, 0)
    bot = jnp.where(row8 >= SUBLANES - j, pltpu.roll(halo, SUBLANES - j, 0), rd[tr - SUBLANES:])
    return jnp.concatenate([rd[:tr - SUBLANES], bot], axis=0)


def _conv_col(cb):
    return jnp.where(cb < 2, AL_XS // CONV_CB + cb, AL_B // CONV_CB)


def _conv_specs(s, tr):
    per8 = tr // SUBLANES
    blk = pl.BlockSpec((tr, CONV_CB), lambda cb, i: (i, _conv_col(cb)))
    prev = pl.BlockSpec((SUBLANES, CONV_CB), lambda cb, i: (jnp.maximum(i * per8 - 1, 0), _conv_col(cb)))
    return blk, prev


def _conv_pre(u, halo, w_ref, b_ref, first):
    halo = jnp.where(first, 0.0, halo)
    acc = b_ref[...] + w_ref[CONV_W - 1:CONV_W, :] * u
    shifted = [u]
    for j in range(1, CONV_W):
        sh = _shift_down(u, halo, j)
        shifted.append(sh)
        acc = acc + w_ref[CONV_W - 1 - j:CONV_W - j, :] * sh
    return acc, shifted


def _conv_fwd(proj, conv_w, conv_b):
    s = proj.shape[0]
    tr = min(CONV_TR, s)
    blk, prev = _conv_specs(s, tr)

    def body(u_ref, h_ref, w_ref, b_ref, o_ref):
        pre, _ = _conv_pre(u_ref[...], h_ref[...], w_ref, b_ref, pl.program_id(1) == 0)
        o_ref[...] = _silu(pre)

    return pl.pallas_call(
        body, name="conv_fwd", grid=(CONV_DIM // CONV_CB, s // tr),
        in_specs=[blk, prev, pl.BlockSpec((CONV_W, CONV_CB), lambda cb, i: (0, cb)), pl.BlockSpec((1, CONV_CB), lambda cb, i: (0, cb))],
        out_specs=pl.BlockSpec((tr, CONV_CB), lambda cb, i: (i, cb)),
        out_shape=jax.ShapeDtypeStruct((s, CONV_DIM), F32),
        compiler_params=_params(("parallel", "parallel")),
    )(proj, proj, conv_w, conv_b)


def _conv_bwd_pre(proj, conv_w, conv_b, dxc):
    s = proj.shape[0]
    tr = min(CONV_TR, s)
    blk, prev = _conv_specs(s, tr)

    def body(u_ref, h_ref, w_ref, b_ref, d_ref, dpre_ref, dw_ref, db_ref):
        i = pl.program_id(1)
        pre, shifted = _conv_pre(u_ref[...], h_ref[...], w_ref, b_ref, i == 0)
        sg = _sigmoid(pre)
        dpre = d_ref[...] * (sg * (1.0 + pre * (1.0 - sg)))
        dpre_ref[...] = dpre

        @pl.when(i == 0)
        def _():
            dw_ref[...] = jnp.zeros_like(dw_ref)
            db_ref[...] = jnp.zeros_like(db_ref)

        db_ref[...] += _colsum(dpre)
        for j in range(CONV_W):
            dw_ref[CONV_W - 1 - j:CONV_W - j, :] += _colsum(dpre * shifted[j])

    own = pl.BlockSpec((tr, CONV_CB), lambda cb, i: (i, cb))
    wspec = pl.BlockSpec((CONV_W, CONV_CB), lambda cb, i: (0, cb))
    bspec = pl.BlockSpec((1, CONV_CB), lambda cb, i: (0, cb))
    return pl.pallas_call(
        body, name="conv_bwd_pre", grid=(CONV_DIM // CONV_CB, s // tr),
        in_specs=[blk, prev, wspec, bspec, own], out_specs=[own, wspec, bspec],
        out_shape=[jax.ShapeDtypeStruct((s, CONV_DIM), F32), jax.ShapeDtypeStruct((CONV_W, CONV_DIM), F32),
                   jax.ShapeDtypeStruct((1, CONV_DIM), F32)],
        compiler_params=_params(("parallel", "arbitrary")),
    )(proj, proj, conv_w, conv_b, dxc)


def _conv_bwd_in(dpre, conv_w, dproj):
    s = dpre.shape[0]
    tr = min(CONV_TR, s)
    per8 = tr // SUBLANES
    last8 = s // SUBLANES - 1
    nb = s // tr

    def body(d_ref, n_ref, w_ref, _, o_ref):
        d = d_ref[...]
        halo = jnp.where(pl.program_id(1) == nb - 1, 0.0, n_ref[...])
        acc = w_ref[CONV_W - 1:CONV_W, :] * d
        for j in range(1, CONV_W):
            acc = acc + w_ref[CONV_W - 1 - j:CONV_W - j, :] * _shift_up(d, halo, j)
        o_ref[...] = acc.astype(o_ref.dtype)

    own = pl.BlockSpec((tr, CONV_CB), lambda cb, i: (i, cb))
    nxt = pl.BlockSpec((SUBLANES, CONV_CB), lambda cb, i: (jnp.minimum((i + 1) * per8, last8), cb))
    return pl.pallas_call(
        body, name="conv_bwd_in", grid=(CONV_DIM // CONV_CB, nb),
        in_specs=[own, nxt, pl.BlockSpec((CONV_W, CONV_CB), lambda cb, i: (0, cb)), ANY],
        out_specs=pl.BlockSpec((tr, CONV_CB), lambda cb, i: (i, _conv_col(cb))),
        out_shape=jax.ShapeDtypeStruct(dproj.shape, dproj.dtype), input_output_aliases={3: 0},
        compiler_params=_params(("parallel", "parallel")),
    )(dpre, dpre, conv_w, dproj)


XC_B, XC_C = 1024, 1280


def _tile_iotas():
    row = lax.broadcasted_iota(jnp.int32, (CHUNK, LANES), 0)
    lane = lax.broadcasted_iota(jnp.int32, (CHUNK, LANES), 1)
    return row, lane


def _ssd_scalars(dtf_ref, bias_ref, alog_ref, row, lane):
    head = lane[:1] < N_HEADS
    raw = dtf_ref[...] + bias_ref[...]
    dt = _softplus(raw)
    a_neg = jnp.where(head, -jnp.exp(alog_ref[...]), 0.0)
    a = dt * a_neg
    tril = (row >= lane).astype(BF16)
    s = _dot3(tril, a)
    return raw, dt, a_neg, s


def _pair(v, j, lo):
    return jnp.where(lo, v[:, 2 * j:2 * j + 1], v[:, 2 * j + 1:2 * j + 2])


def _head_sum(x, lo, hh):
    return jnp.sum(jnp.where(lo == (hh == 0), x, 0.0), axis=1, keepdims=True)


def _decay_masks(s, st, h, row, lane):
    s_col = jnp.broadcast_to(s[:, h:h + 1], (CHUNK, LANES))
    s_row = jnp.broadcast_to(st[h:h + 1, :], (CHUNK, LANES))
    lm = jnp.where(row >= lane, jnp.exp(s_col - s_row), 0.0)
    lmt = jnp.where(row <= lane, jnp.exp(s_row - s_col), 0.0)
    return lm, lmt


def _ssd_fwd(xc_all, proj, dt_bias_l, a_log_l, d_exp):
    s_len = xc_all.shape[0]
    nc = s_len // CHUNK

    def body(x_ref, dtf_ref, bias_ref, alog_ref, dexp_ref, y_ref, prevs_ref, state_ref):
        @pl.when(pl.program_id(0) == 0)
        def _():
            state_ref[...] = jnp.zeros_like(state_ref)

        row, lane = _tile_iotas()
        lo = lane < HEAD_DIM
        _, dt, _, s = _ssd_scalars(dtf_ref, bias_ref, alog_ref, row, lane)
        tot = s[CHUNK - 1:CHUNK, :]
        st = s.T
        for g in range(SSM_GROUPS):
            bg = x_ref[:, XC_B + g * SSM_STATE:XC_B + (g + 1) * SSM_STATE].astype(BF16)
            cg = x_ref[:, XC_C + g * SSM_STATE:XC_C + (g + 1) * SSM_STATE].astype(BF16)
            cb = _dot(cg, bg, NT)
            for j in range(g * 4, g * 4 + 4):
                xs_p = x_ref[:, j * LANES:(j + 1) * LANES]
                dt_p, s_p, tot_p = _pair(dt, j, lo), _pair(s, j, lo), _pair(tot, j, lo[:1])
                xc_p = xs_p * dt_p
                xc_b = xc_p.astype(BF16)
                yd = []
                for hh in range(2):
                    lm, _ = _decay_masks(s, st, 2 * j + hh, row, lane)
                    yd.append(_dot((cb * lm).astype(BF16), xc_b))
                prev = state_ref[j]
                prevs_ref[0, j] = prev
                yo = _dot(cg, prev.astype(BF16)) * jnp.exp(s_p)
                y_ref[:, j * LANES:(j + 1) * LANES] = jnp.where(lo, yd[0], yd[1]) + yo + dexp_ref[:, j * LANES:(j + 1) * LANES] * xs_p
                to_end = jnp.exp(tot_p - s_p)
                state_ref[j] = jnp.exp(tot_p) * prev + _dot(bg, (xc_p * to_end).astype(BF16), TN)

    vec = lambda w: pl.BlockSpec((1, w), lambda c: (0, 0))
    return pl.pallas_call(
        body, name="ssd_fwd", grid=(nc,),
        in_specs=[pl.BlockSpec((CHUNK, CONV_DIM), lambda c: (c, 0)), pl.BlockSpec((CHUNK, LANES), lambda c: (c, AL_DTF // LANES)),
                  vec(LANES), vec(LANES), vec(D_MODEL)],
        out_specs=[pl.BlockSpec((CHUNK, D_MODEL), lambda c: (c, 0)), pl.BlockSpec((1, N_PAIRS, SSM_STATE, LANES), lambda c: (c, 0, 0, 0))],
        out_shape=[jax.ShapeDtypeStruct((s_len, D_MODEL), F32), jax.ShapeDtypeStruct((nc, N_PAIRS, SSM_STATE, LANES), F32)],
        scratch_shapes=[pltpu.VMEM((N_PAIRS, SSM_STATE, LANES), F32)],
        compiler_params=_params(("arbitrary",)),
    )(xc_all, proj, dt_bias_l, a_log_l, d_exp)


def _ssd_bwd(xc_all, proj, dt_bias_l, a_log_l, d_exp, prevs, dy):
    s_len = xc_all.shape[0]
    nc = s_len // CHUNK

    def body(x_ref, dtf_ref, bias_ref, alog_ref, dexp_ref, prevs_ref, dy_ref, dx_ref, ddt_ref, da_ref, dd_ref, dbias_ref, dstate_ref):
        @pl.when(pl.program_id(0) == 0)
        def _():
            dstate_ref[...] = jnp.zeros_like(dstate_ref)
            da_ref[...] = jnp.zeros_like(da_ref)
            dd_ref[...] = jnp.zeros_like(dd_ref)
            dbias_ref[...] = jnp.zeros_like(dbias_ref)

        row, lane = _tile_iotas()
        lo = lane < HEAD_DIM
        last = row == CHUNK - 1
        raw, dt, a_neg, s = _ssd_scalars(dtf_ref, bias_ref, alog_ref, row, lane)
        tot = s[CHUNK - 1:CHUNK, :]
        st = s.T
        ds_acc = jnp.zeros((CHUNK, LANES), F32)
        ddt_acc = jnp.zeros((CHUNK, LANES), F32)
        for g in range(SSM_GROUPS):
            bcol = slice(XC_B + g * SSM_STATE, XC_B + (g + 1) * SSM_STATE)
            ccol = slice(XC_C + g * SSM_STATE, XC_C + (g + 1) * SSM_STATE)
            bg = x_ref[:, bcol].astype(BF16)
            cg = x_ref[:, ccol].astype(BF16)
            cb = _dot(cg, bg, NT)
            cbt = _dot(bg, cg, NT)
            dcb = jnp.zeros((CHUNK, LANES), F32)
            dcbt = jnp.zeros((CHUNK, LANES), F32)
            db_acc = jnp.zeros((CHUNK, LANES), F32)
            dc_acc = jnp.zeros((CHUNK, LANES), F32)
            for j in range(g * 4, g * 4 + 4):
                cols = slice(j * LANES, (j + 1) * LANES)
                xs_p, dy_p = x_ref[:, cols], dy_ref[:, cols]
                dt_p, s_p, tot_p = _pair(dt, j, lo), _pair(s, j, lo), _pair(tot, j, lo[:1])
                xc_p = xs_p * dt_p
                xc_b, dy_b = xc_p.astype(BF16), dy_p.astype(BF16)
                e_p, f_p, etot_p = jnp.exp(s_p), jnp.exp(tot_p - s_p), jnp.exp(tot_p)
                prev, dnext = prevs_ref[0, j], dstate_ref[j]
                prev_b, dnext_b = prev.astype(BF16), dnext.astype(BF16)
                dd_ref[:, cols] += _colsum(dy_p * xs_p)
                dxs_p = dexp_ref[:, cols] * dy_p
                cp = _dot(cg, prev_b)
                gy = (dy_p * e_p).astype(BF16)
                dc_acc += _dot(gy, prev_b, NT)
                dstate_ref[j] = etot_p * dnext + _dot(cg, gy, TN)
                de = dy_p * cp * e_p
                bds = _dot(bg, dnext_b)
                db_acc += _dot((xc_p * f_p).astype(BF16), dnext_b, NT)
                dxc_p = bds * f_p
                df = bds * xc_p * f_p
                dtot_p = _colsum(dnext * prev) * etot_p + _colsum(df)
                dsl = de - df + jnp.where(last, dtot_p, 0.0)
                for hh in range(2):
                    h = 2 * j + hh
                    mine = lo == (hh == 0)
                    lm, lmt = _decay_masks(s, st, h, row, lane)
                    dy_h = jnp.where(mine, dy_p, 0.0).astype(BF16)
                    xc_h = jnp.where(mine, xc_p, 0.0).astype(BF16)
                    dm = _dot(dy_h, xc_b, NT)
                    dmt = _dot(xc_h, dy_b, NT)
                    mt = cbt * lmt
                    dxc_p += _dot(mt.astype(BF16), dy_h)
                    dml, dmtl = dm * lm, dmt * lmt
                    ds_h = jnp.sum(dml * cb - dmtl * cbt + jnp.where(mine, dsl, 0.0), axis=1, keepdims=True)
                    ds_acc += jnp.where(lane == h, ds_h, 0.0)
                    dcb += dml
                    dcbt += dmtl
                    ddt_acc += jnp.where(lane == h, _head_sum(dxc_p * xs_p, lo, hh), 0.0)
                dx_ref[:, cols] = dxs_p + dxc_p * dt_p
            dx_ref[:, ccol] = dc_acc + _dot(dcb.astype(BF16), bg)
            dx_ref[:, bcol] = db_acc + _dot(dcbt.astype(BF16), cg)
        triu = (row <= lane).astype(BF16)
        da = _dot3(triu, ds_acc)
        ddt = ddt_acc + da * a_neg
        da_ref[...] += _colsum(da * dt) * a_neg[:1]
        ddt_raw = jnp.where(lane < N_HEADS, ddt * _sigmoid(raw), 0.0)
        dbias_ref[...] += _colsum(ddt_raw)
        ddt_ref[...] = ddt_raw

    rev = lambda c: nc - 1 - c
    vec = lambda w: pl.BlockSpec((1, w), lambda c: (0, 0))
    return pl.pallas_call(
        body, name="ssd_bwd", grid=(nc,),
        in_specs=[pl.BlockSpec((CHUNK, CONV_DIM), lambda c: (rev(c), 0)), pl.BlockSpec((CHUNK, LANES), lambda c: (rev(c), AL_DTF // LANES)),
                  vec(LANES), vec(LANES), vec(D_MODEL),
                  pl.BlockSpec((1, N_PAIRS, SSM_STATE, LANES), lambda c: (rev(c), 0, 0, 0)),
                  pl.BlockSpec((CHUNK, D_MODEL), lambda c: (rev(c), 0))],
        out_specs=[pl.BlockSpec((CHUNK, CONV_DIM), lambda c: (rev(c), 0)), pl.BlockSpec((CHUNK, LANES), lambda c: (rev(c), 0)),
                   vec(LANES), vec(D_MODEL), vec(LANES)],
        out_shape=[jax.ShapeDtypeStruct((s_len, CONV_DIM), F32), jax.ShapeDtypeStruct((s_len, LANES), F32),
                   jax.ShapeDtypeStruct((1, LANES), F32), jax.ShapeDtypeStruct((1, D_MODEL), F32), jax.ShapeDtypeStruct((1, LANES), F32)],
        scratch_shapes=[pltpu.VMEM((N_PAIRS, SSM_STATE, LANES), F32)],
        compiler_params=_params(("arbitrary",)),
    )(xc_all, proj, dt_bias_l, a_log_l, d_exp, prevs, dy)


AUG_LANES = 6


def _aug_base(hh):
    return HEAD_DIM if hh == 0 else 0


NEG = -1e30
ATT_T = 512


def _fox_cum(proj, f_bias_l):
    s_len = proj.shape[0]
    nc = s_len // CHUNK

    def body(dtf_ref, fb_ref, cum_ref):
        row, lane = _tile_iotas()
        tril = (row >= lane).astype(BF16)
        spread = [(jnp.where(lane == AUG_LANES * row + i, 1.0, 0.0) - jnp.where(lane == AUG_LANES * row + 3 + i, 1.0, 0.0)).astype(BF16)
                  for i in range(3)]

        def step(c, carry):
            rows = pl.ds(pl.multiple_of(c * CHUNK, CHUNK), CHUNK)
            lf = -_softplus(-(dtf_ref[rows, :] + fb_ref[...]))
            lf = jnp.where(lane < N_HEADS, pltpu.roll(lf, LANES - F_LANE, 1), 0.0)
            cs = _dot3(tril, lf) + carry
            parts = _split3(cs)
            cum_ref[rows, :] = _dot(parts[0], spread[0]) + _dot(parts[1], spread[1]) + _dot(parts[2], spread[2])
            return cs[CHUNK - 1:CHUNK, :]

        lax.fori_loop(0, nc, step, jnp.zeros((1, LANES), F32))

    return pl.pallas_call(
        body, name="fox_cum", grid=(1,),
        in_specs=[pl.BlockSpec((s_len, LANES), lambda i: (0, AL_DTF // LANES)), pl.BlockSpec((1, LANES), lambda i: (0, 0))],
        out_specs=pl.BlockSpec((s_len, LANES), lambda i: (0, 0)),
        out_shape=jax.ShapeDtypeStruct((s_len, LANES), F32),
        compiler_params=_params(("arbitrary",)),
    )(proj, f_bias_l)


def _fox_cum_bwd(dcum, proj, f_bias_l, ddt_tile, dproj):
    s_len = proj.shape[0]
    nc = s_len // CHUNK

    def body(dcum_ref, dtf_ref, fb_ref, ddt_ref, _, out_ref, dfb_ref):
        row, lane = _tile_iotas()
        triu = (row <= lane).astype(BF16)
        is_f = (lane >= F_LANE) & (lane < F_LANE + N_HEADS)

        def step(t, carry):
            run, dfb = carry
            rows = pl.ds(pl.multiple_of((nc - 1 - t) * CHUNK, CHUNK), CHUNK)
            rc = _dot3(triu, dcum_ref[rows, :]) + run
            sg = _sigmoid(-(dtf_ref[rows, :] + fb_ref[...]))
            df = jnp.where(is_f, pltpu.roll(rc, F_LANE, 1) * sg, 0.0)
            out_ref[rows, :] = (df + ddt_ref[rows, :]).astype(out_ref.dtype)
            return rc[0:1, :], dfb + _colsum(df)

        _, dfb = lax.fori_loop(0, nc, step, (jnp.zeros((1, LANES), F32), jnp.zeros((1, LANES), F32)))
        dfb_ref[...] = dfb

    whole = pl.BlockSpec((s_len, LANES), lambda i: (0, 0))
    dtf_cols = pl.BlockSpec((s_len, LANES), lambda i: (0, AL_DTF // LANES))
    vec = pl.BlockSpec((1, LANES), lambda i: (0, 0))
    return pl.pallas_call(
        body, name="fox_cum_bwd", grid=(1,),
        in_specs=[whole, dtf_cols, vec, whole, ANY], out_specs=[dtf_cols, vec],
        out_shape=[jax.ShapeDtypeStruct(dproj.shape, dproj.dtype), jax.ShapeDtypeStruct((1, LANES), F32)],
        input_output_aliases={4: 0}, compiler_params=_params(("arbitrary",)),
    )(dcum, proj, f_bias_l, ddt_tile, dproj)


def _attn_prep(proj, cum):
    s_len = proj.shape[0]
    tr = min(512, s_len)

    def body(q_ref, k_ref, v_ref, cum_ref, qa_ref, ka_ref, vb_ref):
        p = pl.program_id(0)
        lane = lax.broadcasted_iota(jnp.int32, (tr, LANES), 1)
        lo = lane < HEAD_DIM
        c = cum_ref[...]
        q, k = q_ref[...] * (HEAD_DIM ** -0.5), k_ref[...]
        for hh in range(2):
            base = _aug_base(hh)
            r = pltpu.roll(c, (base - AUG_LANES * (2 * p + hh)) % LANES, 1)
            first = (lane >= base) & (lane < base + 3)
            second = (lane >= base + 3) & (lane < base + AUG_LANES)
            mine = lo == (hh == 0)
            qa_ref[hh] = jnp.where(mine, q, jnp.where(first, r, jnp.where(second, 1.0, 0.0))).astype(BF16)
            ka_ref[hh] = jnp.where(mine, k, jnp.where(first, 1.0, jnp.where(second, r, 0.0))).astype(BF16)
        vb_ref[...] = v_ref[...].astype(BF16)

    slab = lambda col0: pl.BlockSpec((tr, LANES), lambda p, i: (i, col0 // LANES + p))
    heads = pl.BlockSpec((2, tr, LANES), lambda p, i: (p, i, 0))
    return pl.pallas_call(
        body, name="attn_prep", grid=(N_PAIRS, s_len // tr),
        in_specs=[slab(AL_Q), slab(AL_K), slab(AL_V), pl.BlockSpec((tr, LANES), lambda p, i: (i, 0))],
        out_specs=[heads, heads, pl.BlockSpec((tr, LANES), lambda p, i: (i, p))],
        out_shape=[jax.ShapeDtypeStruct((N_HEADS, s_len, LANES), BF16), jax.ShapeDtypeStruct((N_HEADS, s_len, LANES), BF16),
                   jax.ShapeDtypeStruct((s_len, D_MODEL), BF16)],
        compiler_params=_params(("parallel", "parallel")),
    )(proj, proj, proj, cum)


def _attn_fwd(qa, ka, vb, halves):
    s_len = vb.shape[0]
    t = min(ATT_T, s_len)
    nq = s_len // t
    n = len(halves)

    def body(qa_ref, ka_ref, vb_ref, *rest):
        o_ref, lse_ref = rest[n:n + 2]
        start, finish = _gather_plan(rest[:n], rest[n + 2:2 * n + 2], *rest[2 * n + 2:])
        i = pl.program_id(1)
        pl.when((pl.program_id(0) == 0) & (i == 0))(start)
        row = lax.broadcasted_iota(jnp.int32, (t, t), 0)
        col = lax.broadcasted_iota(jnp.int32, (t, t), 1)
        lo = lax.broadcasted_iota(jnp.int32, (t, LANES), 1) < HEAD_DIM
        qs = (qa_ref[0], qa_ref[1])

        def block(j, carry, masked):
            rows = pl.ds(pl.multiple_of(j * t, t), t)
            v = vb_ref[rows, :]
            new = []
            for hh in range(2):
                m, l, acc = carry[hh]
                s = _dot(qs[hh], ka_ref[hh, rows, :], NT)
                if masked:
                    s = jnp.where(row >= col, s, NEG)
                m_new = jnp.maximum(m, jnp.max(s, axis=1, keepdims=True))
                alpha = jnp.exp(m - m_new)
                p = jnp.exp(s - m_new)
                new.append((m_new, alpha * l + jnp.sum(p, axis=1, keepdims=True), alpha * acc + _dot(p.astype(BF16), v)))
            return tuple(new)

        init = (jnp.full((t, 1), NEG, F32), jnp.zeros((t, 1), F32), jnp.zeros((t, LANES), F32))
        carry = lax.fori_loop(0, i, functools.partial(block, masked=False), (init, init))
        (m0, l0, acc0), (m1, l1, acc1) = block(i, carry, True)
        o_ref[...] = jnp.where(lo, acc0 / l0, acc1 / l1)
        lse_ref[...] = jnp.where(lo, m0 + jnp.log(l0), m1 + jnp.log(l1))
        pl.when((pl.program_id(0) == N_PAIRS - 1) & (i == nq - 1))(finish)

    out = pl.BlockSpec((t, LANES), lambda p, i: (i, p))
    res = pl.pallas_call(
        body, name="attn_fwd", grid=(N_PAIRS, nq),
        in_specs=[pl.BlockSpec((2, t, LANES), lambda p, i: (p, i, 0)), pl.BlockSpec((2, s_len, LANES), lambda p, i: (p, 0, 0)),
                  pl.BlockSpec((s_len, LANES), lambda p, i: (0, p))] + [ANY] * n,
        out_specs=[out, out] + [ANY] * n,
        out_shape=[jax.ShapeDtypeStruct((s_len, D_MODEL), F32), jax.ShapeDtypeStruct((s_len, D_MODEL), F32)]
        + [jax.ShapeDtypeStruct((N_CHIPS, *h.shape), h.dtype) for h in halves],
        scratch_shapes=_exchange_sems(n),
        compiler_params=_params(("arbitrary", "arbitrary")),
    )(qa, ka, vb, *halves)
    return res[0], res[1], res[2:]


def _attn_bwd(qa, ka, vb, o, lse, do, parts, dproj):
    s_len = vb.shape[0]
    t = min(ATT_T, s_len)
    nq = s_len // t
    n = len(parts)

    def body(qa_ref, ka_ref, vb_ref, o_ref, lse_ref, do_ref, *rest):
        dqa_ref, dka_ref, dv_ref = rest[n + 1:n + 4]
        start, finish = _reduce_plan(rest[:n], rest[n + 4:2 * n + 4], *rest[2 * n + 4:])
        j = pl.program_id(1)
        pl.when((pl.program_id(0) == 0) & (j == 0))(start)

        @pl.when(j == 0)
        def _():
            dqa_ref[...] = jnp.zeros_like(dqa_ref)

        row = lax.broadcasted_iota(jnp.int32, (t, t), 0)
        col = lax.broadcasted_iota(jnp.int32, (t, t), 1)
        lo = lax.broadcasted_iota(jnp.int32, (t, LANES), 1) < HEAD_DIM
        v = vb_ref[...]
        ks = (ka_ref[0], ka_ref[1])

        def block(i, carry, masked):
            dk, dv = list(carry[:2]), carry[2]
            rows = pl.ds(pl.multiple_of(i * t, t), t)
            do_p, o_p, lse_p = do_ref[rows, :], o_ref[rows, :], lse_ref[rows, :]
            for hh in range(2):
                q = qa_ref[hh, rows, :]
                do_h = jnp.where(lo == (hh == 0), do_p, 0.0)
                delta = jnp.sum(do_h * o_p, axis=1, keepdims=True)
                s = _dot(q, ks[hh], NT)
                if masked:
                    s = jnp.where(row >= col, s, NEG)
                p = jnp.exp(s - lse_p[:, hh * HEAD_DIM:hh * HEAD_DIM + 1])
                do_b = do_h.astype(BF16)
                ds = (p * (_dot(do_b, v, NT) - delta)).astype(BF16)
                dv = dv + _dot(p.astype(BF16), do_b, TN)
                dk[hh] = dk[hh] + _dot(ds, q, TN)
                dqa_ref[hh, rows, :] += _dot(ds, ks[hh])
            return dk[0], dk[1], dv

        zero = jnp.zeros((t, LANES), F32)
        carry = block(j, (zero, zero, zero), True)
        dk0, dk1, dv = lax.fori_loop(j + 1, nq, functools.partial(block, masked=False), carry)
        dka_ref[0] = dk0
        dka_ref[1] = dk1
        dv_ref[...] = dv.astype(dv_ref.dtype)
        pl.when((pl.program_id(0) == N_PAIRS - 1) & (j == nq - 1))(finish)

    whole_pair = pl.BlockSpec((2, s_len, LANES), lambda p, j: (p, 0, 0))
    blk_pair = pl.BlockSpec((2, t, LANES), lambda p, j: (p, j, 0))
    whole_cols = pl.BlockSpec((s_len, LANES), lambda p, j: (0, p))
    blk_cols = pl.BlockSpec((t, LANES), lambda p, j: (j, p))
    res = pl.pallas_call(
        body, name="attn_bwd", grid=(N_PAIRS, nq),
        in_specs=[whole_pair, blk_pair, blk_cols, whole_cols, whole_cols, whole_cols] + [ANY] * (n + 1),
        out_specs=[whole_pair, blk_pair, pl.BlockSpec((t, LANES), lambda p, j: (j, AL_V // LANES + p))] + [ANY] * n,
        out_shape=[jax.ShapeDtypeStruct((N_HEADS, s_len, LANES), F32), jax.ShapeDtypeStruct((N_HEADS, s_len, LANES), F32),
                   jax.ShapeDtypeStruct(dproj.shape, dproj.dtype)]
        + [jax.ShapeDtypeStruct((N_DEV, g.shape[1] // 2, g.shape[2]), g.dtype) for g in parts],
        scratch_shapes=_exchange_sems(n), input_output_aliases={6 + n: 2},
        compiler_params=_params(("arbitrary", "arbitrary")),
    )(qa, ka, vb, o, lse, do, *parts, dproj)
    return res[0], res[1], res[2], res[3:]


def _attn_post(dqa, dka, dproj):
    s_len = dqa.shape[1]
    tr = min(256, s_len)
    assert AL_K == AL_Q + D_MODEL and AL_Q % (2 * D_MODEL) == 0

    def body(dqa_ref, dka_ref, _, dqk_ref, dcum_ref):
        lane = lax.broadcasted_iota(jnp.int32, (tr, LANES), 1)
        lo = lane < HEAD_DIM
        dcum = jnp.zeros((tr, LANES), F32)
        for p in range(N_PAIRS):
            a0, a1, b0, b1 = dqa_ref[2 * p], dqa_ref[2 * p + 1], dka_ref[2 * p], dka_ref[2 * p + 1]
            dq = jnp.where(lo, a0, a1) * (HEAD_DIM ** -0.5)
            dqk_ref[:, p * LANES:(p + 1) * LANES] = dq.astype(dqk_ref.dtype)
            dqk_ref[:, D_MODEL + p * LANES:D_MODEL + (p + 1) * LANES] = jnp.where(lo, b0, b1).astype(dqk_ref.dtype)
            for hh, (a, b) in enumerate(((a0, b0), (a1, b1))):
                base = _aug_base(hh)
                dcum = dcum + jnp.where(lane == 2 * p + hh, a[:, base:base + 1] - b[:, base + 3:base + 4], 0.0)
        dcum_ref[...] = dcum

    heads = pl.BlockSpec((N_HEADS, tr, LANES), lambda i: (0, i, 0))
    return pl.pallas_call(
        body, name="attn_post", grid=(s_len // tr,),
        in_specs=[heads, heads, ANY],
        out_specs=[pl.BlockSpec((tr, 2 * D_MODEL), lambda i: (i, AL_Q // (2 * D_MODEL))), pl.BlockSpec((tr, LANES), lambda i: (i, 0))],
        out_shape=[jax.ShapeDtypeStruct(dproj.shape, dproj.dtype), jax.ShapeDtypeStruct((s_len, LANES), F32)],
        input_output_aliases={2: 0}, compiler_params=_params(("parallel",)),
    )(dqa, dka, dproj)


def _ln_stats(r):
    mu = _rowmean(r)
    xc = r - mu
    rstd = lax.rsqrt(_rowmean(xc * xc) + LN_EPS)
    return xc * rstd, rstd


def _ln_bwd(dxh, xh, rstd):
    return rstd * (dxh - _rowmean(dxh) - xh * _rowmean(dxh * xh))


def _rms_bwd(dgn, g, r):
    return r * dgn - (r * r * r) * g * _rowmean(dgn * g)


def _to_aligned(wt):
    out = jnp.zeros((AL_COLS, wt.shape[1]), wt.dtype)
    for dst, (lo, hi) in ((0, (0, 2048)), (AL_Q, (2576, 5648)), (AL_B, (2048, 2560)), (AL_DTF, (2560, 2576)), (AL_DTF + 16, (5648, 5664))):
        out = lax.dynamic_update_slice_in_dim(out, wt[lo:hi], dst, axis=0)
    return out


def _from_aligned(gt):
    out = jnp.zeros((IN_COLS, gt.shape[1]), gt.dtype)
    for dst, (lo, hi) in ((0, (0, AL_Q)), (2048, (AL_B, AL_DTF)), (2560, (AL_DTF, AL_DTF + 16)), (2576, (AL_Q, AL_B)),
                          (5648, (AL_DTF + 16, AL_DTF + 32))):
        out = lax.dynamic_update_slice_in_dim(out, gt[lo:hi], dst, axis=0)
    return out


def _lanes(v, at=0):
    return jnp.pad(v, ((0, 0), (at, LANES - at - v.shape[1])))


def _local_step(x, tgt, mod, w_alt, halves, sp):
    d = D_MODEL
    sh1, sc1, g1, sh2, sc2, g2 = [mod[:, i * d:(i + 1) * d] for i in range(6)]
    dt_bias_l, a_log_l, f_bias_l = _lanes(sp["dt_bias"]), _lanes(sp["a_log"]), _lanes(sp["f_bias"], F_LANE)
    d_exp = jnp.repeat(sp["d_skip"], HEAD_DIM, axis=1)
    z_slab = lambda a: (a, d, AL_Z // d)

    (h1,), _ = _rowwise("mod1", lambda x, sc, sh: ([x * (1.0 + sc) + sh], []), [x], [sc1, sh1], [(d, BF16)], [])
    proj = _matmul("proj", h1, w_alt, dims=NT, tn=1152)
    xc_all = _conv_fwd(proj, sp["conv_w"], sp["conv_b"])
    y_ssd, prevs = _ssd_fwd(xc_all, proj, dt_bias_l, a_log_l, d_exp)

    def gated_norm(y, z, w):
        g = y * _silu(z)
        return [g * lax.rsqrt(_rowmean(g * g) + RMS_EPS) * w], []

    (y_mix,), _ = _rowwise("ssm_norm", gated_norm, [y_ssd, z_slab(proj)], [sp["ssm_norm_w"]], [(d, BF16, (2 * d, 0, None))], [])
    cum = _fox_cum(proj, f_bias_l)
    qa, ka, vb = _attn_prep(proj, cum)
    o, lse, (g_out, g_fi, g_fo) = _attn_fwd(qa, ka, vb, halves)
    w_out = g_out.reshape(2 * d, d)
    w_fi = g_fi.transpose(1, 0, 2).reshape(d, D_FF)
    w_fo = g_fo.reshape(D_FF, d)
    (y_mix,), _ = _rowwise("attn_norm", lambda o, w: ([o * lax.rsqrt(_rowmean(o * o) + RMS_EPS) * w], []),
                           [o], [sp["attn_norm_w"]], [(d, BF16, (2 * d, 1, y_mix))], [])
    def ln1_fwd(y, x, g1, sc2, sh2, lg, lb):
        r1 = ALPHA * x + (1.0 + g1) * y
        xh, _ = _ln_stats(r1)
        x1 = xh * lg + lb
        return [y, r1, x1 * (1.0 + sc2) + sh2], []

    y, r1, h2 = _matmul("out_proj", y_mix, w_out, tm=512, tk=2048,
                        epi=(ln1_fwd, [x], [g1, sc2, sh2, sp["ln1_g"], sp["ln1_b"]], [F32, F32, BF16], []))
    act = _matmul("ff_in", h2, w_fi, epi=(lambda u: ([jnp.square(jnp.maximum(u, 0.0))], []), [], [], [BF16], []))

    def head(ff, r1, tgt, g2, l1g, l1b, l2g, l2b):
        xh1, _ = _ln_stats(r1)
        x1 = xh1 * l1g + l1b
        xh2, rstd2 = _ln_stats(ALPHA * x1 + (1.0 + g2) * ff)
        err = xh2 * l2g + l2b - tgt
        loss = 0.5 * jnp.sum(_rowmean(err * err))
        dx2 = err * (1.0 / d)
        dr2 = _ln_bwd(dx2 * l2g, xh2, rstd2)
        return ([dr2, (1.0 + g2) * dr2],
                [_colsum(dx2 * xh2), _colsum(dx2), _colsum(dr2 * ff), jnp.full((1, LANES), loss, F32)])

    dr2, dff, d_ln2_g, d_ln2_b, d_g2, loss = _matmul(
        "ff_out", act, w_fo, tm=512, tk=2048,
        epi=(head, [r1, tgt], [g2, sp["ln1_g"], sp["ln1_b"], sp["ln2_g"], sp["ln2_b"]], [F32, BF16], [d, d, d, LANES]))
    du = _matmul("d_act", dff, w_fo, dims=NT, epi=(lambda da, act: ([da * (2.0 * jnp.sqrt(act.astype(F32)))], []), [act], [], [BF16], []))
    dw_fo = _matmul("dw_ff_out", act, dff, dims=TN, out_dtype=BF16, by_chip="rows")
    dw_fi = _matmul("dw_ff_in", h2, du, dims=TN, out_dtype=BF16, by_chip="cols")

    def ln1_bwd(dh2, r1, dr2, y, sc2, g1, lg, lb):
        xh, rstd = _ln_stats(r1)
        x1 = xh * lg + lb
        dx1 = ALPHA * dr2 + dh2 * (1.0 + sc2)
        dr1 = _ln_bwd(dx1 * lg, xh, rstd)
        return ([dr1, (1.0 + g1) * dr1],
                [_colsum(dh2 * x1), _colsum(dh2), _colsum(dx1 * xh), _colsum(dx1), _colsum(dr1 * y)])

    dr1, dy, d_sc2, d_sh2, d_ln1_g, d_ln1_b, d_g1 = _matmul(
        "dh2", du, w_fi, dims=NT, tm=512, tk=2048,
        epi=(ln1_bwd, [r1, dr2, y], [sc2, g1, sp["ln1_g"], sp["ln1_b"]], [F32, BF16], [d] * 5))
    dymix = _matmul("dy_mix", dy, w_out, dims=NT)
    dw_out = _matmul("dw_out", y_mix, dy, dims=TN, out_dtype=BF16, by_chip="rows")

    def attn_norm_bwd(o, dyo, w):
        r = lax.rsqrt(_rowmean(o * o) + RMS_EPS)
        return [_rms_bwd(dyo * w, o, r)], [_colsum(dyo * o * r)]

    (do,), (d_attn_w,) = _rowwise("attn_norm_bwd", attn_norm_bwd, [o, (dymix, d, 1)], [sp["attn_norm_w"]], [(d, F32)], [d])

    def gated_norm_bwd(y, z, dyo, w):
        sg = _sigmoid(z)
        sz = z * sg
        g = y * sz
        r = lax.rsqrt(_rowmean(g * g) + RMS_EPS)
        dg = _rms_bwd(dyo * w, g, r)
        return [dg * sz, dg * y * (sg * (1.0 + z * (1.0 - sg)))], [_colsum(dyo * g * r)]

    (dy_ssd, dproj), (d_ssm_w,) = _rowwise("ssm_norm_bwd", gated_norm_bwd, [y_ssd, z_slab(proj), (dymix, d, 0)],
                                           [sp["ssm_norm_w"]], [(d, F32), (d, BF16, (AL_COLS, AL_Z // d, None))], [d])
    dqa, dka, dproj, landed = _attn_bwd(qa, ka, vb, o, lse, do, [dw_out, dw_fi, dw_fo], dproj)
    dproj, dcum = _attn_post(dqa, dka, dproj)
    dxc, ddt_tile, d_alog_l, d_dexp, d_dtb_l = _ssd_bwd(xc_all, proj, dt_bias_l, a_log_l, d_exp, prevs, dy_ssd)
    dproj, d_fb_l = _fox_cum_bwd(dcum, proj, f_bias_l, ddt_tile, dproj)
    dpre, d_conv_w, d_conv_b = _conv_bwd_pre(proj, sp["conv_w"], sp["conv_b"], dxc)
    dproj = _conv_bwd_in(dpre, sp["conv_w"], dproj)
    dw_alt = _matmul("dw_in", dproj, h1, dims=TN, tm=1152, out_dtype=BF16)
    part_in = _from_aligned(dw_alt).reshape(N_CHIPS, IN_COLS // N_CHIPS, d)

    def last(dh1, x, dr1, sc1):
        return [ALPHA * dr1 + dh1 * (1.0 + sc1)], [_colsum(dh1 * x), _colsum(dh1)]

    chip_in = _pair_sum(part_in, _pair_exchange(part_in), lax.axis_index("c"))
    dx, d_sc1, d_sh1, landed_in = _matmul("dh1", dproj, w_alt, tm=512, tk=1152, carry=[chip_in],
                                          epi=(last, [x, dr1], [sc1], [F32], [d, d]))

    small = {
        "mod": jnp.concatenate([d_sh1, d_sc1, d_g1, d_sh2, d_sc2, d_g2], axis=1),
        "conv_w": d_conv_w, "conv_b": d_conv_b,
        "dt_bias": d_dtb_l[:, :N_HEADS], "a_log": d_alog_l[:, :N_HEADS],
        "d_skip": jnp.sum(d_dexp.reshape(N_HEADS, HEAD_DIM), axis=1)[None, :],
        "ssm_norm_w": d_ssm_w, "f_bias": d_fb_l[:, F_LANE:F_LANE + N_HEADS], "attn_norm_w": d_attn_w,
        "ln1_g": d_ln1_g, "ln1_b": d_ln1_b, "ln2_g": d_ln2_g, "ln2_b": d_ln2_b, "loss": loss,
    }
    return dx, [landed_in, *landed], small


N_DEV = 8
N_CHIPS = 4
ANY = pl.BlockSpec(memory_space=pl.ANY)
VMEM_SPEC = pl.BlockSpec(memory_space=pltpu.VMEM)


def _place():
    x, y, c = lax.axis_index("x"), lax.axis_index("y"), lax.axis_index("c")
    return x, y, c


def _other_chips(x, y):
    return [(1 - x, y, 2 * (1 - x) + y), (x, 1 - y, 2 * x + 1 - y), (1 - x, 1 - y, 2 * (1 - x) + 1 - y)]


def _small_gather(v_ref, out_ref, send_sems, recv_sems, local_sem):
    x, y, c = _place()
    me = 4 * x + 2 * y + c
    mine = pltpu.make_async_copy(v_ref, out_ref.at[me], local_sem)
    mine.start()
    peers = _peers(x, y, c)

    def copy(rel, slot, to):
        return pltpu.make_async_remote_copy(src_ref=v_ref, dst_ref=out_ref.at[slot], send_sem=send_sems.at[rel],
                                            recv_sem=recv_sems.at[rel], device_id=to, device_id_type=MESH)

    sends = [copy(rel, me, peer) for rel, peer in enumerate(peers)]
    for cp in sends:
        cp.start()
    for rel, (px, py, pc) in enumerate(peers):
        copy(rel, 4 * px + 2 * py + pc, (x, y, c)).wait_recv()
    for cp in sends:
        cp.wait_send()
    mine.wait()


SMALL_GATHER_SEMS = [pltpu.SemaphoreType.DMA((N_DEV - 1,)), pltpu.SemaphoreType.DMA((N_DEV - 1,)), pltpu.SemaphoreType.DMA]


def _allgather_small(name, v, with_sum=False):
    def body(v_ref, out_ref, *rest):
        _small_gather(v_ref, out_ref, *rest[-3:])
        if with_sum:
            acc = out_ref[0]
            for dev in range(1, N_DEV):
                acc = acc + out_ref[dev]
            rest[0][...] = acc

    every = jax.ShapeDtypeStruct((N_DEV, *v.shape), v.dtype)
    return pl.pallas_call(
        body, name=name, out_shape=[every, jax.ShapeDtypeStruct(v.shape, v.dtype)] if with_sum else every,
        in_specs=[VMEM_SPEC], out_specs=[VMEM_SPEC] * 2 if with_sum else VMEM_SPEC, scratch_shapes=SMALL_GATHER_SEMS,
    )(v)


def _gather_shards(shard):
    def body(in_ref, out_ref, stage, send_sems, recv_sems, local_sems):
        start, finish = _shard_gather_plan(in_ref, out_ref, stage, send_sems, recv_sems, local_sems)
        start()
        finish()

    return pl.pallas_call(
        body, name="gather_w_in", out_shape=jax.ShapeDtypeStruct((N_CHIPS, *shard.shape), shard.dtype),
        in_specs=[ANY], out_specs=ANY,
        scratch_shapes=[pltpu.VMEM(shard.shape, shard.dtype), pltpu.SemaphoreType.DMA((6,)), pltpu.SemaphoreType.DMA((6,)),
                        pltpu.SemaphoreType.DMA((2,))],
        compiler_params=_params(),
    )(shard)


def _shard_gather_plan(in_ref, out_ref, stage, send_sems, recv_sems, local_sems):
    ch = in_ref.shape[1] // 2
    x, y, c = _place()
    k_me = 2 * x + y
    me, sibling = (x, y, c), (x, y, 1 - c)
    chips = _other_chips(x, y)

    def copy(idx, k, half, to, src=None):
        cols = out_ref.at[k, :, pl.ds(pl.multiple_of(half * ch, ch), ch)]
        return pltpu.make_async_remote_copy(src_ref=cols if src is None else src, dst_ref=cols, send_sem=send_sems.at[idx],
                                            recv_sem=recv_sems.at[idx], device_id=to, device_id_type=MESH)

    mine = in_ref.at[:, pl.ds(pl.multiple_of(c * ch, ch), ch)]
    sends = [copy(j, k_me, c, (cx, cy, c), src=mine) for j, (cx, cy, _) in enumerate(chips)]
    load = pltpu.make_async_copy(in_ref, stage, local_sems.at[0])
    store = pltpu.make_async_copy(stage, out_ref.at[k_me], local_sems.at[1])

    def start():
        for cp in sends:
            cp.start()
        load.start()

    def finish():
        load.wait()
        store.start()
        forwards = []
        for j, (_, _, kj) in enumerate(chips):
            copy(j, kj, c, me).wait_recv()
            forwards.append(copy(3 + j, kj, c, sibling))
            forwards[-1].start()
        for j, (_, _, kj) in enumerate(chips):
            copy(3 + j, kj, 1 - c, me).wait_recv()
        for cp in sends + forwards:
            cp.wait_send()
        store.wait()

    return start, finish


def _peers(x, y, c):
    return [((1 - x) if rel & 4 else x, (1 - y) if rel & 2 else y, (1 - c) if rel & 1 else c) for rel in range(1, N_DEV)]


def _exchange_sems(n):
    return [pltpu.SemaphoreType.DMA((n, N_DEV - 1)), pltpu.SemaphoreType.DMA((n, N_DEV - 1)), pltpu.SemaphoreType.DMA((n,))]


def _gather_plan(ins, outs, send_sems, recv_sems, local_sems):
    x, y, c = _place()
    k_me = 2 * x + y
    peers = [(rel, p) for rel, p in enumerate(_peers(x, y, c)) if (rel + 1) & 6]

    def copy(w, rel, k, half, to, src=None):
        rh = ins[w].shape[0] // 2
        rows = outs[w].at[k, pl.ds(pl.multiple_of(half * rh, rh), rh), :]
        return pltpu.make_async_remote_copy(src_ref=rows if src is None else src, dst_ref=rows, send_sem=send_sems.at[w, rel],
                                            recv_sem=recv_sems.at[w, rel], device_id=to, device_id_type=MESH)

    def mine(w):
        rh = ins[w].shape[0] // 2
        return ins[w].at[pl.ds(pl.multiple_of(c * rh, rh), rh), :]

    n = len(ins)
    local = [pltpu.make_async_copy(ins[w], outs[w].at[k_me], local_sems.at[w]) for w in range(n)]
    sends = [copy(w, rel, k_me, c, peer, src=mine(w)) for w in range(n) for rel, peer in peers]

    def start():
        for cp in local + sends:
            cp.start()

    def finish():
        for w in range(n):
            for rel, (px, py, pc) in peers:
                copy(w, rel, 2 * px + py, pc, (x, y, c)).wait_recv()
        for cp in sends:
            cp.wait_send()
        for cp in local:
            cp.wait()

    return start, finish


def _reduce_plan(ins, outs, send_sems, recv_sems, local_sems):
    x, y, c = _place()
    me = 4 * x + 2 * y + c
    peers = _peers(x, y, c)

    def block(w, k, half):
        rh = ins[w].shape[1] // 2
        return ins[w].at[k, pl.ds(pl.multiple_of(half * rh, rh), rh), :]

    def copy(w, rel, src, slot, to):
        return pltpu.make_async_remote_copy(src_ref=src, dst_ref=outs[w].at[slot], send_sem=send_sems.at[w, rel],
                                            recv_sem=recv_sems.at[w, rel], device_id=to, device_id_type=MESH)

    n = len(ins)
    local = [pltpu.make_async_copy(block(w, 2 * x + y, c), outs[w].at[me], local_sems.at[w]) for w in range(n)]
    sends = [copy(w, rel, block(w, 2 * px + py, pc), me, (px, py, pc)) for w in range(n) for rel, (px, py, pc) in enumerate(peers)]

    def start():
        for cp in local + sends:
            cp.start()

    def finish():
        for w in range(n):
            for rel, (px, py, pc) in enumerate(peers):
                copy(w, rel, block(w, 2 * x + y, c), 4 * px + 2 * py + pc, (x, y, c)).wait_recv()
        for cp in sends:
            cp.wait_send()
        for cp in local:
            cp.wait()

    return start, finish


def _scatter_plan(ins, outs, send_sems, recv_sems, local_sems):
    x, y, c = _place()
    k_me = 2 * x + y
    chips = _other_chips(x, y)

    def copy(w, j, src_k, dst_k, to):
        return pltpu.make_async_remote_copy(src_ref=ins[w].at[src_k], dst_ref=outs[w].at[dst_k], send_sem=send_sems.at[w, j],
                                            recv_sem=recv_sems.at[w, j], device_id=to, device_id_type=MESH)

    n = len(ins)
    local = [pltpu.make_async_copy(ins[w].at[k_me], outs[w].at[k_me], local_sems.at[w]) for w in range(n)]
    sends = [copy(w, j, kj, k_me, (cx, cy, c)) for w in range(n) for j, (cx, cy, kj) in enumerate(chips)]

    def start():
        for cp in local + sends:
            cp.start()

    def finish():
        for w in range(n):
            for j, (_, _, kj) in enumerate(chips):
                copy(w, j, k_me, kj, (x, y, c)).wait_recv()
        for cp in sends:
            cp.wait_send()
        for cp in local:
            cp.wait()

    return start, finish


def _row_tile(r, mult=2 * SUBLANES):
    if r % 256 == 0:
        return 256
    return max([t for t in range(mult, 513, mult) if r % t == 0], default=r)


def _pair_exchange(g):
    _, r, cdim = g.shape
    ch = cdim // 2

    def body(g_ref, got_ref, send_sem, recv_sem):
        x, y, c = _place()
        cp = pltpu.make_async_remote_copy(src_ref=g_ref.at[:, :, pl.ds(pl.multiple_of((1 - c) * ch, ch), ch)], dst_ref=got_ref,
                                          send_sem=send_sem, recv_sem=recv_sem, device_id=(x, y, 1 - c), device_id_type=MESH)
        cp.start()
        cp.wait_recv()
        cp.wait_send()

    return pl.pallas_call(
        body, name="pair_exchange", out_shape=jax.ShapeDtypeStruct((N_CHIPS, r, ch), g.dtype),
        in_specs=[ANY], out_specs=ANY, scratch_shapes=[pltpu.SemaphoreType.DMA, pltpu.SemaphoreType.DMA],
    )(g)


def _pair_sum(g, got, c):
    _, r, cdim = g.shape
    ch = cdim // 2
    tr = _row_tile(r)

    def body(c_ref, g_ref, got_ref, o_ref):
        o_ref[...] = (g_ref[...].astype(F32) + got_ref[...].astype(F32)).astype(o_ref.dtype)

    blk = pl.BlockSpec((1, tr, ch), lambda k, i, c_ref: (k, i, 0))
    return pl.pallas_call(
        body, name="pair_sum",
        grid_spec=pltpu.PrefetchScalarGridSpec(
            num_scalar_prefetch=1, grid=(N_CHIPS, r // tr),
            in_specs=[pl.BlockSpec((1, tr, ch), lambda k, i, c_ref: (k, i, c_ref[0])), blk], out_specs=blk),
        out_shape=jax.ShapeDtypeStruct((N_CHIPS, r, ch), BF16),
        compiler_params=_params(("parallel", "parallel")),
    )(jnp.reshape(c, (1,)).astype(jnp.int32), g, got)


def _sum_blocks(name, parts):
    k, r, cdim = parts.shape
    tr = _row_tile(r)

    def body(p_ref, o_ref):
        acc = p_ref[0].astype(F32)
        for i in range(1, k):
            acc = acc + p_ref[i].astype(F32)
        o_ref[...] = acc

    return pl.pallas_call(
        body, name=name, grid=(r // tr,),
        in_specs=[pl.BlockSpec((k, tr, cdim), lambda i: (0, i, 0))], out_specs=pl.BlockSpec((tr, cdim), lambda i: (i, 0)),
        out_shape=jax.ShapeDtypeStruct((r, cdim), F32), compiler_params=_params(("parallel",)),
    )(parts)


def _pair_swap(halves):
    n = len(halves)

    def body(*refs):
        ins, outs = refs[:n], refs[n:2 * n]
        send_sems, recv_sems = refs[2 * n:]
        x, y, c = _place()
        cps = [pltpu.make_async_remote_copy(src_ref=ins[w], dst_ref=outs[w], send_sem=send_sems.at[w], recv_sem=recv_sems.at[w],
                                            device_id=(x, y, 1 - c), device_id_type=MESH) for w in range(n)]
        for cp in cps:
            cp.start()
        for cp in cps:
            cp.wait_recv()
        for cp in cps:
            cp.wait_send()

    return pl.pallas_call(
        body, name="pair_swap", out_shape=[jax.ShapeDtypeStruct(h.shape, h.dtype) for h in halves],
        in_specs=[ANY] * n, out_specs=[ANY] * n,
        scratch_shapes=[pltpu.SemaphoreType.DMA((n,)), pltpu.SemaphoreType.DMA((n,))],
    )(*halves)


ADA_SHARD = 6 * D_MODEL // N_CHIPS


def _mod_part(c_all, w_shard, b_shard):
    tn = 512

    def body(c_ref, w_ref, b_ref, o_ref):
        o_ref[...] = _dot(_silu(c_ref[...]).astype(BF16), w_ref[...].astype(BF16)) + b_ref[...]

    return pl.pallas_call(
        body, name="mod_part", grid=(ADA_SHARD // tn,),
        in_specs=[pl.BlockSpec((N_DEV, D_MODEL), lambda j: (0, 0)), pl.BlockSpec((D_MODEL, tn), lambda j: (0, j)),
                  pl.BlockSpec((1, tn), lambda j: (0, j))],
        out_specs=pl.BlockSpec((N_DEV, tn), lambda j: (0, j)),
        out_shape=jax.ShapeDtypeStruct((N_DEV, ADA_SHARD), F32), compiler_params=_params(("parallel",)),
    )(c_all, w_shard, b_shard)


def _w_ada_grad(c_all_t, dmod_shard):
    tm = 256

    def body(ct_ref, dm_ref, o_ref):
        act = _silu(ct_ref[...])
        acc = act[:, 0:1] * dm_ref[0:1, :]
        for dev in range(1, N_DEV):
            acc = acc + act[:, dev:dev + 1] * dm_ref[dev:dev + 1, :]
        o_ref[...] = acc

    return pl.pallas_call(
        body, name="w_ada_grad", grid=(D_MODEL // tm,),
        in_specs=[pl.BlockSpec((tm, N_DEV), lambda i: (i, 0)), pl.BlockSpec((N_DEV, ADA_SHARD), lambda i: (0, 0))],
        out_specs=pl.BlockSpec((tm, ADA_SHARD), lambda i: (i, 0)),
        out_shape=jax.ShapeDtypeStruct((D_MODEL, ADA_SHARD), F32), compiler_params=_params(("parallel",)),
    )(c_all_t, dmod_shard)


def _adamw_math(w, g, m, v):
    nm = ADAM_B1 * m + (1.0 - ADAM_B1) * g
    nv = ADAM_B2 * v + (1.0 - ADAM_B2) * jnp.square(g)
    m_hat = nm / (1.0 - ADAM_B1 ** ADAM_STEP)
    v_hat = nv / (1.0 - ADAM_B2 ** ADAM_STEP)
    return -ADAM_LR * (m_hat / (jnp.sqrt(v_hat) + ADAM_EPS) + ADAM_WD * w), nm, nv


def _adamw(name, w, g, m, v):
    _, r, cdim = w.shape
    tr = 256 if r % 256 == 0 else r

    def body(w_ref, g_ref, m_ref, v_ref, go_ref, d_ref, nm_ref, nv_ref):
        go_ref[...] = g_ref[...]
        d_ref[...], nm_ref[...], nv_ref[...] = _adamw_math(w_ref[...], g_ref[...], m_ref[...], v_ref[...])

    blk = pl.BlockSpec((None, tr, cdim), lambda i: (0, i, 0))
    return pl.pallas_call(
        body, name=name, grid=(r // tr,), in_specs=[blk, pl.BlockSpec((tr, cdim), lambda i: (i, 0)), blk, blk], out_specs=[blk] * 4,
        out_shape=[jax.ShapeDtypeStruct((1, r, cdim), F32)] * 4, compiler_params=_params(("parallel",)),
    )(w, g, m, v)


def _adamw_pair(name, w, mine, other, m, v, c, by_cols=False):
    _, r, cdim = w.shape
    hr, hc = mine.shape
    tr = _row_tile(hr, SUBLANES)
    per = hr // tr

    def body(c_ref, w_ref, a_ref, b_ref, m_ref, v_ref, g_ref, d_ref, nm_ref, nv_ref):
        half = pl.program_id(1) if by_cols else pl.program_id(0) // per
        g = jnp.where(half == c_ref[0], a_ref[...], b_ref[...])
        g_ref[...] = g
        d_ref[...], nm_ref[...], nv_ref[...] = _adamw_math(w_ref[...], g, m_ref[...], v_ref[...])

    blk = pl.BlockSpec((None, tr, hc), lambda i, j, c_ref: (0, i, j))
    half = pl.BlockSpec((tr, hc), lambda i, j, c_ref: (i % per, 0))
    return pl.pallas_call(
        body, name=name,
        grid_spec=pltpu.PrefetchScalarGridSpec(num_scalar_prefetch=1, grid=(r // tr, cdim // hc),
                                               in_specs=[blk, half, half, blk, blk], out_specs=[blk] * 4),
        out_shape=[jax.ShapeDtypeStruct((1, r, cdim), F32)] * 4, compiler_params=_params(("parallel", "parallel")),
    )(jnp.reshape(c, (1,)).astype(jnp.int32), w, mine, other, m, v)


SMALL = ["b_ada", "conv_b", "dt_bias", "a_log", "d_skip", "ssm_norm_w", "f_bias", "attn_norm_w", "ln1_g", "ln1_b", "ln2_g", "ln2_b"]


def _pack(vs):
    pieces = []
    for v in vs:
        pieces.append(v)
        if v.shape[1] % LANES:
            pieces.append(jnp.zeros((1, -v.shape[1] % LANES), v.dtype))
    return jnp.concatenate(pieces, axis=1)


def _adamw_small(total, offs, ws, ms, vs):
    n = len(ws)

    def body(*refs):
        t_ref, outs = refs[0], refs[1 + 3 * n:]
        for i in range(n):
            g = t_ref[:, offs[i]:offs[i] + ws[i].shape[1]]
            dl, nm, nv = _adamw_math(refs[1 + i][...], g, refs[1 + n + i][...], refs[1 + 2 * n + i][...])
            outs[4 * i][...], outs[4 * i + 1][...], outs[4 * i + 2][...], outs[4 * i + 3][...] = g, dl, nm, nv

    res = pl.pallas_call(
        body, name="adamw_small", in_specs=[VMEM_SPEC] * (1 + 3 * n), out_specs=[VMEM_SPEC] * (4 * n),
        out_shape=[jax.ShapeDtypeStruct(w.shape, F32) for w in ws for _ in range(4)],
    )(total, *ws, *ms, *vs)
    return [res[4 * i:4 * i + 4] for i in range(n)]


def kernel(x, c, w_ada, b_ada, w_in, conv_w, conv_b, dt_bias, a_log, d_skip, ssm_norm_w, f_bias, attn_norm_w, w_out, ln1_g, ln1_b, w_ff_in, w_ff_out, ln2_g, ln2_b, loss_target, m_w_ada, m_b_ada, m_w_in, m_conv_w, m_conv_b, m_dt_bias, m_a_log, m_d_skip, m_ssm_norm_w, m_f_bias, m_attn_norm_w, m_w_out, m_ln1_g, m_ln1_b, m_w_ff_in, m_w_ff_out, m_ln2_g, m_ln2_b, v_w_ada, v_b_ada, v_w_in, v_conv_w, v_conv_b, v_dt_bias, v_a_log, v_d_skip, v_ssm_norm_w, v_f_bias, v_attn_norm_w, v_w_out, v_ln1_g, v_ln1_b, v_w_ff_in, v_w_ff_out, v_ln2_g, v_ln2_b):
    a = dict(b_ada=b_ada, conv_b=conv_b, dt_bias=dt_bias, a_log=a_log, d_skip=d_skip, ssm_norm_w=ssm_norm_w, f_bias=f_bias,
             attn_norm_w=attn_norm_w, ln1_g=ln1_g, ln1_b=ln1_b, ln2_g=ln2_g, ln2_b=ln2_b)
    ms = dict(b_ada=m_b_ada, conv_b=m_conv_b, dt_bias=m_dt_bias, a_log=m_a_log, d_skip=m_d_skip, ssm_norm_w=m_ssm_norm_w,
              f_bias=m_f_bias, attn_norm_w=m_attn_norm_w, ln1_g=m_ln1_g, ln1_b=m_ln1_b, ln2_g=m_ln2_g, ln2_b=m_ln2_b)
    vs = dict(b_ada=v_b_ada, conv_b=v_conv_b, dt_bias=v_dt_bias, a_log=v_a_log, d_skip=v_d_skip, ssm_norm_w=v_ssm_norm_w,
              f_bias=v_f_bias, attn_norm_w=v_attn_norm_w, ln1_g=v_ln1_g, ln1_b=v_ln1_b, ln2_g=v_ln2_g, ln2_b=v_ln2_b)
    xi, yi, ci = _place()
    chip = 2 * xi + yi
    me = 4 * xi + 2 * yi + ci
    d = D_MODEL
    conv_shard = CONV_DIM // N_CHIPS

    first = _allgather_small("gather_c", jnp.concatenate([c, conv_w[0].reshape(1, CONV_W * conv_shard)], axis=1))[:, 0]
    c_all = first[:, :d]
    conv_w_full = first[::2, d:].reshape(N_CHIPS, CONV_W, conv_shard).transpose(1, 0, 2).reshape(CONV_W, CONV_DIM)
    b_shard = lax.dynamic_slice_in_dim(b_ada, chip * ADA_SHARD, ADA_SHARD, axis=1)
    parts = _allgather_small("gather_mod", _mod_part(c_all, w_ada[0], b_shard))
    mod = lax.dynamic_index_in_dim(parts[::2], me, axis=1, keepdims=False).reshape(1, 6 * d)

    w_in_t, m_w_in_t, v_w_in_t = [jnp.transpose(t, (0, 2, 1)) for t in (w_in, m_w_in, v_w_in)]
    w_alt = _to_aligned(_gather_shards(w_in_t[0].astype(BF16)).reshape(IN_COLS, d))

    sp = {n: a[n] for n in SMALL[1:]}
    sp["conv_w"] = conv_w_full
    shards = [w_out[0].astype(BF16), w_ff_in[0].astype(BF16), w_ff_out[0].astype(BF16)]
    dx, landed, small = _local_step(x[0], loss_target[0], mod, w_alt, shards, sp)

    names = ["mod"] + SMALL[1:]
    vec = _pack([small[n] for n in names] + [small["conv_w"].reshape(1, CONV_W * CONV_DIM), small["loss"]])
    every, total = _allgather_small("gather_small", vec, with_sum=True)
    widths = [6 * d] + [a[n].shape[1] for n in SMALL[1:]]
    offs = [0]
    for w in widths:
        offs.append(offs[-1] + w + (-w % LANES))
    g_conv_w_full = total[:, offs[-1]:offs[-1] + CONV_W * CONV_DIM].reshape(CONV_W, CONV_DIM)
    loss = total[0, offs[-1] + CONV_W * CONV_DIM]
    dmod_shard = lax.dynamic_slice_in_dim(every[:, 0, :6 * d], chip * ADA_SHARD, ADA_SHARD, axis=1)
    g_w_ada = _w_ada_grad(c_all.T, dmod_shard)
    g_conv_w = lax.dynamic_slice_in_dim(g_conv_w_full, chip * conv_shard, conv_shard, axis=1)

    mine = [_sum_blocks("dev_sum_%d" % i, p) for i, p in enumerate(landed)]
    other = _pair_swap(mine)

    grads, deltas, new_m, new_v = {}, {}, {}, {}
    paired = dict(w_in=(w_in_t, m_w_in_t, v_w_in_t), w_out=(w_out, m_w_out, v_w_out), w_ff_in=(w_ff_in, m_w_ff_in, v_w_ff_in),
                  w_ff_out=(w_ff_out, m_w_ff_out, v_w_ff_out))
    for i, (n, (w, m, v)) in enumerate(paired.items()):
        res = _adamw_pair("adamw_" + n, w, mine[i], other[i], m, v, ci, by_cols=n == "w_in")
        grads[n], deltas[n], new_m[n], new_v[n] = [jnp.transpose(t, (0, 2, 1)) for t in res] if n == "w_in" else res
    for n, g, (w, m, v) in (("w_ada", g_w_ada, (w_ada, m_w_ada, v_w_ada)), ("conv_w", g_conv_w, (conv_w, m_conv_w, v_conv_w))):
        grads[n], deltas[n], new_m[n], new_v[n] = _adamw("adamw_" + n, w, g, m, v)
    for n, res in zip(SMALL, _adamw_small(total, offs, [a[n] for n in SMALL], [ms[n] for n in SMALL], [vs[n] for n in SMALL])):
        grads[n], deltas[n], new_m[n], new_v[n] = res

    order = ["w_ada", "b_ada", "w_in", "conv_w", "conv_b", "dt_bias", "a_log", "d_skip", "ssm_norm_w", "f_bias", "attn_norm_w", "w_out",
             "ln1_g", "ln1_b", "w_ff_in", "w_ff_out", "ln2_g", "ln2_b"]
    return (loss, dx[None], *[grads[n] for n in order], *[deltas[n] for n in order], *[new_m[n] for n in order], *[new_v[n] for n in order])
```

```python
import functools

import jax
import jax.numpy as jnp
from jax import lax
from jax.experimental import pallas as pl
from jax.experimental.pallas import tpu as pltpu

F32, BF16 = jnp.float32, jnp.bfloat16

D_MODEL = 1024
N_HEADS = 16
HEAD_DIM = 64
N_PAIRS = N_HEADS // 2
SSM_GROUPS = 2
SSM_STATE = 128
CHUNK = 128
CONV_W = 4
CONV_DIM = 1536
D_FF = 4096
IN_COLS = 5664
ALPHA = 2.0 ** 0.25
LN_EPS = 1e-5
RMS_EPS = 1e-5
LANES = 128
SUBLANES = 8

AL_Z, AL_XS, AL_Q, AL_K, AL_V, AL_B, AL_C, AL_DTF = 0, 1024, 2048, 3072, 4096, 5120, 5376, 5632
AL_COLS = 5760
F_LANE = 16

ADAM_LR, ADAM_B1, ADAM_B2, ADAM_EPS, ADAM_WD, ADAM_STEP = 0.001, 0.9, 0.999, 1e-08, 0.01, 10

VMEM_LIMIT = 56 * 1024 * 1024
MESH = pl.DeviceIdType.MESH


def _params(sem=None):
    return pltpu.CompilerParams(dimension_semantics=sem, vmem_limit_bytes=VMEM_LIMIT)


def _sigmoid(x):
    return 1.0 / (1.0 + jnp.exp(-x))


def _silu(x):
    return x * _sigmoid(x)


def _softplus(x):
    return jnp.maximum(x, 0.0) + jnp.log(1.0 + jnp.exp(-jnp.abs(x)))


def _split3(a):
    hi = a.astype(BF16)
    r = a - hi.astype(F32)
    mid = r.astype(BF16)
    lo = (r - mid.astype(F32)).astype(BF16)
    return hi, mid, lo


def _dot(a, b, dims=((1,), (0,))):
    return lax.dot_general(a, b, (dims, ((), ())), preferred_element_type=F32)


NN, NT, TN = ((1,), (0,)), ((1,), (1,)), ((0,), (0,))


def _dot3(t, a):
    hi, mid, lo = _split3(a)
    return _dot(t, hi) + _dot(t, mid) + _dot(t, lo)


def _matmul(name, a, b, *, dims=NN, out_dtype=F32, tm=1024, tn=1024, tk=1024, by_chip=None, epi=None, carry=()):
    if dims == NN:
        (m, k), n = a.shape, b.shape[1]
    elif dims == NT:
        (m, k), n = a.shape, b.shape[0]
    else:
        (k, m), n = a.shape, b.shape[1]
    if by_chip == "rows":
        tm = min(tm, m // 4)
    if by_chip == "cols":
        tn = min(tn, n // 4)
    tm, tn, tk = min(tm, m), min(tn, n), min(tk, k)
    assert m % tm == 0 and n % tn == 0 and k % tk == 0, (name, m, n, k, tm, tn, tk)
    nk = k // tk
    if by_chip == "rows":
        per = m // 4 // tm
        out_spec = pl.BlockSpec((None, tm, tn), lambda i, j, l: (i // per, i % per, j))
        out_shape = jax.ShapeDtypeStruct((4, m // 4, n), out_dtype)
    elif by_chip == "cols":
        per = n // 4 // tn
        out_spec = pl.BlockSpec((None, tm, tn), lambda i, j, l: (j // per, i, j % per))
        out_shape = jax.ShapeDtypeStruct((4, m, n // 4), out_dtype)
    else:
        out_spec = pl.BlockSpec((tm, tn), lambda i, j, l: (i, j))
        out_shape = jax.ShapeDtypeStruct((m, n), out_dtype)
    a_spec = pl.BlockSpec((tk, tm), lambda i, j, l: (l, i)) if dims == TN else pl.BlockSpec((tm, tk), lambda i, j, l: (i, l))
    b_spec = pl.BlockSpec((tn, tk), lambda i, j, l: (j, l)) if dims == NT else pl.BlockSpec((tk, tn), lambda i, j, l: (l, j))

    tile = pl.BlockSpec((tm, tn), lambda i, j, l: (i, j))
    in_specs, args, out_specs, out_shape = [a_spec, b_spec], [a, b], [out_spec], [out_shape]
    fn, n_tiles, n_sums = None, 1, 0
    if epi is not None:
        fn, fulls, vecs, outs, sums = epi
        assert by_chip is None and (not sums or n == tn), name
        in_specs = in_specs + [tile] * len(fulls) + [pl.BlockSpec((1, tn), lambda i, j, l: (0, j))] * len(vecs)
        args = args + list(fulls) + list(vecs)
        flipped = pl.BlockSpec((tn, tm), lambda i, j, l: (j, i))
        out_specs = [flipped if isinstance(dt, tuple) else tile for dt in outs] + [pl.BlockSpec((1, w), lambda i, j, l: (0, 0)) for w in sums]
        out_shape = [jax.ShapeDtypeStruct((n, m), dt[1]) if isinstance(dt, tuple) else jax.ShapeDtypeStruct((m, n), dt) for dt in outs]
        out_shape += [jax.ShapeDtypeStruct((1, w), F32) for w in sums]
        n_tiles, n_sums = len(outs), len(sums)
    n_in, n_out, n_c = len(args), len(out_specs), len(carry)
    scratch = [pltpu.VMEM((tm, tn) if nk > 1 else (SUBLANES, LANES), F32)]
    if n_c:
        in_specs, args = in_specs + [ANY] * n_c, args + list(carry)
        out_specs = out_specs + [ANY] * n_c
        out_shape = out_shape + [jax.ShapeDtypeStruct(g.shape, g.dtype) for g in carry]
        scratch = scratch + _exchange_sems(n_c)
    gm, gn = m // tm, n // tn

    def body(*refs):
        a_ref, b_ref = refs[:2]
        ins, outs = refs[2:n_in], refs[n_in + n_c:n_in + n_c + n_out]
        acc_ref = refs[n_in + 2 * n_c + n_out]
        i, j, l = pl.program_id(0), pl.program_id(1), pl.program_id(2)
        if n_c:
            start, wait = _scatter_plan(refs[n_in:n_in + n_c], refs[n_in + n_c + n_out:n_in + 2 * n_c + n_out], *refs[n_in + 2 * n_c + n_out + 1:])
            pl.when((i == 0) & (j == 0) & (l == 0))(start)
        part = _dot(a_ref[...].astype(BF16), b_ref[...].astype(BF16), dims)

        def finish(res):
            if fn is None:
                outs[0][...] = res.astype(outs[0].dtype)
                return
            tiles, colsums = fn(res, *[r[...] for r in ins])
            for r, val in zip(outs[:n_tiles], tiles):
                r[...] = val.astype(r.dtype)
            if n_sums:
                @pl.when(i == 0)
                def _():
                    for r in outs[n_tiles:]:
                        r[...] = jnp.zeros_like(r)
                for r, val in zip(outs[n_tiles:], colsums):
                    r[...] += val

        if nk == 1:
            finish(part)
        else:
            @pl.when(l == 0)
            def _():
                acc_ref[...] = part

            @pl.when((l > 0) & (l < nk - 1))
            def _():
                acc_ref[...] += part

            @pl.when(l == nk - 1)
            def _():
                finish(acc_ref[...] + part)

        if n_c:
            pl.when((i == gm - 1) & (j == gn - 1) & (l == nk - 1))(wait)

    res = pl.pallas_call(
        body, name=name, grid=(gm, gn, nk),
        in_specs=in_specs, out_specs=out_specs, out_shape=out_shape, scratch_shapes=scratch,
        compiler_params=_params(("arbitrary",) * 3 if n_c or n_sums else ("parallel", "parallel", "arbitrary")),
    )(*args)
    return res[0] if len(res) == 1 else res


def _rowwise(name, fn, fulls, vecs, out_fulls, out_vecs, tr=256):
    fulls = [f if isinstance(f, tuple) else (f, f.shape[1], 0) for f in fulls]
    s = fulls[0][0].shape[0]
    tr = min(tr, s)
    out_fulls = [o if len(o) == 3 else (*o, (o[0], 0, None)) for o in out_fulls]
    into = [(k, slab[2]) for k, (_, _, slab) in enumerate(out_fulls) if slab[2] is not None]
    nf, nv, nof, nov = len(fulls), len(vecs), len(out_fulls), len(out_vecs)
    in_specs = [pl.BlockSpec((tr, w), functools.partial(lambda i, cb: (i, cb), cb=cb)) for (_, w, cb) in fulls]
    in_specs += [pl.BlockSpec(v.shape, lambda i: (0, 0)) for v in vecs] + [ANY] * len(into)
    out_shape = [jax.ShapeDtypeStruct((s, slab[0]), dt) for (_, dt, slab) in out_fulls] + [jax.ShapeDtypeStruct((1, w), F32) for w in out_vecs]
    out_specs = [pl.BlockSpec((tr, w), functools.partial(lambda i, cb: (i, cb), cb=slab[1])) for (w, _, slab) in out_fulls]
    out_specs += [pl.BlockSpec((1, w), lambda i: (0, 0)) for w in out_vecs]

    def body(*refs):
        outs = refs[nf + nv + len(into):]
        of, ov = fn(*[r[...] for r in refs[:nf + nv]])
        for r, val in zip(outs[:nof], of):
            r[...] = val.astype(r.dtype)
        if nov:
            @pl.when(pl.program_id(0) == 0)
            def _():
                for r in outs[nof:]:
                    r[...] = jnp.zeros_like(r)
            for r, val in zip(outs[nof:], ov):
                r[...] += val

    res = pl.pallas_call(
        body, name=name, grid=(s // tr,), in_specs=in_specs, out_specs=out_specs, out_shape=out_shape,
        input_output_aliases={nf + nv + pos: k for pos, (k, _) in enumerate(into)},
        compiler_params=_params(("arbitrary",)),
    )(*[f[0] for f in fulls], *vecs, *[buf for _, buf in into])
    return res[:nof], res[nof:]


def _colsum(x):
    return jnp.sum(x, axis=0, keepdims=True)


def _rowmean(x):
    return jnp.mean(x, axis=-1, keepdims=True)


CONV_CB = 512
CONV_TR = 512


def _shift_down(u, halo, j):
    if j == 0:
        return u
    ru = pltpu.roll(u, j, 0)
    row8 = lax.broadcasted_iota(jnp.int32, halo.shape, 0)
    top = jnp.where(row8 < j, pltpu.roll(halo, j, 0), ru[:SUBLANES])
    return jnp.concatenate([top, ru[SUBLANES:]], axis=0)


def _shift_up(d, halo, j):
    if j == 0:
        return d
    tr = d.shape[0]
    rd = pltpu.roll(d, tr - j, 0)
    row8 = lax.broadcasted_iota(jnp.int32, halo.shape, 0)
    bot = jnp.where(row8 >= SUBLANES - j, pltpu.roll(halo, SUBLANES - j, 0), rd[tr - SUBLANES:])
    return jnp.concatenate([rd[:tr - SUBLANES], bot], axis=0)


def _conv_col(cb):
    return jnp.where(cb < 2, AL_XS // CONV_CB + cb, AL_B // CONV_CB)


def _conv_specs(s, tr):
    per8 = tr // SUBLANES
    blk = pl.BlockSpec((tr, CONV_CB), lambda cb, i: (i, _conv_col(cb)))
    prev = pl.BlockSpec((SUBLANES, CONV_CB), lambda cb, i: (jnp.maximum(i * per8 - 1, 0), _conv_col(cb)))
    return blk, prev


def _conv_pre(u, halo, w_ref, b_ref, first):
    halo = jnp.where(first, 0.0, halo)
    acc = b_ref[...] + w_ref[CONV_W - 1:CONV_W, :] * u
    shifted = [u]
    for j in range(1, CONV_W):
        sh = _shift_down(u, halo, j)
        shifted.append(sh)
        acc = acc + w_ref[CONV_W - 1 - j:CONV_W - j, :] * sh
    return acc, shifted


def _conv_fwd(proj, conv_w, conv_b):
    s = proj.shape[0]
    tr = min(CONV_TR, s)
    blk, prev = _conv_specs(s, tr)

    def body(u_ref, h_ref, w_ref, b_ref, o_ref):
        pre, _ = _conv_pre(u_ref[...], h_ref[...], w_ref, b_ref, pl.program_id(1) == 0)
        o_ref[...] = _silu(pre)

    return pl.pallas_call(
        body, name="conv_fwd", grid=(CONV_DIM // CONV_CB, s // tr),
        in_specs=[blk, prev, pl.BlockSpec((CONV_W, CONV_CB), lambda cb, i: (0, cb)), pl.BlockSpec((1, CONV_CB), lambda cb, i: (0, cb))],
        out_specs=pl.BlockSpec((tr, CONV_CB), lambda cb, i: (i, cb)),
        out_shape=jax.ShapeDtypeStruct((s, CONV_DIM), F32),
        compiler_params=_params(("parallel", "parallel")),
    )(proj, proj, conv_w, conv_b)


def _conv_bwd_pre(proj, conv_w, conv_b, dxc):
    s = proj.shape[0]
    tr = min(CONV_TR, s)
    blk, prev = _conv_specs(s, tr)

    def body(u_ref, h_ref, w_ref, b_ref, d_ref, dpre_ref, dw_ref, db_ref):
        i = pl.program_id(1)
        pre, shifted = _conv_pre(u_ref[...], h_ref[...], w_ref, b_ref, i == 0)
        sg = _sigmoid(pre)
        dpre = d_ref[...] * (sg * (1.0 + pre * (1.0 - sg)))
        dpre_ref[...] = dpre

        @pl.when(i == 0)
        def _():
            dw_ref[...] = jnp.zeros_like(dw_ref)
            db_ref[...] = jnp.zeros_like(db_ref)

        db_ref[...] += _colsum(dpre)
        for j in range(CONV_W):
            dw_ref[CONV_W - 1 - j:CONV_W - j, :] += _colsum(dpre * shifted[j])

    own = pl.BlockSpec((tr, CONV_CB), lambda cb, i: (i, cb))
    wspec = pl.BlockSpec((CONV_W, CONV_CB), lambda cb, i: (0, cb))
    bspec = pl.BlockSpec((1, CONV_CB), lambda cb, i: (0, cb))
    return pl.pallas_call(
        body, name="conv_bwd_pre", grid=(CONV_DIM // CONV_CB, s // tr),
        in_specs=[blk, prev, wspec, bspec, own], out_specs=[own, wspec, bspec],
        out_shape=[jax.ShapeDtypeStruct((s, CONV_DIM), F32), jax.ShapeDtypeStruct((CONV_W, CONV_DIM), F32),
                   jax.ShapeDtypeStruct((1, CONV_DIM), F32)],
        compiler_params=_params(("parallel", "arbitrary")),
    )(proj, proj, conv_w, conv_b, dxc)


def _conv_bwd_in(dpre, conv_w, dproj):
    s = dpre.shape[0]
    tr = min(CONV_TR, s)
    per8 = tr // SUBLANES
    last8 = s // SUBLANES - 1
    nb = s // tr

    def body(d_ref, n_ref, w_ref, _, o_ref):
        d = d_ref[...]
        halo = jnp.where(pl.program_id(1) == nb - 1, 0.0, n_ref[...])
        acc = w_ref[CONV_W - 1:CONV_W, :] * d
        for j in range(1, CONV_W):
            acc = acc + w_ref[CONV_W - 1 - j:CONV_W - j, :] * _shift_up(d, halo, j)
        o_ref[...] = acc.astype(o_ref.dtype)

    own = pl.BlockSpec((tr, CONV_CB), lambda cb, i: (i, cb))
    nxt = pl.BlockSpec((SUBLANES, CONV_CB), lambda cb, i: (jnp.minimum((i + 1) * per8, last8), cb))
    return pl.pallas_call(
        body, name="conv_bwd_in", grid=(CONV_DIM // CONV_CB, nb),
        in_specs=[own, nxt, pl.BlockSpec((CONV_W, CONV_CB), lambda cb, i: (0, cb)), ANY],
        out_specs=pl.BlockSpec((tr, CONV_CB), lambda cb, i: (i, _conv_col(cb))),
        out_shape=jax.ShapeDtypeStruct(dproj.shape, dproj.dtype), input_output_aliases={3: 0},
        compiler_params=_params(("parallel", "parallel")),
    )(dpre, dpre, conv_w, dproj)


XC_B, XC_C = 1024, 1280


def _tile_iotas():
    row = lax.broadcasted_iota(jnp.int32, (CHUNK, LANES), 0)
    lane = lax.broadcasted_iota(jnp.int32, (CHUNK, LANES), 1)
    return row, lane


def _ssd_scalars(dtf_ref, bias_ref, alog_ref, row, lane):
    head = lane[:1] < N_HEADS
    raw = dtf_ref[...] + bias_ref[...]
    dt = _softplus(raw)
    a_neg = jnp.where(head, -jnp.exp(alog_ref[...]), 0.0)
    a = dt * a_neg
    tril = (row >= lane).astype(BF16)
    s = _dot3(tril, a)
    return raw, dt, a_neg, s


def _pair(v, j, lo):
    return jnp.where(lo, v[:, 2 * j:2 * j + 1], v[:, 2 * j + 1:2 * j + 2])


def _head_sum(x, lo, hh):
    return jnp.sum(jnp.where(lo == (hh == 0), x, 0.0), axis=1, keepdims=True)


def _decay_masks(s, st, h, row, lane):
    s_col = jnp.broadcast_to(s[:, h:h + 1], (CHUNK, LANES))
    s_row = jnp.broadcast_to(st[h:h + 1, :], (CHUNK, LANES))
    lm = jnp.where(row >= lane, jnp.exp(s_col - s_row), 0.0)
    lmt = jnp.where(row <= lane, jnp.exp(s_row - s_col), 0.0)
    return lm, lmt


def _ssd_fwd(xc_all, proj, dt_bias_l, a_log_l, d_exp):
    s_len = xc_all.shape[0]
    nc = s_len // CHUNK

    def body(x_ref, dtf_ref, bias_ref, alog_ref, dexp_ref, y_ref, prevs_ref, state_ref):
        @pl.when(pl.program_id(0) == 0)
        def _():
            state_ref[...] = jnp.zeros_like(state_ref)

        row, lane = _tile_iotas()
        lo = lane < HEAD_DIM
        _, dt, _, s = _ssd_scalars(dtf_ref, bias_ref, alog_ref, row, lane)
        tot = s[CHUNK - 1:CHUNK, :]
        st = s.T
        for g in range(SSM_GROUPS):
            bg = x_ref[:, XC_B + g * SSM_STATE:XC_B + (g + 1) * SSM_STATE].astype(BF16)
            cg = x_ref[:, XC_C + g * SSM_STATE:XC_C + (g + 1) * SSM_STATE].astype(BF16)
            cb = _dot(cg, bg, NT)
            for j in range(g * 4, g * 4 + 4):
                xs_p = x_ref[:, j * LANES:(j + 1) * LANES]
                dt_p, s_p, tot_p = _pair(dt, j, lo), _pair(s, j, lo), _pair(tot, j, lo[:1])
                xc_p = xs_p * dt_p
                xc_b = xc_p.astype(BF16)
                yd = []
                for hh in range(2):
                    lm, _ = _decay_masks(s, st, 2 * j + hh, row, lane)
                    yd.append(_dot((cb * lm).astype(BF16), xc_b))
                prev = state_ref[j]
                prevs_ref[0, j] = prev
                yo = _dot(cg, prev.astype(BF16)) * jnp.exp(s_p)
                y_ref[:, j * LANES:(j + 1) * LANES] = jnp.where(lo, yd[0], yd[1]) + yo + dexp_ref[:, j * LANES:(j + 1) * LANES] * xs_p
                to_end = jnp.exp(tot_p - s_p)
                state_ref[j] = jnp.exp(tot_p) * prev + _dot(bg, (xc_p * to_end).astype(BF16), TN)

    vec = lambda w: pl.BlockSpec((1, w), lambda c: (0, 0))
    return pl.pallas_call(
        body, name="ssd_fwd", grid=(nc,),
        in_specs=[pl.BlockSpec((CHUNK, CONV_DIM), lambda c: (c, 0)), pl.BlockSpec((CHUNK, LANES), lambda c: (c, AL_DTF // LANES)),
                  vec(LANES), vec(LANES), vec(D_MODEL)],
        out_specs=[pl.BlockSpec((CHUNK, D_MODEL), lambda c: (c, 0)), pl.BlockSpec((1, N_PAIRS, SSM_STATE, LANES), lambda c: (c, 0, 0, 0))],
        out_shape=[jax.ShapeDtypeStruct((s_len, D_MODEL), F32), jax.ShapeDtypeStruct((nc, N_PAIRS, SSM_STATE, LANES), F32)],
        scratch_shapes=[pltpu.VMEM((N_PAIRS, SSM_STATE, LANES), F32)],
        compiler_params=_params(("arbitrary",)),
    )(xc_all, proj, dt_bias_l, a_log_l, d_exp)


def _ssd_bwd(xc_all, proj, dt_bias_l, a_log_l, d_exp, prevs, dy):
    s_len = xc_all.shape[0]
    nc = s_len // CHUNK

    def body(x_ref, dtf_ref, bias_ref, alog_ref, dexp_ref, prevs_ref, dy_ref, dx_ref, ddt_ref, da_ref, dd_ref, dbias_ref, dstate_ref):
        @pl.when(pl.program_id(0) == 0)
        def _():
            dstate_ref[...] = jnp.zeros_like(dstate_ref)
            da_ref[...] = jnp.zeros_like(da_ref)
            dd_ref[...] = jnp.zeros_like(dd_ref)
            dbias_ref[...] = jnp.zeros_like(dbias_ref)

        row, lane = _tile_iotas()
        lo = lane < HEAD_DIM
        last = row == CHUNK - 1
        raw, dt, a_neg, s = _ssd_scalars(dtf_ref, bias_ref, alog_ref, row, lane)
        tot = s[CHUNK - 1:CHUNK, :]
        st = s.T
        ds_acc = jnp.zeros((CHUNK, LANES), F32)
        ddt_acc = jnp.zeros((CHUNK, LANES), F32)
        for g in range(SSM_GROUPS):
            bcol = slice(XC_B + g * SSM_STATE, XC_B + (g + 1) * SSM_STATE)
            ccol = slice(XC_C + g * SSM_STATE, XC_C + (g + 1) * SSM_STATE)
            bg = x_ref[:, bcol].astype(BF16)
            cg = x_ref[:, ccol].astype(BF16)
            cb = _dot(cg, bg, NT)
            cbt = _dot(bg, cg, NT)
            dcb = jnp.zeros((CHUNK, LANES), F32)
            dcbt = jnp.zeros((CHUNK, LANES), F32)
            db_acc = jnp.zeros((CHUNK, LANES), F32)
            dc_acc = jnp.zeros((CHUNK, LANES), F32)
            for j in range(g * 4, g * 4 + 4):
                cols = slice(j * LANES, (j + 1) * LANES)
                xs_p, dy_p = x_ref[:, cols], dy_ref[:, cols]
                dt_p, s_p, tot_p = _pair(dt, j, lo), _pair(s, j, lo), _pair(tot, j, lo[:1])
                xc_p = xs_p * dt_p
                xc_b, dy_b = xc_p.astype(BF16), dy_p.astype(BF16)
                e_p, f_p, etot_p = jnp.exp(s_p), jnp.exp(tot_p - s_p), jnp.exp(tot_p)
                prev, dnext = prevs_ref[0, j], dstate_ref[j]
                prev_b, dnext_b = prev.astype(BF16), dnext.astype(BF16)
                dd_ref[:, cols] += _colsum(dy_p * xs_p)
                dxs_p = dexp_ref[:, cols] * dy_p
                cp = _dot(cg, prev_b)
                gy = (dy_p * e_p).astype(BF16)
                dc_acc += _dot(gy, prev_b, NT)
                dstate_ref[j] = etot_p * dnext + _dot(cg, gy, TN)
                de = dy_p * cp * e_p
                bds = _dot(bg, dnext_b)
                db_acc += _dot((xc_p * f_p).astype(BF16), dnext_b, NT)
                dxc_p = bds * f_p
                df = bds * xc_p * f_p
                dtot_p = _colsum(dnext * prev) * etot_p + _colsum(df)
                dsl = de - df + jnp.where(last, dtot_p, 0.0)
                for hh in range(2):
                    h = 2 * j + hh
                    mine = lo == (hh == 0)
                    lm, lmt = _decay_masks(s, st, h, row, lane)
                    dy_h = jnp.where(mine, dy_p, 0.0).astype(BF16)
                    xc_h = jnp.where(mine, xc_p, 0.0).astype(BF16)
                    dm = _dot(dy_h, xc_b, NT)
                    dmt = _dot(xc_h, dy_b, NT)
                    mt = cbt * lmt
                    dxc_p += _dot(mt.astype(BF16), dy_h)
                    dml, dmtl = dm * lm, dmt * lmt
                    ds_h = jnp.sum(dml * cb - dmtl * cbt + jnp.where(mine, dsl, 0.0), axis=1, keepdims=True)
                    ds_acc += jnp.where(lane == h, ds_h, 0.0)
                    dcb += dml
                    dcbt += dmtl
                    ddt_acc += jnp.where(lane == h, _head_sum(dxc_p * xs_p, lo, hh), 0.0)
                dx_ref[:, cols] = dxs_p + dxc_p * dt_p
            dx_ref[:, ccol] = dc_acc + _dot(dcb.astype(BF16), bg)
            dx_ref[:, bcol] = db_acc + _dot(dcbt.astype(BF16), cg)
        triu = (row <= lane).astype(BF16)
        da = _dot3(triu, ds_acc)
        ddt = ddt_acc + da * a_neg
        da_ref[...] += _colsum(da * dt) * a_neg[:1]
        ddt_raw = jnp.where(lane < N_HEADS, ddt * _sigmoid(raw), 0.0)
        dbias_ref[...] += _colsum(ddt_raw)
        ddt_ref[...] = ddt_raw

    rev = lambda c: nc - 1 - c
    vec = lambda w: pl.BlockSpec((1, w), lambda c: (0, 0))
    return pl.pallas_call(
        body, name="ssd_bwd", grid=(nc,),
        in_specs=[pl.BlockSpec((CHUNK, CONV_DIM), lambda c: (rev(c), 0)), pl.BlockSpec((CHUNK, LANES), lambda c: (rev(c), AL_DTF // LANES)),
                  vec(LANES), vec(LANES), vec(D_MODEL),
                  pl.BlockSpec((1, N_PAIRS, SSM_STATE, LANES), lambda c: (rev(c), 0, 0, 0)),
                  pl.BlockSpec((CHUNK, D_MODEL), lambda c: (rev(c), 0))],
        out_specs=[pl.BlockSpec((CHUNK, CONV_DIM), lambda c: (rev(c), 0)), pl.BlockSpec((CHUNK, LANES), lambda c: (rev(c), 0)),
                   vec(LANES), vec(D_MODEL), vec(LANES)],
        out_shape=[jax.ShapeDtypeStruct((s_len, CONV_DIM), F32), jax.ShapeDtypeStruct((s_len, LANES), F32),
                   jax.ShapeDtypeStruct((1, LANES), F32), jax.ShapeDtypeStruct((1, D_MODEL), F32), jax.ShapeDtypeStruct((1, LANES), F32)],
        scratch_shapes=[pltpu.VMEM((N_PAIRS, SSM_STATE, LANES), F32)],
        compiler_params=_params(("arbitrary",)),
    )(xc_all, proj, dt_bias_l, a_log_l, d_exp, prevs, dy)


AUG_LANES = 6


def _aug_base(hh):
    return HEAD_DIM if hh == 0 else 0


NEG = -1e30
ATT_T = 512


def _fox_cum(proj, f_bias_l):
    s_len = proj.shape[0]
    nc = s_len // CHUNK

    def body(dtf_ref, fb_ref, cum_ref):
        row, lane = _tile_iotas()
        tril = (row >= lane).astype(BF16)
        spread = [(jnp.where(lane == AUG_LANES * row + i, 1.0, 0.0) - jnp.where(lane == AUG_LANES * row + 3 + i, 1.0, 0.0)).astype(BF16)
                  for i in range(3)]

        def step(c, carry):
            rows = pl.ds(pl.multiple_of(c * CHUNK, CHUNK), CHUNK)
            lf = -_softplus(-(dtf_ref[rows, :] + fb_ref[...]))
            lf = jnp.where(lane < N_HEADS, pltpu.roll(lf, LANES - F_LANE, 1), 0.0)
            cs = _dot3(tril, lf) + carry
            parts = _split3(cs)
            cum_ref[rows, :] = _dot(parts[0], spread[0]) + _dot(parts[1], spread[1]) + _dot(parts[2], spread[2])
            return cs[CHUNK - 1:CHUNK, :]

        lax.fori_loop(0, nc, step, jnp.zeros((1, LANES), F32))

    return pl.pallas_call(
        body, name="fox_cum", grid=(1,),
        in_specs=[pl.BlockSpec((s_len, LANES), lambda i: (0, AL_DTF // LANES)), pl.BlockSpec((1, LANES), lambda i: (0, 0))],
        out_specs=pl.BlockSpec((s_len, LANES), lambda i: (0, 0)),
        out_shape=jax.ShapeDtypeStruct((s_len, LANES), F32),
        compiler_params=_params(("arbitrary",)),
    )(proj, f_bias_l)


def _fox_cum_bwd(dcum, proj, f_bias_l, ddt_tile, dproj):
    s_len = proj.shape[0]
    nc = s_len // CHUNK

    def body(dcum_ref, dtf_ref, fb_ref, ddt_ref, _, out_ref, dfb_ref):
        row, lane = _tile_iotas()
        triu = (row <= lane).astype(BF16)
        is_f = (lane >= F_LANE) & (lane < F_LANE + N_HEADS)

        def step(t, carry):
            run, dfb = carry
            rows = pl.ds(pl.multiple_of((nc - 1 - t) * CHUNK, CHUNK), CHUNK)
            rc = _dot3(triu, dcum_ref[rows, :]) + run
            sg = _sigmoid(-(dtf_ref[rows, :] + fb_ref[...]))
            df = jnp.where(is_f, pltpu.roll(rc, F_LANE, 1) * sg, 0.0)
            out_ref[rows, :] = (df + ddt_ref[rows, :]).astype(out_ref.dtype)
            return rc[0:1, :], dfb + _colsum(df)

        _, dfb = lax.fori_loop(0, nc, step, (jnp.zeros((1, LANES), F32), jnp.zeros((1, LANES), F32)))
        dfb_ref[...] = dfb

    whole = pl.BlockSpec((s_len, LANES), lambda i: (0, 0))
    dtf_cols = pl.BlockSpec((s_len, LANES), lambda i: (0, AL_DTF // LANES))
    vec = pl.BlockSpec((1, LANES), lambda i: (0, 0))
    return pl.pallas_call(
        body, name="fox_cum_bwd", grid=(1,),
        in_specs=[whole, dtf_cols, vec, whole, ANY], out_specs=[dtf_cols, vec],
        out_shape=[jax.ShapeDtypeStruct(dproj.shape, dproj.dtype), jax.ShapeDtypeStruct((1, LANES), F32)],
        input_output_aliases={4: 0}, compiler_params=_params(("arbitrary",)),
    )(dcum, proj, f_bias_l, ddt_tile, dproj)


def _attn_prep(proj, cum):
    s_len = proj.shape[0]
    tr = min(256, s_len)

    def body(q_ref, k_ref, v_ref, cum_ref, qa_ref, ka_ref, vb_ref):
        lane = lax.broadcasted_iota(jnp.int32, (tr, LANES), 1)
        lo = lane < HEAD_DIM
        c = cum_ref[...]
        for p in range(N_PAIRS):
            cols = slice(p * LANES, (p + 1) * LANES)
            q, k = q_ref[:, cols] * (HEAD_DIM ** -0.5), k_ref[:, cols]
            for hh in range(2):
                base = _aug_base(hh)
                r = pltpu.roll(c, (base - AUG_LANES * (2 * p + hh)) % LANES, 1)
                first = (lane >= base) & (lane < base + 3)
                second = (lane >= base + 3) & (lane < base + AUG_LANES)
                mine = lo == (hh == 0)
                qa_ref[2 * p + hh] = jnp.where(mine, q, jnp.where(first, r, jnp.where(second, 1.0, 0.0))).astype(BF16)
                ka_ref[2 * p + hh] = jnp.where(mine, k, jnp.where(first, 1.0, jnp.where(second, r, 0.0))).astype(BF16)
        vb_ref[...] = v_ref[...].astype(BF16)

    assert AL_Q % D_MODEL == 0 and AL_K % D_MODEL == 0 and AL_V % D_MODEL == 0
    slab = lambda col0: pl.BlockSpec((tr, D_MODEL), lambda i: (i, col0 // D_MODEL))
    heads = pl.BlockSpec((N_HEADS, tr, LANES), lambda i: (0, i, 0))
    return pl.pallas_call(
        body, name="attn_prep", grid=(s_len // tr,),
        in_specs=[slab(AL_Q), slab(AL_K), slab(AL_V), pl.BlockSpec((tr, LANES), lambda i: (i, 0))],
        out_specs=[heads, heads, pl.BlockSpec((tr, D_MODEL), lambda i: (i, 0))],
        out_shape=[jax.ShapeDtypeStruct((N_HEADS, s_len, LANES), BF16), jax.ShapeDtypeStruct((N_HEADS, s_len, LANES), BF16),
                   jax.ShapeDtypeStruct((s_len, D_MODEL), BF16)],
        compiler_params=_params(("parallel",)),
    )(proj, proj, proj, cum)


def _attn_fwd(qa, ka, vb, halves):
    s_len = vb.shape[0]
    t = min(ATT_T, s_len)
    nq = s_len // t
    n = len(halves)

    def body(qa_ref, ka_ref, vb_ref, *rest):
        o_ref, lse_ref = rest[n:n + 2]
        start, finish = _gather_plan(rest[:n], rest[n + 2:2 * n + 2], *rest[2 * n + 2:])
        i = pl.program_id(1)
        pl.when((pl.program_id(0) == 0) & (i == 0))(start)
        row = lax.broadcasted_iota(jnp.int32, (t, t), 0)
        col = lax.broadcasted_iota(jnp.int32, (t, t), 1)
        lo = lax.broadcasted_iota(jnp.int32, (t, LANES), 1) < HEAD_DIM
        qs = (qa_ref[0], qa_ref[1])

        def block(j, carry, masked):
            rows = pl.ds(pl.multiple_of(j * t, t), t)
            v = vb_ref[rows, :]
            new = []
            for hh in range(2):
                m, l, acc = carry[hh]
                s = _dot(qs[hh], ka_ref[hh, rows, :], NT)
                if masked:
                    s = jnp.where(row >= col, s, NEG)
                m_new = jnp.maximum(m, jnp.max(s, axis=1, keepdims=True))
                alpha = jnp.exp(m - m_new)
                p = jnp.exp(s - m_new)
                new.append((m_new, alpha * l + jnp.sum(p, axis=1, keepdims=True), alpha * acc + _dot(p.astype(BF16), v)))
            return tuple(new)

        init = (jnp.full((t, 1), NEG, F32), jnp.zeros((t, 1), F32), jnp.zeros((t, LANES), F32))
        carry = lax.fori_loop(0, i, functools.partial(block, masked=False), (init, init))
        (m0, l0, acc0), (m1, l1, acc1) = block(i, carry, True)
        o_ref[...] = jnp.where(lo, acc0 / l0, acc1 / l1)
        lse_ref[...] = jnp.where(lo, m0 + jnp.log(l0), m1 + jnp.log(l1))
        pl.when((pl.program_id(0) == N_PAIRS - 1) & (i == nq - 1))(finish)

    out = pl.BlockSpec((t, LANES), lambda p, i: (i, p))
    res = pl.pallas_call(
        body, name="attn_fwd", grid=(N_PAIRS, nq),
        in_specs=[pl.BlockSpec((2, t, LANES), lambda p, i: (p, i, 0)), pl.BlockSpec((2, s_len, LANES), lambda p, i: (p, 0, 0)),
                  pl.BlockSpec((s_len, LANES), lambda p, i: (0, p))] + [ANY] * n,
        out_specs=[out, out] + [ANY] * n,
        out_shape=[jax.ShapeDtypeStruct((s_len, D_MODEL), F32), jax.ShapeDtypeStruct((s_len, D_MODEL), F32)]
        + [jax.ShapeDtypeStruct((N_CHIPS, *h.shape), h.dtype) for h in halves],
        scratch_shapes=_exchange_sems(n),
        compiler_params=_params(("arbitrary", "arbitrary")),
    )(qa, ka, vb, *halves)
    return res[0], res[1], res[2:]


def _attn_bwd(qa, ka, vb, o, lse, do, parts, dproj):
    s_len = vb.shape[0]
    t = min(ATT_T, s_len)
    nq = s_len // t
    n = len(parts)

    def body(qa_ref, ka_ref, vb_ref, o_ref, lse_ref, do_ref, *rest):
        dqa_ref, dka_ref, dv_ref = rest[n + 1:n + 4]
        start, finish = _reduce_plan(rest[:n], rest[n + 4:2 * n + 4], *rest[2 * n + 4:])
        j = pl.program_id(1)
        pl.when((pl.program_id(0) == 0) & (j == 0))(start)

        @pl.when(j == 0)
        def _():
            dqa_ref[...] = jnp.zeros_like(dqa_ref)

        row = lax.broadcasted_iota(jnp.int32, (t, t), 0)
        col = lax.broadcasted_iota(jnp.int32, (t, t), 1)
        lo = lax.broadcasted_iota(jnp.int32, (t, LANES), 1) < HEAD_DIM
        v = vb_ref[...]
        ks = (ka_ref[0], ka_ref[1])

        def block(i, carry, masked):
            dk, dv = list(carry[:2]), carry[2]
            rows = pl.ds(pl.multiple_of(i * t, t), t)
            do_p, o_p, lse_p = do_ref[rows, :], o_ref[rows, :], lse_ref[rows, :]
            for hh in range(2):
                q = qa_ref[hh, rows, :]
                do_h = jnp.where(lo == (hh == 0), do_p, 0.0)
                delta = jnp.sum(do_h * o_p, axis=1, keepdims=True)
                s = _dot(q, ks[hh], NT)
                if masked:
                    s = jnp.where(row >= col, s, NEG)
                p = jnp.exp(s - lse_p[:, hh * HEAD_DIM:hh * HEAD_DIM + 1])
                do_b = do_h.astype(BF16)
                ds = (p * (_dot(do_b, v, NT) - delta)).astype(BF16)
                dv = dv + _dot(p.astype(BF16), do_b, TN)
                dk[hh] = dk[hh] + _dot(ds, q, TN)
                dqa_ref[hh, rows, :] += _dot(ds, ks[hh])
            return dk[0], dk[1], dv

        zero = jnp.zeros((t, LANES), F32)
        carry = block(j, (zero, zero, zero), True)
        dk0, dk1, dv = lax.fori_loop(j + 1, nq, functools.partial(block, masked=False), carry)
        dka_ref[0] = dk0
        dka_ref[1] = dk1
        dv_ref[...] = dv.astype(dv_ref.dtype)
        pl.when((pl.program_id(0) == N_PAIRS - 1) & (j == nq - 1))(finish)

    whole_pair = pl.BlockSpec((2, s_len, LANES), lambda p, j: (p, 0, 0))
    blk_pair = pl.BlockSpec((2, t, LANES), lambda p, j: (p, j, 0))
    whole_cols = pl.BlockSpec((s_len, LANES), lambda p, j: (0, p))
    blk_cols = pl.BlockSpec((t, LANES), lambda p, j: (j, p))
    res = pl.pallas_call(
        body, name="attn_bwd", grid=(N_PAIRS, nq),
        in_specs=[whole_pair, blk_pair, blk_cols, whole_cols, whole_cols, whole_cols] + [ANY] * (n + 1),
        out_specs=[whole_pair, blk_pair, pl.BlockSpec((t, LANES), lambda p, j: (j, AL_V // LANES + p))] + [ANY] * n,
        out_shape=[jax.ShapeDtypeStruct((N_HEADS, s_len, LANES), F32), jax.ShapeDtypeStruct((N_HEADS, s_len, LANES), F32),
                   jax.ShapeDtypeStruct(dproj.shape, dproj.dtype)]
        + [jax.ShapeDtypeStruct((N_DEV, g.shape[1] // 2, g.shape[2]), g.dtype) for g in parts],
        scratch_shapes=_exchange_sems(n), input_output_aliases={6 + n: 2},
        compiler_params=_params(("arbitrary", "arbitrary")),
    )(qa, ka, vb, o, lse, do, *parts, dproj)
    return res[0], res[1], res[2], res[3:]


def _attn_post(dqa, dka, dproj):
    s_len = dqa.shape[1]
    tr = min(256, s_len)
    assert AL_K == AL_Q + D_MODEL and AL_Q % (2 * D_MODEL) == 0

    def body(dqa_ref, dka_ref, _, dqk_ref, dcum_ref):
        lane = lax.broadcasted_iota(jnp.int32, (tr, LANES), 1)
        lo = lane < HEAD_DIM
        dcum = jnp.zeros((tr, LANES), F32)
        for p in range(N_PAIRS):
            a0, a1, b0, b1 = dqa_ref[2 * p], dqa_ref[2 * p + 1], dka_ref[2 * p], dka_ref[2 * p + 1]
            dq = jnp.where(lo, a0, a1) * (HEAD_DIM ** -0.5)
            dqk_ref[:, p * LANES:(p + 1) * LANES] = dq.astype(dqk_ref.dtype)
            dqk_ref[:, D_MODEL + p * LANES:D_MODEL + (p + 1) * LANES] = jnp.where(lo, b0, b1).astype(dqk_ref.dtype)
            for hh, (a, b) in enumerate(((a0, b0), (a1, b1))):
                base = _aug_base(hh)
                dcum = dcum + jnp.where(lane == 2 * p + hh, a[:, base:base + 1] - b[:, base + 3:base + 4], 0.0)
        dcum_ref[...] = dcum

    heads = pl.BlockSpec((N_HEADS, tr, LANES), lambda i: (0, i, 0))
    return pl.pallas_call(
        body, name="attn_post", grid=(s_len // tr,),
        in_specs=[heads, heads, ANY],
        out_specs=[pl.BlockSpec((tr, 2 * D_MODEL), lambda i: (i, AL_Q // (2 * D_MODEL))), pl.BlockSpec((tr, LANES), lambda i: (i, 0))],
        out_shape=[jax.ShapeDtypeStruct(dproj.shape, dproj.dtype), jax.ShapeDtypeStruct((s_len, LANES), F32)],
        input_output_aliases={2: 0}, compiler_params=_params(("parallel",)),
    )(dqa, dka, dproj)


def _ln_stats(r):
    mu = _rowmean(r)
    xc = r - mu
    rstd = lax.rsqrt(_rowmean(xc * xc) + LN_EPS)
    return xc * rstd, rstd


def _ln_bwd(dxh, xh, rstd):
    return rstd * (dxh - _rowmean(dxh) - xh * _rowmean(dxh * xh))


def _rms_bwd(dgn, g, r):
    return r * dgn - (r * r * r) * g * _rowmean(dgn * g)


def _to_aligned(wt):
    out = jnp.zeros((AL_COLS, wt.shape[1]), wt.dtype)
    for dst, (lo, hi) in ((0, (0, 2048)), (AL_Q, (2576, 5648)), (AL_B, (2048, 2560)), (AL_DTF, (2560, 2576)), (AL_DTF + 16, (5648, 5664))):
        out = lax.dynamic_update_slice_in_dim(out, wt[lo:hi], dst, axis=0)
    return out


def _from_aligned(gt):
    out = jnp.zeros((IN_COLS, gt.shape[1]), gt.dtype)
    for dst, (lo, hi) in ((0, (0, AL_Q)), (2048, (AL_B, AL_DTF)), (2560, (AL_DTF, AL_DTF + 16)), (2576, (AL_Q, AL_B)),
                          (5648, (AL_DTF + 16, AL_DTF + 32))):
        out = lax.dynamic_update_slice_in_dim(out, gt[lo:hi], dst, axis=0)
    return out


def _lanes(v, at=0):
    return jnp.pad(v, ((0, 0), (at, LANES - at - v.shape[1])))


def _local_step(x, tgt, mod, w_alt, halves, sp):
    d = D_MODEL
    sh1, sc1, g1, sh2, sc2, g2 = [mod[:, i * d:(i + 1) * d] for i in range(6)]
    dt_bias_l, a_log_l, f_bias_l = _lanes(sp["dt_bias"]), _lanes(sp["a_log"]), _lanes(sp["f_bias"], F_LANE)
    d_exp = jnp.repeat(sp["d_skip"], HEAD_DIM, axis=1)
    z_slab = lambda a: (a, d, AL_Z // d)

    (h1,), _ = _rowwise("mod1", lambda x, sc, sh: ([x * (1.0 + sc) + sh], []), [x], [sc1, sh1], [(d, BF16)], [])
    proj = _matmul("proj", h1, w_alt, dims=NT, tn=1152)
    xc_all = _conv_fwd(proj, sp["conv_w"], sp["conv_b"])
    y_ssd, prevs = _ssd_fwd(xc_all, proj, dt_bias_l, a_log_l, d_exp)

    def gated_norm(y, z, w):
        g = y * _silu(z)
        return [g * lax.rsqrt(_rowmean(g * g) + RMS_EPS) * w], []

    (y_mix,), _ = _rowwise("ssm_norm", gated_norm, [y_ssd, z_slab(proj)], [sp["ssm_norm_w"]], [(d, BF16, (2 * d, 0, None))], [])
    cum = _fox_cum(proj, f_bias_l)
    qa, ka, vb = _attn_prep(proj, cum)
    o, lse, (g_out, g_fi, g_fo) = _attn_fwd(qa, ka, vb, halves)
    w_out = g_out.reshape(2 * d, d)
    w_fi = g_fi.transpose(1, 0, 2).reshape(d, D_FF)
    w_fo = g_fo.reshape(D_FF, d)
    (y_mix,), _ = _rowwise("attn_norm", lambda o, w: ([o * lax.rsqrt(_rowmean(o * o) + RMS_EPS) * w], []),
                           [o], [sp["attn_norm_w"]], [(d, BF16, (2 * d, 1, y_mix))], [])
    def ln1_fwd(y, x, g1, sc2, sh2, lg, lb):
        r1 = ALPHA * x + (1.0 + g1) * y
        xh, _ = _ln_stats(r1)
        x1 = xh * lg + lb
        h2 = x1 * (1.0 + sc2) + sh2
        return [y, r1, h2, h2.T], []

    def relu2(u):
        a = jnp.square(jnp.maximum(u, 0.0))
        return [a, a.T], []

    y, r1, h2, h2_t = _matmul("out_proj", y_mix, w_out, tm=512, tk=2048,
                              epi=(ln1_fwd, [x], [g1, sc2, sh2, sp["ln1_g"], sp["ln1_b"]], [F32, F32, BF16, ("T", BF16)], []))
    act, act_t = _matmul("ff_in", h2, w_fi, epi=(relu2, [], [], [BF16, ("T", BF16)], []))

    def head(ff, r1, tgt, g2, l1g, l1b, l2g, l2b):
        xh1, _ = _ln_stats(r1)
        x1 = xh1 * l1g + l1b
        xh2, rstd2 = _ln_stats(ALPHA * x1 + (1.0 + g2) * ff)
        err = xh2 * l2g + l2b - tgt
        loss = 0.5 * jnp.sum(_rowmean(err * err))
        dx2 = err * (1.0 / d)
        dr2 = _ln_bwd(dx2 * l2g, xh2, rstd2)
        return ([dr2, (1.0 + g2) * dr2],
                [_colsum(dx2 * xh2), _colsum(dx2), _colsum(dr2 * ff), jnp.full((1, LANES), loss, F32)])

    dr2, dff, d_ln2_g, d_ln2_b, d_g2, loss = _matmul(
        "ff_out", act, w_fo, tm=512, tk=2048,
        epi=(head, [r1, tgt], [g2, sp["ln1_g"], sp["ln1_b"], sp["ln2_g"], sp["ln2_b"]], [F32, BF16], [d, d, d, LANES]))
    du = _matmul("d_act", dff, w_fo, dims=NT, epi=(lambda da, act: ([da * (2.0 * jnp.sqrt(act.astype(F32)))], []), [act], [], [BF16], []))
    dw_fo = _matmul("dw_ff_out", act_t, dff, out_dtype=BF16, by_chip="rows")
    dw_fi = _matmul("dw_ff_in", h2_t, du, out_dtype=BF16, by_chip="cols")

    def ln1_bwd(dh2, r1, dr2, y, sc2, g1, lg, lb):
        xh, rstd = _ln_stats(r1)
        x1 = xh * lg + lb
        dx1 = ALPHA * dr2 + dh2 * (1.0 + sc2)
        dr1 = _ln_bwd(dx1 * lg, xh, rstd)
        return ([dr1, (1.0 + g1) * dr1],
                [_colsum(dh2 * x1), _colsum(dh2), _colsum(dx1 * xh), _colsum(dx1), _colsum(dr1 * y)])

    dr1, dy, d_sc2, d_sh2, d_ln1_g, d_ln1_b, d_g1 = _matmul(
        "dh2", du, w_fi, dims=NT, tm=512, tk=2048,
        epi=(ln1_bwd, [r1, dr2, y], [sc2, g1, sp["ln1_g"], sp["ln1_b"]], [F32, BF16], [d] * 5))
    dymix = _matmul("dy_mix", dy, w_out, dims=NT)
    dw_out = _matmul("dw_out", y_mix, dy, dims=TN, out_dtype=BF16, by_chip="rows")

    def attn_norm_bwd(o, dyo, w):
        r = lax.rsqrt(_rowmean(o * o) + RMS_EPS)
        return [_rms_bwd(dyo * w, o, r)], [_colsum(dyo * o * r)]

    (do,), (d_attn_w,) = _rowwise("attn_norm_bwd", attn_norm_bwd, [o, (dymix, d, 1)], [sp["attn_norm_w"]], [(d, F32)], [d])

    def gated_norm_bwd(y, z, dyo, w):
        sg = _sigmoid(z)
        sz = z * sg
        g = y * sz
        r = lax.rsqrt(_rowmean(g * g) + RMS_EPS)
        dg = _rms_bwd(dyo * w, g, r)
        return [dg * sz, dg * y * (sg * (1.0 + z * (1.0 - sg)))], [_colsum(dyo * g * r)]

    (dy_ssd, dproj), (d_ssm_w,) = _rowwise("ssm_norm_bwd", gated_norm_bwd, [y_ssd, z_slab(proj), (dymix, d, 0)],
                                           [sp["ssm_norm_w"]], [(d, F32), (d, BF16, (AL_COLS, AL_Z // d, None))], [d])
    dqa, dka, dproj, landed = _attn_bwd(qa, ka, vb, o, lse, do, [dw_out, dw_fi, dw_fo], dproj)
    dproj, dcum = _attn_post(dqa, dka, dproj)
    dxc, ddt_tile, d_alog_l, d_dexp, d_dtb_l = _ssd_bwd(xc_all, proj, dt_bias_l, a_log_l, d_exp, prevs, dy_ssd)
    dproj, d_fb_l = _fox_cum_bwd(dcum, proj, f_bias_l, ddt_tile, dproj)
    dpre, d_conv_w, d_conv_b = _conv_bwd_pre(proj, sp["conv_w"], sp["conv_b"], dxc)
    dproj = _conv_bwd_in(dpre, sp["conv_w"], dproj)
    dw_alt = _matmul("dw_in", dproj, h1, dims=TN, tm=1152, out_dtype=BF16)
    part_in = _from_aligned(dw_alt).reshape(N_CHIPS, IN_COLS // N_CHIPS, d)

    def last(dh1, x, dr1, sc1):
        return [ALPHA * dr1 + dh1 * (1.0 + sc1)], [_colsum(dh1 * x), _colsum(dh1)]

    chip_in = _pair_sum(part_in, _pair_exchange(part_in), lax.axis_index("c"))
    dx, d_sc1, d_sh1, landed_in = _matmul("dh1", dproj, w_alt, tm=512, tk=1152, carry=[chip_in],
                                          epi=(last, [x, dr1], [sc1], [F32], [d, d]))

    small = {
        "mod": jnp.concatenate([d_sh1, d_sc1, d_g1, d_sh2, d_sc2, d_g2], axis=1),
        "conv_w": d_conv_w, "conv_b": d_conv_b,
        "dt_bias": d_dtb_l[:, :N_HEADS], "a_log": d_alog_l[:, :N_HEADS],
        "d_skip": jnp.sum(d_dexp.reshape(N_HEADS, HEAD_DIM), axis=1)[None, :],
        "ssm_norm_w": d_ssm_w, "f_bias": d_fb_l[:, F_LANE:F_LANE + N_HEADS], "attn_norm_w": d_attn_w,
        "ln1_g": d_ln1_g, "ln1_b": d_ln1_b, "ln2_g": d_ln2_g, "ln2_b": d_ln2_b, "loss": loss,
    }
    return dx, [landed_in, *landed], small


N_DEV = 8
N_CHIPS = 4
ANY = pl.BlockSpec(memory_space=pl.ANY)
VMEM_SPEC = pl.BlockSpec(memory_space=pltpu.VMEM)


def _place():
    x, y, c = lax.axis_index("x"), lax.axis_index("y"), lax.axis_index("c")
    return x, y, c


def _other_chips(x, y):
    return [(1 - x, y, 2 * (1 - x) + y), (x, 1 - y, 2 * x + 1 - y), (1 - x, 1 - y, 2 * (1 - x) + 1 - y)]


def _small_gather(v_ref, out_ref, send_sems, recv_sems, local_sem):
    x, y, c = _place()
    me = 4 * x + 2 * y + c
    mine = pltpu.make_async_copy(v_ref, out_ref.at[me], local_sem)
    mine.start()
    peers = _peers(x, y, c)

    def copy(rel, slot, to):
        return pltpu.make_async_remote_copy(src_ref=v_ref, dst_ref=out_ref.at[slot], send_sem=send_sems.at[rel],
                                            recv_sem=recv_sems.at[rel], device_id=to, device_id_type=MESH)

    sends = [copy(rel, me, peer) for rel, peer in enumerate(peers)]
    for cp in sends:
        cp.start()
    for rel, (px, py, pc) in enumerate(peers):
        copy(rel, 4 * px + 2 * py + pc, (x, y, c)).wait_recv()
    for cp in sends:
        cp.wait_send()
    mine.wait()


SMALL_GATHER_SEMS = [pltpu.SemaphoreType.DMA((N_DEV - 1,)), pltpu.SemaphoreType.DMA((N_DEV - 1,)), pltpu.SemaphoreType.DMA]


def _allgather_small(name, v, with_sum=False):
    def body(v_ref, out_ref, *rest):
        _small_gather(v_ref, out_ref, *rest[-3:])
        if with_sum:
            acc = out_ref[0]
            for dev in range(1, N_DEV):
                acc = acc + out_ref[dev]
            rest[0][...] = acc

    every = jax.ShapeDtypeStruct((N_DEV, *v.shape), v.dtype)
    return pl.pallas_call(
        body, name=name, out_shape=[every, jax.ShapeDtypeStruct(v.shape, v.dtype)] if with_sum else every,
        in_specs=[VMEM_SPEC], out_specs=[VMEM_SPEC] * 2 if with_sum else VMEM_SPEC, scratch_shapes=SMALL_GATHER_SEMS,
    )(v)


def _gather_shards(shard):
    def body(in_ref, out_ref, stage, send_sems, recv_sems, local_sems):
        start, finish = _shard_gather_plan(in_ref, out_ref, stage, send_sems, recv_sems, local_sems)
        start()
        finish()

    return pl.pallas_call(
        body, name="gather_w_in", out_shape=jax.ShapeDtypeStruct((N_CHIPS, *shard.shape), shard.dtype),
        in_specs=[ANY], out_specs=ANY,
        scratch_shapes=[pltpu.VMEM(shard.shape, shard.dtype), pltpu.SemaphoreType.DMA((6,)), pltpu.SemaphoreType.DMA((6,)),
                        pltpu.SemaphoreType.DMA((2,))],
        compiler_params=_params(),
    )(shard)


def _shard_gather_plan(in_ref, out_ref, stage, send_sems, recv_sems, local_sems):
    ch = in_ref.shape[1] // 2
    x, y, c = _place()
    k_me = 2 * x + y
    me, sibling = (x, y, c), (x, y, 1 - c)
    chips = _other_chips(x, y)

    def copy(idx, k, half, to, src=None):
        cols = out_ref.at[k, :, pl.ds(pl.multiple_of(half * ch, ch), ch)]
        return pltpu.make_async_remote_copy(src_ref=cols if src is None else src, dst_ref=cols, send_sem=send_sems.at[idx],
                                            recv_sem=recv_sems.at[idx], device_id=to, device_id_type=MESH)

    mine = in_ref.at[:, pl.ds(pl.multiple_of(c * ch, ch), ch)]
    sends = [copy(j, k_me, c, (cx, cy, c), src=mine) for j, (cx, cy, _) in enumerate(chips)]
    load = pltpu.make_async_copy(in_ref, stage, local_sems.at[0])
    store = pltpu.make_async_copy(stage, out_ref.at[k_me], local_sems.at[1])

    def start():
        for cp in sends:
            cp.start()
        load.start()

    def finish():
        load.wait()
        store.start()
        forwards = []
        for j, (_, _, kj) in enumerate(chips):
            copy(j, kj, c, me).wait_recv()
            forwards.append(copy(3 + j, kj, c, sibling))
            forwards[-1].start()
        for j, (_, _, kj) in enumerate(chips):
            copy(3 + j, kj, 1 - c, me).wait_recv()
        for cp in sends + forwards:
            cp.wait_send()
        store.wait()

    return start, finish


def _peers(x, y, c):
    return [((1 - x) if rel & 4 else x, (1 - y) if rel & 2 else y, (1 - c) if rel & 1 else c) for rel in range(1, N_DEV)]


def _exchange_sems(n):
    return [pltpu.SemaphoreType.DMA((n, N_DEV - 1)), pltpu.SemaphoreType.DMA((n, N_DEV - 1)), pltpu.SemaphoreType.DMA((n,))]


def _gather_plan(ins, outs, send_sems, recv_sems, local_sems):
    x, y, c = _place()
    k_me = 2 * x + y
    peers = [(rel, p) for rel, p in enumerate(_peers(x, y, c)) if (rel + 1) & 6]

    def copy(w, rel, k, half, to, src=None):
        rh = ins[w].shape[0] // 2
        rows = outs[w].at[k, pl.ds(pl.multiple_of(half * rh, rh), rh), :]
        return pltpu.make_async_remote_copy(src_ref=rows if src is None else src, dst_ref=rows, send_sem=send_sems.at[w, rel],
                                            recv_sem=recv_sems.at[w, rel], device_id=to, device_id_type=MESH)

    def mine(w):
        rh = ins[w].shape[0] // 2
        return ins[w].at[pl.ds(pl.multiple_of(c * rh, rh), rh), :]

    n = len(ins)
    local = [pltpu.make_async_copy(ins[w], outs[w].at[k_me], local_sems.at[w]) for w in range(n)]
    sends = [copy(w, rel, k_me, c, peer, src=mine(w)) for w in range(n) for rel, peer in peers]

    def start():
        for cp in local + sends:
            cp.start()

    def finish():
        for w in range(n):
            for rel, (px, py, pc) in peers:
                copy(w, rel, 2 * px + py, pc, (x, y, c)).wait_recv()
        for cp in sends:
            cp.wait_send()
        for cp in local:
            cp.wait()

    return start, finish


def _reduce_plan(ins, outs, send_sems, recv_sems, local_sems):
    x, y, c = _place()
    me = 4 * x + 2 * y + c
    peers = _peers(x, y, c)

    def block(w, k, half):
        rh = ins[w].shape[1] // 2
        return ins[w].at[k, pl.ds(pl.multiple_of(half * rh, rh), rh), :]

    def copy(w, rel, src, slot, to):
        return pltpu.make_async_remote_copy(src_ref=src, dst_ref=outs[w].at[slot], send_sem=send_sems.at[w, rel],
                                            recv_sem=recv_sems.at[w, rel], device_id=to, device_id_type=MESH)

    n = len(ins)
    local = [pltpu.make_async_copy(block(w, 2 * x + y, c), outs[w].at[me], local_sems.at[w]) for w in range(n)]
    sends = [copy(w, rel, block(w, 2 * px + py, pc), me, (px, py, pc)) for w in range(n) for rel, (px, py, pc) in enumerate(peers)]

    def start():
        for cp in local + sends:
            cp.start()

    def finish():
        for w in range(n):
            for rel, (px, py, pc) in enumerate(peers):
                copy(w, rel, block(w, 2 * x + y, c), 4 * px + 2 * py + pc, (x, y, c)).wait_recv()
        for cp in sends:
            cp.wait_send()
        for cp in local:
            cp.wait()

    return start, finish


def _scatter_plan(ins, outs, send_sems, recv_sems, local_sems):
    x, y, c = _place()
    k_me = 2 * x + y
    chips = _other_chips(x, y)

    def copy(w, j, src_k, dst_k, to):
        return pltpu.make_async_remote_copy(src_ref=ins[w].at[src_k], dst_ref=outs[w].at[dst_k], send_sem=send_sems.at[w, j],
                                            recv_sem=recv_sems.at[w, j], device_id=to, device_id_type=MESH)

    n = len(ins)
    local = [pltpu.make_async_copy(ins[w].at[k_me], outs[w].at[k_me], local_sems.at[w]) for w in range(n)]
    sends = [copy(w, j, kj, k_me, (cx, cy, c)) for w in range(n) for j, (cx, cy, kj) in enumerate(chips)]

    def start():
        for cp in local + sends:
            cp.start()

    def finish():
        for w in range(n):
            for j, (_, _, kj) in enumerate(chips):
                copy(w, j, k_me, kj, (x, y, c)).wait_recv()
        for cp in sends:
            cp.wait_send()
        for cp in local:
            cp.wait()

    return start, finish


def _row_tile(r, mult=2 * SUBLANES):
    if r % 256 == 0:
        return 256
    return max([t for t in range(mult, 513, mult) if r % t == 0], default=r)


def _pair_exchange(g):
    _, r, cdim = g.shape
    ch = cdim // 2

    def body(g_ref, got_ref, send_sem, recv_sem):
        x, y, c = _place()
        cp = pltpu.make_async_remote_copy(src_ref=g_ref.at[:, :, pl.ds(pl.multiple_of((1 - c) * ch, ch), ch)], dst_ref=got_ref,
                                          send_sem=send_sem, recv_sem=recv_sem, device_id=(x, y, 1 - c), device_id_type=MESH)
        cp.start()
        cp.wait_recv()
        cp.wait_send()

    return pl.pallas_call(
        body, name="pair_exchange", out_shape=jax.ShapeDtypeStruct((N_CHIPS, r, ch), g.dtype),
        in_specs=[ANY], out_specs=ANY, scratch_shapes=[pltpu.SemaphoreType.DMA, pltpu.SemaphoreType.DMA],
    )(g)


def _pair_sum(g, got, c):
    _, r, cdim = g.shape
    ch = cdim // 2
    tr = _row_tile(r)

    def body(c_ref, g_ref, got_ref, o_ref):
        o_ref[...] = (g_ref[...].astype(F32) + got_ref[...].astype(F32)).astype(o_ref.dtype)

    blk = pl.BlockSpec((1, tr, ch), lambda k, i, c_ref: (k, i, 0))
    return pl.pallas_call(
        body, name="pair_sum",
        grid_spec=pltpu.PrefetchScalarGridSpec(
            num_scalar_prefetch=1, grid=(N_CHIPS, r // tr),
            in_specs=[pl.BlockSpec((1, tr, ch), lambda k, i, c_ref: (k, i, c_ref[0])), blk], out_specs=blk),
        out_shape=jax.ShapeDtypeStruct((N_CHIPS, r, ch), BF16),
        compiler_params=_params(("parallel", "parallel")),
    )(jnp.reshape(c, (1,)).astype(jnp.int32), g, got)


def _sum_blocks(name, parts):
    k, r, cdim = parts.shape
    tr = _row_tile(r)

    def body(p_ref, o_ref):
        acc = p_ref[0].astype(F32)
        for i in range(1, k):
            acc = acc + p_ref[i].astype(F32)
        o_ref[...] = acc

    return pl.pallas_call(
        body, name=name, grid=(r // tr,),
        in_specs=[pl.BlockSpec((k, tr, cdim), lambda i: (0, i, 0))], out_specs=pl.BlockSpec((tr, cdim), lambda i: (i, 0)),
        out_shape=jax.ShapeDtypeStruct((r, cdim), F32), compiler_params=_params(("parallel",)),
    )(parts)


def _pair_swap(halves):
    n = len(halves)

    def body(*refs):
        ins, outs = refs[:n], refs[n:2 * n]
        send_sems, recv_sems = refs[2 * n:]
        x, y, c = _place()
        cps = [pltpu.make_async_remote_copy(src_ref=ins[w], dst_ref=outs[w], send_sem=send_sems.at[w], recv_sem=recv_sems.at[w],
                                            device_id=(x, y, 1 - c), device_id_type=MESH) for w in range(n)]
        for cp in cps:
            cp.start()
        for cp in cps:
            cp.wait_recv()
        for cp in cps:
            cp.wait_send()

    return pl.pallas_call(
        body, name="pair_swap", out_shape=[jax.ShapeDtypeStruct(h.shape, h.dtype) for h in halves],
        in_specs=[ANY] * n, out_specs=[ANY] * n,
        scratch_shapes=[pltpu.SemaphoreType.DMA((n,)), pltpu.SemaphoreType.DMA((n,))],
    )(*halves)


ADA_SHARD = 6 * D_MODEL // N_CHIPS


def _mod_part(c_all, w_shard, b_shard):
    tn = 512

    def body(c_ref, w_ref, b_ref, o_ref):
        o_ref[...] = _dot(_silu(c_ref[...]).astype(BF16), w_ref[...].astype(BF16)) + b_ref[...]

    return pl.pallas_call(
        body, name="mod_part", grid=(ADA_SHARD // tn,),
        in_specs=[pl.BlockSpec((N_DEV, D_MODEL), lambda j: (0, 0)), pl.BlockSpec((D_MODEL, tn), lambda j: (0, j)),
                  pl.BlockSpec((1, tn), lambda j: (0, j))],
        out_specs=pl.BlockSpec((N_DEV, tn), lambda j: (0, j)),
        out_shape=jax.ShapeDtypeStruct((N_DEV, ADA_SHARD), F32), compiler_params=_params(("parallel",)),
    )(c_all, w_shard, b_shard)


def _w_ada_grad(c_all_t, dmod_shard):
    tm = 256

    def body(ct_ref, dm_ref, o_ref):
        act = _silu(ct_ref[...])
        acc = act[:, 0:1] * dm_ref[0:1, :]
        for dev in range(1, N_DEV):
            acc = acc + act[:, dev:dev + 1] * dm_ref[dev:dev + 1, :]
        o_ref[...] = acc

    return pl.pallas_call(
        body, name="w_ada_grad", grid=(D_MODEL // tm,),
        in_specs=[pl.BlockSpec((tm, N_DEV), lambda i: (i, 0)), pl.BlockSpec((N_DEV, ADA_SHARD), lambda i: (0, 0))],
        out_specs=pl.BlockSpec((tm, ADA_SHARD), lambda i: (i, 0)),
        out_shape=jax.ShapeDtypeStruct((D_MODEL, ADA_SHARD), F32), compiler_params=_params(("parallel",)),
    )(c_all_t, dmod_shard)


def _adamw_math(w, g, m, v):
    nm = ADAM_B1 * m + (1.0 - ADAM_B1) * g
    nv = ADAM_B2 * v + (1.0 - ADAM_B2) * jnp.square(g)
    m_hat = nm / (1.0 - ADAM_B1 ** ADAM_STEP)
    v_hat = nv / (1.0 - ADAM_B2 ** ADAM_STEP)
    return -ADAM_LR * (m_hat / (jnp.sqrt(v_hat) + ADAM_EPS) + ADAM_WD * w), nm, nv


def _adamw(name, w, g, m, v):
    _, r, cdim = w.shape
    tr = 256 if r % 256 == 0 else r

    def body(w_ref, g_ref, m_ref, v_ref, go_ref, d_ref, nm_ref, nv_ref):
        go_ref[...] = g_ref[...]
        d_ref[...], nm_ref[...], nv_ref[...] = _adamw_math(w_ref[...], g_ref[...], m_ref[...], v_ref[...])

    blk = pl.BlockSpec((None, tr, cdim), lambda i: (0, i, 0))
    return pl.pallas_call(
        body, name=name, grid=(r // tr,), in_specs=[blk, pl.BlockSpec((tr, cdim), lambda i: (i, 0)), blk, blk], out_specs=[blk] * 4,
        out_shape=[jax.ShapeDtypeStruct((1, r, cdim), F32)] * 4, compiler_params=_params(("parallel",)),
    )(w, g, m, v)


def _adamw_pair(name, w, mine, other, m, v, c, by_cols=False):
    _, r, cdim = w.shape
    hr, hc = mine.shape
    tr = _row_tile(hr, SUBLANES)
    per = hr // tr

    def body(c_ref, w_ref, a_ref, b_ref, m_ref, v_ref, g_ref, d_ref, nm_ref, nv_ref):
        half = pl.program_id(1) if by_cols else pl.program_id(0) // per
        g = jnp.where(half == c_ref[0], a_ref[...], b_ref[...])
        g_ref[...] = g
        d_ref[...], nm_ref[...], nv_ref[...] = _adamw_math(w_ref[...], g, m_ref[...], v_ref[...])

    blk = pl.BlockSpec((None, tr, hc), lambda i, j, c_ref: (0, i, j))
    half = pl.BlockSpec((tr, hc), lambda i, j, c_ref: (i % per, 0))
    return pl.pallas_call(
        body, name=name,
        grid_spec=pltpu.PrefetchScalarGridSpec(num_scalar_prefetch=1, grid=(r // tr, cdim // hc),
                                               in_specs=[blk, half, half, blk, blk], out_specs=[blk] * 4),
        out_shape=[jax.ShapeDtypeStruct((1, r, cdim), F32)] * 4, compiler_params=_params(("parallel", "parallel")),
    )(jnp.reshape(c, (1,)).astype(jnp.int32), w, mine, other, m, v)


SMALL = ["b_ada", "conv_b", "dt_bias", "a_log", "d_skip", "ssm_norm_w", "f_bias", "attn_norm_w", "ln1_g", "ln1_b", "ln2_g", "ln2_b"]


def _pack(vs):
    pieces = []
    for v in vs:
        pieces.append(v)
        if v.shape[1] % LANES:
            pieces.append(jnp.zeros((1, -v.shape[1] % LANES), v.dtype))
    return jnp.concatenate(pieces, axis=1)


def _adamw_small(total, offs, ws, ms, vs):
    n = len(ws)

    def body(*refs):
        t_ref, outs = refs[0], refs[1 + 3 * n:]
        for i in range(n):
            g = t_ref[:, offs[i]:offs[i] + ws[i].shape[1]]
            dl, nm, nv = _adamw_math(refs[1 + i][...], g, refs[1 + n + i][...], refs[1 + 2 * n + i][...])
            outs[4 * i][...], outs[4 * i + 1][...], outs[4 * i + 2][...], outs[4 * i + 3][...] = g, dl, nm, nv

    res = pl.pallas_call(
        body, name="adamw_small", in_specs=[VMEM_SPEC] * (1 + 3 * n), out_specs=[VMEM_SPEC] * (4 * n),
        out_shape=[jax.ShapeDtypeStruct(w.shape, F32) for w in ws for _ in range(4)],
    )(total, *ws, *ms, *vs)
    return [res[4 * i:4 * i + 4] for i in range(n)]


def kernel(x, c, w_ada, b_ada, w_in, conv_w, conv_b, dt_bias, a_log, d_skip, ssm_norm_w, f_bias, attn_norm_w, w_out, ln1_g, ln1_b, w_ff_in, w_ff_out, ln2_g, ln2_b, loss_target, m_w_ada, m_b_ada, m_w_in, m_conv_w, m_conv_b, m_dt_bias, m_a_log, m_d_skip, m_ssm_norm_w, m_f_bias, m_attn_norm_w, m_w_out, m_ln1_g, m_ln1_b, m_w_ff_in, m_w_ff_out, m_ln2_g, m_ln2_b, v_w_ada, v_b_ada, v_w_in, v_conv_w, v_conv_b, v_dt_bias, v_a_log, v_d_skip, v_ssm_norm_w, v_f_bias, v_attn_norm_w, v_w_out, v_ln1_g, v_ln1_b, v_w_ff_in, v_w_ff_out, v_ln2_g, v_ln2_b):
    a = dict(b_ada=b_ada, conv_b=conv_b, dt_bias=dt_bias, a_log=a_log, d_skip=d_skip, ssm_norm_w=ssm_norm_w, f_bias=f_bias,
             attn_norm_w=attn_norm_w, ln1_g=ln1_g, ln1_b=ln1_b, ln2_g=ln2_g, ln2_b=ln2_b)
    ms = dict(b_ada=m_b_ada, conv_b=m_conv_b, dt_bias=m_dt_bias, a_log=m_a_log, d_skip=m_d_skip, ssm_norm_w=m_ssm_norm_w,
              f_bias=m_f_bias, attn_norm_w=m_attn_norm_w, ln1_g=m_ln1_g, ln1_b=m_ln1_b, ln2_g=m_ln2_g, ln2_b=m_ln2_b)
    vs = dict(b_ada=v_b_ada, conv_b=v_conv_b, dt_bias=v_dt_bias, a_log=v_a_log, d_skip=v_d_skip, ssm_norm_w=v_ssm_norm_w,
              f_bias=v_f_bias, attn_norm_w=v_attn_norm_w, ln1_g=v_ln1_g, ln1_b=v_ln1_b, ln2_g=v_ln2_g, ln2_b=v_ln2_b)
    xi, yi, ci = _place()
    chip = 2 * xi + yi
    me = 4 * xi + 2 * yi + ci
    d = D_MODEL
    conv_shard = CONV_DIM // N_CHIPS

    first = _allgather_small("gather_c", jnp.concatenate([c, conv_w[0].reshape(1, CONV_W * conv_shard)], axis=1))[:, 0]
    c_all = first[:, :d]
    conv_w_full = first[::2, d:].reshape(N_CHIPS, CONV_W, conv_shard).transpose(1, 0, 2).reshape(CONV_W, CONV_DIM)
    b_shard = lax.dynamic_slice_in_dim(b_ada, chip * ADA_SHARD, ADA_SHARD, axis=1)
    parts = _allgather_small("gather_mod", _mod_part(c_all, w_ada[0], b_shard))
    mod = lax.dynamic_index_in_dim(parts[::2], me, axis=1, keepdims=False).reshape(1, 6 * d)

    w_in_t, m_w_in_t, v_w_in_t = [jnp.transpose(t, (0, 2, 1)) for t in (w_in, m_w_in, v_w_in)]
    w_alt = _to_aligned(_gather_shards(w_in_t[0].astype(BF16)).reshape(IN_COLS, d))

    sp = {n: a[n] for n in SMALL[1:]}
    sp["conv_w"] = conv_w_full
    shards = [w_out[0].astype(BF16), w_ff_in[0].astype(BF16), w_ff_out[0].astype(BF16)]
    dx, landed, small = _local_step(x[0], loss_target[0], mod, w_alt, shards, sp)

    names = ["mod"] + SMALL[1:]
    vec = _pack([small[n] for n in names] + [small["conv_w"].reshape(1, CONV_W * CONV_DIM), small["loss"]])
    every, total = _allgather_small("gather_small", vec, with_sum=True)
    widths = [6 * d] + [a[n].shape[1] for n in SMALL[1:]]
    offs = [0]
    for w in widths:
        offs.append(offs[-1] + w + (-w % LANES))
    g_conv_w_full = total[:, offs[-1]:offs[-1] + CONV_W * CONV_DIM].reshape(CONV_W, CONV_DIM)
    loss = total[0, offs[-1] + CONV_W * CONV_DIM]
    dmod_shard = lax.dynamic_slice_in_dim(every[:, 0, :6 * d], chip * ADA_SHARD, ADA_SHARD, axis=1)
    g_w_ada = _w_ada_grad(c_all.T, dmod_shard)
    g_conv_w = lax.dynamic_slice_in_dim(g_conv_w_full, chip * conv_shard, conv_shard, axis=1)

    mine = [_sum_blocks("dev_sum_%d" % i, p) for i, p in enumerate(landed)]
    other = _pair_swap(mine)

    grads, deltas, new_m, new_v = {}, {}, {}, {}
    paired = dict(w_in=(w_in_t, m_w_in_t, v_w_in_t), w_out=(w_out, m_w_out, v_w_out), w_ff_in=(w_ff_in, m_w_ff_in, v_w_ff_in),
                  w_ff_out=(w_ff_out, m_w_ff_out, v_w_ff_out))
    for i, (n, (w, m, v)) in enumerate(paired.items()):
        res = _adamw_pair("adamw_" + n, w, mine[i], other[i], m, v, ci, by_cols=n == "w_in")
        grads[n], deltas[n], new_m[n], new_v[n] = [jnp.transpose(t, (0, 2, 1)) for t in res] if n == "w_in" else res
    for n, g, (w, m, v) in (("w_ada", g_w_ada, (w_ada, m_w_ada, v_w_ada)), ("conv_w", g_conv_w, (conv_w, m_conv_w, v_conv_w))):
        grads[n], deltas[n], new_m[n], new_v[n] = _adamw("adamw_" + n, w, g, m, v)
    for n, res in zip(SMALL, _adamw_small(total, offs, [a[n] for n in SMALL], [ms[n] for n in SMALL], [vs[n] for n in SMALL])):
        grads[n], deltas[n], new_m[n], new_v[n] = res

    order = ["w_ada", "b_ada", "w_in", "conv_w", "conv_b", "dt_bias", "a_log", "d_skip", "ssm_norm_w", "f_bias", "attn_norm_w", "w_out",
             "ln1_g", "ln1_b", "w_ff_in", "w_ff_out", "ln2_g", "ln2_b"]
    return (loss, dx[None], *[grads[n] for n in order], *[deltas[n] for n in order], *[new_m[n] for n in order], *[new_v[n] for n in order])
```

```python
import functools

import jax
import jax.numpy as jnp
from jax import lax
from jax.experimental import pallas as pl
from jax.experimental.pallas import tpu as pltpu

F32, BF16 = jnp.float32, jnp.bfloat16

D_MODEL = 1024
N_HEADS = 16
HEAD_DIM = 64
N_PAIRS = N_HEADS // 2
SSM_GROUPS = 2
SSM_STATE = 128
CHUNK = 128
CONV_W = 4
CONV_DIM = 1536
D_FF = 4096
IN_COLS = 5664
ALPHA = 2.0 ** 0.25
LN_EPS = 1e-5
RMS_EPS = 1e-5
LANES = 128
SUBLANES = 8

AL_Z, AL_XS, AL_Q, AL_K, AL_V, AL_B, AL_C, AL_DTF = 0, 1024, 2048, 3072, 4096, 5120, 5376, 5632
AL_COLS = 5760
F_LANE = 16

ADAM_LR, ADAM_B1, ADAM_B2, ADAM_EPS, ADAM_WD, ADAM_STEP = 0.001, 0.9, 0.999, 1e-08, 0.01, 10

VMEM_LIMIT = 56 * 1024 * 1024
SEQ_TK = 4096
MESH = pl.DeviceIdType.MESH


def _params(sem=None):
    return pltpu.CompilerParams(dimension_semantics=sem, vmem_limit_bytes=VMEM_LIMIT)


def _sigmoid(x):
    return 1.0 / (1.0 + jnp.exp(-x))


def _silu(x):
    return x * _sigmoid(x)


def _softplus(x):
    return jnp.maximum(x, 0.0) + jnp.log(1.0 + jnp.exp(-jnp.abs(x)))


def _split3(a):
    hi = a.astype(BF16)
    r = a - hi.astype(F32)
    mid = r.astype(BF16)
    lo = (r - mid.astype(F32)).astype(BF16)
    return hi, mid, lo


def _dot(a, b, dims=((1,), (0,))):
    return lax.dot_general(a, b, (dims, ((), ())), preferred_element_type=F32)


NN, NT, TN = ((1,), (0,)), ((1,), (1,)), ((0,), (0,))


def _dot3(t, a):
    hi, mid, lo = _split3(a)
    return _dot(t, hi) + _dot(t, mid) + _dot(t, lo)


def _matmul(name, a, b, *, dims=NN, out_dtype=F32, tm=1024, tn=1024, tk=1024, by_chip=None, epi=None, carry=()):
    if dims == NN:
        (m, k), n = a.shape, b.shape[1]
    elif dims == NT:
        (m, k), n = a.shape, b.shape[0]
    else:
        (k, m), n = a.shape, b.shape[1]
    if by_chip == "rows":
        tm = min(tm, m // 4)
    if by_chip == "cols":
        tn = min(tn, n // 4)
    tm, tn, tk = min(tm, m), min(tn, n), min(tk, k)
    assert m % tm == 0 and n % tn == 0 and k % tk == 0, (name, m, n, k, tm, tn, tk)
    nk = k // tk
    if by_chip == "rows":
        per = m // 4 // tm
        out_spec = pl.BlockSpec((None, tm, tn), lambda i, j, l: (i // per, i % per, j))
        out_shape = jax.ShapeDtypeStruct((4, m // 4, n), out_dtype)
    elif by_chip == "cols":
        per = n // 4 // tn
        out_spec = pl.BlockSpec((None, tm, tn), lambda i, j, l: (j // per, i, j % per))
        out_shape = jax.ShapeDtypeStruct((4, m, n // 4), out_dtype)
    else:
        out_spec = pl.BlockSpec((tm, tn), lambda i, j, l: (i, j))
        out_shape = jax.ShapeDtypeStruct((m, n), out_dtype)
    a_spec = pl.BlockSpec((tk, tm), lambda i, j, l: (l, i)) if dims == TN else pl.BlockSpec((tm, tk), lambda i, j, l: (i, l))
    b_spec = pl.BlockSpec((tn, tk), lambda i, j, l: (j, l)) if dims == NT else pl.BlockSpec((tk, tn), lambda i, j, l: (l, j))

    tile = pl.BlockSpec((tm, tn), lambda i, j, l: (i, j))
    in_specs, args, out_specs, out_shape = [a_spec, b_spec], [a, b], [out_spec], [out_shape]
    fn, n_tiles, n_sums = None, 1, 0
    if epi is not None:
        fn, fulls, vecs, outs, sums = epi
        assert by_chip is None and (not sums or n == tn), name
        in_specs = in_specs + [tile] * len(fulls) + [pl.BlockSpec((1, tn), lambda i, j, l: (0, j))] * len(vecs)
        args = args + list(fulls) + list(vecs)
        flipped = pl.BlockSpec((tn, tm), lambda i, j, l: (j, i))
        out_specs = [flipped if isinstance(dt, tuple) else tile for dt in outs] + [pl.BlockSpec((1, w), lambda i, j, l: (0, 0)) for w in sums]
        out_shape = [jax.ShapeDtypeStruct((n, m), dt[1]) if isinstance(dt, tuple) else jax.ShapeDtypeStruct((m, n), dt) for dt in outs]
        out_shape += [jax.ShapeDtypeStruct((1, w), F32) for w in sums]
        n_tiles, n_sums = len(outs), len(sums)
    n_in, n_out, n_c = len(args), len(out_specs), len(carry)
    scratch = [pltpu.VMEM((tm, tn) if nk > 1 else (SUBLANES, LANES), F32)]
    if n_c:
        in_specs, args = in_specs + [ANY] * n_c, args + list(carry)
        out_specs = out_specs + [ANY] * n_c
        out_shape = out_shape + [jax.ShapeDtypeStruct(g.shape, g.dtype) for g in carry]
        scratch = scratch + _exchange_sems(n_c)
    gm, gn = m // tm, n // tn

    def body(*refs):
        a_ref, b_ref = refs[:2]
        ins, outs = refs[2:n_in], refs[n_in + n_c:n_in + n_c + n_out]
        acc_ref = refs[n_in + 2 * n_c + n_out]
        i, j, l = pl.program_id(0), pl.program_id(1), pl.program_id(2)
        if n_c:
            start, wait = _scatter_plan(refs[n_in:n_in + n_c], refs[n_in + n_c + n_out:n_in + 2 * n_c + n_out], *refs[n_in + 2 * n_c + n_out + 1:])
            pl.when((i == 0) & (j == 0) & (l == 0))(start)
        part = _dot(a_ref[...].astype(BF16), b_ref[...].astype(BF16), dims)

        def finish(res):
            if fn is None:
                outs[0][...] = res.astype(outs[0].dtype)
                return
            tiles, colsums = fn(res, *[r[...] for r in ins])
            for r, val in zip(outs[:n_tiles], tiles):
                r[...] = val.astype(r.dtype)
            if n_sums:
                @pl.when(i == 0)
                def _():
                    for r in outs[n_tiles:]:
                        r[...] = jnp.zeros_like(r)
                for r, val in zip(outs[n_tiles:], colsums):
                    r[...] += val

        if nk == 1:
            finish(part)
        else:
            @pl.when(l == 0)
            def _():
                acc_ref[...] = part

            @pl.when((l > 0) & (l < nk - 1))
            def _():
                acc_ref[...] += part

            @pl.when(l == nk - 1)
            def _():
                finish(acc_ref[...] + part)

        if n_c:
            pl.when((i == gm - 1) & (j == gn - 1) & (l == nk - 1))(wait)

    res = pl.pallas_call(
        body, name=name, grid=(gm, gn, nk),
        in_specs=in_specs, out_specs=out_specs, out_shape=out_shape, scratch_shapes=scratch,
        compiler_params=_params(("arbitrary",) * 3 if n_c or n_sums else ("parallel", "parallel", "arbitrary")),
    )(*args)
    return res[0] if len(res) == 1 else res


def _rowwise(name, fn, fulls, vecs, out_fulls, out_vecs, tr=256):
    fulls = [f if isinstance(f, tuple) else (f, f.shape[1], 0) for f in fulls]
    s = fulls[0][0].shape[0]
    tr = min(tr, s)
    out_fulls = [o if len(o) == 3 else (*o, (o[0], 0, None)) for o in out_fulls]
    into = [(k, slab[2]) for k, (_, _, slab) in enumerate(out_fulls) if slab[2] is not None]
    nf, nv, nof, nov = len(fulls), len(vecs), len(out_fulls), len(out_vecs)
    in_specs = [pl.BlockSpec((tr, w), functools.partial(lambda i, cb: (i, cb), cb=cb)) for (_, w, cb) in fulls]
    in_specs += [pl.BlockSpec(v.shape, lambda i: (0, 0)) for v in vecs] + [ANY] * len(into)
    out_shape = [jax.ShapeDtypeStruct((s, slab[0]), dt) for (_, dt, slab) in out_fulls] + [jax.ShapeDtypeStruct((1, w), F32) for w in out_vecs]
    out_specs = [pl.BlockSpec((tr, w), functools.partial(lambda i, cb: (i, cb), cb=slab[1])) for (w, _, slab) in out_fulls]
    out_specs += [pl.BlockSpec((1, w), lambda i: (0, 0)) for w in out_vecs]

    def body(*refs):
        outs = refs[nf + nv + len(into):]
        of, ov = fn(*[r[...] for r in refs[:nf + nv]])
        for r, val in zip(outs[:nof], of):
            r[...] = val.astype(r.dtype)
        if nov:
            @pl.when(pl.program_id(0) == 0)
            def _():
                for r in outs[nof:]:
                    r[...] = jnp.zeros_like(r)
            for r, val in zip(outs[nof:], ov):
                r[...] += val

    res = pl.pallas_call(
        body, name=name, grid=(s // tr,), in_specs=in_specs, out_specs=out_specs, out_shape=out_shape,
        input_output_aliases={nf + nv + pos: k for pos, (k, _) in enumerate(into)},
        compiler_params=_params(("arbitrary",)),
    )(*[f[0] for f in fulls], *vecs, *[buf for _, buf in into])
    return res[:nof], res[nof:]


def _colsum(x):
    return jnp.sum(x, axis=0, keepdims=True)


def _rowmean(x):
    return jnp.mean(x, axis=-1, keepdims=True)


CONV_CB = 512
CONV_TR = 512


def _shift_down(u, halo, j):
    if j == 0:
        return u
    ru = pltpu.roll(u, j, 0)
    row8 = lax.broadcasted_iota(jnp.int32, halo.shape, 0)
    top = jnp.where(row8 < j, pltpu.roll(halo, j, 0), ru[:SUBLANES])
    return jnp.concatenate([top, ru[SUBLANES:]], axis=0)


def _shift_up(d, halo, j):
    if j == 0:
        return d
    tr = d.shape[0]
    rd = pltpu.roll(d, tr - j, 0)
    row8 = lax.broadcasted_iota(jnp.int32, halo.shape, 0)
    bot = jnp.where(row8 >= SUBLANES - j, pltpu.roll(halo, SUBLANES - j, 0), rd[tr - SUBLANES:])
    return jnp.concatenate([rd[:tr - SUBLANES], bot], axis=0)


def _conv_col(cb):
    return jnp.where(cb < 2, AL_XS // CONV_CB + cb, AL_B // CONV_CB)


def _conv_specs(s, tr):
    per8 = tr // SUBLANES
    blk = pl.BlockSpec((tr, CONV_CB), lambda cb, i: (i, _conv_col(cb)))
    prev = pl.BlockSpec((SUBLANES, CONV_CB), lambda cb, i: (jnp.maximum(i * per8 - 1, 0), _conv_col(cb)))
    return blk, prev


def _conv_pre(u, halo, w_ref, b_ref, first):
    halo = jnp.where(first, 0.0, halo)
    acc = b_ref[...] + w_ref[CONV_W - 1:CONV_W, :] * u
    shifted = [u]
    for j in range(1, CONV_W):
        sh = _shift_down(u, halo, j)
        shifted.append(sh)
        acc = acc + w_ref[CONV_W - 1 - j:CONV_W - j, :] * sh
    return acc, shifted


def _conv_fwd(proj, conv_w, conv_b):
    s = proj.shape[0]
    tr = min(CONV_TR, s)
    blk, prev = _conv_specs(s, tr)

    def body(u_ref, h_ref, w_ref, b_ref, o_ref):
        pre, _ = _conv_pre(u_ref[...], h_ref[...], w_ref, b_ref, pl.program_id(1) == 0)
        o_ref[...] = _silu(pre)

    return pl.pallas_call(
        body, name="conv_fwd", grid=(CONV_DIM // CONV_CB, s // tr),
        in_specs=[blk, prev, pl.BlockSpec((CONV_W, CONV_CB), lambda cb, i: (0, cb)), pl.BlockSpec((1, CONV_CB), lambda cb, i: (0, cb))],
        out_specs=pl.BlockSpec((tr, CONV_CB), lambda cb, i: (i, cb)),
        out_shape=jax.ShapeDtypeStruct((s, CONV_DIM), F32),
        compiler_params=_params(("parallel", "parallel")),
    )(proj, proj, conv_w, conv_b)


def _conv_bwd_pre(proj, conv_w, conv_b, dxc):
    s = proj.shape[0]
    tr = min(CONV_TR, s)
    blk, prev = _conv_specs(s, tr)

    def body(u_ref, h_ref, w_ref, b_ref, d_ref, dpre_ref, dw_ref, db_ref):
        i = pl.program_id(1)
        pre, shifted = _conv_pre(u_ref[...], h_ref[...], w_ref, b_ref, i == 0)
        sg = _sigmoid(pre)
        dpre = d_ref[...] * (sg * (1.0 + pre * (1.0 - sg)))
        dpre_ref[...] = dpre

        @pl.when(i == 0)
        def _():
            dw_ref[...] = jnp.zeros_like(dw_ref)
            db_ref[...] = jnp.zeros_like(db_ref)

        db_ref[...] += _colsum(dpre)
        for j in range(CONV_W):
            dw_ref[CONV_W - 1 - j:CONV_W - j, :] += _colsum(dpre * shifted[j])

    own = pl.BlockSpec((tr, CONV_CB), lambda cb, i: (i, cb))
    wspec = pl.BlockSpec((CONV_W, CONV_CB), lambda cb, i: (0, cb))
    bspec = pl.BlockSpec((1, CONV_CB), lambda cb, i: (0, cb))
    return pl.pallas_call(
        body, name="conv_bwd_pre", grid=(CONV_DIM // CONV_CB, s // tr),
        in_specs=[blk, prev, wspec, bspec, own], out_specs=[own, wspec, bspec],
        out_shape=[jax.ShapeDtypeStruct((s, CONV_DIM), F32), jax.ShapeDtypeStruct((CONV_W, CONV_DIM), F32),
                   jax.ShapeDtypeStruct((1, CONV_DIM), F32)],
        compiler_params=_params(("parallel", "arbitrary")),
    )(proj, proj, conv_w, conv_b, dxc)


def _conv_bwd_in(dpre, conv_w, dproj):
    s = dpre.shape[0]
    tr = min(CONV_TR, s)
    per8 = tr // SUBLANES
    last8 = s // SUBLANES - 1
    nb = s // tr

    def body(d_ref, n_ref, w_ref, _, o_ref):
        d = d_ref[...]
        halo = jnp.where(pl.program_id(1) == nb - 1, 0.0, n_ref[...])
        acc = w_ref[CONV_W - 1:CONV_W, :] * d
        for j in range(1, CONV_W):
            acc = acc + w_ref[CONV_W - 1 - j:CONV_W - j, :] * _shift_up(d, halo, j)
        o_ref[...] = acc.astype(o_ref.dtype)

    own = pl.BlockSpec((tr, CONV_CB), lambda cb, i: (i, cb))
    nxt = pl.BlockSpec((SUBLANES, CONV_CB), lambda cb, i: (jnp.minimum((i + 1) * per8, last8), cb))
    return pl.pallas_call(
        body, name="conv_bwd_in", grid=(CONV_DIM // CONV_CB, nb),
        in_specs=[own, nxt, pl.BlockSpec((CONV_W, CONV_CB), lambda cb, i: (0, cb)), ANY],
        out_specs=pl.BlockSpec((tr, CONV_CB), lambda cb, i: (i, _conv_col(cb))),
        out_shape=jax.ShapeDtypeStruct(dproj.shape, dproj.dtype), input_output_aliases={3: 0},
        compiler_params=_params(("parallel", "parallel")),
    )(dpre, dpre, conv_w, dproj)


XC_B, XC_C = 1024, 1280


def _tile_iotas():
    row = lax.broadcasted_iota(jnp.int32, (CHUNK, LANES), 0)
    lane = lax.broadcasted_iota(jnp.int32, (CHUNK, LANES), 1)
    return row, lane


def _ssd_scalars(dtf_ref, bias_ref, alog_ref, row, lane):
    head = lane[:1] < N_HEADS
    raw = dtf_ref[...] + bias_ref[...]
    dt = _softplus(raw)
    a_neg = jnp.where(head, -jnp.exp(alog_ref[...]), 0.0)
    a = dt * a_neg
    tril = (row >= lane).astype(BF16)
    s = _dot3(tril, a)
    return raw, dt, a_neg, s


def _pair(v, j, lo):
    return jnp.where(lo, v[:, 2 * j:2 * j + 1], v[:, 2 * j + 1:2 * j + 2])


def _head_sum(x, lo, hh):
    return jnp.sum(jnp.where(lo == (hh == 0), x, 0.0), axis=1, keepdims=True)


def _decay_masks(s, st, h, row, lane):
    s_col = jnp.broadcast_to(s[:, h:h + 1], (CHUNK, LANES))
    s_row = jnp.broadcast_to(st[h:h + 1, :], (CHUNK, LANES))
    lm = jnp.where(row >= lane, jnp.exp(s_col - s_row), 0.0)
    lmt = jnp.where(row <= lane, jnp.exp(s_row - s_col), 0.0)
    return lm, lmt


def _ssd_fwd(xc_all, proj, dt_bias_l, a_log_l, d_exp):
    s_len = xc_all.shape[0]
    nc = s_len // CHUNK

    def body(x_ref, dtf_ref, bias_ref, alog_ref, dexp_ref, y_ref, prevs_ref, state_ref):
        @pl.when(pl.program_id(0) == 0)
        def _():
            state_ref[...] = jnp.zeros_like(state_ref)

        row, lane = _tile_iotas()
        lo = lane < HEAD_DIM
        _, dt, _, s = _ssd_scalars(dtf_ref, bias_ref, alog_ref, row, lane)
        tot = s[CHUNK - 1:CHUNK, :]
        st = s.T
        for g in range(SSM_GROUPS):
            bg = x_ref[:, XC_B + g * SSM_STATE:XC_B + (g + 1) * SSM_STATE].astype(BF16)
            cg = x_ref[:, XC_C + g * SSM_STATE:XC_C + (g + 1) * SSM_STATE].astype(BF16)
            cb = _dot(cg, bg, NT)
            for j in range(g * 4, g * 4 + 4):
                xs_p = x_ref[:, j * LANES:(j + 1) * LANES]
                dt_p, s_p, tot_p = _pair(dt, j, lo), _pair(s, j, lo), _pair(tot, j, lo[:1])
                xc_p = xs_p * dt_p
                xc_b = xc_p.astype(BF16)
                yd = []
                for hh in range(2):
                    lm, _ = _decay_masks(s, st, 2 * j + hh, row, lane)
                    yd.append(_dot((cb * lm).astype(BF16), xc_b))
                prev = state_ref[j]
                prevs_ref[0, j] = prev
                yo = _dot(cg, prev.astype(BF16)) * jnp.exp(s_p)
                y_ref[:, j * LANES:(j + 1) * LANES] = jnp.where(lo, yd[0], yd[1]) + yo + dexp_ref[:, j * LANES:(j + 1) * LANES] * xs_p
                to_end = jnp.exp(tot_p - s_p)
                state_ref[j] = jnp.exp(tot_p) * prev + _dot(bg, (xc_p * to_end).astype(BF16), TN)

    vec = lambda w: pl.BlockSpec((1, w), lambda c: (0, 0))
    return pl.pallas_call(
        body, name="ssd_fwd", grid=(nc,),
        in_specs=[pl.BlockSpec((CHUNK, CONV_DIM), lambda c: (c, 0)), pl.BlockSpec((CHUNK, LANES), lambda c: (c, AL_DTF // LANES)),
                  vec(LANES), vec(LANES), vec(D_MODEL)],
        out_specs=[pl.BlockSpec((CHUNK, D_MODEL), lambda c: (c, 0)), pl.BlockSpec((1, N_PAIRS, SSM_STATE, LANES), lambda c: (c, 0, 0, 0))],
        out_shape=[jax.ShapeDtypeStruct((s_len, D_MODEL), F32), jax.ShapeDtypeStruct((nc, N_PAIRS, SSM_STATE, LANES), F32)],
        scratch_shapes=[pltpu.VMEM((N_PAIRS, SSM_STATE, LANES), F32)],
        compiler_params=_params(("arbitrary",)),
    )(xc_all, proj, dt_bias_l, a_log_l, d_exp)


def _ssd_bwd(xc_all, proj, dt_bias_l, a_log_l, d_exp, prevs, dy):
    s_len = xc_all.shape[0]
    nc = s_len // CHUNK

    def body(x_ref, dtf_ref, bias_ref, alog_ref, dexp_ref, prevs_ref, dy_ref, dx_ref, ddt_ref, da_ref, dd_ref, dbias_ref, dstate_ref):
        @pl.when(pl.program_id(0) == 0)
        def _():
            dstate_ref[...] = jnp.zeros_like(dstate_ref)
            da_ref[...] = jnp.zeros_like(da_ref)
            dd_ref[...] = jnp.zeros_like(dd_ref)
            dbias_ref[...] = jnp.zeros_like(dbias_ref)

        row, lane = _tile_iotas()
        lo = lane < HEAD_DIM
        last = row == CHUNK - 1
        raw, dt, a_neg, s = _ssd_scalars(dtf_ref, bias_ref, alog_ref, row, lane)
        tot = s[CHUNK - 1:CHUNK, :]
        st = s.T
        ds_acc = jnp.zeros((CHUNK, LANES), F32)
        ddt_acc = jnp.zeros((CHUNK, LANES), F32)
        for g in range(SSM_GROUPS):
            bcol = slice(XC_B + g * SSM_STATE, XC_B + (g + 1) * SSM_STATE)
            ccol = slice(XC_C + g * SSM_STATE, XC_C + (g + 1) * SSM_STATE)
            bg = x_ref[:, bcol].astype(BF16)
            cg = x_ref[:, ccol].astype(BF16)
            cb = _dot(cg, bg, NT)
            cbt = _dot(bg, cg, NT)
            dcb = jnp.zeros((CHUNK, LANES), F32)
            dcbt = jnp.zeros((CHUNK, LANES), F32)
            db_acc = jnp.zeros((CHUNK, LANES), F32)
            dc_acc = jnp.zeros((CHUNK, LANES), F32)
            for j in range(g * 4, g * 4 + 4):
                cols = slice(j * LANES, (j + 1) * LANES)
                xs_p, dy_p = x_ref[:, cols], dy_ref[:, cols]
                dt_p, s_p, tot_p = _pair(dt, j, lo), _pair(s, j, lo), _pair(tot, j, lo[:1])
                xc_p = xs_p * dt_p
                xc_b, dy_b = xc_p.astype(BF16), dy_p.astype(BF16)
                e_p, f_p, etot_p = jnp.exp(s_p), jnp.exp(tot_p - s_p), jnp.exp(tot_p)
                prev, dnext = prevs_ref[0, j], dstate_ref[j]
                prev_b, dnext_b = prev.astype(BF16), dnext.astype(BF16)
                dd_ref[:, cols] += _colsum(dy_p * xs_p)
                dxs_p = dexp_ref[:, cols] * dy_p
                cp = _dot(cg, prev_b)
                gy = (dy_p * e_p).astype(BF16)
                dc_acc += _dot(gy, prev_b, NT)
                dstate_ref[j] = etot_p * dnext + _dot(cg, gy, TN)
                de = dy_p * cp * e_p
                bds = _dot(bg, dnext_b)
                db_acc += _dot((xc_p * f_p).astype(BF16), dnext_b, NT)
                dxc_p = bds * f_p
                df = bds * xc_p * f_p
                dtot_p = _colsum(dnext * prev) * etot_p + _colsum(df)
                dsl = de - df + jnp.where(last, dtot_p, 0.0)
                for hh in range(2):
                    h = 2 * j + hh
                    mine = lo == (hh == 0)
                    lm, lmt = _decay_masks(s, st, h, row, lane)
                    dy_h = jnp.where(mine, dy_p, 0.0).astype(BF16)
                    xc_h = jnp.where(mine, xc_p, 0.0).astype(BF16)
                    dm = _dot(dy_h, xc_b, NT)
                    dmt = _dot(xc_h, dy_b, NT)
                    mt = cbt * lmt
                    dxc_p += _dot(mt.astype(BF16), dy_h)
                    dml, dmtl = dm * lm, dmt * lmt
                    ds_h = jnp.sum(dml * cb - dmtl * cbt + jnp.where(mine, dsl, 0.0), axis=1, keepdims=True)
                    ds_acc += jnp.where(lane == h, ds_h, 0.0)
                    dcb += dml
                    dcbt += dmtl
                    ddt_acc += jnp.where(lane == h, _head_sum(dxc_p * xs_p, lo, hh), 0.0)
                dx_ref[:, cols] = dxs_p + dxc_p * dt_p
            dx_ref[:, ccol] = dc_acc + _dot(dcb.astype(BF16), bg)
            dx_ref[:, bcol] = db_acc + _dot(dcbt.astype(BF16), cg)
        triu = (row <= lane).astype(BF16)
        da = _dot3(triu, ds_acc)
        ddt = ddt_acc + da * a_neg
        da_ref[...] += _colsum(da * dt) * a_neg[:1]
        ddt_raw = jnp.where(lane < N_HEADS, ddt * _sigmoid(raw), 0.0)
        dbias_ref[...] += _colsum(ddt_raw)
        ddt_ref[...] = ddt_raw

    rev = lambda c: nc - 1 - c
    vec = lambda w: pl.BlockSpec((1, w), lambda c: (0, 0))
    return pl.pallas_call(
        body, name="ssd_bwd", grid=(nc,),
        in_specs=[pl.BlockSpec((CHUNK, CONV_DIM), lambda c: (rev(c), 0)), pl.BlockSpec((CHUNK, LANES), lambda c: (rev(c), AL_DTF // LANES)),
                  vec(LANES), vec(LANES), vec(D_MODEL),
                  pl.BlockSpec((1, N_PAIRS, SSM_STATE, LANES), lambda c: (rev(c), 0, 0, 0)),
                  pl.BlockSpec((CHUNK, D_MODEL), lambda c: (rev(c), 0))],
        out_specs=[pl.BlockSpec((CHUNK, CONV_DIM), lambda c: (rev(c), 0)), pl.BlockSpec((CHUNK, LANES), lambda c: (rev(c), 0)),
                   vec(LANES), vec(D_MODEL), vec(LANES)],
        out_shape=[jax.ShapeDtypeStruct((s_len, CONV_DIM), F32), jax.ShapeDtypeStruct((s_len, LANES), F32),
                   jax.ShapeDtypeStruct((1, LANES), F32), jax.ShapeDtypeStruct((1, D_MODEL), F32), jax.ShapeDtypeStruct((1, LANES), F32)],
        scratch_shapes=[pltpu.VMEM((N_PAIRS, SSM_STATE, LANES), F32)],
        compiler_params=_params(("arbitrary",)),
    )(xc_all, proj, dt_bias_l, a_log_l, d_exp, prevs, dy)


AUG_LANES = 6


def _aug_base(hh):
    return HEAD_DIM if hh == 0 else 0


NEG = -1e30
ATT_T = 512


def _fox_cum(proj, f_bias_l):
    s_len = proj.shape[0]
    nc = s_len // CHUNK

    def body(dtf_ref, fb_ref, cum_ref):
        row, lane = _tile_iotas()
        tril = (row >= lane).astype(BF16)
        spread = [(jnp.where(lane == AUG_LANES * row + i, 1.0, 0.0) - jnp.where(lane == AUG_LANES * row + 3 + i, 1.0, 0.0)).astype(BF16)
                  for i in range(3)]

        def step(c, carry):
            rows = pl.ds(pl.multiple_of(c * CHUNK, CHUNK), CHUNK)
            lf = -_softplus(-(dtf_ref[rows, :] + fb_ref[...]))
            lf = jnp.where(lane < N_HEADS, pltpu.roll(lf, LANES - F_LANE, 1), 0.0)
            cs = _dot3(tril, lf) + carry
            parts = _split3(cs)
            cum_ref[rows, :] = _dot(parts[0], spread[0]) + _dot(parts[1], spread[1]) + _dot(parts[2], spread[2])
            return cs[CHUNK - 1:CHUNK, :]

        lax.fori_loop(0, nc, step, jnp.zeros((1, LANES), F32))

    return pl.pallas_call(
        body, name="fox_cum", grid=(1,),
        in_specs=[pl.BlockSpec((s_len, LANES), lambda i: (0, AL_DTF // LANES)), pl.BlockSpec((1, LANES), lambda i: (0, 0))],
        out_specs=pl.BlockSpec((s_len, LANES), lambda i: (0, 0)),
        out_shape=jax.ShapeDtypeStruct((s_len, LANES), F32),
        compiler_params=_params(("arbitrary",)),
    )(proj, f_bias_l)


def _fox_cum_bwd(dcum, proj, f_bias_l, ddt_tile, dproj):
    s_len = proj.shape[0]
    nc = s_len // CHUNK

    def body(dcum_ref, dtf_ref, fb_ref, ddt_ref, _, out_ref, dfb_ref):
        row, lane = _tile_iotas()
        triu = (row <= lane).astype(BF16)
        is_f = (lane >= F_LANE) & (lane < F_LANE + N_HEADS)

        def step(t, carry):
            run, dfb = carry
            rows = pl.ds(pl.multiple_of((nc - 1 - t) * CHUNK, CHUNK), CHUNK)
            rc = _dot3(triu, dcum_ref[rows, :]) + run
            sg = _sigmoid(-(dtf_ref[rows, :] + fb_ref[...]))
            df = jnp.where(is_f, pltpu.roll(rc, F_LANE, 1) * sg, 0.0)
            out_ref[rows, :] = (df + ddt_ref[rows, :]).astype(out_ref.dtype)
            return rc[0:1, :], dfb + _colsum(df)

        _, dfb = lax.fori_loop(0, nc, step, (jnp.zeros((1, LANES), F32), jnp.zeros((1, LANES), F32)))
        dfb_ref[...] = dfb

    whole = pl.BlockSpec((s_len, LANES), lambda i: (0, 0))
    dtf_cols = pl.BlockSpec((s_len, LANES), lambda i: (0, AL_DTF // LANES))
    vec = pl.BlockSpec((1, LANES), lambda i: (0, 0))
    return pl.pallas_call(
        body, name="fox_cum_bwd", grid=(1,),
        in_specs=[whole, dtf_cols, vec, whole, ANY], out_specs=[dtf_cols, vec],
        out_shape=[jax.ShapeDtypeStruct(dproj.shape, dproj.dtype), jax.ShapeDtypeStruct((1, LANES), F32)],
        input_output_aliases={4: 0}, compiler_params=_params(("arbitrary",)),
    )(dcum, proj, f_bias_l, ddt_tile, dproj)


def _attn_prep(proj, cum):
    s_len = proj.shape[0]
    tr = min(256, s_len)

    def body(q_ref, k_ref, v_ref, cum_ref, qa_ref, ka_ref, vb_ref):
        lane = lax.broadcasted_iota(jnp.int32, (tr, LANES), 1)
        lo = lane < HEAD_DIM
        c = cum_ref[...]
        for p in range(N_PAIRS):
            cols = slice(p * LANES, (p + 1) * LANES)
            q, k = q_ref[:, cols] * (HEAD_DIM ** -0.5), k_ref[:, cols]
            for hh in range(2):
                base = _aug_base(hh)
                r = pltpu.roll(c, (base - AUG_LANES * (2 * p + hh)) % LANES, 1)
                first = (lane >= base) & (lane < base + 3)
                second = (lane >= base + 3) & (lane < base + AUG_LANES)
                mine = lo == (hh == 0)
                qa_ref[2 * p + hh] = jnp.where(mine, q, jnp.where(first, r, jnp.where(second, 1.0, 0.0))).astype(BF16)
                ka_ref[2 * p + hh] = jnp.where(mine, k, jnp.where(first, 1.0, jnp.where(second, r, 0.0))).astype(BF16)
        vb_ref[...] = v_ref[...].astype(BF16)

    assert AL_Q % D_MODEL == 0 and AL_K % D_MODEL == 0 and AL_V % D_MODEL == 0
    slab = lambda col0: pl.BlockSpec((tr, D_MODEL), lambda i: (i, col0 // D_MODEL))
    heads = pl.BlockSpec((N_HEADS, tr, LANES), lambda i: (0, i, 0))
    return pl.pallas_call(
        body, name="attn_prep", grid=(s_len // tr,),
        in_specs=[slab(AL_Q), slab(AL_K), slab(AL_V), pl.BlockSpec((tr, LANES), lambda i: (i, 0))],
        out_specs=[heads, heads, pl.BlockSpec((tr, D_MODEL), lambda i: (i, 0))],
        out_shape=[jax.ShapeDtypeStruct((N_HEADS, s_len, LANES), BF16), jax.ShapeDtypeStruct((N_HEADS, s_len, LANES), BF16),
                   jax.ShapeDtypeStruct((s_len, D_MODEL), BF16)],
        compiler_params=_params(("parallel",)),
    )(proj, proj, proj, cum)


def _attn_fwd(qa, ka, vb, halves):
    s_len = vb.shape[0]
    t = min(ATT_T, s_len)
    nq = s_len // t
    n = len(halves)

    def body(qa_ref, ka_ref, vb_ref, *rest):
        o_ref, lse_ref = rest[n:n + 2]
        start, finish = _gather_plan(rest[:n], rest[n + 2:2 * n + 2], *rest[2 * n + 2:])
        i = pl.program_id(1)
        pl.when((pl.program_id(0) == 0) & (i == 0))(start)
        row = lax.broadcasted_iota(jnp.int32, (t, t), 0)
        col = lax.broadcasted_iota(jnp.int32, (t, t), 1)
        lo = lax.broadcasted_iota(jnp.int32, (t, LANES), 1) < HEAD_DIM
        qs = (qa_ref[0], qa_ref[1])

        def block(j, carry, masked):
            rows = pl.ds(pl.multiple_of(j * t, t), t)
            v = vb_ref[rows, :]
            new = []
            for hh in range(2):
                m, l, acc = carry[hh]
                s = _dot(qs[hh], ka_ref[hh, rows, :], NT)
                if masked:
                    s = jnp.where(row >= col, s, NEG)
                m_new = jnp.maximum(m, jnp.max(s, axis=1, keepdims=True))
                alpha = jnp.exp(m - m_new)
                p = jnp.exp(s - m_new)
                new.append((m_new, alpha * l + jnp.sum(p, axis=1, keepdims=True), alpha * acc + _dot(p.astype(BF16), v)))
            return tuple(new)

        init = (jnp.full((t, 1), NEG, F32), jnp.zeros((t, 1), F32), jnp.zeros((t, LANES), F32))
        carry = lax.fori_loop(0, i, functools.partial(block, masked=False), (init, init))
        (m0, l0, acc0), (m1, l1, acc1) = block(i, carry, True)
        o_ref[...] = jnp.where(lo, acc0 / l0, acc1 / l1)
        lse_ref[...] = jnp.where(lo, m0 + jnp.log(l0), m1 + jnp.log(l1))
        pl.when((pl.program_id(0) == N_PAIRS - 1) & (i == nq - 1))(finish)

    out = pl.BlockSpec((t, LANES), lambda p, i: (i, p))
    res = pl.pallas_call(
        body, name="attn_fwd", grid=(N_PAIRS, nq),
        in_specs=[pl.BlockSpec((2, t, LANES), lambda p, i: (p, i, 0)), pl.BlockSpec((2, s_len, LANES), lambda p, i: (p, 0, 0)),
                  pl.BlockSpec((s_len, LANES), lambda p, i: (0, p))] + [ANY] * n,
        out_specs=[out, out] + [ANY] * n,
        out_shape=[jax.ShapeDtypeStruct((s_len, D_MODEL), F32), jax.ShapeDtypeStruct((s_len, D_MODEL), F32)]
        + [jax.ShapeDtypeStruct((N_CHIPS, *h.shape), h.dtype) for h in halves],
        scratch_shapes=_exchange_sems(n),
        compiler_params=_params(("arbitrary", "arbitrary")),
    )(qa, ka, vb, *halves)
    return res[0], res[1], res[2:]


def _attn_bwd(qa, ka, vb, o, lse, do, parts, dproj):
    s_len = vb.shape[0]
    t = min(ATT_T, s_len)
    nq = s_len // t
    n = len(parts)

    def body(qa_ref, ka_ref, vb_ref, o_ref, lse_ref, do_ref, *rest):
        dqa_ref, dka_ref, dv_ref = rest[n + 1:n + 4]
        start, finish = _reduce_plan(rest[:n], rest[n + 4:2 * n + 4], *rest[2 * n + 4:])
        j = pl.program_id(1)
        pl.when((pl.program_id(0) == 0) & (j == 0))(start)

        @pl.when(j == 0)
        def _():
            dqa_ref[...] = jnp.zeros_like(dqa_ref)

        row = lax.broadcasted_iota(jnp.int32, (t, t), 0)
        col = lax.broadcasted_iota(jnp.int32, (t, t), 1)
        lo = lax.broadcasted_iota(jnp.int32, (t, LANES), 1) < HEAD_DIM
        v = vb_ref[...]
        ks = (ka_ref[0], ka_ref[1])

        def block(i, carry, masked):
            dk, dv = list(carry[:2]), carry[2]
            rows = pl.ds(pl.multiple_of(i * t, t), t)
            do_p, o_p, lse_p = do_ref[rows, :], o_ref[rows, :], lse_ref[rows, :]
            for hh in range(2):
                q = qa_ref[hh, rows, :]
                do_h = jnp.where(lo == (hh == 0), do_p, 0.0)
                delta = jnp.sum(do_h * o_p, axis=1, keepdims=True)
                s = _dot(q, ks[hh], NT)
                if masked:
                    s = jnp.where(row >= col, s, NEG)
                p = jnp.exp(s - lse_p[:, hh * HEAD_DIM:hh * HEAD_DIM + 1])
                do_b = do_h.astype(BF16)
                ds = (p * (_dot(do_b, v, NT) - delta)).astype(BF16)
                dv = dv + _dot(p.astype(BF16), do_b, TN)
                dk[hh] = dk[hh] + _dot(ds, q, TN)
                dqa_ref[hh, rows, :] += _dot(ds, ks[hh])
            return dk[0], dk[1], dv

        zero = jnp.zeros((t, LANES), F32)
        carry = block(j, (zero, zero, zero), True)
        dk0, dk1, dv = lax.fori_loop(j + 1, nq, functools.partial(block, masked=False), carry)
        dka_ref[0] = dk0
        dka_ref[1] = dk1
        dv_ref[...] = dv.astype(dv_ref.dtype)
        pl.when((pl.program_id(0) == N_PAIRS - 1) & (j == nq - 1))(finish)

    whole_pair = pl.BlockSpec((2, s_len, LANES), lambda p, j: (p, 0, 0))
    blk_pair = pl.BlockSpec((2, t, LANES), lambda p, j: (p, j, 0))
    whole_cols = pl.BlockSpec((s_len, LANES), lambda p, j: (0, p))
    blk_cols = pl.BlockSpec((t, LANES), lambda p, j: (j, p))
    res = pl.pallas_call(
        body, name="attn_bwd", grid=(N_PAIRS, nq),
        in_specs=[whole_pair, blk_pair, blk_cols, whole_cols, whole_cols, whole_cols] + [ANY] * (n + 1),
        out_specs=[whole_pair, blk_pair, pl.BlockSpec((t, LANES), lambda p, j: (j, AL_V // LANES + p))] + [ANY] * n,
        out_shape=[jax.ShapeDtypeStruct((N_HEADS, s_len, LANES), F32), jax.ShapeDtypeStruct((N_HEADS, s_len, LANES), F32),
                   jax.ShapeDtypeStruct(dproj.shape, dproj.dtype)]
        + [jax.ShapeDtypeStruct((N_DEV, g.shape[1] // 2, g.shape[2]), g.dtype) for g in parts],
        scratch_shapes=_exchange_sems(n), input_output_aliases={6 + n: 2},
        compiler_params=_params(("arbitrary", "arbitrary")),
    )(qa, ka, vb, o, lse, do, *parts, dproj)
    return res[0], res[1], res[2], res[3:]


def _attn_post(dqa, dka, dproj):
    s_len = dqa.shape[1]
    tr = min(256, s_len)
    assert AL_K == AL_Q + D_MODEL and AL_Q % (2 * D_MODEL) == 0

    def body(dqa_ref, dka_ref, _, dqk_ref, dcum_ref):
        lane = lax.broadcasted_iota(jnp.int32, (tr, LANES), 1)
        lo = lane < HEAD_DIM
        dcum = jnp.zeros((tr, LANES), F32)
        for p in range(N_PAIRS):
            a0, a1, b0, b1 = dqa_ref[2 * p], dqa_ref[2 * p + 1], dka_ref[2 * p], dka_ref[2 * p + 1]
            dq = jnp.where(lo, a0, a1) * (HEAD_DIM ** -0.5)
            dqk_ref[:, p * LANES:(p + 1) * LANES] = dq.astype(dqk_ref.dtype)
            dqk_ref[:, D_MODEL + p * LANES:D_MODEL + (p + 1) * LANES] = jnp.where(lo, b0, b1).astype(dqk_ref.dtype)
            for hh, (a, b) in enumerate(((a0, b0), (a1, b1))):
                base = _aug_base(hh)
                dcum = dcum + jnp.where(lane == 2 * p + hh, a[:, base:base + 1] - b[:, base + 3:base + 4], 0.0)
        dcum_ref[...] = dcum

    heads = pl.BlockSpec((N_HEADS, tr, LANES), lambda i: (0, i, 0))
    return pl.pallas_call(
        body, name="attn_post", grid=(s_len // tr,),
        in_specs=[heads, heads, ANY],
        out_specs=[pl.BlockSpec((tr, 2 * D_MODEL), lambda i: (i, AL_Q // (2 * D_MODEL))), pl.BlockSpec((tr, LANES), lambda i: (i, 0))],
        out_shape=[jax.ShapeDtypeStruct(dproj.shape, dproj.dtype), jax.ShapeDtypeStruct((s_len, LANES), F32)],
        input_output_aliases={2: 0}, compiler_params=_params(("parallel",)),
    )(dqa, dka, dproj)


def _ln_stats(r):
    mu = _rowmean(r)
    xc = r - mu
    rstd = lax.rsqrt(_rowmean(xc * xc) + LN_EPS)
    return xc * rstd, rstd


def _ln_bwd(dxh, xh, rstd):
    return rstd * (dxh - _rowmean(dxh) - xh * _rowmean(dxh * xh))


def _rms_bwd(dgn, g, r):
    return r * dgn - (r * r * r) * g * _rowmean(dgn * g)


def _to_aligned(wt):
    out = jnp.zeros((AL_COLS, wt.shape[1]), wt.dtype)
    for dst, (lo, hi) in ((0, (0, 2048)), (AL_Q, (2576, 5648)), (AL_B, (2048, 2560)), (AL_DTF, (2560, 2576)), (AL_DTF + 16, (5648, 5664))):
        out = lax.dynamic_update_slice_in_dim(out, wt[lo:hi], dst, axis=0)
    return out


def _from_aligned(gt):
    out = jnp.zeros((IN_COLS, gt.shape[1]), gt.dtype)
    for dst, (lo, hi) in ((0, (0, AL_Q)), (2048, (AL_B, AL_DTF)), (2560, (AL_DTF, AL_DTF + 16)), (2576, (AL_Q, AL_B)),
                          (5648, (AL_DTF + 16, AL_DTF + 32))):
        out = lax.dynamic_update_slice_in_dim(out, gt[lo:hi], dst, axis=0)
    return out


def _lanes(v, at=0):
    return jnp.pad(v, ((0, 0), (at, LANES - at - v.shape[1])))


def _local_step(x, tgt, mod, w_alt, halves, sp):
    d = D_MODEL
    sh1, sc1, g1, sh2, sc2, g2 = [mod[:, i * d:(i + 1) * d] for i in range(6)]
    dt_bias_l, a_log_l, f_bias_l = _lanes(sp["dt_bias"]), _lanes(sp["a_log"]), _lanes(sp["f_bias"], F_LANE)
    d_exp = jnp.repeat(sp["d_skip"], HEAD_DIM, axis=1)
    z_slab = lambda a: (a, d, AL_Z // d)

    (h1,), _ = _rowwise("mod1", lambda x, sc, sh: ([x * (1.0 + sc) + sh], []), [x], [sc1, sh1], [(d, BF16)], [])
    proj = _matmul("proj", h1, w_alt, dims=NT, tn=1152)
    xc_all = _conv_fwd(proj, sp["conv_w"], sp["conv_b"])
    y_ssd, prevs = _ssd_fwd(xc_all, proj, dt_bias_l, a_log_l, d_exp)

    def gated_norm(y, z, w):
        g = y * _silu(z)
        return [g * lax.rsqrt(_rowmean(g * g) + RMS_EPS) * w], []

    (y_mix,), _ = _rowwise("ssm_norm", gated_norm, [y_ssd, z_slab(proj)], [sp["ssm_norm_w"]], [(d, BF16, (2 * d, 0, None))], [])
    cum = _fox_cum(proj, f_bias_l)
    qa, ka, vb = _attn_prep(proj, cum)
    o, lse, (g_out, g_fi, g_fo) = _attn_fwd(qa, ka, vb, halves)
    w_out = g_out.reshape(2 * d, d)
    w_fi = g_fi.transpose(1, 0, 2).reshape(d, D_FF)
    w_fo = g_fo.reshape(D_FF, d)
    (y_mix,), _ = _rowwise("attn_norm", lambda o, w: ([o * lax.rsqrt(_rowmean(o * o) + RMS_EPS) * w], []),
                           [o], [sp["attn_norm_w"]], [(d, BF16, (2 * d, 1, y_mix))], [])
    def ln1_fwd(y, x, g1, sc2, sh2, lg, lb):
        r1 = ALPHA * x + (1.0 + g1) * y
        xh, _ = _ln_stats(r1)
        x1 = xh * lg + lb
        h2 = x1 * (1.0 + sc2) + sh2
        return [y, r1, h2, h2.T], []

    def relu2(u):
        a = jnp.square(jnp.maximum(u, 0.0))
        return [a, a.T], []

    y, r1, h2, h2_t = _matmul("out_proj", y_mix, w_out, tm=512, tk=2048,
                              epi=(ln1_fwd, [x], [g1, sc2, sh2, sp["ln1_g"], sp["ln1_b"]], [F32, F32, BF16, ("T", BF16)], []))
    act, act_t = _matmul("ff_in", h2, w_fi, epi=(relu2, [], [], [BF16, ("T", BF16)], []))

    def head(ff, r1, tgt, g2, l1g, l1b, l2g, l2b):
        xh1, _ = _ln_stats(r1)
        x1 = xh1 * l1g + l1b
        xh2, rstd2 = _ln_stats(ALPHA * x1 + (1.0 + g2) * ff)
        err = xh2 * l2g + l2b - tgt
        loss = 0.5 * jnp.sum(_rowmean(err * err))
        dx2 = err * (1.0 / d)
        dr2 = _ln_bwd(dx2 * l2g, xh2, rstd2)
        return ([dr2, (1.0 + g2) * dr2],
                [_colsum(dx2 * xh2), _colsum(dx2), _colsum(dr2 * ff), jnp.full((1, LANES), loss, F32)])

    dr2, dff, d_ln2_g, d_ln2_b, d_g2, loss = _matmul(
        "ff_out", act, w_fo, tm=512, tk=D_FF,
        epi=(head, [r1, tgt], [g2, sp["ln1_g"], sp["ln1_b"], sp["ln2_g"], sp["ln2_b"]], [F32, BF16], [d, d, d, LANES]))
    du = _matmul("d_act", dff, w_fo, dims=NT, epi=(lambda da, act: ([da * (2.0 * jnp.sqrt(act.astype(F32)))], []), [act], [], [BF16], []))
    dw_fo = _matmul("dw_ff_out", act_t, dff, tk=SEQ_TK, out_dtype=BF16, by_chip="rows")
    dw_fi = _matmul("dw_ff_in", h2_t, du, tk=SEQ_TK, out_dtype=BF16, by_chip="cols")

    def ln1_bwd(dh2, r1, dr2, y, sc2, g1, lg, lb):
        xh, rstd = _ln_stats(r1)
        x1 = xh * lg + lb
        dx1 = ALPHA * dr2 + dh2 * (1.0 + sc2)
        dr1 = _ln_bwd(dx1 * lg, xh, rstd)
        return ([dr1, (1.0 + g1) * dr1],
                [_colsum(dh2 * x1), _colsum(dh2), _colsum(dx1 * xh), _colsum(dx1), _colsum(dr1 * y)])

    dr1, dy, d_sc2, d_sh2, d_ln1_g, d_ln1_b, d_g1 = _matmul(
        "dh2", du, w_fi, dims=NT, tm=512, tk=D_FF,
        epi=(ln1_bwd, [r1, dr2, y], [sc2, g1, sp["ln1_g"], sp["ln1_b"]], [F32, BF16], [d] * 5))
    dymix = _matmul("dy_mix", dy, w_out, dims=NT)
    dw_out = _matmul("dw_out", y_mix, dy, dims=TN, tk=SEQ_TK, out_dtype=BF16, by_chip="rows")

    def attn_norm_bwd(o, dyo, w):
        r = lax.rsqrt(_rowmean(o * o) + RMS_EPS)
        return [_rms_bwd(dyo * w, o, r)], [_colsum(dyo * o * r)]

    (do,), (d_attn_w,) = _rowwise("attn_norm_bwd", attn_norm_bwd, [o, (dymix, d, 1)], [sp["attn_norm_w"]], [(d, F32)], [d])

    def gated_norm_bwd(y, z, dyo, w):
        sg = _sigmoid(z)
        sz = z * sg
        g = y * sz
        r = lax.rsqrt(_rowmean(g * g) + RMS_EPS)
        dg = _rms_bwd(dyo * w, g, r)
        return [dg * sz, dg * y * (sg * (1.0 + z * (1.0 - sg)))], [_colsum(dyo * g * r)]

    (dy_ssd, dproj), (d_ssm_w,) = _rowwise("ssm_norm_bwd", gated_norm_bwd, [y_ssd, z_slab(proj), (dymix, d, 0)],
                                           [sp["ssm_norm_w"]], [(d, F32), (d, BF16, (AL_COLS, AL_Z // d, None))], [d])
    dqa, dka, dproj, landed = _attn_bwd(qa, ka, vb, o, lse, do, [dw_out, dw_fi, dw_fo], dproj)
    dproj, dcum = _attn_post(dqa, dka, dproj)
    dxc, ddt_tile, d_alog_l, d_dexp, d_dtb_l = _ssd_bwd(xc_all, proj, dt_bias_l, a_log_l, d_exp, prevs, dy_ssd)
    dproj, d_fb_l = _fox_cum_bwd(dcum, proj, f_bias_l, ddt_tile, dproj)
    dpre, d_conv_w, d_conv_b = _conv_bwd_pre(proj, sp["conv_w"], sp["conv_b"], dxc)
    dproj = _conv_bwd_in(dpre, sp["conv_w"], dproj)
    dw_alt = _matmul("dw_in", dproj, h1, dims=TN, tm=1152, tk=SEQ_TK, out_dtype=BF16)
    part_in = _from_aligned(dw_alt).reshape(N_CHIPS, IN_COLS // N_CHIPS, d)

    def last(dh1, x, dr1, sc1):
        return [ALPHA * dr1 + dh1 * (1.0 + sc1)], [_colsum(dh1 * x), _colsum(dh1)]

    chip_in = _pair_sum(part_in, _pair_exchange(part_in), lax.axis_index("c"))
    dx, d_sc1, d_sh1, landed_in = _matmul("dh1", dproj, w_alt, tm=512, tk=AL_COLS, carry=[chip_in],
                                          epi=(last, [x, dr1], [sc1], [F32], [d, d]))

    small = {
        "mod": jnp.concatenate([d_sh1, d_sc1, d_g1, d_sh2, d_sc2, d_g2], axis=1),
        "conv_w": d_conv_w, "conv_b": d_conv_b,
        "dt_bias": d_dtb_l[:, :N_HEADS], "a_log": d_alog_l[:, :N_HEADS],
        "d_skip": jnp.sum(d_dexp.reshape(N_HEADS, HEAD_DIM), axis=1)[None, :],
        "ssm_norm_w": d_ssm_w, "f_bias": d_fb_l[:, F_LANE:F_LANE + N_HEADS], "attn_norm_w": d_attn_w,
        "ln1_g": d_ln1_g, "ln1_b": d_ln1_b, "ln2_g": d_ln2_g, "ln2_b": d_ln2_b, "loss": loss,
    }
    return dx, [landed_in, *landed], small


N_DEV = 8
N_CHIPS = 4
ANY = pl.BlockSpec(memory_space=pl.ANY)
VMEM_SPEC = pl.BlockSpec(memory_space=pltpu.VMEM)


def _place():
    x, y, c = lax.axis_index("x"), lax.axis_index("y"), lax.axis_index("c")
    return x, y, c


def _other_chips(x, y):
    return [(1 - x, y, 2 * (1 - x) + y), (x, 1 - y, 2 * x + 1 - y), (1 - x, 1 - y, 2 * (1 - x) + 1 - y)]


def _small_gather(v_ref, out_ref, send_sems, recv_sems, local_sem):
    x, y, c = _place()
    me = 4 * x + 2 * y + c
    mine = pltpu.make_async_copy(v_ref, out_ref.at[me], local_sem)
    mine.start()
    peers = _peers(x, y, c)

    def copy(rel, slot, to):
        return pltpu.make_async_remote_copy(src_ref=v_ref, dst_ref=out_ref.at[slot], send_sem=send_sems.at[rel],
                                            recv_sem=recv_sems.at[rel], device_id=to, device_id_type=MESH)

    sends = [copy(rel, me, peer) for rel, peer in enumerate(peers)]
    for cp in sends:
        cp.start()
    for rel, (px, py, pc) in enumerate(peers):
        copy(rel, 4 * px + 2 * py + pc, (x, y, c)).wait_recv()
    for cp in sends:
        cp.wait_send()
    mine.wait()


SMALL_GATHER_SEMS = [pltpu.SemaphoreType.DMA((N_DEV - 1,)), pltpu.SemaphoreType.DMA((N_DEV - 1,)), pltpu.SemaphoreType.DMA]


def _allgather_small(name, v, with_sum=False):
    def body(v_ref, out_ref, *rest):
        _small_gather(v_ref, out_ref, *rest[-3:])
        if with_sum:
            acc = out_ref[0]
            for dev in range(1, N_DEV):
                acc = acc + out_ref[dev]
            rest[0][...] = acc

    every = jax.ShapeDtypeStruct((N_DEV, *v.shape), v.dtype)
    return pl.pallas_call(
        body, name=name, out_shape=[every, jax.ShapeDtypeStruct(v.shape, v.dtype)] if with_sum else every,
        in_specs=[VMEM_SPEC], out_specs=[VMEM_SPEC] * 2 if with_sum else VMEM_SPEC, scratch_shapes=SMALL_GATHER_SEMS,
    )(v)


def _gather_shards(shard):
    def body(in_ref, out_ref, stage, send_sems, recv_sems, local_sems):
        start, finish = _shard_gather_plan(in_ref, out_ref, stage, send_sems, recv_sems, local_sems)
        start()
        finish()

    return pl.pallas_call(
        body, name="gather_w_in", out_shape=jax.ShapeDtypeStruct((N_CHIPS, *shard.shape), shard.dtype),
        in_specs=[ANY], out_specs=ANY,
        scratch_shapes=[pltpu.VMEM(shard.shape, shard.dtype), pltpu.SemaphoreType.DMA((6,)), pltpu.SemaphoreType.DMA((6,)),
                        pltpu.SemaphoreType.DMA((2,))],
        compiler_params=_params(),
    )(shard)


def _shard_gather_plan(in_ref, out_ref, stage, send_sems, recv_sems, local_sems):
    ch = in_ref.shape[1] // 2
    x, y, c = _place()
    k_me = 2 * x + y
    me, sibling = (x, y, c), (x, y, 1 - c)
    chips = _other_chips(x, y)

    def copy(idx, k, half, to, src=None):
        cols = out_ref.at[k, :, pl.ds(pl.multiple_of(half * ch, ch), ch)]
        return pltpu.make_async_remote_copy(src_ref=cols if src is None else src, dst_ref=cols, send_sem=send_sems.at[idx],
                                            recv_sem=recv_sems.at[idx], device_id=to, device_id_type=MESH)

    mine = in_ref.at[:, pl.ds(pl.multiple_of(c * ch, ch), ch)]
    sends = [copy(j, k_me, c, (cx, cy, c), src=mine) for j, (cx, cy, _) in enumerate(chips)]
    load = pltpu.make_async_copy(in_ref, stage, local_sems.at[0])
    store = pltpu.make_async_copy(stage, out_ref.at[k_me], local_sems.at[1])

    def start():
        for cp in sends:
            cp.start()
        load.start()

    def finish():
        load.wait()
        store.start()
        forwards = []
        for j, (_, _, kj) in enumerate(chips):
            copy(j, kj, c, me).wait_recv()
            forwards.append(copy(3 + j, kj, c, sibling))
            forwards[-1].start()
        for j, (_, _, kj) in enumerate(chips):
            copy(3 + j, kj, 1 - c, me).wait_recv()
        for cp in sends + forwards:
            cp.wait_send()
        store.wait()

    return start, finish


def _peers(x, y, c):
    return [((1 - x) if rel & 4 else x, (1 - y) if rel & 2 else y, (1 - c) if rel & 1 else c) for rel in range(1, N_DEV)]


def _exchange_sems(n):
    return [pltpu.SemaphoreType.DMA((n, N_DEV - 1)), pltpu.SemaphoreType.DMA((n, N_DEV - 1)), pltpu.SemaphoreType.DMA((n,))]


def _gather_plan(ins, outs, send_sems, recv_sems, local_sems):
    x, y, c = _place()
    k_me = 2 * x + y
    peers = [(rel, p) for rel, p in enumerate(_peers(x, y, c)) if (rel + 1) & 6]

    def copy(w, rel, k, half, to, src=None):
        rh = ins[w].shape[0] // 2
        rows = outs[w].at[k, pl.ds(pl.multiple_of(half * rh, rh), rh), :]
        return pltpu.make_async_remote_copy(src_ref=rows if src is None else src, dst_ref=rows, send_sem=send_sems.at[w, rel],
                                            recv_sem=recv_sems.at[w, rel], device_id=to, device_id_type=MESH)

    def mine(w):
        rh = ins[w].shape[0] // 2
        return ins[w].at[pl.ds(pl.multiple_of(c * rh, rh), rh), :]

    n = len(ins)
    local = [pltpu.make_async_copy(ins[w], outs[w].at[k_me], local_sems.at[w]) for w in range(n)]
    sends = [copy(w, rel, k_me, c, peer, src=mine(w)) for w in range(n) for rel, peer in peers]

    def start():
        for cp in local + sends:
            cp.start()

    def finish():
        for w in range(n):
            for rel, (px, py, pc) in peers:
                copy(w, rel, 2 * px + py, pc, (x, y, c)).wait_recv()
        for cp in sends:
            cp.wait_send()
        for cp in local:
            cp.wait()

    return start, finish


def _reduce_plan(ins, outs, send_sems, recv_sems, local_sems):
    x, y, c = _place()
    me = 4 * x + 2 * y + c
    peers = _peers(x, y, c)

    def block(w, k, half):
        rh = ins[w].shape[1] // 2
        return ins[w].at[k, pl.ds(pl.multiple_of(half * rh, rh), rh), :]

    def copy(w, rel, src, slot, to):
        return pltpu.make_async_remote_copy(src_ref=src, dst_ref=outs[w].at[slot], send_sem=send_sems.at[w, rel],
                                            recv_sem=recv_sems.at[w, rel], device_id=to, device_id_type=MESH)

    n = len(ins)
    local = [pltpu.make_async_copy(block(w, 2 * x + y, c), outs[w].at[me], local_sems.at[w]) for w in range(n)]
    sends = [copy(w, rel, block(w, 2 * px + py, pc), me, (px, py, pc)) for w in range(n) for rel, (px, py, pc) in enumerate(peers)]

    def start():
        for cp in local + sends:
            cp.start()

    def finish():
        for w in range(n):
            for rel, (px, py, pc) in enumerate(peers):
                copy(w, rel, block(w, 2 * x + y, c), 4 * px + 2 * py + pc, (x, y, c)).wait_recv()
        for cp in sends:
            cp.wait_send()
        for cp in local:
            cp.wait()

    return start, finish


def _scatter_plan(ins, outs, send_sems, recv_sems, local_sems):
    x, y, c = _place()
    k_me = 2 * x + y
    chips = _other_chips(x, y)

    def copy(w, j, src_k, dst_k, to):
        return pltpu.make_async_remote_copy(src_ref=ins[w].at[src_k], dst_ref=outs[w].at[dst_k], send_sem=send_sems.at[w, j],
                                            recv_sem=recv_sems.at[w, j], device_id=to, device_id_type=MESH)

    n = len(ins)
    local = [pltpu.make_async_copy(ins[w].at[k_me], outs[w].at[k_me], local_sems.at[w]) for w in range(n)]
    sends = [copy(w, j, kj, k_me, (cx, cy, c)) for w in range(n) for j, (cx, cy, kj) in enumerate(chips)]

    def start():
        for cp in local + sends:
            cp.start()

    def finish():
        for w in range(n):
            for j, (_, _, kj) in enumerate(chips):
                copy(w, j, k_me, kj, (x, y, c)).wait_recv()
        for cp in sends:
            cp.wait_send()
        for cp in local:
            cp.wait()

    return start, finish


def _row_tile(r, mult=2 * SUBLANES):
    if r % 256 == 0:
        return 256
    return max([t for t in range(mult, 513, mult) if r % t == 0], default=r)


def _pair_exchange(g):
    _, r, cdim = g.shape
    ch = cdim // 2

    def body(g_ref, got_ref, send_sem, recv_sem):
        x, y, c = _place()
        cp = pltpu.make_async_remote_copy(src_ref=g_ref.at[:, :, pl.ds(pl.multiple_of((1 - c) * ch, ch), ch)], dst_ref=got_ref,
                                          send_sem=send_sem, recv_sem=recv_sem, device_id=(x, y, 1 - c), device_id_type=MESH)
        cp.start()
        cp.wait_recv()
        cp.wait_send()

    return pl.pallas_call(
        body, name="pair_exchange", out_shape=jax.ShapeDtypeStruct((N_CHIPS, r, ch), g.dtype),
        in_specs=[ANY], out_specs=ANY, scratch_shapes=[pltpu.SemaphoreType.DMA, pltpu.SemaphoreType.DMA],
    )(g)


def _pair_sum(g, got, c):
    _, r, cdim = g.shape
    ch = cdim // 2
    tr = _row_tile(r)

    def body(c_ref, g_ref, got_ref, o_ref):
        o_ref[...] = (g_ref[...].astype(F32) + got_ref[...].astype(F32)).astype(o_ref.dtype)

    blk = pl.BlockSpec((1, tr, ch), lambda k, i, c_ref: (k, i, 0))
    return pl.pallas_call(
        body, name="pair_sum",
        grid_spec=pltpu.PrefetchScalarGridSpec(
            num_scalar_prefetch=1, grid=(N_CHIPS, r // tr),
            in_specs=[pl.BlockSpec((1, tr, ch), lambda k, i, c_ref: (k, i, c_ref[0])), blk], out_specs=blk),
        out_shape=jax.ShapeDtypeStruct((N_CHIPS, r, ch), BF16),
        compiler_params=_params(("parallel", "parallel")),
    )(jnp.reshape(c, (1,)).astype(jnp.int32), g, got)


def _sum_blocks(name, parts):
    k, r, cdim = parts.shape
    tr = _row_tile(r)

    def body(p_ref, o_ref):
        acc = p_ref[0].astype(F32)
        for i in range(1, k):
            acc = acc + p_ref[i].astype(F32)
        o_ref[...] = acc

    return pl.pallas_call(
        body, name=name, grid=(r // tr,),
        in_specs=[pl.BlockSpec((k, tr, cdim), lambda i: (0, i, 0))], out_specs=pl.BlockSpec((tr, cdim), lambda i: (i, 0)),
        out_shape=jax.ShapeDtypeStruct((r, cdim), F32), compiler_params=_params(("parallel",)),
    )(parts)


def _pair_swap(halves):
    n = len(halves)

    def body(*refs):
        ins, outs = refs[:n], refs[n:2 * n]
        send_sems, recv_sems = refs[2 * n:]
        x, y, c = _place()
        cps = [pltpu.make_async_remote_copy(src_ref=ins[w], dst_ref=outs[w], send_sem=send_sems.at[w], recv_sem=recv_sems.at[w],
                                            device_id=(x, y, 1 - c), device_id_type=MESH) for w in range(n)]
        for cp in cps:
            cp.start()
        for cp in cps:
            cp.wait_recv()
        for cp in cps:
            cp.wait_send()

    return pl.pallas_call(
        body, name="pair_swap", out_shape=[jax.ShapeDtypeStruct(h.shape, h.dtype) for h in halves],
        in_specs=[ANY] * n, out_specs=[ANY] * n,
        scratch_shapes=[pltpu.SemaphoreType.DMA((n,)), pltpu.SemaphoreType.DMA((n,))],
    )(*halves)


ADA_SHARD = 6 * D_MODEL // N_CHIPS


def _mod_part(c_all, w_shard, b_shard):
    tn = 512

    def body(c_ref, w_ref, b_ref, o_ref):
        o_ref[...] = _dot(_silu(c_ref[...]).astype(BF16), w_ref[...].astype(BF16)) + b_ref[...]

    return pl.pallas_call(
        body, name="mod_part", grid=(ADA_SHARD // tn,),
        in_specs=[pl.BlockSpec((N_DEV, D_MODEL), lambda j: (0, 0)), pl.BlockSpec((D_MODEL, tn), lambda j: (0, j)),
                  pl.BlockSpec((1, tn), lambda j: (0, j))],
        out_specs=pl.BlockSpec((N_DEV, tn), lambda j: (0, j)),
        out_shape=jax.ShapeDtypeStruct((N_DEV, ADA_SHARD), F32), compiler_params=_params(("parallel",)),
    )(c_all, w_shard, b_shard)


def _w_ada_grad(c_all_t, dmod_shard):
    tm = 256

    def body(ct_ref, dm_ref, o_ref):
        act = _silu(ct_ref[...])
        acc = act[:, 0:1] * dm_ref[0:1, :]
        for dev in range(1, N_DEV):
            acc = acc + act[:, dev:dev + 1] * dm_ref[dev:dev + 1, :]
        o_ref[...] = acc

    return pl.pallas_call(
        body, name="w_ada_grad", grid=(D_MODEL // tm,),
        in_specs=[pl.BlockSpec((tm, N_DEV), lambda i: (i, 0)), pl.BlockSpec((N_DEV, ADA_SHARD), lambda i: (0, 0))],
        out_specs=pl.BlockSpec((tm, ADA_SHARD), lambda i: (i, 0)),
        out_shape=jax.ShapeDtypeStruct((D_MODEL, ADA_SHARD), F32), compiler_params=_params(("parallel",)),
    )(c_all_t, dmod_shard)


def _adamw_math(w, g, m, v):
    nm = ADAM_B1 * m + (1.0 - ADAM_B1) * g
    nv = ADAM_B2 * v + (1.0 - ADAM_B2) * jnp.square(g)
    m_hat = nm / (1.0 - ADAM_B1 ** ADAM_STEP)
    v_hat = nv / (1.0 - ADAM_B2 ** ADAM_STEP)
    return -ADAM_LR * (m_hat / (jnp.sqrt(v_hat) + ADAM_EPS) + ADAM_WD * w), nm, nv


def _adamw(name, w, g, m, v):
    _, r, cdim = w.shape
    tr = 256 if r % 256 == 0 else r

    def body(w_ref, g_ref, m_ref, v_ref, go_ref, d_ref, nm_ref, nv_ref):
        go_ref[...] = g_ref[...]
        d_ref[...], nm_ref[...], nv_ref[...] = _adamw_math(w_ref[...], g_ref[...], m_ref[...], v_ref[...])

    blk = pl.BlockSpec((None, tr, cdim), lambda i: (0, i, 0))
    return pl.pallas_call(
        body, name=name, grid=(r // tr,), in_specs=[blk, pl.BlockSpec((tr, cdim), lambda i: (i, 0)), blk, blk], out_specs=[blk] * 4,
        out_shape=[jax.ShapeDtypeStruct((1, r, cdim), F32)] * 4, compiler_params=_params(("parallel",)),
    )(w, g, m, v)


def _adamw_pair(name, w, mine, other, m, v, c, by_cols=False):
    _, r, cdim = w.shape
    hr, hc = mine.shape
    tr = _row_tile(hr, SUBLANES)
    per = hr // tr

    def body(c_ref, w_ref, a_ref, b_ref, m_ref, v_ref, g_ref, d_ref, nm_ref, nv_ref):
        half = pl.program_id(1) if by_cols else pl.program_id(0) // per
        g = jnp.where(half == c_ref[0], a_ref[...], b_ref[...])
        g_ref[...] = g
        d_ref[...], nm_ref[...], nv_ref[...] = _adamw_math(w_ref[...], g, m_ref[...], v_ref[...])

    blk = pl.BlockSpec((None, tr, hc), lambda i, j, c_ref: (0, i, j))
    half = pl.BlockSpec((tr, hc), lambda i, j, c_ref: (i % per, 0))
    return pl.pallas_call(
        body, name=name,
        grid_spec=pltpu.PrefetchScalarGridSpec(num_scalar_prefetch=1, grid=(r // tr, cdim // hc),
                                               in_specs=[blk, half, half, blk, blk], out_specs=[blk] * 4),
        out_shape=[jax.ShapeDtypeStruct((1, r, cdim), F32)] * 4, compiler_params=_params(("parallel", "parallel")),
    )(jnp.reshape(c, (1,)).astype(jnp.int32), w, mine, other, m, v)


SMALL = ["b_ada", "conv_b", "dt_bias", "a_log", "d_skip", "ssm_norm_w", "f_bias", "attn_norm_w", "ln1_g", "ln1_b", "ln2_g", "ln2_b"]


def _pack(vs):
    pieces = []
    for v in vs:
        pieces.append(v)
        if v.shape[1] % LANES:
            pieces.append(jnp.zeros((1, -v.shape[1] % LANES), v.dtype))
    return jnp.concatenate(pieces, axis=1)


def _adamw_small(total, offs, ws, ms, vs):
    n = len(ws)

    def body(*refs):
        t_ref, outs = refs[0], refs[1 + 3 * n:]
        for i in range(n):
            g = t_ref[:, offs[i]:offs[i] + ws[i].shape[1]]
            dl, nm, nv = _adamw_math(refs[1 + i][...], g, refs[1 + n + i][...], refs[1 + 2 * n + i][...])
            outs[4 * i][...], outs[4 * i + 1][...], outs[4 * i + 2][...], outs[4 * i + 3][...] = g, dl, nm, nv

    res = pl.pallas_call(
        body, name="adamw_small", in_specs=[VMEM_SPEC] * (1 + 3 * n), out_specs=[VMEM_SPEC] * (4 * n),
        out_shape=[jax.ShapeDtypeStruct(w.shape, F32) for w in ws for _ in range(4)],
    )(total, *ws, *ms, *vs)
    return [res[4 * i:4 * i + 4] for i in range(n)]


def kernel(x, c, w_ada, b_ada, w_in, conv_w, conv_b, dt_bias, a_log, d_skip, ssm_norm_w, f_bias, attn_norm_w, w_out, ln1_g, ln1_b, w_ff_in, w_ff_out, ln2_g, ln2_b, loss_target, m_w_ada, m_b_ada, m_w_in, m_conv_w, m_conv_b, m_dt_bias, m_a_log, m_d_skip, m_ssm_norm_w, m_f_bias, m_attn_norm_w, m_w_out, m_ln1_g, m_ln1_b, m_w_ff_in, m_w_ff_out, m_ln2_g, m_ln2_b, v_w_ada, v_b_ada, v_w_in, v_conv_w, v_conv_b, v_dt_bias, v_a_log, v_d_skip, v_ssm_norm_w, v_f_bias, v_attn_norm_w, v_w_out, v_ln1_g, v_ln1_b, v_w_ff_in, v_w_ff_out, v_ln2_g, v_ln2_b):
    a = dict(b_ada=b_ada, conv_b=conv_b, dt_bias=dt_bias, a_log=a_log, d_skip=d_skip, ssm_norm_w=ssm_norm_w, f_bias=f_bias,
             attn_norm_w=attn_norm_w, ln1_g=ln1_g, ln1_b=ln1_b, ln2_g=ln2_g, ln2_b=ln2_b)
    ms = dict(b_ada=m_b_ada, conv_b=m_conv_b, dt_bias=m_dt_bias, a_log=m_a_log, d_skip=m_d_skip, ssm_norm_w=m_ssm_norm_w,
              f_bias=m_f_bias, attn_norm_w=m_attn_norm_w, ln1_g=m_ln1_g, ln1_b=m_ln1_b, ln2_g=m_ln2_g, ln2_b=m_ln2_b)
    vs = dict(b_ada=v_b_ada, conv_b=v_conv_b, dt_bias=v_dt_bias, a_log=v_a_log, d_skip=v_d_skip, ssm_norm_w=v_ssm_norm_w,
              f_bias=v_f_bias, attn_norm_w=v_attn_norm_w, ln1_g=v_ln1_g, ln1_b=v_ln1_b, ln2_g=v_ln2_g, ln2_b=v_ln2_b)
    xi, yi, ci = _place()
    chip = 2 * xi + yi
    me = 4 * xi + 2 * yi + ci
    d = D_MODEL
    conv_shard = CONV_DIM // N_CHIPS

    first = _allgather_small("gather_c", jnp.concatenate([c, conv_w[0].reshape(1, CONV_W * conv_shard)], axis=1))[:, 0]
    c_all = first[:, :d]
    conv_w_full = first[::2, d:].reshape(N_CHIPS, CONV_W, conv_shard).transpose(1, 0, 2).reshape(CONV_W, CONV_DIM)
    b_shard = lax.dynamic_slice_in_dim(b_ada, chip * ADA_SHARD, ADA_SHARD, axis=1)
    parts = _allgather_small("gather_mod", _mod_part(c_all, w_ada[0], b_shard))
    mod = lax.dynamic_index_in_dim(parts[::2], me, axis=1, keepdims=False).reshape(1, 6 * d)

    w_in_t, m_w_in_t, v_w_in_t = [jnp.transpose(t, (0, 2, 1)) for t in (w_in, m_w_in, v_w_in)]
    w_alt = _to_aligned(_gather_shards(w_in_t[0].astype(BF16)).reshape(IN_COLS, d))

    sp = {n: a[n] for n in SMALL[1:]}
    sp["conv_w"] = conv_w_full
    shards = [w_out[0].astype(BF16), w_ff_in[0].astype(BF16), w_ff_out[0].astype(BF16)]
    dx, landed, small = _local_step(x[0], loss_target[0], mod, w_alt, shards, sp)

    names = ["mod"] + SMALL[1:]
    vec = _pack([small[n] for n in names] + [small["conv_w"].reshape(1, CONV_W * CONV_DIM), small["loss"]])
    every, total = _allgather_small("gather_small", vec, with_sum=True)
    widths = [6 * d] + [a[n].shape[1] for n in SMALL[1:]]
    offs = [0]
    for w in widths:
        offs.append(offs[-1] + w + (-w % LANES))
    g_conv_w_full = total[:, offs[-1]:offs[-1] + CONV_W * CONV_DIM].reshape(CONV_W, CONV_DIM)
    loss = total[0, offs[-1] + CONV_W * CONV_DIM]
    dmod_shard = lax.dynamic_slice_in_dim(every[:, 0, :6 * d], chip * ADA_SHARD, ADA_SHARD, axis=1)
    g_w_ada = _w_ada_grad(c_all.T, dmod_shard)
    g_conv_w = lax.dynamic_slice_in_dim(g_conv_w_full, chip * conv_shard, conv_shard, axis=1)

    mine = [_sum_blocks("dev_sum_%d" % i, p) for i, p in enumerate(landed)]
    other = _pair_swap(mine)

    grads, deltas, new_m, new_v = {}, {}, {}, {}
    paired = dict(w_in=(w_in_t, m_w_in_t, v_w_in_t), w_out=(w_out, m_w_out, v_w_out), w_ff_in=(w_ff_in, m_w_ff_in, v_w_ff_in),
                  w_ff_out=(w_ff_out, m_w_ff_out, v_w_ff_out))
    for i, (n, (w, m, v)) in enumerate(paired.items()):
        res = _adamw_pair("adamw_" + n, w, mine[i], other[i], m, v, ci, by_cols=n == "w_in")
        grads[n], deltas[n], new_m[n], new_v[n] = [jnp.transpose(t, (0, 2, 1)) for t in res] if n == "w_in" else res
    for n, g, (w, m, v) in (("w_ada", g_w_ada, (w_ada, m_w_ada, v_w_ada)), ("conv_w", g_conv_w, (conv_w, m_conv_w, v_conv_w))):
        grads[n], deltas[n], new_m[n], new_v[n] = _adamw("adamw_" + n, w, g, m, v)
    for n, res in zip(SMALL, _adamw_small(total, offs, [a[n] for n in SMALL], [ms[n] for n in SMALL], [vs[n] for n in SMALL])):
        grads[n], deltas[n], new_m[n], new_v[n] = res

    order = ["w_ada", "b_ada", "w_in", "conv_w", "conv_b", "dt_bias", "a_log", "d_skip", "ssm_norm_w", "f_bias", "attn_norm_w", "w_out",
             "ln1_g", "ln1_b", "w_ff_in", "w_ff_out", "ln2_g", "ln2_b"]
    return (loss, dx[None], *[grads[n] for n in order], *[deltas[n] for n in order], *[new_m[n] for n in order], *[new_v[n] for n in order])
```

```python
import functools

import jax
import jax.numpy as jnp
from jax import lax
from jax.experimental import pallas as pl
from jax.experimental.pallas import tpu as pltpu

F32, BF16 = jnp.float32, jnp.bfloat16

D_MODEL = 1024
N_HEADS = 16
HEAD_DIM = 64
N_PAIRS = N_HEADS // 2
SSM_GROUPS = 2
SSM_STATE = 128
CHUNK = 128
CONV_W = 4
CONV_DIM = 1536
D_FF = 4096
IN_COLS = 5664
ALPHA = 2.0 ** 0.25
LN_EPS = 1e-5
RMS_EPS = 1e-5
LANES = 128
SUBLANES = 8

AL_Z, AL_XS, AL_Q, AL_K, AL_V, AL_B, AL_C, AL_DTF = 0, 1024, 2048, 3072, 4096, 5120, 5376, 5632
AL_COLS = 5760
F_LANE = 16

ADAM_LR, ADAM_B1, ADAM_B2, ADAM_EPS, ADAM_WD, ADAM_STEP = 0.001, 0.9, 0.999, 1e-08, 0.01, 10

VMEM_LIMIT = 56 * 1024 * 1024
SEQ_TK = 4096
MESH = pl.DeviceIdType.MESH


def _params(sem=None):
    return pltpu.CompilerParams(dimension_semantics=sem, vmem_limit_bytes=VMEM_LIMIT)


def _sigmoid(x):
    return 1.0 / (1.0 + jnp.exp(-x))


def _silu(x):
    return x * _sigmoid(x)


def _softplus(x):
    return jnp.maximum(x, 0.0) + jnp.log(1.0 + jnp.exp(-jnp.abs(x)))


def _split3(a):
    hi = a.astype(BF16)
    r = a - hi.astype(F32)
    mid = r.astype(BF16)
    lo = (r - mid.astype(F32)).astype(BF16)
    return hi, mid, lo


def _dot(a, b, dims=((1,), (0,))):
    return lax.dot_general(a, b, (dims, ((), ())), preferred_element_type=F32)


NN, NT, TN = ((1,), (0,)), ((1,), (1,)), ((0,), (0,))


def _dot3(t, a):
    hi, mid, lo = _split3(a)
    return _dot(t, hi) + _dot(t, mid) + _dot(t, lo)


def _matmul(name, a, b, *, dims=NN, out_dtype=F32, tm=1024, tn=1024, tk=1024, by_chip=None, epi=None, carry=()):
    if dims == NN:
        (m, k), n = a.shape, b.shape[1]
    elif dims == NT:
        (m, k), n = a.shape, b.shape[0]
    else:
        (k, m), n = a.shape, b.shape[1]
    if by_chip == "rows":
        tm = min(tm, m // 4)
    if by_chip == "cols":
        tn = min(tn, n // 4)
    tm, tn, tk = min(tm, m), min(tn, n), min(tk, k)
    assert m % tm == 0 and n % tn == 0 and k % tk == 0, (name, m, n, k, tm, tn, tk)
    nk = k // tk
    if by_chip == "rows":
        per = m // 4 // tm
        out_spec = pl.BlockSpec((None, tm, tn), lambda i, j, l: (i // per, i % per, j))
        out_shape = jax.ShapeDtypeStruct((4, m // 4, n), out_dtype)
    elif by_chip == "cols":
        per = n // 4 // tn
        out_spec = pl.BlockSpec((None, tm, tn), lambda i, j, l: (j // per, i, j % per))
        out_shape = jax.ShapeDtypeStruct((4, m, n // 4), out_dtype)
    else:
        out_spec = pl.BlockSpec((tm, tn), lambda i, j, l: (i, j))
        out_shape = jax.ShapeDtypeStruct((m, n), out_dtype)
    a_spec = pl.BlockSpec((tk, tm), lambda i, j, l: (l, i)) if dims == TN else pl.BlockSpec((tm, tk), lambda i, j, l: (i, l))
    b_spec = pl.BlockSpec((tn, tk), lambda i, j, l: (j, l)) if dims == NT else pl.BlockSpec((tk, tn), lambda i, j, l: (l, j))

    tile = pl.BlockSpec((tm, tn), lambda i, j, l: (i, j))
    in_specs, args, out_specs, out_shape = [a_spec, b_spec], [a, b], [out_spec], [out_shape]
    fn, n_tiles, n_sums = None, 1, 0
    if epi is not None:
        fn, fulls, vecs, outs, sums = epi
        assert by_chip is None and (not sums or n == tn), name
        in_specs = in_specs + [tile] * len(fulls) + [pl.BlockSpec((1, tn), lambda i, j, l: (0, j))] * len(vecs)
        args = args + list(fulls) + list(vecs)
        flipped = pl.BlockSpec((tn, tm), lambda i, j, l: (j, i))
        out_specs = [flipped if isinstance(dt, tuple) else tile for dt in outs] + [pl.BlockSpec((1, w), lambda i, j, l: (0, 0)) for w in sums]
        out_shape = [jax.ShapeDtypeStruct((n, m), dt[1]) if isinstance(dt, tuple) else jax.ShapeDtypeStruct((m, n), dt) for dt in outs]
        out_shape += [jax.ShapeDtypeStruct((1, w), F32) for w in sums]
        n_tiles, n_sums = len(outs), len(sums)
    n_in, n_out, n_c = len(args), len(out_specs), len(carry)
    scratch = [pltpu.VMEM((tm, tn) if nk > 1 else (SUBLANES, LANES), F32)]
    if n_c:
        in_specs, args = in_specs + [ANY] * n_c, args + list(carry)
        out_specs = out_specs + [ANY] * n_c
        out_shape = out_shape + [jax.ShapeDtypeStruct(g.shape, g.dtype) for g in carry]
        scratch = scratch + _exchange_sems(n_c)
    gm, gn = m // tm, n // tn

    def body(*refs):
        a_ref, b_ref = refs[:2]
        ins, outs = refs[2:n_in], refs[n_in + n_c:n_in + n_c + n_out]
        acc_ref = refs[n_in + 2 * n_c + n_out]
        i, j, l = pl.program_id(0), pl.program_id(1), pl.program_id(2)
        if n_c:
            start, wait = _scatter_plan(refs[n_in:n_in + n_c], refs[n_in + n_c + n_out:n_in + 2 * n_c + n_out], *refs[n_in + 2 * n_c + n_out + 1:])
            pl.when((i == 0) & (j == 0) & (l == 0))(start)
        part = _dot(a_ref[...].astype(BF16), b_ref[...].astype(BF16), dims)

        def finish(res):
            if fn is None:
                outs[0][...] = res.astype(outs[0].dtype)
                return
            tiles, colsums = fn(res, *[r[...] for r in ins])
            for r, val in zip(outs[:n_tiles], tiles):
                r[...] = val.astype(r.dtype)
            if n_sums:
                @pl.when(i == 0)
                def _():
                    for r in outs[n_tiles:]:
                        r[...] = jnp.zeros_like(r)
                for r, val in zip(outs[n_tiles:], colsums):
                    r[...] += val

        if nk == 1:
            finish(part)
        else:
            @pl.when(l == 0)
            def _():
                acc_ref[...] = part

            @pl.when((l > 0) & (l < nk - 1))
            def _():
                acc_ref[...] += part

            @pl.when(l == nk - 1)
            def _():
                finish(acc_ref[...] + part)

        if n_c:
            pl.when((i == gm - 1) & (j == gn - 1) & (l == nk - 1))(wait)

    res = pl.pallas_call(
        body, name=name, grid=(gm, gn, nk),
        in_specs=in_specs, out_specs=out_specs, out_shape=out_shape, scratch_shapes=scratch,
        compiler_params=_params(("arbitrary",) * 3 if n_c or n_sums else ("parallel", "parallel", "arbitrary")),
    )(*args)
    return res[0] if len(res) == 1 else res


def _rowwise(name, fn, fulls, vecs, out_fulls, out_vecs, tr=256):
    fulls = [f if isinstance(f, tuple) else (f, f.shape[1], 0) for f in fulls]
    s = fulls[0][0].shape[0]
    tr = min(tr, s)
    out_fulls = [o if len(o) == 3 else (*o, (o[0], 0, None)) for o in out_fulls]
    into = [(k, slab[2]) for k, (_, _, slab) in enumerate(out_fulls) if slab[2] is not None]
    nf, nv, nof, nov = len(fulls), len(vecs), len(out_fulls), len(out_vecs)
    in_specs = [pl.BlockSpec((tr, w), functools.partial(lambda i, cb: (i, cb), cb=cb)) for (_, w, cb) in fulls]
    in_specs += [pl.BlockSpec(v.shape, lambda i: (0, 0)) for v in vecs] + [ANY] * len(into)
    out_shape = [jax.ShapeDtypeStruct((s, slab[0]), dt) for (_, dt, slab) in out_fulls] + [jax.ShapeDtypeStruct((1, w), F32) for w in out_vecs]
    out_specs = [pl.BlockSpec((tr, w), functools.partial(lambda i, cb: (i, cb), cb=slab[1])) for (w, _, slab) in out_fulls]
    out_specs += [pl.BlockSpec((1, w), lambda i: (0, 0)) for w in out_vecs]

    def body(*refs):
        outs = refs[nf + nv + len(into):]
        of, ov = fn(*[r[...] for r in refs[:nf + nv]])
        for r, val in zip(outs[:nof], of):
            r[...] = val.astype(r.dtype)
        if nov:
            @pl.when(pl.program_id(0) == 0)
            def _():
                for r in outs[nof:]:
                    r[...] = jnp.zeros_like(r)
            for r, val in zip(outs[nof:], ov):
                r[...] += val

    res = pl.pallas_call(
        body, name=name, grid=(s // tr,), in_specs=in_specs, out_specs=out_specs, out_shape=out_shape,
        input_output_aliases={nf + nv + pos: k for pos, (k, _) in enumerate(into)},
        compiler_params=_params(("arbitrary",)),
    )(*[f[0] for f in fulls], *vecs, *[buf for _, buf in into])
    return res[:nof], res[nof:]


def _colsum(x):
    return jnp.sum(x, axis=0, keepdims=True)


def _rowmean(x):
    return jnp.mean(x, axis=-1, keepdims=True)


CONV_CB = 512
CONV_TR = 512


def _shift_down(u, halo, j):
    if j == 0:
        return u
    ru = pltpu.roll(u, j, 0)
    row8 = lax.broadcasted_iota(jnp.int32, halo.shape, 0)
    top = jnp.where(row8 < j, pltpu.roll(halo, j, 0), ru[:SUBLANES])
    return jnp.concatenate([top, ru[SUBLANES:]], axis=0)


def _shift_up(d, halo, j):
    if j == 0:
        return d
    tr = d.shape[0]
    rd = pltpu.roll(d, tr - j, 0)
    row8 = lax.broadcasted_iota(jnp.int32, halo.shape, 0)
    bot = jnp.where(row8 >= SUBLANES - j, pltpu.roll(halo, SUBLANES - j, 0), rd[tr - SUBLANES:])
    return jnp.concatenate([rd[:tr - SUBLANES], bot], axis=0)


def _conv_col(cb):
    return jnp.where(cb < 2, AL_XS // CONV_CB + cb, AL_B // CONV_CB)


def _conv_specs(s, tr):
    per8 = tr // SUBLANES
    blk = pl.BlockSpec((tr, CONV_CB), lambda cb, i: (i, _conv_col(cb)))
    prev = pl.BlockSpec((SUBLANES, CONV_CB), lambda cb, i: (jnp.maximum(i * per8 - 1, 0), _conv_col(cb)))
    return blk, prev


def _conv_pre(u, halo, w_ref, b_ref, first):
    halo = jnp.where(first, 0.0, halo)
    acc = b_ref[...] + w_ref[CONV_W - 1:CONV_W, :] * u
    shifted = [u]
    for j in range(1, CONV_W):
        sh = _shift_down(u, halo, j)
        shifted.append(sh)
        acc = acc + w_ref[CONV_W - 1 - j:CONV_W - j, :] * sh
    return acc, shifted


def _conv_fwd(proj, conv_w, conv_b):
    s = proj.shape[0]
    tr = min(CONV_TR, s)
    blk, prev = _conv_specs(s, tr)

    def body(u_ref, h_ref, w_ref, b_ref, o_ref):
        pre, _ = _conv_pre(u_ref[...], h_ref[...], w_ref, b_ref, pl.program_id(1) == 0)
        o_ref[...] = _silu(pre)

    return pl.pallas_call(
        body, name="conv_fwd", grid=(CONV_DIM // CONV_CB, s // tr),
        in_specs=[blk, prev, pl.BlockSpec((CONV_W, CONV_CB), lambda cb, i: (0, cb)), pl.BlockSpec((1, CONV_CB), lambda cb, i: (0, cb))],
        out_specs=pl.BlockSpec((tr, CONV_CB), lambda cb, i: (i, cb)),
        out_shape=jax.ShapeDtypeStruct((s, CONV_DIM), F32),
        compiler_params=_params(("parallel", "parallel")),
    )(proj, proj, conv_w, conv_b)


def _conv_bwd_pre(proj, conv_w, conv_b, dxc):
    s = proj.shape[0]
    tr = min(CONV_TR, s)
    blk, prev = _conv_specs(s, tr)

    def body(u_ref, h_ref, w_ref, b_ref, d_ref, dpre_ref, dw_ref, db_ref):
        i = pl.program_id(1)
        pre, shifted = _conv_pre(u_ref[...], h_ref[...], w_ref, b_ref, i == 0)
        sg = _sigmoid(pre)
        dpre = d_ref[...] * (sg * (1.0 + pre * (1.0 - sg)))
        dpre_ref[...] = dpre

        @pl.when(i == 0)
        def _():
            dw_ref[...] = jnp.zeros_like(dw_ref)
            db_ref[...] = jnp.zeros_like(db_ref)

        db_ref[...] += _colsum(dpre)
        for j in range(CONV_W):
            dw_ref[CONV_W - 1 - j:CONV_W - j, :] += _colsum(dpre * shifted[j])

    own = pl.BlockSpec((tr, CONV_CB), lambda cb, i: (i, cb))
    wspec = pl.BlockSpec((CONV_W, CONV_CB), lambda cb, i: (0, cb))
    bspec = pl.BlockSpec((1, CONV_CB), lambda cb, i: (0, cb))
    return pl.pallas_call(
        body, name="conv_bwd_pre", grid=(CONV_DIM // CONV_CB, s // tr),
        in_specs=[blk, prev, wspec, bspec, own], out_specs=[own, wspec, bspec],
        out_shape=[jax.ShapeDtypeStruct((s, CONV_DIM), F32), jax.ShapeDtypeStruct((CONV_W, CONV_DIM), F32),
                   jax.ShapeDtypeStruct((1, CONV_DIM), F32)],
        compiler_params=_params(("parallel", "arbitrary")),
    )(proj, proj, conv_w, conv_b, dxc)


def _conv_bwd_in(dpre, conv_w, dproj):
    s = dpre.shape[0]
    tr = min(CONV_TR, s)
    per8 = tr // SUBLANES
    last8 = s // SUBLANES - 1
    nb = s // tr

    def body(d_ref, n_ref, w_ref, _, o_ref):
        d = d_ref[...]
        halo = jnp.where(pl.program_id(1) == nb - 1, 0.0, n_ref[...])
        acc = w_ref[CONV_W - 1:CONV_W, :] * d
        for j in range(1, CONV_W):
            acc = acc + w_ref[CONV_W - 1 - j:CONV_W - j, :] * _shift_up(d, halo, j)
        o_ref[...] = acc.astype(o_ref.dtype)

    own = pl.BlockSpec((tr, CONV_CB), lambda cb, i: (i, cb))
    nxt = pl.BlockSpec((SUBLANES, CONV_CB), lambda cb, i: (jnp.minimum((i + 1) * per8, last8), cb))
    return pl.pallas_call(
        body, name="conv_bwd_in", grid=(CONV_DIM // CONV_CB, nb),
        in_specs=[own, nxt, pl.BlockSpec((CONV_W, CONV_CB), lambda cb, i: (0, cb)), ANY],
        out_specs=pl.BlockSpec((tr, CONV_CB), lambda cb, i: (i, _conv_col(cb))),
        out_shape=jax.ShapeDtypeStruct(dproj.shape, dproj.dtype), input_output_aliases={3: 0},
        compiler_params=_params(("parallel", "parallel")),
    )(dpre, dpre, conv_w, dproj)


XC_B, XC_C = 1024, 1280


def _tile_iotas():
    row = lax.broadcasted_iota(jnp.int32, (CHUNK, LANES), 0)
    lane = lax.broadcasted_iota(jnp.int32, (CHUNK, LANES), 1)
    return row, lane


def _ssd_scalars(dtf_ref, bias_ref, alog_ref, row, lane):
    head = lane[:1] < N_HEADS
    raw = dtf_ref[...] + bias_ref[...]
    dt = _softplus(raw)
    a_neg = jnp.where(head, -jnp.exp(alog_ref[...]), 0.0)
    a = dt * a_neg
    tril = (row >= lane).astype(BF16)
    s = _dot3(tril, a)
    return raw, dt, a_neg, s


def _pair(v, j, lo):
    return jnp.where(lo, v[:, 2 * j:2 * j + 1], v[:, 2 * j + 1:2 * j + 2])


def _head_sum(x, lo, hh):
    return jnp.sum(jnp.where(lo == (hh == 0), x, 0.0), axis=1, keepdims=True)


def _decay_masks(s, st, h, row, lane):
    s_col = jnp.broadcast_to(s[:, h:h + 1], (CHUNK, LANES))
    s_row = jnp.broadcast_to(st[h:h + 1, :], (CHUNK, LANES))
    lm = jnp.where(row >= lane, jnp.exp(s_col - s_row), 0.0)
    lmt = jnp.where(row <= lane, jnp.exp(s_row - s_col), 0.0)
    return lm, lmt


def _gated_norm(y, z, w):
    g = y * _silu(z)
    return g * lax.rsqrt(_rowmean(g * g) + RMS_EPS) * w


def _ssd_fwd(xc_all, proj, dt_bias_l, a_log_l, d_exp, norm_w):
    s_len = xc_all.shape[0]
    nc = s_len // CHUNK

    def body(x_ref, dtf_ref, bias_ref, alog_ref, dexp_ref, z_ref, w_ref, y_ref, prevs_ref, ymix_ref, state_ref):
        @pl.when(pl.program_id(0) == 0)
        def _():
            state_ref[...] = jnp.zeros_like(state_ref)

        row, lane = _tile_iotas()
        lo = lane < HEAD_DIM
        _, dt, _, s = _ssd_scalars(dtf_ref, bias_ref, alog_ref, row, lane)
        tot = s[CHUNK - 1:CHUNK, :]
        st = s.T
        for g in range(SSM_GROUPS):
            bg = x_ref[:, XC_B + g * SSM_STATE:XC_B + (g + 1) * SSM_STATE].astype(BF16)
            cg = x_ref[:, XC_C + g * SSM_STATE:XC_C + (g + 1) * SSM_STATE].astype(BF16)
            cb = _dot(cg, bg, NT)
            for j in range(g * 4, g * 4 + 4):
                xs_p = x_ref[:, j * LANES:(j + 1) * LANES]
                dt_p, s_p, tot_p = _pair(dt, j, lo), _pair(s, j, lo), _pair(tot, j, lo[:1])
                xc_p = xs_p * dt_p
                xc_b = xc_p.astype(BF16)
                yd = []
                for hh in range(2):
                    lm, _ = _decay_masks(s, st, 2 * j + hh, row, lane)
                    yd.append(_dot((cb * lm).astype(BF16), xc_b))
                prev = state_ref[j]
                prevs_ref[0, j] = prev
                yo = _dot(cg, prev.astype(BF16)) * jnp.exp(s_p)
                y_ref[:, j * LANES:(j + 1) * LANES] = jnp.where(lo, yd[0], yd[1]) + yo + dexp_ref[:, j * LANES:(j + 1) * LANES] * xs_p
                to_end = jnp.exp(tot_p - s_p)
                state_ref[j] = jnp.exp(tot_p) * prev + _dot(bg, (xc_p * to_end).astype(BF16), TN)
        ymix_ref[...] = _gated_norm(y_ref[...], z_ref[...], w_ref[...]).astype(ymix_ref.dtype)

    vec = lambda w: pl.BlockSpec((1, w), lambda c: (0, 0))
    rows = pl.BlockSpec((CHUNK, D_MODEL), lambda c: (c, 0))
    return pl.pallas_call(
        body, name="ssd_fwd", grid=(nc,),
        in_specs=[pl.BlockSpec((CHUNK, CONV_DIM), lambda c: (c, 0)), pl.BlockSpec((CHUNK, LANES), lambda c: (c, AL_DTF // LANES)),
                  vec(LANES), vec(LANES), vec(D_MODEL), pl.BlockSpec((CHUNK, D_MODEL), lambda c: (c, AL_Z // D_MODEL)), vec(D_MODEL)],
        out_specs=[rows, pl.BlockSpec((1, N_PAIRS, SSM_STATE, LANES), lambda c: (c, 0, 0, 0)), rows],
        out_shape=[jax.ShapeDtypeStruct((s_len, D_MODEL), F32), jax.ShapeDtypeStruct((nc, N_PAIRS, SSM_STATE, LANES), F32),
                   jax.ShapeDtypeStruct((s_len, 2 * D_MODEL), BF16)],
        scratch_shapes=[pltpu.VMEM((N_PAIRS, SSM_STATE, LANES), F32)],
        compiler_params=_params(("arbitrary",)),
    )(xc_all, proj, dt_bias_l, a_log_l, d_exp, proj, norm_w)


def _ssd_bwd(xc_all, proj, dt_bias_l, a_log_l, d_exp, prevs, y_ssd, dymix, norm_w):
    s_len = xc_all.shape[0]
    nc = s_len // CHUNK

    def body(x_ref, dtf_ref, bias_ref, alog_ref, dexp_ref, prevs_ref, y_ref, z_ref, dym_ref, w_ref,
             dx_ref, ddt_ref, da_ref, dd_ref, dbias_ref, dz_ref, dw_ref, dstate_ref):
        @pl.when(pl.program_id(0) == 0)
        def _():
            dstate_ref[...] = jnp.zeros_like(dstate_ref)
            da_ref[...] = jnp.zeros_like(da_ref)
            dd_ref[...] = jnp.zeros_like(dd_ref)
            dbias_ref[...] = jnp.zeros_like(dbias_ref)
            dw_ref[...] = jnp.zeros_like(dw_ref)

        y, z, dyo = y_ref[...], z_ref[...], dym_ref[...]
        sg = _sigmoid(z)
        sz = z * sg
        gated = y * sz
        rn = lax.rsqrt(_rowmean(gated * gated) + RMS_EPS)
        dg = _rms_bwd(dyo * w_ref[...], gated, rn)
        dy_full = dg * sz
        dz_ref[...] = (dg * y * (sg * (1.0 + z * (1.0 - sg)))).astype(dz_ref.dtype)
        dw_ref[...] += _colsum(dyo * gated * rn)

        row, lane = _tile_iotas()
        lo = lane < HEAD_DIM
        last = row == CHUNK - 1
        raw, dt, a_neg, s = _ssd_scalars(dtf_ref, bias_ref, alog_ref, row, lane)
        tot = s[CHUNK - 1:CHUNK, :]
        st = s.T
        ds_acc = jnp.zeros((CHUNK, LANES), F32)
        ddt_acc = jnp.zeros((CHUNK, LANES), F32)
        for g in range(SSM_GROUPS):
            bcol = slice(XC_B + g * SSM_STATE, XC_B + (g + 1) * SSM_STATE)
            ccol = slice(XC_C + g * SSM_STATE, XC_C + (g + 1) * SSM_STATE)
            bg = x_ref[:, bcol].astype(BF16)
            cg = x_ref[:, ccol].astype(BF16)
            cb = _dot(cg, bg, NT)
            cbt = _dot(bg, cg, NT)
            dcb = jnp.zeros((CHUNK, LANES), F32)
            dcbt = jnp.zeros((CHUNK, LANES), F32)
            db_acc = jnp.zeros((CHUNK, LANES), F32)
            dc_acc = jnp.zeros((CHUNK, LANES), F32)
            for j in range(g * 4, g * 4 + 4):
                cols = slice(j * LANES, (j + 1) * LANES)
                xs_p, dy_p = x_ref[:, cols], dy_full[:, cols]
                dt_p, s_p, tot_p = _pair(dt, j, lo), _pair(s, j, lo), _pair(tot, j, lo[:1])
                xc_p = xs_p * dt_p
                xc_b, dy_b = xc_p.astype(BF16), dy_p.astype(BF16)
                e_p, f_p, etot_p = jnp.exp(s_p), jnp.exp(tot_p - s_p), jnp.exp(tot_p)
                prev, dnext = prevs_ref[0, j], dstate_ref[j]
                prev_b, dnext_b = prev.astype(BF16), dnext.astype(BF16)
                dd_ref[:, cols] += _colsum(dy_p * xs_p)
                dxs_p = dexp_ref[:, cols] * dy_p
                cp = _dot(cg, prev_b)
                gy = (dy_p * e_p).astype(BF16)
                dc_acc += _dot(gy, prev_b, NT)
                dstate_ref[j] = etot_p * dnext + _dot(cg, gy, TN)
                de = dy_p * cp * e_p
                bds = _dot(bg, dnext_b)
                db_acc += _dot((xc_p * f_p).astype(BF16), dnext_b, NT)
                dxc_p = bds * f_p
                df = bds * xc_p * f_p
                dtot_p = _colsum(dnext * prev) * etot_p + _colsum(df)
                dsl = de - df + jnp.where(last, dtot_p, 0.0)
                for hh in range(2):
                    h = 2 * j + hh
                    mine = lo == (hh == 0)
                    lm, lmt = _decay_masks(s, st, h, row, lane)
                    dy_h = jnp.where(mine, dy_p, 0.0).astype(BF16)
                    xc_h = jnp.where(mine, xc_p, 0.0).astype(BF16)
                    dm = _dot(dy_h, xc_b, NT)
                    dmt = _dot(xc_h, dy_b, NT)
                    mt = cbt * lmt
                    dxc_p += _dot(mt.astype(BF16), dy_h)
                    dml, dmtl = dm * lm, dmt * lmt
                    ds_h = jnp.sum(dml * cb - dmtl * cbt + jnp.where(mine, dsl, 0.0), axis=1, keepdims=True)
                    ds_acc += jnp.where(lane == h, ds_h, 0.0)
                    dcb += dml
                    dcbt += dmtl
                    ddt_acc += jnp.where(lane == h, _head_sum(dxc_p * xs_p, lo, hh), 0.0)
                dx_ref[:, cols] = dxs_p + dxc_p * dt_p
            dx_ref[:, ccol] = dc_acc + _dot(dcb.astype(BF16), bg)
            dx_ref[:, bcol] = db_acc + _dot(dcbt.astype(BF16), cg)
        triu = (row <= lane).astype(BF16)
        da = _dot3(triu, ds_acc)
        ddt = ddt_acc + da * a_neg
        da_ref[...] += _colsum(da * dt) * a_neg[:1]
        ddt_raw = jnp.where(lane < N_HEADS, ddt * _sigmoid(raw), 0.0)
        dbias_ref[...] += _colsum(ddt_raw)
        ddt_ref[...] = ddt_raw

    rev = lambda c: nc - 1 - c
    vec = lambda w: pl.BlockSpec((1, w), lambda c: (0, 0))
    rows = lambda cb: pl.BlockSpec((CHUNK, D_MODEL), lambda c: (rev(c), cb))
    return pl.pallas_call(
        body, name="ssd_bwd", grid=(nc,),
        in_specs=[pl.BlockSpec((CHUNK, CONV_DIM), lambda c: (rev(c), 0)), pl.BlockSpec((CHUNK, LANES), lambda c: (rev(c), AL_DTF // LANES)),
                  vec(LANES), vec(LANES), vec(D_MODEL),
                  pl.BlockSpec((1, N_PAIRS, SSM_STATE, LANES), lambda c: (rev(c), 0, 0, 0)),
                  rows(0), rows(AL_Z // D_MODEL), rows(0), vec(D_MODEL)],
        out_specs=[pl.BlockSpec((CHUNK, CONV_DIM), lambda c: (rev(c), 0)), pl.BlockSpec((CHUNK, LANES), lambda c: (rev(c), 0)),
                   vec(LANES), vec(D_MODEL), vec(LANES), rows(AL_Z // D_MODEL), vec(D_MODEL)],
        out_shape=[jax.ShapeDtypeStruct((s_len, CONV_DIM), F32), jax.ShapeDtypeStruct((s_len, LANES), F32),
                   jax.ShapeDtypeStruct((1, LANES), F32), jax.ShapeDtypeStruct((1, D_MODEL), F32), jax.ShapeDtypeStruct((1, LANES), F32),
                   jax.ShapeDtypeStruct((s_len, AL_COLS), BF16), jax.ShapeDtypeStruct((1, D_MODEL), F32)],
        scratch_shapes=[pltpu.VMEM((N_PAIRS, SSM_STATE, LANES), F32)],
        compiler_params=_params(("arbitrary",)),
    )(xc_all, proj, dt_bias_l, a_log_l, d_exp, prevs, y_ssd, proj, dymix, norm_w)


AUG_LANES = 6


def _aug_base(hh):
    return HEAD_DIM if hh == 0 else 0


NEG = -1e30
ATT_T = 512


def _fox_cum(proj, f_bias_l):
    s_len = proj.shape[0]
    nc = s_len // CHUNK

    def body(dtf_ref, fb_ref, cum_ref):
        row, lane = _tile_iotas()
        tril = (row >= lane).astype(BF16)
        spread = [(jnp.where(lane == AUG_LANES * row + i, 1.0, 0.0) - jnp.where(lane == AUG_LANES * row + 3 + i, 1.0, 0.0)).astype(BF16)
                  for i in range(3)]

        def step(c, carry):
            rows = pl.ds(pl.multiple_of(c * CHUNK, CHUNK), CHUNK)
            lf = -_softplus(-(dtf_ref[rows, :] + fb_ref[...]))
            lf = jnp.where(lane < N_HEADS, pltpu.roll(lf, LANES - F_LANE, 1), 0.0)
            cs = _dot3(tril, lf) + carry
            parts = _split3(cs)
            cum_ref[rows, :] = _dot(parts[0], spread[0]) + _dot(parts[1], spread[1]) + _dot(parts[2], spread[2])
            return cs[CHUNK - 1:CHUNK, :]

        lax.fori_loop(0, nc, step, jnp.zeros((1, LANES), F32))

    return pl.pallas_call(
        body, name="fox_cum", grid=(1,),
        in_specs=[pl.BlockSpec((s_len, LANES), lambda i: (0, AL_DTF // LANES)), pl.BlockSpec((1, LANES), lambda i: (0, 0))],
        out_specs=pl.BlockSpec((s_len, LANES), lambda i: (0, 0)),
        out_shape=jax.ShapeDtypeStruct((s_len, LANES), F32),
        compiler_params=_params(("arbitrary",)),
    )(proj, f_bias_l)


def _fox_cum_bwd(dcum, proj, f_bias_l, ddt_tile, dproj):
    s_len = proj.shape[0]
    nc = s_len // CHUNK

    def body(dcum_ref, dtf_ref, fb_ref, ddt_ref, _, out_ref, dfb_ref):
        row, lane = _tile_iotas()
        triu = (row <= lane).astype(BF16)
        is_f = (lane >= F_LANE) & (lane < F_LANE + N_HEADS)

        def step(t, carry):
            run, dfb = carry
            rows = pl.ds(pl.multiple_of((nc - 1 - t) * CHUNK, CHUNK), CHUNK)
            rc = _dot3(triu, dcum_ref[rows, :]) + run
            sg = _sigmoid(-(dtf_ref[rows, :] + fb_ref[...]))
            df = jnp.where(is_f, pltpu.roll(rc, F_LANE, 1) * sg, 0.0)
            out_ref[rows, :] = (df + ddt_ref[rows, :]).astype(out_ref.dtype)
            return rc[0:1, :], dfb + _colsum(df)

        _, dfb = lax.fori_loop(0, nc, step, (jnp.zeros((1, LANES), F32), jnp.zeros((1, LANES), F32)))
        dfb_ref[...] = dfb

    whole = pl.BlockSpec((s_len, LANES), lambda i: (0, 0))
    dtf_cols = pl.BlockSpec((s_len, LANES), lambda i: (0, AL_DTF // LANES))
    vec = pl.BlockSpec((1, LANES), lambda i: (0, 0))
    return pl.pallas_call(
        body, name="fox_cum_bwd", grid=(1,),
        in_specs=[whole, dtf_cols, vec, whole, ANY], out_specs=[dtf_cols, vec],
        out_shape=[jax.ShapeDtypeStruct(dproj.shape, dproj.dtype), jax.ShapeDtypeStruct((1, LANES), F32)],
        input_output_aliases={4: 0}, compiler_params=_params(("arbitrary",)),
    )(dcum, proj, f_bias_l, ddt_tile, dproj)


def _attn_prep(proj, cum):
    s_len = proj.shape[0]
    tr = min(256, s_len)

    def body(q_ref, k_ref, v_ref, cum_ref, qa_ref, ka_ref, vb_ref):
        lane = lax.broadcasted_iota(jnp.int32, (tr, LANES), 1)
        lo = lane < HEAD_DIM
        c = cum_ref[...]
        for p in range(N_PAIRS):
            cols = slice(p * LANES, (p + 1) * LANES)
            q, k = q_ref[:, cols] * (HEAD_DIM ** -0.5), k_ref[:, cols]
            for hh in range(2):
                base = _aug_base(hh)
                r = pltpu.roll(c, (base - AUG_LANES * (2 * p + hh)) % LANES, 1)
                first = (lane >= base) & (lane < base + 3)
                second = (lane >= base + 3) & (lane < base + AUG_LANES)
                mine = lo == (hh == 0)
                qa_ref[2 * p + hh] = jnp.where(mine, q, jnp.where(first, r, jnp.where(second, 1.0, 0.0))).astype(BF16)
                ka_ref[2 * p + hh] = jnp.where(mine, k, jnp.where(first, 1.0, jnp.where(second, r, 0.0))).astype(BF16)
        vb_ref[...] = v_ref[...].astype(BF16)

    assert AL_Q % D_MODEL == 0 and AL_K % D_MODEL == 0 and AL_V % D_MODEL == 0
    slab = lambda col0: pl.BlockSpec((tr, D_MODEL), lambda i: (i, col0 // D_MODEL))
    heads = pl.BlockSpec((N_HEADS, tr, LANES), lambda i: (0, i, 0))
    return pl.pallas_call(
        body, name="attn_prep", grid=(s_len // tr,),
        in_specs=[slab(AL_Q), slab(AL_K), slab(AL_V), pl.BlockSpec((tr, LANES), lambda i: (i, 0))],
        out_specs=[heads, heads, pl.BlockSpec((tr, D_MODEL), lambda i: (i, 0))],
        out_shape=[jax.ShapeDtypeStruct((N_HEADS, s_len, LANES), BF16), jax.ShapeDtypeStruct((N_HEADS, s_len, LANES), BF16),
                   jax.ShapeDtypeStruct((s_len, D_MODEL), BF16)],
        compiler_params=_params(("parallel",)),
    )(proj, proj, proj, cum)


def _attn_fwd(qa, ka, vb, halves):
    s_len = vb.shape[0]
    t = min(ATT_T, s_len)
    nq = s_len // t
    n = len(halves)

    def body(qa_ref, ka_ref, vb_ref, *rest):
        o_ref, lse_ref = rest[n:n + 2]
        start, finish = _gather_plan(rest[:n], rest[n + 2:2 * n + 2], *rest[2 * n + 2:])
        i = pl.program_id(1)
        pl.when((pl.program_id(0) == 0) & (i == 0))(start)
        row = lax.broadcasted_iota(jnp.int32, (t, t), 0)
        col = lax.broadcasted_iota(jnp.int32, (t, t), 1)
        lo = lax.broadcasted_iota(jnp.int32, (t, LANES), 1) < HEAD_DIM
        qs = (qa_ref[0], qa_ref[1])

        def block(j, carry, masked):
            rows = pl.ds(pl.multiple_of(j * t, t), t)
            v = vb_ref[rows, :]
            new = []
            for hh in range(2):
                m, l, acc = carry[hh]
                s = _dot(qs[hh], ka_ref[hh, rows, :], NT)
                if masked:
                    s = jnp.where(row >= col, s, NEG)
                m_new = jnp.maximum(m, jnp.max(s, axis=1, keepdims=True))
                alpha = jnp.exp(m - m_new)
                p = jnp.exp(s - m_new)
                new.append((m_new, alpha * l + jnp.sum(p, axis=1, keepdims=True), alpha * acc + _dot(p.astype(BF16), v)))
            return tuple(new)

        init = (jnp.full((t, 1), NEG, F32), jnp.zeros((t, 1), F32), jnp.zeros((t, LANES), F32))
        carry = lax.fori_loop(0, i, functools.partial(block, masked=False), (init, init))
        (m0, l0, acc0), (m1, l1, acc1) = block(i, carry, True)
        o_ref[...] = jnp.where(lo, acc0 / l0, acc1 / l1)
        lse_ref[...] = jnp.where(lo, m0 + jnp.log(l0), m1 + jnp.log(l1))
        pl.when((pl.program_id(0) == N_PAIRS - 1) & (i == nq - 1))(finish)

    out = pl.BlockSpec((t, LANES), lambda p, i: (i, p))
    res = pl.pallas_call(
        body, name="attn_fwd", grid=(N_PAIRS, nq),
        in_specs=[pl.BlockSpec((2, t, LANES), lambda p, i: (p, i, 0)), pl.BlockSpec((2, s_len, LANES), lambda p, i: (p, 0, 0)),
                  pl.BlockSpec((s_len, LANES), lambda p, i: (0, p))] + [ANY] * n,
        out_specs=[out, out] + [ANY] * n,
        out_shape=[jax.ShapeDtypeStruct((s_len, D_MODEL), F32), jax.ShapeDtypeStruct((s_len, D_MODEL), F32)]
        + [jax.ShapeDtypeStruct((N_CHIPS, *h.shape), h.dtype) for h in halves],
        scratch_shapes=_exchange_sems(n),
        compiler_params=_params(("arbitrary", "arbitrary")),
    )(qa, ka, vb, *halves)
    return res[0], res[1], res[2:]


def _attn_bwd(qa, ka, vb, o, lse, do, parts, dproj):
    s_len = vb.shape[0]
    t = min(ATT_T, s_len)
    nq = s_len // t
    n = len(parts)

    def body(qa_ref, ka_ref, vb_ref, o_ref, lse_ref, do_ref, *rest):
        dqa_ref, dka_ref, dv_ref = rest[n + 1:n + 4]
        start, finish = _reduce_plan(rest[:n], rest[n + 4:2 * n + 4], *rest[2 * n + 4:])
        j = pl.program_id(1)
        pl.when((pl.program_id(0) == 0) & (j == 0))(start)

        @pl.when(j == 0)
        def _():
            dqa_ref[...] = jnp.zeros_like(dqa_ref)

        row = lax.broadcasted_iota(jnp.int32, (t, t), 0)
        col = lax.broadcasted_iota(jnp.int32, (t, t), 1)
        lo = lax.broadcasted_iota(jnp.int32, (t, LANES), 1) < HEAD_DIM
        v = vb_ref[...]
        ks = (ka_ref[0], ka_ref[1])

        def block(i, carry, masked):
            dk, dv = list(carry[:2]), carry[2]
            rows = pl.ds(pl.multiple_of(i * t, t), t)
            do_p, o_p, lse_p = do_ref[rows, :], o_ref[rows, :], lse_ref[rows, :]
            for hh in range(2):
                q = qa_ref[hh, rows, :]
                do_h = jnp.where(lo == (hh == 0), do_p, 0.0)
                delta = jnp.sum(do_h * o_p, axis=1, keepdims=True)
                s = _dot(q, ks[hh], NT)
                if masked:
                    s = jnp.where(row >= col, s, NEG)
                p = jnp.exp(s - lse_p[:, hh * HEAD_DIM:hh * HEAD_DIM + 1])
                do_b = do_h.astype(BF16)
                ds = (p * (_dot(do_b, v, NT) - delta)).astype(BF16)
                dv = dv + _dot(p.astype(BF16), do_b, TN)
                dk[hh] = dk[hh] + _dot(ds, q, TN)
                dqa_ref[hh, rows, :] += _dot(ds, ks[hh])
            return dk[0], dk[1], dv

        zero = jnp.zeros((t, LANES), F32)
        carry = block(j, (zero, zero, zero), True)
        dk0, dk1, dv = lax.fori_loop(j + 1, nq, functools.partial(block, masked=False), carry)
        dka_ref[0] = dk0
        dka_ref[1] = dk1
        dv_ref[...] = dv.astype(dv_ref.dtype)
        pl.when((pl.program_id(0) == N_PAIRS - 1) & (j == nq - 1))(finish)

    whole_pair = pl.BlockSpec((2, s_len, LANES), lambda p, j: (p, 0, 0))
    blk_pair = pl.BlockSpec((2, t, LANES), lambda p, j: (p, j, 0))
    whole_cols = pl.BlockSpec((s_len, LANES), lambda p, j: (0, p))
    blk_cols = pl.BlockSpec((t, LANES), lambda p, j: (j, p))
    res = pl.pallas_call(
        body, name="attn_bwd", grid=(N_PAIRS, nq),
        in_specs=[whole_pair, blk_pair, blk_cols, whole_cols, whole_cols, whole_cols] + [ANY] * (n + 1),
        out_specs=[whole_pair, blk_pair, pl.BlockSpec((t, LANES), lambda p, j: (j, AL_V // LANES + p))] + [ANY] * n,
        out_shape=[jax.ShapeDtypeStruct((N_HEADS, s_len, LANES), F32), jax.ShapeDtypeStruct((N_HEADS, s_len, LANES), F32),
                   jax.ShapeDtypeStruct(dproj.shape, dproj.dtype)]
        + [jax.ShapeDtypeStruct((N_DEV, g.shape[1] // 2, g.shape[2]), g.dtype) for g in parts],
        scratch_shapes=_exchange_sems(n), input_output_aliases={6 + n: 2},
        compiler_params=_params(("arbitrary", "arbitrary")),
    )(qa, ka, vb, o, lse, do, *parts, dproj)
    return res[0], res[1], res[2], res[3:]


def _attn_post(dqa, dka, dproj):
    s_len = dqa.shape[1]
    tr = min(256, s_len)
    assert AL_K == AL_Q + D_MODEL and AL_Q % (2 * D_MODEL) == 0

    def body(dqa_ref, dka_ref, _, dqk_ref, dcum_ref):
        lane = lax.broadcasted_iota(jnp.int32, (tr, LANES), 1)
        lo = lane < HEAD_DIM
        dcum = jnp.zeros((tr, LANES), F32)
        for p in range(N_PAIRS):
            a0, a1, b0, b1 = dqa_ref[2 * p], dqa_ref[2 * p + 1], dka_ref[2 * p], dka_ref[2 * p + 1]
            dq = jnp.where(lo, a0, a1) * (HEAD_DIM ** -0.5)
            dqk_ref[:, p * LANES:(p + 1) * LANES] = dq.astype(dqk_ref.dtype)
            dqk_ref[:, D_MODEL + p * LANES:D_MODEL + (p + 1) * LANES] = jnp.where(lo, b0, b1).astype(dqk_ref.dtype)
            for hh, (a, b) in enumerate(((a0, b0), (a1, b1))):
                base = _aug_base(hh)
                dcum = dcum + jnp.where(lane == 2 * p + hh, a[:, base:base + 1] - b[:, base + 3:base + 4], 0.0)
        dcum_ref[...] = dcum

    heads = pl.BlockSpec((N_HEADS, tr, LANES), lambda i: (0, i, 0))
    return pl.pallas_call(
        body, name="attn_post", grid=(s_len // tr,),
        in_specs=[heads, heads, ANY],
        out_specs=[pl.BlockSpec((tr, 2 * D_MODEL), lambda i: (i, AL_Q // (2 * D_MODEL))), pl.BlockSpec((tr, LANES), lambda i: (i, 0))],
        out_shape=[jax.ShapeDtypeStruct(dproj.shape, dproj.dtype), jax.ShapeDtypeStruct((s_len, LANES), F32)],
        input_output_aliases={2: 0}, compiler_params=_params(("parallel",)),
    )(dqa, dka, dproj)


def _ln_stats(r):
    mu = _rowmean(r)
    xc = r - mu
    rstd = lax.rsqrt(_rowmean(xc * xc) + LN_EPS)
    return xc * rstd, rstd


def _ln_bwd(dxh, xh, rstd):
    return rstd * (dxh - _rowmean(dxh) - xh * _rowmean(dxh * xh))


def _rms_bwd(dgn, g, r):
    return r * dgn - (r * r * r) * g * _rowmean(dgn * g)


def _to_aligned(wt):
    out = jnp.zeros((AL_COLS, wt.shape[1]), wt.dtype)
    for dst, (lo, hi) in ((0, (0, 2048)), (AL_Q, (2576, 5648)), (AL_B, (2048, 2560)), (AL_DTF, (2560, 2576)), (AL_DTF + 16, (5648, 5664))):
        out = lax.dynamic_update_slice_in_dim(out, wt[lo:hi], dst, axis=0)
    return out


def _from_aligned(gt):
    out = jnp.zeros((IN_COLS, gt.shape[1]), gt.dtype)
    for dst, (lo, hi) in ((0, (0, AL_Q)), (2048, (AL_B, AL_DTF)), (2560, (AL_DTF, AL_DTF + 16)), (2576, (AL_Q, AL_B)),
                          (5648, (AL_DTF + 16, AL_DTF + 32))):
        out = lax.dynamic_update_slice_in_dim(out, gt[lo:hi], dst, axis=0)
    return out


def _lanes(v, at=0):
    return jnp.pad(v, ((0, 0), (at, LANES - at - v.shape[1])))


def _local_step(x, tgt, mod, w_alt, halves, sp):
    d = D_MODEL
    sh1, sc1, g1, sh2, sc2, g2 = [mod[:, i * d:(i + 1) * d] for i in range(6)]
    dt_bias_l, a_log_l, f_bias_l = _lanes(sp["dt_bias"]), _lanes(sp["a_log"]), _lanes(sp["f_bias"], F_LANE)
    d_exp = jnp.repeat(sp["d_skip"], HEAD_DIM, axis=1)

    (h1,), _ = _rowwise("mod1", lambda x, sc, sh: ([x * (1.0 + sc) + sh], []), [x], [sc1, sh1], [(d, BF16)], [])
    proj = _matmul("proj", h1, w_alt, dims=NT, tn=1152)
    xc_all = _conv_fwd(proj, sp["conv_w"], sp["conv_b"])
    y_ssd, prevs, y_mix = _ssd_fwd(xc_all, proj, dt_bias_l, a_log_l, d_exp, sp["ssm_norm_w"])
    cum = _fox_cum(proj, f_bias_l)
    qa, ka, vb = _attn_prep(proj, cum)
    o, lse, (g_out, g_fi, g_fo) = _attn_fwd(qa, ka, vb, halves)
    w_out = g_out.reshape(2 * d, d)
    w_fi = g_fi.transpose(1, 0, 2).reshape(d, D_FF)
    w_fo = g_fo.reshape(D_FF, d)
    (y_mix,), _ = _rowwise("attn_norm", lambda o, w: ([o * lax.rsqrt(_rowmean(o * o) + RMS_EPS) * w], []),
                           [o], [sp["attn_norm_w"]], [(d, BF16, (2 * d, 1, y_mix))], [])
    def ln1_fwd(y, x, g1, sc2, sh2, lg, lb):
        r1 = ALPHA * x + (1.0 + g1) * y
        xh, _ = _ln_stats(r1)
        x1 = xh * lg + lb
        h2 = x1 * (1.0 + sc2) + sh2
        return [y, r1, h2, h2.T], []

    def relu2(u):
        a = jnp.square(jnp.maximum(u, 0.0))
        return [a, a.T], []

    y, r1, h2, h2_t = _matmul("out_proj", y_mix, w_out, tm=512, tk=2048,
                              epi=(ln1_fwd, [x], [g1, sc2, sh2, sp["ln1_g"], sp["ln1_b"]], [F32, F32, BF16, ("T", BF16)], []))
    act, act_t = _matmul("ff_in", h2, w_fi, epi=(relu2, [], [], [BF16, ("T", BF16)], []))

    def head(ff, r1, tgt, g2, l1g, l1b, l2g, l2b):
        xh1, _ = _ln_stats(r1)
        x1 = xh1 * l1g + l1b
        xh2, rstd2 = _ln_stats(ALPHA * x1 + (1.0 + g2) * ff)
        err = xh2 * l2g + l2b - tgt
        loss = 0.5 * jnp.sum(_rowmean(err * err))
        dx2 = err * (1.0 / d)
        dr2 = _ln_bwd(dx2 * l2g, xh2, rstd2)
        return ([dr2, (1.0 + g2) * dr2],
                [_colsum(dx2 * xh2), _colsum(dx2), _colsum(dr2 * ff), jnp.full((1, LANES), loss, F32)])

    dr2, dff, d_ln2_g, d_ln2_b, d_g2, loss = _matmul(
        "ff_out", act, w_fo, tm=512, tk=D_FF,
        epi=(head, [r1, tgt], [g2, sp["ln1_g"], sp["ln1_b"], sp["ln2_g"], sp["ln2_b"]], [F32, BF16], [d, d, d, LANES]))
    du = _matmul("d_act", dff, w_fo, dims=NT, epi=(lambda da, act: ([da * (2.0 * jnp.sqrt(act.astype(F32)))], []), [act], [], [BF16], []))
    dw_fo = _matmul("dw_ff_out", act_t, dff, tk=SEQ_TK, out_dtype=BF16, by_chip="rows")
    dw_fi = _matmul("dw_ff_in", h2_t, du, tk=SEQ_TK, out_dtype=BF16, by_chip="cols")

    def ln1_bwd(dh2, r1, dr2, y, sc2, g1, lg, lb):
        xh, rstd = _ln_stats(r1)
        x1 = xh * lg + lb
        dx1 = ALPHA * dr2 + dh2 * (1.0 + sc2)
        dr1 = _ln_bwd(dx1 * lg, xh, rstd)
        return ([dr1, (1.0 + g1) * dr1],
                [_colsum(dh2 * x1), _colsum(dh2), _colsum(dx1 * xh), _colsum(dx1), _colsum(dr1 * y)])

    dr1, dy, d_sc2, d_sh2, d_ln1_g, d_ln1_b, d_g1 = _matmul(
        "dh2", du, w_fi, dims=NT, tm=512, tk=D_FF,
        epi=(ln1_bwd, [r1, dr2, y], [sc2, g1, sp["ln1_g"], sp["ln1_b"]], [F32, BF16], [d] * 5))
    dymix = _matmul("dy_mix", dy, w_out, dims=NT)
    dw_out = _matmul("dw_out", y_mix, dy, dims=TN, tk=SEQ_TK, out_dtype=BF16, by_chip="rows")

    def attn_norm_bwd(o, dyo, w):
        r = lax.rsqrt(_rowmean(o * o) + RMS_EPS)
        return [_rms_bwd(dyo * w, o, r)], [_colsum(dyo * o * r)]

    (do,), (d_attn_w,) = _rowwise("attn_norm_bwd", attn_norm_bwd, [o, (dymix, d, 1)], [sp["attn_norm_w"]], [(d, F32)], [d])

    dxc, ddt_tile, d_alog_l, d_dexp, d_dtb_l, dproj, d_ssm_w = _ssd_bwd(
        xc_all, proj, dt_bias_l, a_log_l, d_exp, prevs, y_ssd, dymix, sp["ssm_norm_w"])
    dqa, dka, dproj, landed = _attn_bwd(qa, ka, vb, o, lse, do, [dw_out, dw_fi, dw_fo], dproj)
    dproj, dcum = _attn_post(dqa, dka, dproj)
    dproj, d_fb_l = _fox_cum_bwd(dcum, proj, f_bias_l, ddt_tile, dproj)
    dpre, d_conv_w, d_conv_b = _conv_bwd_pre(proj, sp["conv_w"], sp["conv_b"], dxc)
    dproj = _conv_bwd_in(dpre, sp["conv_w"], dproj)
    dw_alt = _matmul("dw_in", dproj, h1, dims=TN, tm=1152, tk=SEQ_TK, out_dtype=BF16)
    part_in = _from_aligned(dw_alt).reshape(N_CHIPS, IN_COLS // N_CHIPS, d)

    def last(dh1, x, dr1, sc1):
        return [ALPHA * dr1 + dh1 * (1.0 + sc1)], [_colsum(dh1 * x), _colsum(dh1)]

    chip_in = _pair_sum(part_in, _pair_exchange(part_in), lax.axis_index("c"))
    dx, d_sc1, d_sh1, landed_in = _matmul("dh1", dproj, w_alt, tm=512, tk=AL_COLS, carry=[chip_in],
                                          epi=(last, [x, dr1], [sc1], [F32], [d, d]))

    small = {
        "mod": jnp.concatenate([d_sh1, d_sc1, d_g1, d_sh2, d_sc2, d_g2], axis=1),
        "conv_w": d_conv_w, "conv_b": d_conv_b,
        "dt_bias": d_dtb_l[:, :N_HEADS], "a_log": d_alog_l[:, :N_HEADS],
        "d_skip": jnp.sum(d_dexp.reshape(N_HEADS, HEAD_DIM), axis=1)[None, :],
        "ssm_norm_w": d_ssm_w, "f_bias": d_fb_l[:, F_LANE:F_LANE + N_HEADS], "attn_norm_w": d_attn_w,
        "ln1_g": d_ln1_g, "ln1_b": d_ln1_b, "ln2_g": d_ln2_g, "ln2_b": d_ln2_b, "loss": loss,
    }
    return dx, [landed_in, *landed], small


N_DEV = 8
N_CHIPS = 4
ANY = pl.BlockSpec(memory_space=pl.ANY)
VMEM_SPEC = pl.BlockSpec(memory_space=pltpu.VMEM)


def _place():
    x, y, c = lax.axis_index("x"), lax.axis_index("y"), lax.axis_index("c")
    return x, y, c


def _other_chips(x, y):
    return [(1 - x, y, 2 * (1 - x) + y), (x, 1 - y, 2 * x + 1 - y), (1 - x, 1 - y, 2 * (1 - x) + 1 - y)]


def _small_gather(v_ref, out_ref, send_sems, recv_sems, local_sem):
    x, y, c = _place()
    me = 4 * x + 2 * y + c
    mine = pltpu.make_async_copy(v_ref, out_ref.at[me], local_sem)
    mine.start()
    peers = _peers(x, y, c)

    def copy(rel, slot, to):
        return pltpu.make_async_remote_copy(src_ref=v_ref, dst_ref=out_ref.at[slot], send_sem=send_sems.at[rel],
                                            recv_sem=recv_sems.at[rel], device_id=to, device_id_type=MESH)

    sends = [copy(rel, me, peer) for rel, peer in enumerate(peers)]
    for cp in sends:
        cp.start()
    for rel, (px, py, pc) in enumerate(peers):
        copy(rel, 4 * px + 2 * py + pc, (x, y, c)).wait_recv()
    for cp in sends:
        cp.wait_send()
    mine.wait()


SMALL_GATHER_SEMS = [pltpu.SemaphoreType.DMA((N_DEV - 1,)), pltpu.SemaphoreType.DMA((N_DEV - 1,)), pltpu.SemaphoreType.DMA]


def _allgather_small(name, v, with_sum=False):
    def body(v_ref, out_ref, *rest):
        _small_gather(v_ref, out_ref, *rest[-3:])
        if with_sum:
            acc = out_ref[0]
            for dev in range(1, N_DEV):
                acc = acc + out_ref[dev]
            rest[0][...] = acc

    every = jax.ShapeDtypeStruct((N_DEV, *v.shape), v.dtype)
    return pl.pallas_call(
        body, name=name, out_shape=[every, jax.ShapeDtypeStruct(v.shape, v.dtype)] if with_sum else every,
        in_specs=[VMEM_SPEC], out_specs=[VMEM_SPEC] * 2 if with_sum else VMEM_SPEC, scratch_shapes=SMALL_GATHER_SEMS,
    )(v)


def _gather_shards(shard):
    def body(in_ref, out_ref, stage, send_sems, recv_sems, local_sems):
        start, finish = _shard_gather_plan(in_ref, out_ref, stage, send_sems, recv_sems, local_sems)
        start()
        finish()

    return pl.pallas_call(
        body, name="gather_w_in", out_shape=jax.ShapeDtypeStruct((N_CHIPS, *shard.shape), shard.dtype),
        in_specs=[ANY], out_specs=ANY,
        scratch_shapes=[pltpu.VMEM(shard.shape, shard.dtype), pltpu.SemaphoreType.DMA((6,)), pltpu.SemaphoreType.DMA((6,)),
                        pltpu.SemaphoreType.DMA((2,))],
        compiler_params=_params(),
    )(shard)


def _shard_gather_plan(in_ref, out_ref, stage, send_sems, recv_sems, local_sems):
    ch = in_ref.shape[1] // 2
    x, y, c = _place()
    k_me = 2 * x + y
    me, sibling = (x, y, c), (x, y, 1 - c)
    chips = _other_chips(x, y)

    def copy(idx, k, half, to, src=None):
        cols = out_ref.at[k, :, pl.ds(pl.multiple_of(half * ch, ch), ch)]
        return pltpu.make_async_remote_copy(src_ref=cols if src is None else src, dst_ref=cols, send_sem=send_sems.at[idx],
                                            recv_sem=recv_sems.at[idx], device_id=to, device_id_type=MESH)

    mine = in_ref.at[:, pl.ds(pl.multiple_of(c * ch, ch), ch)]
    sends = [copy(j, k_me, c, (cx, cy, c), src=mine) for j, (cx, cy, _) in enumerate(chips)]
    load = pltpu.make_async_copy(in_ref, stage, local_sems.at[0])
    store = pltpu.make_async_copy(stage, out_ref.at[k_me], local_sems.at[1])

    def start():
        for cp in sends:
            cp.start()
        load.start()

    def finish():
        load.wait()
        store.start()
        forwards = []
        for j, (_, _, kj) in enumerate(chips):
            copy(j, kj, c, me).wait_recv()
            forwards.append(copy(3 + j, kj, c, sibling))
            forwards[-1].start()
        for j, (_, _, kj) in enumerate(chips):
            copy(3 + j, kj, 1 - c, me).wait_recv()
        for cp in sends + forwards:
            cp.wait_send()
        store.wait()

    return start, finish


def _peers(x, y, c):
    return [((1 - x) if rel & 4 else x, (1 - y) if rel & 2 else y, (1 - c) if rel & 1 else c) for rel in range(1, N_DEV)]


def _exchange_sems(n):
    return [pltpu.SemaphoreType.DMA((n, N_DEV - 1)), pltpu.SemaphoreType.DMA((n, N_DEV - 1)), pltpu.SemaphoreType.DMA((n,))]


def _gather_plan(ins, outs, send_sems, recv_sems, local_sems):
    x, y, c = _place()
    k_me = 2 * x + y
    peers = [(rel, p) for rel, p in enumerate(_peers(x, y, c)) if (rel + 1) & 6]

    def copy(w, rel, k, half, to, src=None):
        rh = ins[w].shape[0] // 2
        rows = outs[w].at[k, pl.ds(pl.multiple_of(half * rh, rh), rh), :]
        return pltpu.make_async_remote_copy(src_ref=rows if src is None else src, dst_ref=rows, send_sem=send_sems.at[w, rel],
                                            recv_sem=recv_sems.at[w, rel], device_id=to, device_id_type=MESH)

    def mine(w):
        rh = ins[w].shape[0] // 2
        return ins[w].at[pl.ds(pl.multiple_of(c * rh, rh), rh), :]

    n = len(ins)
    local = [pltpu.make_async_copy(ins[w], outs[w].at[k_me], local_sems.at[w]) for w in range(n)]
    sends = [copy(w, rel, k_me, c, peer, src=mine(w)) for w in range(n) for rel, peer in peers]

    def start():
        for cp in local + sends:
            cp.start()

    def finish():
        for w in range(n):
            for rel, (px, py, pc) in peers:
                copy(w, rel, 2 * px + py, pc, (x, y, c)).wait_recv()
        for cp in sends:
            cp.wait_send()
        for cp in local:
            cp.wait()

    return start, finish


def _reduce_plan(ins, outs, send_sems, recv_sems, local_sems):
    x, y, c = _place()
    me = 4 * x + 2 * y + c
    peers = _peers(x, y, c)

    def block(w, k, half):
        rh = ins[w].shape[1] // 2
        return ins[w].at[k, pl.ds(pl.multiple_of(half * rh, rh), rh), :]

    def copy(w, rel, src, slot, to):
        return pltpu.make_async_remote_copy(src_ref=src, dst_ref=outs[w].at[slot], send_sem=send_sems.at[w, rel],
                                            recv_sem=recv_sems.at[w, rel], device_id=to, device_id_type=MESH)

    n = len(ins)
    local = [pltpu.make_async_copy(block(w, 2 * x + y, c), outs[w].at[me], local_sems.at[w]) for w in range(n)]
    sends = [copy(w, rel, block(w, 2 * px + py, pc), me, (px, py, pc)) for w in range(n) for rel, (px, py, pc) in enumerate(peers)]

    def start():
        for cp in local + sends:
            cp.start()

    def finish():
        for w in range(n):
            for rel, (px, py, pc) in enumerate(peers):
                copy(w, rel, block(w, 2 * x + y, c), 4 * px + 2 * py + pc, (x, y, c)).wait_recv()
        for cp in sends:
            cp.wait_send()
        for cp in local:
            cp.wait()

    return start, finish


def _scatter_plan(ins, outs, send_sems, recv_sems, local_sems):
    x, y, c = _place()
    k_me = 2 * x + y
    chips = _other_chips(x, y)

    def copy(w, j, src_k, dst_k, to):
        return pltpu.make_async_remote_copy(src_ref=ins[w].at[src_k], dst_ref=outs[w].at[dst_k], send_sem=send_sems.at[w, j],
                                            recv_sem=recv_sems.at[w, j], device_id=to, device_id_type=MESH)

    n = len(ins)
    local = [pltpu.make_async_copy(ins[w].at[k_me], outs[w].at[k_me], local_sems.at[w]) for w in range(n)]
    sends = [copy(w, j, kj, k_me, (cx, cy, c)) for w in range(n) for j, (cx, cy, kj) in enumerate(chips)]

    def start():
        for cp in local + sends:
            cp.start()

    def finish():
        for w in range(n):
            for j, (_, _, kj) in enumerate(chips):
                copy(w, j, k_me, kj, (x, y, c)).wait_recv()
        for cp in sends:
            cp.wait_send()
        for cp in local:
            cp.wait()

    return start, finish


def _row_tile(r, mult=2 * SUBLANES):
    if r % 256 == 0:
        return 256
    return max([t for t in range(mult, 513, mult) if r % t == 0], default=r)


def _pair_exchange(g):
    _, r, cdim = g.shape
    ch = cdim // 2

    def body(g_ref, got_ref, send_sem, recv_sem):
        x, y, c = _place()
        cp = pltpu.make_async_remote_copy(src_ref=g_ref.at[:, :, pl.ds(pl.multiple_of((1 - c) * ch, ch), ch)], dst_ref=got_ref,
                                          send_sem=send_sem, recv_sem=recv_sem, device_id=(x, y, 1 - c), device_id_type=MESH)
        cp.start()
        cp.wait_recv()
        cp.wait_send()

    return pl.pallas_call(
        body, name="pair_exchange", out_shape=jax.ShapeDtypeStruct((N_CHIPS, r, ch), g.dtype),
        in_specs=[ANY], out_specs=ANY, scratch_shapes=[pltpu.SemaphoreType.DMA, pltpu.SemaphoreType.DMA],
    )(g)


def _pair_sum(g, got, c):
    _, r, cdim = g.shape
    ch = cdim // 2
    tr = _row_tile(r)

    def body(c_ref, g_ref, got_ref, o_ref):
        o_ref[...] = (g_ref[...].astype(F32) + got_ref[...].astype(F32)).astype(o_ref.dtype)

    blk = pl.BlockSpec((1, tr, ch), lambda k, i, c_ref: (k, i, 0))
    return pl.pallas_call(
        body, name="pair_sum",
        grid_spec=pltpu.PrefetchScalarGridSpec(
            num_scalar_prefetch=1, grid=(N_CHIPS, r // tr),
            in_specs=[pl.BlockSpec((1, tr, ch), lambda k, i, c_ref: (k, i, c_ref[0])), blk], out_specs=blk),
        out_shape=jax.ShapeDtypeStruct((N_CHIPS, r, ch), BF16),
        compiler_params=_params(("parallel", "parallel")),
    )(jnp.reshape(c, (1,)).astype(jnp.int32), g, got)


def _sum_blocks(name, parts):
    k, r, cdim = parts.shape
    tr = _row_tile(r)

    def body(p_ref, o_ref):
        acc = p_ref[0].astype(F32)
        for i in range(1, k):
            acc = acc + p_ref[i].astype(F32)
        o_ref[...] = acc

    return pl.pallas_call(
        body, name=name, grid=(r // tr,),
        in_specs=[pl.BlockSpec((k, tr, cdim), lambda i: (0, i, 0))], out_specs=pl.BlockSpec((tr, cdim), lambda i: (i, 0)),
        out_shape=jax.ShapeDtypeStruct((r, cdim), F32), compiler_params=_params(("parallel",)),
    )(parts)


def _pair_swap(halves):
    n = len(halves)

    def body(*refs):
        ins, outs = refs[:n], refs[n:2 * n]
        send_sems, recv_sems = refs[2 * n:]
        x, y, c = _place()
        cps = [pltpu.make_async_remote_copy(src_ref=ins[w], dst_ref=outs[w], send_sem=send_sems.at[w], recv_sem=recv_sems.at[w],
                                            device_id=(x, y, 1 - c), device_id_type=MESH) for w in range(n)]
        for cp in cps:
            cp.start()
        for cp in cps:
            cp.wait_recv()
        for cp in cps:
            cp.wait_send()

    return pl.pallas_call(
        body, name="pair_swap", out_shape=[jax.ShapeDtypeStruct(h.shape, h.dtype) for h in halves],
        in_specs=[ANY] * n, out_specs=[ANY] * n,
        scratch_shapes=[pltpu.SemaphoreType.DMA((n,)), pltpu.SemaphoreType.DMA((n,))],
    )(*halves)


ADA_SHARD = 6 * D_MODEL // N_CHIPS


def _mod_part(c_all, w_shard, b_shard):
    tn = 512

    def body(c_ref, w_ref, b_ref, o_ref):
        o_ref[...] = _dot(_silu(c_ref[...]).astype(BF16), w_ref[...].astype(BF16)) + b_ref[...]

    return pl.pallas_call(
        body, name="mod_part", grid=(ADA_SHARD // tn,),
        in_specs=[pl.BlockSpec((N_DEV, D_MODEL), lambda j: (0, 0)), pl.BlockSpec((D_MODEL, tn), lambda j: (0, j)),
                  pl.BlockSpec((1, tn), lambda j: (0, j))],
        out_specs=pl.BlockSpec((N_DEV, tn), lambda j: (0, j)),
        out_shape=jax.ShapeDtypeStruct((N_DEV, ADA_SHARD), F32), compiler_params=_params(("parallel",)),
    )(c_all, w_shard, b_shard)


def _w_ada_grad(c_all_t, dmod_shard):
    tm = 256

    def body(ct_ref, dm_ref, o_ref):
        act = _silu(ct_ref[...])
        acc = act[:, 0:1] * dm_ref[0:1, :]
        for dev in range(1, N_DEV):
            acc = acc + act[:, dev:dev + 1] * dm_ref[dev:dev + 1, :]
        o_ref[...] = acc

    return pl.pallas_call(
        body, name="w_ada_grad", grid=(D_MODEL // tm,),
        in_specs=[pl.BlockSpec((tm, N_DEV), lambda i: (i, 0)), pl.BlockSpec((N_DEV, ADA_SHARD), lambda i: (0, 0))],
        out_specs=pl.BlockSpec((tm, ADA_SHARD), lambda i: (i, 0)),
        out_shape=jax.ShapeDtypeStruct((D_MODEL, ADA_SHARD), F32), compiler_params=_params(("parallel",)),
    )(c_all_t, dmod_shard)


def _adamw_math(w, g, m, v):
    nm = ADAM_B1 * m + (1.0 - ADAM_B1) * g
    nv = ADAM_B2 * v + (1.0 - ADAM_B2) * jnp.square(g)
    m_hat = nm / (1.0 - ADAM_B1 ** ADAM_STEP)
    v_hat = nv / (1.0 - ADAM_B2 ** ADAM_STEP)
    return -ADAM_LR * (m_hat / (jnp.sqrt(v_hat) + ADAM_EPS) + ADAM_WD * w), nm, nv


def _adamw(name, w, g, m, v):
    _, r, cdim = w.shape
    tr = 256 if r % 256 == 0 else r

    def body(w_ref, g_ref, m_ref, v_ref, go_ref, d_ref, nm_ref, nv_ref):
        go_ref[...] = g_ref[...]
        d_ref[...], nm_ref[...], nv_ref[...] = _adamw_math(w_ref[...], g_ref[...], m_ref[...], v_ref[...])

    blk = pl.BlockSpec((None, tr, cdim), lambda i: (0, i, 0))
    return pl.pallas_call(
        body, name=name, grid=(r // tr,), in_specs=[blk, pl.BlockSpec((tr, cdim), lambda i: (i, 0)), blk, blk], out_specs=[blk] * 4,
        out_shape=[jax.ShapeDtypeStruct((1, r, cdim), F32)] * 4, compiler_params=_params(("parallel",)),
    )(w, g, m, v)


def _adamw_pair(name, w, mine, other, m, v, c, by_cols=False):
    _, r, cdim = w.shape
    hr, hc = mine.shape
    tr = _row_tile(hr, SUBLANES)
    per = hr // tr

    def body(c_ref, w_ref, a_ref, b_ref, m_ref, v_ref, g_ref, d_ref, nm_ref, nv_ref):
        half = pl.program_id(1) if by_cols else pl.program_id(0) // per
        g = jnp.where(half == c_ref[0], a_ref[...], b_ref[...])
        g_ref[...] = g
        d_ref[...], nm_ref[...], nv_ref[...] = _adamw_math(w_ref[...], g, m_ref[...], v_ref[...])

    blk = pl.BlockSpec((None, tr, hc), lambda i, j, c_ref: (0, i, j))
    half = pl.BlockSpec((tr, hc), lambda i, j, c_ref: (i % per, 0))
    return pl.pallas_call(
        body, name=name,
        grid_spec=pltpu.PrefetchScalarGridSpec(num_scalar_prefetch=1, grid=(r // tr, cdim // hc),
                                               in_specs=[blk, half, half, blk, blk], out_specs=[blk] * 4),
        out_shape=[jax.ShapeDtypeStruct((1, r, cdim), F32)] * 4, compiler_params=_params(("parallel", "parallel")),
    )(jnp.reshape(c, (1,)).astype(jnp.int32), w, mine, other, m, v)


SMALL = ["b_ada", "conv_b", "dt_bias", "a_log", "d_skip", "ssm_norm_w", "f_bias", "attn_norm_w", "ln1_g", "ln1_b", "ln2_g", "ln2_b"]


def _pack(vs):
    pieces = []
    for v in vs:
        pieces.append(v)
        if v.shape[1] % LANES:
            pieces.append(jnp.zeros((1, -v.shape[1] % LANES), v.dtype))
    return jnp.concatenate(pieces, axis=1)


def _adamw_small(total, offs, ws, ms, vs):
    n = len(ws)

    def body(*refs):
        t_ref, outs = refs[0], refs[1 + 3 * n:]
        for i in range(n):
            g = t_ref[:, offs[i]:offs[i] + ws[i].shape[1]]
            dl, nm, nv = _adamw_math(refs[1 + i][...], g, refs[1 + n + i][...], refs[1 + 2 * n + i][...])
            outs[4 * i][...], outs[4 * i + 1][...], outs[4 * i + 2][...], outs[4 * i + 3][...] = g, dl, nm, nv

    res = pl.pallas_call(
        body, name="adamw_small", in_specs=[VMEM_SPEC] * (1 + 3 * n), out_specs=[VMEM_SPEC] * (4 * n),
        out_shape=[jax.ShapeDtypeStruct(w.shape, F32) for w in ws for _ in range(4)],
    )(total, *ws, *ms, *vs)
    return [res[4 * i:4 * i + 4] for i in range(n)]


def kernel(x, c, w_ada, b_ada, w_in, conv_w, conv_b, dt_bias, a_log, d_skip, ssm_norm_w, f_bias, attn_norm_w, w_out, ln1_g, ln1_b, w_ff_in, w_ff_out, ln2_g, ln2_b, loss_target, m_w_ada, m_b_ada, m_w_in, m_conv_w, m_conv_b, m_dt_bias, m_a_log, m_d_skip, m_ssm_norm_w, m_f_bias, m_attn_norm_w, m_w_out, m_ln1_g, m_ln1_b, m_w_ff_in, m_w_ff_out, m_ln2_g, m_ln2_b, v_w_ada, v_b_ada, v_w_in, v_conv_w, v_conv_b, v_dt_bias, v_a_log, v_d_skip, v_ssm_norm_w, v_f_bias, v_attn_norm_w, v_w_out, v_ln1_g, v_ln1_b, v_w_ff_in, v_w_ff_out, v_ln2_g, v_ln2_b):
    a = dict(b_ada=b_ada, conv_b=conv_b, dt_bias=dt_bias, a_log=a_log, d_skip=d_skip, ssm_norm_w=ssm_norm_w, f_bias=f_bias,
             attn_norm_w=attn_norm_w, ln1_g=ln1_g, ln1_b=ln1_b, ln2_g=ln2_g, ln2_b=ln2_b)
    ms = dict(b_ada=m_b_ada, conv_b=m_conv_b, dt_bias=m_dt_bias, a_log=m_a_log, d_skip=m_d_skip, ssm_norm_w=m_ssm_norm_w,
              f_bias=m_f_bias, attn_norm_w=m_attn_norm_w, ln1_g=m_ln1_g, ln1_b=m_ln1_b, ln2_g=m_ln2_g, ln2_b=m_ln2_b)
    vs = dict(b_ada=v_b_ada, conv_b=v_conv_b, dt_bias=v_dt_bias, a_log=v_a_log, d_skip=v_d_skip, ssm_norm_w=v_ssm_norm_w,
              f_bias=v_f_bias, attn_norm_w=v_attn_norm_w, ln1_g=v_ln1_g, ln1_b=v_ln1_b, ln2_g=v_ln2_g, ln2_b=v_ln2_b)
    xi, yi, ci = _place()
    chip = 2 * xi + yi
    me = 4 * xi + 2 * yi + ci
    d = D_MODEL
    conv_shard = CONV_DIM // N_CHIPS

    first = _allgather_small("gather_c", jnp.concatenate([c, conv_w[0].reshape(1, CONV_W * conv_shard)], axis=1))[:, 0]
    c_all = first[:, :d]
    conv_w_full = first[::2, d:].reshape(N_CHIPS, CONV_W, conv_shard).transpose(1, 0, 2).reshape(CONV_W, CONV_DIM)
    b_shard = lax.dynamic_slice_in_dim(b_ada, chip * ADA_SHARD, ADA_SHARD, axis=1)
    parts = _allgather_small("gather_mod", _mod_part(c_all, w_ada[0], b_shard))
    mod = lax.dynamic_index_in_dim(parts[::2], me, axis=1, keepdims=False).reshape(1, 6 * d)

    w_in_t, m_w_in_t, v_w_in_t = [jnp.transpose(t, (0, 2, 1)) for t in (w_in, m_w_in, v_w_in)]
    w_alt = _to_aligned(_gather_shards(w_in_t[0].astype(BF16)).reshape(IN_COLS, d))

    sp = {n: a[n] for n in SMALL[1:]}
    sp["conv_w"] = conv_w_full
    shards = [w_out[0].astype(BF16), w_ff_in[0].astype(BF16), w_ff_out[0].astype(BF16)]
    dx, landed, small = _local_step(x[0], loss_target[0], mod, w_alt, shards, sp)

    names = ["mod"] + SMALL[1:]
    vec = _pack([small[n] for n in names] + [small["conv_w"].reshape(1, CONV_W * CONV_DIM), small["loss"]])
    every, total = _allgather_small("gather_small", vec, with_sum=True)
    widths = [6 * d] + [a[n].shape[1] for n in SMALL[1:]]
    offs = [0]
    for w in widths:
        offs.append(offs[-1] + w + (-w % LANES))
    g_conv_w_full = total[:, offs[-1]:offs[-1] + CONV_W * CONV_DIM].reshape(CONV_W, CONV_DIM)
    loss = total[0, offs[-1] + CONV_W * CONV_DIM]
    dmod_shard = lax.dynamic_slice_in_dim(every[:, 0, :6 * d], chip * ADA_SHARD, ADA_SHARD, axis=1)
    g_w_ada = _w_ada_grad(c_all.T, dmod_shard)
    g_conv_w = lax.dynamic_slice_in_dim(g_conv_w_full, chip * conv_shard, conv_shard, axis=1)

    mine = [_sum_blocks("dev_sum_%d" % i, p) for i, p in enumerate(landed)]
    other = _pair_swap(mine)

    grads, deltas, new_m, new_v = {}, {}, {}, {}
    paired = dict(w_in=(w_in_t, m_w_in_t, v_w_in_t), w_out=(w_out, m_w_out, v_w_out), w_ff_in=(w_ff_in, m_w_ff_in, v_w_ff_in),
                  w_ff_out=(w_ff_out, m_w_ff_out, v_w_ff_out))
    for i, (n, (w, m, v)) in enumerate(paired.items()):
        res = _adamw_pair("adamw_" + n, w, mine[i], other[i], m, v, ci, by_cols=n == "w_in")
        grads[n], deltas[n], new_m[n], new_v[n] = [jnp.transpose(t, (0, 2, 1)) for t in res] if n == "w_in" else res
    for n, g, (w, m, v) in (("w_ada", g_w_ada, (w_ada, m_w_ada, v_w_ada)), ("conv_w", g_conv_w, (conv_w, m_conv_w, v_conv_w))):
        grads[n], deltas[n], new_m[n], new_v[n] = _adamw("adamw_" + n, w, g, m, v)
    for n, res in zip(SMALL, _adamw_small(total, offs, [a[n] for n in SMALL], [ms[n] for n in SMALL], [vs[n] for n in SMALL])):
        grads[n], deltas[n], new_m[n], new_v[n] = res

    order = ["w_ada", "b_ada", "w_in", "conv_w", "conv_b", "dt_bias", "a_log", "d_skip", "ssm_norm_w", "f_bias", "attn_norm_w", "w_out",
             "ln1_g", "ln1_b", "w_ff_in", "w_ff_out", "ln2_g", "ln2_b"]
    return (loss, dx[None], *[grads[n] for n in order], *[deltas[n] for n in order], *[new_m[n] for n in order], *[new_v[n] for n in order])
```

```python
import functools

import jax
import jax.numpy as jnp
from jax import lax
from jax.experimental import pallas as pl
from jax.experimental.pallas import tpu as pltpu

F32, BF16 = jnp.float32, jnp.bfloat16

D_MODEL = 1024
N_HEADS = 16
HEAD_DIM = 64
N_PAIRS = N_HEADS // 2
SSM_GROUPS = 2
SSM_STATE = 128
CHUNK = 128
CONV_W = 4
CONV_DIM = 1536
D_FF = 4096
IN_COLS = 5664
ALPHA = 2.0 ** 0.25
LN_EPS = 1e-5
RMS_EPS = 1e-5
LANES = 128
SUBLANES = 8

AL_Z, AL_XS, AL_Q, AL_K, AL_V, AL_B, AL_C, AL_DTF = 0, 1024, 2048, 3072, 4096, 5120, 5376, 5632
AL_COLS = 5760
F_LANE = 16

ADAM_LR, ADAM_B1, ADAM_B2, ADAM_EPS, ADAM_WD, ADAM_STEP = 0.001, 0.9, 0.999, 1e-08, 0.01, 10

VMEM_LIMIT = 56 * 1024 * 1024
SEQ_TK = 4096
MESH = pl.DeviceIdType.MESH


def _params(sem=None):
    return pltpu.CompilerParams(dimension_semantics=sem, vmem_limit_bytes=VMEM_LIMIT)


def _sigmoid(x):
    return 1.0 / (1.0 + jnp.exp(-x))


def _silu(x):
    return x * _sigmoid(x)


def _softplus(x):
    return jnp.maximum(x, 0.0) + jnp.log(1.0 + jnp.exp(-jnp.abs(x)))


def _split3(a):
    hi = a.astype(BF16)
    r = a - hi.astype(F32)
    mid = r.astype(BF16)
    lo = (r - mid.astype(F32)).astype(BF16)
    return hi, mid, lo


def _dot(a, b, dims=((1,), (0,))):
    return lax.dot_general(a, b, (dims, ((), ())), preferred_element_type=F32)


NN, NT, TN = ((1,), (0,)), ((1,), (1,)), ((0,), (0,))


def _dot3(t, a):
    hi, mid, lo = _split3(a)
    return _dot(t, hi) + _dot(t, mid) + _dot(t, lo)


def _matmul(name, a, b, *, dims=NN, out_dtype=F32, tm=1024, tn=1024, tk=1024, by_chip=None, epi=None, carry=()):
    if dims == NN:
        (m, k), n = a.shape, b.shape[1]
    elif dims == NT:
        (m, k), n = a.shape, b.shape[0]
    else:
        (k, m), n = a.shape, b.shape[1]
    if by_chip == "rows":
        tm = min(tm, m // 4)
    if by_chip == "cols":
        tn = min(tn, n // 4)
    tm, tn, tk = min(tm, m), min(tn, n), min(tk, k)
    assert m % tm == 0 and n % tn == 0 and k % tk == 0, (name, m, n, k, tm, tn, tk)
    nk = k // tk
    if by_chip == "rows":
        per = m // 4 // tm
        out_spec = pl.BlockSpec((None, tm, tn), lambda i, j, l: (i // per, i % per, j))
        out_shape = jax.ShapeDtypeStruct((4, m // 4, n), out_dtype)
    elif by_chip == "cols":
        per = n // 4 // tn
        out_spec = pl.BlockSpec((None, tm, tn), lambda i, j, l: (j // per, i, j % per))
        out_shape = jax.ShapeDtypeStruct((4, m, n // 4), out_dtype)
    else:
        out_spec = pl.BlockSpec((tm, tn), lambda i, j, l: (i, j))
        out_shape = jax.ShapeDtypeStruct((m, n), out_dtype)
    a_spec = pl.BlockSpec((tk, tm), lambda i, j, l: (l, i)) if dims == TN else pl.BlockSpec((tm, tk), lambda i, j, l: (i, l))
    b_spec = pl.BlockSpec((tn, tk), lambda i, j, l: (j, l)) if dims == NT else pl.BlockSpec((tk, tn), lambda i, j, l: (l, j))

    tile = pl.BlockSpec((tm, tn), lambda i, j, l: (i, j))
    in_specs, args, out_specs, out_shape = [a_spec, b_spec], [a, b], [out_spec], [out_shape]
    fn, n_tiles, n_sums = None, 1, 0
    if epi is not None:
        fn, fulls, vecs, outs, sums = epi
        assert by_chip is None and (not sums or n == tn), name
        in_specs = in_specs + [tile] * len(fulls) + [pl.BlockSpec((1, tn), lambda i, j, l: (0, j))] * len(vecs)
        args = args + list(fulls) + list(vecs)
        flipped = pl.BlockSpec((tn, tm), lambda i, j, l: (j, i))
        out_specs = [flipped if isinstance(dt, tuple) else tile for dt in outs] + [pl.BlockSpec((1, w), lambda i, j, l: (0, 0)) for w in sums]
        out_shape = [jax.ShapeDtypeStruct((n, m), dt[1]) if isinstance(dt, tuple) else jax.ShapeDtypeStruct((m, n), dt) for dt in outs]
        out_shape += [jax.ShapeDtypeStruct((1, w), F32) for w in sums]
        n_tiles, n_sums = len(outs), len(sums)
    n_in, n_out, n_c = len(args), len(out_specs), len(carry)
    scratch = [pltpu.VMEM((tm, tn) if nk > 1 else (SUBLANES, LANES), F32)]
    if n_c:
        in_specs, args = in_specs + [ANY] * n_c, args + list(carry)
        out_specs = out_specs + [ANY] * n_c
        out_shape = out_shape + [jax.ShapeDtypeStruct(g.shape, g.dtype) for g in carry]
        scratch = scratch + _exchange_sems(n_c)
    gm, gn = m // tm, n // tn

    def body(*refs):
        a_ref, b_ref = refs[:2]
        ins, outs = refs[2:n_in], refs[n_in + n_c:n_in + n_c + n_out]
        acc_ref = refs[n_in + 2 * n_c + n_out]
        i, j, l = pl.program_id(0), pl.program_id(1), pl.program_id(2)
        if n_c:
            start, wait = _scatter_plan(refs[n_in:n_in + n_c], refs[n_in + n_c + n_out:n_in + 2 * n_c + n_out], *refs[n_in + 2 * n_c + n_out + 1:])
            pl.when((i == 0) & (j == 0) & (l == 0))(start)
        part = _dot(a_ref[...].astype(BF16), b_ref[...].astype(BF16), dims)

        def finish(res):
            if fn is None:
                outs[0][...] = res.astype(outs[0].dtype)
                return
            tiles, colsums = fn(res, *[r[...] for r in ins])
            for r, val in zip(outs[:n_tiles], tiles):
                r[...] = val.astype(r.dtype)
            if n_sums:
                @pl.when(i == 0)
                def _():
                    for r in outs[n_tiles:]:
                        r[...] = jnp.zeros_like(r)
                for r, val in zip(outs[n_tiles:], colsums):
                    r[...] += val

        if nk == 1:
            finish(part)
        else:
            @pl.when(l == 0)
            def _():
                acc_ref[...] = part

            @pl.when((l > 0) & (l < nk - 1))
            def _():
                acc_ref[...] += part

            @pl.when(l == nk - 1)
            def _():
                finish(acc_ref[...] + part)

        if n_c:
            pl.when((i == gm - 1) & (j == gn - 1) & (l == nk - 1))(wait)

    res = pl.pallas_call(
        body, name=name, grid=(gm, gn, nk),
        in_specs=in_specs, out_specs=out_specs, out_shape=out_shape, scratch_shapes=scratch,
        compiler_params=_params(("arbitrary",) * 3 if n_c or n_sums else ("parallel", "parallel", "arbitrary")),
    )(*args)
    return res[0] if len(res) == 1 else res


def _rowwise(name, fn, fulls, vecs, out_fulls, out_vecs, tr=256):
    fulls = [f if isinstance(f, tuple) else (f, f.shape[1], 0) for f in fulls]
    s = fulls[0][0].shape[0]
    tr = min(tr, s)
    out_fulls = [o if len(o) == 3 else (*o, (o[0], 0, None)) for o in out_fulls]
    into = [(k, slab[2]) for k, (_, _, slab) in enumerate(out_fulls) if slab[2] is not None]
    nf, nv, nof, nov = len(fulls), len(vecs), len(out_fulls), len(out_vecs)
    in_specs = [pl.BlockSpec((tr, w), functools.partial(lambda i, cb: (i, cb), cb=cb)) for (_, w, cb) in fulls]
    in_specs += [pl.BlockSpec(v.shape, lambda i: (0, 0)) for v in vecs] + [ANY] * len(into)
    out_shape = [jax.ShapeDtypeStruct((s, slab[0]), dt) for (_, dt, slab) in out_fulls] + [jax.ShapeDtypeStruct((1, w), F32) for w in out_vecs]
    out_specs = [pl.BlockSpec((tr, w), functools.partial(lambda i, cb: (i, cb), cb=slab[1])) for (w, _, slab) in out_fulls]
    out_specs += [pl.BlockSpec((1, w), lambda i: (0, 0)) for w in out_vecs]

    def body(*refs):
        outs = refs[nf + nv + len(into):]
        of, ov = fn(*[r[...] for r in refs[:nf + nv]])
        for r, val in zip(outs[:nof], of):
            r[...] = val.astype(r.dtype)
        if nov:
            @pl.when(pl.program_id(0) == 0)
            def _():
                for r in outs[nof:]:
                    r[...] = jnp.zeros_like(r)
            for r, val in zip(outs[nof:], ov):
                r[...] += val

    res = pl.pallas_call(
        body, name=name, grid=(s // tr,), in_specs=in_specs, out_specs=out_specs, out_shape=out_shape,
        input_output_aliases={nf + nv + pos: k for pos, (k, _) in enumerate(into)},
        compiler_params=_params(("arbitrary",)),
    )(*[f[0] for f in fulls], *vecs, *[buf for _, buf in into])
    return res[:nof], res[nof:]


def _colsum(x):
    return jnp.sum(x, axis=0, keepdims=True)


def _rowmean(x):
    return jnp.mean(x, axis=-1, keepdims=True)


CONV_CB = 512
CONV_TR = 512


def _shift_down(u, halo, j):
    if j == 0:
        return u
    ru = pltpu.roll(u, j, 0)
    row8 = lax.broadcasted_iota(jnp.int32, halo.shape, 0)
    top = jnp.where(row8 < j, pltpu.roll(halo, j, 0), ru[:SUBLANES])
    return jnp.concatenate([top, ru[SUBLANES:]], axis=0)


def _shift_up(d, halo, j):
    if j == 0:
        return d
    tr = d.shape[0]
    rd = pltpu.roll(d, tr - j, 0)
    row8 = lax.broadcasted_iota(jnp.int32, halo.shape, 0)
    bot = jnp.where(row8 >= SUBLANES - j, pltpu.roll(halo, SUBLANES - j, 0), rd[tr - SUBLANES:])
    return jnp.concatenate([rd[:tr - SUBLANES], bot], axis=0)


def _conv_col(cb):
    return jnp.where(cb < 2, AL_XS // CONV_CB + cb, AL_B // CONV_CB)


def _conv_specs(s, tr):
    per8 = tr // SUBLANES
    blk = pl.BlockSpec((tr, CONV_CB), lambda cb, i: (i, _conv_col(cb)))
    prev = pl.BlockSpec((SUBLANES, CONV_CB), lambda cb, i: (jnp.maximum(i * per8 - 1, 0), _conv_col(cb)))
    return blk, prev


def _conv_pre(u, halo, w_ref, b_ref, first):
    halo = jnp.where(first, 0.0, halo)
    acc = b_ref[...] + w_ref[CONV_W - 1:CONV_W, :] * u
    shifted = [u]
    for j in range(1, CONV_W):
        sh = _shift_down(u, halo, j)
        shifted.append(sh)
        acc = acc + w_ref[CONV_W - 1 - j:CONV_W - j, :] * sh
    return acc, shifted


def _conv_bwd_pre(proj, conv_w, conv_b, dxc):
    s = proj.shape[0]
    tr = min(CONV_TR, s)
    blk, prev = _conv_specs(s, tr)

    def body(u_ref, h_ref, w_ref, b_ref, d_ref, dpre_ref, dw_ref, db_ref):
        i = pl.program_id(1)
        pre, shifted = _conv_pre(u_ref[...], h_ref[...], w_ref, b_ref, i == 0)
        sg = _sigmoid(pre)
        dpre = d_ref[...] * (sg * (1.0 + pre * (1.0 - sg)))
        dpre_ref[...] = dpre

        @pl.when(i == 0)
        def _():
            dw_ref[...] = jnp.zeros_like(dw_ref)
            db_ref[...] = jnp.zeros_like(db_ref)

        db_ref[...] += _colsum(dpre)
        for j in range(CONV_W):
            dw_ref[CONV_W - 1 - j:CONV_W - j, :] += _colsum(dpre * shifted[j])

    own = pl.BlockSpec((tr, CONV_CB), lambda cb, i: (i, cb))
    wspec = pl.BlockSpec((CONV_W, CONV_CB), lambda cb, i: (0, cb))
    bspec = pl.BlockSpec((1, CONV_CB), lambda cb, i: (0, cb))
    return pl.pallas_call(
        body, name="conv_bwd_pre", grid=(CONV_DIM // CONV_CB, s // tr),
        in_specs=[blk, prev, wspec, bspec, own], out_specs=[own, wspec, bspec],
        out_shape=[jax.ShapeDtypeStruct((s, CONV_DIM), F32), jax.ShapeDtypeStruct((CONV_W, CONV_DIM), F32),
                   jax.ShapeDtypeStruct((1, CONV_DIM), F32)],
        compiler_params=_params(("parallel", "arbitrary")),
    )(proj, proj, conv_w, conv_b, dxc)


def _conv_bwd_in(dpre, conv_w, dproj):
    s = dpre.shape[0]
    tr = min(CONV_TR, s)
    per8 = tr // SUBLANES
    last8 = s // SUBLANES - 1
    nb = s // tr

    def body(d_ref, n_ref, w_ref, _, o_ref):
        d = d_ref[...]
        halo = jnp.where(pl.program_id(1) == nb - 1, 0.0, n_ref[...])
        acc = w_ref[CONV_W - 1:CONV_W, :] * d
        for j in range(1, CONV_W):
            acc = acc + w_ref[CONV_W - 1 - j:CONV_W - j, :] * _shift_up(d, halo, j)
        o_ref[...] = acc.astype(o_ref.dtype)

    own = pl.BlockSpec((tr, CONV_CB), lambda cb, i: (i, cb))
    nxt = pl.BlockSpec((SUBLANES, CONV_CB), lambda cb, i: (jnp.minimum((i + 1) * per8, last8), cb))
    return pl.pallas_call(
        body, name="conv_bwd_in", grid=(CONV_DIM // CONV_CB, nb),
        in_specs=[own, nxt, pl.BlockSpec((CONV_W, CONV_CB), lambda cb, i: (0, cb)), ANY],
        out_specs=pl.BlockSpec((tr, CONV_CB), lambda cb, i: (i, _conv_col(cb))),
        out_shape=jax.ShapeDtypeStruct(dproj.shape, dproj.dtype), input_output_aliases={3: 0},
        compiler_params=_params(("parallel", "parallel")),
    )(dpre, dpre, conv_w, dproj)


XC_B, XC_C = 1024, 1280


def _tile_iotas():
    row = lax.broadcasted_iota(jnp.int32, (CHUNK, LANES), 0)
    lane = lax.broadcasted_iota(jnp.int32, (CHUNK, LANES), 1)
    return row, lane


def _ssd_scalars(dtf_ref, bias_ref, alog_ref, row, lane):
    head = lane[:1] < N_HEADS
    raw = dtf_ref[...] + bias_ref[...]
    dt = _softplus(raw)
    a_neg = jnp.where(head, -jnp.exp(alog_ref[...]), 0.0)
    a = dt * a_neg
    tril = (row >= lane).astype(BF16)
    s = _dot3(tril, a)
    return raw, dt, a_neg, s


def _pair(v, j, lo):
    return jnp.where(lo, v[:, 2 * j:2 * j + 1], v[:, 2 * j + 1:2 * j + 2])


def _head_sum(x, lo, hh):
    return jnp.sum(jnp.where(lo == (hh == 0), x, 0.0), axis=1, keepdims=True)


def _decay_masks(s, st, h, row, lane):
    s_col = jnp.broadcast_to(s[:, h:h + 1], (CHUNK, LANES))
    s_row = jnp.broadcast_to(st[h:h + 1, :], (CHUNK, LANES))
    lm = jnp.where(row >= lane, jnp.exp(s_col - s_row), 0.0)
    lmt = jnp.where(row <= lane, jnp.exp(s_row - s_col), 0.0)
    return lm, lmt


def _gated_norm(y, z, w):
    g = y * _silu(z)
    return g * lax.rsqrt(_rowmean(g * g) + RMS_EPS) * w


def _ssd_fwd(proj, conv_w, conv_b, dt_bias_l, a_log_l, d_exp, norm_w):
    s_len = proj.shape[0]
    nc = s_len // CHUNK

    def body(xs_ref, bc_ref, cw_ref, cb_ref, dtf_ref, bias_ref, alog_ref, dexp_ref, z_ref, w_ref,
             x_ref, y_ref, prevs_ref, ymix_ref, state_ref, halo_ref):
        first = pl.program_id(0) == 0

        @pl.when(first)
        def _():
            state_ref[...] = jnp.zeros_like(state_ref)
            halo_ref[...] = jnp.zeros_like(halo_ref)

        u = jnp.concatenate([xs_ref[...], bc_ref[...]], axis=1)
        pre, _ = _conv_pre(u, halo_ref[...], cw_ref, cb_ref, first)
        halo_ref[...] = u[CHUNK - SUBLANES:]
        x_ref[...] = _silu(pre)

        row, lane = _tile_iotas()
        lo = lane < HEAD_DIM
        _, dt, _, s = _ssd_scalars(dtf_ref, bias_ref, alog_ref, row, lane)
        tot = s[CHUNK - 1:CHUNK, :]
        st = s.T
        for g in range(SSM_GROUPS):
            bg = x_ref[:, XC_B + g * SSM_STATE:XC_B + (g + 1) * SSM_STATE].astype(BF16)
            cg = x_ref[:, XC_C + g * SSM_STATE:XC_C + (g + 1) * SSM_STATE].astype(BF16)
            cb = _dot(cg, bg, NT)
            for j in range(g * 4, g * 4 + 4):
                xs_p = x_ref[:, j * LANES:(j + 1) * LANES]
                dt_p, s_p, tot_p = _pair(dt, j, lo), _pair(s, j, lo), _pair(tot, j, lo[:1])
                xc_p = xs_p * dt_p
                xc_b = xc_p.astype(BF16)
                yd = []
                for hh in range(2):
                    lm, _ = _decay_masks(s, st, 2 * j + hh, row, lane)
                    yd.append(_dot((cb * lm).astype(BF16), xc_b))
                prev = state_ref[j]
                prevs_ref[0, j] = prev
                yo = _dot(cg, prev.astype(BF16)) * jnp.exp(s_p)
                y_ref[:, j * LANES:(j + 1) * LANES] = jnp.where(lo, yd[0], yd[1]) + yo + dexp_ref[:, j * LANES:(j + 1) * LANES] * xs_p
                to_end = jnp.exp(tot_p - s_p)
                state_ref[j] = jnp.exp(tot_p) * prev + _dot(bg, (xc_p * to_end).astype(BF16), TN)
        ymix_ref[...] = _gated_norm(y_ref[...], z_ref[...], w_ref[...]).astype(ymix_ref.dtype)

    vec = lambda w: pl.BlockSpec((1, w), lambda c: (0, 0))
    rows = pl.BlockSpec((CHUNK, D_MODEL), lambda c: (c, 0))
    return pl.pallas_call(
        body, name="ssd_fwd", grid=(nc,),
        in_specs=[pl.BlockSpec((CHUNK, D_MODEL), lambda c: (c, AL_XS // D_MODEL)),
                  pl.BlockSpec((CHUNK, CONV_DIM - D_MODEL), lambda c: (c, AL_B // (CONV_DIM - D_MODEL))),
                  pl.BlockSpec((CONV_W, CONV_DIM), lambda c: (0, 0)), vec(CONV_DIM),
                  pl.BlockSpec((CHUNK, LANES), lambda c: (c, AL_DTF // LANES)),
                  vec(LANES), vec(LANES), vec(D_MODEL), pl.BlockSpec((CHUNK, D_MODEL), lambda c: (c, AL_Z // D_MODEL)), vec(D_MODEL)],
        out_specs=[pl.BlockSpec((CHUNK, CONV_DIM), lambda c: (c, 0)), rows,
                   pl.BlockSpec((1, N_PAIRS, SSM_STATE, LANES), lambda c: (c, 0, 0, 0)), rows],
        out_shape=[jax.ShapeDtypeStruct((s_len, CONV_DIM), F32), jax.ShapeDtypeStruct((s_len, D_MODEL), F32),
                   jax.ShapeDtypeStruct((nc, N_PAIRS, SSM_STATE, LANES), F32), jax.ShapeDtypeStruct((s_len, 2 * D_MODEL), BF16)],
        scratch_shapes=[pltpu.VMEM((N_PAIRS, SSM_STATE, LANES), F32), pltpu.VMEM((SUBLANES, CONV_DIM), F32)],
        compiler_params=_params(("arbitrary",)),
    )(proj, proj, conv_w, conv_b, proj, dt_bias_l, a_log_l, d_exp, proj, norm_w)


def _ssd_bwd(xc_all, proj, dt_bias_l, a_log_l, d_exp, prevs, y_ssd, dymix, norm_w):
    s_len = xc_all.shape[0]
    nc = s_len // CHUNK

    def body(x_ref, dtf_ref, bias_ref, alog_ref, dexp_ref, prevs_ref, y_ref, z_ref, dym_ref, w_ref,
             dx_ref, ddt_ref, da_ref, dd_ref, dbias_ref, dz_ref, dw_ref, dstate_ref):
        @pl.when(pl.program_id(0) == 0)
        def _():
            dstate_ref[...] = jnp.zeros_like(dstate_ref)
            da_ref[...] = jnp.zeros_like(da_ref)
            dd_ref[...] = jnp.zeros_like(dd_ref)
            dbias_ref[...] = jnp.zeros_like(dbias_ref)
            dw_ref[...] = jnp.zeros_like(dw_ref)

        y, z, dyo = y_ref[...], z_ref[...], dym_ref[...]
        sg = _sigmoid(z)
        sz = z * sg
        gated = y * sz
        rn = lax.rsqrt(_rowmean(gated * gated) + RMS_EPS)
        dg = _rms_bwd(dyo * w_ref[...], gated, rn)
        dy_full = dg * sz
        dz_ref[...] = (dg * y * (sg * (1.0 + z * (1.0 - sg)))).astype(dz_ref.dtype)
        dw_ref[...] += _colsum(dyo * gated * rn)

        row, lane = _tile_iotas()
        lo = lane < HEAD_DIM
        last = row == CHUNK - 1
        raw, dt, a_neg, s = _ssd_scalars(dtf_ref, bias_ref, alog_ref, row, lane)
        tot = s[CHUNK - 1:CHUNK, :]
        st = s.T
        ds_acc = jnp.zeros((CHUNK, LANES), F32)
        ddt_acc = jnp.zeros((CHUNK, LANES), F32)
        for g in range(SSM_GROUPS):
            bcol = slice(XC_B + g * SSM_STATE, XC_B + (g + 1) * SSM_STATE)
            ccol = slice(XC_C + g * SSM_STATE, XC_C + (g + 1) * SSM_STATE)
            bg = x_ref[:, bcol].astype(BF16)
            cg = x_ref[:, ccol].astype(BF16)
            cb = _dot(cg, bg, NT)
            cbt = _dot(bg, cg, NT)
            dcb = jnp.zeros((CHUNK, LANES), F32)
            dcbt = jnp.zeros((CHUNK, LANES), F32)
            db_acc = jnp.zeros((CHUNK, LANES), F32)
            dc_acc = jnp.zeros((CHUNK, LANES), F32)
            for j in range(g * 4, g * 4 + 4):
                cols = slice(j * LANES, (j + 1) * LANES)
                xs_p, dy_p = x_ref[:, cols], dy_full[:, cols]
                dt_p, s_p, tot_p = _pair(dt, j, lo), _pair(s, j, lo), _pair(tot, j, lo[:1])
                xc_p = xs_p * dt_p
                xc_b, dy_b = xc_p.astype(BF16), dy_p.astype(BF16)
                e_p, f_p, etot_p = jnp.exp(s_p), jnp.exp(tot_p - s_p), jnp.exp(tot_p)
                prev, dnext = prevs_ref[0, j], dstate_ref[j]
                prev_b, dnext_b = prev.astype(BF16), dnext.astype(BF16)
                dd_ref[:, cols] += _colsum(dy_p * xs_p)
                dxs_p = dexp_ref[:, cols] * dy_p
                cp = _dot(cg, prev_b)
                gy = (dy_p * e_p).astype(BF16)
                dc_acc += _dot(gy, prev_b, NT)
                dstate_ref[j] = etot_p * dnext + _dot(cg, gy, TN)
                de = dy_p * cp * e_p
                bds = _dot(bg, dnext_b)
                db_acc += _dot((xc_p * f_p).astype(BF16), dnext_b, NT)
                dxc_p = bds * f_p
                df = bds * xc_p * f_p
                dtot_p = _colsum(dnext * prev) * etot_p + _colsum(df)
                dsl = de - df + jnp.where(last, dtot_p, 0.0)
                for hh in range(2):
                    h = 2 * j + hh
                    mine = lo == (hh == 0)
                    lm, lmt = _decay_masks(s, st, h, row, lane)
                    dy_h = jnp.where(mine, dy_p, 0.0).astype(BF16)
                    xc_h = jnp.where(mine, xc_p, 0.0).astype(BF16)
                    dm = _dot(dy_h, xc_b, NT)
                    dmt = _dot(xc_h, dy_b, NT)
                    mt = cbt * lmt
                    dxc_p += _dot(mt.astype(BF16), dy_h)
                    dml, dmtl = dm * lm, dmt * lmt
                    ds_h = jnp.sum(dml * cb - dmtl * cbt + jnp.where(mine, dsl, 0.0), axis=1, keepdims=True)
                    ds_acc += jnp.where(lane == h, ds_h, 0.0)
                    dcb += dml
                    dcbt += dmtl
                    ddt_acc += jnp.where(lane == h, _head_sum(dxc_p * xs_p, lo, hh), 0.0)
                dx_ref[:, cols] = dxs_p + dxc_p * dt_p
            dx_ref[:, ccol] = dc_acc + _dot(dcb.astype(BF16), bg)
            dx_ref[:, bcol] = db_acc + _dot(dcbt.astype(BF16), cg)
        triu = (row <= lane).astype(BF16)
        da = _dot3(triu, ds_acc)
        ddt = ddt_acc + da * a_neg
        da_ref[...] += _colsum(da * dt) * a_neg[:1]
        ddt_raw = jnp.where(lane < N_HEADS, ddt * _sigmoid(raw), 0.0)
        dbias_ref[...] += _colsum(ddt_raw)
        ddt_ref[...] = ddt_raw

    rev = lambda c: nc - 1 - c
    vec = lambda w: pl.BlockSpec((1, w), lambda c: (0, 0))
    rows = lambda cb: pl.BlockSpec((CHUNK, D_MODEL), lambda c: (rev(c), cb))
    return pl.pallas_call(
        body, name="ssd_bwd", grid=(nc,),
        in_specs=[pl.BlockSpec((CHUNK, CONV_DIM), lambda c: (rev(c), 0)), pl.BlockSpec((CHUNK, LANES), lambda c: (rev(c), AL_DTF // LANES)),
                  vec(LANES), vec(LANES), vec(D_MODEL),
                  pl.BlockSpec((1, N_PAIRS, SSM_STATE, LANES), lambda c: (rev(c), 0, 0, 0)),
                  rows(0), rows(AL_Z // D_MODEL), rows(0), vec(D_MODEL)],
        out_specs=[pl.BlockSpec((CHUNK, CONV_DIM), lambda c: (rev(c), 0)), pl.BlockSpec((CHUNK, LANES), lambda c: (rev(c), 0)),
                   vec(LANES), vec(D_MODEL), vec(LANES), rows(AL_Z // D_MODEL), vec(D_MODEL)],
        out_shape=[jax.ShapeDtypeStruct((s_len, CONV_DIM), F32), jax.ShapeDtypeStruct((s_len, LANES), F32),
                   jax.ShapeDtypeStruct((1, LANES), F32), jax.ShapeDtypeStruct((1, D_MODEL), F32), jax.ShapeDtypeStruct((1, LANES), F32),
                   jax.ShapeDtypeStruct((s_len, AL_COLS), BF16), jax.ShapeDtypeStruct((1, D_MODEL), F32)],
        scratch_shapes=[pltpu.VMEM((N_PAIRS, SSM_STATE, LANES), F32)],
        compiler_params=_params(("arbitrary",)),
    )(xc_all, proj, dt_bias_l, a_log_l, d_exp, prevs, y_ssd, proj, dymix, norm_w)


AUG_LANES = 6


def _aug_base(hh):
    return HEAD_DIM if hh == 0 else 0


NEG = -1e30
ATT_T = 512


def _fox_cum(proj, f_bias_l):
    s_len = proj.shape[0]
    nc = s_len // CHUNK

    def body(dtf_ref, fb_ref, cum_ref):
        row, lane = _tile_iotas()
        tril = (row >= lane).astype(BF16)
        spread = [(jnp.where(lane == AUG_LANES * row + i, 1.0, 0.0) - jnp.where(lane == AUG_LANES * row + 3 + i, 1.0, 0.0)).astype(BF16)
                  for i in range(3)]

        def step(c, carry):
            rows = pl.ds(pl.multiple_of(c * CHUNK, CHUNK), CHUNK)
            lf = -_softplus(-(dtf_ref[rows, :] + fb_ref[...]))
            lf = jnp.where(lane < N_HEADS, pltpu.roll(lf, LANES - F_LANE, 1), 0.0)
            cs = _dot3(tril, lf) + carry
            parts = _split3(cs)
            cum_ref[rows, :] = _dot(parts[0], spread[0]) + _dot(parts[1], spread[1]) + _dot(parts[2], spread[2])
            return cs[CHUNK - 1:CHUNK, :]

        lax.fori_loop(0, nc, step, jnp.zeros((1, LANES), F32))

    return pl.pallas_call(
        body, name="fox_cum", grid=(1,),
        in_specs=[pl.BlockSpec((s_len, LANES), lambda i: (0, AL_DTF // LANES)), pl.BlockSpec((1, LANES), lambda i: (0, 0))],
        out_specs=pl.BlockSpec((s_len, LANES), lambda i: (0, 0)),
        out_shape=jax.ShapeDtypeStruct((s_len, LANES), F32),
        compiler_params=_params(("arbitrary",)),
    )(proj, f_bias_l)


def _fox_cum_bwd(dcum, proj, f_bias_l, ddt_tile, dproj):
    s_len = proj.shape[0]
    nc = s_len // CHUNK

    def body(dcum_ref, dtf_ref, fb_ref, ddt_ref, _, out_ref, dfb_ref):
        row, lane = _tile_iotas()
        triu = (row <= lane).astype(BF16)
        is_f = (lane >= F_LANE) & (lane < F_LANE + N_HEADS)

        def step(t, carry):
            run, dfb = carry
            rows = pl.ds(pl.multiple_of((nc - 1 - t) * CHUNK, CHUNK), CHUNK)
            rc = _dot3(triu, dcum_ref[rows, :]) + run
            sg = _sigmoid(-(dtf_ref[rows, :] + fb_ref[...]))
            df = jnp.where(is_f, pltpu.roll(rc, F_LANE, 1) * sg, 0.0)
            out_ref[rows, :] = (df + ddt_ref[rows, :]).astype(out_ref.dtype)
            return rc[0:1, :], dfb + _colsum(df)

        _, dfb = lax.fori_loop(0, nc, step, (jnp.zeros((1, LANES), F32), jnp.zeros((1, LANES), F32)))
        dfb_ref[...] = dfb

    whole = pl.BlockSpec((s_len, LANES), lambda i: (0, 0))
    dtf_cols = pl.BlockSpec((s_len, LANES), lambda i: (0, AL_DTF // LANES))
    vec = pl.BlockSpec((1, LANES), lambda i: (0, 0))
    return pl.pallas_call(
        body, name="fox_cum_bwd", grid=(1,),
        in_specs=[whole, dtf_cols, vec, whole, ANY], out_specs=[dtf_cols, vec],
        out_shape=[jax.ShapeDtypeStruct(dproj.shape, dproj.dtype), jax.ShapeDtypeStruct((1, LANES), F32)],
        input_output_aliases={4: 0}, compiler_params=_params(("arbitrary",)),
    )(dcum, proj, f_bias_l, ddt_tile, dproj)


def _attn_prep(proj, cum):
    s_len = proj.shape[0]
    tr = min(256, s_len)

    def body(q_ref, k_ref, v_ref, cum_ref, qa_ref, ka_ref, vb_ref):
        lane = lax.broadcasted_iota(jnp.int32, (tr, LANES), 1)
        lo = lane < HEAD_DIM
        c = cum_ref[...]
        for p in range(N_PAIRS):
            cols = slice(p * LANES, (p + 1) * LANES)
            q, k = q_ref[:, cols] * (HEAD_DIM ** -0.5), k_ref[:, cols]
            for hh in range(2):
                base = _aug_base(hh)
                r = pltpu.roll(c, (base - AUG_LANES * (2 * p + hh)) % LANES, 1)
                first = (lane >= base) & (lane < base + 3)
                second = (lane >= base + 3) & (lane < base + AUG_LANES)
                mine = lo == (hh == 0)
                qa_ref[2 * p + hh] = jnp.where(mine, q, jnp.where(first, r, jnp.where(second, 1.0, 0.0))).astype(BF16)
                ka_ref[2 * p + hh] = jnp.where(mine, k, jnp.where(first, 1.0, jnp.where(second, r, 0.0))).astype(BF16)
        vb_ref[...] = v_ref[...].astype(BF16)

    assert AL_Q % D_MODEL == 0 and AL_K % D_MODEL == 0 and AL_V % D_MODEL == 0
    slab = lambda col0: pl.BlockSpec((tr, D_MODEL), lambda i: (i, col0 // D_MODEL))
    heads = pl.BlockSpec((N_HEADS, tr, LANES), lambda i: (0, i, 0))
    return pl.pallas_call(
        body, name="attn_prep", grid=(s_len // tr,),
        in_specs=[slab(AL_Q), slab(AL_K), slab(AL_V), pl.BlockSpec((tr, LANES), lambda i: (i, 0))],
        out_specs=[heads, heads, pl.BlockSpec((tr, D_MODEL), lambda i: (i, 0))],
        out_shape=[jax.ShapeDtypeStruct((N_HEADS, s_len, LANES), BF16), jax.ShapeDtypeStruct((N_HEADS, s_len, LANES), BF16),
                   jax.ShapeDtypeStruct((s_len, D_MODEL), BF16)],
        compiler_params=_params(("parallel",)),
    )(proj, proj, proj, cum)


def _attn_fwd(qa, ka, vb, halves):
    s_len = vb.shape[0]
    t = min(ATT_T, s_len)
    nq = s_len // t
    n = len(halves)

    def body(qa_ref, ka_ref, vb_ref, *rest):
        o_ref, lse_ref = rest[n:n + 2]
        start, finish = _gather_plan(rest[:n], rest[n + 2:2 * n + 2], *rest[2 * n + 2:])
        i = pl.program_id(1)
        pl.when((pl.program_id(0) == 0) & (i == 0))(start)
        row = lax.broadcasted_iota(jnp.int32, (t, t), 0)
        col = lax.broadcasted_iota(jnp.int32, (t, t), 1)
        lo = lax.broadcasted_iota(jnp.int32, (t, LANES), 1) < HEAD_DIM
        qs = (qa_ref[0], qa_ref[1])

        def block(j, carry, masked):
            rows = pl.ds(pl.multiple_of(j * t, t), t)
            v = vb_ref[rows, :]
            new = []
            for hh in range(2):
                m, l, acc = carry[hh]
                s = _dot(qs[hh], ka_ref[hh, rows, :], NT)
                if masked:
                    s = jnp.where(row >= col, s, NEG)
                m_new = jnp.maximum(m, jnp.max(s, axis=1, keepdims=True))
                alpha = jnp.exp(m - m_new)
                p = jnp.exp(s - m_new)
                new.append((m_new, alpha * l + jnp.sum(p, axis=1, keepdims=True), alpha * acc + _dot(p.astype(BF16), v)))
            return tuple(new)

        init = (jnp.full((t, 1), NEG, F32), jnp.zeros((t, 1), F32), jnp.zeros((t, LANES), F32))
        carry = lax.fori_loop(0, i, functools.partial(block, masked=False), (init, init))
        (m0, l0, acc0), (m1, l1, acc1) = block(i, carry, True)
        o_ref[...] = jnp.where(lo, acc0 / l0, acc1 / l1)
        lse_ref[...] = jnp.where(lo, m0 + jnp.log(l0), m1 + jnp.log(l1))
        pl.when((pl.program_id(0) == N_PAIRS - 1) & (i == nq - 1))(finish)

    out = pl.BlockSpec((t, LANES), lambda p, i: (i, p))
    res = pl.pallas_call(
        body, name="attn_fwd", grid=(N_PAIRS, nq),
        in_specs=[pl.BlockSpec((2, t, LANES), lambda p, i: (p, i, 0)), pl.BlockSpec((2, s_len, LANES), lambda p, i: (p, 0, 0)),
                  pl.BlockSpec((s_len, LANES), lambda p, i: (0, p))] + [ANY] * n,
        out_specs=[out, out] + [ANY] * n,
        out_shape=[jax.ShapeDtypeStruct((s_len, D_MODEL), F32), jax.ShapeDtypeStruct((s_len, D_MODEL), F32)]
        + [jax.ShapeDtypeStruct((N_CHIPS, *h.shape), h.dtype) for h in halves],
        scratch_shapes=_exchange_sems(n),
        compiler_params=_params(("arbitrary", "arbitrary")),
    )(qa, ka, vb, *halves)
    return res[0], res[1], res[2:]


def _attn_bwd(qa, ka, vb, o, lse, do, parts, dproj):
    s_len = vb.shape[0]
    t = min(ATT_T, s_len)
    nq = s_len // t
    n = len(parts)

    def body(qa_ref, ka_ref, vb_ref, o_ref, lse_ref, do_ref, *rest):
        dqa_ref, dka_ref, dv_ref = rest[n + 1:n + 4]
        start, finish = _reduce_plan(rest[:n], rest[n + 4:2 * n + 4], *rest[2 * n + 4:])
        j = pl.program_id(1)
        pl.when((pl.program_id(0) == 0) & (j == 0))(start)

        @pl.when(j == 0)
        def _():
            dqa_ref[...] = jnp.zeros_like(dqa_ref)

        row = lax.broadcasted_iota(jnp.int32, (t, t), 0)
        col = lax.broadcasted_iota(jnp.int32, (t, t), 1)
        lo = lax.broadcasted_iota(jnp.int32, (t, LANES), 1) < HEAD_DIM
        v = vb_ref[...]
        ks = (ka_ref[0], ka_ref[1])

        def block(i, carry, masked):
            dk, dv = list(carry[:2]), carry[2]
            rows = pl.ds(pl.multiple_of(i * t, t), t)
            do_p, o_p, lse_p = do_ref[rows, :], o_ref[rows, :], lse_ref[rows, :]
            for hh in range(2):
                q = qa_ref[hh, rows, :]
                do_h = jnp.where(lo == (hh == 0), do_p, 0.0)
                delta = jnp.sum(do_h * o_p, axis=1, keepdims=True)
                s = _dot(q, ks[hh], NT)
                if masked:
                    s = jnp.where(row >= col, s, NEG)
                p = jnp.exp(s - lse_p[:, hh * HEAD_DIM:hh * HEAD_DIM + 1])
                do_b = do_h.astype(BF16)
                ds = (p * (_dot(do_b, v, NT) - delta)).astype(BF16)
                dv = dv + _dot(p.astype(BF16), do_b, TN)
                dk[hh] = dk[hh] + _dot(ds, q, TN)
                dqa_ref[hh, rows, :] += _dot(ds, ks[hh])
            return dk[0], dk[1], dv

        zero = jnp.zeros((t, LANES), F32)
        carry = block(j, (zero, zero, zero), True)
        dk0, dk1, dv = lax.fori_loop(j + 1, nq, functools.partial(block, masked=False), carry)
        dka_ref[0] = dk0
        dka_ref[1] = dk1
        dv_ref[...] = dv.astype(dv_ref.dtype)
        pl.when((pl.program_id(0) == N_PAIRS - 1) & (j == nq - 1))(finish)

    whole_pair = pl.BlockSpec((2, s_len, LANES), lambda p, j: (p, 0, 0))
    blk_pair = pl.BlockSpec((2, t, LANES), lambda p, j: (p, j, 0))
    whole_cols = pl.BlockSpec((s_len, LANES), lambda p, j: (0, p))
    blk_cols = pl.BlockSpec((t, LANES), lambda p, j: (j, p))
    res = pl.pallas_call(
        body, name="attn_bwd", grid=(N_PAIRS, nq),
        in_specs=[whole_pair, blk_pair, blk_cols, whole_cols, whole_cols, whole_cols] + [ANY] * (n + 1),
        out_specs=[whole_pair, blk_pair, pl.BlockSpec((t, LANES), lambda p, j: (j, AL_V // LANES + p))] + [ANY] * n,
        out_shape=[jax.ShapeDtypeStruct((N_HEADS, s_len, LANES), F32), jax.ShapeDtypeStruct((N_HEADS, s_len, LANES), F32),
                   jax.ShapeDtypeStruct(dproj.shape, dproj.dtype)]
        + [jax.ShapeDtypeStruct((N_DEV, g.shape[1] // 2, g.shape[2]), g.dtype) for g in parts],
        scratch_shapes=_exchange_sems(n), input_output_aliases={6 + n: 2},
        compiler_params=_params(("arbitrary", "arbitrary")),
    )(qa, ka, vb, o, lse, do, *parts, dproj)
    return res[0], res[1], res[2], res[3:]


def _attn_post(dqa, dka, dproj):
    s_len = dqa.shape[1]
    tr = min(256, s_len)
    assert AL_K == AL_Q + D_MODEL and AL_Q % (2 * D_MODEL) == 0

    def body(dqa_ref, dka_ref, _, dqk_ref, dcum_ref):
        lane = lax.broadcasted_iota(jnp.int32, (tr, LANES), 1)
        lo = lane < HEAD_DIM
        dcum = jnp.zeros((tr, LANES), F32)
        for p in range(N_PAIRS):
            a0, a1, b0, b1 = dqa_ref[2 * p], dqa_ref[2 * p + 1], dka_ref[2 * p], dka_ref[2 * p + 1]
            dq = jnp.where(lo, a0, a1) * (HEAD_DIM ** -0.5)
            dqk_ref[:, p * LANES:(p + 1) * LANES] = dq.astype(dqk_ref.dtype)
            dqk_ref[:, D_MODEL + p * LANES:D_MODEL + (p + 1) * LANES] = jnp.where(lo, b0, b1).astype(dqk_ref.dtype)
            for hh, (a, b) in enumerate(((a0, b0), (a1, b1))):
                base = _aug_base(hh)
                dcum = dcum + jnp.where(lane == 2 * p + hh, a[:, base:base + 1] - b[:, base + 3:base + 4], 0.0)
        dcum_ref[...] = dcum

    heads = pl.BlockSpec((N_HEADS, tr, LANES), lambda i: (0, i, 0))
    return pl.pallas_call(
        body, name="attn_post", grid=(s_len // tr,),
        in_specs=[heads, heads, ANY],
        out_specs=[pl.BlockSpec((tr, 2 * D_MODEL), lambda i: (i, AL_Q // (2 * D_MODEL))), pl.BlockSpec((tr, LANES), lambda i: (i, 0))],
        out_shape=[jax.ShapeDtypeStruct(dproj.shape, dproj.dtype), jax.ShapeDtypeStruct((s_len, LANES), F32)],
        input_output_aliases={2: 0}, compiler_params=_params(("parallel",)),
    )(dqa, dka, dproj)


def _ln_stats(r):
    mu = _rowmean(r)
    xc = r - mu
    rstd = lax.rsqrt(_rowmean(xc * xc) + LN_EPS)
    return xc * rstd, rstd


def _ln_bwd(dxh, xh, rstd):
    return rstd * (dxh - _rowmean(dxh) - xh * _rowmean(dxh * xh))


def _rms_bwd(dgn, g, r):
    return r * dgn - (r * r * r) * g * _rowmean(dgn * g)


def _to_aligned(wt):
    out = jnp.zeros((AL_COLS, wt.shape[1]), wt.dtype)
    for dst, (lo, hi) in ((0, (0, 2048)), (AL_Q, (2576, 5648)), (AL_B, (2048, 2560)), (AL_DTF, (2560, 2576)), (AL_DTF + 16, (5648, 5664))):
        out = lax.dynamic_update_slice_in_dim(out, wt[lo:hi], dst, axis=0)
    return out


def _from_aligned(gt):
    out = jnp.zeros((IN_COLS, gt.shape[1]), gt.dtype)
    for dst, (lo, hi) in ((0, (0, AL_Q)), (2048, (AL_B, AL_DTF)), (2560, (AL_DTF, AL_DTF + 16)), (2576, (AL_Q, AL_B)),
                          (5648, (AL_DTF + 16, AL_DTF + 32))):
        out = lax.dynamic_update_slice_in_dim(out, gt[lo:hi], dst, axis=0)
    return out


def _lanes(v, at=0):
    return jnp.pad(v, ((0, 0), (at, LANES - at - v.shape[1])))


def _local_step(x, tgt, mod, w_alt, halves, sp):
    d = D_MODEL
    sh1, sc1, g1, sh2, sc2, g2 = [mod[:, i * d:(i + 1) * d] for i in range(6)]
    dt_bias_l, a_log_l, f_bias_l = _lanes(sp["dt_bias"]), _lanes(sp["a_log"]), _lanes(sp["f_bias"], F_LANE)
    d_exp = jnp.repeat(sp["d_skip"], HEAD_DIM, axis=1)

    (h1,), _ = _rowwise("mod1", lambda x, sc, sh: ([x * (1.0 + sc) + sh], []), [x], [sc1, sh1], [(d, BF16)], [])
    proj = _matmul("proj", h1, w_alt, dims=NT, tn=1152)
    xc_all, y_ssd, prevs, y_mix = _ssd_fwd(proj, sp["conv_w"], sp["conv_b"], dt_bias_l, a_log_l, d_exp, sp["ssm_norm_w"])
    cum = _fox_cum(proj, f_bias_l)
    qa, ka, vb = _attn_prep(proj, cum)
    o, lse, (g_out, g_fi, g_fo) = _attn_fwd(qa, ka, vb, halves)
    w_out = g_out.reshape(2 * d, d)
    w_fi = g_fi.transpose(1, 0, 2).reshape(d, D_FF)
    w_fo = g_fo.reshape(D_FF, d)
    (y_mix,), _ = _rowwise("attn_norm", lambda o, w: ([o * lax.rsqrt(_rowmean(o * o) + RMS_EPS) * w], []),
                           [o], [sp["attn_norm_w"]], [(d, BF16, (2 * d, 1, y_mix))], [])
    def ln1_fwd(y, x, g1, sc2, sh2, lg, lb):
        r1 = ALPHA * x + (1.0 + g1) * y
        xh, _ = _ln_stats(r1)
        x1 = xh * lg + lb
        h2 = x1 * (1.0 + sc2) + sh2
        return [y, r1, h2, h2.T], []

    def relu2(u):
        a = jnp.square(jnp.maximum(u, 0.0))
        return [a, a.T], []

    y, r1, h2, h2_t = _matmul("out_proj", y_mix, w_out, tm=512, tk=2048,
                              epi=(ln1_fwd, [x], [g1, sc2, sh2, sp["ln1_g"], sp["ln1_b"]], [F32, F32, BF16, ("T", BF16)], []))
    act, act_t = _matmul("ff_in", h2, w_fi, epi=(relu2, [], [], [BF16, ("T", BF16)], []))

    def head(ff, r1, tgt, g2, l1g, l1b, l2g, l2b):
        xh1, _ = _ln_stats(r1)
        x1 = xh1 * l1g + l1b
        xh2, rstd2 = _ln_stats(ALPHA * x1 + (1.0 + g2) * ff)
        err = xh2 * l2g + l2b - tgt
        loss = 0.5 * jnp.sum(_rowmean(err * err))
        dx2 = err * (1.0 / d)
        dr2 = _ln_bwd(dx2 * l2g, xh2, rstd2)
        return ([dr2, (1.0 + g2) * dr2],
                [_colsum(dx2 * xh2), _colsum(dx2), _colsum(dr2 * ff), jnp.full((1, LANES), loss, F32)])

    dr2, dff, d_ln2_g, d_ln2_b, d_g2, loss = _matmul(
        "ff_out", act, w_fo, tm=512, tk=D_FF,
        epi=(head, [r1, tgt], [g2, sp["ln1_g"], sp["ln1_b"], sp["ln2_g"], sp["ln2_b"]], [F32, BF16], [d, d, d, LANES]))
    du = _matmul("d_act", dff, w_fo, dims=NT, epi=(lambda da, act: ([da * (2.0 * jnp.sqrt(act.astype(F32)))], []), [act], [], [BF16], []))
    dw_fo = _matmul("dw_ff_out", act_t, dff, tk=SEQ_TK, out_dtype=BF16, by_chip="rows")
    dw_fi = _matmul("dw_ff_in", h2_t, du, tk=SEQ_TK, out_dtype=BF16, by_chip="cols")

    def ln1_bwd(dh2, r1, dr2, y, sc2, g1, lg, lb):
        xh, rstd = _ln_stats(r1)
        x1 = xh * lg + lb
        dx1 = ALPHA * dr2 + dh2 * (1.0 + sc2)
        dr1 = _ln_bwd(dx1 * lg, xh, rstd)
        return ([dr1, (1.0 + g1) * dr1],
                [_colsum(dh2 * x1), _colsum(dh2), _colsum(dx1 * xh), _colsum(dx1), _colsum(dr1 * y)])

    dr1, dy, d_sc2, d_sh2, d_ln1_g, d_ln1_b, d_g1 = _matmul(
        "dh2", du, w_fi, dims=NT, tm=512, tk=D_FF,
        epi=(ln1_bwd, [r1, dr2, y], [sc2, g1, sp["ln1_g"], sp["ln1_b"]], [F32, BF16], [d] * 5))
    dw_out = _matmul("dw_out", y_mix, dy, dims=TN, tk=SEQ_TK, out_dtype=BF16, by_chip="rows")

    def attn_norm_bwd(dyo, o, w):
        r = lax.rsqrt(_rowmean(o * o) + RMS_EPS)
        return [_rms_bwd(dyo * w, o, r)], [_colsum(dyo * o * r)]

    dymix = _matmul("dy_mix_ssm", dy, w_out[:d], dims=NT)
    do, d_attn_w = _matmul("dy_mix_att", dy, w_out[d:], dims=NT, tm=512, epi=(attn_norm_bwd, [o], [sp["attn_norm_w"]], [F32], [d]))

    dxc, ddt_tile, d_alog_l, d_dexp, d_dtb_l, dproj, d_ssm_w = _ssd_bwd(
        xc_all, proj, dt_bias_l, a_log_l, d_exp, prevs, y_ssd, dymix, sp["ssm_norm_w"])
    dqa, dka, dproj, landed = _attn_bwd(qa, ka, vb, o, lse, do, [dw_out, dw_fi, dw_fo], dproj)
    dproj, dcum = _attn_post(dqa, dka, dproj)
    dproj, d_fb_l = _fox_cum_bwd(dcum, proj, f_bias_l, ddt_tile, dproj)
    dpre, d_conv_w, d_conv_b = _conv_bwd_pre(proj, sp["conv_w"], sp["conv_b"], dxc)
    dproj = _conv_bwd_in(dpre, sp["conv_w"], dproj)
    dw_alt = _matmul("dw_in", dproj, h1, dims=TN, tm=1152, tk=SEQ_TK, out_dtype=BF16)
    part_in = _from_aligned(dw_alt).reshape(N_CHIPS, IN_COLS // N_CHIPS, d)

    def last(dh1, x, dr1, sc1):
        return [ALPHA * dr1 + dh1 * (1.0 + sc1)], [_colsum(dh1 * x), _colsum(dh1)]

    chip_in = _pair_sum(part_in, _pair_exchange(part_in), lax.axis_index("c"))
    dx, d_sc1, d_sh1, landed_in = _matmul("dh1", dproj, w_alt, tm=512, tk=AL_COLS, carry=[chip_in],
                                          epi=(last, [x, dr1], [sc1], [F32], [d, d]))

    small = {
        "mod": jnp.concatenate([d_sh1, d_sc1, d_g1, d_sh2, d_sc2, d_g2], axis=1),
        "conv_w": d_conv_w, "conv_b": d_conv_b,
        "dt_bias": d_dtb_l[:, :N_HEADS], "a_log": d_alog_l[:, :N_HEADS],
        "d_skip": jnp.sum(d_dexp.reshape(N_HEADS, HEAD_DIM), axis=1)[None, :],
        "ssm_norm_w": d_ssm_w, "f_bias": d_fb_l[:, F_LANE:F_LANE + N_HEADS], "attn_norm_w": d_attn_w,
        "ln1_g": d_ln1_g, "ln1_b": d_ln1_b, "ln2_g": d_ln2_g, "ln2_b": d_ln2_b, "loss": loss,
    }
    return dx, [landed_in, *landed], small


N_DEV = 8
N_CHIPS = 4
ANY = pl.BlockSpec(memory_space=pl.ANY)
VMEM_SPEC = pl.BlockSpec(memory_space=pltpu.VMEM)


def _place():
    x, y, c = lax.axis_index("x"), lax.axis_index("y"), lax.axis_index("c")
    return x, y, c


def _other_chips(x, y):
    return [(1 - x, y, 2 * (1 - x) + y), (x, 1 - y, 2 * x + 1 - y), (1 - x, 1 - y, 2 * (1 - x) + 1 - y)]


def _small_gather(v_ref, out_ref, send_sems, recv_sems, local_sem):
    x, y, c = _place()
    me = 4 * x + 2 * y + c
    mine = pltpu.make_async_copy(v_ref, out_ref.at[me], local_sem)
    mine.start()
    peers = _peers(x, y, c)

    def copy(rel, slot, to):
        return pltpu.make_async_remote_copy(src_ref=v_ref, dst_ref=out_ref.at[slot], send_sem=send_sems.at[rel],
                                            recv_sem=recv_sems.at[rel], device_id=to, device_id_type=MESH)

    sends = [copy(rel, me, peer) for rel, peer in enumerate(peers)]
    for cp in sends:
        cp.start()
    for rel, (px, py, pc) in enumerate(peers):
        copy(rel, 4 * px + 2 * py + pc, (x, y, c)).wait_recv()
    for cp in sends:
        cp.wait_send()
    mine.wait()


SMALL_GATHER_SEMS = [pltpu.SemaphoreType.DMA((N_DEV - 1,)), pltpu.SemaphoreType.DMA((N_DEV - 1,)), pltpu.SemaphoreType.DMA]


def _allgather_small(name, v, with_sum=False):
    def body(v_ref, out_ref, *rest):
        _small_gather(v_ref, out_ref, *rest[-3:])
        if with_sum:
            acc = out_ref[0]
            for dev in range(1, N_DEV):
                acc = acc + out_ref[dev]
            rest[0][...] = acc

    every = jax.ShapeDtypeStruct((N_DEV, *v.shape), v.dtype)
    return pl.pallas_call(
        body, name=name, out_shape=[every, jax.ShapeDtypeStruct(v.shape, v.dtype)] if with_sum else every,
        in_specs=[VMEM_SPEC], out_specs=[VMEM_SPEC] * 2 if with_sum else VMEM_SPEC, scratch_shapes=SMALL_GATHER_SEMS,
    )(v)


def _gather_shards(shard):
    def body(in_ref, out_ref, stage, send_sems, recv_sems, local_sems):
        start, finish = _shard_gather_plan(in_ref, out_ref, stage, send_sems, recv_sems, local_sems)
        start()
        finish()

    return pl.pallas_call(
        body, name="gather_w_in", out_shape=jax.ShapeDtypeStruct((N_CHIPS, *shard.shape), shard.dtype),
        in_specs=[ANY], out_specs=ANY,
        scratch_shapes=[pltpu.VMEM(shard.shape, shard.dtype), pltpu.SemaphoreType.DMA((6,)), pltpu.SemaphoreType.DMA((6,)),
                        pltpu.SemaphoreType.DMA((2,))],
        compiler_params=_params(),
    )(shard)


def _shard_gather_plan(in_ref, out_ref, stage, send_sems, recv_sems, local_sems):
    ch = in_ref.shape[1] // 2
    x, y, c = _place()
    k_me = 2 * x + y
    me, sibling = (x, y, c), (x, y, 1 - c)
    chips = _other_chips(x, y)

    def copy(idx, k, half, to, src=None):
        cols = out_ref.at[k, :, pl.ds(pl.multiple_of(half * ch, ch), ch)]
        return pltpu.make_async_remote_copy(src_ref=cols if src is None else src, dst_ref=cols, send_sem=send_sems.at[idx],
                                            recv_sem=recv_sems.at[idx], device_id=to, device_id_type=MESH)

    mine = in_ref.at[:, pl.ds(pl.multiple_of(c * ch, ch), ch)]
    sends = [copy(j, k_me, c, (cx, cy, c), src=mine) for j, (cx, cy, _) in enumerate(chips)]
    load = pltpu.make_async_copy(in_ref, stage, local_sems.at[0])
    store = pltpu.make_async_copy(stage, out_ref.at[k_me], local_sems.at[1])

    def start():
        for cp in sends:
            cp.start()
        load.start()

    def finish():
        load.wait()
        store.start()
        forwards = []
        for j, (_, _, kj) in enumerate(chips):
            copy(j, kj, c, me).wait_recv()
            forwards.append(copy(3 + j, kj, c, sibling))
            forwards[-1].start()
        for j, (_, _, kj) in enumerate(chips):
            copy(3 + j, kj, 1 - c, me).wait_recv()
        for cp in sends + forwards:
            cp.wait_send()
        store.wait()

    return start, finish


def _peers(x, y, c):
    return [((1 - x) if rel & 4 else x, (1 - y) if rel & 2 else y, (1 - c) if rel & 1 else c) for rel in range(1, N_DEV)]


def _exchange_sems(n):
    return [pltpu.SemaphoreType.DMA((n, N_DEV - 1)), pltpu.SemaphoreType.DMA((n, N_DEV - 1)), pltpu.SemaphoreType.DMA((n,))]


def _gather_plan(ins, outs, send_sems, recv_sems, local_sems):
    x, y, c = _place()
    k_me = 2 * x + y
    peers = [(rel, p) for rel, p in enumerate(_peers(x, y, c)) if (rel + 1) & 6]

    def copy(w, rel, k, half, to, src=None):
        rh = ins[w].shape[0] // 2
        rows = outs[w].at[k, pl.ds(pl.multiple_of(half * rh, rh), rh), :]
        return pltpu.make_async_remote_copy(src_ref=rows if src is None else src, dst_ref=rows, send_sem=send_sems.at[w, rel],
                                            recv_sem=recv_sems.at[w, rel], device_id=to, device_id_type=MESH)

    def mine(w):
        rh = ins[w].shape[0] // 2
        return ins[w].at[pl.ds(pl.multiple_of(c * rh, rh), rh), :]

    n = len(ins)
    local = [pltpu.make_async_copy(ins[w], outs[w].at[k_me], local_sems.at[w]) for w in range(n)]
    sends = [copy(w, rel, k_me, c, peer, src=mine(w)) for w in range(n) for rel, peer in peers]

    def start():
        for cp in local + sends:
            cp.start()

    def finish():
        for w in range(n):
            for rel, (px, py, pc) in peers:
                copy(w, rel, 2 * px + py, pc, (x, y, c)).wait_recv()
        for cp in sends:
            cp.wait_send()
        for cp in local:
            cp.wait()

    return start, finish


def _reduce_plan(ins, outs, send_sems, recv_sems, local_sems):
    x, y, c = _place()
    me = 4 * x + 2 * y + c
    peers = _peers(x, y, c)

    def block(w, k, half):
        rh = ins[w].shape[1] // 2
        return ins[w].at[k, pl.ds(pl.multiple_of(half * rh, rh), rh), :]

    def copy(w, rel, src, slot, to):
        return pltpu.make_async_remote_copy(src_ref=src, dst_ref=outs[w].at[slot], send_sem=send_sems.at[w, rel],
                                            recv_sem=recv_sems.at[w, rel], device_id=to, device_id_type=MESH)

    n = len(ins)
    local = [pltpu.make_async_copy(block(w, 2 * x + y, c), outs[w].at[me], local_sems.at[w]) for w in range(n)]
    sends = [copy(w, rel, block(w, 2 * px + py, pc), me, (px, py, pc)) for w in range(n) for rel, (px, py, pc) in enumerate(peers)]

    def start():
        for cp in local + sends:
            cp.start()

    def finish():
        for w in range(n):
            for rel, (px, py, pc) in enumerate(peers):
                copy(w, rel, block(w, 2 * x + y, c), 4 * px + 2 * py + pc, (x, y, c)).wait_recv()
        for cp in sends:
            cp.wait_send()
        for cp in local:
            cp.wait()

    return start, finish


def _scatter_plan(ins, outs, send_sems, recv_sems, local_sems):
    x, y, c = _place()
    k_me = 2 * x + y
    chips = _other_chips(x, y)

    def copy(w, j, src_k, dst_k, to):
        return pltpu.make_async_remote_copy(src_ref=ins[w].at[src_k], dst_ref=outs[w].at[dst_k], send_sem=send_sems.at[w, j],
                                            recv_sem=recv_sems.at[w, j], device_id=to, device_id_type=MESH)

    n = len(ins)
    local = [pltpu.make_async_copy(ins[w].at[k_me], outs[w].at[k_me], local_sems.at[w]) for w in range(n)]
    sends = [copy(w, j, kj, k_me, (cx, cy, c)) for w in range(n) for j, (cx, cy, kj) in enumerate(chips)]

    def start():
        for cp in local + sends:
            cp.start()

    def finish():
        for w in range(n):
            for j, (_, _, kj) in enumerate(chips):
                copy(w, j, k_me, kj, (x, y, c)).wait_recv()
        for cp in sends:
            cp.wait_send()
        for cp in local:
            cp.wait()

    return start, finish


def _row_tile(r, mult=2 * SUBLANES):
    if r % 256 == 0:
        return 256
    return max([t for t in range(mult, 513, mult) if r % t == 0], default=r)


def _pair_exchange(g):
    _, r, cdim = g.shape
    ch = cdim // 2

    def body(g_ref, got_ref, send_sem, recv_sem):
        x, y, c = _place()
        cp = pltpu.make_async_remote_copy(src_ref=g_ref.at[:, :, pl.ds(pl.multiple_of((1 - c) * ch, ch), ch)], dst_ref=got_ref,
                                          send_sem=send_sem, recv_sem=recv_sem, device_id=(x, y, 1 - c), device_id_type=MESH)
        cp.start()
        cp.wait_recv()
        cp.wait_send()

    return pl.pallas_call(
        body, name="pair_exchange", out_shape=jax.ShapeDtypeStruct((N_CHIPS, r, ch), g.dtype),
        in_specs=[ANY], out_specs=ANY, scratch_shapes=[pltpu.SemaphoreType.DMA, pltpu.SemaphoreType.DMA],
    )(g)


def _pair_sum(g, got, c):
    _, r, cdim = g.shape
    ch = cdim // 2
    tr = _row_tile(r)

    def body(c_ref, g_ref, got_ref, o_ref):
        o_ref[...] = (g_ref[...].astype(F32) + got_ref[...].astype(F32)).astype(o_ref.dtype)

    blk = pl.BlockSpec((1, tr, ch), lambda k, i, c_ref: (k, i, 0))
    return pl.pallas_call(
        body, name="pair_sum",
        grid_spec=pltpu.PrefetchScalarGridSpec(
            num_scalar_prefetch=1, grid=(N_CHIPS, r // tr),
            in_specs=[pl.BlockSpec((1, tr, ch), lambda k, i, c_ref: (k, i, c_ref[0])), blk], out_specs=blk),
        out_shape=jax.ShapeDtypeStruct((N_CHIPS, r, ch), BF16),
        compiler_params=_params(("parallel", "parallel")),
    )(jnp.reshape(c, (1,)).astype(jnp.int32), g, got)


def _sum_blocks(name, parts):
    k, r, cdim = parts.shape
    tr = _row_tile(r)

    def body(p_ref, o_ref):
        acc = p_ref[0].astype(F32)
        for i in range(1, k):
            acc = acc + p_ref[i].astype(F32)
        o_ref[...] = acc

    return pl.pallas_call(
        body, name=name, grid=(r // tr,),
        in_specs=[pl.BlockSpec((k, tr, cdim), lambda i: (0, i, 0))], out_specs=pl.BlockSpec((tr, cdim), lambda i: (i, 0)),
        out_shape=jax.ShapeDtypeStruct((r, cdim), F32), compiler_params=_params(("parallel",)),
    )(parts)


def _pair_swap(halves):
    n = len(halves)

    def body(*refs):
        ins, outs = refs[:n], refs[n:2 * n]
        send_sems, recv_sems = refs[2 * n:]
        x, y, c = _place()
        cps = [pltpu.make_async_remote_copy(src_ref=ins[w], dst_ref=outs[w], send_sem=send_sems.at[w], recv_sem=recv_sems.at[w],
                                            device_id=(x, y, 1 - c), device_id_type=MESH) for w in range(n)]
        for cp in cps:
            cp.start()
        for cp in cps:
            cp.wait_recv()
        for cp in cps:
            cp.wait_send()

    return pl.pallas_call(
        body, name="pair_swap", out_shape=[jax.ShapeDtypeStruct(h.shape, h.dtype) for h in halves],
        in_specs=[ANY] * n, out_specs=[ANY] * n,
        scratch_shapes=[pltpu.SemaphoreType.DMA((n,)), pltpu.SemaphoreType.DMA((n,))],
    )(*halves)


ADA_SHARD = 6 * D_MODEL // N_CHIPS


def _mod_part(c_all, w_shard, b_shard):
    tn = 512

    def body(c_ref, w_ref, b_ref, o_ref):
        o_ref[...] = _dot(_silu(c_ref[...]).astype(BF16), w_ref[...].astype(BF16)) + b_ref[...]

    return pl.pallas_call(
        body, name="mod_part", grid=(ADA_SHARD // tn,),
        in_specs=[pl.BlockSpec((N_DEV, D_MODEL), lambda j: (0, 0)), pl.BlockSpec((D_MODEL, tn), lambda j: (0, j)),
                  pl.BlockSpec((1, tn), lambda j: (0, j))],
        out_specs=pl.BlockSpec((N_DEV, tn), lambda j: (0, j)),
        out_shape=jax.ShapeDtypeStruct((N_DEV, ADA_SHARD), F32), compiler_params=_params(("parallel",)),
    )(c_all, w_shard, b_shard)


def _w_ada_grad(c_all_t, dmod_shard):
    tm = 256

    def body(ct_ref, dm_ref, o_ref):
        act = _silu(ct_ref[...])
        acc = act[:, 0:1] * dm_ref[0:1, :]
        for dev in range(1, N_DEV):
            acc = acc + act[:, dev:dev + 1] * dm_ref[dev:dev + 1, :]
        o_ref[...] = acc

    return pl.pallas_call(
        body, name="w_ada_grad", grid=(D_MODEL // tm,),
        in_specs=[pl.BlockSpec((tm, N_DEV), lambda i: (i, 0)), pl.BlockSpec((N_DEV, ADA_SHARD), lambda i: (0, 0))],
        out_specs=pl.BlockSpec((tm, ADA_SHARD), lambda i: (i, 0)),
        out_shape=jax.ShapeDtypeStruct((D_MODEL, ADA_SHARD), F32), compiler_params=_params(("parallel",)),
    )(c_all_t, dmod_shard)


def _adamw_math(w, g, m, v):
    nm = ADAM_B1 * m + (1.0 - ADAM_B1) * g
    nv = ADAM_B2 * v + (1.0 - ADAM_B2) * jnp.square(g)
    m_hat = nm / (1.0 - ADAM_B1 ** ADAM_STEP)
    v_hat = nv / (1.0 - ADAM_B2 ** ADAM_STEP)
    return -ADAM_LR * (m_hat / (jnp.sqrt(v_hat) + ADAM_EPS) + ADAM_WD * w), nm, nv


def _adamw(name, w, g, m, v):
    _, r, cdim = w.shape
    tr = 256 if r % 256 == 0 else r

    def body(w_ref, g_ref, m_ref, v_ref, go_ref, d_ref, nm_ref, nv_ref):
        go_ref[...] = g_ref[...]
        d_ref[...], nm_ref[...], nv_ref[...] = _adamw_math(w_ref[...], g_ref[...], m_ref[...], v_ref[...])

    blk = pl.BlockSpec((None, tr, cdim), lambda i: (0, i, 0))
    return pl.pallas_call(
        body, name=name, grid=(r // tr,), in_specs=[blk, pl.BlockSpec((tr, cdim), lambda i: (i, 0)), blk, blk], out_specs=[blk] * 4,
        out_shape=[jax.ShapeDtypeStruct((1, r, cdim), F32)] * 4, compiler_params=_params(("parallel",)),
    )(w, g, m, v)


def _adamw_pair(name, w, mine, other, m, v, c, by_cols=False):
    _, r, cdim = w.shape
    hr, hc = mine.shape
    tr = _row_tile(hr, SUBLANES)
    per = hr // tr

    def body(c_ref, w_ref, a_ref, b_ref, m_ref, v_ref, g_ref, d_ref, nm_ref, nv_ref):
        half = pl.program_id(1) if by_cols else pl.program_id(0) // per
        g = jnp.where(half == c_ref[0], a_ref[...], b_ref[...])
        g_ref[...] = g
        d_ref[...], nm_ref[...], nv_ref[...] = _adamw_math(w_ref[...], g, m_ref[...], v_ref[...])

    blk = pl.BlockSpec((None, tr, hc), lambda i, j, c_ref: (0, i, j))
    half = pl.BlockSpec((tr, hc), lambda i, j, c_ref: (i % per, 0))
    return pl.pallas_call(
        body, name=name,
        grid_spec=pltpu.PrefetchScalarGridSpec(num_scalar_prefetch=1, grid=(r // tr, cdim // hc),
                                               in_specs=[blk, half, half, blk, blk], out_specs=[blk] * 4),
        out_shape=[jax.ShapeDtypeStruct((1, r, cdim), F32)] * 4, compiler_params=_params(("parallel", "parallel")),
    )(jnp.reshape(c, (1,)).astype(jnp.int32), w, mine, other, m, v)


SMALL = ["b_ada", "conv_b", "dt_bias", "a_log", "d_skip", "ssm_norm_w", "f_bias", "attn_norm_w", "ln1_g", "ln1_b", "ln2_g", "ln2_b"]


def _pack(vs):
    pieces = []
    for v in vs:
        pieces.append(v)
        if v.shape[1] % LANES:
            pieces.append(jnp.zeros((1, -v.shape[1] % LANES), v.dtype))
    return jnp.concatenate(pieces, axis=1)


def _adamw_small(total, offs, ws, ms, vs):
    n = len(ws)

    def body(*refs):
        t_ref, outs = refs[0], refs[1 + 3 * n:]
        for i in range(n):
            g = t_ref[:, offs[i]:offs[i] + ws[i].shape[1]]
            dl, nm, nv = _adamw_math(refs[1 + i][...], g, refs[1 + n + i][...], refs[1 + 2 * n + i][...])
            outs[4 * i][...], outs[4 * i + 1][...], outs[4 * i + 2][...], outs[4 * i + 3][...] = g, dl, nm, nv

    res = pl.pallas_call(
        body, name="adamw_small", in_specs=[VMEM_SPEC] * (1 + 3 * n), out_specs=[VMEM_SPEC] * (4 * n),
        out_shape=[jax.ShapeDtypeStruct(w.shape, F32) for w in ws for _ in range(4)],
    )(total, *ws, *ms, *vs)
    return [res[4 * i:4 * i + 4] for i in range(n)]


def kernel(x, c, w_ada, b_ada, w_in, conv_w, conv_b, dt_bias, a_log, d_skip, ssm_norm_w, f_bias, attn_norm_w, w_out, ln1_g, ln1_b, w_ff_in, w_ff_out, ln2_g, ln2_b, loss_target, m_w_ada, m_b_ada, m_w_in, m_conv_w, m_conv_b, m_dt_bias, m_a_log, m_d_skip, m_ssm_norm_w, m_f_bias, m_attn_norm_w, m_w_out, m_ln1_g, m_ln1_b, m_w_ff_in, m_w_ff_out, m_ln2_g, m_ln2_b, v_w_ada, v_b_ada, v_w_in, v_conv_w, v_conv_b, v_dt_bias, v_a_log, v_d_skip, v_ssm_norm_w, v_f_bias, v_attn_norm_w, v_w_out, v_ln1_g, v_ln1_b, v_w_ff_in, v_w_ff_out, v_ln2_g, v_ln2_b):
    a = dict(b_ada=b_ada, conv_b=conv_b, dt_bias=dt_bias, a_log=a_log, d_skip=d_skip, ssm_norm_w=ssm_norm_w, f_bias=f_bias,
             attn_norm_w=attn_norm_w, ln1_g=ln1_g, ln1_b=ln1_b, ln2_g=ln2_g, ln2_b=ln2_b)
    ms = dict(b_ada=m_b_ada, conv_b=m_conv_b, dt_bias=m_dt_bias, a_log=m_a_log, d_skip=m_d_skip, ssm_norm_w=m_ssm_norm_w,
              f_bias=m_f_bias, attn_norm_w=m_attn_norm_w, ln1_g=m_ln1_g, ln1_b=m_ln1_b, ln2_g=m_ln2_g, ln2_b=m_ln2_b)
    vs = dict(b_ada=v_b_ada, conv_b=v_conv_b, dt_bias=v_dt_bias, a_log=v_a_log, d_skip=v_d_skip, ssm_norm_w=v_ssm_norm_w,
              f_bias=v_f_bias, attn_norm_w=v_attn_norm_w, ln1_g=v_ln1_g, ln1_b=v_ln1_b, ln2_g=v_ln2_g, ln2_b=v_ln2_b)
    xi, yi, ci = _place()
    chip = 2 * xi + yi
    me = 4 * xi + 2 * yi + ci
    d = D_MODEL
    conv_shard = CONV_DIM // N_CHIPS

    first = _allgather_small("gather_c", jnp.concatenate([c, conv_w[0].reshape(1, CONV_W * conv_shard)], axis=1))[:, 0]
    c_all = first[:, :d]
    conv_w_full = first[::2, d:].reshape(N_CHIPS, CONV_W, conv_shard).transpose(1, 0, 2).reshape(CONV_W, CONV_DIM)
    b_shard = lax.dynamic_slice_in_dim(b_ada, chip * ADA_SHARD, ADA_SHARD, axis=1)
    parts = _allgather_small("gather_mod", _mod_part(c_all, w_ada[0], b_shard))
    mod = lax.dynamic_index_in_dim(parts[::2], me, axis=1, keepdims=False).reshape(1, 6 * d)

    w_in_t, m_w_in_t, v_w_in_t = [jnp.transpose(t, (0, 2, 1)) for t in (w_in, m_w_in, v_w_in)]
    w_alt = _to_aligned(_gather_shards(w_in_t[0].astype(BF16)).reshape(IN_COLS, d))

    sp = {n: a[n] for n in SMALL[1:]}
    sp["conv_w"] = conv_w_full
    shards = [w_out[0].astype(BF16), w_ff_in[0].astype(BF16), w_ff_out[0].astype(BF16)]
    dx, landed, small = _local_step(x[0], loss_target[0], mod, w_alt, shards, sp)

    names = ["mod"] + SMALL[1:]
    vec = _pack([small[n] for n in names] + [small["conv_w"].reshape(1, CONV_W * CONV_DIM), small["loss"]])
    every, total = _allgather_small("gather_small", vec, with_sum=True)
    widths = [6 * d] + [a[n].shape[1] for n in SMALL[1:]]
    offs = [0]
    for w in widths:
        offs.append(offs[-1] + w + (-w % LANES))
    g_conv_w_full = total[:, offs[-1]:offs[-1] + CONV_W * CONV_DIM].reshape(CONV_W, CONV_DIM)
    loss = total[0, offs[-1] + CONV_W * CONV_DIM]
    dmod_shard = lax.dynamic_slice_in_dim(every[:, 0, :6 * d], chip * ADA_SHARD, ADA_SHARD, axis=1)
    g_w_ada = _w_ada_grad(c_all.T, dmod_shard)
    g_conv_w = lax.dynamic_slice_in_dim(g_conv_w_full, chip * conv_shard, conv_shard, axis=1)

    mine = [_sum_blocks("dev_sum_%d" % i, p) for i, p in enumerate(landed)]
    other = _pair_swap(mine)

    grads, deltas, new_m, new_v = {}, {}, {}, {}
    paired = dict(w_in=(w_in_t, m_w_in_t, v_w_in_t), w_out=(w_out, m_w_out, v_w_out), w_ff_in=(w_ff_in, m_w_ff_in, v_w_ff_in),
                  w_ff_out=(w_ff_out, m_w_ff_out, v_w_ff_out))
    for i, (n, (w, m, v)) in enumerate(paired.items()):
        res = _adamw_pair("adamw_" + n, w, mine[i], other[i], m, v, ci, by_cols=n == "w_in")
        grads[n], deltas[n], new_m[n], new_v[n] = [jnp.transpose(t, (0, 2, 1)) for t in res] if n == "w_in" else res
    for n, g, (w, m, v) in (("w_ada", g_w_ada, (w_ada, m_w_ada, v_w_ada)), ("conv_w", g_conv_w, (conv_w, m_conv_w, v_conv_w))):
        grads[n], deltas[n], new_m[n], new_v[n] = _adamw("adamw_" + n, w, g, m, v)
    for n, res in zip(SMALL, _adamw_small(total, offs, [a[n] for n in SMALL], [ms[n] for n in SMALL], [vs[n] for n in SMALL])):
        grads[n], deltas[n], new_m[n], new_v[n] = res

    order = ["w_ada", "b_ada", "w_in", "conv_w", "conv_b", "dt_bias", "a_log", "d_skip", "ssm_norm_w", "f_bias", "attn_norm_w", "w_out",
             "ln1_g", "ln1_b", "w_ff_in", "w_ff_out", "ln2_g", "ln2_b"]
    return (loss, dx[None], *[grads[n] for n in order], *[deltas[n] for n in order], *[new_m[n] for n in order], *[new_v[n] for n in order])
```

```python
import functools

import jax
import jax.numpy as jnp
from jax import lax
from jax.experimental import pallas as pl
from jax.experimental.pallas import tpu as pltpu

F32, BF16 = jnp.float32, jnp.bfloat16

D_MODEL = 1024
N_HEADS = 16
HEAD_DIM = 64
N_PAIRS = N_HEADS // 2
SSM_GROUPS = 2
SSM_STATE = 128
CHUNK = 128
CONV_W = 4
CONV_DIM = 1536
D_FF = 4096
IN_COLS = 5664
ALPHA = 2.0 ** 0.25
LN_EPS = 1e-5
RMS_EPS = 1e-5
LANES = 128
SUBLANES = 8

AL_Z, AL_XS, AL_Q, AL_K, AL_V, AL_B, AL_C, AL_DTF = 0, 1024, 2048, 3072, 4096, 5120, 5376, 5632
AL_COLS = 5760
F_LANE = 16

ADAM_LR, ADAM_B1, ADAM_B2, ADAM_EPS, ADAM_WD, ADAM_STEP = 0.001, 0.9, 0.999, 1e-08, 0.01, 10

VMEM_LIMIT = 56 * 1024 * 1024
SEQ_TK = 4096
MESH = pl.DeviceIdType.MESH


def _params(sem=None):
    return pltpu.CompilerParams(dimension_semantics=sem, vmem_limit_bytes=VMEM_LIMIT)


def _sigmoid(x):
    return 1.0 / (1.0 + jnp.exp(-x))


def _silu(x):
    return x * _sigmoid(x)


def _softplus(x):
    return jnp.maximum(x, 0.0) + jnp.log(1.0 + jnp.exp(-jnp.abs(x)))


def _split3(a):
    hi = a.astype(BF16)
    r = a - hi.astype(F32)
    mid = r.astype(BF16)
    lo = (r - mid.astype(F32)).astype(BF16)
    return hi, mid, lo


def _dot(a, b, dims=((1,), (0,))):
    return lax.dot_general(a, b, (dims, ((), ())), preferred_element_type=F32)


NN, NT, TN = ((1,), (0,)), ((1,), (1,)), ((0,), (0,))


def _dot3(t, a):
    hi, mid, lo = _split3(a)
    return _dot(t, hi) + _dot(t, mid) + _dot(t, lo)


def _matmul(name, a, b, *, dims=NN, out_dtype=F32, tm=1024, tn=1024, tk=1024, by_chip=None, epi=None, carry=()):
    if dims == NN:
        (m, k), n = a.shape, b.shape[1]
    elif dims == NT:
        (m, k), n = a.shape, b.shape[0]
    else:
        (k, m), n = a.shape, b.shape[1]
    if by_chip == "rows":
        tm = min(tm, m // 4)
    if by_chip == "cols":
        tn = min(tn, n // 4)
    tm, tn, tk = min(tm, m), min(tn, n), min(tk, k)
    assert m % tm == 0 and n % tn == 0 and k % tk == 0, (name, m, n, k, tm, tn, tk)
    nk = k // tk
    if by_chip == "rows":
        per = m // 4 // tm
        out_spec = pl.BlockSpec((None, tm, tn), lambda i, j, l: (i // per, i % per, j))
        out_shape = jax.ShapeDtypeStruct((4, m // 4, n), out_dtype)
    elif by_chip == "cols":
        per = n // 4 // tn
        out_spec = pl.BlockSpec((None, tm, tn), lambda i, j, l: (j // per, i, j % per))
        out_shape = jax.ShapeDtypeStruct((4, m, n // 4), out_dtype)
    else:
        out_spec = pl.BlockSpec((tm, tn), lambda i, j, l: (i, j))
        out_shape = jax.ShapeDtypeStruct((m, n), out_dtype)
    a_spec = pl.BlockSpec((tk, tm), lambda i, j, l: (l, i)) if dims == TN else pl.BlockSpec((tm, tk), lambda i, j, l: (i, l))
    b_spec = pl.BlockSpec((tn, tk), lambda i, j, l: (j, l)) if dims == NT else pl.BlockSpec((tk, tn), lambda i, j, l: (l, j))

    tile = pl.BlockSpec((tm, tn), lambda i, j, l: (i, j))
    in_specs, args, out_specs, out_shape = [a_spec, b_spec], [a, b], [out_spec], [out_shape]
    fn, n_tiles, n_sums = None, 1, 0
    if epi is not None:
        fn, fulls, vecs, outs, sums = epi
        assert by_chip is None and (not sums or n == tn), name
        in_specs = in_specs + [tile] * len(fulls) + [pl.BlockSpec((1, tn), lambda i, j, l: (0, j))] * len(vecs)
        args = args + list(fulls) + list(vecs)
        flipped = pl.BlockSpec((tn, tm), lambda i, j, l: (j, i))
        out_specs = [flipped if isinstance(dt, tuple) else tile for dt in outs] + [pl.BlockSpec((1, w), lambda i, j, l: (0, 0)) for w in sums]
        out_shape = [jax.ShapeDtypeStruct((n, m), dt[1]) if isinstance(dt, tuple) else jax.ShapeDtypeStruct((m, n), dt) for dt in outs]
        out_shape += [jax.ShapeDtypeStruct((1, w), F32) for w in sums]
        n_tiles, n_sums = len(outs), len(sums)
    n_in, n_out, n_c = len(args), len(out_specs), len(carry)
    scratch = [pltpu.VMEM((tm, tn) if nk > 1 else (SUBLANES, LANES), F32)]
    if n_c:
        in_specs, args = in_specs + [ANY] * n_c, args + list(carry)
        out_specs = out_specs + [ANY] * n_c
        out_shape = out_shape + [jax.ShapeDtypeStruct(g.shape, g.dtype) for g in carry]
        scratch = scratch + _exchange_sems(n_c)
    gm, gn = m // tm, n // tn

    def body(*refs):
        a_ref, b_ref = refs[:2]
        ins, outs = refs[2:n_in], refs[n_in + n_c:n_in + n_c + n_out]
        acc_ref = refs[n_in + 2 * n_c + n_out]
        i, j, l = pl.program_id(0), pl.program_id(1), pl.program_id(2)
        if n_c:
            start, wait = _scatter_plan(refs[n_in:n_in + n_c], refs[n_in + n_c + n_out:n_in + 2 * n_c + n_out], *refs[n_in + 2 * n_c + n_out + 1:])
            pl.when((i == 0) & (j == 0) & (l == 0))(start)
        part = _dot(a_ref[...].astype(BF16), b_ref[...].astype(BF16), dims)

        def finish(res):
            if fn is None:
                outs[0][...] = res.astype(outs[0].dtype)
                return
            tiles, colsums = fn(res, *[r[...] for r in ins])
            for r, val in zip(outs[:n_tiles], tiles):
                r[...] = val.astype(r.dtype)
            if n_sums:
                @pl.when(i == 0)
                def _():
                    for r in outs[n_tiles:]:
                        r[...] = jnp.zeros_like(r)
                for r, val in zip(outs[n_tiles:], colsums):
                    r[...] += val

        if nk == 1:
            finish(part)
        else:
            @pl.when(l == 0)
            def _():
                acc_ref[...] = part

            @pl.when((l > 0) & (l < nk - 1))
            def _():
                acc_ref[...] += part

            @pl.when(l == nk - 1)
            def _():
                finish(acc_ref[...] + part)

        if n_c:
            pl.when((i == gm - 1) & (j == gn - 1) & (l == nk - 1))(wait)

    res = pl.pallas_call(
        body, name=name, grid=(gm, gn, nk),
        in_specs=in_specs, out_specs=out_specs, out_shape=out_shape, scratch_shapes=scratch,
        compiler_params=_params(("arbitrary",) * 3 if n_c or n_sums else ("parallel", "parallel", "arbitrary")),
    )(*args)
    return res[0] if len(res) == 1 else res


def _rowwise(name, fn, fulls, vecs, out_fulls, out_vecs, tr=256):
    fulls = [f if isinstance(f, tuple) else (f, f.shape[1], 0) for f in fulls]
    s = fulls[0][0].shape[0]
    tr = min(tr, s)
    out_fulls = [o if len(o) == 3 else (*o, (o[0], 0, None)) for o in out_fulls]
    into = [(k, slab[2]) for k, (_, _, slab) in enumerate(out_fulls) if slab[2] is not None]
    nf, nv, nof, nov = len(fulls), len(vecs), len(out_fulls), len(out_vecs)
    in_specs = [pl.BlockSpec((tr, w), functools.partial(lambda i, cb: (i, cb), cb=cb)) for (_, w, cb) in fulls]
    in_specs += [pl.BlockSpec(v.shape, lambda i: (0, 0)) for v in vecs] + [ANY] * len(into)
    out_shape = [jax.ShapeDtypeStruct((s, slab[0]), dt) for (_, dt, slab) in out_fulls] + [jax.ShapeDtypeStruct((1, w), F32) for w in out_vecs]
    out_specs = [pl.BlockSpec((tr, w), functools.partial(lambda i, cb: (i, cb), cb=slab[1])) for (w, _, slab) in out_fulls]
    out_specs += [pl.BlockSpec((1, w), lambda i: (0, 0)) for w in out_vecs]

    def body(*refs):
        outs = refs[nf + nv + len(into):]
        of, ov = fn(*[r[...] for r in refs[:nf + nv]])
        for r, val in zip(outs[:nof], of):
            r[...] = val.astype(r.dtype)
        if nov:
            @pl.when(pl.program_id(0) == 0)
            def _():
                for r in outs[nof:]:
                    r[...] = jnp.zeros_like(r)
            for r, val in zip(outs[nof:], ov):
                r[...] += val

    res = pl.pallas_call(
        body, name=name, grid=(s // tr,), in_specs=in_specs, out_specs=out_specs, out_shape=out_shape,
        input_output_aliases={nf + nv + pos: k for pos, (k, _) in enumerate(into)},
        compiler_params=_params(("arbitrary",)),
    )(*[f[0] for f in fulls], *vecs, *[buf for _, buf in into])
    return res[:nof], res[nof:]


def _colsum(x):
    return jnp.sum(x, axis=0, keepdims=True)


def _rowmean(x):
    return jnp.mean(x, axis=-1, keepdims=True)


CONV_CB = 512
CONV_TR = 512


def _shift_down(u, halo, j):
    if j == 0:
        return u
    ru = pltpu.roll(u, j, 0)
    row8 = lax.broadcasted_iota(jnp.int32, halo.shape, 0)
    top = jnp.where(row8 < j, pltpu.roll(halo, j, 0), ru[:SUBLANES])
    return jnp.concatenate([top, ru[SUBLANES:]], axis=0)


def _shift_up(d, halo, j):
    if j == 0:
        return d
    tr = d.shape[0]
    rd = pltpu.roll(d, tr - j, 0)
    row8 = lax.broadcasted_iota(jnp.int32, halo.shape, 0)
    bot = jnp.where(row8 >= SUBLANES - j, pltpu.roll(halo, SUBLANES - j, 0), rd[tr - SUBLANES:])
    return jnp.concatenate([rd[:tr - SUBLANES], bot], axis=0)


def _conv_col(cb):
    return jnp.where(cb < 2, AL_XS // CONV_CB + cb, AL_B // CONV_CB)


def _conv_specs(s, tr):
    per8 = tr // SUBLANES
    blk = pl.BlockSpec((tr, CONV_CB), lambda cb, i: (i, _conv_col(cb)))
    prev = pl.BlockSpec((SUBLANES, CONV_CB), lambda cb, i: (jnp.maximum(i * per8 - 1, 0), _conv_col(cb)))
    return blk, prev


def _conv_pre(u, halo, w_ref, b_ref, first):
    halo = jnp.where(first, 0.0, halo)
    acc = b_ref[...] + w_ref[CONV_W - 1:CONV_W, :] * u
    shifted = [u]
    for j in range(1, CONV_W):
        sh = _shift_down(u, halo, j)
        shifted.append(sh)
        acc = acc + w_ref[CONV_W - 1 - j:CONV_W - j, :] * sh
    return acc, shifted


def _conv_bwd_pre(proj, conv_w, conv_b, dxc):
    s = proj.shape[0]
    tr = min(CONV_TR, s)
    blk, prev = _conv_specs(s, tr)

    def body(u_ref, h_ref, w_ref, b_ref, d_ref, dpre_ref, dw_ref, db_ref):
        i = pl.program_id(1)
        pre, shifted = _conv_pre(u_ref[...], h_ref[...], w_ref, b_ref, i == 0)
        sg = _sigmoid(pre)
        dpre = d_ref[...] * (sg * (1.0 + pre * (1.0 - sg)))
        dpre_ref[...] = dpre

        @pl.when(i == 0)
        def _():
            dw_ref[...] = jnp.zeros_like(dw_ref)
            db_ref[...] = jnp.zeros_like(db_ref)

        db_ref[...] += _colsum(dpre)
        for j in range(CONV_W):
            dw_ref[CONV_W - 1 - j:CONV_W - j, :] += _colsum(dpre * shifted[j])

    own = pl.BlockSpec((tr, CONV_CB), lambda cb, i: (i, cb))
    wspec = pl.BlockSpec((CONV_W, CONV_CB), lambda cb, i: (0, cb))
    bspec = pl.BlockSpec((1, CONV_CB), lambda cb, i: (0, cb))
    return pl.pallas_call(
        body, name="conv_bwd_pre", grid=(CONV_DIM // CONV_CB, s // tr),
        in_specs=[blk, prev, wspec, bspec, own], out_specs=[own, wspec, bspec],
        out_shape=[jax.ShapeDtypeStruct((s, CONV_DIM), F32), jax.ShapeDtypeStruct((CONV_W, CONV_DIM), F32),
                   jax.ShapeDtypeStruct((1, CONV_DIM), F32)],
        compiler_params=_params(("parallel", "arbitrary")),
    )(proj, proj, conv_w, conv_b, dxc)


def _conv_bwd_in(dpre, conv_w, dproj):
    s = dpre.shape[0]
    tr = min(CONV_TR, s)
    per8 = tr // SUBLANES
    last8 = s // SUBLANES - 1
    nb = s // tr

    def body(d_ref, n_ref, w_ref, _, o_ref):
        d = d_ref[...]
        halo = jnp.where(pl.program_id(1) == nb - 1, 0.0, n_ref[...])
        acc = w_ref[CONV_W - 1:CONV_W, :] * d
        for j in range(1, CONV_W):
            acc = acc + w_ref[CONV_W - 1 - j:CONV_W - j, :] * _shift_up(d, halo, j)
        o_ref[...] = acc.astype(o_ref.dtype)

    own = pl.BlockSpec((tr, CONV_CB), lambda cb, i: (i, cb))
    nxt = pl.BlockSpec((SUBLANES, CONV_CB), lambda cb, i: (jnp.minimum((i + 1) * per8, last8), cb))
    return pl.pallas_call(
        body, name="conv_bwd_in", grid=(CONV_DIM // CONV_CB, nb),
        in_specs=[own, nxt, pl.BlockSpec((CONV_W, CONV_CB), lambda cb, i: (0, cb)), ANY],
        out_specs=pl.BlockSpec((tr, CONV_CB), lambda cb, i: (i, _conv_col(cb))),
        out_shape=jax.ShapeDtypeStruct(dproj.shape, dproj.dtype), input_output_aliases={3: 0},
        compiler_params=_params(("parallel", "parallel")),
    )(dpre, dpre, conv_w, dproj)


XC_B, XC_C = 1024, 1280


def _tile_iotas():
    row = lax.broadcasted_iota(jnp.int32, (CHUNK, LANES), 0)
    lane = lax.broadcasted_iota(jnp.int32, (CHUNK, LANES), 1)
    return row, lane


def _ssd_scalars(dtf_ref, bias_ref, alog_ref, row, lane):
    head = lane[:1] < N_HEADS
    raw = dtf_ref[...] + bias_ref[...]
    dt = _softplus(raw)
    a_neg = jnp.where(head, -jnp.exp(alog_ref[...]), 0.0)
    a = dt * a_neg
    tril = (row >= lane).astype(BF16)
    s = _dot3(tril, a)
    return raw, dt, a_neg, s


def _pair(v, j, lo):
    return jnp.where(lo, v[:, 2 * j:2 * j + 1], v[:, 2 * j + 1:2 * j + 2])


def _head_sum(x, lo, hh):
    return jnp.sum(jnp.where(lo == (hh == 0), x, 0.0), axis=1, keepdims=True)


def _decay_masks(s, st, h, row, lane):
    s_col = jnp.broadcast_to(s[:, h:h + 1], (CHUNK, LANES))
    s_row = jnp.broadcast_to(st[h:h + 1, :], (CHUNK, LANES))
    lm = jnp.where(row >= lane, jnp.exp(s_col - s_row), 0.0)
    lmt = jnp.where(row <= lane, jnp.exp(s_row - s_col), 0.0)
    return lm, lmt


def _gated_norm(y, z, w):
    g = y * _silu(z)
    return g * lax.rsqrt(_rowmean(g * g) + RMS_EPS) * w


def _ssd_fwd(proj, conv_w, conv_b, dt_bias_l, a_log_l, d_exp, norm_w):
    s_len = proj.shape[0]
    nc = s_len // CHUNK

    def body(xs_ref, bc_ref, cw_ref, cb_ref, dtf_ref, bias_ref, alog_ref, dexp_ref, z_ref, w_ref,
             x_ref, y_ref, prevs_ref, ymix_ref, state_ref, halo_ref):
        first = pl.program_id(0) == 0

        @pl.when(first)
        def _():
            state_ref[...] = jnp.zeros_like(state_ref)
            halo_ref[...] = jnp.zeros_like(halo_ref)

        u = jnp.concatenate([xs_ref[...], bc_ref[...]], axis=1)
        pre, _ = _conv_pre(u, halo_ref[...], cw_ref, cb_ref, first)
        halo_ref[...] = u[CHUNK - SUBLANES:]
        x_ref[...] = _silu(pre)

        row, lane = _tile_iotas()
        lo = lane < HEAD_DIM
        _, dt, _, s = _ssd_scalars(dtf_ref, bias_ref, alog_ref, row, lane)
        tot = s[CHUNK - 1:CHUNK, :]
        st = s.T
        for g in range(SSM_GROUPS):
            bg = x_ref[:, XC_B + g * SSM_STATE:XC_B + (g + 1) * SSM_STATE].astype(BF16)
            cg = x_ref[:, XC_C + g * SSM_STATE:XC_C + (g + 1) * SSM_STATE].astype(BF16)
            cb = _dot(cg, bg, NT)
            for j in range(g * 4, g * 4 + 4):
                xs_p = x_ref[:, j * LANES:(j + 1) * LANES]
                dt_p, s_p, tot_p = _pair(dt, j, lo), _pair(s, j, lo), _pair(tot, j, lo[:1])
                xc_p = xs_p * dt_p
                xc_b = xc_p.astype(BF16)
                yd = []
                for hh in range(2):
                    lm, _ = _decay_masks(s, st, 2 * j + hh, row, lane)
                    yd.append(_dot((cb * lm).astype(BF16), xc_b))
                prev = state_ref[j]
                prevs_ref[0, j] = prev
                yo = _dot(cg, prev.astype(BF16)) * jnp.exp(s_p)
                y_ref[:, j * LANES:(j + 1) * LANES] = jnp.where(lo, yd[0], yd[1]) + yo + dexp_ref[:, j * LANES:(j + 1) * LANES] * xs_p
                to_end = jnp.exp(tot_p - s_p)
                state_ref[j] = jnp.exp(tot_p) * prev + _dot(bg, (xc_p * to_end).astype(BF16), TN)
        ymix_ref[...] = _gated_norm(y_ref[...], z_ref[...], w_ref[...]).astype(ymix_ref.dtype)

    vec = lambda w: pl.BlockSpec((1, w), lambda c: (0, 0))
    rows = pl.BlockSpec((CHUNK, D_MODEL), lambda c: (c, 0))
    return pl.pallas_call(
        body, name="ssd_fwd", grid=(nc,),
        in_specs=[pl.BlockSpec((CHUNK, D_MODEL), lambda c: (c, AL_XS // D_MODEL)),
                  pl.BlockSpec((CHUNK, CONV_DIM - D_MODEL), lambda c: (c, AL_B // (CONV_DIM - D_MODEL))),
                  pl.BlockSpec((CONV_W, CONV_DIM), lambda c: (0, 0)), vec(CONV_DIM),
                  pl.BlockSpec((CHUNK, LANES), lambda c: (c, AL_DTF // LANES)),
                  vec(LANES), vec(LANES), vec(D_MODEL), pl.BlockSpec((CHUNK, D_MODEL), lambda c: (c, AL_Z // D_MODEL)), vec(D_MODEL)],
        out_specs=[pl.BlockSpec((CHUNK, CONV_DIM), lambda c: (c, 0)), rows,
                   pl.BlockSpec((1, N_PAIRS, SSM_STATE, LANES), lambda c: (c, 0, 0, 0)), rows],
        out_shape=[jax.ShapeDtypeStruct((s_len, CONV_DIM), F32), jax.ShapeDtypeStruct((s_len, D_MODEL), F32),
                   jax.ShapeDtypeStruct((nc, N_PAIRS, SSM_STATE, LANES), F32), jax.ShapeDtypeStruct((s_len, 2 * D_MODEL), BF16)],
        scratch_shapes=[pltpu.VMEM((N_PAIRS, SSM_STATE, LANES), F32), pltpu.VMEM((SUBLANES, CONV_DIM), F32)],
        compiler_params=_params(("arbitrary",)),
    )(proj, proj, conv_w, conv_b, proj, dt_bias_l, a_log_l, d_exp, proj, norm_w)


def _ssd_bwd(xc_all, proj, dt_bias_l, a_log_l, d_exp, prevs, y_ssd, dymix, norm_w):
    s_len = xc_all.shape[0]
    nc = s_len // CHUNK

    def body(x_ref, dtf_ref, bias_ref, alog_ref, dexp_ref, prevs_ref, y_ref, z_ref, dym_ref, w_ref,
             dx_ref, ddt_ref, da_ref, dd_ref, dbias_ref, dz_ref, dw_ref, dstate_ref):
        @pl.when(pl.program_id(0) == 0)
        def _():
            dstate_ref[...] = jnp.zeros_like(dstate_ref)
            da_ref[...] = jnp.zeros_like(da_ref)
            dd_ref[...] = jnp.zeros_like(dd_ref)
            dbias_ref[...] = jnp.zeros_like(dbias_ref)
            dw_ref[...] = jnp.zeros_like(dw_ref)

        y, z, dyo = y_ref[...], z_ref[...], dym_ref[...]
        sg = _sigmoid(z)
        sz = z * sg
        gated = y * sz
        rn = lax.rsqrt(_rowmean(gated * gated) + RMS_EPS)
        dg = _rms_bwd(dyo * w_ref[...], gated, rn)
        dy_full = dg * sz
        dz_ref[...] = (dg * y * (sg * (1.0 + z * (1.0 - sg)))).astype(dz_ref.dtype)
        dw_ref[...] += _colsum(dyo * gated * rn)

        row, lane = _tile_iotas()
        lo = lane < HEAD_DIM
        last = row == CHUNK - 1
        raw, dt, a_neg, s = _ssd_scalars(dtf_ref, bias_ref, alog_ref, row, lane)
        tot = s[CHUNK - 1:CHUNK, :]
        st = s.T
        ds_acc = jnp.zeros((CHUNK, LANES), F32)
        ddt_acc = jnp.zeros((CHUNK, LANES), F32)
        for g in range(SSM_GROUPS):
            bcol = slice(XC_B + g * SSM_STATE, XC_B + (g + 1) * SSM_STATE)
            ccol = slice(XC_C + g * SSM_STATE, XC_C + (g + 1) * SSM_STATE)
            bg = x_ref[:, bcol].astype(BF16)
            cg = x_ref[:, ccol].astype(BF16)
            cb = _dot(cg, bg, NT)
            cbt = _dot(bg, cg, NT)
            dcb = jnp.zeros((CHUNK, LANES), F32)
            dcbt = jnp.zeros((CHUNK, LANES), F32)
            db_acc = jnp.zeros((CHUNK, LANES), F32)
            dc_acc = jnp.zeros((CHUNK, LANES), F32)
            for j in range(g * 4, g * 4 + 4):
                cols = slice(j * LANES, (j + 1) * LANES)
                xs_p, dy_p = x_ref[:, cols], dy_full[:, cols]
                dt_p, s_p, tot_p = _pair(dt, j, lo), _pair(s, j, lo), _pair(tot, j, lo[:1])
                xc_p = xs_p * dt_p
                xc_b, dy_b = xc_p.astype(BF16), dy_p.astype(BF16)
                e_p, f_p, etot_p = jnp.exp(s_p), jnp.exp(tot_p - s_p), jnp.exp(tot_p)
                prev, dnext = prevs_ref[0, j], dstate_ref[j]
                prev_b, dnext_b = prev.astype(BF16), dnext.astype(BF16)
                dd_ref[:, cols] += _colsum(dy_p * xs_p)
                dxs_p = dexp_ref[:, cols] * dy_p
                cp = _dot(cg, prev_b)
                gy = (dy_p * e_p).astype(BF16)
                dc_acc += _dot(gy, prev_b, NT)
                dstate_ref[j] = etot_p * dnext + _dot(cg, gy, TN)
                de = dy_p * cp * e_p
                bds = _dot(bg, dnext_b)
                db_acc += _dot((xc_p * f_p).astype(BF16), dnext_b, NT)
                dxc_p = bds * f_p
                df = bds * xc_p * f_p
                dtot_p = _colsum(dnext * prev) * etot_p + _colsum(df)
                dsl = de - df + jnp.where(last, dtot_p, 0.0)
                for hh in range(2):
                    h = 2 * j + hh
                    mine = lo == (hh == 0)
                    lm, lmt = _decay_masks(s, st, h, row, lane)
                    dy_h = jnp.where(mine, dy_p, 0.0).astype(BF16)
                    xc_h = jnp.where(mine, xc_p, 0.0).astype(BF16)
                    dm = _dot(dy_h, xc_b, NT)
                    dmt = _dot(xc_h, dy_b, NT)
                    mt = cbt * lmt
                    dxc_p += _dot(mt.astype(BF16), dy_h)
                    dml, dmtl = dm * lm, dmt * lmt
                    ds_h = jnp.sum(dml * cb - dmtl * cbt + jnp.where(mine, dsl, 0.0), axis=1, keepdims=True)
                    ds_acc += jnp.where(lane == h, ds_h, 0.0)
                    dcb += dml
                    dcbt += dmtl
                    ddt_acc += jnp.where(lane == h, _head_sum(dxc_p * xs_p, lo, hh), 0.0)
                dx_ref[:, cols] = dxs_p + dxc_p * dt_p
            dx_ref[:, ccol] = dc_acc + _dot(dcb.astype(BF16), bg)
            dx_ref[:, bcol] = db_acc + _dot(dcbt.astype(BF16), cg)
        triu = (row <= lane).astype(BF16)
        da = _dot3(triu, ds_acc)
        ddt = ddt_acc + da * a_neg
        da_ref[...] += _colsum(da * dt) * a_neg[:1]
        ddt_raw = jnp.where(lane < N_HEADS, ddt * _sigmoid(raw), 0.0)
        dbias_ref[...] += _colsum(ddt_raw)
        ddt_ref[...] = ddt_raw

    rev = lambda c: nc - 1 - c
    vec = lambda w: pl.BlockSpec((1, w), lambda c: (0, 0))
    rows = lambda cb: pl.BlockSpec((CHUNK, D_MODEL), lambda c: (rev(c), cb))
    return pl.pallas_call(
        body, name="ssd_bwd", grid=(nc,),
        in_specs=[pl.BlockSpec((CHUNK, CONV_DIM), lambda c: (rev(c), 0)), pl.BlockSpec((CHUNK, LANES), lambda c: (rev(c), AL_DTF // LANES)),
                  vec(LANES), vec(LANES), vec(D_MODEL),
                  pl.BlockSpec((1, N_PAIRS, SSM_STATE, LANES), lambda c: (rev(c), 0, 0, 0)),
                  rows(0), rows(AL_Z // D_MODEL), rows(0), vec(D_MODEL)],
        out_specs=[pl.BlockSpec((CHUNK, CONV_DIM), lambda c: (rev(c), 0)), pl.BlockSpec((CHUNK, LANES), lambda c: (rev(c), 0)),
                   vec(LANES), vec(D_MODEL), vec(LANES), rows(AL_Z // D_MODEL), vec(D_MODEL)],
        out_shape=[jax.ShapeDtypeStruct((s_len, CONV_DIM), F32), jax.ShapeDtypeStruct((s_len, LANES), F32),
                   jax.ShapeDtypeStruct((1, LANES), F32), jax.ShapeDtypeStruct((1, D_MODEL), F32), jax.ShapeDtypeStruct((1, LANES), F32),
                   jax.ShapeDtypeStruct((s_len, AL_COLS), BF16), jax.ShapeDtypeStruct((1, D_MODEL), F32)],
        scratch_shapes=[pltpu.VMEM((N_PAIRS, SSM_STATE, LANES), F32)],
        compiler_params=_params(("arbitrary",)),
    )(xc_all, proj, dt_bias_l, a_log_l, d_exp, prevs, y_ssd, proj, dymix, norm_w)


AUG_LANES = 6


def _aug_base(hh):
    return HEAD_DIM if hh == 0 else 0


NEG = -1e30
ATT_T = 512


def _fox_cum(proj, f_bias_l):
    s_len = proj.shape[0]
    nc = s_len // CHUNK

    def body(dtf_ref, fb_ref, cum_ref):
        row, lane = _tile_iotas()
        tril = (row >= lane).astype(BF16)
        spread = [(jnp.where(lane == AUG_LANES * row + i, 1.0, 0.0) - jnp.where(lane == AUG_LANES * row + 3 + i, 1.0, 0.0)).astype(BF16)
                  for i in range(3)]

        def step(c, carry):
            rows = pl.ds(pl.multiple_of(c * CHUNK, CHUNK), CHUNK)
            lf = -_softplus(-(dtf_ref[rows, :] + fb_ref[...]))
            lf = jnp.where(lane < N_HEADS, pltpu.roll(lf, LANES - F_LANE, 1), 0.0)
            cs = _dot3(tril, lf) + carry
            parts = _split3(cs)
            cum_ref[rows, :] = _dot(parts[0], spread[0]) + _dot(parts[1], spread[1]) + _dot(parts[2], spread[2])
            return cs[CHUNK - 1:CHUNK, :]

        lax.fori_loop(0, nc, step, jnp.zeros((1, LANES), F32))

    return pl.pallas_call(
        body, name="fox_cum", grid=(1,),
        in_specs=[pl.BlockSpec((s_len, LANES), lambda i: (0, AL_DTF // LANES)), pl.BlockSpec((1, LANES), lambda i: (0, 0))],
        out_specs=pl.BlockSpec((s_len, LANES), lambda i: (0, 0)),
        out_shape=jax.ShapeDtypeStruct((s_len, LANES), F32),
        compiler_params=_params(("arbitrary",)),
    )(proj, f_bias_l)


def _fox_cum_bwd(dcum, proj, f_bias_l, ddt_tile, dproj):
    s_len = proj.shape[0]
    nc = s_len // CHUNK

    def body(dcum_ref, dtf_ref, fb_ref, ddt_ref, _, out_ref, dfb_ref):
        row, lane = _tile_iotas()
        triu = (row <= lane).astype(BF16)
        is_f = (lane >= F_LANE) & (lane < F_LANE + N_HEADS)

        def step(t, carry):
            run, dfb = carry
            rows = pl.ds(pl.multiple_of((nc - 1 - t) * CHUNK, CHUNK), CHUNK)
            rc = _dot3(triu, dcum_ref[rows, :]) + run
            sg = _sigmoid(-(dtf_ref[rows, :] + fb_ref[...]))
            df = jnp.where(is_f, pltpu.roll(rc, F_LANE, 1) * sg, 0.0)
            out_ref[rows, :] = (df + ddt_ref[rows, :]).astype(out_ref.dtype)
            return rc[0:1, :], dfb + _colsum(df)

        _, dfb = lax.fori_loop(0, nc, step, (jnp.zeros((1, LANES), F32), jnp.zeros((1, LANES), F32)))
        dfb_ref[...] = dfb

    whole = pl.BlockSpec((s_len, LANES), lambda i: (0, 0))
    dtf_cols = pl.BlockSpec((s_len, LANES), lambda i: (0, AL_DTF // LANES))
    vec = pl.BlockSpec((1, LANES), lambda i: (0, 0))
    return pl.pallas_call(
        body, name="fox_cum_bwd", grid=(1,),
        in_specs=[whole, dtf_cols, vec, whole, ANY], out_specs=[dtf_cols, vec],
        out_shape=[jax.ShapeDtypeStruct(dproj.shape, dproj.dtype), jax.ShapeDtypeStruct((1, LANES), F32)],
        input_output_aliases={4: 0}, compiler_params=_params(("arbitrary",)),
    )(dcum, proj, f_bias_l, ddt_tile, dproj)


def _attn_prep(proj, cum):
    s_len = proj.shape[0]
    tr = min(256, s_len)

    def body(q_ref, k_ref, v_ref, cum_ref, qa_ref, ka_ref, vb_ref):
        lane = lax.broadcasted_iota(jnp.int32, (tr, LANES), 1)
        lo = lane < HEAD_DIM
        c = cum_ref[...]
        for p in range(N_PAIRS):
            cols = slice(p * LANES, (p + 1) * LANES)
            q, k = q_ref[:, cols] * (HEAD_DIM ** -0.5), k_ref[:, cols]
            for hh in range(2):
                base = _aug_base(hh)
                r = pltpu.roll(c, (base - AUG_LANES * (2 * p + hh)) % LANES, 1)
                first = (lane >= base) & (lane < base + 3)
                second = (lane >= base + 3) & (lane < base + AUG_LANES)
                mine = lo == (hh == 0)
                qa_ref[2 * p + hh] = jnp.where(mine, q, jnp.where(first, r, jnp.where(second, 1.0, 0.0))).astype(BF16)
                ka_ref[2 * p + hh] = jnp.where(mine, k, jnp.where(first, 1.0, jnp.where(second, r, 0.0))).astype(BF16)
        vb_ref[...] = v_ref[...].astype(BF16)

    assert AL_Q % D_MODEL == 0 and AL_K % D_MODEL == 0 and AL_V % D_MODEL == 0
    slab = lambda col0: pl.BlockSpec((tr, D_MODEL), lambda i: (i, col0 // D_MODEL))
    heads = pl.BlockSpec((N_HEADS, tr, LANES), lambda i: (0, i, 0))
    return pl.pallas_call(
        body, name="attn_prep", grid=(s_len // tr,),
        in_specs=[slab(AL_Q), slab(AL_K), slab(AL_V), pl.BlockSpec((tr, LANES), lambda i: (i, 0))],
        out_specs=[heads, heads, pl.BlockSpec((tr, D_MODEL), lambda i: (i, 0))],
        out_shape=[jax.ShapeDtypeStruct((N_HEADS, s_len, LANES), BF16), jax.ShapeDtypeStruct((N_HEADS, s_len, LANES), BF16),
                   jax.ShapeDtypeStruct((s_len, D_MODEL), BF16)],
        compiler_params=_params(("parallel",)),
    )(proj, proj, proj, cum)


def _attn_fwd(qa, ka, vb, shards):
    s_len = vb.shape[0]
    t = min(ATT_T, s_len)
    nq = s_len // t
    n = len(shards)

    def body(qa_ref, ka_ref, vb_ref, *rest):
        o_ref, lse_ref = rest[n:n + 2]
        start, finish = _gather_plan(rest[:n], rest[n + 2:2 * n + 2], *rest[2 * n + 2:])
        i = pl.program_id(1)
        pl.when((pl.program_id(0) == 0) & (i == 0))(start)
        row = lax.broadcasted_iota(jnp.int32, (t, t), 0)
        col = lax.broadcasted_iota(jnp.int32, (t, t), 1)
        lo = lax.broadcasted_iota(jnp.int32, (t, LANES), 1) < HEAD_DIM
        qs = (qa_ref[0], qa_ref[1])

        def block(j, carry, masked):
            rows = pl.ds(pl.multiple_of(j * t, t), t)
            v = vb_ref[rows, :]
            new = []
            for hh in range(2):
                m, l, acc = carry[hh]
                s = _dot(qs[hh], ka_ref[hh, rows, :], NT)
                if masked:
                    s = jnp.where(row >= col, s, NEG)
                m_new = jnp.maximum(m, jnp.max(s, axis=1, keepdims=True))
                alpha = jnp.exp(m - m_new)
                p = jnp.exp(s - m_new)
                new.append((m_new, alpha * l + jnp.sum(p, axis=1, keepdims=True), alpha * acc + _dot(p.astype(BF16), v)))
            return tuple(new)

        init = (jnp.full((t, 1), NEG, F32), jnp.zeros((t, 1), F32), jnp.zeros((t, LANES), F32))
        carry = lax.fori_loop(0, i, functools.partial(block, masked=False), (init, init))
        (m0, l0, acc0), (m1, l1, acc1) = block(i, carry, True)
        o_ref[...] = jnp.where(lo, acc0 / l0, acc1 / l1)
        lse_ref[...] = jnp.where(lo, m0 + jnp.log(l0), m1 + jnp.log(l1))
        pl.when((pl.program_id(0) == N_PAIRS - 1) & (i == nq - 1))(finish)

    out = pl.BlockSpec((t, LANES), lambda p, i: (i, p))
    res = pl.pallas_call(
        body, name="attn_fwd", grid=(N_PAIRS, nq),
        in_specs=[pl.BlockSpec((2, t, LANES), lambda p, i: (p, i, 0)), pl.BlockSpec((2, s_len, LANES), lambda p, i: (p, 0, 0)),
                  pl.BlockSpec((s_len, LANES), lambda p, i: (0, p))] + [ANY] * n,
        out_specs=[out, out] + [ANY] * n,
        out_shape=[jax.ShapeDtypeStruct((s_len, D_MODEL), F32), jax.ShapeDtypeStruct((s_len, D_MODEL), F32)]
        + [jax.ShapeDtypeStruct((N_CHIPS, *h.shape), h.dtype) for h in shards],
        scratch_shapes=_exchange_sems(n),
        compiler_params=_params(("arbitrary", "arbitrary")),
    )(qa, ka, vb, *shards)
    return res[0], res[1], res[2:]


def _attn_bwd(qa, ka, vb, o, lse, do, parts, dproj):
    s_len = vb.shape[0]
    t = min(ATT_T, s_len)
    nq = s_len // t
    n = len(parts)

    def body(qa_ref, ka_ref, vb_ref, o_ref, lse_ref, do_ref, *rest):
        dqa_ref, dka_ref, dv_ref = rest[n + 1:n + 4]
        start, finish = _reduce_plan(rest[:n], rest[n + 4:2 * n + 4], *rest[2 * n + 4:])
        j = pl.program_id(1)
        pl.when((pl.program_id(0) == 0) & (j == 0))(start)

        @pl.when(j == 0)
        def _():
            dqa_ref[...] = jnp.zeros_like(dqa_ref)

        row = lax.broadcasted_iota(jnp.int32, (t, t), 0)
        col = lax.broadcasted_iota(jnp.int32, (t, t), 1)
        lo = lax.broadcasted_iota(jnp.int32, (t, LANES), 1) < HEAD_DIM
        v = vb_ref[...]
        ks = (ka_ref[0], ka_ref[1])

        def block(i, carry, masked):
            dk, dv = list(carry[:2]), carry[2]
            rows = pl.ds(pl.multiple_of(i * t, t), t)
            do_p, o_p, lse_p = do_ref[rows, :], o_ref[rows, :], lse_ref[rows, :]
            for hh in range(2):
                q = qa_ref[hh, rows, :]
                do_h = jnp.where(lo == (hh == 0), do_p, 0.0)
                delta = jnp.sum(do_h * o_p, axis=1, keepdims=True)
                s = _dot(q, ks[hh], NT)
                if masked:
                    s = jnp.where(row >= col, s, NEG)
                p = jnp.exp(s - lse_p[:, hh * HEAD_DIM:hh * HEAD_DIM + 1])
                do_b = do_h.astype(BF16)
                ds = (p * (_dot(do_b, v, NT) - delta)).astype(BF16)
                dv = dv + _dot(p.astype(BF16), do_b, TN)
                dk[hh] = dk[hh] + _dot(ds, q, TN)
                dqa_ref[hh, rows, :] += _dot(ds, ks[hh])
            return dk[0], dk[1], dv

        zero = jnp.zeros((t, LANES), F32)
        carry = block(j, (zero, zero, zero), True)
        dk0, dk1, dv = lax.fori_loop(j + 1, nq, functools.partial(block, masked=False), carry)
        dka_ref[0] = dk0
        dka_ref[1] = dk1
        dv_ref[...] = dv.astype(dv_ref.dtype)
        pl.when((pl.program_id(0) == N_PAIRS - 1) & (j == nq - 1))(finish)

    whole_pair = pl.BlockSpec((2, s_len, LANES), lambda p, j: (p, 0, 0))
    blk_pair = pl.BlockSpec((2, t, LANES), lambda p, j: (p, j, 0))
    whole_cols = pl.BlockSpec((s_len, LANES), lambda p, j: (0, p))
    blk_cols = pl.BlockSpec((t, LANES), lambda p, j: (j, p))
    res = pl.pallas_call(
        body, name="attn_bwd", grid=(N_PAIRS, nq),
        in_specs=[whole_pair, blk_pair, blk_cols, whole_cols, whole_cols, whole_cols] + [ANY] * (n + 1),
        out_specs=[whole_pair, blk_pair, pl.BlockSpec((t, LANES), lambda p, j: (j, AL_V // LANES + p))] + [ANY] * n,
        out_shape=[jax.ShapeDtypeStruct((N_HEADS, s_len, LANES), F32), jax.ShapeDtypeStruct((N_HEADS, s_len, LANES), F32),
                   jax.ShapeDtypeStruct(dproj.shape, dproj.dtype)]
        + [jax.ShapeDtypeStruct((N_DEV, g.shape[1] // 2, g.shape[2]), g.dtype) for g in parts],
        scratch_shapes=_exchange_sems(n), input_output_aliases={6 + n: 2},
        compiler_params=_params(("arbitrary", "arbitrary")),
    )(qa, ka, vb, o, lse, do, *parts, dproj)
    return res[0], res[1], res[2], res[3:]


def _attn_post(dqa, dka, dproj):
    s_len = dqa.shape[1]
    tr = min(256, s_len)
    assert AL_K == AL_Q + D_MODEL and AL_Q % (2 * D_MODEL) == 0

    def body(dqa_ref, dka_ref, _, dqk_ref, dcum_ref):
        lane = lax.broadcasted_iota(jnp.int32, (tr, LANES), 1)
        lo = lane < HEAD_DIM
        dcum = jnp.zeros((tr, LANES), F32)
        for p in range(N_PAIRS):
            a0, a1, b0, b1 = dqa_ref[2 * p], dqa_ref[2 * p + 1], dka_ref[2 * p], dka_ref[2 * p + 1]
            dq = jnp.where(lo, a0, a1) * (HEAD_DIM ** -0.5)
            dqk_ref[:, p * LANES:(p + 1) * LANES] = dq.astype(dqk_ref.dtype)
            dqk_ref[:, D_MODEL + p * LANES:D_MODEL + (p + 1) * LANES] = jnp.where(lo, b0, b1).astype(dqk_ref.dtype)
            for hh, (a, b) in enumerate(((a0, b0), (a1, b1))):
                base = _aug_base(hh)
                dcum = dcum + jnp.where(lane == 2 * p + hh, a[:, base:base + 1] - b[:, base + 3:base + 4], 0.0)
        dcum_ref[...] = dcum

    heads = pl.BlockSpec((N_HEADS, tr, LANES), lambda i: (0, i, 0))
    return pl.pallas_call(
        body, name="attn_post", grid=(s_len // tr,),
        in_specs=[heads, heads, ANY],
        out_specs=[pl.BlockSpec((tr, 2 * D_MODEL), lambda i: (i, AL_Q // (2 * D_MODEL))), pl.BlockSpec((tr, LANES), lambda i: (i, 0))],
        out_shape=[jax.ShapeDtypeStruct(dproj.shape, dproj.dtype), jax.ShapeDtypeStruct((s_len, LANES), F32)],
        input_output_aliases={2: 0}, compiler_params=_params(("parallel",)),
    )(dqa, dka, dproj)


def _ln_stats(r):
    mu = _rowmean(r)
    xc = r - mu
    rstd = lax.rsqrt(_rowmean(xc * xc) + LN_EPS)
    return xc * rstd, rstd


def _ln_bwd(dxh, xh, rstd):
    return rstd * (dxh - _rowmean(dxh) - xh * _rowmean(dxh * xh))


def _rms_bwd(dgn, g, r):
    return r * dgn - (r * r * r) * g * _rowmean(dgn * g)


def _to_aligned(wt):
    out = jnp.zeros((AL_COLS, wt.shape[1]), wt.dtype)
    for dst, (lo, hi) in ((0, (0, 2048)), (AL_Q, (2576, 5648)), (AL_B, (2048, 2560)), (AL_DTF, (2560, 2576)), (AL_DTF + 16, (5648, 5664))):
        out = lax.dynamic_update_slice_in_dim(out, wt[lo:hi], dst, axis=0)
    return out


def _from_aligned(gt):
    out = jnp.zeros((IN_COLS, gt.shape[1]), gt.dtype)
    for dst, (lo, hi) in ((0, (0, AL_Q)), (2048, (AL_B, AL_DTF)), (2560, (AL_DTF, AL_DTF + 16)), (2576, (AL_Q, AL_B)),
                          (5648, (AL_DTF + 16, AL_DTF + 32))):
        out = lax.dynamic_update_slice_in_dim(out, gt[lo:hi], dst, axis=0)
    return out


def _lanes(v, at=0):
    return jnp.pad(v, ((0, 0), (at, LANES - at - v.shape[1])))


def _local_step(x, tgt, mod, w_alt, shards, sp):
    d = D_MODEL
    sh1, sc1, g1, sh2, sc2, g2 = [mod[:, i * d:(i + 1) * d] for i in range(6)]
    dt_bias_l, a_log_l, f_bias_l = _lanes(sp["dt_bias"]), _lanes(sp["a_log"]), _lanes(sp["f_bias"], F_LANE)
    d_exp = jnp.repeat(sp["d_skip"], HEAD_DIM, axis=1)

    (h1,), _ = _rowwise("mod1", lambda x, sc, sh: ([x * (1.0 + sc) + sh], []), [x], [sc1, sh1], [(d, BF16)], [])
    proj = _matmul("proj", h1, w_alt, dims=NT, tn=1152)
    xc_all, y_ssd, prevs, y_mix = _ssd_fwd(proj, sp["conv_w"], sp["conv_b"], dt_bias_l, a_log_l, d_exp, sp["ssm_norm_w"])
    cum = _fox_cum(proj, f_bias_l)
    qa, ka, vb = _attn_prep(proj, cum)
    o, lse, (g_out, g_fi, g_fo) = _attn_fwd(qa, ka, vb, shards)
    w_out = g_out.reshape(2 * d, d)
    w_fi = g_fi.transpose(1, 0, 2).reshape(d, D_FF)
    w_fo = g_fo.reshape(D_FF, d)
    (y_mix,), _ = _rowwise("attn_norm", lambda o, w: ([o * lax.rsqrt(_rowmean(o * o) + RMS_EPS) * w], []),
                           [o], [sp["attn_norm_w"]], [(d, BF16, (2 * d, 1, y_mix))], [])
    def ln1_fwd(y, x, g1, sc2, sh2, lg, lb):
        r1 = ALPHA * x + (1.0 + g1) * y
        xh, _ = _ln_stats(r1)
        x1 = xh * lg + lb
        h2 = x1 * (1.0 + sc2) + sh2
        return [y, r1, h2, h2.T], []

    def relu2(u):
        a = jnp.square(jnp.maximum(u, 0.0))
        return [a, a.T], []

    y, r1, h2, h2_t = _matmul("out_proj", y_mix, w_out, tm=512, tk=2048,
                              epi=(ln1_fwd, [x], [g1, sc2, sh2, sp["ln1_g"], sp["ln1_b"]], [F32, F32, BF16, ("T", BF16)], []))
    act, act_t = _matmul("ff_in", h2, w_fi, epi=(relu2, [], [], [BF16, ("T", BF16)], []))

    def head(ff, r1, tgt, g2, l1g, l1b, l2g, l2b):
        xh1, _ = _ln_stats(r1)
        x1 = xh1 * l1g + l1b
        xh2, rstd2 = _ln_stats(ALPHA * x1 + (1.0 + g2) * ff)
        err = xh2 * l2g + l2b - tgt
        loss = 0.5 * jnp.sum(_rowmean(err * err))
        dx2 = err * (1.0 / d)
        dr2 = _ln_bwd(dx2 * l2g, xh2, rstd2)
        return ([dr2, (1.0 + g2) * dr2],
                [_colsum(dx2 * xh2), _colsum(dx2), _colsum(dr2 * ff), jnp.full((1, LANES), loss, F32)])

    dr2, dff, d_ln2_g, d_ln2_b, d_g2, loss = _matmul(
        "ff_out", act, w_fo, tm=512, tk=D_FF,
        epi=(head, [r1, tgt], [g2, sp["ln1_g"], sp["ln1_b"], sp["ln2_g"], sp["ln2_b"]], [F32, BF16], [d, d, d, LANES]))
    du = _matmul("d_act", dff, w_fo, dims=NT, epi=(lambda da, act: ([da * (2.0 * jnp.sqrt(act.astype(F32)))], []), [act], [], [BF16], []))
    dw_fo = _matmul("dw_ff_out", act_t, dff, tk=SEQ_TK, out_dtype=BF16, by_chip="rows")
    dw_fi = _matmul("dw_ff_in", h2_t, du, tk=SEQ_TK, out_dtype=BF16, by_chip="cols")

    def ln1_bwd(dh2, r1, dr2, y, sc2, g1, lg, lb):
        xh, rstd = _ln_stats(r1)
        x1 = xh * lg + lb
        dx1 = ALPHA * dr2 + dh2 * (1.0 + sc2)
        dr1 = _ln_bwd(dx1 * lg, xh, rstd)
        return ([dr1, (1.0 + g1) * dr1],
                [_colsum(dh2 * x1), _colsum(dh2), _colsum(dx1 * xh), _colsum(dx1), _colsum(dr1 * y)])

    dr1, dy, d_sc2, d_sh2, d_ln1_g, d_ln1_b, d_g1 = _matmul(
        "dh2", du, w_fi, dims=NT, tm=512, tk=D_FF,
        epi=(ln1_bwd, [r1, dr2, y], [sc2, g1, sp["ln1_g"], sp["ln1_b"]], [F32, BF16], [d] * 5))
    dw_out = _matmul("dw_out", y_mix, dy, dims=TN, tk=SEQ_TK, out_dtype=BF16, by_chip="rows")

    def attn_norm_bwd(dyo, o, w):
        r = lax.rsqrt(_rowmean(o * o) + RMS_EPS)
        return [_rms_bwd(dyo * w, o, r)], [_colsum(dyo * o * r)]

    dymix = _matmul("dy_mix_ssm", dy, w_out[:d], dims=NT)
    do, d_attn_w = _matmul("dy_mix_att", dy, w_out[d:], dims=NT, tm=512, epi=(attn_norm_bwd, [o], [sp["attn_norm_w"]], [F32], [d]))

    dxc, ddt_tile, d_alog_l, d_dexp, d_dtb_l, dproj, d_ssm_w = _ssd_bwd(
        xc_all, proj, dt_bias_l, a_log_l, d_exp, prevs, y_ssd, dymix, sp["ssm_norm_w"])
    dqa, dka, dproj, landed = _attn_bwd(qa, ka, vb, o, lse, do, [dw_out, dw_fi, dw_fo], dproj)
    dproj, dcum = _attn_post(dqa, dka, dproj)
    dproj, d_fb_l = _fox_cum_bwd(dcum, proj, f_bias_l, ddt_tile, dproj)
    dpre, d_conv_w, d_conv_b = _conv_bwd_pre(proj, sp["conv_w"], sp["conv_b"], dxc)
    dproj = _conv_bwd_in(dpre, sp["conv_w"], dproj)
    dw_alt = _matmul("dw_in", dproj, h1, dims=TN, tm=1152, tk=SEQ_TK, out_dtype=BF16)
    part_in = _from_aligned(dw_alt).reshape(N_CHIPS, IN_COLS // N_CHIPS, d)

    def last(dh1, x, dr1, sc1):
        return [ALPHA * dr1 + dh1 * (1.0 + sc1)], [_colsum(dh1 * x), _colsum(dh1)]

    chip_in = _pair_sum(part_in, _pair_exchange(part_in), lax.axis_index("c"))
    dx, d_sc1, d_sh1, landed_in = _matmul("dh1", dproj, w_alt, tm=512, tk=AL_COLS, carry=[chip_in],
                                          epi=(last, [x, dr1], [sc1], [F32], [d, d]))

    small = {
        "mod": jnp.concatenate([d_sh1, d_sc1, d_g1, d_sh2, d_sc2, d_g2], axis=1),
        "conv_w": d_conv_w, "conv_b": d_conv_b,
        "dt_bias": d_dtb_l[:, :N_HEADS], "a_log": d_alog_l[:, :N_HEADS],
        "d_skip": jnp.sum(d_dexp.reshape(N_HEADS, HEAD_DIM), axis=1)[None, :],
        "ssm_norm_w": d_ssm_w, "f_bias": d_fb_l[:, F_LANE:F_LANE + N_HEADS], "attn_norm_w": d_attn_w,
        "ln1_g": d_ln1_g, "ln1_b": d_ln1_b, "ln2_g": d_ln2_g, "ln2_b": d_ln2_b, "loss": loss,
    }
    return dx, [landed_in, *landed], small


N_DEV = 8
N_CHIPS = 4
ANY = pl.BlockSpec(memory_space=pl.ANY)
VMEM_SPEC = pl.BlockSpec(memory_space=pltpu.VMEM)


def _place():
    x, y, c = lax.axis_index("x"), lax.axis_index("y"), lax.axis_index("c")
    return x, y, c


def _other_chips(x, y):
    return [(1 - x, y, 2 * (1 - x) + y), (x, 1 - y, 2 * x + 1 - y), (1 - x, 1 - y, 2 * (1 - x) + 1 - y)]


def _small_gather(v_ref, out_ref, send_sems, recv_sems, local_sem, after_start=None):
    x, y, c = _place()
    me = 4 * x + 2 * y + c
    mine = pltpu.make_async_copy(v_ref, out_ref.at[me], local_sem)
    mine.start()
    peers = _peers(x, y, c)

    def copy(rel, slot, to):
        return pltpu.make_async_remote_copy(src_ref=v_ref, dst_ref=out_ref.at[slot], send_sem=send_sems.at[rel],
                                            recv_sem=recv_sems.at[rel], device_id=to, device_id_type=MESH)

    sends = [copy(rel, me, peer) for rel, peer in enumerate(peers)]
    for cp in sends:
        cp.start()
    if after_start is not None:
        after_start()
    for rel, (px, py, pc) in enumerate(peers):
        copy(rel, 4 * px + 2 * py + pc, (x, y, c)).wait_recv()
    for cp in sends:
        cp.wait_send()
    mine.wait()


SMALL_GATHER_SEMS = [pltpu.SemaphoreType.DMA((N_DEV - 1,)), pltpu.SemaphoreType.DMA((N_DEV - 1,)), pltpu.SemaphoreType.DMA]


def _allgather_small(name, v):
    def body(v_ref, out_ref, *sems):
        _small_gather(v_ref, out_ref, *sems)

    return pl.pallas_call(
        body, name=name, out_shape=jax.ShapeDtypeStruct((N_DEV, *v.shape), v.dtype),
        in_specs=[VMEM_SPEC], out_specs=VMEM_SPEC, scratch_shapes=SMALL_GATHER_SEMS,
    )(v)


def _tail_exchange(vec, halves):
    n = len(halves)

    def body(v_ref, *rest):
        ins, (every_ref, total_ref), outs = rest[:n], rest[n:n + 2], rest[n + 2:2 * n + 2]
        gather_sems, (swap_send, swap_recv) = rest[2 * n + 2:2 * n + 5], rest[2 * n + 5:]
        x, y, c = _place()
        swaps = [pltpu.make_async_remote_copy(src_ref=ins[w], dst_ref=outs[w], send_sem=swap_send.at[w], recv_sem=swap_recv.at[w],
                                              device_id=(x, y, 1 - c), device_id_type=MESH) for w in range(n)]

        def start_swaps():
            for cp in swaps:
                cp.start()

        _small_gather(v_ref, every_ref, *gather_sems, after_start=start_swaps)
        acc = every_ref[0]
        for dev in range(1, N_DEV):
            acc = acc + every_ref[dev]
        total_ref[...] = acc
        for cp in swaps:
            cp.wait_recv()
        for cp in swaps:
            cp.wait_send()

    res = pl.pallas_call(
        body, name="tail_exchange",
        out_shape=[jax.ShapeDtypeStruct((N_DEV, *vec.shape), vec.dtype), jax.ShapeDtypeStruct(vec.shape, vec.dtype)]
        + [jax.ShapeDtypeStruct(h.shape, h.dtype) for h in halves],
        in_specs=[VMEM_SPEC] + [ANY] * n, out_specs=[VMEM_SPEC, VMEM_SPEC] + [ANY] * n,
        scratch_shapes=SMALL_GATHER_SEMS + [pltpu.SemaphoreType.DMA((n,)), pltpu.SemaphoreType.DMA((n,))],
    )(vec, *halves)
    return res[0], res[1], res[2:]


def _gather_shards(shard):
    def body(in_ref, out_ref, stage, send_sems, recv_sems, local_sems):
        start, finish = _shard_gather_plan(in_ref, out_ref, stage, send_sems, recv_sems, local_sems)
        start()
        finish()

    return pl.pallas_call(
        body, name="gather_w_in", out_shape=jax.ShapeDtypeStruct((N_CHIPS, *shard.shape), shard.dtype),
        in_specs=[ANY], out_specs=ANY,
        scratch_shapes=[pltpu.VMEM(shard.shape, shard.dtype), pltpu.SemaphoreType.DMA((6,)), pltpu.SemaphoreType.DMA((6,)),
                        pltpu.SemaphoreType.DMA((2,))],
        compiler_params=_params(),
    )(shard)


def _shard_gather_plan(in_ref, out_ref, stage, send_sems, recv_sems, local_sems):
    ch = in_ref.shape[1] // 2
    x, y, c = _place()
    k_me = 2 * x + y
    me, sibling = (x, y, c), (x, y, 1 - c)
    chips = _other_chips(x, y)

    def copy(idx, k, half, to, src=None):
        cols = out_ref.at[k, :, pl.ds(pl.multiple_of(half * ch, ch), ch)]
        return pltpu.make_async_remote_copy(src_ref=cols if src is None else src, dst_ref=cols, send_sem=send_sems.at[idx],
                                            recv_sem=recv_sems.at[idx], device_id=to, device_id_type=MESH)

    mine = in_ref.at[:, pl.ds(pl.multiple_of(c * ch, ch), ch)]
    sends = [copy(j, k_me, c, (cx, cy, c), src=mine) for j, (cx, cy, _) in enumerate(chips)]
    load = pltpu.make_async_copy(in_ref, stage, local_sems.at[0])
    store = pltpu.make_async_copy(stage, out_ref.at[k_me], local_sems.at[1])

    def start():
        for cp in sends:
            cp.start()
        load.start()

    def finish():
        load.wait()
        store.start()
        forwards = []
        for j, (_, _, kj) in enumerate(chips):
            copy(j, kj, c, me).wait_recv()
            forwards.append(copy(3 + j, kj, c, sibling))
            forwards[-1].start()
        for j, (_, _, kj) in enumerate(chips):
            copy(3 + j, kj, 1 - c, me).wait_recv()
        for cp in sends + forwards:
            cp.wait_send()
        store.wait()

    return start, finish


def _peers(x, y, c):
    return [((1 - x) if rel & 4 else x, (1 - y) if rel & 2 else y, (1 - c) if rel & 1 else c) for rel in range(1, N_DEV)]


def _exchange_sems(n):
    return [pltpu.SemaphoreType.DMA((n, N_DEV - 1)), pltpu.SemaphoreType.DMA((n, N_DEV - 1)), pltpu.SemaphoreType.DMA((n,))]


def _gather_plan(ins, outs, send_sems, recv_sems, local_sems):
    x, y, c = _place()
    k_me = 2 * x + y
    peers = [(rel, p) for rel, p in enumerate(_peers(x, y, c)) if (rel + 1) & 6]

    def copy(w, rel, k, half, to, src=None):
        rh = ins[w].shape[0] // 2
        rows = outs[w].at[k, pl.ds(pl.multiple_of(half * rh, rh), rh), :]
        return pltpu.make_async_remote_copy(src_ref=rows if src is None else src, dst_ref=rows, send_sem=send_sems.at[w, rel],
                                            recv_sem=recv_sems.at[w, rel], device_id=to, device_id_type=MESH)

    def mine(w):
        rh = ins[w].shape[0] // 2
        return ins[w].at[pl.ds(pl.multiple_of(c * rh, rh), rh), :]

    n = len(ins)
    local = [pltpu.make_async_copy(ins[w], outs[w].at[k_me], local_sems.at[w]) for w in range(n)]
    sends = [copy(w, rel, k_me, c, peer, src=mine(w)) for w in range(n) for rel, peer in peers]

    def start():
        for cp in local + sends:
            cp.start()

    def finish():
        for w in range(n):
            for rel, (px, py, pc) in peers:
                copy(w, rel, 2 * px + py, pc, (x, y, c)).wait_recv()
        for cp in sends:
            cp.wait_send()
        for cp in local:
            cp.wait()

    return start, finish


def _reduce_plan(ins, outs, send_sems, recv_sems, local_sems):
    x, y, c = _place()
    me = 4 * x + 2 * y + c
    peers = _peers(x, y, c)

    def block(w, k, half):
        rh = ins[w].shape[1] // 2
        return ins[w].at[k, pl.ds(pl.multiple_of(half * rh, rh), rh), :]

    def copy(w, rel, src, slot, to):
        return pltpu.make_async_remote_copy(src_ref=src, dst_ref=outs[w].at[slot], send_sem=send_sems.at[w, rel],
                                            recv_sem=recv_sems.at[w, rel], device_id=to, device_id_type=MESH)

    n = len(ins)
    local = [pltpu.make_async_copy(block(w, 2 * x + y, c), outs[w].at[me], local_sems.at[w]) for w in range(n)]
    sends = [copy(w, rel, block(w, 2 * px + py, pc), me, (px, py, pc)) for w in range(n) for rel, (px, py, pc) in enumerate(peers)]

    def start():
        for cp in local + sends:
            cp.start()

    def finish():
        for w in range(n):
            for rel, (px, py, pc) in enumerate(peers):
                copy(w, rel, block(w, 2 * x + y, c), 4 * px + 2 * py + pc, (x, y, c)).wait_recv()
        for cp in sends:
            cp.wait_send()
        for cp in local:
            cp.wait()

    return start, finish


def _scatter_plan(ins, outs, send_sems, recv_sems, local_sems):
    x, y, c = _place()
    k_me = 2 * x + y
    chips = _other_chips(x, y)

    def copy(w, j, src_k, dst_k, to):
        return pltpu.make_async_remote_copy(src_ref=ins[w].at[src_k], dst_ref=outs[w].at[dst_k], send_sem=send_sems.at[w, j],
                                            recv_sem=recv_sems.at[w, j], device_id=to, device_id_type=MESH)

    n = len(ins)
    local = [pltpu.make_async_copy(ins[w].at[k_me], outs[w].at[k_me], local_sems.at[w]) for w in range(n)]
    sends = [copy(w, j, kj, k_me, (cx, cy, c)) for w in range(n) for j, (cx, cy, kj) in enumerate(chips)]

    def start():
        for cp in local + sends:
            cp.start()

    def finish():
        for w in range(n):
            for j, (_, _, kj) in enumerate(chips):
                copy(w, j, k_me, kj, (x, y, c)).wait_recv()
        for cp in sends:
            cp.wait_send()
        for cp in local:
            cp.wait()

    return start, finish


def _row_tile(r, mult=2 * SUBLANES):
    if r % 256 == 0:
        return 256
    return max([t for t in range(mult, 513, mult) if r % t == 0], default=r)


def _pair_exchange(g):
    _, r, cdim = g.shape
    ch = cdim // 2

    def body(g_ref, got_ref, send_sem, recv_sem):
        x, y, c = _place()
        cp = pltpu.make_async_remote_copy(src_ref=g_ref.at[:, :, pl.ds(pl.multiple_of((1 - c) * ch, ch), ch)], dst_ref=got_ref,
                                          send_sem=send_sem, recv_sem=recv_sem, device_id=(x, y, 1 - c), device_id_type=MESH)
        cp.start()
        cp.wait_recv()
        cp.wait_send()

    return pl.pallas_call(
        body, name="pair_exchange", out_shape=jax.ShapeDtypeStruct((N_CHIPS, r, ch), g.dtype),
        in_specs=[ANY], out_specs=ANY, scratch_shapes=[pltpu.SemaphoreType.DMA, pltpu.SemaphoreType.DMA],
    )(g)


def _pair_sum(g, got, c):
    _, r, cdim = g.shape
    ch = cdim // 2
    tr = _row_tile(r)

    def body(c_ref, g_ref, got_ref, o_ref):
        o_ref[...] = (g_ref[...].astype(F32) + got_ref[...].astype(F32)).astype(o_ref.dtype)

    blk = pl.BlockSpec((1, tr, ch), lambda k, i, c_ref: (k, i, 0))
    return pl.pallas_call(
        body, name="pair_sum",
        grid_spec=pltpu.PrefetchScalarGridSpec(
            num_scalar_prefetch=1, grid=(N_CHIPS, r // tr),
            in_specs=[pl.BlockSpec((1, tr, ch), lambda k, i, c_ref: (k, i, c_ref[0])), blk], out_specs=blk),
        out_shape=jax.ShapeDtypeStruct((N_CHIPS, r, ch), BF16),
        compiler_params=_params(("parallel", "parallel")),
    )(jnp.reshape(c, (1,)).astype(jnp.int32), g, got)


def _sum_blocks(name, parts):
    k, r, cdim = parts.shape
    tr = _row_tile(r)

    def body(p_ref, o_ref):
        acc = p_ref[0].astype(F32)
        for i in range(1, k):
            acc = acc + p_ref[i].astype(F32)
        o_ref[...] = acc

    return pl.pallas_call(
        body, name=name, grid=(r // tr,),
        in_specs=[pl.BlockSpec((k, tr, cdim), lambda i: (0, i, 0))], out_specs=pl.BlockSpec((tr, cdim), lambda i: (i, 0)),
        out_shape=jax.ShapeDtypeStruct((r, cdim), F32), compiler_params=_params(("parallel",)),
    )(parts)


ADA_SHARD = 6 * D_MODEL // N_CHIPS


def _mod_part(c_all, w_shard, b_shard):
    tn = 512

    def body(c_ref, w_ref, b_ref, o_ref):
        o_ref[...] = _dot(_silu(c_ref[...]).astype(BF16), w_ref[...].astype(BF16)) + b_ref[...]

    return pl.pallas_call(
        body, name="mod_part", grid=(ADA_SHARD // tn,),
        in_specs=[pl.BlockSpec((N_DEV, D_MODEL), lambda j: (0, 0)), pl.BlockSpec((D_MODEL, tn), lambda j: (0, j)),
                  pl.BlockSpec((1, tn), lambda j: (0, j))],
        out_specs=pl.BlockSpec((N_DEV, tn), lambda j: (0, j)),
        out_shape=jax.ShapeDtypeStruct((N_DEV, ADA_SHARD), F32), compiler_params=_params(("parallel",)),
    )(c_all, w_shard, b_shard)


def _w_ada_grad(c_all_t, dmod_shard):
    tm = 256

    def body(ct_ref, dm_ref, o_ref):
        act = _silu(ct_ref[...])
        acc = act[:, 0:1] * dm_ref[0:1, :]
        for dev in range(1, N_DEV):
            acc = acc + act[:, dev:dev + 1] * dm_ref[dev:dev + 1, :]
        o_ref[...] = acc

    return pl.pallas_call(
        body, name="w_ada_grad", grid=(D_MODEL // tm,),
        in_specs=[pl.BlockSpec((tm, N_DEV), lambda i: (i, 0)), pl.BlockSpec((N_DEV, ADA_SHARD), lambda i: (0, 0))],
        out_specs=pl.BlockSpec((tm, ADA_SHARD), lambda i: (i, 0)),
        out_shape=jax.ShapeDtypeStruct((D_MODEL, ADA_SHARD), F32), compiler_params=_params(("parallel",)),
    )(c_all_t, dmod_shard)


def _adamw_math(w, g, m, v):
    nm = ADAM_B1 * m + (1.0 - ADAM_B1) * g
    nv = ADAM_B2 * v + (1.0 - ADAM_B2) * jnp.square(g)
    m_hat = nm / (1.0 - ADAM_B1 ** ADAM_STEP)
    v_hat = nv / (1.0 - ADAM_B2 ** ADAM_STEP)
    return -ADAM_LR * (m_hat / (jnp.sqrt(v_hat) + ADAM_EPS) + ADAM_WD * w), nm, nv


def _adamw(name, w, g, m, v):
    _, r, cdim = w.shape
    tr = 256 if r % 256 == 0 else r

    def body(w_ref, g_ref, m_ref, v_ref, go_ref, d_ref, nm_ref, nv_ref):
        go_ref[...] = g_ref[...]
        d_ref[...], nm_ref[...], nv_ref[...] = _adamw_math(w_ref[...], g_ref[...], m_ref[...], v_ref[...])

    blk = pl.BlockSpec((None, tr, cdim), lambda i: (0, i, 0))
    return pl.pallas_call(
        body, name=name, grid=(r // tr,), in_specs=[blk, pl.BlockSpec((tr, cdim), lambda i: (i, 0)), blk, blk], out_specs=[blk] * 4,
        out_shape=[jax.ShapeDtypeStruct((1, r, cdim), F32)] * 4, compiler_params=_params(("parallel",)),
    )(w, g, m, v)


def _adamw_pair(name, w, mine, other, m, v, c, by_cols=False):
    _, r, cdim = w.shape
    hr, hc = mine.shape
    tr = _row_tile(hr, SUBLANES)
    per = hr // tr

    def body(c_ref, w_ref, a_ref, b_ref, m_ref, v_ref, g_ref, d_ref, nm_ref, nv_ref):
        half = pl.program_id(1) if by_cols else pl.program_id(0) // per
        g = jnp.where(half == c_ref[0], a_ref[...], b_ref[...])
        g_ref[...] = g
        d_ref[...], nm_ref[...], nv_ref[...] = _adamw_math(w_ref[...], g, m_ref[...], v_ref[...])

    blk = pl.BlockSpec((None, tr, hc), lambda i, j, c_ref: (0, i, j))
    half = pl.BlockSpec((tr, hc), lambda i, j, c_ref: (i % per, 0))
    return pl.pallas_call(
        body, name=name,
        grid_spec=pltpu.PrefetchScalarGridSpec(num_scalar_prefetch=1, grid=(r // tr, cdim // hc),
                                               in_specs=[blk, half, half, blk, blk], out_specs=[blk] * 4),
        out_shape=[jax.ShapeDtypeStruct((1, r, cdim), F32)] * 4, compiler_params=_params(("parallel", "parallel")),
    )(jnp.reshape(c, (1,)).astype(jnp.int32), w, mine, other, m, v)


SMALL = ["b_ada", "conv_b", "dt_bias", "a_log", "d_skip", "ssm_norm_w", "f_bias", "attn_norm_w", "ln1_g", "ln1_b", "ln2_g", "ln2_b"]


def _pack(vs):
    pieces = []
    for v in vs:
        pieces.append(v)
        if v.shape[1] % LANES:
            pieces.append(jnp.zeros((1, -v.shape[1] % LANES), v.dtype))
    return jnp.concatenate(pieces, axis=1)


def _adamw_small(total, offs, ws, ms, vs):
    n = len(ws)

    def body(*refs):
        t_ref, outs = refs[0], refs[1 + 3 * n:]
        for i in range(n):
            g = t_ref[:, offs[i]:offs[i] + ws[i].shape[1]]
            dl, nm, nv = _adamw_math(refs[1 + i][...], g, refs[1 + n + i][...], refs[1 + 2 * n + i][...])
            outs[4 * i][...], outs[4 * i + 1][...], outs[4 * i + 2][...], outs[4 * i + 3][...] = g, dl, nm, nv

    res = pl.pallas_call(
        body, name="adamw_small", in_specs=[VMEM_SPEC] * (1 + 3 * n), out_specs=[VMEM_SPEC] * (4 * n),
        out_shape=[jax.ShapeDtypeStruct(w.shape, F32) for w in ws for _ in range(4)],
    )(total, *ws, *ms, *vs)
    return [res[4 * i:4 * i + 4] for i in range(n)]


def kernel(x, c, w_ada, b_ada, w_in, conv_w, conv_b, dt_bias, a_log, d_skip, ssm_norm_w, f_bias, attn_norm_w, w_out, ln1_g, ln1_b, w_ff_in, w_ff_out, ln2_g, ln2_b, loss_target, m_w_ada, m_b_ada, m_w_in, m_conv_w, m_conv_b, m_dt_bias, m_a_log, m_d_skip, m_ssm_norm_w, m_f_bias, m_attn_norm_w, m_w_out, m_ln1_g, m_ln1_b, m_w_ff_in, m_w_ff_out, m_ln2_g, m_ln2_b, v_w_ada, v_b_ada, v_w_in, v_conv_w, v_conv_b, v_dt_bias, v_a_log, v_d_skip, v_ssm_norm_w, v_f_bias, v_attn_norm_w, v_w_out, v_ln1_g, v_ln1_b, v_w_ff_in, v_w_ff_out, v_ln2_g, v_ln2_b):
    a = dict(b_ada=b_ada, conv_b=conv_b, dt_bias=dt_bias, a_log=a_log, d_skip=d_skip, ssm_norm_w=ssm_norm_w, f_bias=f_bias,
             attn_norm_w=attn_norm_w, ln1_g=ln1_g, ln1_b=ln1_b, ln2_g=ln2_g, ln2_b=ln2_b)
    ms = dict(b_ada=m_b_ada, conv_b=m_conv_b, dt_bias=m_dt_bias, a_log=m_a_log, d_skip=m_d_skip, ssm_norm_w=m_ssm_norm_w,
              f_bias=m_f_bias, attn_norm_w=m_attn_norm_w, ln1_g=m_ln1_g, ln1_b=m_ln1_b, ln2_g=m_ln2_g, ln2_b=m_ln2_b)
    vs = dict(b_ada=v_b_ada, conv_b=v_conv_b, dt_bias=v_dt_bias, a_log=v_a_log, d_skip=v_d_skip, ssm_norm_w=v_ssm_norm_w,
              f_bias=v_f_bias, attn_norm_w=v_attn_norm_w, ln1_g=v_ln1_g, ln1_b=v_ln1_b, ln2_g=v_ln2_g, ln2_b=v_ln2_b)
    xi, yi, ci = _place()
    chip = 2 * xi + yi
    me = 4 * xi + 2 * yi + ci
    d = D_MODEL
    conv_shard = CONV_DIM // N_CHIPS

    first = _allgather_small("gather_c", jnp.concatenate([c, conv_w[0].reshape(1, CONV_W * conv_shard)], axis=1))[:, 0]
    c_all = first[:, :d]
    conv_w_full = first[::2, d:].reshape(N_CHIPS, CONV_W, conv_shard).transpose(1, 0, 2).reshape(CONV_W, CONV_DIM)
    b_shard = lax.dynamic_slice_in_dim(b_ada, chip * ADA_SHARD, ADA_SHARD, axis=1)
    parts = _allgather_small("gather_mod", _mod_part(c_all, w_ada[0], b_shard))
    mod = lax.dynamic_index_in_dim(parts[::2], me, axis=1, keepdims=False).reshape(1, 6 * d)

    w_in_t, m_w_in_t, v_w_in_t = [jnp.transpose(t, (0, 2, 1)) for t in (w_in, m_w_in, v_w_in)]
    w_alt = _to_aligned(_gather_shards(w_in_t[0].astype(BF16)).reshape(IN_COLS, d))

    sp = {n: a[n] for n in SMALL[1:]}
    sp["conv_w"] = conv_w_full
    shards = [w_out[0].astype(BF16), w_ff_in[0].astype(BF16), w_ff_out[0].astype(BF16)]
    dx, landed, small = _local_step(x[0], loss_target[0], mod, w_alt, shards, sp)

    mine = [_sum_blocks("dev_sum_%d" % i, p) for i, p in enumerate(landed)]
    names = ["mod"] + SMALL[1:]
    vec = _pack([small[n] for n in names] + [small["conv_w"].reshape(1, CONV_W * CONV_DIM), small["loss"]])
    every, total, other = _tail_exchange(vec, mine)
    widths = [6 * d] + [a[n].shape[1] for n in SMALL[1:]]
    offs = [0]
    for w in widths:
        offs.append(offs[-1] + w + (-w % LANES))
    g_conv_w_full = total[:, offs[-1]:offs[-1] + CONV_W * CONV_DIM].reshape(CONV_W, CONV_DIM)
    loss = total[0, offs[-1] + CONV_W * CONV_DIM]
    dmod_shard = lax.dynamic_slice_in_dim(every[:, 0, :6 * d], chip * ADA_SHARD, ADA_SHARD, axis=1)
    g_w_ada = _w_ada_grad(c_all.T, dmod_shard)
    g_conv_w = lax.dynamic_slice_in_dim(g_conv_w_full, chip * conv_shard, conv_shard, axis=1)

    grads, deltas, new_m, new_v = {}, {}, {}, {}
    paired = dict(w_in=(w_in_t, m_w_in_t, v_w_in_t), w_out=(w_out, m_w_out, v_w_out), w_ff_in=(w_ff_in, m_w_ff_in, v_w_ff_in),
                  w_ff_out=(w_ff_out, m_w_ff_out, v_w_ff_out))
    for i, (n, (w, m, v)) in enumerate(paired.items()):
        res = _adamw_pair("adamw_" + n, w, mine[i], other[i], m, v, ci, by_cols=n == "w_in")
        grads[n], deltas[n], new_m[n], new_v[n] = [jnp.transpose(t, (0, 2, 1)) for t in res] if n == "w_in" else res
    for n, g, (w, m, v) in (("w_ada", g_w_ada, (w_ada, m_w_ada, v_w_ada)), ("conv_w", g_conv_w, (conv_w, m_conv_w, v_conv_w))):
        grads[n], deltas[n], new_m[n], new_v[n] = _adamw("adamw_" + n, w, g, m, v)
    for n, res in zip(SMALL, _adamw_small(total, offs, [a[n] for n in SMALL], [ms[n] for n in SMALL], [vs[n] for n in SMALL])):
        grads[n], deltas[n], new_m[n], new_v[n] = res

    order = ["w_ada", "b_ada", "w_in", "conv_w", "conv_b", "dt_bias", "a_log", "d_skip", "ssm_norm_w", "f_bias", "attn_norm_w", "w_out",
             "ln1_g", "ln1_b", "w_ff_in", "w_ff_out", "ln2_g", "ln2_b"]
    return (loss, dx[None], *[grads[n] for n in order], *[deltas[n] for n in order], *[new_m[n] for n in order], *[new_v[n] for n in order])
```

```python
import functools

import jax
import jax.numpy as jnp
from jax import lax
from jax.experimental import pallas as pl
from jax.experimental.pallas import tpu as pltpu

F32, BF16 = jnp.float32, jnp.bfloat16

D_MODEL = 1024
N_HEADS = 16
HEAD_DIM = 64
N_PAIRS = N_HEADS // 2
SSM_GROUPS = 2
SSM_STATE = 128
CHUNK = 128
CONV_W = 4
CONV_DIM = 1536
D_FF = 4096
IN_COLS = 5664
ALPHA = 2.0 ** 0.25
LN_EPS = 1e-5
RMS_EPS = 1e-5
LANES = 128
SUBLANES = 8

AL_Z, AL_XS, AL_Q, AL_K, AL_V, AL_B, AL_C, AL_DTF = 0, 1024, 2048, 3072, 4096, 5120, 5376, 5632
AL_COLS = 5760
F_LANE = 16

ADAM_LR, ADAM_B1, ADAM_B2, ADAM_EPS, ADAM_WD, ADAM_STEP = 0.001, 0.9, 0.999, 1e-08, 0.01, 10

VMEM_LIMIT = 56 * 1024 * 1024
SEQ_TK = 4096
MESH = pl.DeviceIdType.MESH


def _params(sem=None):
    return pltpu.CompilerParams(dimension_semantics=sem, vmem_limit_bytes=VMEM_LIMIT)


def _sigmoid(x):
    return 1.0 / (1.0 + jnp.exp(-x))


def _silu(x):
    return x * _sigmoid(x)


def _softplus(x):
    return jnp.maximum(x, 0.0) + jnp.log(1.0 + jnp.exp(-jnp.abs(x)))


def _split3(a):
    hi = a.astype(BF16)
    r = a - hi.astype(F32)
    mid = r.astype(BF16)
    lo = (r - mid.astype(F32)).astype(BF16)
    return hi, mid, lo


def _dot(a, b, dims=((1,), (0,))):
    return lax.dot_general(a, b, (dims, ((), ())), preferred_element_type=F32)


NN, NT, TN = ((1,), (0,)), ((1,), (1,)), ((0,), (0,))


def _dot3(t, a):
    hi, mid, lo = _split3(a)
    return _dot(t, hi) + _dot(t, mid) + _dot(t, lo)


def _matmul(name, a, b, *, dims=NN, out_dtype=F32, tm=1024, tn=1024, tk=1024, by_chip=None, epi=None, carry=()):
    if dims == NN:
        (m, k), n = a.shape, b.shape[1]
    elif dims == NT:
        (m, k), n = a.shape, b.shape[0]
    else:
        (k, m), n = a.shape, b.shape[1]
    if by_chip == "rows":
        tm = min(tm, m // 4)
    if by_chip == "cols":
        tn = min(tn, n // 4)
    tm, tn, tk = min(tm, m), min(tn, n), min(tk, k)
    assert m % tm == 0 and n % tn == 0 and k % tk == 0, (name, m, n, k, tm, tn, tk)
    nk = k // tk
    if by_chip == "rows":
        per = m // 4 // tm
        out_spec = pl.BlockSpec((None, tm, tn), lambda i, j, l: (i // per, i % per, j))
        out_shape = jax.ShapeDtypeStruct((4, m // 4, n), out_dtype)
    elif by_chip == "cols":
        per = n // 4 // tn
        out_spec = pl.BlockSpec((None, tm, tn), lambda i, j, l: (j // per, i, j % per))
        out_shape = jax.ShapeDtypeStruct((4, m, n // 4), out_dtype)
    else:
        out_spec = pl.BlockSpec((tm, tn), lambda i, j, l: (i, j))
        out_shape = jax.ShapeDtypeStruct((m, n), out_dtype)
    a_spec = pl.BlockSpec((tk, tm), lambda i, j, l: (l, i)) if dims == TN else pl.BlockSpec((tm, tk), lambda i, j, l: (i, l))
    b_spec = pl.BlockSpec((tn, tk), lambda i, j, l: (j, l)) if dims == NT else pl.BlockSpec((tk, tn), lambda i, j, l: (l, j))

    tile = pl.BlockSpec((tm, tn), lambda i, j, l: (i, j))
    in_specs, args, out_specs, out_shape = [a_spec, b_spec], [a, b], [out_spec], [out_shape]
    fn, n_tiles, n_sums = None, 1, 0
    if epi is not None:
        fn, fulls, vecs, outs, sums = epi
        assert by_chip is None and (not sums or n == tn), name
        in_specs = in_specs + [tile] * len(fulls) + [pl.BlockSpec((1, tn), lambda i, j, l: (0, j))] * len(vecs)
        args = args + list(fulls) + list(vecs)
        flipped = pl.BlockSpec((tn, tm), lambda i, j, l: (j, i))
        out_specs = [flipped if isinstance(dt, tuple) else tile for dt in outs] + [pl.BlockSpec((1, w), lambda i, j, l: (0, 0)) for w in sums]
        out_shape = [jax.ShapeDtypeStruct((n, m), dt[1]) if isinstance(dt, tuple) else jax.ShapeDtypeStruct((m, n), dt) for dt in outs]
        out_shape += [jax.ShapeDtypeStruct((1, w), F32) for w in sums]
        n_tiles, n_sums = len(outs), len(sums)
    n_in, n_out, n_c = len(args), len(out_specs), len(carry)
    scratch = [pltpu.VMEM((tm, tn) if nk > 1 else (SUBLANES, LANES), F32)]
    if n_c:
        in_specs, args = in_specs + [ANY] * n_c, args + list(carry)
        out_specs = out_specs + [ANY] * n_c
        out_shape = out_shape + [jax.ShapeDtypeStruct(g.shape, g.dtype) for g in carry]
        scratch = scratch + _exchange_sems(n_c)
    gm, gn = m // tm, n // tn

    def body(*refs):
        a_ref, b_ref = refs[:2]
        ins, outs = refs[2:n_in], refs[n_in + n_c:n_in + n_c + n_out]
        acc_ref = refs[n_in + 2 * n_c + n_out]
        i, j, l = pl.program_id(0), pl.program_id(1), pl.program_id(2)
        if n_c:
            start, wait = _scatter_plan(refs[n_in:n_in + n_c], refs[n_in + n_c + n_out:n_in + 2 * n_c + n_out], *refs[n_in + 2 * n_c + n_out + 1:])
            pl.when((i == 0) & (j == 0) & (l == 0))(start)
        part = _dot(a_ref[...].astype(BF16), b_ref[...].astype(BF16), dims)

        def finish(res):
            if fn is None:
                outs[0][...] = res.astype(outs[0].dtype)
                return
            tiles, colsums = fn(res, *[r[...] for r in ins])
            for r, val in zip(outs[:n_tiles], tiles):
                r[...] = val.astype(r.dtype)
            if n_sums:
                @pl.when(i == 0)
                def _():
                    for r in outs[n_tiles:]:
                        r[...] = jnp.zeros_like(r)
                for r, val in zip(outs[n_tiles:], colsums):
                    r[...] += val

        if nk == 1:
            finish(part)
        else:
            @pl.when(l == 0)
            def _():
                acc_ref[...] = part

            @pl.when((l > 0) & (l < nk - 1))
            def _():
                acc_ref[...] += part

            @pl.when(l == nk - 1)
            def _():
                finish(acc_ref[...] + part)

        if n_c:
            pl.when((i == gm - 1) & (j == gn - 1) & (l == nk - 1))(wait)

    res = pl.pallas_call(
        body, name=name, grid=(gm, gn, nk),
        in_specs=in_specs, out_specs=out_specs, out_shape=out_shape, scratch_shapes=scratch,
        compiler_params=_params(("arbitrary",) * 3 if n_c or n_sums else ("parallel", "parallel", "arbitrary")),
    )(*args)
    return res[0] if len(res) == 1 else res


def _rowwise(name, fn, fulls, vecs, out_fulls, out_vecs, tr=256):
    fulls = [f if isinstance(f, tuple) else (f, f.shape[1], 0) for f in fulls]
    s = fulls[0][0].shape[0]
    tr = min(tr, s)
    out_fulls = [o if len(o) == 3 else (*o, (o[0], 0, None)) for o in out_fulls]
    into = [(k, slab[2]) for k, (_, _, slab) in enumerate(out_fulls) if slab[2] is not None]
    nf, nv, nof, nov = len(fulls), len(vecs), len(out_fulls), len(out_vecs)
    in_specs = [pl.BlockSpec((tr, w), functools.partial(lambda i, cb: (i, cb), cb=cb)) for (_, w, cb) in fulls]
    in_specs += [pl.BlockSpec(v.shape, lambda i: (0, 0)) for v in vecs] + [ANY] * len(into)
    out_shape = [jax.ShapeDtypeStruct((s, slab[0]), dt) for (_, dt, slab) in out_fulls] + [jax.ShapeDtypeStruct((1, w), F32) for w in out_vecs]
    out_specs = [pl.BlockSpec((tr, w), functools.partial(lambda i, cb: (i, cb), cb=slab[1])) for (w, _, slab) in out_fulls]
    out_specs += [pl.BlockSpec((1, w), lambda i: (0, 0)) for w in out_vecs]

    def body(*refs):
        outs = refs[nf + nv + len(into):]
        of, ov = fn(*[r[...] for r in refs[:nf + nv]])
        for r, val in zip(outs[:nof], of):
            r[...] = val.astype(r.dtype)
        if nov:
            @pl.when(pl.program_id(0) == 0)
            def _():
                for r in outs[nof:]:
                    r[...] = jnp.zeros_like(r)
            for r, val in zip(outs[nof:], ov):
                r[...] += val

    res = pl.pallas_call(
        body, name=name, grid=(s // tr,), in_specs=in_specs, out_specs=out_specs, out_shape=out_shape,
        input_output_aliases={nf + nv + pos: k for pos, (k, _) in enumerate(into)},
        compiler_params=_params(("arbitrary",)),
    )(*[f[0] for f in fulls], *vecs, *[buf for _, buf in into])
    return res[:nof], res[nof:]


def _colsum(x):
    return jnp.sum(x, axis=0, keepdims=True)


def _rowmean(x):
    return jnp.mean(x, axis=-1, keepdims=True)


CONV_CB = 512
CONV_TR = 512


def _shift_down(u, halo, j):
    if j == 0:
        return u
    ru = pltpu.roll(u, j, 0)
    row8 = lax.broadcasted_iota(jnp.int32, halo.shape, 0)
    top = jnp.where(row8 < j, pltpu.roll(halo, j, 0), ru[:SUBLANES])
    return jnp.concatenate([top, ru[SUBLANES:]], axis=0)


def _shift_up(d, halo, j):
    if j == 0:
        return d
    tr = d.shape[0]
    rd = pltpu.roll(d, tr - j, 0)
    row8 = lax.broadcasted_iota(jnp.int32, halo.shape, 0)
    bot = jnp.where(row8 >= SUBLANES - j, pltpu.roll(halo, SUBLANES - j, 0), rd[tr - SUBLANES:])
    return jnp.concatenate([rd[:tr - SUBLANES], bot], axis=0)


def _conv_col(cb):
    return jnp.where(cb < 2, AL_XS // CONV_CB + cb, AL_B // CONV_CB)


def _conv_specs(s, tr):
    per8 = tr // SUBLANES
    blk = pl.BlockSpec((tr, CONV_CB), lambda cb, i: (i, _conv_col(cb)))
    prev = pl.BlockSpec((SUBLANES, CONV_CB), lambda cb, i: (jnp.maximum(i * per8 - 1, 0), _conv_col(cb)))
    return blk, prev


def _conv_pre(u, halo, w_ref, b_ref, first):
    halo = jnp.where(first, 0.0, halo)
    acc = b_ref[...] + w_ref[CONV_W - 1:CONV_W, :] * u
    shifted = [u]
    for j in range(1, CONV_W):
        sh = _shift_down(u, halo, j)
        shifted.append(sh)
        acc = acc + w_ref[CONV_W - 1 - j:CONV_W - j, :] * sh
    return acc, shifted


def _conv_bwd_pre(proj, conv_w, conv_b, dxc):
    s = proj.shape[0]
    tr = min(CONV_TR, s)
    blk, prev = _conv_specs(s, tr)

    def body(u_ref, h_ref, w_ref, b_ref, d_ref, dpre_ref, dw_ref, db_ref):
        i = pl.program_id(1)
        pre, shifted = _conv_pre(u_ref[...], h_ref[...], w_ref, b_ref, i == 0)
        sg = _sigmoid(pre)
        dpre = d_ref[...] * (sg * (1.0 + pre * (1.0 - sg)))
        dpre_ref[...] = dpre

        @pl.when(i == 0)
        def _():
            dw_ref[...] = jnp.zeros_like(dw_ref)
            db_ref[...] = jnp.zeros_like(db_ref)

        db_ref[...] += _colsum(dpre)
        for j in range(CONV_W):
            dw_ref[CONV_W - 1 - j:CONV_W - j, :] += _colsum(dpre * shifted[j])

    own = pl.BlockSpec((tr, CONV_CB), lambda cb, i: (i, cb))
    wspec = pl.BlockSpec((CONV_W, CONV_CB), lambda cb, i: (0, cb))
    bspec = pl.BlockSpec((1, CONV_CB), lambda cb, i: (0, cb))
    return pl.pallas_call(
        body, name="conv_bwd_pre", grid=(CONV_DIM // CONV_CB, s // tr),
        in_specs=[blk, prev, wspec, bspec, own], out_specs=[own, wspec, bspec],
        out_shape=[jax.ShapeDtypeStruct((s, CONV_DIM), F32), jax.ShapeDtypeStruct((CONV_W, CONV_DIM), F32),
                   jax.ShapeDtypeStruct((1, CONV_DIM), F32)],
        compiler_params=_params(("parallel", "arbitrary")),
    )(proj, proj, conv_w, conv_b, dxc)


def _conv_bwd_in(dpre, conv_w, dproj):
    s = dpre.shape[0]
    tr = min(CONV_TR, s)
    per8 = tr // SUBLANES
    last8 = s // SUBLANES - 1
    nb = s // tr

    def body(d_ref, n_ref, w_ref, _, o_ref):
        d = d_ref[...]
        halo = jnp.where(pl.program_id(1) == nb - 1, 0.0, n_ref[...])
        acc = w_ref[CONV_W - 1:CONV_W, :] * d
        for j in range(1, CONV_W):
            acc = acc + w_ref[CONV_W - 1 - j:CONV_W - j, :] * _shift_up(d, halo, j)
        o_ref[...] = acc.astype(o_ref.dtype)

    own = pl.BlockSpec((tr, CONV_CB), lambda cb, i: (i, cb))
    nxt = pl.BlockSpec((SUBLANES, CONV_CB), lambda cb, i: (jnp.minimum((i + 1) * per8, last8), cb))
    return pl.pallas_call(
        body, name="conv_bwd_in", grid=(CONV_DIM // CONV_CB, nb),
        in_specs=[own, nxt, pl.BlockSpec((CONV_W, CONV_CB), lambda cb, i: (0, cb)), ANY],
        out_specs=pl.BlockSpec((tr, CONV_CB), lambda cb, i: (i, _conv_col(cb))),
        out_shape=jax.ShapeDtypeStruct(dproj.shape, dproj.dtype), input_output_aliases={3: 0},
        compiler_params=_params(("parallel", "parallel")),
    )(dpre, dpre, conv_w, dproj)


XC_B, XC_C = 1024, 1280


def _tile_iotas():
    row = lax.broadcasted_iota(jnp.int32, (CHUNK, LANES), 0)
    lane = lax.broadcasted_iota(jnp.int32, (CHUNK, LANES), 1)
    return row, lane


def _ssd_scalars(dtf_ref, bias_ref, alog_ref, row, lane):
    head = lane[:1] < N_HEADS
    raw = dtf_ref[...] + bias_ref[...]
    dt = _softplus(raw)
    a_neg = jnp.where(head, -jnp.exp(alog_ref[...]), 0.0)
    a = dt * a_neg
    tril = (row >= lane).astype(BF16)
    s = _dot3(tril, a)
    return raw, dt, a_neg, s


def _pair(v, j, lo):
    return jnp.where(lo, v[:, 2 * j:2 * j + 1], v[:, 2 * j + 1:2 * j + 2])


def _head_sum(x, lo, hh):
    return jnp.sum(jnp.where(lo == (hh == 0), x, 0.0), axis=1, keepdims=True)


def _decay_masks(s, st, h, row, lane):
    s_col = jnp.broadcast_to(s[:, h:h + 1], (CHUNK, LANES))
    s_row = jnp.broadcast_to(st[h:h + 1, :], (CHUNK, LANES))
    lm = jnp.where(row >= lane, jnp.exp(s_col - s_row), 0.0)
    lmt = jnp.where(row <= lane, jnp.exp(s_row - s_col), 0.0)
    return lm, lmt


def _gated_norm(y, z, w):
    g = y * _silu(z)
    return g * lax.rsqrt(_rowmean(g * g) + RMS_EPS) * w


def _ssd_fwd(proj, conv_w, conv_b, dt_bias_l, a_log_l, d_exp, norm_w):
    s_len = proj.shape[0]
    nc = s_len // CHUNK

    def body(xs_ref, bc_ref, cw_ref, cb_ref, dtf_ref, bias_ref, alog_ref, dexp_ref, z_ref, w_ref,
             x_ref, y_ref, prevs_ref, ymix_ref, state_ref, halo_ref):
        first = pl.program_id(0) == 0

        @pl.when(first)
        def _():
            state_ref[...] = jnp.zeros_like(state_ref)
            halo_ref[...] = jnp.zeros_like(halo_ref)

        u = jnp.concatenate([xs_ref[...], bc_ref[...]], axis=1)
        pre, _ = _conv_pre(u, halo_ref[...], cw_ref, cb_ref, first)
        halo_ref[...] = u[CHUNK - SUBLANES:]
        x_ref[...] = _silu(pre)

        row, lane = _tile_iotas()
        lo = lane < HEAD_DIM
        _, dt, _, s = _ssd_scalars(dtf_ref, bias_ref, alog_ref, row, lane)
        tot = s[CHUNK - 1:CHUNK, :]
        st = s.T
        for g in range(SSM_GROUPS):
            bg = x_ref[:, XC_B + g * SSM_STATE:XC_B + (g + 1) * SSM_STATE].astype(BF16)
            cg = x_ref[:, XC_C + g * SSM_STATE:XC_C + (g + 1) * SSM_STATE].astype(BF16)
            cb = _dot(cg, bg, NT)
            for j in range(g * 4, g * 4 + 4):
                xs_p = x_ref[:, j * LANES:(j + 1) * LANES]
                dt_p, s_p, tot_p = _pair(dt, j, lo), _pair(s, j, lo), _pair(tot, j, lo[:1])
                xc_p = xs_p * dt_p
                xc_b = xc_p.astype(BF16)
                yd = []
                for hh in range(2):
                    lm, _ = _decay_masks(s, st, 2 * j + hh, row, lane)
                    yd.append(_dot((cb * lm).astype(BF16), xc_b))
                prev = state_ref[j]
                prevs_ref[0, j] = prev
                yo = _dot(cg, prev.astype(BF16)) * jnp.exp(s_p)
                y_ref[:, j * LANES:(j + 1) * LANES] = jnp.where(lo, yd[0], yd[1]) + yo + dexp_ref[:, j * LANES:(j + 1) * LANES] * xs_p
                to_end = jnp.exp(tot_p - s_p)
                state_ref[j] = jnp.exp(tot_p) * prev + _dot(bg, (xc_p * to_end).astype(BF16), TN)
        ymix_ref[...] = _gated_norm(y_ref[...], z_ref[...], w_ref[...]).astype(ymix_ref.dtype)

    vec = lambda w: pl.BlockSpec((1, w), lambda c: (0, 0))
    rows = pl.BlockSpec((CHUNK, D_MODEL), lambda c: (c, 0))
    return pl.pallas_call(
        body, name="ssd_fwd", grid=(nc,),
        in_specs=[pl.BlockSpec((CHUNK, D_MODEL), lambda c: (c, AL_XS // D_MODEL)),
                  pl.BlockSpec((CHUNK, CONV_DIM - D_MODEL), lambda c: (c, AL_B // (CONV_DIM - D_MODEL))),
                  pl.BlockSpec((CONV_W, CONV_DIM), lambda c: (0, 0)), vec(CONV_DIM),
                  pl.BlockSpec((CHUNK, LANES), lambda c: (c, AL_DTF // LANES)),
                  vec(LANES), vec(LANES), vec(D_MODEL), pl.BlockSpec((CHUNK, D_MODEL), lambda c: (c, AL_Z // D_MODEL)), vec(D_MODEL)],
        out_specs=[pl.BlockSpec((CHUNK, CONV_DIM), lambda c: (c, 0)), rows,
                   pl.BlockSpec((1, N_PAIRS, SSM_STATE, LANES), lambda c: (c, 0, 0, 0)), rows],
        out_shape=[jax.ShapeDtypeStruct((s_len, CONV_DIM), F32), jax.ShapeDtypeStruct((s_len, D_MODEL), F32),
                   jax.ShapeDtypeStruct((nc, N_PAIRS, SSM_STATE, LANES), F32), jax.ShapeDtypeStruct((s_len, 2 * D_MODEL), BF16)],
        scratch_shapes=[pltpu.VMEM((N_PAIRS, SSM_STATE, LANES), F32), pltpu.VMEM((SUBLANES, CONV_DIM), F32)],
        compiler_params=_params(("arbitrary",)),
    )(proj, proj, conv_w, conv_b, proj, dt_bias_l, a_log_l, d_exp, proj, norm_w)


def _ssd_bwd(xc_all, proj, dt_bias_l, a_log_l, d_exp, prevs, y_ssd, dymix, norm_w):
    s_len = xc_all.shape[0]
    nc = s_len // CHUNK

    def body(x_ref, dtf_ref, bias_ref, alog_ref, dexp_ref, prevs_ref, y_ref, z_ref, dym_ref, w_ref,
             dx_ref, ddt_ref, da_ref, dd_ref, dbias_ref, dz_ref, dw_ref, dstate_ref):
        @pl.when(pl.program_id(0) == 0)
        def _():
            dstate_ref[...] = jnp.zeros_like(dstate_ref)
            da_ref[...] = jnp.zeros_like(da_ref)
            dd_ref[...] = jnp.zeros_like(dd_ref)
            dbias_ref[...] = jnp.zeros_like(dbias_ref)
            dw_ref[...] = jnp.zeros_like(dw_ref)

        y, z, dyo = y_ref[...], z_ref[...], dym_ref[...]
        sg = _sigmoid(z)
        sz = z * sg
        gated = y * sz
        rn = lax.rsqrt(_rowmean(gated * gated) + RMS_EPS)
        dg = _rms_bwd(dyo * w_ref[...], gated, rn)
        dy_full = dg * sz
        dz_ref[...] = (dg * y * (sg * (1.0 + z * (1.0 - sg)))).astype(dz_ref.dtype)
        dw_ref[...] += _colsum(dyo * gated * rn)

        row, lane = _tile_iotas()
        lo = lane < HEAD_DIM
        last = row == CHUNK - 1
        raw, dt, a_neg, s = _ssd_scalars(dtf_ref, bias_ref, alog_ref, row, lane)
        tot = s[CHUNK - 1:CHUNK, :]
        st = s.T
        ds_acc = jnp.zeros((CHUNK, LANES), F32)
        ddt_acc = jnp.zeros((CHUNK, LANES), F32)
        for g in range(SSM_GROUPS):
            bcol = slice(XC_B + g * SSM_STATE, XC_B + (g + 1) * SSM_STATE)
            ccol = slice(XC_C + g * SSM_STATE, XC_C + (g + 1) * SSM_STATE)
            bg = x_ref[:, bcol].astype(BF16)
            cg = x_ref[:, ccol].astype(BF16)
            cb = _dot(cg, bg, NT)
            cbt = _dot(bg, cg, NT)
            dcb = jnp.zeros((CHUNK, LANES), F32)
            dcbt = jnp.zeros((CHUNK, LANES), F32)
            db_acc = jnp.zeros((CHUNK, LANES), F32)
            dc_acc = jnp.zeros((CHUNK, LANES), F32)
            for j in range(g * 4, g * 4 + 4):
                cols = slice(j * LANES, (j + 1) * LANES)
                xs_p, dy_p = x_ref[:, cols], dy_full[:, cols]
                dt_p, s_p, tot_p = _pair(dt, j, lo), _pair(s, j, lo), _pair(tot, j, lo[:1])
                xc_p = xs_p * dt_p
                xc_b, dy_b = xc_p.astype(BF16), dy_p.astype(BF16)
                e_p, f_p, etot_p = jnp.exp(s_p), jnp.exp(tot_p - s_p), jnp.exp(tot_p)
                prev, dnext = prevs_ref[0, j], dstate_ref[j]
                prev_b, dnext_b = prev.astype(BF16), dnext.astype(BF16)
                dd_ref[:, cols] += _colsum(dy_p * xs_p)
                dxs_p = dexp_ref[:, cols] * dy_p
                cp = _dot(cg, prev_b)
                gy = (dy_p * e_p).astype(BF16)
                dc_acc += _dot(gy, prev_b, NT)
                dstate_ref[j] = etot_p * dnext + _dot(cg, gy, TN)
                de = dy_p * cp * e_p
                bds = _dot(bg, dnext_b)
                db_acc += _dot((xc_p * f_p).astype(BF16), dnext_b, NT)
                dxc_p = bds * f_p
                df = bds * xc_p * f_p
                dtot_p = _colsum(dnext * prev) * etot_p + _colsum(df)
                dsl = de - df + jnp.where(last, dtot_p, 0.0)
                for hh in range(2):
                    h = 2 * j + hh
                    mine = lo == (hh == 0)
                    lm, lmt = _decay_masks(s, st, h, row, lane)
                    dy_h = jnp.where(mine, dy_p, 0.0).astype(BF16)
                    xc_h = jnp.where(mine, xc_p, 0.0).astype(BF16)
                    dm = _dot(dy_h, xc_b, NT)
                    dmt = _dot(xc_h, dy_b, NT)
                    mt = cbt * lmt
                    dxc_p += _dot(mt.astype(BF16), dy_h)
                    dml, dmtl = dm * lm, dmt * lmt
                    ds_h = jnp.sum(dml * cb - dmtl * cbt + jnp.where(mine, dsl, 0.0), axis=1, keepdims=True)
                    ds_acc += jnp.where(lane == h, ds_h, 0.0)
                    dcb += dml
                    dcbt += dmtl
                    ddt_acc += jnp.where(lane == h, _head_sum(dxc_p * xs_p, lo, hh), 0.0)
                dx_ref[:, cols] = dxs_p + dxc_p * dt_p
            dx_ref[:, ccol] = dc_acc + _dot(dcb.astype(BF16), bg)
            dx_ref[:, bcol] = db_acc + _dot(dcbt.astype(BF16), cg)
        triu = (row <= lane).astype(BF16)
        da = _dot3(triu, ds_acc)
        ddt = ddt_acc + da * a_neg
        da_ref[...] += _colsum(da * dt) * a_neg[:1]
        ddt_raw = jnp.where(lane < N_HEADS, ddt * _sigmoid(raw), 0.0)
        dbias_ref[...] += _colsum(ddt_raw)
        ddt_ref[...] = ddt_raw

    rev = lambda c: nc - 1 - c
    vec = lambda w: pl.BlockSpec((1, w), lambda c: (0, 0))
    rows = lambda cb: pl.BlockSpec((CHUNK, D_MODEL), lambda c: (rev(c), cb))
    return pl.pallas_call(
        body, name="ssd_bwd", grid=(nc,),
        in_specs=[pl.BlockSpec((CHUNK, CONV_DIM), lambda c: (rev(c), 0)), pl.BlockSpec((CHUNK, LANES), lambda c: (rev(c), AL_DTF // LANES)),
                  vec(LANES), vec(LANES), vec(D_MODEL),
                  pl.BlockSpec((1, N_PAIRS, SSM_STATE, LANES), lambda c: (rev(c), 0, 0, 0)),
                  rows(0), rows(AL_Z // D_MODEL), rows(0), vec(D_MODEL)],
        out_specs=[pl.BlockSpec((CHUNK, CONV_DIM), lambda c: (rev(c), 0)), pl.BlockSpec((CHUNK, LANES), lambda c: (rev(c), 0)),
                   vec(LANES), vec(D_MODEL), vec(LANES), rows(AL_Z // D_MODEL), vec(D_MODEL)],
        out_shape=[jax.ShapeDtypeStruct((s_len, CONV_DIM), F32), jax.ShapeDtypeStruct((s_len, LANES), F32),
                   jax.ShapeDtypeStruct((1, LANES), F32), jax.ShapeDtypeStruct((1, D_MODEL), F32), jax.ShapeDtypeStruct((1, LANES), F32),
                   jax.ShapeDtypeStruct((s_len, AL_COLS), BF16), jax.ShapeDtypeStruct((1, D_MODEL), F32)],
        scratch_shapes=[pltpu.VMEM((N_PAIRS, SSM_STATE, LANES), F32)],
        compiler_params=_params(("arbitrary",)),
    )(xc_all, proj, dt_bias_l, a_log_l, d_exp, prevs, y_ssd, proj, dymix, norm_w)


AUG_LANES = 6


def _aug_base(hh):
    return HEAD_DIM if hh == 0 else 0


NEG = -1e30
ATT_T = 512


def _fox_cum(proj, f_bias_l):
    s_len = proj.shape[0]
    nc = s_len // CHUNK

    def body(dtf_ref, fb_ref, cum_ref):
        row, lane = _tile_iotas()
        tril = (row >= lane).astype(BF16)
        spread = [(jnp.where(lane == AUG_LANES * row + i, 1.0, 0.0) - jnp.where(lane == AUG_LANES * row + 3 + i, 1.0, 0.0)).astype(BF16)
                  for i in range(3)]

        def step(c, carry):
            rows = pl.ds(pl.multiple_of(c * CHUNK, CHUNK), CHUNK)
            lf = -_softplus(-(dtf_ref[rows, :] + fb_ref[...]))
            lf = jnp.where(lane < N_HEADS, pltpu.roll(lf, LANES - F_LANE, 1), 0.0)
            cs = _dot3(tril, lf) + carry
            parts = _split3(cs)
            cum_ref[rows, :] = _dot(parts[0], spread[0]) + _dot(parts[1], spread[1]) + _dot(parts[2], spread[2])
            return cs[CHUNK - 1:CHUNK, :]

        lax.fori_loop(0, nc, step, jnp.zeros((1, LANES), F32))

    return pl.pallas_call(
        body, name="fox_cum", grid=(1,),
        in_specs=[pl.BlockSpec((s_len, LANES), lambda i: (0, AL_DTF // LANES)), pl.BlockSpec((1, LANES), lambda i: (0, 0))],
        out_specs=pl.BlockSpec((s_len, LANES), lambda i: (0, 0)),
        out_shape=jax.ShapeDtypeStruct((s_len, LANES), F32),
        compiler_params=_params(("arbitrary",)),
    )(proj, f_bias_l)


def _fox_cum_bwd(dcum, proj, f_bias_l, ddt_tile, dproj):
    s_len = proj.shape[0]
    nc = s_len // CHUNK

    def body(dcum_ref, dtf_ref, fb_ref, ddt_ref, _, out_ref, dfb_ref):
        row, lane = _tile_iotas()
        triu = (row <= lane).astype(BF16)
        is_f = (lane >= F_LANE) & (lane < F_LANE + N_HEADS)

        def step(t, carry):
            run, dfb = carry
            rows = pl.ds(pl.multiple_of((nc - 1 - t) * CHUNK, CHUNK), CHUNK)
            rc = _dot3(triu, dcum_ref[rows, :]) + run
            sg = _sigmoid(-(dtf_ref[rows, :] + fb_ref[...]))
            df = jnp.where(is_f, pltpu.roll(rc, F_LANE, 1) * sg, 0.0)
            out_ref[rows, :] = (df + ddt_ref[rows, :]).astype(out_ref.dtype)
            return rc[0:1, :], dfb + _colsum(df)

        _, dfb = lax.fori_loop(0, nc, step, (jnp.zeros((1, LANES), F32), jnp.zeros((1, LANES), F32)))
        dfb_ref[...] = dfb

    whole = pl.BlockSpec((s_len, LANES), lambda i: (0, 0))
    dtf_cols = pl.BlockSpec((s_len, LANES), lambda i: (0, AL_DTF // LANES))
    vec = pl.BlockSpec((1, LANES), lambda i: (0, 0))
    return pl.pallas_call(
        body, name="fox_cum_bwd", grid=(1,),
        in_specs=[whole, dtf_cols, vec, whole, ANY], out_specs=[dtf_cols, vec],
        out_shape=[jax.ShapeDtypeStruct(dproj.shape, dproj.dtype), jax.ShapeDtypeStruct((1, LANES), F32)],
        input_output_aliases={4: 0}, compiler_params=_params(("arbitrary",)),
    )(dcum, proj, f_bias_l, ddt_tile, dproj)


def _attn_prep(proj, cum):
    s_len = proj.shape[0]
    tr = min(256, s_len)

    def body(q_ref, k_ref, v_ref, cum_ref, qa_ref, ka_ref, vb_ref):
        lane = lax.broadcasted_iota(jnp.int32, (tr, LANES), 1)
        lo = lane < HEAD_DIM
        c = cum_ref[...]
        for p in range(N_PAIRS):
            cols = slice(p * LANES, (p + 1) * LANES)
            q, k = q_ref[:, cols] * (HEAD_DIM ** -0.5), k_ref[:, cols]
            for hh in range(2):
                base = _aug_base(hh)
                r = pltpu.roll(c, (base - AUG_LANES * (2 * p + hh)) % LANES, 1)
                first = (lane >= base) & (lane < base + 3)
                second = (lane >= base + 3) & (lane < base + AUG_LANES)
                mine = lo == (hh == 0)
                qa_ref[2 * p + hh] = jnp.where(mine, q, jnp.where(first, r, jnp.where(second, 1.0, 0.0))).astype(BF16)
                ka_ref[2 * p + hh] = jnp.where(mine, k, jnp.where(first, 1.0, jnp.where(second, r, 0.0))).astype(BF16)
        vb_ref[...] = v_ref[...].astype(BF16)

    assert AL_Q % D_MODEL == 0 and AL_K % D_MODEL == 0 and AL_V % D_MODEL == 0
    slab = lambda col0: pl.BlockSpec((tr, D_MODEL), lambda i: (i, col0 // D_MODEL))
    heads = pl.BlockSpec((N_HEADS, tr, LANES), lambda i: (0, i, 0))
    return pl.pallas_call(
        body, name="attn_prep", grid=(s_len // tr,),
        in_specs=[slab(AL_Q), slab(AL_K), slab(AL_V), pl.BlockSpec((tr, LANES), lambda i: (i, 0))],
        out_specs=[heads, heads, pl.BlockSpec((tr, D_MODEL), lambda i: (i, 0))],
        out_shape=[jax.ShapeDtypeStruct((N_HEADS, s_len, LANES), BF16), jax.ShapeDtypeStruct((N_HEADS, s_len, LANES), BF16),
                   jax.ShapeDtypeStruct((s_len, D_MODEL), BF16)],
        compiler_params=_params(("parallel",)),
    )(proj, proj, proj, cum)


def _attn_fwd(qa, ka, vb, shards):
    s_len = vb.shape[0]
    t = min(2 * ATT_T, s_len)
    nq = s_len // t
    n = len(shards)

    def body(qa_ref, ka_ref, vb_ref, *rest):
        o_ref, lse_ref = rest[n:n + 2]
        start, finish = _gather_plan(rest[:n], rest[n + 2:2 * n + 2], *rest[2 * n + 2:])
        i = pl.program_id(1)
        pl.when((pl.program_id(0) == 0) & (i == 0))(start)
        row = lax.broadcasted_iota(jnp.int32, (t, t), 0)
        col = lax.broadcasted_iota(jnp.int32, (t, t), 1)
        lo = lax.broadcasted_iota(jnp.int32, (t, LANES), 1) < HEAD_DIM
        qs = (qa_ref[0], qa_ref[1])

        def block(j, carry, masked):
            rows = pl.ds(pl.multiple_of(j * t, t), t)
            v = vb_ref[rows, :]
            new = []
            for hh in range(2):
                m, l, acc = carry[hh]
                s = _dot(qs[hh], ka_ref[hh, rows, :], NT)
                if masked:
                    s = jnp.where(row >= col, s, NEG)
                m_new = jnp.maximum(m, jnp.max(s, axis=1, keepdims=True))
                alpha = jnp.exp(m - m_new)
                p = jnp.exp(s - m_new)
                new.append((m_new, alpha * l + jnp.sum(p, axis=1, keepdims=True), alpha * acc + _dot(p.astype(BF16), v)))
            return tuple(new)

        init = (jnp.full((t, 1), NEG, F32), jnp.zeros((t, 1), F32), jnp.zeros((t, LANES), F32))
        carry = lax.fori_loop(0, i, functools.partial(block, masked=False), (init, init))
        (m0, l0, acc0), (m1, l1, acc1) = block(i, carry, True)
        o_ref[...] = jnp.where(lo, acc0 / l0, acc1 / l1)
        lse_ref[...] = jnp.where(lo, m0 + jnp.log(l0), m1 + jnp.log(l1))
        pl.when((pl.program_id(0) == N_PAIRS - 1) & (i == nq - 1))(finish)

    out = pl.BlockSpec((t, LANES), lambda p, i: (i, p))
    res = pl.pallas_call(
        body, name="attn_fwd", grid=(N_PAIRS, nq),
        in_specs=[pl.BlockSpec((2, t, LANES), lambda p, i: (p, i, 0)), pl.BlockSpec((2, s_len, LANES), lambda p, i: (p, 0, 0)),
                  pl.BlockSpec((s_len, LANES), lambda p, i: (0, p))] + [ANY] * n,
        out_specs=[out, out] + [ANY] * n,
        out_shape=[jax.ShapeDtypeStruct((s_len, D_MODEL), F32), jax.ShapeDtypeStruct((s_len, D_MODEL), F32)]
        + [jax.ShapeDtypeStruct((N_CHIPS, *h.shape), h.dtype) for h in shards],
        scratch_shapes=_exchange_sems(n),
        compiler_params=_params(("arbitrary", "arbitrary")),
    )(qa, ka, vb, *shards)
    return res[0], res[1], res[2:]


def _attn_bwd(qa, ka, vb, o, lse, do, parts, dproj):
    s_len = vb.shape[0]
    t = min(ATT_T, s_len)
    nq = s_len // t
    n = len(parts)

    def body(qa_ref, ka_ref, vb_ref, o_ref, lse_ref, do_ref, *rest):
        dqa_ref, dka_ref, dv_ref = rest[n + 1:n + 4]
        start, finish = _reduce_plan(rest[:n], rest[n + 4:2 * n + 4], *rest[2 * n + 4:])
        j = pl.program_id(1)
        pl.when((pl.program_id(0) == 0) & (j == 0))(start)

        @pl.when(j == 0)
        def _():
            dqa_ref[...] = jnp.zeros_like(dqa_ref)

        row = lax.broadcasted_iota(jnp.int32, (t, t), 0)
        col = lax.broadcasted_iota(jnp.int32, (t, t), 1)
        lo = lax.broadcasted_iota(jnp.int32, (t, LANES), 1) < HEAD_DIM
        v = vb_ref[...]
        ks = (ka_ref[0], ka_ref[1])

        def block(i, carry, masked):
            dk, dv = list(carry[:2]), carry[2]
            rows = pl.ds(pl.multiple_of(i * t, t), t)
            do_p, o_p, lse_p = do_ref[rows, :], o_ref[rows, :], lse_ref[rows, :]
            for hh in range(2):
                q = qa_ref[hh, rows, :]
                do_h = jnp.where(lo == (hh == 0), do_p, 0.0)
                delta = jnp.sum(do_h * o_p, axis=1, keepdims=True)
                s = _dot(q, ks[hh], NT)
                if masked:
                    s = jnp.where(row >= col, s, NEG)
                p = jnp.exp(s - lse_p[:, hh * HEAD_DIM:hh * HEAD_DIM + 1])
                do_b = do_h.astype(BF16)
                ds = (p * (_dot(do_b, v, NT) - delta)).astype(BF16)
                dv = dv + _dot(p.astype(BF16), do_b, TN)
                dk[hh] = dk[hh] + _dot(ds, q, TN)
                dqa_ref[hh, rows, :] += _dot(ds, ks[hh])
            return dk[0], dk[1], dv

        zero = jnp.zeros((t, LANES), F32)
        carry = block(j, (zero, zero, zero), True)
        dk0, dk1, dv = lax.fori_loop(j + 1, nq, functools.partial(block, masked=False), carry)
        dka_ref[0] = dk0
        dka_ref[1] = dk1
        dv_ref[...] = dv.astype(dv_ref.dtype)
        pl.when((pl.program_id(0) == N_PAIRS - 1) & (j == nq - 1))(finish)

    whole_pair = pl.BlockSpec((2, s_len, LANES), lambda p, j: (p, 0, 0))
    blk_pair = pl.BlockSpec((2, t, LANES), lambda p, j: (p, j, 0))
    whole_cols = pl.BlockSpec((s_len, LANES), lambda p, j: (0, p))
    blk_cols = pl.BlockSpec((t, LANES), lambda p, j: (j, p))
    res = pl.pallas_call(
        body, name="attn_bwd", grid=(N_PAIRS, nq),
        in_specs=[whole_pair, blk_pair, blk_cols, whole_cols, whole_cols, whole_cols] + [ANY] * (n + 1),
        out_specs=[whole_pair, blk_pair, pl.BlockSpec((t, LANES), lambda p, j: (j, AL_V // LANES + p))] + [ANY] * n,
        out_shape=[jax.ShapeDtypeStruct((N_HEADS, s_len, LANES), F32), jax.ShapeDtypeStruct((N_HEADS, s_len, LANES), F32),
                   jax.ShapeDtypeStruct(dproj.shape, dproj.dtype)]
        + [jax.ShapeDtypeStruct((N_DEV, g.shape[1] // 2, g.shape[2]), g.dtype) for g in parts],
        scratch_shapes=_exchange_sems(n), input_output_aliases={6 + n: 2},
        compiler_params=_params(("arbitrary", "arbitrary")),
    )(qa, ka, vb, o, lse, do, *parts, dproj)
    return res[0], res[1], res[2], res[3:]


def _attn_post(dqa, dka, dproj):
    s_len = dqa.shape[1]
    tr = min(256, s_len)
    assert AL_K == AL_Q + D_MODEL and AL_Q % (2 * D_MODEL) == 0

    def body(dqa_ref, dka_ref, _, dqk_ref, dcum_ref):
        lane = lax.broadcasted_iota(jnp.int32, (tr, LANES), 1)
        lo = lane < HEAD_DIM
        dcum = jnp.zeros((tr, LANES), F32)
        for p in range(N_PAIRS):
            a0, a1, b0, b1 = dqa_ref[2 * p], dqa_ref[2 * p + 1], dka_ref[2 * p], dka_ref[2 * p + 1]
            dq = jnp.where(lo, a0, a1) * (HEAD_DIM ** -0.5)
            dqk_ref[:, p * LANES:(p + 1) * LANES] = dq.astype(dqk_ref.dtype)
            dqk_ref[:, D_MODEL + p * LANES:D_MODEL + (p + 1) * LANES] = jnp.where(lo, b0, b1).astype(dqk_ref.dtype)
            for hh, (a, b) in enumerate(((a0, b0), (a1, b1))):
                base = _aug_base(hh)
                dcum = dcum + jnp.where(lane == 2 * p + hh, a[:, base:base + 1] - b[:, base + 3:base + 4], 0.0)
        dcum_ref[...] = dcum

    heads = pl.BlockSpec((N_HEADS, tr, LANES), lambda i: (0, i, 0))
    return pl.pallas_call(
        body, name="attn_post", grid=(s_len // tr,),
        in_specs=[heads, heads, ANY],
        out_specs=[pl.BlockSpec((tr, 2 * D_MODEL), lambda i: (i, AL_Q // (2 * D_MODEL))), pl.BlockSpec((tr, LANES), lambda i: (i, 0))],
        out_shape=[jax.ShapeDtypeStruct(dproj.shape, dproj.dtype), jax.ShapeDtypeStruct((s_len, LANES), F32)],
        input_output_aliases={2: 0}, compiler_params=_params(("parallel",)),
    )(dqa, dka, dproj)


def _ln_stats(r):
    mu = _rowmean(r)
    xc = r - mu
    rstd = lax.rsqrt(_rowmean(xc * xc) + LN_EPS)
    return xc * rstd, rstd


def _ln_bwd(dxh, xh, rstd):
    return rstd * (dxh - _rowmean(dxh) - xh * _rowmean(dxh * xh))


def _rms_bwd(dgn, g, r):
    return r * dgn - (r * r * r) * g * _rowmean(dgn * g)


def _to_aligned(wt):
    out = jnp.zeros((AL_COLS, wt.shape[1]), wt.dtype)
    for dst, (lo, hi) in ((0, (0, 2048)), (AL_Q, (2576, 5648)), (AL_B, (2048, 2560)), (AL_DTF, (2560, 2576)), (AL_DTF + 16, (5648, 5664))):
        out = lax.dynamic_update_slice_in_dim(out, wt[lo:hi], dst, axis=0)
    return out


def _from_aligned(gt):
    out = jnp.zeros((IN_COLS, gt.shape[1]), gt.dtype)
    for dst, (lo, hi) in ((0, (0, AL_Q)), (2048, (AL_B, AL_DTF)), (2560, (AL_DTF, AL_DTF + 16)), (2576, (AL_Q, AL_B)),
                          (5648, (AL_DTF + 16, AL_DTF + 32))):
        out = lax.dynamic_update_slice_in_dim(out, gt[lo:hi], dst, axis=0)
    return out


def _lanes(v, at=0):
    return jnp.pad(v, ((0, 0), (at, LANES - at - v.shape[1])))


def _local_step(x, tgt, mod, w_alt, shards, sp):
    d = D_MODEL
    sh1, sc1, g1, sh2, sc2, g2 = [mod[:, i * d:(i + 1) * d] for i in range(6)]
    dt_bias_l, a_log_l, f_bias_l = _lanes(sp["dt_bias"]), _lanes(sp["a_log"]), _lanes(sp["f_bias"], F_LANE)
    d_exp = jnp.repeat(sp["d_skip"], HEAD_DIM, axis=1)

    (h1,), _ = _rowwise("mod1", lambda x, sc, sh: ([x * (1.0 + sc) + sh], []), [x], [sc1, sh1], [(d, BF16)], [])
    proj = _matmul("proj", h1, w_alt, dims=NT, tn=1152)
    xc_all, y_ssd, prevs, y_mix = _ssd_fwd(proj, sp["conv_w"], sp["conv_b"], dt_bias_l, a_log_l, d_exp, sp["ssm_norm_w"])
    cum = _fox_cum(proj, f_bias_l)
    qa, ka, vb = _attn_prep(proj, cum)
    o, lse, (g_out, g_fi, g_fo) = _attn_fwd(qa, ka, vb, shards)
    w_out = g_out.reshape(2 * d, d)
    w_fi = g_fi.transpose(1, 0, 2).reshape(d, D_FF)
    w_fo = g_fo.reshape(D_FF, d)
    (y_mix,), _ = _rowwise("attn_norm", lambda o, w: ([o * lax.rsqrt(_rowmean(o * o) + RMS_EPS) * w], []),
                           [o], [sp["attn_norm_w"]], [(d, BF16, (2 * d, 1, y_mix))], [])
    def ln1_fwd(y, x, g1, sc2, sh2, lg, lb):
        r1 = ALPHA * x + (1.0 + g1) * y
        xh, _ = _ln_stats(r1)
        x1 = xh * lg + lb
        h2 = x1 * (1.0 + sc2) + sh2
        return [y, r1, h2, h2.T], []

    def relu2(u):
        a = jnp.square(jnp.maximum(u, 0.0))
        return [a, a.T], []

    y, r1, h2, h2_t = _matmul("out_proj", y_mix, w_out, tm=512, tk=2048,
                              epi=(ln1_fwd, [x], [g1, sc2, sh2, sp["ln1_g"], sp["ln1_b"]], [F32, F32, BF16, ("T", BF16)], []))
    act, act_t = _matmul("ff_in", h2, w_fi, epi=(relu2, [], [], [BF16, ("T", BF16)], []))

    def head(ff, r1, tgt, g2, l1g, l1b, l2g, l2b):
        xh1, _ = _ln_stats(r1)
        x1 = xh1 * l1g + l1b
        xh2, rstd2 = _ln_stats(ALPHA * x1 + (1.0 + g2) * ff)
        err = xh2 * l2g + l2b - tgt
        loss = 0.5 * jnp.sum(_rowmean(err * err))
        dx2 = err * (1.0 / d)
        dr2 = _ln_bwd(dx2 * l2g, xh2, rstd2)
        return ([dr2, (1.0 + g2) * dr2],
                [_colsum(dx2 * xh2), _colsum(dx2), _colsum(dr2 * ff), jnp.full((1, LANES), loss, F32)])

    dr2, dff, d_ln2_g, d_ln2_b, d_g2, loss = _matmul(
        "ff_out", act, w_fo, tm=512, tk=D_FF,
        epi=(head, [r1, tgt], [g2, sp["ln1_g"], sp["ln1_b"], sp["ln2_g"], sp["ln2_b"]], [F32, BF16], [d, d, d, LANES]))
    du = _matmul("d_act", dff, w_fo, dims=NT, epi=(lambda da, act: ([da * (2.0 * jnp.sqrt(act.astype(F32)))], []), [act], [], [BF16], []))
    dw_fo = _matmul("dw_ff_out", act_t, dff, tk=SEQ_TK, out_dtype=BF16, by_chip="rows")
    dw_fi = _matmul("dw_ff_in", h2_t, du, tk=SEQ_TK, out_dtype=BF16, by_chip="cols")

    def ln1_bwd(dh2, r1, dr2, y, sc2, g1, lg, lb):
        xh, rstd = _ln_stats(r1)
        x1 = xh * lg + lb
        dx1 = ALPHA * dr2 + dh2 * (1.0 + sc2)
        dr1 = _ln_bwd(dx1 * lg, xh, rstd)
        return ([dr1, (1.0 + g1) * dr1],
                [_colsum(dh2 * x1), _colsum(dh2), _colsum(dx1 * xh), _colsum(dx1), _colsum(dr1 * y)])

    dr1, dy, d_sc2, d_sh2, d_ln1_g, d_ln1_b, d_g1 = _matmul(
        "dh2", du, w_fi, dims=NT, tm=512, tk=D_FF,
        epi=(ln1_bwd, [r1, dr2, y], [sc2, g1, sp["ln1_g"], sp["ln1_b"]], [F32, BF16], [d] * 5))
    dw_out = _matmul("dw_out", y_mix, dy, dims=TN, tk=SEQ_TK, out_dtype=BF16, by_chip="rows")

    def attn_norm_bwd(dyo, o, w):
        r = lax.rsqrt(_rowmean(o * o) + RMS_EPS)
        return [_rms_bwd(dyo * w, o, r)], [_colsum(dyo * o * r)]

    dymix = _matmul("dy_mix_ssm", dy, w_out[:d], dims=NT)
    do, d_attn_w = _matmul("dy_mix_att", dy, w_out[d:], dims=NT, tm=512, epi=(attn_norm_bwd, [o], [sp["attn_norm_w"]], [F32], [d]))

    dxc, ddt_tile, d_alog_l, d_dexp, d_dtb_l, dproj, d_ssm_w = _ssd_bwd(
        xc_all, proj, dt_bias_l, a_log_l, d_exp, prevs, y_ssd, dymix, sp["ssm_norm_w"])
    dqa, dka, dproj, landed = _attn_bwd(qa, ka, vb, o, lse, do, [dw_out, dw_fi, dw_fo], dproj)
    dproj, dcum = _attn_post(dqa, dka, dproj)
    dproj, d_fb_l = _fox_cum_bwd(dcum, proj, f_bias_l, ddt_tile, dproj)
    dpre, d_conv_w, d_conv_b = _conv_bwd_pre(proj, sp["conv_w"], sp["conv_b"], dxc)
    dproj = _conv_bwd_in(dpre, sp["conv_w"], dproj)
    dw_alt = _matmul("dw_in", dproj, h1, dims=TN, tm=1152, tk=SEQ_TK, out_dtype=BF16)
    part_in = _from_aligned(dw_alt).reshape(N_CHIPS, IN_COLS // N_CHIPS, d)

    def last(dh1, x, dr1, sc1):
        return [ALPHA * dr1 + dh1 * (1.0 + sc1)], [_colsum(dh1 * x), _colsum(dh1)]

    chip_in = _pair_sum(part_in, _pair_exchange(part_in), lax.axis_index("c"))
    dx, d_sc1, d_sh1, landed_in = _matmul("dh1", dproj, w_alt, tm=512, tk=AL_COLS, carry=[chip_in],
                                          epi=(last, [x, dr1], [sc1], [F32], [d, d]))

    small = {
        "mod": jnp.concatenate([d_sh1, d_sc1, d_g1, d_sh2, d_sc2, d_g2], axis=1),
        "conv_w": d_conv_w, "conv_b": d_conv_b,
        "dt_bias": d_dtb_l[:, :N_HEADS], "a_log": d_alog_l[:, :N_HEADS],
        "d_skip": jnp.sum(d_dexp.reshape(N_HEADS, HEAD_DIM), axis=1)[None, :],
        "ssm_norm_w": d_ssm_w, "f_bias": d_fb_l[:, F_LANE:F_LANE + N_HEADS], "attn_norm_w": d_attn_w,
        "ln1_g": d_ln1_g, "ln1_b": d_ln1_b, "ln2_g": d_ln2_g, "ln2_b": d_ln2_b, "loss": loss,
    }
    return dx, [landed_in, *landed], small


N_DEV = 8
N_CHIPS = 4
ANY = pl.BlockSpec(memory_space=pl.ANY)
VMEM_SPEC = pl.BlockSpec(memory_space=pltpu.VMEM)


def _place():
    x, y, c = lax.axis_index("x"), lax.axis_index("y"), lax.axis_index("c")
    return x, y, c


def _other_chips(x, y):
    return [(1 - x, y, 2 * (1 - x) + y), (x, 1 - y, 2 * x + 1 - y), (1 - x, 1 - y, 2 * (1 - x) + 1 - y)]


def _small_gather(v_ref, out_ref, send_sems, recv_sems, local_sem, after_start=None):
    x, y, c = _place()
    me = 4 * x + 2 * y + c
    mine = pltpu.make_async_copy(v_ref, out_ref.at[me], local_sem)
    mine.start()
    peers = _peers(x, y, c)

    def copy(rel, slot, to):
        return pltpu.make_async_remote_copy(src_ref=v_ref, dst_ref=out_ref.at[slot], send_sem=send_sems.at[rel],
                                            recv_sem=recv_sems.at[rel], device_id=to, device_id_type=MESH)

    sends = [copy(rel, me, peer) for rel, peer in enumerate(peers)]
    for cp in sends:
        cp.start()
    if after_start is not None:
        after_start()
    for rel, (px, py, pc) in enumerate(peers):
        copy(rel, 4 * px + 2 * py + pc, (x, y, c)).wait_recv()
    for cp in sends:
        cp.wait_send()
    mine.wait()


SMALL_GATHER_SEMS = [pltpu.SemaphoreType.DMA((N_DEV - 1,)), pltpu.SemaphoreType.DMA((N_DEV - 1,)), pltpu.SemaphoreType.DMA]


def _allgather_small(name, v):
    def body(v_ref, out_ref, *sems):
        _small_gather(v_ref, out_ref, *sems)

    return pl.pallas_call(
        body, name=name, out_shape=jax.ShapeDtypeStruct((N_DEV, *v.shape), v.dtype),
        in_specs=[VMEM_SPEC], out_specs=VMEM_SPEC, scratch_shapes=SMALL_GATHER_SEMS,
    )(v)


def _tail_exchange(vec, halves):
    n = len(halves)

    def body(v_ref, *rest):
        ins, (every_ref, total_ref), outs = rest[:n], rest[n:n + 2], rest[n + 2:2 * n + 2]
        gather_sems, (swap_send, swap_recv) = rest[2 * n + 2:2 * n + 5], rest[2 * n + 5:]
        x, y, c = _place()
        swaps = [pltpu.make_async_remote_copy(src_ref=ins[w], dst_ref=outs[w], send_sem=swap_send.at[w], recv_sem=swap_recv.at[w],
                                              device_id=(x, y, 1 - c), device_id_type=MESH) for w in range(n)]

        def start_swaps():
            for cp in swaps:
                cp.start()

        _small_gather(v_ref, every_ref, *gather_sems, after_start=start_swaps)
        acc = every_ref[0]
        for dev in range(1, N_DEV):
            acc = acc + every_ref[dev]
        total_ref[...] = acc
        for cp in swaps:
            cp.wait_recv()
        for cp in swaps:
            cp.wait_send()

    res = pl.pallas_call(
        body, name="tail_exchange",
        out_shape=[jax.ShapeDtypeStruct((N_DEV, *vec.shape), vec.dtype), jax.ShapeDtypeStruct(vec.shape, vec.dtype)]
        + [jax.ShapeDtypeStruct(h.shape, h.dtype) for h in halves],
        in_specs=[VMEM_SPEC] + [ANY] * n, out_specs=[VMEM_SPEC, VMEM_SPEC] + [ANY] * n,
        scratch_shapes=SMALL_GATHER_SEMS + [pltpu.SemaphoreType.DMA((n,)), pltpu.SemaphoreType.DMA((n,))],
    )(vec, *halves)
    return res[0], res[1], res[2:]


def _gather_shards(shard):
    def body(in_ref, out_ref, stage, send_sems, recv_sems, local_sems):
        start, finish = _shard_gather_plan(in_ref, out_ref, stage, send_sems, recv_sems, local_sems)
        start()
        finish()

    return pl.pallas_call(
        body, name="gather_w_in", out_shape=jax.ShapeDtypeStruct((N_CHIPS, *shard.shape), shard.dtype),
        in_specs=[ANY], out_specs=ANY,
        scratch_shapes=[pltpu.VMEM(shard.shape, shard.dtype), pltpu.SemaphoreType.DMA((6,)), pltpu.SemaphoreType.DMA((6,)),
                        pltpu.SemaphoreType.DMA((2,))],
        compiler_params=_params(),
    )(shard)


def _shard_gather_plan(in_ref, out_ref, stage, send_sems, recv_sems, local_sems):
    ch = in_ref.shape[1] // 2
    x, y, c = _place()
    k_me = 2 * x + y
    me, sibling = (x, y, c), (x, y, 1 - c)
    chips = _other_chips(x, y)

    def copy(idx, k, half, to, src=None):
        cols = out_ref.at[k, :, pl.ds(pl.multiple_of(half * ch, ch), ch)]
        return pltpu.make_async_remote_copy(src_ref=cols if src is None else src, dst_ref=cols, send_sem=send_sems.at[idx],
                                            recv_sem=recv_sems.at[idx], device_id=to, device_id_type=MESH)

    mine = in_ref.at[:, pl.ds(pl.multiple_of(c * ch, ch), ch)]
    sends = [copy(j, k_me, c, (cx, cy, c), src=mine) for j, (cx, cy, _) in enumerate(chips)]
    load = pltpu.make_async_copy(in_ref, stage, local_sems.at[0])
    store = pltpu.make_async_copy(stage, out_ref.at[k_me], local_sems.at[1])

    def start():
        for cp in sends:
            cp.start()
        load.start()

    def finish():
        load.wait()
        store.start()
        forwards = []
        for j, (_, _, kj) in enumerate(chips):
            copy(j, kj, c, me).wait_recv()
            forwards.append(copy(3 + j, kj, c, sibling))
            forwards[-1].start()
        for j, (_, _, kj) in enumerate(chips):
            copy(3 + j, kj, 1 - c, me).wait_recv()
        for cp in sends + forwards:
            cp.wait_send()
        store.wait()

    return start, finish


def _peers(x, y, c):
    return [((1 - x) if rel & 4 else x, (1 - y) if rel & 2 else y, (1 - c) if rel & 1 else c) for rel in range(1, N_DEV)]


def _exchange_sems(n):
    return [pltpu.SemaphoreType.DMA((n, N_DEV - 1)), pltpu.SemaphoreType.DMA((n, N_DEV - 1)), pltpu.SemaphoreType.DMA((n,))]


def _gather_plan(ins, outs, send_sems, recv_sems, local_sems):
    x, y, c = _place()
    k_me = 2 * x + y
    peers = [(rel, p) for rel, p in enumerate(_peers(x, y, c)) if (rel + 1) & 6]

    def copy(w, rel, k, half, to, src=None):
        rh = ins[w].shape[0] // 2
        rows = outs[w].at[k, pl.ds(pl.multiple_of(half * rh, rh), rh), :]
        return pltpu.make_async_remote_copy(src_ref=rows if src is None else src, dst_ref=rows, send_sem=send_sems.at[w, rel],
                                            recv_sem=recv_sems.at[w, rel], device_id=to, device_id_type=MESH)

    def mine(w):
        rh = ins[w].shape[0] // 2
        return ins[w].at[pl.ds(pl.multiple_of(c * rh, rh), rh), :]

    n = len(ins)
    local = [pltpu.make_async_copy(ins[w], outs[w].at[k_me], local_sems.at[w]) for w in range(n)]
    sends = [copy(w, rel, k_me, c, peer, src=mine(w)) for w in range(n) for rel, peer in peers]

    def start():
        for cp in local + sends:
            cp.start()

    def finish():
        for w in range(n):
            for rel, (px, py, pc) in peers:
                copy(w, rel, 2 * px + py, pc, (x, y, c)).wait_recv()
        for cp in sends:
            cp.wait_send()
        for cp in local:
            cp.wait()

    return start, finish


def _reduce_plan(ins, outs, send_sems, recv_sems, local_sems):
    x, y, c = _place()
    me = 4 * x + 2 * y + c
    peers = _peers(x, y, c)

    def block(w, k, half):
        rh = ins[w].shape[1] // 2
        return ins[w].at[k, pl.ds(pl.multiple_of(half * rh, rh), rh), :]

    def copy(w, rel, src, slot, to):
        return pltpu.make_async_remote_copy(src_ref=src, dst_ref=outs[w].at[slot], send_sem=send_sems.at[w, rel],
                                            recv_sem=recv_sems.at[w, rel], device_id=to, device_id_type=MESH)

    n = len(ins)
    local = [pltpu.make_async_copy(block(w, 2 * x + y, c), outs[w].at[me], local_sems.at[w]) for w in range(n)]
    sends = [copy(w, rel, block(w, 2 * px + py, pc), me, (px, py, pc)) for w in range(n) for rel, (px, py, pc) in enumerate(peers)]

    def start():
        for cp in local + sends:
            cp.start()

    def finish():
        for w in range(n):
            for rel, (px, py, pc) in enumerate(peers):
                copy(w, rel, block(w, 2 * x + y, c), 4 * px + 2 * py + pc, (x, y, c)).wait_recv()
        for cp in sends:
            cp.wait_send()
        for cp in local:
            cp.wait()

    return start, finish


def _scatter_plan(ins, outs, send_sems, recv_sems, local_sems):
    x, y, c = _place()
    k_me = 2 * x + y
    chips = _other_chips(x, y)

    def copy(w, j, src_k, dst_k, to):
        return pltpu.make_async_remote_copy(src_ref=ins[w].at[src_k], dst_ref=outs[w].at[dst_k], send_sem=send_sems.at[w, j],
                                            recv_sem=recv_sems.at[w, j], device_id=to, device_id_type=MESH)

    n = len(ins)
    local = [pltpu.make_async_copy(ins[w].at[k_me], outs[w].at[k_me], local_sems.at[w]) for w in range(n)]
    sends = [copy(w, j, kj, k_me, (cx, cy, c)) for w in range(n) for j, (cx, cy, kj) in enumerate(chips)]

    def start():
        for cp in local + sends:
            cp.start()

    def finish():
        for w in range(n):
            for j, (_, _, kj) in enumerate(chips):
                copy(w, j, k_me, kj, (x, y, c)).wait_recv()
        for cp in sends:
            cp.wait_send()
        for cp in local:
            cp.wait()

    return start, finish


def _row_tile(r, mult=2 * SUBLANES):
    if r % 256 == 0:
        return 256
    return max([t for t in range(mult, 513, mult) if r % t == 0], default=r)


def _pair_exchange(g):
    _, r, cdim = g.shape
    ch = cdim // 2

    def body(g_ref, got_ref, send_sem, recv_sem):
        x, y, c = _place()
        cp = pltpu.make_async_remote_copy(src_ref=g_ref.at[:, :, pl.ds(pl.multiple_of((1 - c) * ch, ch), ch)], dst_ref=got_ref,
                                          send_sem=send_sem, recv_sem=recv_sem, device_id=(x, y, 1 - c), device_id_type=MESH)
        cp.start()
        cp.wait_recv()
        cp.wait_send()

    return pl.pallas_call(
        body, name="pair_exchange", out_shape=jax.ShapeDtypeStruct((N_CHIPS, r, ch), g.dtype),
        in_specs=[ANY], out_specs=ANY, scratch_shapes=[pltpu.SemaphoreType.DMA, pltpu.SemaphoreType.DMA],
    )(g)


def _pair_sum(g, got, c):
    _, r, cdim = g.shape
    ch = cdim // 2
    tr = _row_tile(r)

    def body(c_ref, g_ref, got_ref, o_ref):
        o_ref[...] = (g_ref[...].astype(F32) + got_ref[...].astype(F32)).astype(o_ref.dtype)

    blk = pl.BlockSpec((1, tr, ch), lambda k, i, c_ref: (k, i, 0))
    return pl.pallas_call(
        body, name="pair_sum",
        grid_spec=pltpu.PrefetchScalarGridSpec(
            num_scalar_prefetch=1, grid=(N_CHIPS, r // tr),
            in_specs=[pl.BlockSpec((1, tr, ch), lambda k, i, c_ref: (k, i, c_ref[0])), blk], out_specs=blk),
        out_shape=jax.ShapeDtypeStruct((N_CHIPS, r, ch), BF16),
        compiler_params=_params(("parallel", "parallel")),
    )(jnp.reshape(c, (1,)).astype(jnp.int32), g, got)


def _sum_blocks(name, parts):
    k, r, cdim = parts.shape
    tr = _row_tile(r)

    def body(p_ref, o_ref):
        acc = p_ref[0].astype(F32)
        for i in range(1, k):
            acc = acc + p_ref[i].astype(F32)
        o_ref[...] = acc

    return pl.pallas_call(
        body, name=name, grid=(r // tr,),
        in_specs=[pl.BlockSpec((k, tr, cdim), lambda i: (0, i, 0))], out_specs=pl.BlockSpec((tr, cdim), lambda i: (i, 0)),
        out_shape=jax.ShapeDtypeStruct((r, cdim), F32), compiler_params=_params(("parallel",)),
    )(parts)


ADA_SHARD = 6 * D_MODEL // N_CHIPS


def _mod_part(c_all, w_shard, b_shard):
    tn = 512

    def body(c_ref, w_ref, b_ref, o_ref):
        o_ref[...] = _dot(_silu(c_ref[...]).astype(BF16), w_ref[...].astype(BF16)) + b_ref[...]

    return pl.pallas_call(
        body, name="mod_part", grid=(ADA_SHARD // tn,),
        in_specs=[pl.BlockSpec((N_DEV, D_MODEL), lambda j: (0, 0)), pl.BlockSpec((D_MODEL, tn), lambda j: (0, j)),
                  pl.BlockSpec((1, tn), lambda j: (0, j))],
        out_specs=pl.BlockSpec((N_DEV, tn), lambda j: (0, j)),
        out_shape=jax.ShapeDtypeStruct((N_DEV, ADA_SHARD), F32), compiler_params=_params(("parallel",)),
    )(c_all, w_shard, b_shard)


def _w_ada_grad(c_all_t, dmod_shard):
    tm = 256

    def body(ct_ref, dm_ref, o_ref):
        act = _silu(ct_ref[...])
        acc = act[:, 0:1] * dm_ref[0:1, :]
        for dev in range(1, N_DEV):
            acc = acc + act[:, dev:dev + 1] * dm_ref[dev:dev + 1, :]
        o_ref[...] = acc

    return pl.pallas_call(
        body, name="w_ada_grad", grid=(D_MODEL // tm,),
        in_specs=[pl.BlockSpec((tm, N_DEV), lambda i: (i, 0)), pl.BlockSpec((N_DEV, ADA_SHARD), lambda i: (0, 0))],
        out_specs=pl.BlockSpec((tm, ADA_SHARD), lambda i: (i, 0)),
        out_shape=jax.ShapeDtypeStruct((D_MODEL, ADA_SHARD), F32), compiler_params=_params(("parallel",)),
    )(c_all_t, dmod_shard)


def _adamw_math(w, g, m, v):
    nm = ADAM_B1 * m + (1.0 - ADAM_B1) * g
    nv = ADAM_B2 * v + (1.0 - ADAM_B2) * jnp.square(g)
    m_hat = nm / (1.0 - ADAM_B1 ** ADAM_STEP)
    v_hat = nv / (1.0 - ADAM_B2 ** ADAM_STEP)
    return -ADAM_LR * (m_hat / (jnp.sqrt(v_hat) + ADAM_EPS) + ADAM_WD * w), nm, nv


def _adamw(name, w, g, m, v):
    _, r, cdim = w.shape
    tr = 256 if r % 256 == 0 else r

    def body(w_ref, g_ref, m_ref, v_ref, go_ref, d_ref, nm_ref, nv_ref):
        go_ref[...] = g_ref[...]
        d_ref[...], nm_ref[...], nv_ref[...] = _adamw_math(w_ref[...], g_ref[...], m_ref[...], v_ref[...])

    blk = pl.BlockSpec((None, tr, cdim), lambda i: (0, i, 0))
    return pl.pallas_call(
        body, name=name, grid=(r // tr,), in_specs=[blk, pl.BlockSpec((tr, cdim), lambda i: (i, 0)), blk, blk], out_specs=[blk] * 4,
        out_shape=[jax.ShapeDtypeStruct((1, r, cdim), F32)] * 4, compiler_params=_params(("parallel",)),
    )(w, g, m, v)


def _adamw_pair(name, w, mine, other, m, v, c, by_cols=False):
    _, r, cdim = w.shape
    hr, hc = mine.shape
    tr = _row_tile(hr, SUBLANES)
    per = hr // tr

    def body(c_ref, w_ref, a_ref, b_ref, m_ref, v_ref, g_ref, d_ref, nm_ref, nv_ref):
        half = pl.program_id(1) if by_cols else pl.program_id(0) // per
        g = jnp.where(half == c_ref[0], a_ref[...], b_ref[...])
        g_ref[...] = g
        d_ref[...], nm_ref[...], nv_ref[...] = _adamw_math(w_ref[...], g, m_ref[...], v_ref[...])

    blk = pl.BlockSpec((None, tr, hc), lambda i, j, c_ref: (0, i, j))
    half = pl.BlockSpec((tr, hc), lambda i, j, c_ref: (i % per, 0))
    return pl.pallas_call(
        body, name=name,
        grid_spec=pltpu.PrefetchScalarGridSpec(num_scalar_prefetch=1, grid=(r // tr, cdim // hc),
                                               in_specs=[blk, half, half, blk, blk], out_specs=[blk] * 4),
        out_shape=[jax.ShapeDtypeStruct((1, r, cdim), F32)] * 4, compiler_params=_params(("parallel", "parallel")),
    )(jnp.reshape(c, (1,)).astype(jnp.int32), w, mine, other, m, v)


SMALL = ["b_ada", "conv_b", "dt_bias", "a_log", "d_skip", "ssm_norm_w", "f_bias", "attn_norm_w", "ln1_g", "ln1_b", "ln2_g", "ln2_b"]


def _pack(vs):
    pieces = []
    for v in vs:
        pieces.append(v)
        if v.shape[1] % LANES:
            pieces.append(jnp.zeros((1, -v.shape[1] % LANES), v.dtype))
    return jnp.concatenate(pieces, axis=1)


def _adamw_small(total, offs, ws, ms, vs):
    n = len(ws)

    def body(*refs):
        t_ref, outs = refs[0], refs[1 + 3 * n:]
        for i in range(n):
            g = t_ref[:, offs[i]:offs[i] + ws[i].shape[1]]
            dl, nm, nv = _adamw_math(refs[1 + i][...], g, refs[1 + n + i][...], refs[1 + 2 * n + i][...])
            outs[4 * i][...], outs[4 * i + 1][...], outs[4 * i + 2][...], outs[4 * i + 3][...] = g, dl, nm, nv

    res = pl.pallas_call(
        body, name="adamw_small", in_specs=[VMEM_SPEC] * (1 + 3 * n), out_specs=[VMEM_SPEC] * (4 * n),
        out_shape=[jax.ShapeDtypeStruct(w.shape, F32) for w in ws for _ in range(4)],
    )(total, *ws, *ms, *vs)
    return [res[4 * i:4 * i + 4] for i in range(n)]


def kernel(x, c, w_ada, b_ada, w_in, conv_w, conv_b, dt_bias, a_log, d_skip, ssm_norm_w, f_bias, attn_norm_w, w_out, ln1_g, ln1_b, w_ff_in, w_ff_out, ln2_g, ln2_b, loss_target, m_w_ada, m_b_ada, m_w_in, m_conv_w, m_conv_b, m_dt_bias, m_a_log, m_d_skip, m_ssm_norm_w, m_f_bias, m_attn_norm_w, m_w_out, m_ln1_g, m_ln1_b, m_w_ff_in, m_w_ff_out, m_ln2_g, m_ln2_b, v_w_ada, v_b_ada, v_w_in, v_conv_w, v_conv_b, v_dt_bias, v_a_log, v_d_skip, v_ssm_norm_w, v_f_bias, v_attn_norm_w, v_w_out, v_ln1_g, v_ln1_b, v_w_ff_in, v_w_ff_out, v_ln2_g, v_ln2_b):
    a = dict(b_ada=b_ada, conv_b=conv_b, dt_bias=dt_bias, a_log=a_log, d_skip=d_skip, ssm_norm_w=ssm_norm_w, f_bias=f_bias,
             attn_norm_w=attn_norm_w, ln1_g=ln1_g, ln1_b=ln1_b, ln2_g=ln2_g, ln2_b=ln2_b)
    ms = dict(b_ada=m_b_ada, conv_b=m_conv_b, dt_bias=m_dt_bias, a_log=m_a_log, d_skip=m_d_skip, ssm_norm_w=m_ssm_norm_w,
              f_bias=m_f_bias, attn_norm_w=m_attn_norm_w, ln1_g=m_ln1_g, ln1_b=m_ln1_b, ln2_g=m_ln2_g, ln2_b=m_ln2_b)
    vs = dict(b_ada=v_b_ada, conv_b=v_conv_b, dt_bias=v_dt_bias, a_log=v_a_log, d_skip=v_d_skip, ssm_norm_w=v_ssm_norm_w,
              f_bias=v_f_bias, attn_norm_w=v_attn_norm_w, ln1_g=v_ln1_g, ln1_b=v_ln1_b, ln2_g=v_ln2_g, ln2_b=v_ln2_b)
    xi, yi, ci = _place()
    chip = 2 * xi + yi
    me = 4 * xi + 2 * yi + ci
    d = D_MODEL
    conv_shard = CONV_DIM // N_CHIPS

    first = _allgather_small("gather_c", jnp.concatenate([c, conv_w[0].reshape(1, CONV_W * conv_shard)], axis=1))[:, 0]
    c_all = first[:, :d]
    conv_w_full = first[::2, d:].reshape(N_CHIPS, CONV_W, conv_shard).transpose(1, 0, 2).reshape(CONV_W, CONV_DIM)
    b_shard = lax.dynamic_slice_in_dim(b_ada, chip * ADA_SHARD, ADA_SHARD, axis=1)
    parts = _allgather_small("gather_mod", _mod_part(c_all, w_ada[0], b_shard))
    mod = lax.dynamic_index_in_dim(parts[::2], me, axis=1, keepdims=False).reshape(1, 6 * d)

    w_in_t, m_w_in_t, v_w_in_t = [jnp.transpose(t, (0, 2, 1)) for t in (w_in, m_w_in, v_w_in)]
    w_alt = _to_aligned(_gather_shards(w_in_t[0].astype(BF16)).reshape(IN_COLS, d))

    sp = {n: a[n] for n in SMALL[1:]}
    sp["conv_w"] = conv_w_full
    shards = [w_out[0].astype(BF16), w_ff_in[0].astype(BF16), w_ff_out[0].astype(BF16)]
    dx, landed, small = _local_step(x[0], loss_target[0], mod, w_alt, shards, sp)

    mine = [_sum_blocks("dev_sum_%d" % i, p) for i, p in enumerate(landed)]
    names = ["mod"] + SMALL[1:]
    vec = _pack([small[n] for n in names] + [small["conv_w"].reshape(1, CONV_W * CONV_DIM), small["loss"]])
    every, total, other = _tail_exchange(vec, mine)
    widths = [6 * d] + [a[n].shape[1] for n in SMALL[1:]]
    offs = [0]
    for w in widths:
        offs.append(offs[-1] + w + (-w % LANES))
    g_conv_w_full = total[:, offs[-1]:offs[-1] + CONV_W * CONV_DIM].reshape(CONV_W, CONV_DIM)
    loss = total[0, offs[-1] + CONV_W * CONV_DIM]
    dmod_shard = lax.dynamic_slice_in_dim(every[:, 0, :6 * d], chip * ADA_SHARD, ADA_SHARD, axis=1)
    g_w_ada = _w_ada_grad(c_all.T, dmod_shard)
    g_conv_w = lax.dynamic_slice_in_dim(g_conv_w_full, chip * conv_shard, conv_shard, axis=1)

    grads, deltas, new_m, new_v = {}, {}, {}, {}
    paired = dict(w_in=(w_in_t, m_w_in_t, v_w_in_t), w_out=(w_out, m_w_out, v_w_out), w_ff_in=(w_ff_in, m_w_ff_in, v_w_ff_in),
                  w_ff_out=(w_ff_out, m_w_ff_out, v_w_ff_out))
    for i, (n, (w, m, v)) in enumerate(paired.items()):
        res = _adamw_pair("adamw_" + n, w, mine[i], other[i], m, v, ci, by_cols=n == "w_in")
        grads[n], deltas[n], new_m[n], new_v[n] = [jnp.transpose(t, (0, 2, 1)) for t in res] if n == "w_in" else res
    for n, g, (w, m, v) in (("w_ada", g_w_ada, (w_ada, m_w_ada, v_w_ada)), ("conv_w", g_conv_w, (conv_w, m_conv_w, v_conv_w))):
        grads[n], deltas[n], new_m[n], new_v[n] = _adamw("adamw_" + n, w, g, m, v)
    for n, res in zip(SMALL, _adamw_small(total, offs, [a[n] for n in SMALL], [ms[n] for n in SMALL], [vs[n] for n in SMALL])):
        grads[n], deltas[n], new_m[n], new_v[n] = res

    order = ["w_ada", "b_ada", "w_in", "conv_w", "conv_b", "dt_bias", "a_log", "d_skip", "ssm_norm_w", "f_bias", "attn_norm_w", "w_out",
             "ln1_g", "ln1_b", "w_ff_in", "w_ff_out", "ln2_g", "ln2_b"]
    return (loss, dx[None], *[grads[n] for n in order], *[deltas[n] for n in order], *[new_m[n] for n in order], *[new_v[n] for n in order])
```

```python
import functools

import jax
import jax.numpy as jnp
from jax import lax
from jax.experimental import pallas as pl
from jax.experimental.pallas import tpu as pltpu

F32, BF16 = jnp.float32, jnp.bfloat16

D_MODEL = 1024
N_HEADS = 16
HEAD_DIM = 64
N_PAIRS = N_HEADS // 2
SSM_GROUPS = 2
SSM_STATE = 128
CHUNK = 128
CONV_W = 4
CONV_DIM = 1536
D_FF = 4096
IN_COLS = 5664
ALPHA = 2.0 ** 0.25
LN_EPS = 1e-5
RMS_EPS = 1e-5
LANES = 128
SUBLANES = 8

AL_Z, AL_XS, AL_Q, AL_K, AL_V, AL_B, AL_C, AL_DTF = 0, 1024, 2048, 3072, 4096, 5120, 5376, 5632
AL_COLS = 5760
F_LANE = 16

ADAM_LR, ADAM_B1, ADAM_B2, ADAM_EPS, ADAM_WD, ADAM_STEP = 0.001, 0.9, 0.999, 1e-08, 0.01, 10

VMEM_LIMIT = 56 * 1024 * 1024
SEQ_TK = 4096
MESH = pl.DeviceIdType.MESH


def _params(sem=None):
    return pltpu.CompilerParams(dimension_semantics=sem, vmem_limit_bytes=VMEM_LIMIT)


def _sigmoid(x):
    return 1.0 / (1.0 + jnp.exp(-x))


def _silu(x):
    return x * _sigmoid(x)


def _softplus(x):
    return jnp.maximum(x, 0.0) + jnp.log(1.0 + jnp.exp(-jnp.abs(x)))


def _split3(a):
    hi = a.astype(BF16)
    r = a - hi.astype(F32)
    mid = r.astype(BF16)
    lo = (r - mid.astype(F32)).astype(BF16)
    return hi, mid, lo


def _dot(a, b, dims=((1,), (0,))):
    return lax.dot_general(a, b, (dims, ((), ())), preferred_element_type=F32)


NN, NT, TN = ((1,), (0,)), ((1,), (1,)), ((0,), (0,))


def _dot3(t, a):
    hi, mid, lo = _split3(a)
    return _dot(t, hi) + _dot(t, mid) + _dot(t, lo)


def _matmul(name, a, b, *, dims=NN, out_dtype=F32, tm=1024, tn=1024, tk=1024, by_chip=None, epi=None, carry=()):
    if dims == NN:
        (m, k), n = a.shape, b.shape[1]
    elif dims == NT:
        (m, k), n = a.shape, b.shape[0]
    else:
        (k, m), n = a.shape, b.shape[1]
    if by_chip == "rows":
        tm = min(tm, m // 4)
    if by_chip == "cols":
        tn = min(tn, n // 4)
    tm, tn, tk = min(tm, m), min(tn, n), min(tk, k)
    assert m % tm == 0 and n % tn == 0 and k % tk == 0, (name, m, n, k, tm, tn, tk)
    nk = k // tk
    if by_chip == "rows":
        per = m // 4 // tm
        out_spec = pl.BlockSpec((None, tm, tn), lambda i, j, l: (i // per, i % per, j))
        out_shape = jax.ShapeDtypeStruct((4, m // 4, n), out_dtype)
    elif by_chip == "cols":
        per = n // 4 // tn
        out_spec = pl.BlockSpec((None, tm, tn), lambda i, j, l: (j // per, i, j % per))
        out_shape = jax.ShapeDtypeStruct((4, m, n // 4), out_dtype)
    else:
        out_spec = pl.BlockSpec((tm, tn), lambda i, j, l: (i, j))
        out_shape = jax.ShapeDtypeStruct((m, n), out_dtype)
    a_spec = pl.BlockSpec((tk, tm), lambda i, j, l: (l, i)) if dims == TN else pl.BlockSpec((tm, tk), lambda i, j, l: (i, l))
    b_spec = pl.BlockSpec((tn, tk), lambda i, j, l: (j, l)) if dims == NT else pl.BlockSpec((tk, tn), lambda i, j, l: (l, j))

    tile = pl.BlockSpec((tm, tn), lambda i, j, l: (i, j))
    in_specs, args, out_specs, out_shape = [a_spec, b_spec], [a, b], [out_spec], [out_shape]
    fn, n_tiles, n_sums = None, 1, 0
    if epi is not None:
        fn, fulls, vecs, outs, sums = epi
        assert by_chip is None and (not sums or n == tn), name
        in_specs = in_specs + [tile] * len(fulls) + [pl.BlockSpec((1, tn), lambda i, j, l: (0, j))] * len(vecs)
        args = args + list(fulls) + list(vecs)
        flipped = pl.BlockSpec((tn, tm), lambda i, j, l: (j, i))
        out_specs = [flipped if isinstance(dt, tuple) else tile for dt in outs] + [pl.BlockSpec((1, w), lambda i, j, l: (0, 0)) for w in sums]
        out_shape = [jax.ShapeDtypeStruct((n, m), dt[1]) if isinstance(dt, tuple) else jax.ShapeDtypeStruct((m, n), dt) for dt in outs]
        out_shape += [jax.ShapeDtypeStruct((1, w), F32) for w in sums]
        n_tiles, n_sums = len(outs), len(sums)
    n_in, n_out, n_c = len(args), len(out_specs), len(carry)
    scratch = [pltpu.VMEM((tm, tn) if nk > 1 else (SUBLANES, LANES), F32)]
    if n_c:
        in_specs, args = in_specs + [ANY] * n_c, args + list(carry)
        out_specs = out_specs + [ANY] * n_c
        out_shape = out_shape + [jax.ShapeDtypeStruct(g.shape, g.dtype) for g in carry]
        scratch = scratch + _exchange_sems(n_c)
    gm, gn = m // tm, n // tn

    def body(*refs):
        a_ref, b_ref = refs[:2]
        ins, outs = refs[2:n_in], refs[n_in + n_c:n_in + n_c + n_out]
        acc_ref = refs[n_in + 2 * n_c + n_out]
        i, j, l = pl.program_id(0), pl.program_id(1), pl.program_id(2)
        if n_c:
            start, wait = _scatter_plan(refs[n_in:n_in + n_c], refs[n_in + n_c + n_out:n_in + 2 * n_c + n_out], *refs[n_in + 2 * n_c + n_out + 1:])
            pl.when((i == 0) & (j == 0) & (l == 0))(start)
        part = _dot(a_ref[...].astype(BF16), b_ref[...].astype(BF16), dims)

        def finish(res):
            if fn is None:
                outs[0][...] = res.astype(outs[0].dtype)
                return
            tiles, colsums = fn(res, *[r[...] for r in ins])
            for r, val in zip(outs[:n_tiles], tiles):
                r[...] = val.astype(r.dtype)
            if n_sums:
                @pl.when(i == 0)
                def _():
                    for r in outs[n_tiles:]:
                        r[...] = jnp.zeros_like(r)
                for r, val in zip(outs[n_tiles:], colsums):
                    r[...] += val

        if nk == 1:
            finish(part)
        else:
            @pl.when(l == 0)
            def _():
                acc_ref[...] = part

            @pl.when((l > 0) & (l < nk - 1))
            def _():
                acc_ref[...] += part

            @pl.when(l == nk - 1)
            def _():
                finish(acc_ref[...] + part)

        if n_c:
            pl.when((i == gm - 1) & (j == gn - 1) & (l == nk - 1))(wait)

    res = pl.pallas_call(
        body, name=name, grid=(gm, gn, nk),
        in_specs=in_specs, out_specs=out_specs, out_shape=out_shape, scratch_shapes=scratch,
        compiler_params=_params(("arbitrary",) * 3 if n_c or n_sums else ("parallel", "parallel", "arbitrary")),
    )(*args)
    return res[0] if len(res) == 1 else res


def _rowwise(name, fn, fulls, vecs, out_fulls, out_vecs, tr=256):
    fulls = [f if isinstance(f, tuple) else (f, f.shape[1], 0) for f in fulls]
    s = fulls[0][0].shape[0]
    tr = min(tr, s)
    out_fulls = [o if len(o) == 3 else (*o, (o[0], 0, None)) for o in out_fulls]
    into = [(k, slab[2]) for k, (_, _, slab) in enumerate(out_fulls) if slab[2] is not None]
    nf, nv, nof, nov = len(fulls), len(vecs), len(out_fulls), len(out_vecs)
    in_specs = [pl.BlockSpec((tr, w), functools.partial(lambda i, cb: (i, cb), cb=cb)) for (_, w, cb) in fulls]
    in_specs += [pl.BlockSpec(v.shape, lambda i: (0, 0)) for v in vecs] + [ANY] * len(into)
    out_shape = [jax.ShapeDtypeStruct((s, slab[0]), dt) for (_, dt, slab) in out_fulls] + [jax.ShapeDtypeStruct((1, w), F32) for w in out_vecs]
    out_specs = [pl.BlockSpec((tr, w), functools.partial(lambda i, cb: (i, cb), cb=slab[1])) for (w, _, slab) in out_fulls]
    out_specs += [pl.BlockSpec((1, w), lambda i: (0, 0)) for w in out_vecs]

    def body(*refs):
        outs = refs[nf + nv + len(into):]
        of, ov = fn(*[r[...] for r in refs[:nf + nv]])
        for r, val in zip(outs[:nof], of):
            r[...] = val.astype(r.dtype)
        if nov:
            @pl.when(pl.program_id(0) == 0)
            def _():
                for r in outs[nof:]:
                    r[...] = jnp.zeros_like(r)
            for r, val in zip(outs[nof:], ov):
                r[...] += val

    res = pl.pallas_call(
        body, name=name, grid=(s // tr,), in_specs=in_specs, out_specs=out_specs, out_shape=out_shape,
        input_output_aliases={nf + nv + pos: k for pos, (k, _) in enumerate(into)},
        compiler_params=_params(("arbitrary",)),
    )(*[f[0] for f in fulls], *vecs, *[buf for _, buf in into])
    return res[:nof], res[nof:]


def _colsum(x):
    return jnp.sum(x, axis=0, keepdims=True)


def _rowmean(x):
    return jnp.mean(x, axis=-1, keepdims=True)


CONV_CB = 512
CONV_TR = 512


def _shift_down(u, halo, j):
    if j == 0:
        return u
    ru = pltpu.roll(u, j, 0)
    row8 = lax.broadcasted_iota(jnp.int32, halo.shape, 0)
    top = jnp.where(row8 < j, pltpu.roll(halo, j, 0), ru[:SUBLANES])
    return jnp.concatenate([top, ru[SUBLANES:]], axis=0)


def _shift_up(d, halo, j):
    if j == 0:
        return d
    tr = d.shape[0]
    rd = pltpu.roll(d, tr - j, 0)
    row8 = lax.broadcasted_iota(jnp.int32, halo.shape, 0)
    bot = jnp.where(row8 >= SUBLANES - j, pltpu.roll(halo, SUBLANES - j, 0), rd[tr - SUBLANES:])
    return jnp.concatenate([rd[:tr - SUBLANES], bot], axis=0)


def _conv_col(cb):
    return jnp.where(cb < 2, AL_XS // CONV_CB + cb, AL_B // CONV_CB)


def _conv_specs(s, tr):
    per8 = tr // SUBLANES
    blk = pl.BlockSpec((tr, CONV_CB), lambda cb, i: (i, _conv_col(cb)))
    prev = pl.BlockSpec((SUBLANES, CONV_CB), lambda cb, i: (jnp.maximum(i * per8 - 1, 0), _conv_col(cb)))
    return blk, prev


def _conv_pre(u, halo, w_ref, b_ref, first):
    halo = jnp.where(first, 0.0, halo)
    acc = b_ref[...] + w_ref[CONV_W - 1:CONV_W, :] * u
    shifted = [u]
    for j in range(1, CONV_W):
        sh = _shift_down(u, halo, j)
        shifted.append(sh)
        acc = acc + w_ref[CONV_W - 1 - j:CONV_W - j, :] * sh
    return acc, shifted


def _conv_bwd_pre(proj, conv_w, conv_b, dxc):
    s = proj.shape[0]
    tr = min(CONV_TR, s)
    blk, prev = _conv_specs(s, tr)

    def body(u_ref, h_ref, w_ref, b_ref, d_ref, dpre_ref, dw_ref, db_ref):
        i = pl.program_id(1)
        pre, shifted = _conv_pre(u_ref[...], h_ref[...], w_ref, b_ref, i == 0)
        sg = _sigmoid(pre)
        dpre = d_ref[...] * (sg * (1.0 + pre * (1.0 - sg)))
        dpre_ref[...] = dpre

        @pl.when(i == 0)
        def _():
            dw_ref[...] = jnp.zeros_like(dw_ref)
            db_ref[...] = jnp.zeros_like(db_ref)

        db_ref[...] += _colsum(dpre)
        for j in range(CONV_W):
            dw_ref[CONV_W - 1 - j:CONV_W - j, :] += _colsum(dpre * shifted[j])

    own = pl.BlockSpec((tr, CONV_CB), lambda cb, i: (i, cb))
    wspec = pl.BlockSpec((CONV_W, CONV_CB), lambda cb, i: (0, cb))
    bspec = pl.BlockSpec((1, CONV_CB), lambda cb, i: (0, cb))
    return pl.pallas_call(
        body, name="conv_bwd_pre", grid=(CONV_DIM // CONV_CB, s // tr),
        in_specs=[blk, prev, wspec, bspec, own], out_specs=[own, wspec, bspec],
        out_shape=[jax.ShapeDtypeStruct((s, CONV_DIM), F32), jax.ShapeDtypeStruct((CONV_W, CONV_DIM), F32),
                   jax.ShapeDtypeStruct((1, CONV_DIM), F32)],
        compiler_params=_params(("parallel", "arbitrary")),
    )(proj, proj, conv_w, conv_b, dxc)


def _conv_bwd_in(dpre, conv_w, dproj):
    s = dpre.shape[0]
    tr = min(CONV_TR, s)
    per8 = tr // SUBLANES
    last8 = s // SUBLANES - 1
    nb = s // tr

    def body(d_ref, n_ref, w_ref, _, o_ref):
        d = d_ref[...]
        halo = jnp.where(pl.program_id(1) == nb - 1, 0.0, n_ref[...])
        acc = w_ref[CONV_W - 1:CONV_W, :] * d
        for j in range(1, CONV_W):
            acc = acc + w_ref[CONV_W - 1 - j:CONV_W - j, :] * _shift_up(d, halo, j)
        o_ref[...] = acc.astype(o_ref.dtype)

    own = pl.BlockSpec((tr, CONV_CB), lambda cb, i: (i, cb))
    nxt = pl.BlockSpec((SUBLANES, CONV_CB), lambda cb, i: (jnp.minimum((i + 1) * per8, last8), cb))
    return pl.pallas_call(
        body, name="conv_bwd_in", grid=(CONV_DIM // CONV_CB, nb),
        in_specs=[own, nxt, pl.BlockSpec((CONV_W, CONV_CB), lambda cb, i: (0, cb)), ANY],
        out_specs=pl.BlockSpec((tr, CONV_CB), lambda cb, i: (i, _conv_col(cb))),
        out_shape=jax.ShapeDtypeStruct(dproj.shape, dproj.dtype), input_output_aliases={3: 0},
        compiler_params=_params(("parallel", "parallel")),
    )(dpre, dpre, conv_w, dproj)


XC_B, XC_C = 1024, 1280


def _tile_iotas():
    row = lax.broadcasted_iota(jnp.int32, (CHUNK, LANES), 0)
    lane = lax.broadcasted_iota(jnp.int32, (CHUNK, LANES), 1)
    return row, lane


def _ssd_scalars(dtf_ref, bias_ref, alog_ref, row, lane):
    head = lane[:1] < N_HEADS
    raw = dtf_ref[...] + bias_ref[...]
    dt = _softplus(raw)
    a_neg = jnp.where(head, -jnp.exp(alog_ref[...]), 0.0)
    a = dt * a_neg
    tril = (row >= lane).astype(BF16)
    s = _dot3(tril, a)
    return raw, dt, a_neg, s


def _pair(v, j, lo):
    return jnp.where(lo, v[:, 2 * j:2 * j + 1], v[:, 2 * j + 1:2 * j + 2])


def _head_sum(x, lo, hh):
    return jnp.sum(jnp.where(lo == (hh == 0), x, 0.0), axis=1, keepdims=True)


def _decay_masks(s, st, h, row, lane):
    s_col = jnp.broadcast_to(s[:, h:h + 1], (CHUNK, LANES))
    s_row = jnp.broadcast_to(st[h:h + 1, :], (CHUNK, LANES))
    lm = jnp.where(row >= lane, jnp.exp(s_col - s_row), 0.0)
    lmt = jnp.where(row <= lane, jnp.exp(s_row - s_col), 0.0)
    return lm, lmt


def _gated_norm(y, z, w):
    g = y * _silu(z)
    return g * lax.rsqrt(_rowmean(g * g) + RMS_EPS) * w


def _ssd_fwd(proj, conv_w, conv_b, dt_bias_l, a_log_l, d_exp, norm_w):
    s_len = proj.shape[0]
    nc = s_len // CHUNK

    def body(xs_ref, bc_ref, cw_ref, cb_ref, dtf_ref, bias_ref, alog_ref, dexp_ref, z_ref, w_ref,
             x_ref, y_ref, prevs_ref, ymix_ref, state_ref, halo_ref):
        first = pl.program_id(0) == 0

        @pl.when(first)
        def _():
            state_ref[...] = jnp.zeros_like(state_ref)
            halo_ref[...] = jnp.zeros_like(halo_ref)

        u = jnp.concatenate([xs_ref[...], bc_ref[...]], axis=1)
        pre, _ = _conv_pre(u, halo_ref[...], cw_ref, cb_ref, first)
        halo_ref[...] = u[CHUNK - SUBLANES:]
        x_ref[...] = _silu(pre)

        row, lane = _tile_iotas()
        lo = lane < HEAD_DIM
        _, dt, _, s = _ssd_scalars(dtf_ref, bias_ref, alog_ref, row, lane)
        tot = s[CHUNK - 1:CHUNK, :]
        st = s.T
        for g in range(SSM_GROUPS):
            bg = x_ref[:, XC_B + g * SSM_STATE:XC_B + (g + 1) * SSM_STATE].astype(BF16)
            cg = x_ref[:, XC_C + g * SSM_STATE:XC_C + (g + 1) * SSM_STATE].astype(BF16)
            cb = _dot(cg, bg, NT)
            for j in range(g * 4, g * 4 + 4):
                xs_p = x_ref[:, j * LANES:(j + 1) * LANES]
                dt_p, s_p, tot_p = _pair(dt, j, lo), _pair(s, j, lo), _pair(tot, j, lo[:1])
                xc_p = xs_p * dt_p
                xc_b = xc_p.astype(BF16)
                yd = []
                for hh in range(2):
                    lm, _ = _decay_masks(s, st, 2 * j + hh, row, lane)
                    yd.append(_dot((cb * lm).astype(BF16), xc_b))
                prev = state_ref[j]
                prevs_ref[0, j] = prev
                yo = _dot(cg, prev.astype(BF16)) * jnp.exp(s_p)
                y_ref[:, j * LANES:(j + 1) * LANES] = jnp.where(lo, yd[0], yd[1]) + yo + dexp_ref[:, j * LANES:(j + 1) * LANES] * xs_p
                to_end = jnp.exp(tot_p - s_p)
                state_ref[j] = jnp.exp(tot_p) * prev + _dot(bg, (xc_p * to_end).astype(BF16), TN)
        ymix_ref[...] = _gated_norm(y_ref[...], z_ref[...], w_ref[...]).astype(ymix_ref.dtype)

    vec = lambda w: pl.BlockSpec((1, w), lambda c: (0, 0))
    rows = pl.BlockSpec((CHUNK, D_MODEL), lambda c: (c, 0))
    return pl.pallas_call(
        body, name="ssd_fwd", grid=(nc,),
        in_specs=[pl.BlockSpec((CHUNK, D_MODEL), lambda c: (c, AL_XS // D_MODEL)),
                  pl.BlockSpec((CHUNK, CONV_DIM - D_MODEL), lambda c: (c, AL_B // (CONV_DIM - D_MODEL))),
                  pl.BlockSpec((CONV_W, CONV_DIM), lambda c: (0, 0)), vec(CONV_DIM),
                  pl.BlockSpec((CHUNK, LANES), lambda c: (c, AL_DTF // LANES)),
                  vec(LANES), vec(LANES), vec(D_MODEL), pl.BlockSpec((CHUNK, D_MODEL), lambda c: (c, AL_Z // D_MODEL)), vec(D_MODEL)],
        out_specs=[pl.BlockSpec((CHUNK, CONV_DIM), lambda c: (c, 0)), rows,
                   pl.BlockSpec((1, N_PAIRS, SSM_STATE, LANES), lambda c: (c, 0, 0, 0)), rows],
        out_shape=[jax.ShapeDtypeStruct((s_len, CONV_DIM), F32), jax.ShapeDtypeStruct((s_len, D_MODEL), F32),
                   jax.ShapeDtypeStruct((nc, N_PAIRS, SSM_STATE, LANES), F32), jax.ShapeDtypeStruct((s_len, 2 * D_MODEL), BF16)],
        scratch_shapes=[pltpu.VMEM((N_PAIRS, SSM_STATE, LANES), F32), pltpu.VMEM((SUBLANES, CONV_DIM), F32)],
        compiler_params=_params(("arbitrary",)),
    )(proj, proj, conv_w, conv_b, proj, dt_bias_l, a_log_l, d_exp, proj, norm_w)


def _ssd_bwd(xc_all, proj, dt_bias_l, a_log_l, d_exp, prevs, y_ssd, dymix, norm_w):
    s_len = xc_all.shape[0]
    nc = s_len // CHUNK

    def body(x_ref, dtf_ref, bias_ref, alog_ref, dexp_ref, prevs_ref, y_ref, z_ref, dym_ref, w_ref,
             dx_ref, ddt_ref, da_ref, dd_ref, dbias_ref, dz_ref, dw_ref, dstate_ref):
        @pl.when(pl.program_id(0) == 0)
        def _():
            dstate_ref[...] = jnp.zeros_like(dstate_ref)
            da_ref[...] = jnp.zeros_like(da_ref)
            dd_ref[...] = jnp.zeros_like(dd_ref)
            dbias_ref[...] = jnp.zeros_like(dbias_ref)
            dw_ref[...] = jnp.zeros_like(dw_ref)

        y, z, dyo = y_ref[...], z_ref[...], dym_ref[...]
        sg = _sigmoid(z)
        sz = z * sg
        gated = y * sz
        rn = lax.rsqrt(_rowmean(gated * gated) + RMS_EPS)
        dg = _rms_bwd(dyo * w_ref[...], gated, rn)
        dy_full = dg * sz
        dz_ref[...] = (dg * y * (sg * (1.0 + z * (1.0 - sg)))).astype(dz_ref.dtype)
        dw_ref[...] += _colsum(dyo * gated * rn)

        row, lane = _tile_iotas()
        lo = lane < HEAD_DIM
        last = row == CHUNK - 1
        raw, dt, a_neg, s = _ssd_scalars(dtf_ref, bias_ref, alog_ref, row, lane)
        tot = s[CHUNK - 1:CHUNK, :]
        st = s.T
        ds_acc = jnp.zeros((CHUNK, LANES), F32)
        ddt_acc = jnp.zeros((CHUNK, LANES), F32)
        for g in range(SSM_GROUPS):
            bcol = slice(XC_B + g * SSM_STATE, XC_B + (g + 1) * SSM_STATE)
            ccol = slice(XC_C + g * SSM_STATE, XC_C + (g + 1) * SSM_STATE)
            bg = x_ref[:, bcol].astype(BF16)
            cg = x_ref[:, ccol].astype(BF16)
            cb = _dot(cg, bg, NT)
            cbt = _dot(bg, cg, NT)
            dcb = jnp.zeros((CHUNK, LANES), F32)
            dcbt = jnp.zeros((CHUNK, LANES), F32)
            db_acc = jnp.zeros((CHUNK, LANES), F32)
            dc_acc = jnp.zeros((CHUNK, LANES), F32)
            for j in range(g * 4, g * 4 + 4):
                cols = slice(j * LANES, (j + 1) * LANES)
                xs_p, dy_p = x_ref[:, cols], dy_full[:, cols]
                dt_p, s_p, tot_p = _pair(dt, j, lo), _pair(s, j, lo), _pair(tot, j, lo[:1])
                xc_p = xs_p * dt_p
                xc_b, dy_b = xc_p.astype(BF16), dy_p.astype(BF16)
                e_p, f_p, etot_p = jnp.exp(s_p), jnp.exp(tot_p - s_p), jnp.exp(tot_p)
                prev, dnext = prevs_ref[0, j], dstate_ref[j]
                prev_b, dnext_b = prev.astype(BF16), dnext.astype(BF16)
                dd_ref[:, cols] += _colsum(dy_p * xs_p)
                dxs_p = dexp_ref[:, cols] * dy_p
                cp = _dot(cg, prev_b)
                gy = (dy_p * e_p).astype(BF16)
                dc_acc += _dot(gy, prev_b, NT)
                dstate_ref[j] = etot_p * dnext + _dot(cg, gy, TN)
                de = dy_p * cp * e_p
                bds = _dot(bg, dnext_b)
                db_acc += _dot((xc_p * f_p).astype(BF16), dnext_b, NT)
                dxc_p = bds * f_p
                df = bds * xc_p * f_p
                dtot_p = _colsum(dnext * prev) * etot_p + _colsum(df)
                dsl = de - df + jnp.where(last, dtot_p, 0.0)
                for hh in range(2):
                    h = 2 * j + hh
                    mine = lo == (hh == 0)
                    lm, lmt = _decay_masks(s, st, h, row, lane)
                    dy_h = jnp.where(mine, dy_p, 0.0).astype(BF16)
                    xc_h = jnp.where(mine, xc_p, 0.0).astype(BF16)
                    dm = _dot(dy_h, xc_b, NT)
                    dmt = _dot(xc_h, dy_b, NT)
                    mt = cbt * lmt
                    dxc_p += _dot(mt.astype(BF16), dy_h)
                    dml, dmtl = dm * lm, dmt * lmt
                    ds_h = jnp.sum(dml * cb - dmtl * cbt + jnp.where(mine, dsl, 0.0), axis=1, keepdims=True)
                    ds_acc += jnp.where(lane == h, ds_h, 0.0)
                    dcb += dml
                    dcbt += dmtl
                    ddt_acc += jnp.where(lane == h, _head_sum(dxc_p * xs_p, lo, hh), 0.0)
                dx_ref[:, cols] = dxs_p + dxc_p * dt_p
            dx_ref[:, ccol] = dc_acc + _dot(dcb.astype(BF16), bg)
            dx_ref[:, bcol] = db_acc + _dot(dcbt.astype(BF16), cg)
        triu = (row <= lane).astype(BF16)
        da = _dot3(triu, ds_acc)
        ddt = ddt_acc + da * a_neg
        da_ref[...] += _colsum(da * dt) * a_neg[:1]
        ddt_raw = jnp.where(lane < N_HEADS, ddt * _sigmoid(raw), 0.0)
        dbias_ref[...] += _colsum(ddt_raw)
        ddt_ref[...] = ddt_raw

    rev = lambda c: nc - 1 - c
    vec = lambda w: pl.BlockSpec((1, w), lambda c: (0, 0))
    rows = lambda cb: pl.BlockSpec((CHUNK, D_MODEL), lambda c: (rev(c), cb))
    return pl.pallas_call(
        body, name="ssd_bwd", grid=(nc,),
        in_specs=[pl.BlockSpec((CHUNK, CONV_DIM), lambda c: (rev(c), 0)), pl.BlockSpec((CHUNK, LANES), lambda c: (rev(c), AL_DTF // LANES)),
                  vec(LANES), vec(LANES), vec(D_MODEL),
                  pl.BlockSpec((1, N_PAIRS, SSM_STATE, LANES), lambda c: (rev(c), 0, 0, 0)),
                  rows(0), rows(AL_Z // D_MODEL), rows(0), vec(D_MODEL)],
        out_specs=[pl.BlockSpec((CHUNK, CONV_DIM), lambda c: (rev(c), 0)), pl.BlockSpec((CHUNK, LANES), lambda c: (rev(c), 0)),
                   vec(LANES), vec(D_MODEL), vec(LANES), rows(AL_Z // D_MODEL), vec(D_MODEL)],
        out_shape=[jax.ShapeDtypeStruct((s_len, CONV_DIM), F32), jax.ShapeDtypeStruct((s_len, LANES), F32),
                   jax.ShapeDtypeStruct((1, LANES), F32), jax.ShapeDtypeStruct((1, D_MODEL), F32), jax.ShapeDtypeStruct((1, LANES), F32),
                   jax.ShapeDtypeStruct((s_len, AL_COLS), BF16), jax.ShapeDtypeStruct((1, D_MODEL), F32)],
        scratch_shapes=[pltpu.VMEM((N_PAIRS, SSM_STATE, LANES), F32)],
        compiler_params=_params(("arbitrary",)),
    )(xc_all, proj, dt_bias_l, a_log_l, d_exp, prevs, y_ssd, proj, dymix, norm_w)


AUG_LANES = 6


def _aug_base(hh):
    return HEAD_DIM if hh == 0 else 0


NEG = -1e30
ATT_T = 512


def _fox_cum(proj, f_bias_l):
    s_len = proj.shape[0]
    nc = s_len // CHUNK

    def body(dtf_ref, fb_ref, cum_ref):
        row, lane = _tile_iotas()
        tril = (row >= lane).astype(BF16)
        spread = [(jnp.where(lane == AUG_LANES * row + i, 1.0, 0.0) - jnp.where(lane == AUG_LANES * row + 3 + i, 1.0, 0.0)).astype(BF16)
                  for i in range(3)]

        def step(c, carry):
            rows = pl.ds(pl.multiple_of(c * CHUNK, CHUNK), CHUNK)
            lf = -_softplus(-(dtf_ref[rows, :] + fb_ref[...]))
            lf = jnp.where(lane < N_HEADS, pltpu.roll(lf, LANES - F_LANE, 1), 0.0)
            cs = _dot3(tril, lf) + carry
            parts = _split3(cs)
            cum_ref[rows, :] = _dot(parts[0], spread[0]) + _dot(parts[1], spread[1]) + _dot(parts[2], spread[2])
            return cs[CHUNK - 1:CHUNK, :]

        lax.fori_loop(0, nc, step, jnp.zeros((1, LANES), F32))

    return pl.pallas_call(
        body, name="fox_cum", grid=(1,),
        in_specs=[pl.BlockSpec((s_len, LANES), lambda i: (0, AL_DTF // LANES)), pl.BlockSpec((1, LANES), lambda i: (0, 0))],
        out_specs=pl.BlockSpec((s_len, LANES), lambda i: (0, 0)),
        out_shape=jax.ShapeDtypeStruct((s_len, LANES), F32),
        compiler_params=_params(("arbitrary",)),
    )(proj, f_bias_l)


def _fox_cum_bwd(dcum, proj, f_bias_l, ddt_tile, dproj):
    s_len = proj.shape[0]
    nc = s_len // CHUNK

    def body(dcum_ref, dtf_ref, fb_ref, ddt_ref, _, out_ref, dfb_ref):
        row, lane = _tile_iotas()
        triu = (row <= lane).astype(BF16)
        is_f = (lane >= F_LANE) & (lane < F_LANE + N_HEADS)

        def step(t, carry):
            run, dfb = carry
            rows = pl.ds(pl.multiple_of((nc - 1 - t) * CHUNK, CHUNK), CHUNK)
            rc = _dot3(triu, dcum_ref[rows, :]) + run
            sg = _sigmoid(-(dtf_ref[rows, :] + fb_ref[...]))
            df = jnp.where(is_f, pltpu.roll(rc, F_LANE, 1) * sg, 0.0)
            out_ref[rows, :] = (df + ddt_ref[rows, :]).astype(out_ref.dtype)
            return rc[0:1, :], dfb + _colsum(df)

        _, dfb = lax.fori_loop(0, nc, step, (jnp.zeros((1, LANES), F32), jnp.zeros((1, LANES), F32)))
        dfb_ref[...] = dfb

    whole = pl.BlockSpec((s_len, LANES), lambda i: (0, 0))
    dtf_cols = pl.BlockSpec((s_len, LANES), lambda i: (0, AL_DTF // LANES))
    vec = pl.BlockSpec((1, LANES), lambda i: (0, 0))
    return pl.pallas_call(
        body, name="fox_cum_bwd", grid=(1,),
        in_specs=[whole, dtf_cols, vec, whole, ANY], out_specs=[dtf_cols, vec],
        out_shape=[jax.ShapeDtypeStruct(dproj.shape, dproj.dtype), jax.ShapeDtypeStruct((1, LANES), F32)],
        input_output_aliases={4: 0}, compiler_params=_params(("arbitrary",)),
    )(dcum, proj, f_bias_l, ddt_tile, dproj)


def _attn_prep(proj, cum):
    s_len = proj.shape[0]
    tr = min(256, s_len)

    def body(q_ref, k_ref, v_ref, cum_ref, qa_ref, ka_ref, vb_ref):
        lane = lax.broadcasted_iota(jnp.int32, (tr, LANES), 1)
        lo = lane < HEAD_DIM
        c = cum_ref[...]
        for p in range(N_PAIRS):
            cols = slice(p * LANES, (p + 1) * LANES)
            q, k = q_ref[:, cols] * (HEAD_DIM ** -0.5), k_ref[:, cols]
            for hh in range(2):
                base = _aug_base(hh)
                r = pltpu.roll(c, (base - AUG_LANES * (2 * p + hh)) % LANES, 1)
                first = (lane >= base) & (lane < base + 3)
                second = (lane >= base + 3) & (lane < base + AUG_LANES)
                mine = lo == (hh == 0)
                qa_ref[2 * p + hh] = jnp.where(mine, q, jnp.where(first, r, jnp.where(second, 1.0, 0.0))).astype(BF16)
                ka_ref[2 * p + hh] = jnp.where(mine, k, jnp.where(first, 1.0, jnp.where(second, r, 0.0))).astype(BF16)
        vb_ref[...] = v_ref[...].astype(BF16)

    assert AL_Q % D_MODEL == 0 and AL_K % D_MODEL == 0 and AL_V % D_MODEL == 0
    slab = lambda col0: pl.BlockSpec((tr, D_MODEL), lambda i: (i, col0 // D_MODEL))
    heads = pl.BlockSpec((N_HEADS, tr, LANES), lambda i: (0, i, 0))
    return pl.pallas_call(
        body, name="attn_prep", grid=(s_len // tr,),
        in_specs=[slab(AL_Q), slab(AL_K), slab(AL_V), pl.BlockSpec((tr, LANES), lambda i: (i, 0))],
        out_specs=[heads, heads, pl.BlockSpec((tr, D_MODEL), lambda i: (i, 0))],
        out_shape=[jax.ShapeDtypeStruct((N_HEADS, s_len, LANES), BF16), jax.ShapeDtypeStruct((N_HEADS, s_len, LANES), BF16),
                   jax.ShapeDtypeStruct((s_len, D_MODEL), BF16)],
        compiler_params=_params(("parallel",)),
    )(proj, proj, proj, cum)


def _attn_fwd(qa, ka, vb, shards):
    s_len = vb.shape[0]
    t = min(2 * ATT_T, s_len)
    nq = s_len // t
    n = len(shards)

    def body(qa_ref, ka_ref, vb_ref, *rest):
        o_ref, lse_ref = rest[n:n + 2]
        start, finish = _gather_plan(rest[:n], rest[n + 2:2 * n + 2], *rest[2 * n + 2:])
        i = pl.program_id(1)
        pl.when((pl.program_id(0) == 0) & (i == 0))(start)
        row = lax.broadcasted_iota(jnp.int32, (t, t), 0)
        col = lax.broadcasted_iota(jnp.int32, (t, t), 1)
        lo = lax.broadcasted_iota(jnp.int32, (t, LANES), 1) < HEAD_DIM
        qs = (qa_ref[0], qa_ref[1])

        def block(j, carry, masked):
            rows = pl.ds(pl.multiple_of(j * t, t), t)
            v = vb_ref[rows, :]
            new = []
            for hh in range(2):
                m, l, acc = carry[hh]
                s = _dot(qs[hh], ka_ref[hh, rows, :], NT)
                if masked:
                    s = jnp.where(row >= col, s, NEG)
                m_new = jnp.maximum(m, jnp.max(s, axis=1, keepdims=True))
                alpha = jnp.exp(m - m_new)
                p = jnp.exp(s - m_new)
                new.append((m_new, alpha * l + jnp.sum(p, axis=1, keepdims=True), alpha * acc + _dot(p.astype(BF16), v)))
            return tuple(new)

        init = (jnp.full((t, 1), NEG, F32), jnp.zeros((t, 1), F32), jnp.zeros((t, LANES), F32))
        carry = lax.fori_loop(0, i, functools.partial(block, masked=False), (init, init))
        (m0, l0, acc0), (m1, l1, acc1) = block(i, carry, True)
        o_ref[...] = jnp.where(lo, acc0 / l0, acc1 / l1)
        lse_ref[...] = jnp.where(lo, m0 + jnp.log(l0), m1 + jnp.log(l1))
        pl.when((pl.program_id(0) == N_PAIRS - 1) & (i == nq - 1))(finish)

    out = pl.BlockSpec((t, LANES), lambda p, i: (i, p))
    res = pl.pallas_call(
        body, name="attn_fwd", grid=(N_PAIRS, nq),
        in_specs=[pl.BlockSpec((2, t, LANES), lambda p, i: (p, i, 0)), pl.BlockSpec((2, s_len, LANES), lambda p, i: (p, 0, 0)),
                  pl.BlockSpec((s_len, LANES), lambda p, i: (0, p))] + [ANY] * n,
        out_specs=[out, out] + [ANY] * n,
        out_shape=[jax.ShapeDtypeStruct((s_len, D_MODEL), F32), jax.ShapeDtypeStruct((s_len, D_MODEL), F32)]
        + [jax.ShapeDtypeStruct((N_CHIPS, *h.shape), h.dtype) for h in shards],
        scratch_shapes=_exchange_sems(n),
        compiler_params=_params(("arbitrary", "arbitrary")),
    )(qa, ka, vb, *shards)
    return res[0], res[1], res[2:]


def _attn_bwd(qa, ka, vb, o, lse, do, parts, dproj):
    s_len = vb.shape[0]
    t = min(2 * ATT_T, s_len)
    nq = s_len // t
    n = len(parts)

    def body(qa_ref, ka_ref, vb_ref, o_ref, lse_ref, do_ref, *rest):
        dqa_ref, dka_ref, dv_ref = rest[n + 1:n + 4]
        start, finish = _reduce_plan(rest[:n], rest[n + 4:2 * n + 4], *rest[2 * n + 4:])
        j = pl.program_id(1)
        pl.when((pl.program_id(0) == 0) & (j == 0))(start)

        @pl.when(j == 0)
        def _():
            dqa_ref[...] = jnp.zeros_like(dqa_ref)

        row = lax.broadcasted_iota(jnp.int32, (t, t), 0)
        col = lax.broadcasted_iota(jnp.int32, (t, t), 1)
        lo = lax.broadcasted_iota(jnp.int32, (t, LANES), 1) < HEAD_DIM
        v = vb_ref[...]
        ks = (ka_ref[0], ka_ref[1])

        def block(i, carry, masked):
            dk, dv = list(carry[:2]), carry[2]
            rows = pl.ds(pl.multiple_of(i * t, t), t)
            do_p, o_p, lse_p = do_ref[rows, :], o_ref[rows, :], lse_ref[rows, :]
            for hh in range(2):
                q = qa_ref[hh, rows, :]
                do_h = jnp.where(lo == (hh == 0), do_p, 0.0)
                delta = jnp.sum(do_h * o_p, axis=1, keepdims=True)
                s = _dot(q, ks[hh], NT)
                if masked:
                    s = jnp.where(row >= col, s, NEG)
                p = jnp.exp(s - lse_p[:, hh * HEAD_DIM:hh * HEAD_DIM + 1])
                do_b = do_h.astype(BF16)
                ds = (p * (_dot(do_b, v, NT) - delta)).astype(BF16)
                dv = dv + _dot(p.astype(BF16), do_b, TN)
                dk[hh] = dk[hh] + _dot(ds, q, TN)
                dqa_ref[hh, rows, :] += _dot(ds, ks[hh])
            return dk[0], dk[1], dv

        zero = jnp.zeros((t, LANES), F32)
        carry = block(j, (zero, zero, zero), True)
        dk0, dk1, dv = lax.fori_loop(j + 1, nq, functools.partial(block, masked=False), carry)
        dka_ref[0] = dk0
        dka_ref[1] = dk1
        dv_ref[...] = dv.astype(dv_ref.dtype)
        pl.when((pl.program_id(0) == N_PAIRS - 1) & (j == nq - 1))(finish)

    whole_pair = pl.BlockSpec((2, s_len, LANES), lambda p, j: (p, 0, 0))
    blk_pair = pl.BlockSpec((2, t, LANES), lambda p, j: (p, j, 0))
    whole_cols = pl.BlockSpec((s_len, LANES), lambda p, j: (0, p))
    blk_cols = pl.BlockSpec((t, LANES), lambda p, j: (j, p))
    res = pl.pallas_call(
        body, name="attn_bwd", grid=(N_PAIRS, nq),
        in_specs=[whole_pair, blk_pair, blk_cols, whole_cols, whole_cols, whole_cols] + [ANY] * (n + 1),
        out_specs=[whole_pair, blk_pair, pl.BlockSpec((t, LANES), lambda p, j: (j, AL_V // LANES + p))] + [ANY] * n,
        out_shape=[jax.ShapeDtypeStruct((N_HEADS, s_len, LANES), F32), jax.ShapeDtypeStruct((N_HEADS, s_len, LANES), F32),
                   jax.ShapeDtypeStruct(dproj.shape, dproj.dtype)]
        + [jax.ShapeDtypeStruct((N_DEV, g.shape[1] // 2, g.shape[2]), g.dtype) for g in parts],
        scratch_shapes=_exchange_sems(n), input_output_aliases={6 + n: 2},
        compiler_params=_params(("arbitrary", "arbitrary")),
    )(qa, ka, vb, o, lse, do, *parts, dproj)
    return res[0], res[1], res[2], res[3:]


def _attn_post(dqa, dka, dproj):
    s_len = dqa.shape[1]
    tr = min(256, s_len)
    assert AL_K == AL_Q + D_MODEL and AL_Q % (2 * D_MODEL) == 0

    def body(dqa_ref, dka_ref, _, dqk_ref, dcum_ref):
        lane = lax.broadcasted_iota(jnp.int32, (tr, LANES), 1)
        lo = lane < HEAD_DIM
        dcum = jnp.zeros((tr, LANES), F32)
        for p in range(N_PAIRS):
            a0, a1, b0, b1 = dqa_ref[2 * p], dqa_ref[2 * p + 1], dka_ref[2 * p], dka_ref[2 * p + 1]
            dq = jnp.where(lo, a0, a1) * (HEAD_DIM ** -0.5)
            dqk_ref[:, p * LANES:(p + 1) * LANES] = dq.astype(dqk_ref.dtype)
            dqk_ref[:, D_MODEL + p * LANES:D_MODEL + (p + 1) * LANES] = jnp.where(lo, b0, b1).astype(dqk_ref.dtype)
            for hh, (a, b) in enumerate(((a0, b0), (a1, b1))):
                base = _aug_base(hh)
                dcum = dcum + jnp.where(lane == 2 * p + hh, a[:, base:base + 1] - b[:, base + 3:base + 4], 0.0)
        dcum_ref[...] = dcum

    heads = pl.BlockSpec((N_HEADS, tr, LANES), lambda i: (0, i, 0))
    return pl.pallas_call(
        body, name="attn_post", grid=(s_len // tr,),
        in_specs=[heads, heads, ANY],
        out_specs=[pl.BlockSpec((tr, 2 * D_MODEL), lambda i: (i, AL_Q // (2 * D_MODEL))), pl.BlockSpec((tr, LANES), lambda i: (i, 0))],
        out_shape=[jax.ShapeDtypeStruct(dproj.shape, dproj.dtype), jax.ShapeDtypeStruct((s_len, LANES), F32)],
        input_output_aliases={2: 0}, compiler_params=_params(("parallel",)),
    )(dqa, dka, dproj)


def _ln_stats(r):
    mu = _rowmean(r)
    xc = r - mu
    rstd = lax.rsqrt(_rowmean(xc * xc) + LN_EPS)
    return xc * rstd, rstd


def _ln_bwd(dxh, xh, rstd):
    return rstd * (dxh - _rowmean(dxh) - xh * _rowmean(dxh * xh))


def _rms_bwd(dgn, g, r):
    return r * dgn - (r * r * r) * g * _rowmean(dgn * g)


def _to_aligned(wt):
    out = jnp.zeros((AL_COLS, wt.shape[1]), wt.dtype)
    for dst, (lo, hi) in ((0, (0, 2048)), (AL_Q, (2576, 5648)), (AL_B, (2048, 2560)), (AL_DTF, (2560, 2576)), (AL_DTF + 16, (5648, 5664))):
        out = lax.dynamic_update_slice_in_dim(out, wt[lo:hi], dst, axis=0)
    return out


def _from_aligned(gt):
    out = jnp.zeros((IN_COLS, gt.shape[1]), gt.dtype)
    for dst, (lo, hi) in ((0, (0, AL_Q)), (2048, (AL_B, AL_DTF)), (2560, (AL_DTF, AL_DTF + 16)), (2576, (AL_Q, AL_B)),
                          (5648, (AL_DTF + 16, AL_DTF + 32))):
        out = lax.dynamic_update_slice_in_dim(out, gt[lo:hi], dst, axis=0)
    return out


def _lanes(v, at=0):
    return jnp.pad(v, ((0, 0), (at, LANES - at - v.shape[1])))


def _local_step(x, tgt, mod, w_alt, shards, sp):
    d = D_MODEL
    sh1, sc1, g1, sh2, sc2, g2 = [mod[:, i * d:(i + 1) * d] for i in range(6)]
    dt_bias_l, a_log_l, f_bias_l = _lanes(sp["dt_bias"]), _lanes(sp["a_log"]), _lanes(sp["f_bias"], F_LANE)
    d_exp = jnp.repeat(sp["d_skip"], HEAD_DIM, axis=1)

    (h1,), _ = _rowwise("mod1", lambda x, sc, sh: ([x * (1.0 + sc) + sh], []), [x], [sc1, sh1], [(d, BF16)], [])
    proj = _matmul("proj", h1, w_alt, dims=NT, tn=1152)
    xc_all, y_ssd, prevs, y_mix = _ssd_fwd(proj, sp["conv_w"], sp["conv_b"], dt_bias_l, a_log_l, d_exp, sp["ssm_norm_w"])
    cum = _fox_cum(proj, f_bias_l)
    qa, ka, vb = _attn_prep(proj, cum)
    o, lse, (g_out, g_fi, g_fo) = _attn_fwd(qa, ka, vb, shards)
    w_out = g_out.reshape(2 * d, d)
    w_fi = g_fi.transpose(1, 0, 2).reshape(d, D_FF)
    w_fo = g_fo.reshape(D_FF, d)
    (y_mix,), _ = _rowwise("attn_norm", lambda o, w: ([o * lax.rsqrt(_rowmean(o * o) + RMS_EPS) * w], []),
                           [o], [sp["attn_norm_w"]], [(d, BF16, (2 * d, 1, y_mix))], [])
    def ln1_fwd(y, x, g1, sc2, sh2, lg, lb):
        r1 = ALPHA * x + (1.0 + g1) * y
        xh, _ = _ln_stats(r1)
        x1 = xh * lg + lb
        h2 = x1 * (1.0 + sc2) + sh2
        return [y, r1, h2, h2.T], []

    def relu2(u):
        a = jnp.square(jnp.maximum(u, 0.0))
        return [a, a.T], []

    y, r1, h2, h2_t = _matmul("out_proj", y_mix, w_out, tm=512, tk=2048,
                              epi=(ln1_fwd, [x], [g1, sc2, sh2, sp["ln1_g"], sp["ln1_b"]], [F32, F32, BF16, ("T", BF16)], []))
    act, act_t = _matmul("ff_in", h2, w_fi, epi=(relu2, [], [], [BF16, ("T", BF16)], []))

    def head(ff, r1, tgt, g2, l1g, l1b, l2g, l2b):
        xh1, _ = _ln_stats(r1)
        x1 = xh1 * l1g + l1b
        xh2, rstd2 = _ln_stats(ALPHA * x1 + (1.0 + g2) * ff)
        err = xh2 * l2g + l2b - tgt
        loss = 0.5 * jnp.sum(_rowmean(err * err))
        dx2 = err * (1.0 / d)
        dr2 = _ln_bwd(dx2 * l2g, xh2, rstd2)
        return ([dr2, (1.0 + g2) * dr2],
                [_colsum(dx2 * xh2), _colsum(dx2), _colsum(dr2 * ff), jnp.full((1, LANES), loss, F32)])

    dr2, dff, d_ln2_g, d_ln2_b, d_g2, loss = _matmul(
        "ff_out", act, w_fo, tm=512, tk=D_FF,
        epi=(head, [r1, tgt], [g2, sp["ln1_g"], sp["ln1_b"], sp["ln2_g"], sp["ln2_b"]], [F32, BF16], [d, d, d, LANES]))
    du = _matmul("d_act", dff, w_fo, dims=NT, epi=(lambda da, act: ([da * (2.0 * jnp.sqrt(act.astype(F32)))], []), [act], [], [BF16], []))
    dw_fo = _matmul("dw_ff_out", act_t, dff, tk=SEQ_TK, out_dtype=BF16, by_chip="rows")
    dw_fi = _matmul("dw_ff_in", h2_t, du, tk=SEQ_TK, out_dtype=BF16, by_chip="cols")

    def ln1_bwd(dh2, r1, dr2, y, sc2, g1, lg, lb):
        xh, rstd = _ln_stats(r1)
        x1 = xh * lg + lb
        dx1 = ALPHA * dr2 + dh2 * (1.0 + sc2)
        dr1 = _ln_bwd(dx1 * lg, xh, rstd)
        return ([dr1, (1.0 + g1) * dr1],
                [_colsum(dh2 * x1), _colsum(dh2), _colsum(dx1 * xh), _colsum(dx1), _colsum(dr1 * y)])

    dr1, dy, d_sc2, d_sh2, d_ln1_g, d_ln1_b, d_g1 = _matmul(
        "dh2", du, w_fi, dims=NT, tm=512, tk=D_FF,
        epi=(ln1_bwd, [r1, dr2, y], [sc2, g1, sp["ln1_g"], sp["ln1_b"]], [F32, BF16], [d] * 5))
    dw_out = _matmul("dw_out", y_mix, dy, dims=TN, tk=SEQ_TK, out_dtype=BF16, by_chip="rows")

    def attn_norm_bwd(dyo, o, w):
        r = lax.rsqrt(_rowmean(o * o) + RMS_EPS)
        return [_rms_bwd(dyo * w, o, r)], [_colsum(dyo * o * r)]

    dymix = _matmul("dy_mix_ssm", dy, w_out[:d], dims=NT)
    do, d_attn_w = _matmul("dy_mix_att", dy, w_out[d:], dims=NT, tm=512, epi=(attn_norm_bwd, [o], [sp["attn_norm_w"]], [F32], [d]))

    dxc, ddt_tile, d_alog_l, d_dexp, d_dtb_l, dproj, d_ssm_w = _ssd_bwd(
        xc_all, proj, dt_bias_l, a_log_l, d_exp, prevs, y_ssd, dymix, sp["ssm_norm_w"])
    dqa, dka, dproj, landed = _attn_bwd(qa, ka, vb, o, lse, do, [dw_out, dw_fi, dw_fo], dproj)
    dproj, dcum = _attn_post(dqa, dka, dproj)
    dproj, d_fb_l = _fox_cum_bwd(dcum, proj, f_bias_l, ddt_tile, dproj)
    dpre, d_conv_w, d_conv_b = _conv_bwd_pre(proj, sp["conv_w"], sp["conv_b"], dxc)
    dproj = _conv_bwd_in(dpre, sp["conv_w"], dproj)
    dw_alt = _matmul("dw_in", dproj, h1, dims=TN, tm=1152, tk=SEQ_TK, out_dtype=BF16)
    part_in = _from_aligned(dw_alt).reshape(N_CHIPS, IN_COLS // N_CHIPS, d)

    def last(dh1, x, dr1, sc1):
        return [ALPHA * dr1 + dh1 * (1.0 + sc1)], [_colsum(dh1 * x), _colsum(dh1)]

    chip_in = _pair_sum(part_in, _pair_exchange(part_in), lax.axis_index("c"))
    dx, d_sc1, d_sh1, landed_in = _matmul("dh1", dproj, w_alt, tm=512, tk=AL_COLS, carry=[chip_in],
                                          epi=(last, [x, dr1], [sc1], [F32], [d, d]))

    small = {
        "mod": jnp.concatenate([d_sh1, d_sc1, d_g1, d_sh2, d_sc2, d_g2], axis=1),
        "conv_w": d_conv_w, "conv_b": d_conv_b,
        "dt_bias": d_dtb_l[:, :N_HEADS], "a_log": d_alog_l[:, :N_HEADS],
        "d_skip": jnp.sum(d_dexp.reshape(N_HEADS, HEAD_DIM), axis=1)[None, :],
        "ssm_norm_w": d_ssm_w, "f_bias": d_fb_l[:, F_LANE:F_LANE + N_HEADS], "attn_norm_w": d_attn_w,
        "ln1_g": d_ln1_g, "ln1_b": d_ln1_b, "ln2_g": d_ln2_g, "ln2_b": d_ln2_b, "loss": loss,
    }
    return dx, [landed_in, *landed], small


N_DEV = 8
N_CHIPS = 4
ANY = pl.BlockSpec(memory_space=pl.ANY)
VMEM_SPEC = pl.BlockSpec(memory_space=pltpu.VMEM)


def _place():
    x, y, c = lax.axis_index("x"), lax.axis_index("y"), lax.axis_index("c")
    return x, y, c


def _other_chips(x, y):
    return [(1 - x, y, 2 * (1 - x) + y), (x, 1 - y, 2 * x + 1 - y), (1 - x, 1 - y, 2 * (1 - x) + 1 - y)]


def _small_gather(v_ref, out_ref, send_sems, recv_sems, local_sem, after_start=None):
    x, y, c = _place()
    me = 4 * x + 2 * y + c
    mine = pltpu.make_async_copy(v_ref, out_ref.at[me], local_sem)
    mine.start()
    peers = _peers(x, y, c)

    def copy(rel, slot, to):
        return pltpu.make_async_remote_copy(src_ref=v_ref, dst_ref=out_ref.at[slot], send_sem=send_sems.at[rel],
                                            recv_sem=recv_sems.at[rel], device_id=to, device_id_type=MESH)

    sends = [copy(rel, me, peer) for rel, peer in enumerate(peers)]
    for cp in sends:
        cp.start()
    if after_start is not None:
        after_start()
    for rel, (px, py, pc) in enumerate(peers):
        copy(rel, 4 * px + 2 * py + pc, (x, y, c)).wait_recv()
    for cp in sends:
        cp.wait_send()
    mine.wait()


SMALL_GATHER_SEMS = [pltpu.SemaphoreType.DMA((N_DEV - 1,)), pltpu.SemaphoreType.DMA((N_DEV - 1,)), pltpu.SemaphoreType.DMA]


def _allgather_small(name, v):
    def body(v_ref, out_ref, *sems):
        _small_gather(v_ref, out_ref, *sems)

    return pl.pallas_call(
        body, name=name, out_shape=jax.ShapeDtypeStruct((N_DEV, *v.shape), v.dtype),
        in_specs=[VMEM_SPEC], out_specs=VMEM_SPEC, scratch_shapes=SMALL_GATHER_SEMS,
    )(v)


def _tail_exchange(vec, halves):
    n = len(halves)

    def body(v_ref, *rest):
        ins, (every_ref, total_ref), outs = rest[:n], rest[n:n + 2], rest[n + 2:2 * n + 2]
        gather_sems, (swap_send, swap_recv) = rest[2 * n + 2:2 * n + 5], rest[2 * n + 5:]
        x, y, c = _place()
        swaps = [pltpu.make_async_remote_copy(src_ref=ins[w], dst_ref=outs[w], send_sem=swap_send.at[w], recv_sem=swap_recv.at[w],
                                              device_id=(x, y, 1 - c), device_id_type=MESH) for w in range(n)]

        def start_swaps():
            for cp in swaps:
                cp.start()

        _small_gather(v_ref, every_ref, *gather_sems, after_start=start_swaps)
        acc = every_ref[0]
        for dev in range(1, N_DEV):
            acc = acc + every_ref[dev]
        total_ref[...] = acc
        for cp in swaps:
            cp.wait_recv()
        for cp in swaps:
            cp.wait_send()

    res = pl.pallas_call(
        body, name="tail_exchange",
        out_shape=[jax.ShapeDtypeStruct((N_DEV, *vec.shape), vec.dtype), jax.ShapeDtypeStruct(vec.shape, vec.dtype)]
        + [jax.ShapeDtypeStruct(h.shape, h.dtype) for h in halves],
        in_specs=[VMEM_SPEC] + [ANY] * n, out_specs=[VMEM_SPEC, VMEM_SPEC] + [ANY] * n,
        scratch_shapes=SMALL_GATHER_SEMS + [pltpu.SemaphoreType.DMA((n,)), pltpu.SemaphoreType.DMA((n,))],
    )(vec, *halves)
    return res[0], res[1], res[2:]


def _gather_shards(shard):
    def body(in_ref, out_ref, stage, send_sems, recv_sems, local_sems):
        start, finish = _shard_gather_plan(in_ref, out_ref, stage, send_sems, recv_sems, local_sems)
        start()
        finish()

    return pl.pallas_call(
        body, name="gather_w_in", out_shape=jax.ShapeDtypeStruct((N_CHIPS, *shard.shape), shard.dtype),
        in_specs=[ANY], out_specs=ANY,
        scratch_shapes=[pltpu.VMEM(shard.shape, shard.dtype), pltpu.SemaphoreType.DMA((6,)), pltpu.SemaphoreType.DMA((6,)),
                        pltpu.SemaphoreType.DMA((2,))],
        compiler_params=_params(),
    )(shard)


def _shard_gather_plan(in_ref, out_ref, stage, send_sems, recv_sems, local_sems):
    ch = in_ref.shape[1] // 2
    x, y, c = _place()
    k_me = 2 * x + y
    me, sibling = (x, y, c), (x, y, 1 - c)
    chips = _other_chips(x, y)

    def copy(idx, k, half, to, src=None):
        cols = out_ref.at[k, :, pl.ds(pl.multiple_of(half * ch, ch), ch)]
        return pltpu.make_async_remote_copy(src_ref=cols if src is None else src, dst_ref=cols, send_sem=send_sems.at[idx],
                                            recv_sem=recv_sems.at[idx], device_id=to, device_id_type=MESH)

    mine = in_ref.at[:, pl.ds(pl.multiple_of(c * ch, ch), ch)]
    sends = [copy(j, k_me, c, (cx, cy, c), src=mine) for j, (cx, cy, _) in enumerate(chips)]
    load = pltpu.make_async_copy(in_ref, stage, local_sems.at[0])
    store = pltpu.make_async_copy(stage, out_ref.at[k_me], local_sems.at[1])

    def start():
        for cp in sends:
            cp.start()
        load.start()

    def finish():
        load.wait()
        store.start()
        forwards = []
        for j, (_, _, kj) in enumerate(chips):
            copy(j, kj, c, me).wait_recv()
            forwards.append(copy(3 + j, kj, c, sibling))
            forwards[-1].start()
        for j, (_, _, kj) in enumerate(chips):
            copy(3 + j, kj, 1 - c, me).wait_recv()
        for cp in sends + forwards:
            cp.wait_send()
        store.wait()

    return start, finish


def _peers(x, y, c):
    return [((1 - x) if rel & 4 else x, (1 - y) if rel & 2 else y, (1 - c) if rel & 1 else c) for rel in range(1, N_DEV)]


def _exchange_sems(n):
    return [pltpu.SemaphoreType.DMA((n, N_DEV - 1)), pltpu.SemaphoreType.DMA((n, N_DEV - 1)), pltpu.SemaphoreType.DMA((n,))]


def _gather_plan(ins, outs, send_sems, recv_sems, local_sems):
    x, y, c = _place()
    k_me = 2 * x + y
    peers = [(rel, p) for rel, p in enumerate(_peers(x, y, c)) if (rel + 1) & 6]

    def copy(w, rel, k, half, to, src=None):
        rh = ins[w].shape[0] // 2
        rows = outs[w].at[k, pl.ds(pl.multiple_of(half * rh, rh), rh), :]
        return pltpu.make_async_remote_copy(src_ref=rows if src is None else src, dst_ref=rows, send_sem=send_sems.at[w, rel],
                                            recv_sem=recv_sems.at[w, rel], device_id=to, device_id_type=MESH)

    def mine(w):
        rh = ins[w].shape[0] // 2
        return ins[w].at[pl.ds(pl.multiple_of(c * rh, rh), rh), :]

    n = len(ins)
    local = [pltpu.make_async_copy(ins[w], outs[w].at[k_me], local_sems.at[w]) for w in range(n)]
    sends = [copy(w, rel, k_me, c, peer, src=mine(w)) for w in range(n) for rel, peer in peers]

    def start():
        for cp in local + sends:
            cp.start()

    def finish():
        for w in range(n):
            for rel, (px, py, pc) in peers:
                copy(w, rel, 2 * px + py, pc, (x, y, c)).wait_recv()
        for cp in sends:
            cp.wait_send()
        for cp in local:
            cp.wait()

    return start, finish


def _reduce_plan(ins, outs, send_sems, recv_sems, local_sems):
    x, y, c = _place()
    me = 4 * x + 2 * y + c
    peers = _peers(x, y, c)

    def block(w, k, half):
        rh = ins[w].shape[1] // 2
        return ins[w].at[k, pl.ds(pl.multiple_of(half * rh, rh), rh), :]

    def copy(w, rel, src, slot, to):
        return pltpu.make_async_remote_copy(src_ref=src, dst_ref=outs[w].at[slot], send_sem=send_sems.at[w, rel],
                                            recv_sem=recv_sems.at[w, rel], device_id=to, device_id_type=MESH)

    n = len(ins)
    local = [pltpu.make_async_copy(block(w, 2 * x + y, c), outs[w].at[me], local_sems.at[w]) for w in range(n)]
    sends = [copy(w, rel, block(w, 2 * px + py, pc), me, (px, py, pc)) for w in range(n) for rel, (px, py, pc) in enumerate(peers)]

    def start():
        for cp in local + sends:
            cp.start()

    def finish():
        for w in range(n):
            for rel, (px, py, pc) in enumerate(peers):
                copy(w, rel, block(w, 2 * x + y, c), 4 * px + 2 * py + pc, (x, y, c)).wait_recv()
        for cp in sends:
            cp.wait_send()
        for cp in local:
            cp.wait()

    return start, finish


def _scatter_plan(ins, outs, send_sems, recv_sems, local_sems):
    x, y, c = _place()
    k_me = 2 * x + y
    chips = _other_chips(x, y)

    def copy(w, j, src_k, dst_k, to):
        return pltpu.make_async_remote_copy(src_ref=ins[w].at[src_k], dst_ref=outs[w].at[dst_k], send_sem=send_sems.at[w, j],
                                            recv_sem=recv_sems.at[w, j], device_id=to, device_id_type=MESH)

    n = len(ins)
    local = [pltpu.make_async_copy(ins[w].at[k_me], outs[w].at[k_me], local_sems.at[w]) for w in range(n)]
    sends = [copy(w, j, kj, k_me, (cx, cy, c)) for w in range(n) for j, (cx, cy, kj) in enumerate(chips)]

    def start():
        for cp in local + sends:
            cp.start()

    def finish():
        for w in range(n):
            for j, (_, _, kj) in enumerate(chips):
                copy(w, j, k_me, kj, (x, y, c)).wait_recv()
        for cp in sends:
            cp.wait_send()
        for cp in local:
            cp.wait()

    return start, finish


def _row_tile(r, mult=2 * SUBLANES):
    if r % 256 == 0:
        return 256
    return max([t for t in range(mult, 513, mult) if r % t == 0], default=r)


def _pair_exchange(g):
    _, r, cdim = g.shape
    ch = cdim // 2

    def body(g_ref, got_ref, send_sem, recv_sem):
        x, y, c = _place()
        cp = pltpu.make_async_remote_copy(src_ref=g_ref.at[:, :, pl.ds(pl.multiple_of((1 - c) * ch, ch), ch)], dst_ref=got_ref,
                                          send_sem=send_sem, recv_sem=recv_sem, device_id=(x, y, 1 - c), device_id_type=MESH)
        cp.start()
        cp.wait_recv()
        cp.wait_send()

    return pl.pallas_call(
        body, name="pair_exchange", out_shape=jax.ShapeDtypeStruct((N_CHIPS, r, ch), g.dtype),
        in_specs=[ANY], out_specs=ANY, scratch_shapes=[pltpu.SemaphoreType.DMA, pltpu.SemaphoreType.DMA],
    )(g)


def _pair_sum(g, got, c):
    _, r, cdim = g.shape
    ch = cdim // 2
    tr = _row_tile(r)

    def body(c_ref, g_ref, got_ref, o_ref):
        o_ref[...] = (g_ref[...].astype(F32) + got_ref[...].astype(F32)).astype(o_ref.dtype)

    blk = pl.BlockSpec((1, tr, ch), lambda k, i, c_ref: (k, i, 0))
    return pl.pallas_call(
        body, name="pair_sum",
        grid_spec=pltpu.PrefetchScalarGridSpec(
            num_scalar_prefetch=1, grid=(N_CHIPS, r // tr),
            in_specs=[pl.BlockSpec((1, tr, ch), lambda k, i, c_ref: (k, i, c_ref[0])), blk], out_specs=blk),
        out_shape=jax.ShapeDtypeStruct((N_CHIPS, r, ch), BF16),
        compiler_params=_params(("parallel", "parallel")),
    )(jnp.reshape(c, (1,)).astype(jnp.int32), g, got)


def _sum_blocks(name, parts):
    k, r, cdim = parts.shape
    tr = _row_tile(r)

    def body(p_ref, o_ref):
        acc = p_ref[0].astype(F32)
        for i in range(1, k):
            acc = acc + p_ref[i].astype(F32)
        o_ref[...] = acc

    return pl.pallas_call(
        body, name=name, grid=(r // tr,),
        in_specs=[pl.BlockSpec((k, tr, cdim), lambda i: (0, i, 0))], out_specs=pl.BlockSpec((tr, cdim), lambda i: (i, 0)),
        out_shape=jax.ShapeDtypeStruct((r, cdim), F32), compiler_params=_params(("parallel",)),
    )(parts)


ADA_SHARD = 6 * D_MODEL // N_CHIPS


def _mod_part(c_all, w_shard, b_shard):
    tn = 512

    def body(c_ref, w_ref, b_ref, o_ref):
        o_ref[...] = _dot(_silu(c_ref[...]).astype(BF16), w_ref[...].astype(BF16)) + b_ref[...]

    return pl.pallas_call(
        body, name="mod_part", grid=(ADA_SHARD // tn,),
        in_specs=[pl.BlockSpec((N_DEV, D_MODEL), lambda j: (0, 0)), pl.BlockSpec((D_MODEL, tn), lambda j: (0, j)),
                  pl.BlockSpec((1, tn), lambda j: (0, j))],
        out_specs=pl.BlockSpec((N_DEV, tn), lambda j: (0, j)),
        out_shape=jax.ShapeDtypeStruct((N_DEV, ADA_SHARD), F32), compiler_params=_params(("parallel",)),
    )(c_all, w_shard, b_shard)


def _w_ada_grad(c_all_t, dmod_shard):
    tm = 256

    def body(ct_ref, dm_ref, o_ref):
        act = _silu(ct_ref[...])
        acc = act[:, 0:1] * dm_ref[0:1, :]
        for dev in range(1, N_DEV):
            acc = acc + act[:, dev:dev + 1] * dm_ref[dev:dev + 1, :]
        o_ref[...] = acc

    return pl.pallas_call(
        body, name="w_ada_grad", grid=(D_MODEL // tm,),
        in_specs=[pl.BlockSpec((tm, N_DEV), lambda i: (i, 0)), pl.BlockSpec((N_DEV, ADA_SHARD), lambda i: (0, 0))],
        out_specs=pl.BlockSpec((tm, ADA_SHARD), lambda i: (i, 0)),
        out_shape=jax.ShapeDtypeStruct((D_MODEL, ADA_SHARD), F32), compiler_params=_params(("parallel",)),
    )(c_all_t, dmod_shard)


def _adamw_math(w, g, m, v):
    nm = ADAM_B1 * m + (1.0 - ADAM_B1) * g
    nv = ADAM_B2 * v + (1.0 - ADAM_B2) * jnp.square(g)
    m_hat = nm / (1.0 - ADAM_B1 ** ADAM_STEP)
    v_hat = nv / (1.0 - ADAM_B2 ** ADAM_STEP)
    return -ADAM_LR * (m_hat / (jnp.sqrt(v_hat) + ADAM_EPS) + ADAM_WD * w), nm, nv


def _adamw(name, w, g, m, v):
    _, r, cdim = w.shape
    tr = 256 if r % 256 == 0 else r

    def body(w_ref, g_ref, m_ref, v_ref, go_ref, d_ref, nm_ref, nv_ref):
        go_ref[...] = g_ref[...]
        d_ref[...], nm_ref[...], nv_ref[...] = _adamw_math(w_ref[...], g_ref[...], m_ref[...], v_ref[...])

    blk = pl.BlockSpec((None, tr, cdim), lambda i: (0, i, 0))
    return pl.pallas_call(
        body, name=name, grid=(r // tr,), in_specs=[blk, pl.BlockSpec((tr, cdim), lambda i: (i, 0)), blk, blk], out_specs=[blk] * 4,
        out_shape=[jax.ShapeDtypeStruct((1, r, cdim), F32)] * 4, compiler_params=_params(("parallel",)),
    )(w, g, m, v)


def _adamw_pair(name, w, mine, other, m, v, c, by_cols=False):
    _, r, cdim = w.shape
    hr, hc = mine.shape
    tr = _row_tile(hr, SUBLANES)
    per = hr // tr

    def body(c_ref, w_ref, a_ref, b_ref, m_ref, v_ref, g_ref, d_ref, nm_ref, nv_ref):
        half = pl.program_id(1) if by_cols else pl.program_id(0) // per
        g = jnp.where(half == c_ref[0], a_ref[...], b_ref[...])
        g_ref[...] = g
        d_ref[...], nm_ref[...], nv_ref[...] = _adamw_math(w_ref[...], g, m_ref[...], v_ref[...])

    blk = pl.BlockSpec((None, tr, hc), lambda i, j, c_ref: (0, i, j))
    half = pl.BlockSpec((tr, hc), lambda i, j, c_ref: (i % per, 0))
    return pl.pallas_call(
        body, name=name,
        grid_spec=pltpu.PrefetchScalarGridSpec(num_scalar_prefetch=1, grid=(r // tr, cdim // hc),
                                               in_specs=[blk, half, half, blk, blk], out_specs=[blk] * 4),
        out_shape=[jax.ShapeDtypeStruct((1, r, cdim), F32)] * 4, compiler_params=_params(("parallel", "parallel")),
    )(jnp.reshape(c, (1,)).astype(jnp.int32), w, mine, other, m, v)


SMALL = ["b_ada", "conv_b", "dt_bias", "a_log", "d_skip", "ssm_norm_w", "f_bias", "attn_norm_w", "ln1_g", "ln1_b", "ln2_g", "ln2_b"]


def _pack(vs):
    pieces = []
    for v in vs:
        pieces.append(v)
        if v.shape[1] % LANES:
            pieces.append(jnp.zeros((1, -v.shape[1] % LANES), v.dtype))
    return jnp.concatenate(pieces, axis=1)


def _adamw_small(total, offs, ws, ms, vs):
    n = len(ws)

    def body(*refs):
        t_ref, outs = refs[0], refs[1 + 3 * n:]
        for i in range(n):
            g = t_ref[:, offs[i]:offs[i] + ws[i].shape[1]]
            dl, nm, nv = _adamw_math(refs[1 + i][...], g, refs[1 + n + i][...], refs[1 + 2 * n + i][...])
            outs[4 * i][...], outs[4 * i + 1][...], outs[4 * i + 2][...], outs[4 * i + 3][...] = g, dl, nm, nv

    res = pl.pallas_call(
        body, name="adamw_small", in_specs=[VMEM_SPEC] * (1 + 3 * n), out_specs=[VMEM_SPEC] * (4 * n),
        out_shape=[jax.ShapeDtypeStruct(w.shape, F32) for w in ws for _ in range(4)],
    )(total, *ws, *ms, *vs)
    return [res[4 * i:4 * i + 4] for i in range(n)]


def kernel(x, c, w_ada, b_ada, w_in, conv_w, conv_b, dt_bias, a_log, d_skip, ssm_norm_w, f_bias, attn_norm_w, w_out, ln1_g, ln1_b, w_ff_in, w_ff_out, ln2_g, ln2_b, loss_target, m_w_ada, m_b_ada, m_w_in, m_conv_w, m_conv_b, m_dt_bias, m_a_log, m_d_skip, m_ssm_norm_w, m_f_bias, m_attn_norm_w, m_w_out, m_ln1_g, m_ln1_b, m_w_ff_in, m_w_ff_out, m_ln2_g, m_ln2_b, v_w_ada, v_b_ada, v_w_in, v_conv_w, v_conv_b, v_dt_bias, v_a_log, v_d_skip, v_ssm_norm_w, v_f_bias, v_attn_norm_w, v_w_out, v_ln1_g, v_ln1_b, v_w_ff_in, v_w_ff_out, v_ln2_g, v_ln2_b):
    a = dict(b_ada=b_ada, conv_b=conv_b, dt_bias=dt_bias, a_log=a_log, d_skip=d_skip, ssm_norm_w=ssm_norm_w, f_bias=f_bias,
             attn_norm_w=attn_norm_w, ln1_g=ln1_g, ln1_b=ln1_b, ln2_g=ln2_g, ln2_b=ln2_b)
    ms = dict(b_ada=m_b_ada, conv_b=m_conv_b, dt_bias=m_dt_bias, a_log=m_a_log, d_skip=m_d_skip, ssm_norm_w=m_ssm_norm_w,
              f_bias=m_f_bias, attn_norm_w=m_attn_norm_w, ln1_g=m_ln1_g, ln1_b=m_ln1_b, ln2_g=m_ln2_g, ln2_b=m_ln2_b)
    vs = dict(b_ada=v_b_ada, conv_b=v_conv_b, dt_bias=v_dt_bias, a_log=v_a_log, d_skip=v_d_skip, ssm_norm_w=v_ssm_norm_w,
              f_bias=v_f_bias, attn_norm_w=v_attn_norm_w, ln1_g=v_ln1_g, ln1_b=v_ln1_b, ln2_g=v_ln2_g, ln2_b=v_ln2_b)
    xi, yi, ci = _place()
    chip = 2 * xi + yi
    me = 4 * xi + 2 * yi + ci
    d = D_MODEL
    conv_shard = CONV_DIM // N_CHIPS

    first = _allgather_small("gather_c", jnp.concatenate([c, conv_w[0].reshape(1, CONV_W * conv_shard)], axis=1))[:, 0]
    c_all = first[:, :d]
    conv_w_full = first[::2, d:].reshape(N_CHIPS, CONV_W, conv_shard).transpose(1, 0, 2).reshape(CONV_W, CONV_DIM)
    b_shard = lax.dynamic_slice_in_dim(b_ada, chip * ADA_SHARD, ADA_SHARD, axis=1)
    parts = _allgather_small("gather_mod", _mod_part(c_all, w_ada[0], b_shard))
    mod = lax.dynamic_index_in_dim(parts[::2], me, axis=1, keepdims=False).reshape(1, 6 * d)

    w_in_t, m_w_in_t, v_w_in_t = [jnp.transpose(t, (0, 2, 1)) for t in (w_in, m_w_in, v_w_in)]
    w_alt = _to_aligned(_gather_shards(w_in_t[0].astype(BF16)).reshape(IN_COLS, d))

    sp = {n: a[n] for n in SMALL[1:]}
    sp["conv_w"] = conv_w_full
    shards = [w_out[0].astype(BF16), w_ff_in[0].astype(BF16), w_ff_out[0].astype(BF16)]
    dx, landed, small = _local_step(x[0], loss_target[0], mod, w_alt, shards, sp)

    mine = [_sum_blocks("dev_sum_%d" % i, p) for i, p in enumerate(landed)]
    names = ["mod"] + SMALL[1:]
    vec = _pack([small[n] for n in names] + [small["conv_w"].reshape(1, CONV_W * CONV_DIM), small["loss"]])
    every, total, other = _tail_exchange(vec, mine)
    widths = [6 * d] + [a[n].shape[1] for n in SMALL[1:]]
    offs = [0]
    for w in widths:
        offs.append(offs[-1] + w + (-w % LANES))
    g_conv_w_full = total[:, offs[-1]:offs[-1] + CONV_W * CONV_DIM].reshape(CONV_W, CONV_DIM)
    loss = total[0, offs[-1] + CONV_W * CONV_DIM]
    dmod_shard = lax.dynamic_slice_in_dim(every[:, 0, :6 * d], chip * ADA_SHARD, ADA_SHARD, axis=1)
    g_w_ada = _w_ada_grad(c_all.T, dmod_shard)
    g_conv_w = lax.dynamic_slice_in_dim(g_conv_w_full, chip * conv_shard, conv_shard, axis=1)

    grads, deltas, new_m, new_v = {}, {}, {}, {}
    paired = dict(w_in=(w_in_t, m_w_in_t, v_w_in_t), w_out=(w_out, m_w_out, v_w_out), w_ff_in=(w_ff_in, m_w_ff_in, v_w_ff_in),
                  w_ff_out=(w_ff_out, m_w_ff_out, v_w_ff_out))
    for i, (n, (w, m, v)) in enumerate(paired.items()):
        res = _adamw_pair("adamw_" + n, w, mine[i], other[i], m, v, ci, by_cols=n == "w_in")
        grads[n], deltas[n], new_m[n], new_v[n] = [jnp.transpose(t, (0, 2, 1)) for t in res] if n == "w_in" else res
    for n, g, (w, m, v) in (("w_ada", g_w_ada, (w_ada, m_w_ada, v_w_ada)), ("conv_w", g_conv_w, (conv_w, m_conv_w, v_conv_w))):
        grads[n], deltas[n], new_m[n], new_v[n] = _adamw("adamw_" + n, w, g, m, v)
    for n, res in zip(SMALL, _adamw_small(total, offs, [a[n] for n in SMALL], [ms[n] for n in SMALL], [vs[n] for n in SMALL])):
        grads[n], deltas[n], new_m[n], new_v[n] = res

    order = ["w_ada", "b_ada", "w_in", "conv_w", "conv_b", "dt_bias", "a_log", "d_skip", "ssm_norm_w", "f_bias", "attn_norm_w", "w_out",
             "ln1_g", "ln1_b", "w_ff_in", "w_ff_out", "ln2_g", "ln2_b"]
    return (loss, dx[None], *[grads[n] for n in order], *[deltas[n] for n in order], *[new_m[n] for n in order], *[new_v[n] for n in order])
```

```python
import functools

import jax
import jax.numpy as jnp
from jax import lax
from jax.experimental import pallas as pl
from jax.experimental.pallas import tpu as pltpu

F32, BF16 = jnp.float32, jnp.bfloat16

D_MODEL = 1024
N_HEADS = 16
HEAD_DIM = 64
N_PAIRS = N_HEADS // 2
SSM_GROUPS = 2
SSM_STATE = 128
CHUNK = 128
CONV_W = 4
CONV_DIM = 1536
D_FF = 4096
IN_COLS = 5664
ALPHA = 2.0 ** 0.25
LN_EPS = 1e-5
RMS_EPS = 1e-5
LANES = 128
SUBLANES = 8

AL_Z, AL_XS, AL_Q, AL_K, AL_V, AL_B, AL_C, AL_DTF = 0, 1024, 2048, 3072, 4096, 5120, 5376, 5632
AL_COLS = 5760
F_LANE = 16

ADAM_LR, ADAM_B1, ADAM_B2, ADAM_EPS, ADAM_WD, ADAM_STEP = 0.001, 0.9, 0.999, 1e-08, 0.01, 10

VMEM_LIMIT = 56 * 1024 * 1024
SEQ_TK = 4096
MESH = pl.DeviceIdType.MESH


def _params(sem=None):
    return pltpu.CompilerParams(dimension_semantics=sem, vmem_limit_bytes=VMEM_LIMIT)


def _sigmoid(x):
    return 1.0 / (1.0 + jnp.exp(-x))


def _silu(x):
    return x * _sigmoid(x)


def _softplus(x):
    return jnp.maximum(x, 0.0) + jnp.log(1.0 + jnp.exp(-jnp.abs(x)))


def _split3(a):
    hi = a.astype(BF16)
    r = a - hi.astype(F32)
    mid = r.astype(BF16)
    lo = (r - mid.astype(F32)).astype(BF16)
    return hi, mid, lo


def _dot(a, b, dims=((1,), (0,))):
    return lax.dot_general(a, b, (dims, ((), ())), preferred_element_type=F32)


NN, NT, TN = ((1,), (0,)), ((1,), (1,)), ((0,), (0,))


def _dot3(t, a):
    hi, mid, lo = _split3(a)
    return _dot(t, hi) + _dot(t, mid) + _dot(t, lo)


def _matmul(name, a, b, *, dims=NN, out_dtype=F32, tm=1024, tn=1024, tk=1024, by_chip=None, epi=None, carry=()):
    if dims == NN:
        (m, k), n = a.shape, b.shape[1]
    elif dims == NT:
        (m, k), n = a.shape, b.shape[0]
    else:
        (k, m), n = a.shape, b.shape[1]
    if by_chip == "rows":
        tm = min(tm, m // 4)
    if by_chip == "cols":
        tn = min(tn, n // 4)
    tm, tn, tk = min(tm, m), min(tn, n), min(tk, k)
    assert m % tm == 0 and n % tn == 0 and k % tk == 0, (name, m, n, k, tm, tn, tk)
    nk = k // tk
    if by_chip == "rows":
        per = m // 4 // tm
        out_spec = pl.BlockSpec((None, tm, tn), lambda i, j, l: (i // per, i % per, j))
        out_shape = jax.ShapeDtypeStruct((4, m // 4, n), out_dtype)
    elif by_chip == "cols":
        per = n // 4 // tn
        out_spec = pl.BlockSpec((None, tm, tn), lambda i, j, l: (j // per, i, j % per))
        out_shape = jax.ShapeDtypeStruct((4, m, n // 4), out_dtype)
    else:
        out_spec = pl.BlockSpec((tm, tn), lambda i, j, l: (i, j))
        out_shape = jax.ShapeDtypeStruct((m, n), out_dtype)
    a_spec = pl.BlockSpec((tk, tm), lambda i, j, l: (l, i)) if dims == TN else pl.BlockSpec((tm, tk), lambda i, j, l: (i, l))
    b_spec = pl.BlockSpec((tn, tk), lambda i, j, l: (j, l)) if dims == NT else pl.BlockSpec((tk, tn), lambda i, j, l: (l, j))

    tile = pl.BlockSpec((tm, tn), lambda i, j, l: (i, j))
    in_specs, args, out_specs, out_shape = [a_spec, b_spec], [a, b], [out_spec], [out_shape]
    fn, n_tiles, n_sums = None, 1, 0
    if epi is not None:
        fn, fulls, vecs, outs, sums = epi
        assert by_chip is None and (not sums or n == tn), name
        in_specs = in_specs + [tile] * len(fulls) + [pl.BlockSpec((1, tn), lambda i, j, l: (0, j))] * len(vecs)
        args = args + list(fulls) + list(vecs)
        flipped = pl.BlockSpec((tn, tm), lambda i, j, l: (j, i))
        out_specs = [flipped if isinstance(dt, tuple) else tile for dt in outs] + [pl.BlockSpec((1, w), lambda i, j, l: (0, 0)) for w in sums]
        out_shape = [jax.ShapeDtypeStruct((n, m), dt[1]) if isinstance(dt, tuple) else jax.ShapeDtypeStruct((m, n), dt) for dt in outs]
        out_shape += [jax.ShapeDtypeStruct((1, w), F32) for w in sums]
        n_tiles, n_sums = len(outs), len(sums)
    n_in, n_out, n_c = len(args), len(out_specs), len(carry)
    scratch = [pltpu.VMEM((tm, tn) if nk > 1 else (SUBLANES, LANES), F32)]
    if n_c:
        in_specs, args = in_specs + [ANY] * n_c, args + list(carry)
        out_specs = out_specs + [ANY] * n_c
        out_shape = out_shape + [jax.ShapeDtypeStruct(g.shape, g.dtype) for g in carry]
        scratch = scratch + _exchange_sems(n_c)
    gm, gn = m // tm, n // tn

    def body(*refs):
        a_ref, b_ref = refs[:2]
        ins, outs = refs[2:n_in], refs[n_in + n_c:n_in + n_c + n_out]
        acc_ref = refs[n_in + 2 * n_c + n_out]
        i, j, l = pl.program_id(0), pl.program_id(1), pl.program_id(2)
        if n_c:
            start, wait = _scatter_plan(refs[n_in:n_in + n_c], refs[n_in + n_c + n_out:n_in + 2 * n_c + n_out], *refs[n_in + 2 * n_c + n_out + 1:])
            pl.when((i == 0) & (j == 0) & (l == 0))(start)
        part = _dot(a_ref[...].astype(BF16), b_ref[...].astype(BF16), dims)

        def finish(res):
            if fn is None:
                outs[0][...] = res.astype(outs[0].dtype)
                return
            tiles, colsums = fn(res, *[r[...] for r in ins])
            for r, val in zip(outs[:n_tiles], tiles):
                r[...] = val.astype(r.dtype)
            if n_sums:
                @pl.when(i == 0)
                def _():
                    for r in outs[n_tiles:]:
                        r[...] = jnp.zeros_like(r)
                for r, val in zip(outs[n_tiles:], colsums):
                    r[...] += val

        if nk == 1:
            finish(part)
        else:
            @pl.when(l == 0)
            def _():
                acc_ref[...] = part

            @pl.when((l > 0) & (l < nk - 1))
            def _():
                acc_ref[...] += part

            @pl.when(l == nk - 1)
            def _():
                finish(acc_ref[...] + part)

        if n_c:
            pl.when((i == gm - 1) & (j == gn - 1) & (l == nk - 1))(wait)

    res = pl.pallas_call(
        body, name=name, grid=(gm, gn, nk),
        in_specs=in_specs, out_specs=out_specs, out_shape=out_shape, scratch_shapes=scratch,
        compiler_params=_params(("arbitrary",) * 3 if n_c or n_sums else ("parallel", "parallel", "arbitrary")),
    )(*args)
    return res[0] if len(res) == 1 else res


def _rowwise(name, fn, fulls, vecs, out_fulls, out_vecs, tr=256):
    fulls = [f if isinstance(f, tuple) else (f, f.shape[1], 0) for f in fulls]
    s = fulls[0][0].shape[0]
    tr = min(tr, s)
    out_fulls = [o if len(o) == 3 else (*o, (o[0], 0, None)) for o in out_fulls]
    into = [(k, slab[2]) for k, (_, _, slab) in enumerate(out_fulls) if slab[2] is not None]
    nf, nv, nof, nov = len(fulls), len(vecs), len(out_fulls), len(out_vecs)
    in_specs = [pl.BlockSpec((tr, w), functools.partial(lambda i, cb: (i, cb), cb=cb)) for (_, w, cb) in fulls]
    in_specs += [pl.BlockSpec(v.shape, lambda i: (0, 0)) for v in vecs] + [ANY] * len(into)
    out_shape = [jax.ShapeDtypeStruct((s, slab[0]), dt) for (_, dt, slab) in out_fulls] + [jax.ShapeDtypeStruct((1, w), F32) for w in out_vecs]
    out_specs = [pl.BlockSpec((tr, w), functools.partial(lambda i, cb: (i, cb), cb=slab[1])) for (w, _, slab) in out_fulls]
    out_specs += [pl.BlockSpec((1, w), lambda i: (0, 0)) for w in out_vecs]

    def body(*refs):
        outs = refs[nf + nv + len(into):]
        of, ov = fn(*[r[...] for r in refs[:nf + nv]])
        for r, val in zip(outs[:nof], of):
            r[...] = val.astype(r.dtype)
        if nov:
            @pl.when(pl.program_id(0) == 0)
            def _():
                for r in outs[nof:]:
                    r[...] = jnp.zeros_like(r)
            for r, val in zip(outs[nof:], ov):
                r[...] += val

    res = pl.pallas_call(
        body, name=name, grid=(s // tr,), in_specs=in_specs, out_specs=out_specs, out_shape=out_shape,
        input_output_aliases={nf + nv + pos: k for pos, (k, _) in enumerate(into)},
        compiler_params=_params(("arbitrary",)),
    )(*[f[0] for f in fulls], *vecs, *[buf for _, buf in into])
    return res[:nof], res[nof:]


def _colsum(x):
    return jnp.sum(x, axis=0, keepdims=True)


def _rowmean(x):
    return jnp.mean(x, axis=-1, keepdims=True)


CONV_CB = 512
CONV_TR = 512


def _shift_down(u, halo, j):
    if j == 0:
        return u
    ru = pltpu.roll(u, j, 0)
    row8 = lax.broadcasted_iota(jnp.int32, halo.shape, 0)
    top = jnp.where(row8 < j, pltpu.roll(halo, j, 0), ru[:SUBLANES])
    if u.shape[0] == SUBLANES:
        return top
    return jnp.concatenate([top, ru[SUBLANES:]], axis=0)


def _shift_up(d, halo, j):
    if j == 0:
        return d
    tr = d.shape[0]
    rd = pltpu.roll(d, tr - j, 0)
    row8 = lax.broadcasted_iota(jnp.int32, halo.shape, 0)
    bot = jnp.where(row8 >= SUBLANES - j, pltpu.roll(halo, SUBLANES - j, 0), rd[tr - SUBLANES:])
    return jnp.concatenate([rd[:tr - SUBLANES], bot], axis=0)


def _conv_col(cb):
    return jnp.where(cb < 2, AL_XS // CONV_CB + cb, AL_B // CONV_CB)


def _conv_specs(s, tr):
    per8 = tr // SUBLANES
    blk = pl.BlockSpec((tr, CONV_CB), lambda cb, i: (i, _conv_col(cb)))
    prev = pl.BlockSpec((SUBLANES, CONV_CB), lambda cb, i: (jnp.maximum(i * per8 - 1, 0), _conv_col(cb)))
    return blk, prev


def _conv_pre(u, halo, w_ref, b_ref, first):
    halo = jnp.where(first, 0.0, halo)
    acc = b_ref[...] + w_ref[CONV_W - 1:CONV_W, :] * u
    shifted = [u]
    for j in range(1, CONV_W):
        sh = _shift_down(u, halo, j)
        shifted.append(sh)
        acc = acc + w_ref[CONV_W - 1 - j:CONV_W - j, :] * sh
    return acc, shifted


def _silu_grad(pre):
    sg = _sigmoid(pre)
    return sg * (1.0 + pre * (1.0 - sg))


def _conv_bwd(proj, conv_w, conv_b, dxc, dproj):
    s = proj.shape[0]
    tr = min(CONV_TR, s)
    nb = s // tr
    per8 = tr // SUBLANES
    last8 = s // SUBLANES - 1
    blk, prev = _conv_specs(s, tr)

    def body(u_ref, h_ref, un_ref, w_ref, b_ref, d_ref, dn_ref, _, du_ref, dw_ref, db_ref):
        i = pl.program_id(1)
        u = u_ref[...]
        pre, shifted = _conv_pre(u, h_ref[...], w_ref, b_ref, i == 0)
        dpre = d_ref[...] * _silu_grad(pre)
        pre_n, _ = _conv_pre(un_ref[...], u[tr - SUBLANES:], w_ref, b_ref, False)
        dpre_n = jnp.where(i == nb - 1, 0.0, dn_ref[...] * _silu_grad(pre_n))
        acc = w_ref[CONV_W - 1:CONV_W, :] * dpre
        for j in range(1, CONV_W):
            acc = acc + w_ref[CONV_W - 1 - j:CONV_W - j, :] * _shift_up(dpre, dpre_n, j)
        du_ref[...] = acc.astype(du_ref.dtype)

        @pl.when(i == 0)
        def _():
            dw_ref[...] = jnp.zeros_like(dw_ref)
            db_ref[...] = jnp.zeros_like(db_ref)

        db_ref[...] += _colsum(dpre)
        for j in range(CONV_W):
            dw_ref[CONV_W - 1 - j:CONV_W - j, :] += _colsum(dpre * shifted[j])

    nxt_row = lambda i: jnp.minimum((i + 1) * per8, last8)
    own = pl.BlockSpec((tr, CONV_CB), lambda cb, i: (i, cb))
    wspec = pl.BlockSpec((CONV_W, CONV_CB), lambda cb, i: (0, cb))
    bspec = pl.BlockSpec((1, CONV_CB), lambda cb, i: (0, cb))
    return pl.pallas_call(
        body, name="conv_bwd", grid=(CONV_DIM // CONV_CB, nb),
        in_specs=[blk, prev, pl.BlockSpec((SUBLANES, CONV_CB), lambda cb, i: (nxt_row(i), _conv_col(cb))), wspec, bspec, own,
                  pl.BlockSpec((SUBLANES, CONV_CB), lambda cb, i: (nxt_row(i), cb)), ANY],
        out_specs=[pl.BlockSpec((tr, CONV_CB), lambda cb, i: (i, _conv_col(cb))), wspec, bspec],
        out_shape=[jax.ShapeDtypeStruct(dproj.shape, dproj.dtype), jax.ShapeDtypeStruct((CONV_W, CONV_DIM), F32),
                   jax.ShapeDtypeStruct((1, CONV_DIM), F32)],
        input_output_aliases={7: 0}, compiler_params=_params(("parallel", "arbitrary")),
    )(proj, proj, proj, conv_w, conv_b, dxc, dxc, dproj)


XC_B, XC_C = 1024, 1280


def _tile_iotas():
    row = lax.broadcasted_iota(jnp.int32, (CHUNK, LANES), 0)
    lane = lax.broadcasted_iota(jnp.int32, (CHUNK, LANES), 1)
    return row, lane


def _ssd_scalars(dtf_ref, bias_ref, alog_ref, row, lane):
    head = lane[:1] < N_HEADS
    raw = dtf_ref[...] + bias_ref[...]
    dt = _softplus(raw)
    a_neg = jnp.where(head, -jnp.exp(alog_ref[...]), 0.0)
    a = dt * a_neg
    tril = (row >= lane).astype(BF16)
    s = _dot3(tril, a)
    return raw, dt, a_neg, s


def _pair(v, j, lo):
    return jnp.where(lo, v[:, 2 * j:2 * j + 1], v[:, 2 * j + 1:2 * j + 2])


def _head_sum(x, lo, hh):
    return jnp.sum(jnp.where(lo == (hh == 0), x, 0.0), axis=1, keepdims=True)


def _decay_masks(s, st, h, row, lane):
    s_col = jnp.broadcast_to(s[:, h:h + 1], (CHUNK, LANES))
    s_row = jnp.broadcast_to(st[h:h + 1, :], (CHUNK, LANES))
    lm = jnp.where(row >= lane, jnp.exp(s_col - s_row), 0.0)
    lmt = jnp.where(row <= lane, jnp.exp(s_row - s_col), 0.0)
    return lm, lmt


def _gated_norm(y, z, w):
    g = y * _silu(z)
    return g * lax.rsqrt(_rowmean(g * g) + RMS_EPS) * w


def _ssd_fwd(proj, conv_w, conv_b, dt_bias_l, a_log_l, d_exp, norm_w):
    s_len = proj.shape[0]
    nc = s_len // CHUNK

    def body(xs_ref, bc_ref, cw_ref, cb_ref, dtf_ref, bias_ref, alog_ref, dexp_ref, z_ref, w_ref,
             x_ref, y_ref, prevs_ref, ymix_ref, state_ref, halo_ref):
        first = pl.program_id(0) == 0

        @pl.when(first)
        def _():
            state_ref[...] = jnp.zeros_like(state_ref)
            halo_ref[...] = jnp.zeros_like(halo_ref)

        u = jnp.concatenate([xs_ref[...], bc_ref[...]], axis=1)
        pre, _ = _conv_pre(u, halo_ref[...], cw_ref, cb_ref, first)
        halo_ref[...] = u[CHUNK - SUBLANES:]
        x_ref[...] = _silu(pre)

        row, lane = _tile_iotas()
        lo = lane < HEAD_DIM
        _, dt, _, s = _ssd_scalars(dtf_ref, bias_ref, alog_ref, row, lane)
        tot = s[CHUNK - 1:CHUNK, :]
        st = s.T
        for g in range(SSM_GROUPS):
            bg = x_ref[:, XC_B + g * SSM_STATE:XC_B + (g + 1) * SSM_STATE].astype(BF16)
            cg = x_ref[:, XC_C + g * SSM_STATE:XC_C + (g + 1) * SSM_STATE].astype(BF16)
            cb = _dot(cg, bg, NT)
            for j in range(g * 4, g * 4 + 4):
                xs_p = x_ref[:, j * LANES:(j + 1) * LANES]
                dt_p, s_p, tot_p = _pair(dt, j, lo), _pair(s, j, lo), _pair(tot, j, lo[:1])
                xc_p = xs_p * dt_p
                xc_b = xc_p.astype(BF16)
                yd = []
                for hh in range(2):
                    lm, _ = _decay_masks(s, st, 2 * j + hh, row, lane)
                    yd.append(_dot((cb * lm).astype(BF16), xc_b))
                prev = state_ref[j]
                prevs_ref[0, j] = prev
                yo = _dot(cg, prev.astype(BF16)) * jnp.exp(s_p)
                y_ref[:, j * LANES:(j + 1) * LANES] = jnp.where(lo, yd[0], yd[1]) + yo + dexp_ref[:, j * LANES:(j + 1) * LANES] * xs_p
                to_end = jnp.exp(tot_p - s_p)
                state_ref[j] = jnp.exp(tot_p) * prev + _dot(bg, (xc_p * to_end).astype(BF16), TN)
        ymix_ref[...] = _gated_norm(y_ref[...], z_ref[...], w_ref[...]).astype(ymix_ref.dtype)

    vec = lambda w: pl.BlockSpec((1, w), lambda c: (0, 0))
    rows = pl.BlockSpec((CHUNK, D_MODEL), lambda c: (c, 0))
    return pl.pallas_call(
        body, name="ssd_fwd", grid=(nc,),
        in_specs=[pl.BlockSpec((CHUNK, D_MODEL), lambda c: (c, AL_XS // D_MODEL)),
                  pl.BlockSpec((CHUNK, CONV_DIM - D_MODEL), lambda c: (c, AL_B // (CONV_DIM - D_MODEL))),
                  pl.BlockSpec((CONV_W, CONV_DIM), lambda c: (0, 0)), vec(CONV_DIM),
                  pl.BlockSpec((CHUNK, LANES), lambda c: (c, AL_DTF // LANES)),
                  vec(LANES), vec(LANES), vec(D_MODEL), pl.BlockSpec((CHUNK, D_MODEL), lambda c: (c, AL_Z // D_MODEL)), vec(D_MODEL)],
        out_specs=[pl.BlockSpec((CHUNK, CONV_DIM), lambda c: (c, 0)), rows,
                   pl.BlockSpec((1, N_PAIRS, SSM_STATE, LANES), lambda c: (c, 0, 0, 0)), rows],
        out_shape=[jax.ShapeDtypeStruct((s_len, CONV_DIM), F32), jax.ShapeDtypeStruct((s_len, D_MODEL), F32),
                   jax.ShapeDtypeStruct((nc, N_PAIRS, SSM_STATE, LANES), F32), jax.ShapeDtypeStruct((s_len, 2 * D_MODEL), BF16)],
        scratch_shapes=[pltpu.VMEM((N_PAIRS, SSM_STATE, LANES), F32), pltpu.VMEM((SUBLANES, CONV_DIM), F32)],
        compiler_params=_params(("arbitrary",)),
    )(proj, proj, conv_w, conv_b, proj, dt_bias_l, a_log_l, d_exp, proj, norm_w)


def _ssd_bwd(xc_all, proj, dt_bias_l, a_log_l, d_exp, prevs, y_ssd, dymix, norm_w):
    s_len = xc_all.shape[0]
    nc = s_len // CHUNK

    def body(x_ref, dtf_ref, bias_ref, alog_ref, dexp_ref, prevs_ref, y_ref, z_ref, dym_ref, w_ref,
             dx_ref, ddt_ref, da_ref, dd_ref, dbias_ref, dz_ref, dw_ref, dstate_ref):
        @pl.when(pl.program_id(0) == 0)
        def _():
            dstate_ref[...] = jnp.zeros_like(dstate_ref)
            da_ref[...] = jnp.zeros_like(da_ref)
            dd_ref[...] = jnp.zeros_like(dd_ref)
            dbias_ref[...] = jnp.zeros_like(dbias_ref)
            dw_ref[...] = jnp.zeros_like(dw_ref)

        y, z, dyo = y_ref[...], z_ref[...], dym_ref[...]
        sg = _sigmoid(z)
        sz = z * sg
        gated = y * sz
        rn = lax.rsqrt(_rowmean(gated * gated) + RMS_EPS)
        dg = _rms_bwd(dyo * w_ref[...], gated, rn)
        dy_full = dg * sz
        dz_ref[...] = (dg * y * (sg * (1.0 + z * (1.0 - sg)))).astype(dz_ref.dtype)
        dw_ref[...] += _colsum(dyo * gated * rn)

        row, lane = _tile_iotas()
        lo = lane < HEAD_DIM
        last = row == CHUNK - 1
        raw, dt, a_neg, s = _ssd_scalars(dtf_ref, bias_ref, alog_ref, row, lane)
        tot = s[CHUNK - 1:CHUNK, :]
        st = s.T
        ds_acc = jnp.zeros((CHUNK, LANES), F32)
        ddt_acc = jnp.zeros((CHUNK, LANES), F32)
        for g in range(SSM_GROUPS):
            bcol = slice(XC_B + g * SSM_STATE, XC_B + (g + 1) * SSM_STATE)
            ccol = slice(XC_C + g * SSM_STATE, XC_C + (g + 1) * SSM_STATE)
            bg = x_ref[:, bcol].astype(BF16)
            cg = x_ref[:, ccol].astype(BF16)
            cb = _dot(cg, bg, NT)
            cbt = _dot(bg, cg, NT)
            dcb = jnp.zeros((CHUNK, LANES), F32)
            dcbt = jnp.zeros((CHUNK, LANES), F32)
            db_acc = jnp.zeros((CHUNK, LANES), F32)
            dc_acc = jnp.zeros((CHUNK, LANES), F32)
            for j in range(g * 4, g * 4 + 4):
                cols = slice(j * LANES, (j + 1) * LANES)
                xs_p, dy_p = x_ref[:, cols], dy_full[:, cols]
                dt_p, s_p, tot_p = _pair(dt, j, lo), _pair(s, j, lo), _pair(tot, j, lo[:1])
                xc_p = xs_p * dt_p
                xc_b, dy_b = xc_p.astype(BF16), dy_p.astype(BF16)
                e_p, f_p, etot_p = jnp.exp(s_p), jnp.exp(tot_p - s_p), jnp.exp(tot_p)
                prev, dnext = prevs_ref[0, j], dstate_ref[j]
                prev_b, dnext_b = prev.astype(BF16), dnext.astype(BF16)
                dd_ref[:, cols] += _colsum(dy_p * xs_p)
                dxs_p = dexp_ref[:, cols] * dy_p
                cp = _dot(cg, prev_b)
                gy = (dy_p * e_p).astype(BF16)
                dc_acc += _dot(gy, prev_b, NT)
                dstate_ref[j] = etot_p * dnext + _dot(cg, gy, TN)
                de = dy_p * cp * e_p
                bds = _dot(bg, dnext_b)
                db_acc += _dot((xc_p * f_p).astype(BF16), dnext_b, NT)
                dxc_p = bds * f_p
                df = bds * xc_p * f_p
                dtot_p = _colsum(dnext * prev) * etot_p + _colsum(df)
                dsl = de - df + jnp.where(last, dtot_p, 0.0)
                for hh in range(2):
                    h = 2 * j + hh
                    mine = lo == (hh == 0)
                    lm, lmt = _decay_masks(s, st, h, row, lane)
                    dy_h = jnp.where(mine, dy_p, 0.0).astype(BF16)
                    xc_h = jnp.where(mine, xc_p, 0.0).astype(BF16)
                    dm = _dot(dy_h, xc_b, NT)
                    dmt = _dot(xc_h, dy_b, NT)
                    mt = cbt * lmt
                    dxc_p += _dot(mt.astype(BF16), dy_h)
                    dml, dmtl = dm * lm, dmt * lmt
                    ds_h = jnp.sum(dml * cb - dmtl * cbt + jnp.where(mine, dsl, 0.0), axis=1, keepdims=True)
                    ds_acc += jnp.where(lane == h, ds_h, 0.0)
                    dcb += dml
                    dcbt += dmtl
                    ddt_acc += jnp.where(lane == h, _head_sum(dxc_p * xs_p, lo, hh), 0.0)
                dx_ref[:, cols] = dxs_p + dxc_p * dt_p
            dx_ref[:, ccol] = dc_acc + _dot(dcb.astype(BF16), bg)
            dx_ref[:, bcol] = db_acc + _dot(dcbt.astype(BF16), cg)
        triu = (row <= lane).astype(BF16)
        da = _dot3(triu, ds_acc)
        ddt = ddt_acc + da * a_neg
        da_ref[...] += _colsum(da * dt) * a_neg[:1]
        ddt_raw = jnp.where(lane < N_HEADS, ddt * _sigmoid(raw), 0.0)
        dbias_ref[...] += _colsum(ddt_raw)
        ddt_ref[...] = ddt_raw

    rev = lambda c: nc - 1 - c
    vec = lambda w: pl.BlockSpec((1, w), lambda c: (0, 0))
    rows = lambda cb: pl.BlockSpec((CHUNK, D_MODEL), lambda c: (rev(c), cb))
    return pl.pallas_call(
        body, name="ssd_bwd", grid=(nc,),
        in_specs=[pl.BlockSpec((CHUNK, CONV_DIM), lambda c: (rev(c), 0)), pl.BlockSpec((CHUNK, LANES), lambda c: (rev(c), AL_DTF // LANES)),
                  vec(LANES), vec(LANES), vec(D_MODEL),
                  pl.BlockSpec((1, N_PAIRS, SSM_STATE, LANES), lambda c: (rev(c), 0, 0, 0)),
                  rows(0), rows(AL_Z // D_MODEL), rows(0), vec(D_MODEL)],
        out_specs=[pl.BlockSpec((CHUNK, CONV_DIM), lambda c: (rev(c), 0)), pl.BlockSpec((CHUNK, LANES), lambda c: (rev(c), 0)),
                   vec(LANES), vec(D_MODEL), vec(LANES), rows(AL_Z // D_MODEL), vec(D_MODEL)],
        out_shape=[jax.ShapeDtypeStruct((s_len, CONV_DIM), F32), jax.ShapeDtypeStruct((s_len, LANES), F32),
                   jax.ShapeDtypeStruct((1, LANES), F32), jax.ShapeDtypeStruct((1, D_MODEL), F32), jax.ShapeDtypeStruct((1, LANES), F32),
                   jax.ShapeDtypeStruct((s_len, AL_COLS), BF16), jax.ShapeDtypeStruct((1, D_MODEL), F32)],
        scratch_shapes=[pltpu.VMEM((N_PAIRS, SSM_STATE, LANES), F32)],
        compiler_params=_params(("arbitrary",)),
    )(xc_all, proj, dt_bias_l, a_log_l, d_exp, prevs, y_ssd, proj, dymix, norm_w)


AUG_LANES = 6


def _aug_base(hh):
    return HEAD_DIM if hh == 0 else 0


NEG = -1e30
ATT_T = 1024


def _fox_cum(proj, f_bias_l):
    s_len = proj.shape[0]
    nc = s_len // CHUNK

    def body(dtf_ref, fb_ref, cum_ref):
        row, lane = _tile_iotas()
        tril = (row >= lane).astype(BF16)
        spread = [(jnp.where(lane == AUG_LANES * row + i, 1.0, 0.0) - jnp.where(lane == AUG_LANES * row + 3 + i, 1.0, 0.0)).astype(BF16)
                  for i in range(3)]

        def step(c, carry):
            rows = pl.ds(pl.multiple_of(c * CHUNK, CHUNK), CHUNK)
            lf = -_softplus(-(dtf_ref[rows, :] + fb_ref[...]))
            lf = jnp.where(lane < N_HEADS, pltpu.roll(lf, LANES - F_LANE, 1), 0.0)
            cs = _dot3(tril, lf) + carry
            parts = _split3(cs)
            cum_ref[rows, :] = _dot(parts[0], spread[0]) + _dot(parts[1], spread[1]) + _dot(parts[2], spread[2])
            return cs[CHUNK - 1:CHUNK, :]

        lax.fori_loop(0, nc, step, jnp.zeros((1, LANES), F32))

    return pl.pallas_call(
        body, name="fox_cum", grid=(1,),
        in_specs=[pl.BlockSpec((s_len, LANES), lambda i: (0, AL_DTF // LANES)), pl.BlockSpec((1, LANES), lambda i: (0, 0))],
        out_specs=pl.BlockSpec((s_len, LANES), lambda i: (0, 0)),
        out_shape=jax.ShapeDtypeStruct((s_len, LANES), F32),
        compiler_params=_params(("arbitrary",)),
    )(proj, f_bias_l)


def _fox_cum_bwd(dcum, proj, f_bias_l, ddt_tile, dproj):
    s_len = proj.shape[0]
    nc = s_len // CHUNK

    def body(dcum_ref, dtf_ref, fb_ref, ddt_ref, _, out_ref, dfb_ref):
        row, lane = _tile_iotas()
        triu = (row <= lane).astype(BF16)
        is_f = (lane >= F_LANE) & (lane < F_LANE + N_HEADS)

        def step(t, carry):
            run, dfb = carry
            rows = pl.ds(pl.multiple_of((nc - 1 - t) * CHUNK, CHUNK), CHUNK)
            rc = _dot3(triu, dcum_ref[rows, :]) + run
            sg = _sigmoid(-(dtf_ref[rows, :] + fb_ref[...]))
            df = jnp.where(is_f, pltpu.roll(rc, F_LANE, 1) * sg, 0.0)
            out_ref[rows, :] = (df + ddt_ref[rows, :]).astype(out_ref.dtype)
            return rc[0:1, :], dfb + _colsum(df)

        _, dfb = lax.fori_loop(0, nc, step, (jnp.zeros((1, LANES), F32), jnp.zeros((1, LANES), F32)))
        dfb_ref[...] = dfb

    whole = pl.BlockSpec((s_len, LANES), lambda i: (0, 0))
    dtf_cols = pl.BlockSpec((s_len, LANES), lambda i: (0, AL_DTF // LANES))
    vec = pl.BlockSpec((1, LANES), lambda i: (0, 0))
    return pl.pallas_call(
        body, name="fox_cum_bwd", grid=(1,),
        in_specs=[whole, dtf_cols, vec, whole, ANY], out_specs=[dtf_cols, vec],
        out_shape=[jax.ShapeDtypeStruct(dproj.shape, dproj.dtype), jax.ShapeDtypeStruct((1, LANES), F32)],
        input_output_aliases={4: 0}, compiler_params=_params(("arbitrary",)),
    )(dcum, proj, f_bias_l, ddt_tile, dproj)


def _attn_prep(proj, cum):
    s_len = proj.shape[0]
    tr = min(256, s_len)

    def body(q_ref, k_ref, v_ref, cum_ref, qa_ref, ka_ref, vb_ref):
        lane = lax.broadcasted_iota(jnp.int32, (tr, LANES), 1)
        lo = lane < HEAD_DIM
        c = cum_ref[...]
        for p in range(N_PAIRS):
            cols = slice(p * LANES, (p + 1) * LANES)
            q, k = q_ref[:, cols] * (HEAD_DIM ** -0.5), k_ref[:, cols]
            for hh in range(2):
                base = _aug_base(hh)
                r = pltpu.roll(c, (base - AUG_LANES * (2 * p + hh)) % LANES, 1)
                first = (lane >= base) & (lane < base + 3)
                second = (lane >= base + 3) & (lane < base + AUG_LANES)
                mine = lo == (hh == 0)
                qa_ref[2 * p + hh] = jnp.where(mine, q, jnp.where(first, r, jnp.where(second, 1.0, 0.0))).astype(BF16)
                ka_ref[2 * p + hh] = jnp.where(mine, k, jnp.where(first, 1.0, jnp.where(second, r, 0.0))).astype(BF16)
        vb_ref[...] = v_ref[...].astype(BF16)

    assert AL_Q % D_MODEL == 0 and AL_K % D_MODEL == 0 and AL_V % D_MODEL == 0
    slab = lambda col0: pl.BlockSpec((tr, D_MODEL), lambda i: (i, col0 // D_MODEL))
    heads = pl.BlockSpec((N_HEADS, tr, LANES), lambda i: (0, i, 0))
    return pl.pallas_call(
        body, name="attn_prep", grid=(s_len // tr,),
        in_specs=[slab(AL_Q), slab(AL_K), slab(AL_V), pl.BlockSpec((tr, LANES), lambda i: (i, 0))],
        out_specs=[heads, heads, pl.BlockSpec((tr, D_MODEL), lambda i: (i, 0))],
        out_shape=[jax.ShapeDtypeStruct((N_HEADS, s_len, LANES), BF16), jax.ShapeDtypeStruct((N_HEADS, s_len, LANES), BF16),
                   jax.ShapeDtypeStruct((s_len, D_MODEL), BF16)],
        compiler_params=_params(("parallel",)),
    )(proj, proj, proj, cum)


def _attn_fwd(qa, ka, vb, shards):
    s_len = vb.shape[0]
    t = min(ATT_T, s_len)
    nq = s_len // t
    n = len(shards)

    def body(qa_ref, ka_ref, vb_ref, *rest):
        o_ref, lse_ref = rest[n:n + 2]
        start, finish = _gather_plan(rest[:n], rest[n + 2:2 * n + 2], *rest[2 * n + 2:])
        i = pl.program_id(1)
        pl.when((pl.program_id(0) == 0) & (i == 0))(start)
        row = lax.broadcasted_iota(jnp.int32, (t, t), 0)
        col = lax.broadcasted_iota(jnp.int32, (t, t), 1)
        lo = lax.broadcasted_iota(jnp.int32, (t, LANES), 1) < HEAD_DIM
        qs = (qa_ref[0], qa_ref[1])

        def block(j, carry, masked):
            rows = pl.ds(pl.multiple_of(j * t, t), t)
            v = vb_ref[rows, :]
            new = []
            for hh in range(2):
                m, l, acc = carry[hh]
                s = _dot(qs[hh], ka_ref[hh, rows, :], NT)
                if masked:
                    s = jnp.where(row >= col, s, NEG)
                m_new = jnp.maximum(m, jnp.max(s, axis=1, keepdims=True))
                alpha = jnp.exp(m - m_new)
                p = jnp.exp(s - m_new)
                new.append((m_new, alpha * l + jnp.sum(p, axis=1, keepdims=True), alpha * acc + _dot(p.astype(BF16), v)))
            return tuple(new)

        init = (jnp.full((t, 1), NEG, F32), jnp.zeros((t, 1), F32), jnp.zeros((t, LANES), F32))
        carry = lax.fori_loop(0, i, functools.partial(block, masked=False), (init, init))
        (m0, l0, acc0), (m1, l1, acc1) = block(i, carry, True)
        o_ref[...] = jnp.where(lo, acc0 / l0, acc1 / l1)
        lse_ref[...] = jnp.where(lo, m0 + jnp.log(l0), m1 + jnp.log(l1))
        pl.when((pl.program_id(0) == N_PAIRS - 1) & (i == nq - 1))(finish)

    out = pl.BlockSpec((t, LANES), lambda p, i: (i, p))
    res = pl.pallas_call(
        body, name="attn_fwd", grid=(N_PAIRS, nq),
        in_specs=[pl.BlockSpec((2, t, LANES), lambda p, i: (p, i, 0)), pl.BlockSpec((2, s_len, LANES), lambda p, i: (p, 0, 0)),
                  pl.BlockSpec((s_len, LANES), lambda p, i: (0, p))] + [ANY] * n,
        out_specs=[out, out] + [ANY] * n,
        out_shape=[jax.ShapeDtypeStruct((s_len, D_MODEL), F32), jax.ShapeDtypeStruct((s_len, D_MODEL), F32)]
        + [jax.ShapeDtypeStruct((N_CHIPS, *h.shape), h.dtype) for h in shards],
        scratch_shapes=_exchange_sems(n),
        compiler_params=_params(("arbitrary", "arbitrary")),
    )(qa, ka, vb, *shards)
    return res[0], res[1], res[2:]


def _attn_bwd(qa, ka, vb, o, lse, do, parts, dproj):
    s_len = vb.shape[0]
    t = min(ATT_T, s_len)
    nq = s_len // t
    n = len(parts)

    def body(qa_ref, ka_ref, vb_ref, o_ref, lse_ref, do_ref, *rest):
        dqa_ref, dka_ref, dv_ref = rest[n + 1:n + 4]
        start, finish = _reduce_plan(rest[:n], rest[n + 4:2 * n + 4], *rest[2 * n + 4:])
        j = pl.program_id(1)
        pl.when((pl.program_id(0) == 0) & (j == 0))(start)

        @pl.when(j == 0)
        def _():
            dqa_ref[...] = jnp.zeros_like(dqa_ref)

        row = lax.broadcasted_iota(jnp.int32, (t, t), 0)
        col = lax.broadcasted_iota(jnp.int32, (t, t), 1)
        lo = lax.broadcasted_iota(jnp.int32, (t, LANES), 1) < HEAD_DIM
        v = vb_ref[...]
        ks = (ka_ref[0], ka_ref[1])

        def block(i, carry, masked):
            dk, dv = list(carry[:2]), carry[2]
            rows = pl.ds(pl.multiple_of(i * t, t), t)
            do_p, o_p, lse_p = do_ref[rows, :], o_ref[rows, :], lse_ref[rows, :]
            for hh in range(2):
                q = qa_ref[hh, rows, :]
                do_h = jnp.where(lo == (hh == 0), do_p, 0.0)
                delta = jnp.sum(do_h * o_p, axis=1, keepdims=True)
                s = _dot(q, ks[hh], NT)
                if masked:
                    s = jnp.where(row >= col, s, NEG)
                p = jnp.exp(s - lse_p[:, hh * HEAD_DIM:hh * HEAD_DIM + 1])
                do_b = do_h.astype(BF16)
                ds = (p * (_dot(do_b, v, NT) - delta)).astype(BF16)
                dv = dv + _dot(p.astype(BF16), do_b, TN)
                dk[hh] = dk[hh] + _dot(ds, q, TN)
                dqa_ref[hh, rows, :] += _dot(ds, ks[hh])
            return dk[0], dk[1], dv

        zero = jnp.zeros((t, LANES), F32)
        carry = block(j, (zero, zero, zero), True)
        dk0, dk1, dv = lax.fori_loop(j + 1, nq, functools.partial(block, masked=False), carry)
        dka_ref[0] = dk0
        dka_ref[1] = dk1
        dv_ref[...] = dv.astype(dv_ref.dtype)
        pl.when((pl.program_id(0) == N_PAIRS - 1) & (j == nq - 1))(finish)

    whole_pair = pl.BlockSpec((2, s_len, LANES), lambda p, j: (p, 0, 0))
    blk_pair = pl.BlockSpec((2, t, LANES), lambda p, j: (p, j, 0))
    whole_cols = pl.BlockSpec((s_len, LANES), lambda p, j: (0, p))
    blk_cols = pl.BlockSpec((t, LANES), lambda p, j: (j, p))
    res = pl.pallas_call(
        body, name="attn_bwd", grid=(N_PAIRS, nq),
        in_specs=[whole_pair, blk_pair, blk_cols, whole_cols, whole_cols, whole_cols] + [ANY] * (n + 1),
        out_specs=[whole_pair, blk_pair, pl.BlockSpec((t, LANES), lambda p, j: (j, AL_V // LANES + p))] + [ANY] * n,
        out_shape=[jax.ShapeDtypeStruct((N_HEADS, s_len, LANES), F32), jax.ShapeDtypeStruct((N_HEADS, s_len, LANES), F32),
                   jax.ShapeDtypeStruct(dproj.shape, dproj.dtype)]
        + [jax.ShapeDtypeStruct((N_DEV, g.shape[1] // 2, g.shape[2]), g.dtype) for g in parts],
        scratch_shapes=_exchange_sems(n), input_output_aliases={6 + n: 2},
        compiler_params=_params(("arbitrary", "arbitrary")),
    )(qa, ka, vb, o, lse, do, *parts, dproj)
    return res[0], res[1], res[2], res[3:]


def _attn_post(dqa, dka, dproj):
    s_len = dqa.shape[1]
    tr = min(256, s_len)
    assert AL_K == AL_Q + D_MODEL and AL_Q % (2 * D_MODEL) == 0

    def body(dqa_ref, dka_ref, _, dqk_ref, dcum_ref):
        lane = lax.broadcasted_iota(jnp.int32, (tr, LANES), 1)
        lo = lane < HEAD_DIM
        dcum = jnp.zeros((tr, LANES), F32)
        for p in range(N_PAIRS):
            a0, a1, b0, b1 = dqa_ref[2 * p], dqa_ref[2 * p + 1], dka_ref[2 * p], dka_ref[2 * p + 1]
            dq = jnp.where(lo, a0, a1) * (HEAD_DIM ** -0.5)
            dqk_ref[:, p * LANES:(p + 1) * LANES] = dq.astype(dqk_ref.dtype)
            dqk_ref[:, D_MODEL + p * LANES:D_MODEL + (p + 1) * LANES] = jnp.where(lo, b0, b1).astype(dqk_ref.dtype)
            for hh, (a, b) in enumerate(((a0, b0), (a1, b1))):
                base = _aug_base(hh)
                dcum = dcum + jnp.where(lane == 2 * p + hh, a[:, base:base + 1] - b[:, base + 3:base + 4], 0.0)
        dcum_ref[...] = dcum

    heads = pl.BlockSpec((N_HEADS, tr, LANES), lambda i: (0, i, 0))
    return pl.pallas_call(
        body, name="attn_post", grid=(s_len // tr,),
        in_specs=[heads, heads, ANY],
        out_specs=[pl.BlockSpec((tr, 2 * D_MODEL), lambda i: (i, AL_Q // (2 * D_MODEL))), pl.BlockSpec((tr, LANES), lambda i: (i, 0))],
        out_shape=[jax.ShapeDtypeStruct(dproj.shape, dproj.dtype), jax.ShapeDtypeStruct((s_len, LANES), F32)],
        input_output_aliases={2: 0}, compiler_params=_params(("parallel",)),
    )(dqa, dka, dproj)


def _ln_stats(r):
    mu = _rowmean(r)
    xc = r - mu
    rstd = lax.rsqrt(_rowmean(xc * xc) + LN_EPS)
    return xc * rstd, rstd


def _ln_bwd(dxh, xh, rstd):
    return rstd * (dxh - _rowmean(dxh) - xh * _rowmean(dxh * xh))


def _rms_bwd(dgn, g, r):
    return r * dgn - (r * r * r) * g * _rowmean(dgn * g)


def _to_aligned(wt):
    out = jnp.zeros((AL_COLS, wt.shape[1]), wt.dtype)
    for dst, (lo, hi) in ((0, (0, 2048)), (AL_Q, (2576, 5648)), (AL_B, (2048, 2560)), (AL_DTF, (2560, 2576)), (AL_DTF + 16, (5648, 5664))):
        out = lax.dynamic_update_slice_in_dim(out, wt[lo:hi], dst, axis=0)
    return out


def _from_aligned(gt):
    out = jnp.zeros((IN_COLS, gt.shape[1]), gt.dtype)
    for dst, (lo, hi) in ((0, (0, AL_Q)), (2048, (AL_B, AL_DTF)), (2560, (AL_DTF, AL_DTF + 16)), (2576, (AL_Q, AL_B)),
                          (5648, (AL_DTF + 16, AL_DTF + 32))):
        out = lax.dynamic_update_slice_in_dim(out, gt[lo:hi], dst, axis=0)
    return out


def _lanes(v, at=0):
    return jnp.pad(v, ((0, 0), (at, LANES - at - v.shape[1])))


def _local_step(x, tgt, mod, w_alt, shards, sp):
    d = D_MODEL
    sh1, sc1, g1, sh2, sc2, g2 = [mod[:, i * d:(i + 1) * d] for i in range(6)]
    dt_bias_l, a_log_l, f_bias_l = _lanes(sp["dt_bias"]), _lanes(sp["a_log"]), _lanes(sp["f_bias"], F_LANE)
    d_exp = jnp.repeat(sp["d_skip"], HEAD_DIM, axis=1)

    (h1,), _ = _rowwise("mod1", lambda x, sc, sh: ([x * (1.0 + sc) + sh], []), [x], [sc1, sh1], [(d, BF16)], [])
    proj = _matmul("proj", h1, w_alt, dims=NT, tn=1152)
    xc_all, y_ssd, prevs, y_mix = _ssd_fwd(proj, sp["conv_w"], sp["conv_b"], dt_bias_l, a_log_l, d_exp, sp["ssm_norm_w"])
    cum = _fox_cum(proj, f_bias_l)
    qa, ka, vb = _attn_prep(proj, cum)
    o, lse, (g_out, g_fi, g_fo) = _attn_fwd(qa, ka, vb, shards)
    w_out = g_out.reshape(2 * d, d)
    w_fi = g_fi.transpose(1, 0, 2).reshape(d, D_FF)
    w_fo = g_fo.reshape(D_FF, d)
    (y_mix,), _ = _rowwise("attn_norm", lambda o, w: ([o * lax.rsqrt(_rowmean(o * o) + RMS_EPS) * w], []),
                           [o], [sp["attn_norm_w"]], [(d, BF16, (2 * d, 1, y_mix))], [])
    def ln1_fwd(y, x, g1, sc2, sh2, lg, lb):
        r1 = ALPHA * x + (1.0 + g1) * y
        xh, _ = _ln_stats(r1)
        x1 = xh * lg + lb
        h2 = x1 * (1.0 + sc2) + sh2
        return [y, r1, h2, h2.T], []

    def relu2(u):
        a = jnp.square(jnp.maximum(u, 0.0))
        return [a, a.T], []

    y, r1, h2, h2_t = _matmul("out_proj", y_mix, w_out, tm=512, tk=2048,
                              epi=(ln1_fwd, [x], [g1, sc2, sh2, sp["ln1_g"], sp["ln1_b"]], [F32, F32, BF16, ("T", BF16)], []))
    act, act_t = _matmul("ff_in", h2, w_fi, epi=(relu2, [], [], [BF16, ("T", BF16)], []))

    def head(ff, r1, tgt, g2, l1g, l1b, l2g, l2b):
        xh1, _ = _ln_stats(r1)
        x1 = xh1 * l1g + l1b
        xh2, rstd2 = _ln_stats(ALPHA * x1 + (1.0 + g2) * ff)
        err = xh2 * l2g + l2b - tgt
        loss = 0.5 * jnp.sum(_rowmean(err * err))
        dx2 = err * (1.0 / d)
        dr2 = _ln_bwd(dx2 * l2g, xh2, rstd2)
        return ([dr2, (1.0 + g2) * dr2],
                [_colsum(dx2 * xh2), _colsum(dx2), _colsum(dr2 * ff), jnp.full((1, LANES), loss, F32)])

    dr2, dff, d_ln2_g, d_ln2_b, d_g2, loss = _matmul(
        "ff_out", act, w_fo, tm=512, tk=D_FF,
        epi=(head, [r1, tgt], [g2, sp["ln1_g"], sp["ln1_b"], sp["ln2_g"], sp["ln2_b"]], [F32, BF16], [d, d, d, LANES]))
    du = _matmul("d_act", dff, w_fo, dims=NT, epi=(lambda da, act: ([da * (2.0 * jnp.sqrt(act.astype(F32)))], []), [act], [], [BF16], []))
    dw_fo = _matmul("dw_ff_out", act_t, dff, tk=SEQ_TK, out_dtype=BF16, by_chip="rows")
    dw_fi = _matmul("dw_ff_in", h2_t, du, tk=SEQ_TK, out_dtype=BF16, by_chip="cols")

    def ln1_bwd(dh2, r1, dr2, y, sc2, g1, lg, lb):
        xh, rstd = _ln_stats(r1)
        x1 = xh * lg + lb
        dx1 = ALPHA * dr2 + dh2 * (1.0 + sc2)
        dr1 = _ln_bwd(dx1 * lg, xh, rstd)
        return ([dr1, (1.0 + g1) * dr1],
                [_colsum(dh2 * x1), _colsum(dh2), _colsum(dx1 * xh), _colsum(dx1), _colsum(dr1 * y)])

    dr1, dy, d_sc2, d_sh2, d_ln1_g, d_ln1_b, d_g1 = _matmul(
        "dh2", du, w_fi, dims=NT, tm=512, tk=D_FF,
        epi=(ln1_bwd, [r1, dr2, y], [sc2, g1, sp["ln1_g"], sp["ln1_b"]], [F32, BF16], [d] * 5))
    dw_out = _matmul("dw_out", y_mix, dy, dims=TN, tk=SEQ_TK, out_dtype=BF16, by_chip="rows")

    def attn_norm_bwd(dyo, o, w):
        r = lax.rsqrt(_rowmean(o * o) + RMS_EPS)
        return [_rms_bwd(dyo * w, o, r)], [_colsum(dyo * o * r)]

    dymix = _matmul("dy_mix_ssm", dy, w_out[:d], dims=NT)
    do, d_attn_w = _matmul("dy_mix_att", dy, w_out[d:], dims=NT, tm=512, epi=(attn_norm_bwd, [o], [sp["attn_norm_w"]], [F32], [d]))

    dxc, ddt_tile, d_alog_l, d_dexp, d_dtb_l, dproj, d_ssm_w = _ssd_bwd(
        xc_all, proj, dt_bias_l, a_log_l, d_exp, prevs, y_ssd, dymix, sp["ssm_norm_w"])
    dqa, dka, dproj, landed = _attn_bwd(qa, ka, vb, o, lse, do, [dw_out, dw_fi, dw_fo], dproj)
    dproj, dcum = _attn_post(dqa, dka, dproj)
    dproj, d_fb_l = _fox_cum_bwd(dcum, proj, f_bias_l, ddt_tile, dproj)
    dproj, d_conv_w, d_conv_b = _conv_bwd(proj, sp["conv_w"], sp["conv_b"], dxc, dproj)
    dw_alt = _matmul("dw_in", dproj, h1, dims=TN, tm=1152, tk=SEQ_TK, out_dtype=BF16)
    part_in = _from_aligned(dw_alt).reshape(N_CHIPS, IN_COLS // N_CHIPS, d)

    def last(dh1, x, dr1, sc1):
        return [ALPHA * dr1 + dh1 * (1.0 + sc1)], [_colsum(dh1 * x), _colsum(dh1)]

    chip_in = _pair_sum(part_in, _pair_exchange(part_in), lax.axis_index("c"))
    dx, d_sc1, d_sh1, landed_in = _matmul("dh1", dproj, w_alt, tm=512, tk=AL_COLS, carry=[chip_in],
                                          epi=(last, [x, dr1], [sc1], [F32], [d, d]))

    small = {
        "mod": jnp.concatenate([d_sh1, d_sc1, d_g1, d_sh2, d_sc2, d_g2], axis=1),
        "conv_w": d_conv_w, "conv_b": d_conv_b,
        "dt_bias": d_dtb_l[:, :N_HEADS], "a_log": d_alog_l[:, :N_HEADS],
        "d_skip": jnp.sum(d_dexp.reshape(N_HEADS, HEAD_DIM), axis=1)[None, :],
        "ssm_norm_w": d_ssm_w, "f_bias": d_fb_l[:, F_LANE:F_LANE + N_HEADS], "attn_norm_w": d_attn_w,
        "ln1_g": d_ln1_g, "ln1_b": d_ln1_b, "ln2_g": d_ln2_g, "ln2_b": d_ln2_b, "loss": loss,
    }
    return dx, [landed_in, *landed], small


N_DEV = 8
N_CHIPS = 4
ANY = pl.BlockSpec(memory_space=pl.ANY)
VMEM_SPEC = pl.BlockSpec(memory_space=pltpu.VMEM)


def _place():
    x, y, c = lax.axis_index("x"), lax.axis_index("y"), lax.axis_index("c")
    return x, y, c


def _other_chips(x, y):
    return [(1 - x, y, 2 * (1 - x) + y), (x, 1 - y, 2 * x + 1 - y), (1 - x, 1 - y, 2 * (1 - x) + 1 - y)]


def _small_gather(v_ref, out_ref, send_sems, recv_sems, local_sem, after_start=None):
    x, y, c = _place()
    me = 4 * x + 2 * y + c
    mine = pltpu.make_async_copy(v_ref, out_ref.at[me], local_sem)
    mine.start()
    peers = _peers(x, y, c)

    def copy(rel, slot, to):
        return pltpu.make_async_remote_copy(src_ref=v_ref, dst_ref=out_ref.at[slot], send_sem=send_sems.at[rel],
                                            recv_sem=recv_sems.at[rel], device_id=to, device_id_type=MESH)

    sends = [copy(rel, me, peer) for rel, peer in enumerate(peers)]
    for cp in sends:
        cp.start()
    if after_start is not None:
        after_start()
    for rel, (px, py, pc) in enumerate(peers):
        copy(rel, 4 * px + 2 * py + pc, (x, y, c)).wait_recv()
    for cp in sends:
        cp.wait_send()
    mine.wait()


SMALL_GATHER_SEMS = [pltpu.SemaphoreType.DMA((N_DEV - 1,)), pltpu.SemaphoreType.DMA((N_DEV - 1,)), pltpu.SemaphoreType.DMA]


def _allgather_small(name, v):
    def body(v_ref, out_ref, *sems):
        _small_gather(v_ref, out_ref, *sems)

    return pl.pallas_call(
        body, name=name, out_shape=jax.ShapeDtypeStruct((N_DEV, *v.shape), v.dtype),
        in_specs=[VMEM_SPEC], out_specs=VMEM_SPEC, scratch_shapes=SMALL_GATHER_SEMS,
    )(v)


def _tail_exchange(vec, halves):
    n = len(halves)

    def body(v_ref, *rest):
        ins, (every_ref, total_ref), outs = rest[:n], rest[n:n + 2], rest[n + 2:2 * n + 2]
        gather_sems, (swap_send, swap_recv) = rest[2 * n + 2:2 * n + 5], rest[2 * n + 5:]
        x, y, c = _place()
        swaps = [pltpu.make_async_remote_copy(src_ref=ins[w], dst_ref=outs[w], send_sem=swap_send.at[w], recv_sem=swap_recv.at[w],
                                              device_id=(x, y, 1 - c), device_id_type=MESH) for w in range(n)]

        def start_swaps():
            for cp in swaps:
                cp.start()

        _small_gather(v_ref, every_ref, *gather_sems, after_start=start_swaps)
        acc = every_ref[0]
        for dev in range(1, N_DEV):
            acc = acc + every_ref[dev]
        total_ref[...] = acc
        for cp in swaps:
            cp.wait_recv()
        for cp in swaps:
            cp.wait_send()

    res = pl.pallas_call(
        body, name="tail_exchange",
        out_shape=[jax.ShapeDtypeStruct((N_DEV, *vec.shape), vec.dtype), jax.ShapeDtypeStruct(vec.shape, vec.dtype)]
        + [jax.ShapeDtypeStruct(h.shape, h.dtype) for h in halves],
        in_specs=[VMEM_SPEC] + [ANY] * n, out_specs=[VMEM_SPEC, VMEM_SPEC] + [ANY] * n,
        scratch_shapes=SMALL_GATHER_SEMS + [pltpu.SemaphoreType.DMA((n,)), pltpu.SemaphoreType.DMA((n,))],
    )(vec, *halves)
    return res[0], res[1], res[2:]


def _gather_shards(shard):
    def body(in_ref, out_ref, stage, send_sems, recv_sems, local_sems):
        start, finish = _shard_gather_plan(in_ref, out_ref, stage, send_sems, recv_sems, local_sems)
        start()
        finish()

    return pl.pallas_call(
        body, name="gather_w_in", out_shape=jax.ShapeDtypeStruct((N_CHIPS, *shard.shape), shard.dtype),
        in_specs=[ANY], out_specs=ANY,
        scratch_shapes=[pltpu.VMEM(shard.shape, shard.dtype), pltpu.SemaphoreType.DMA((6,)), pltpu.SemaphoreType.DMA((6,)),
                        pltpu.SemaphoreType.DMA((2,))],
        compiler_params=_params(),
    )(shard)


def _shard_gather_plan(in_ref, out_ref, stage, send_sems, recv_sems, local_sems):
    ch = in_ref.shape[1] // 2
    x, y, c = _place()
    k_me = 2 * x + y
    me, sibling = (x, y, c), (x, y, 1 - c)
    chips = _other_chips(x, y)

    def copy(idx, k, half, to, src=None):
        cols = out_ref.at[k, :, pl.ds(pl.multiple_of(half * ch, ch), ch)]
        return pltpu.make_async_remote_copy(src_ref=cols if src is None else src, dst_ref=cols, send_sem=send_sems.at[idx],
                                            recv_sem=recv_sems.at[idx], device_id=to, device_id_type=MESH)

    mine = in_ref.at[:, pl.ds(pl.multiple_of(c * ch, ch), ch)]
    sends = [copy(j, k_me, c, (cx, cy, c), src=mine) for j, (cx, cy, _) in enumerate(chips)]
    load = pltpu.make_async_copy(in_ref, stage, local_sems.at[0])
    store = pltpu.make_async_copy(stage, out_ref.at[k_me], local_sems.at[1])

    def start():
        for cp in sends:
            cp.start()
        load.start()

    def finish():
        load.wait()
        store.start()
        forwards = []
        for j, (_, _, kj) in enumerate(chips):
            copy(j, kj, c, me).wait_recv()
            forwards.append(copy(3 + j, kj, c, sibling))
            forwards[-1].start()
        for j, (_, _, kj) in enumerate(chips):
            copy(3 + j, kj, 1 - c, me).wait_recv()
        for cp in sends + forwards:
            cp.wait_send()
        store.wait()

    return start, finish


def _peers(x, y, c):
    return [((1 - x) if rel & 4 else x, (1 - y) if rel & 2 else y, (1 - c) if rel & 1 else c) for rel in range(1, N_DEV)]


def _exchange_sems(n):
    return [pltpu.SemaphoreType.DMA((n, N_DEV - 1)), pltpu.SemaphoreType.DMA((n, N_DEV - 1)), pltpu.SemaphoreType.DMA((n,))]


def _gather_plan(ins, outs, send_sems, recv_sems, local_sems):
    x, y, c = _place()
    k_me = 2 * x + y
    peers = [(rel, p) for rel, p in enumerate(_peers(x, y, c)) if (rel + 1) & 6]

    def copy(w, rel, k, half, to, src=None):
        rh = ins[w].shape[0] // 2
        rows = outs[w].at[k, pl.ds(pl.multiple_of(half * rh, rh), rh), :]
        return pltpu.make_async_remote_copy(src_ref=rows if src is None else src, dst_ref=rows, send_sem=send_sems.at[w, rel],
                                            recv_sem=recv_sems.at[w, rel], device_id=to, device_id_type=MESH)

    def mine(w):
        rh = ins[w].shape[0] // 2
        return ins[w].at[pl.ds(pl.multiple_of(c * rh, rh), rh), :]

    n = len(ins)
    local = [pltpu.make_async_copy(ins[w], outs[w].at[k_me], local_sems.at[w]) for w in range(n)]
    sends = [copy(w, rel, k_me, c, peer, src=mine(w)) for w in range(n) for rel, peer in peers]

    def start():
        for cp in local + sends:
            cp.start()

    def finish():
        for w in range(n):
            for rel, (px, py, pc) in peers:
                copy(w, rel, 2 * px + py, pc, (x, y, c)).wait_recv()
        for cp in sends:
            cp.wait_send()
        for cp in local:
            cp.wait()

    return start, finish


def _reduce_plan(ins, outs, send_sems, recv_sems, local_sems):
    x, y, c = _place()
    me = 4 * x + 2 * y + c
    peers = _peers(x, y, c)

    def block(w, k, half):
        rh = ins[w].shape[1] // 2
        return ins[w].at[k, pl.ds(pl.multiple_of(half * rh, rh), rh), :]

    def copy(w, rel, src, slot, to):
        return pltpu.make_async_remote_copy(src_ref=src, dst_ref=outs[w].at[slot], send_sem=send_sems.at[w, rel],
                                            recv_sem=recv_sems.at[w, rel], device_id=to, device_id_type=MESH)

    n = len(ins)
    local = [pltpu.make_async_copy(block(w, 2 * x + y, c), outs[w].at[me], local_sems.at[w]) for w in range(n)]
    sends = [copy(w, rel, block(w, 2 * px + py, pc), me, (px, py, pc)) for w in range(n) for rel, (px, py, pc) in enumerate(peers)]

    def start():
        for cp in local + sends:
            cp.start()

    def finish():
        for w in range(n):
            for rel, (px, py, pc) in enumerate(peers):
                copy(w, rel, block(w, 2 * x + y, c), 4 * px + 2 * py + pc, (x, y, c)).wait_recv()
        for cp in sends:
            cp.wait_send()
        for cp in local:
            cp.wait()

    return start, finish


def _scatter_plan(ins, outs, send_sems, recv_sems, local_sems):
    x, y, c = _place()
    k_me = 2 * x + y
    chips = _other_chips(x, y)

    def copy(w, j, src_k, dst_k, to):
        return pltpu.make_async_remote_copy(src_ref=ins[w].at[src_k], dst_ref=outs[w].at[dst_k], send_sem=send_sems.at[w, j],
                                            recv_sem=recv_sems.at[w, j], device_id=to, device_id_type=MESH)

    n = len(ins)
    local = [pltpu.make_async_copy(ins[w].at[k_me], outs[w].at[k_me], local_sems.at[w]) for w in range(n)]
    sends = [copy(w, j, kj, k_me, (cx, cy, c)) for w in range(n) for j, (cx, cy, kj) in enumerate(chips)]

    def start():
        for cp in local + sends:
            cp.start()

    def finish():
        for w in range(n):
            for j, (_, _, kj) in enumerate(chips):
                copy(w, j, k_me, kj, (x, y, c)).wait_recv()
        for cp in sends:
            cp.wait_send()
        for cp in local:
            cp.wait()

    return start, finish


def _row_tile(r, mult=2 * SUBLANES):
    if r % 256 == 0:
        return 256
    return max([t for t in range(mult, 513, mult) if r % t == 0], default=r)


def _pair_exchange(g):
    _, r, cdim = g.shape
    ch = cdim // 2

    def body(g_ref, got_ref, send_sem, recv_sem):
        x, y, c = _place()
        cp = pltpu.make_async_remote_copy(src_ref=g_ref.at[:, :, pl.ds(pl.multiple_of((1 - c) * ch, ch), ch)], dst_ref=got_ref,
                                          send_sem=send_sem, recv_sem=recv_sem, device_id=(x, y, 1 - c), device_id_type=MESH)
        cp.start()
        cp.wait_recv()
        cp.wait_send()

    return pl.pallas_call(
        body, name="pair_exchange", out_shape=jax.ShapeDtypeStruct((N_CHIPS, r, ch), g.dtype),
        in_specs=[ANY], out_specs=ANY, scratch_shapes=[pltpu.SemaphoreType.DMA, pltpu.SemaphoreType.DMA],
    )(g)


def _pair_sum(g, got, c):
    _, r, cdim = g.shape
    ch = cdim // 2
    tr = _row_tile(r)

    def body(c_ref, g_ref, got_ref, o_ref):
        o_ref[...] = (g_ref[...].astype(F32) + got_ref[...].astype(F32)).astype(o_ref.dtype)

    blk = pl.BlockSpec((1, tr, ch), lambda k, i, c_ref: (k, i, 0))
    return pl.pallas_call(
        body, name="pair_sum",
        grid_spec=pltpu.PrefetchScalarGridSpec(
            num_scalar_prefetch=1, grid=(N_CHIPS, r // tr),
            in_specs=[pl.BlockSpec((1, tr, ch), lambda k, i, c_ref: (k, i, c_ref[0])), blk], out_specs=blk),
        out_shape=jax.ShapeDtypeStruct((N_CHIPS, r, ch), BF16),
        compiler_params=_params(("parallel", "parallel")),
    )(jnp.reshape(c, (1,)).astype(jnp.int32), g, got)


def _sum_blocks(name, parts):
    k, r, cdim = parts.shape
    tr = _row_tile(r)

    def body(p_ref, o_ref):
        acc = p_ref[0].astype(F32)
        for i in range(1, k):
            acc = acc + p_ref[i].astype(F32)
        o_ref[...] = acc

    return pl.pallas_call(
        body, name=name, grid=(r // tr,),
        in_specs=[pl.BlockSpec((k, tr, cdim), lambda i: (0, i, 0))], out_specs=pl.BlockSpec((tr, cdim), lambda i: (i, 0)),
        out_shape=jax.ShapeDtypeStruct((r, cdim), F32), compiler_params=_params(("parallel",)),
    )(parts)


ADA_SHARD = 6 * D_MODEL // N_CHIPS


def _mod_part(c_all, w_shard, b_shard):
    tn = 512

    def body(c_ref, w_ref, b_ref, o_ref):
        o_ref[...] = _dot(_silu(c_ref[...]).astype(BF16), w_ref[...].astype(BF16)) + b_ref[...]

    return pl.pallas_call(
        body, name="mod_part", grid=(ADA_SHARD // tn,),
        in_specs=[pl.BlockSpec((N_DEV, D_MODEL), lambda j: (0, 0)), pl.BlockSpec((D_MODEL, tn), lambda j: (0, j)),
                  pl.BlockSpec((1, tn), lambda j: (0, j))],
        out_specs=pl.BlockSpec((N_DEV, tn), lambda j: (0, j)),
        out_shape=jax.ShapeDtypeStruct((N_DEV, ADA_SHARD), F32), compiler_params=_params(("parallel",)),
    )(c_all, w_shard, b_shard)


def _w_ada_grad(c_all_t, dmod_shard):
    tm = 256

    def body(ct_ref, dm_ref, o_ref):
        act = _silu(ct_ref[...])
        acc = act[:, 0:1] * dm_ref[0:1, :]
        for dev in range(1, N_DEV):
            acc = acc + act[:, dev:dev + 1] * dm_ref[dev:dev + 1, :]
        o_ref[...] = acc

    return pl.pallas_call(
        body, name="w_ada_grad", grid=(D_MODEL // tm,),
        in_specs=[pl.BlockSpec((tm, N_DEV), lambda i: (i, 0)), pl.BlockSpec((N_DEV, ADA_SHARD), lambda i: (0, 0))],
        out_specs=pl.BlockSpec((tm, ADA_SHARD), lambda i: (i, 0)),
        out_shape=jax.ShapeDtypeStruct((D_MODEL, ADA_SHARD), F32), compiler_params=_params(("parallel",)),
    )(c_all_t, dmod_shard)


def _adamw_math(w, g, m, v):
    nm = ADAM_B1 * m + (1.0 - ADAM_B1) * g
    nv = ADAM_B2 * v + (1.0 - ADAM_B2) * jnp.square(g)
    m_hat = nm / (1.0 - ADAM_B1 ** ADAM_STEP)
    v_hat = nv / (1.0 - ADAM_B2 ** ADAM_STEP)
    return -ADAM_LR * (m_hat / (jnp.sqrt(v_hat) + ADAM_EPS) + ADAM_WD * w), nm, nv


def _adamw(name, w, g, m, v):
    _, r, cdim = w.shape
    tr = 256 if r % 256 == 0 else r

    def body(w_ref, g_ref, m_ref, v_ref, go_ref, d_ref, nm_ref, nv_ref):
        go_ref[...] = g_ref[...]
        d_ref[...], nm_ref[...], nv_ref[...] = _adamw_math(w_ref[...], g_ref[...], m_ref[...], v_ref[...])

    blk = pl.BlockSpec((None, tr, cdim), lambda i: (0, i, 0))
    return pl.pallas_call(
        body, name=name, grid=(r // tr,), in_specs=[blk, pl.BlockSpec((tr, cdim), lambda i: (i, 0)), blk, blk], out_specs=[blk] * 4,
        out_shape=[jax.ShapeDtypeStruct((1, r, cdim), F32)] * 4, compiler_params=_params(("parallel",)),
    )(w, g, m, v)


def _adamw_pair(name, w, mine, other, m, v, c, by_cols=False):
    _, r, cdim = w.shape
    hr, hc = mine.shape
    tr = _row_tile(hr, SUBLANES)
    per = hr // tr

    def body(c_ref, w_ref, a_ref, b_ref, m_ref, v_ref, g_ref, d_ref, nm_ref, nv_ref):
        half = pl.program_id(1) if by_cols else pl.program_id(0) // per
        g = jnp.where(half == c_ref[0], a_ref[...], b_ref[...])
        g_ref[...] = g
        d_ref[...], nm_ref[...], nv_ref[...] = _adamw_math(w_ref[...], g, m_ref[...], v_ref[...])

    blk = pl.BlockSpec((None, tr, hc), lambda i, j, c_ref: (0, i, j))
    half = pl.BlockSpec((tr, hc), lambda i, j, c_ref: (i % per, 0))
    return pl.pallas_call(
        body, name=name,
        grid_spec=pltpu.PrefetchScalarGridSpec(num_scalar_prefetch=1, grid=(r // tr, cdim // hc),
                                               in_specs=[blk, half, half, blk, blk], out_specs=[blk] * 4),
        out_shape=[jax.ShapeDtypeStruct((1, r, cdim), F32)] * 4, compiler_params=_params(("parallel", "parallel")),
    )(jnp.reshape(c, (1,)).astype(jnp.int32), w, mine, other, m, v)


SMALL = ["b_ada", "conv_b", "dt_bias", "a_log", "d_skip", "ssm_norm_w", "f_bias", "attn_norm_w", "ln1_g", "ln1_b", "ln2_g", "ln2_b"]


def _pack(vs):
    pieces = []
    for v in vs:
        pieces.append(v)
        if v.shape[1] % LANES:
            pieces.append(jnp.zeros((1, -v.shape[1] % LANES), v.dtype))
    return jnp.concatenate(pieces, axis=1)


def _adamw_small(total, offs, ws, ms, vs):
    n = len(ws)

    def body(*refs):
        t_ref, outs = refs[0], refs[1 + 3 * n:]
        for i in range(n):
            g = t_ref[:, offs[i]:offs[i] + ws[i].shape[1]]
            dl, nm, nv = _adamw_math(refs[1 + i][...], g, refs[1 + n + i][...], refs[1 + 2 * n + i][...])
            outs[4 * i][...], outs[4 * i + 1][...], outs[4 * i + 2][...], outs[4 * i + 3][...] = g, dl, nm, nv

    res = pl.pallas_call(
        body, name="adamw_small", in_specs=[VMEM_SPEC] * (1 + 3 * n), out_specs=[VMEM_SPEC] * (4 * n),
        out_shape=[jax.ShapeDtypeStruct(w.shape, F32) for w in ws for _ in range(4)],
    )(total, *ws, *ms, *vs)
    return [res[4 * i:4 * i + 4] for i in range(n)]


def kernel(x, c, w_ada, b_ada, w_in, conv_w, conv_b, dt_bias, a_log, d_skip, ssm_norm_w, f_bias, attn_norm_w, w_out, ln1_g, ln1_b, w_ff_in, w_ff_out, ln2_g, ln2_b, loss_target, m_w_ada, m_b_ada, m_w_in, m_conv_w, m_conv_b, m_dt_bias, m_a_log, m_d_skip, m_ssm_norm_w, m_f_bias, m_attn_norm_w, m_w_out, m_ln1_g, m_ln1_b, m_w_ff_in, m_w_ff_out, m_ln2_g, m_ln2_b, v_w_ada, v_b_ada, v_w_in, v_conv_w, v_conv_b, v_dt_bias, v_a_log, v_d_skip, v_ssm_norm_w, v_f_bias, v_attn_norm_w, v_w_out, v_ln1_g, v_ln1_b, v_w_ff_in, v_w_ff_out, v_ln2_g, v_ln2_b):
    a = dict(b_ada=b_ada, conv_b=conv_b, dt_bias=dt_bias, a_log=a_log, d_skip=d_skip, ssm_norm_w=ssm_norm_w, f_bias=f_bias,
             attn_norm_w=attn_norm_w, ln1_g=ln1_g, ln1_b=ln1_b, ln2_g=ln2_g, ln2_b=ln2_b)
    ms = dict(b_ada=m_b_ada, conv_b=m_conv_b, dt_bias=m_dt_bias, a_log=m_a_log, d_skip=m_d_skip, ssm_norm_w=m_ssm_norm_w,
              f_bias=m_f_bias, attn_norm_w=m_attn_norm_w, ln1_g=m_ln1_g, ln1_b=m_ln1_b, ln2_g=m_ln2_g, ln2_b=m_ln2_b)
    vs = dict(b_ada=v_b_ada, conv_b=v_conv_b, dt_bias=v_dt_bias, a_log=v_a_log, d_skip=v_d_skip, ssm_norm_w=v_ssm_norm_w,
              f_bias=v_f_bias, attn_norm_w=v_attn_norm_w, ln1_g=v_ln1_g, ln1_b=v_ln1_b, ln2_g=v_ln2_g, ln2_b=v_ln2_b)
    xi, yi, ci = _place()
    chip = 2 * xi + yi
    me = 4 * xi + 2 * yi + ci
    d = D_MODEL
    conv_shard = CONV_DIM // N_CHIPS

    first = _allgather_small("gather_c", jnp.concatenate([c, conv_w[0].reshape(1, CONV_W * conv_shard)], axis=1))[:, 0]
    c_all = first[:, :d]
    conv_w_full = first[::2, d:].reshape(N_CHIPS, CONV_W, conv_shard).transpose(1, 0, 2).reshape(CONV_W, CONV_DIM)
    b_shard = lax.dynamic_slice_in_dim(b_ada, chip * ADA_SHARD, ADA_SHARD, axis=1)
    parts = _allgather_small("gather_mod", _mod_part(c_all, w_ada[0], b_shard))
    mod = lax.dynamic_index_in_dim(parts[::2], me, axis=1, keepdims=False).reshape(1, 6 * d)

    w_in_t, m_w_in_t, v_w_in_t = [jnp.transpose(t, (0, 2, 1)) for t in (w_in, m_w_in, v_w_in)]
    w_alt = _to_aligned(_gather_shards(w_in_t[0].astype(BF16)).reshape(IN_COLS, d))

    sp = {n: a[n] for n in SMALL[1:]}
    sp["conv_w"] = conv_w_full
    shards = [w_out[0].astype(BF16), w_ff_in[0].astype(BF16), w_ff_out[0].astype(BF16)]
    dx, landed, small = _local_step(x[0], loss_target[0], mod, w_alt, shards, sp)

    mine = [_sum_blocks("dev_sum_%d" % i, p) for i, p in enumerate(landed)]
    names = ["mod"] + SMALL[1:]
    vec = _pack([small[n] for n in names] + [small["conv_w"].reshape(1, CONV_W * CONV_DIM), small["loss"]])
    every, total, other = _tail_exchange(vec, mine)
    widths = [6 * d] + [a[n].shape[1] for n in SMALL[1:]]
    offs = [0]
    for w in widths:
        offs.append(offs[-1] + w + (-w % LANES))
    g_conv_w_full = total[:, offs[-1]:offs[-1] + CONV_W * CONV_DIM].reshape(CONV_W, CONV_DIM)
    loss = total[0, offs[-1] + CONV_W * CONV_DIM]
    dmod_shard = lax.dynamic_slice_in_dim(every[:, 0, :6 * d], chip * ADA_SHARD, ADA_SHARD, axis=1)
    g_w_ada = _w_ada_grad(c_all.T, dmod_shard)
    g_conv_w = lax.dynamic_slice_in_dim(g_conv_w_full, chip * conv_shard, conv_shard, axis=1)

    grads, deltas, new_m, new_v = {}, {}, {}, {}
    paired = dict(w_in=(w_in_t, m_w_in_t, v_w_in_t), w_out=(w_out, m_w_out, v_w_out), w_ff_in=(w_ff_in, m_w_ff_in, v_w_ff_in),
                  w_ff_out=(w_ff_out, m_w_ff_out, v_w_ff_out))
    for i, (n, (w, m, v)) in enumerate(paired.items()):
        res = _adamw_pair("adamw_" + n, w, mine[i], other[i], m, v, ci, by_cols=n == "w_in")
        grads[n], deltas[n], new_m[n], new_v[n] = [jnp.transpose(t, (0, 2, 1)) for t in res] if n == "w_in" else res
    for n, g, (w, m, v) in (("w_ada", g_w_ada, (w_ada, m_w_ada, v_w_ada)), ("conv_w", g_conv_w, (conv_w, m_conv_w, v_conv_w))):
        grads[n], deltas[n], new_m[n], new_v[n] = _adamw("adamw_" + n, w, g, m, v)
    for n, res in zip(SMALL, _adamw_small(total, offs, [a[n] for n in SMALL], [ms[n] for n in SMALL], [vs[n] for n in SMALL])):
        grads[n], deltas[n], new_m[n], new_v[n] = res

    order = ["w_ada", "b_ada", "w_in", "conv_w", "conv_b", "dt_bias", "a_log", "d_skip", "ssm_norm_w", "f_bias", "attn_norm_w", "w_out",
             "ln1_g", "ln1_b", "w_ff_in", "w_ff_out", "ln2_g", "ln2_b"]
    return (loss, dx[None], *[grads[n] for n in order], *[deltas[n] for n in order], *[new_m[n] for n in order], *[new_v[n] for n in order])
```

```python
import functools

import jax
import jax.numpy as jnp
from jax import lax
from jax.experimental import pallas as pl
from jax.experimental.pallas import tpu as pltpu

F32, BF16 = jnp.float32, jnp.bfloat16

D_MODEL = 1024
N_HEADS = 16
HEAD_DIM = 64
N_PAIRS = N_HEADS // 2
SSM_GROUPS = 2
SSM_STATE = 128
CHUNK = 128
CONV_W = 4
CONV_DIM = 1536
D_FF = 4096
IN_COLS = 5664
ALPHA = 2.0 ** 0.25
LN_EPS = 1e-5
RMS_EPS = 1e-5
LANES = 128
SUBLANES = 8

AL_Z, AL_XS, AL_Q, AL_K, AL_V, AL_B, AL_C, AL_DTF = 0, 1024, 2048, 3072, 4096, 5120, 5376, 5632
AL_COLS = 5760
F_LANE = 16

ADAM_LR, ADAM_B1, ADAM_B2, ADAM_EPS, ADAM_WD, ADAM_STEP = 0.001, 0.9, 0.999, 1e-08, 0.01, 10

VMEM_LIMIT = 56 * 1024 * 1024
SEQ_TK = 4096
MESH = pl.DeviceIdType.MESH


def _params(sem=None):
    return pltpu.CompilerParams(dimension_semantics=sem, vmem_limit_bytes=VMEM_LIMIT)


def _sigmoid(x):
    return 1.0 / (1.0 + jnp.exp(-x))


def _silu(x):
    return x * _sigmoid(x)


def _softplus(x):
    return jnp.maximum(x, 0.0) + jnp.log(1.0 + jnp.exp(-jnp.abs(x)))


def _split3(a):
    hi = a.astype(BF16)
    r = a - hi.astype(F32)
    mid = r.astype(BF16)
    lo = (r - mid.astype(F32)).astype(BF16)
    return hi, mid, lo


def _dot(a, b, dims=((1,), (0,))):
    return lax.dot_general(a, b, (dims, ((), ())), preferred_element_type=F32)


NN, NT, TN = ((1,), (0,)), ((1,), (1,)), ((0,), (0,))


def _dot3(t, a):
    hi, mid, lo = _split3(a)
    return _dot(t, hi) + _dot(t, mid) + _dot(t, lo)


def _matmul(name, a, b, *, dims=NN, out_dtype=F32, tm=1024, tn=1024, tk=1024, by_chip=None, epi=None, carry=()):
    if dims == NN:
        (m, k), n = a.shape, b.shape[1]
    elif dims == NT:
        (m, k), n = a.shape, b.shape[0]
    else:
        (k, m), n = a.shape, b.shape[1]
    if by_chip == "rows":
        tm = min(tm, m // 4)
    if by_chip == "cols":
        tn = min(tn, n // 4)
    tm, tn, tk = min(tm, m), min(tn, n), min(tk, k)
    assert m % tm == 0 and n % tn == 0 and k % tk == 0, (name, m, n, k, tm, tn, tk)
    nk = k // tk
    if by_chip == "rows":
        per = m // 4 // tm
        out_spec = pl.BlockSpec((None, tm, tn), lambda i, j, l: (i // per, i % per, j))
        out_shape = jax.ShapeDtypeStruct((4, m // 4, n), out_dtype)
    elif by_chip == "cols":
        per = n // 4 // tn
        out_spec = pl.BlockSpec((None, tm, tn), lambda i, j, l: (j // per, i, j % per))
        out_shape = jax.ShapeDtypeStruct((4, m, n // 4), out_dtype)
    else:
        out_spec = pl.BlockSpec((tm, tn), lambda i, j, l: (i, j))
        out_shape = jax.ShapeDtypeStruct((m, n), out_dtype)
    a_spec = pl.BlockSpec((tk, tm), lambda i, j, l: (l, i)) if dims == TN else pl.BlockSpec((tm, tk), lambda i, j, l: (i, l))
    b_spec = pl.BlockSpec((tn, tk), lambda i, j, l: (j, l)) if dims == NT else pl.BlockSpec((tk, tn), lambda i, j, l: (l, j))

    tile = pl.BlockSpec((tm, tn), lambda i, j, l: (i, j))
    in_specs, args, out_specs, out_shape = [a_spec, b_spec], [a, b], [out_spec], [out_shape]
    fn, n_tiles, n_sums = None, 1, 0
    if epi is not None:
        fn, fulls, vecs, outs, sums = epi
        assert by_chip is None and (not sums or n == tn), name
        in_specs = in_specs + [tile] * len(fulls) + [pl.BlockSpec((1, tn), lambda i, j, l: (0, j))] * len(vecs)
        args = args + list(fulls) + list(vecs)
        flipped = pl.BlockSpec((tn, tm), lambda i, j, l: (j, i))
        out_specs = [flipped if isinstance(dt, tuple) else tile for dt in outs] + [pl.BlockSpec((1, w), lambda i, j, l: (0, 0)) for w in sums]
        out_shape = [jax.ShapeDtypeStruct((n, m), dt[1]) if isinstance(dt, tuple) else jax.ShapeDtypeStruct((m, n), dt) for dt in outs]
        out_shape += [jax.ShapeDtypeStruct((1, w), F32) for w in sums]
        n_tiles, n_sums = len(outs), len(sums)
    n_in, n_out, n_c = len(args), len(out_specs), len(carry)
    scratch = [pltpu.VMEM((tm, tn) if nk > 1 else (SUBLANES, LANES), F32)]
    if n_c:
        in_specs, args = in_specs + [ANY] * n_c, args + list(carry)
        out_specs = out_specs + [ANY] * n_c
        out_shape = out_shape + [jax.ShapeDtypeStruct(g.shape, g.dtype) for g in carry]
        scratch = scratch + _exchange_sems(n_c)
    gm, gn = m // tm, n // tn

    def body(*refs):
        a_ref, b_ref = refs[:2]
        ins, outs = refs[2:n_in], refs[n_in + n_c:n_in + n_c + n_out]
        acc_ref = refs[n_in + 2 * n_c + n_out]
        i, j, l = pl.program_id(0), pl.program_id(1), pl.program_id(2)
        if n_c:
            start, wait = _scatter_plan(refs[n_in:n_in + n_c], refs[n_in + n_c + n_out:n_in + 2 * n_c + n_out], *refs[n_in + 2 * n_c + n_out + 1:])
            pl.when((i == 0) & (j == 0) & (l == 0))(start)
        part = _dot(a_ref[...].astype(BF16), b_ref[...].astype(BF16), dims)

        def finish(res):
            if fn is None:
                outs[0][...] = res.astype(outs[0].dtype)
                return
            tiles, colsums = fn(res, *[r[...] for r in ins])
            for r, val in zip(outs[:n_tiles], tiles):
                r[...] = val.astype(r.dtype)
            if n_sums:
                @pl.when(i == 0)
                def _():
                    for r in outs[n_tiles:]:
                        r[...] = jnp.zeros_like(r)
                for r, val in zip(outs[n_tiles:], colsums):
                    r[...] += val

        if nk == 1:
            finish(part)
        else:
            @pl.when(l == 0)
            def _():
                acc_ref[...] = part

            @pl.when((l > 0) & (l < nk - 1))
            def _():
                acc_ref[...] += part

            @pl.when(l == nk - 1)
            def _():
                finish(acc_ref[...] + part)

        if n_c:
            pl.when((i == gm - 1) & (j == gn - 1) & (l == nk - 1))(wait)

    res = pl.pallas_call(
        body, name=name, grid=(gm, gn, nk),
        in_specs=in_specs, out_specs=out_specs, out_shape=out_shape, scratch_shapes=scratch,
        compiler_params=_params(("arbitrary",) * 3 if n_c or n_sums else ("parallel", "parallel", "arbitrary")),
    )(*args)
    return res[0] if len(res) == 1 else res


def _rowwise(name, fn, fulls, vecs, out_fulls, out_vecs, tr=256):
    fulls = [f if isinstance(f, tuple) else (f, f.shape[1], 0) for f in fulls]
    s = fulls[0][0].shape[0]
    tr = min(tr, s)
    out_fulls = [o if len(o) == 3 else (*o, (o[0], 0, None)) for o in out_fulls]
    into = [(k, slab[2]) for k, (_, _, slab) in enumerate(out_fulls) if slab[2] is not None]
    nf, nv, nof, nov = len(fulls), len(vecs), len(out_fulls), len(out_vecs)
    in_specs = [pl.BlockSpec((tr, w), functools.partial(lambda i, cb: (i, cb), cb=cb)) for (_, w, cb) in fulls]
    in_specs += [pl.BlockSpec(v.shape, lambda i: (0, 0)) for v in vecs] + [ANY] * len(into)
    out_shape = [jax.ShapeDtypeStruct((s, slab[0]), dt) for (_, dt, slab) in out_fulls] + [jax.ShapeDtypeStruct((1, w), F32) for w in out_vecs]
    out_specs = [pl.BlockSpec((tr, w), functools.partial(lambda i, cb: (i, cb), cb=slab[1])) for (w, _, slab) in out_fulls]
    out_specs += [pl.BlockSpec((1, w), lambda i: (0, 0)) for w in out_vecs]

    def body(*refs):
        outs = refs[nf + nv + len(into):]
        of, ov = fn(*[r[...] for r in refs[:nf + nv]])
        for r, val in zip(outs[:nof], of):
            r[...] = val.astype(r.dtype)
        if nov:
            @pl.when(pl.program_id(0) == 0)
            def _():
                for r in outs[nof:]:
                    r[...] = jnp.zeros_like(r)
            for r, val in zip(outs[nof:], ov):
                r[...] += val

    res = pl.pallas_call(
        body, name=name, grid=(s // tr,), in_specs=in_specs, out_specs=out_specs, out_shape=out_shape,
        input_output_aliases={nf + nv + pos: k for pos, (k, _) in enumerate(into)},
        compiler_params=_params(("arbitrary",)),
    )(*[f[0] for f in fulls], *vecs, *[buf for _, buf in into])
    return res[:nof], res[nof:]


def _colsum(x):
    return jnp.sum(x, axis=0, keepdims=True)


def _rowmean(x):
    return jnp.mean(x, axis=-1, keepdims=True)


CONV_CB = 512
CONV_TR = 512


def _shift_down(u, halo, j):
    if j == 0:
        return u
    ru = pltpu.roll(u, j, 0)
    row8 = lax.broadcasted_iota(jnp.int32, halo.shape, 0)
    top = jnp.where(row8 < j, pltpu.roll(halo, j, 0), ru[:SUBLANES])
    if u.shape[0] == SUBLANES:
        return top
    return jnp.concatenate([top, ru[SUBLANES:]], axis=0)


def _shift_up(d, halo, j):
    if j == 0:
        return d
    tr = d.shape[0]
    rd = pltpu.roll(d, tr - j, 0)
    row8 = lax.broadcasted_iota(jnp.int32, halo.shape, 0)
    bot = jnp.where(row8 >= SUBLANES - j, pltpu.roll(halo, SUBLANES - j, 0), rd[tr - SUBLANES:])
    return jnp.concatenate([rd[:tr - SUBLANES], bot], axis=0)


def _conv_col(cb):
    return jnp.where(cb < 2, AL_XS // CONV_CB + cb, AL_B // CONV_CB)


def _conv_specs(s, tr):
    per8 = tr // SUBLANES
    blk = pl.BlockSpec((tr, CONV_CB), lambda cb, i: (i, _conv_col(cb)))
    prev = pl.BlockSpec((SUBLANES, CONV_CB), lambda cb, i: (jnp.maximum(i * per8 - 1, 0), _conv_col(cb)))
    return blk, prev


def _conv_pre(u, halo, w_ref, b_ref, first):
    halo = jnp.where(first, 0.0, halo)
    acc = b_ref[...] + w_ref[CONV_W - 1:CONV_W, :] * u
    shifted = [u]
    for j in range(1, CONV_W):
        sh = _shift_down(u, halo, j)
        shifted.append(sh)
        acc = acc + w_ref[CONV_W - 1 - j:CONV_W - j, :] * sh
    return acc, shifted


def _silu_grad(pre):
    sg = _sigmoid(pre)
    return sg * (1.0 + pre * (1.0 - sg))


def _conv_bwd(proj, conv_w, conv_b, dxc, dproj):
    s = proj.shape[0]
    tr = min(CONV_TR, s)
    nb = s // tr
    per8 = tr // SUBLANES
    last8 = s // SUBLANES - 1
    blk, prev = _conv_specs(s, tr)

    def body(u_ref, h_ref, un_ref, w_ref, b_ref, d_ref, dn_ref, _, du_ref, dw_ref, db_ref):
        i = pl.program_id(1)
        u = u_ref[...]
        pre, shifted = _conv_pre(u, h_ref[...], w_ref, b_ref, i == 0)
        dpre = d_ref[...] * _silu_grad(pre)
        pre_n, _ = _conv_pre(un_ref[...], u[tr - SUBLANES:], w_ref, b_ref, False)
        dpre_n = jnp.where(i == nb - 1, 0.0, dn_ref[...] * _silu_grad(pre_n))
        acc = w_ref[CONV_W - 1:CONV_W, :] * dpre
        for j in range(1, CONV_W):
            acc = acc + w_ref[CONV_W - 1 - j:CONV_W - j, :] * _shift_up(dpre, dpre_n, j)
        du_ref[...] = acc.astype(du_ref.dtype)

        @pl.when(i == 0)
        def _():
            dw_ref[...] = jnp.zeros_like(dw_ref)
            db_ref[...] = jnp.zeros_like(db_ref)

        db_ref[...] += _colsum(dpre)
        for j in range(CONV_W):
            dw_ref[CONV_W - 1 - j:CONV_W - j, :] += _colsum(dpre * shifted[j])

    nxt_row = lambda i: jnp.minimum((i + 1) * per8, last8)
    own = pl.BlockSpec((tr, CONV_CB), lambda cb, i: (i, cb))
    wspec = pl.BlockSpec((CONV_W, CONV_CB), lambda cb, i: (0, cb))
    bspec = pl.BlockSpec((1, CONV_CB), lambda cb, i: (0, cb))
    return pl.pallas_call(
        body, name="conv_bwd", grid=(CONV_DIM // CONV_CB, nb),
        in_specs=[blk, prev, pl.BlockSpec((SUBLANES, CONV_CB), lambda cb, i: (nxt_row(i), _conv_col(cb))), wspec, bspec, own,
                  pl.BlockSpec((SUBLANES, CONV_CB), lambda cb, i: (nxt_row(i), cb)), ANY],
        out_specs=[pl.BlockSpec((tr, CONV_CB), lambda cb, i: (i, _conv_col(cb))), wspec, bspec],
        out_shape=[jax.ShapeDtypeStruct(dproj.shape, dproj.dtype), jax.ShapeDtypeStruct((CONV_W, CONV_DIM), F32),
                   jax.ShapeDtypeStruct((1, CONV_DIM), F32)],
        input_output_aliases={7: 0}, compiler_params=_params(("parallel", "arbitrary")),
    )(proj, proj, proj, conv_w, conv_b, dxc, dxc, dproj)


XC_B, XC_C = 1024, 1280


def _tile_iotas():
    row = lax.broadcasted_iota(jnp.int32, (CHUNK, LANES), 0)
    lane = lax.broadcasted_iota(jnp.int32, (CHUNK, LANES), 1)
    return row, lane


def _ssd_scalars(dtf_ref, bias_ref, alog_ref, row, lane):
    head = lane[:1] < N_HEADS
    raw = dtf_ref[...] + bias_ref[...]
    dt = _softplus(raw)
    a_neg = jnp.where(head, -jnp.exp(alog_ref[...]), 0.0)
    a = dt * a_neg
    tril = (row >= lane).astype(BF16)
    s = _dot3(tril, a)
    return raw, dt, a_neg, s


def _pair(v, j, lo):
    return jnp.where(lo, v[:, 2 * j:2 * j + 1], v[:, 2 * j + 1:2 * j + 2])


def _head_sum(x, lo, hh):
    return jnp.sum(jnp.where(lo == (hh == 0), x, 0.0), axis=1, keepdims=True)


def _decay_masks(s, st, h, row, lane):
    s_col = jnp.broadcast_to(s[:, h:h + 1], (CHUNK, LANES))
    s_row = jnp.broadcast_to(st[h:h + 1, :], (CHUNK, LANES))
    lm = jnp.where(row >= lane, jnp.exp(s_col - s_row), 0.0)
    lmt = jnp.where(row <= lane, jnp.exp(s_row - s_col), 0.0)
    return lm, lmt


def _gated_norm(y, z, w):
    g = y * _silu(z)
    return g * lax.rsqrt(_rowmean(g * g) + RMS_EPS) * w


def _ssd_fwd(proj, conv_w, conv_b, dt_bias_l, a_log_l, d_exp, norm_w):
    s_len = proj.shape[0]
    nc = s_len // CHUNK

    def body(xs_ref, bc_ref, cw_ref, cb_ref, dtf_ref, bias_ref, alog_ref, dexp_ref, z_ref, w_ref,
             x_ref, y_ref, prevs_ref, ymix_ref, state_ref, halo_ref):
        first = pl.program_id(0) == 0

        @pl.when(first)
        def _():
            state_ref[...] = jnp.zeros_like(state_ref)
            halo_ref[...] = jnp.zeros_like(halo_ref)

        u = jnp.concatenate([xs_ref[...], bc_ref[...]], axis=1)
        pre, _ = _conv_pre(u, halo_ref[...], cw_ref, cb_ref, first)
        halo_ref[...] = u[CHUNK - SUBLANES:]
        x_ref[...] = _silu(pre)

        row, lane = _tile_iotas()
        lo = lane < HEAD_DIM
        _, dt, _, s = _ssd_scalars(dtf_ref, bias_ref, alog_ref, row, lane)
        tot = s[CHUNK - 1:CHUNK, :]
        st = s.T
        for g in range(SSM_GROUPS):
            bg = x_ref[:, XC_B + g * SSM_STATE:XC_B + (g + 1) * SSM_STATE].astype(BF16)
            cg = x_ref[:, XC_C + g * SSM_STATE:XC_C + (g + 1) * SSM_STATE].astype(BF16)
            cb = _dot(cg, bg, NT)
            for j in range(g * 4, g * 4 + 4):
                xs_p = x_ref[:, j * LANES:(j + 1) * LANES]
                dt_p, s_p, tot_p = _pair(dt, j, lo), _pair(s, j, lo), _pair(tot, j, lo[:1])
                xc_p = xs_p * dt_p
                xc_b = xc_p.astype(BF16)
                yd = []
                for hh in range(2):
                    lm, _ = _decay_masks(s, st, 2 * j + hh, row, lane)
                    yd.append(_dot((cb * lm).astype(BF16), xc_b))
                prev = state_ref[j]
                prevs_ref[0, j] = prev
                yo = _dot(cg, prev.astype(BF16)) * jnp.exp(s_p)
                y_ref[:, j * LANES:(j + 1) * LANES] = jnp.where(lo, yd[0], yd[1]) + yo + dexp_ref[:, j * LANES:(j + 1) * LANES] * xs_p
                to_end = jnp.exp(tot_p - s_p)
                state_ref[j] = jnp.exp(tot_p) * prev + _dot(bg, (xc_p * to_end).astype(BF16), TN)
        ymix_ref[...] = _gated_norm(y_ref[...], z_ref[...], w_ref[...]).astype(ymix_ref.dtype)

    vec = lambda w: pl.BlockSpec((1, w), lambda c: (0, 0))
    rows = pl.BlockSpec((CHUNK, D_MODEL), lambda c: (c, 0))
    return pl.pallas_call(
        body, name="ssd_fwd", grid=(nc,),
        in_specs=[pl.BlockSpec((CHUNK, D_MODEL), lambda c: (c, AL_XS // D_MODEL)),
                  pl.BlockSpec((CHUNK, CONV_DIM - D_MODEL), lambda c: (c, AL_B // (CONV_DIM - D_MODEL))),
                  pl.BlockSpec((CONV_W, CONV_DIM), lambda c: (0, 0)), vec(CONV_DIM),
                  pl.BlockSpec((CHUNK, LANES), lambda c: (c, AL_DTF // LANES)),
                  vec(LANES), vec(LANES), vec(D_MODEL), pl.BlockSpec((CHUNK, D_MODEL), lambda c: (c, AL_Z // D_MODEL)), vec(D_MODEL)],
        out_specs=[pl.BlockSpec((CHUNK, CONV_DIM), lambda c: (c, 0)), rows,
                   pl.BlockSpec((1, N_PAIRS, SSM_STATE, LANES), lambda c: (c, 0, 0, 0)), rows],
        out_shape=[jax.ShapeDtypeStruct((s_len, CONV_DIM), F32), jax.ShapeDtypeStruct((s_len, D_MODEL), F32),
                   jax.ShapeDtypeStruct((nc, N_PAIRS, SSM_STATE, LANES), F32), jax.ShapeDtypeStruct((s_len, 2 * D_MODEL), BF16)],
        scratch_shapes=[pltpu.VMEM((N_PAIRS, SSM_STATE, LANES), F32), pltpu.VMEM((SUBLANES, CONV_DIM), F32)],
        compiler_params=_params(("arbitrary",)),
    )(proj, proj, conv_w, conv_b, proj, dt_bias_l, a_log_l, d_exp, proj, norm_w)


def _ssd_bwd(xc_all, proj, dt_bias_l, a_log_l, d_exp, prevs, y_ssd, dymix, norm_w):
    s_len = xc_all.shape[0]
    nc = s_len // CHUNK

    def body(x_ref, dtf_ref, bias_ref, alog_ref, dexp_ref, prevs_ref, y_ref, z_ref, dym_ref, w_ref,
             dx_ref, ddt_ref, da_ref, dd_ref, dbias_ref, dz_ref, dw_ref, dstate_ref):
        @pl.when(pl.program_id(0) == 0)
        def _():
            dstate_ref[...] = jnp.zeros_like(dstate_ref)
            da_ref[...] = jnp.zeros_like(da_ref)
            dd_ref[...] = jnp.zeros_like(dd_ref)
            dbias_ref[...] = jnp.zeros_like(dbias_ref)
            dw_ref[...] = jnp.zeros_like(dw_ref)

        y, z, dyo = y_ref[...], z_ref[...], dym_ref[...]
        sg = _sigmoid(z)
        sz = z * sg
        gated = y * sz
        rn = lax.rsqrt(_rowmean(gated * gated) + RMS_EPS)
        dg = _rms_bwd(dyo * w_ref[...], gated, rn)
        dy_full = dg * sz
        dz_ref[...] = (dg * y * (sg * (1.0 + z * (1.0 - sg)))).astype(dz_ref.dtype)
        dw_ref[...] += _colsum(dyo * gated * rn)

        row, lane = _tile_iotas()
        lo = lane < HEAD_DIM
        last = row == CHUNK - 1
        raw, dt, a_neg, s = _ssd_scalars(dtf_ref, bias_ref, alog_ref, row, lane)
        tot = s[CHUNK - 1:CHUNK, :]
        st = s.T
        ds_acc = jnp.zeros((CHUNK, LANES), F32)
        ddt_acc = jnp.zeros((CHUNK, LANES), F32)
        for g in range(SSM_GROUPS):
            bcol = slice(XC_B + g * SSM_STATE, XC_B + (g + 1) * SSM_STATE)
            ccol = slice(XC_C + g * SSM_STATE, XC_C + (g + 1) * SSM_STATE)
            bg = x_ref[:, bcol].astype(BF16)
            cg = x_ref[:, ccol].astype(BF16)
            cb = _dot(cg, bg, NT)
            cbt = _dot(bg, cg, NT)
            dcb = jnp.zeros((CHUNK, LANES), F32)
            dcbt = jnp.zeros((CHUNK, LANES), F32)
            db_acc = jnp.zeros((CHUNK, LANES), F32)
            dc_acc = jnp.zeros((CHUNK, LANES), F32)
            for j in range(g * 4, g * 4 + 4):
                cols = slice(j * LANES, (j + 1) * LANES)
                xs_p, dy_p = x_ref[:, cols], dy_full[:, cols]
                dt_p, s_p, tot_p = _pair(dt, j, lo), _pair(s, j, lo), _pair(tot, j, lo[:1])
                xc_p = xs_p * dt_p
                xc_b, dy_b = xc_p.astype(BF16), dy_p.astype(BF16)
                e_p, f_p, etot_p = jnp.exp(s_p), jnp.exp(tot_p - s_p), jnp.exp(tot_p)
                prev, dnext = prevs_ref[0, j], dstate_ref[j]
                prev_b, dnext_b = prev.astype(BF16), dnext.astype(BF16)
                dd_ref[:, cols] += _colsum(dy_p * xs_p)
                dxs_p = dexp_ref[:, cols] * dy_p
                cp = _dot(cg, prev_b)
                gy = (dy_p * e_p).astype(BF16)
                dc_acc += _dot(gy, prev_b, NT)
                dstate_ref[j] = etot_p * dnext + _dot(cg, gy, TN)
                de = dy_p * cp * e_p
                bds = _dot(bg, dnext_b)
                db_acc += _dot((xc_p * f_p).astype(BF16), dnext_b, NT)
                dxc_p = bds * f_p
                df = bds * xc_p * f_p
                dtot_p = _colsum(dnext * prev) * etot_p + _colsum(df)
                dsl = de - df + jnp.where(last, dtot_p, 0.0)
                for hh in range(2):
                    h = 2 * j + hh
                    mine = lo == (hh == 0)
                    lm, lmt = _decay_masks(s, st, h, row, lane)
                    dy_h = jnp.where(mine, dy_p, 0.0).astype(BF16)
                    xc_h = jnp.where(mine, xc_p, 0.0).astype(BF16)
                    dm = _dot(dy_h, xc_b, NT)
                    dmt = _dot(xc_h, dy_b, NT)
                    mt = cbt * lmt
                    dxc_p += _dot(mt.astype(BF16), dy_h)
                    dml, dmtl = dm * lm, dmt * lmt
                    ds_h = jnp.sum(dml * cb - dmtl * cbt + jnp.where(mine, dsl, 0.0), axis=1, keepdims=True)
                    ds_acc += jnp.where(lane == h, ds_h, 0.0)
                    dcb += dml
                    dcbt += dmtl
                    ddt_acc += jnp.where(lane == h, _head_sum(dxc_p * xs_p, lo, hh), 0.0)
                dx_ref[:, cols] = dxs_p + dxc_p * dt_p
            dx_ref[:, ccol] = dc_acc + _dot(dcb.astype(BF16), bg)
            dx_ref[:, bcol] = db_acc + _dot(dcbt.astype(BF16), cg)
        triu = (row <= lane).astype(BF16)
        da = _dot3(triu, ds_acc)
        ddt = ddt_acc + da * a_neg
        da_ref[...] += _colsum(da * dt) * a_neg[:1]
        ddt_raw = jnp.where(lane < N_HEADS, ddt * _sigmoid(raw), 0.0)
        dbias_ref[...] += _colsum(ddt_raw)
        ddt_ref[...] = ddt_raw

    rev = lambda c: nc - 1 - c
    vec = lambda w: pl.BlockSpec((1, w), lambda c: (0, 0))
    rows = lambda cb: pl.BlockSpec((CHUNK, D_MODEL), lambda c: (rev(c), cb))
    return pl.pallas_call(
        body, name="ssd_bwd", grid=(nc,),
        in_specs=[pl.BlockSpec((CHUNK, CONV_DIM), lambda c: (rev(c), 0)), pl.BlockSpec((CHUNK, LANES), lambda c: (rev(c), AL_DTF // LANES)),
                  vec(LANES), vec(LANES), vec(D_MODEL),
                  pl.BlockSpec((1, N_PAIRS, SSM_STATE, LANES), lambda c: (rev(c), 0, 0, 0)),
                  rows(0), rows(AL_Z // D_MODEL), rows(0), vec(D_MODEL)],
        out_specs=[pl.BlockSpec((CHUNK, CONV_DIM), lambda c: (rev(c), 0)), pl.BlockSpec((CHUNK, LANES), lambda c: (rev(c), 0)),
                   vec(LANES), vec(D_MODEL), vec(LANES), rows(AL_Z // D_MODEL), vec(D_MODEL)],
        out_shape=[jax.ShapeDtypeStruct((s_len, CONV_DIM), F32), jax.ShapeDtypeStruct((s_len, LANES), F32),
                   jax.ShapeDtypeStruct((1, LANES), F32), jax.ShapeDtypeStruct((1, D_MODEL), F32), jax.ShapeDtypeStruct((1, LANES), F32),
                   jax.ShapeDtypeStruct((s_len, AL_COLS), BF16), jax.ShapeDtypeStruct((1, D_MODEL), F32)],
        scratch_shapes=[pltpu.VMEM((N_PAIRS, SSM_STATE, LANES), F32)],
        compiler_params=_params(("arbitrary",)),
    )(xc_all, proj, dt_bias_l, a_log_l, d_exp, prevs, y_ssd, proj, dymix, norm_w)


AUG_LANES = 6


def _aug_base(hh):
    return HEAD_DIM if hh == 0 else 0


NEG = -1e30
ATT_T = 1024


def _fox_cum_bwd(dcum, proj, f_bias_l, ddt_tile, dproj):
    s_len = proj.shape[0]
    nc = s_len // CHUNK

    def body(dcum_ref, dtf_ref, fb_ref, ddt_ref, _, out_ref, dfb_ref):
        row, lane = _tile_iotas()
        triu = (row <= lane).astype(BF16)
        is_f = (lane >= F_LANE) & (lane < F_LANE + N_HEADS)

        def step(t, carry):
            run, dfb = carry
            rows = pl.ds(pl.multiple_of((nc - 1 - t) * CHUNK, CHUNK), CHUNK)
            rc = _dot3(triu, dcum_ref[rows, :]) + run
            sg = _sigmoid(-(dtf_ref[rows, :] + fb_ref[...]))
            df = jnp.where(is_f, pltpu.roll(rc, F_LANE, 1) * sg, 0.0)
            out_ref[rows, :] = (df + ddt_ref[rows, :]).astype(out_ref.dtype)
            return rc[0:1, :], dfb + _colsum(df)

        _, dfb = lax.fori_loop(0, nc, step, (jnp.zeros((1, LANES), F32), jnp.zeros((1, LANES), F32)))
        dfb_ref[...] = dfb

    whole = pl.BlockSpec((s_len, LANES), lambda i: (0, 0))
    dtf_cols = pl.BlockSpec((s_len, LANES), lambda i: (0, AL_DTF // LANES))
    vec = pl.BlockSpec((1, LANES), lambda i: (0, 0))
    return pl.pallas_call(
        body, name="fox_cum_bwd", grid=(1,),
        in_specs=[whole, dtf_cols, vec, whole, ANY], out_specs=[dtf_cols, vec],
        out_shape=[jax.ShapeDtypeStruct(dproj.shape, dproj.dtype), jax.ShapeDtypeStruct((1, LANES), F32)],
        input_output_aliases={4: 0}, compiler_params=_params(("arbitrary",)),
    )(dcum, proj, f_bias_l, ddt_tile, dproj)


def _attn_prep(proj, f_bias_l):
    s_len = proj.shape[0]
    tr = min(256, s_len)

    def body(q_ref, k_ref, v_ref, dtf_ref, fb_ref, qa_ref, ka_ref, vb_ref, carry_ref):
        @pl.when(pl.program_id(0) == 0)
        def _():
            carry_ref[...] = jnp.zeros_like(carry_ref)

        row_c, lane_c = _tile_iotas()
        tril = (row_c >= lane_c).astype(BF16)
        spread = [(jnp.where(lane_c == AUG_LANES * row_c + i, 1.0, 0.0) - jnp.where(lane_c == AUG_LANES * row_c + 3 + i, 1.0, 0.0)).astype(BF16)
                  for i in range(3)]
        run, pieces = carry_ref[...], []
        for ch in range(tr // CHUNK):
            lf = -_softplus(-(dtf_ref[ch * CHUNK:(ch + 1) * CHUNK, :] + fb_ref[...]))
            lf = jnp.where(lane_c < N_HEADS, pltpu.roll(lf, LANES - F_LANE, 1), 0.0)
            cs = _dot3(tril, lf) + run
            parts = _split3(cs)
            pieces.append(_dot(parts[0], spread[0]) + _dot(parts[1], spread[1]) + _dot(parts[2], spread[2]))
            run = cs[CHUNK - 1:CHUNK, :]
        carry_ref[...] = run
        c = jnp.concatenate(pieces, axis=0)

        lane = lax.broadcasted_iota(jnp.int32, (tr, LANES), 1)
        lo = lane < HEAD_DIM
        for p in range(N_PAIRS):
            cols = slice(p * LANES, (p + 1) * LANES)
            q, k = q_ref[:, cols] * (HEAD_DIM ** -0.5), k_ref[:, cols]
            for hh in range(2):
                base = _aug_base(hh)
                r = pltpu.roll(c, (base - AUG_LANES * (2 * p + hh)) % LANES, 1)
                first = (lane >= base) & (lane < base + 3)
                second = (lane >= base + 3) & (lane < base + AUG_LANES)
                mine = lo == (hh == 0)
                qa_ref[2 * p + hh] = jnp.where(mine, q, jnp.where(first, r, jnp.where(second, 1.0, 0.0))).astype(BF16)
                ka_ref[2 * p + hh] = jnp.where(mine, k, jnp.where(first, 1.0, jnp.where(second, r, 0.0))).astype(BF16)
        vb_ref[...] = v_ref[...].astype(BF16)

    assert AL_Q % D_MODEL == 0 and AL_K % D_MODEL == 0 and AL_V % D_MODEL == 0
    slab = lambda col0: pl.BlockSpec((tr, D_MODEL), lambda i: (i, col0 // D_MODEL))
    heads = pl.BlockSpec((N_HEADS, tr, LANES), lambda i: (0, i, 0))
    return pl.pallas_call(
        body, name="attn_prep", grid=(s_len // tr,),
        in_specs=[slab(AL_Q), slab(AL_K), slab(AL_V), pl.BlockSpec((tr, LANES), lambda i: (i, AL_DTF // LANES)),
                  pl.BlockSpec((1, LANES), lambda i: (0, 0))],
        out_specs=[heads, heads, pl.BlockSpec((tr, D_MODEL), lambda i: (i, 0))],
        out_shape=[jax.ShapeDtypeStruct((N_HEADS, s_len, LANES), BF16), jax.ShapeDtypeStruct((N_HEADS, s_len, LANES), BF16),
                   jax.ShapeDtypeStruct((s_len, D_MODEL), BF16)],
        scratch_shapes=[pltpu.VMEM((1, LANES), F32)],
        compiler_params=_params(("arbitrary",)),
    )(proj, proj, proj, proj, f_bias_l)


def _attn_fwd(qa, ka, vb, shards):
    s_len = vb.shape[0]
    t = min(ATT_T, s_len)
    nq = s_len // t
    n = len(shards)

    def body(qa_ref, ka_ref, vb_ref, *rest):
        o_ref, lse_ref = rest[n:n + 2]
        start, finish = _gather_plan(rest[:n], rest[n + 2:2 * n + 2], *rest[2 * n + 2:])
        i = pl.program_id(1)
        pl.when((pl.program_id(0) == 0) & (i == 0))(start)
        row = lax.broadcasted_iota(jnp.int32, (t, t), 0)
        col = lax.broadcasted_iota(jnp.int32, (t, t), 1)
        lo = lax.broadcasted_iota(jnp.int32, (t, LANES), 1) < HEAD_DIM
        qs = (qa_ref[0], qa_ref[1])

        def block(j, carry, masked):
            rows = pl.ds(pl.multiple_of(j * t, t), t)
            v = vb_ref[rows, :]
            new = []
            for hh in range(2):
                m, l, acc = carry[hh]
                s = _dot(qs[hh], ka_ref[hh, rows, :], NT)
                if masked:
                    s = jnp.where(row >= col, s, NEG)
                m_new = jnp.maximum(m, jnp.max(s, axis=1, keepdims=True))
                alpha = jnp.exp(m - m_new)
                p = jnp.exp(s - m_new)
                new.append((m_new, alpha * l + jnp.sum(p, axis=1, keepdims=True), alpha * acc + _dot(p.astype(BF16), v)))
            return tuple(new)

        init = (jnp.full((t, 1), NEG, F32), jnp.zeros((t, 1), F32), jnp.zeros((t, LANES), F32))
        carry = lax.fori_loop(0, i, functools.partial(block, masked=False), (init, init))
        (m0, l0, acc0), (m1, l1, acc1) = block(i, carry, True)
        o_ref[...] = jnp.where(lo, acc0 / l0, acc1 / l1)
        lse_ref[...] = jnp.where(lo, m0 + jnp.log(l0), m1 + jnp.log(l1))
        pl.when((pl.program_id(0) == N_PAIRS - 1) & (i == nq - 1))(finish)

    out = pl.BlockSpec((t, LANES), lambda p, i: (i, p))
    res = pl.pallas_call(
        body, name="attn_fwd", grid=(N_PAIRS, nq),
        in_specs=[pl.BlockSpec((2, t, LANES), lambda p, i: (p, i, 0)), pl.BlockSpec((2, s_len, LANES), lambda p, i: (p, 0, 0)),
                  pl.BlockSpec((s_len, LANES), lambda p, i: (0, p))] + [ANY] * n,
        out_specs=[out, out] + [ANY] * n,
        out_shape=[jax.ShapeDtypeStruct((s_len, D_MODEL), F32), jax.ShapeDtypeStruct((s_len, D_MODEL), F32)]
        + [jax.ShapeDtypeStruct((N_CHIPS, *h.shape), h.dtype) for h in shards],
        scratch_shapes=_exchange_sems(n),
        compiler_params=_params(("arbitrary", "arbitrary")),
    )(qa, ka, vb, *shards)
    return res[0], res[1], res[2:]


def _attn_bwd(qa, ka, vb, o, lse, do, parts, dproj):
    s_len = vb.shape[0]
    t = min(ATT_T, s_len)
    nq = s_len // t
    n = len(parts)

    def body(qa_ref, ka_ref, vb_ref, o_ref, lse_ref, do_ref, *rest):
        dqa_ref, dka_ref, dv_ref = rest[n + 1:n + 4]
        start, finish = _reduce_plan(rest[:n], rest[n + 4:2 * n + 4], *rest[2 * n + 4:])
        j = pl.program_id(1)
        pl.when((pl.program_id(0) == 0) & (j == 0))(start)

        @pl.when(j == 0)
        def _():
            dqa_ref[...] = jnp.zeros_like(dqa_ref)

        row = lax.broadcasted_iota(jnp.int32, (t, t), 0)
        col = lax.broadcasted_iota(jnp.int32, (t, t), 1)
        lo = lax.broadcasted_iota(jnp.int32, (t, LANES), 1) < HEAD_DIM
        v = vb_ref[...]
        ks = (ka_ref[0], ka_ref[1])

        def block(i, carry, masked):
            dk, dv = list(carry[:2]), carry[2]
            rows = pl.ds(pl.multiple_of(i * t, t), t)
            do_p, o_p, lse_p = do_ref[rows, :], o_ref[rows, :], lse_ref[rows, :]
            for hh in range(2):
                q = qa_ref[hh, rows, :]
                do_h = jnp.where(lo == (hh == 0), do_p, 0.0)
                delta = jnp.sum(do_h * o_p, axis=1, keepdims=True)
                s = _dot(q, ks[hh], NT)
                if masked:
                    s = jnp.where(row >= col, s, NEG)
                p = jnp.exp(s - lse_p[:, hh * HEAD_DIM:hh * HEAD_DIM + 1])
                do_b = do_h.astype(BF16)
                ds = (p * (_dot(do_b, v, NT) - delta)).astype(BF16)
                dv = dv + _dot(p.astype(BF16), do_b, TN)
                dk[hh] = dk[hh] + _dot(ds, q, TN)
                dqa_ref[hh, rows, :] += _dot(ds, ks[hh])
            return dk[0], dk[1], dv

        zero = jnp.zeros((t, LANES), F32)
        carry = block(j, (zero, zero, zero), True)
        dk0, dk1, dv = lax.fori_loop(j + 1, nq, functools.partial(block, masked=False), carry)
        dka_ref[0] = dk0
        dka_ref[1] = dk1
        dv_ref[...] = dv.astype(dv_ref.dtype)
        pl.when((pl.program_id(0) == N_PAIRS - 1) & (j == nq - 1))(finish)

    whole_pair = pl.BlockSpec((2, s_len, LANES), lambda p, j: (p, 0, 0))
    blk_pair = pl.BlockSpec((2, t, LANES), lambda p, j: (p, j, 0))
    whole_cols = pl.BlockSpec((s_len, LANES), lambda p, j: (0, p))
    blk_cols = pl.BlockSpec((t, LANES), lambda p, j: (j, p))
    res = pl.pallas_call(
        body, name="attn_bwd", grid=(N_PAIRS, nq),
        in_specs=[whole_pair, blk_pair, blk_cols, whole_cols, whole_cols, whole_cols] + [ANY] * (n + 1),
        out_specs=[whole_pair, blk_pair, pl.BlockSpec((t, LANES), lambda p, j: (j, AL_V // LANES + p))] + [ANY] * n,
        out_shape=[jax.ShapeDtypeStruct((N_HEADS, s_len, LANES), F32), jax.ShapeDtypeStruct((N_HEADS, s_len, LANES), F32),
                   jax.ShapeDtypeStruct(dproj.shape, dproj.dtype)]
        + [jax.ShapeDtypeStruct((N_DEV, g.shape[1] // 2, g.shape[2]), g.dtype) for g in parts],
        scratch_shapes=_exchange_sems(n), input_output_aliases={6 + n: 2},
        compiler_params=_params(("arbitrary", "arbitrary")),
    )(qa, ka, vb, o, lse, do, *parts, dproj)
    return res[0], res[1], res[2], res[3:]


def _attn_post(dqa, dka, dproj):
    s_len = dqa.shape[1]
    tr = min(256, s_len)
    assert AL_K == AL_Q + D_MODEL and AL_Q % (2 * D_MODEL) == 0

    def body(dqa_ref, dka_ref, _, dqk_ref, dcum_ref):
        lane = lax.broadcasted_iota(jnp.int32, (tr, LANES), 1)
        lo = lane < HEAD_DIM
        dcum = jnp.zeros((tr, LANES), F32)
        for p in range(N_PAIRS):
            a0, a1, b0, b1 = dqa_ref[2 * p], dqa_ref[2 * p + 1], dka_ref[2 * p], dka_ref[2 * p + 1]
            dq = jnp.where(lo, a0, a1) * (HEAD_DIM ** -0.5)
            dqk_ref[:, p * LANES:(p + 1) * LANES] = dq.astype(dqk_ref.dtype)
            dqk_ref[:, D_MODEL + p * LANES:D_MODEL + (p + 1) * LANES] = jnp.where(lo, b0, b1).astype(dqk_ref.dtype)
            for hh, (a, b) in enumerate(((a0, b0), (a1, b1))):
                base = _aug_base(hh)
                dcum = dcum + jnp.where(lane == 2 * p + hh, a[:, base:base + 1] - b[:, base + 3:base + 4], 0.0)
        dcum_ref[...] = dcum

    heads = pl.BlockSpec((N_HEADS, tr, LANES), lambda i: (0, i, 0))
    return pl.pallas_call(
        body, name="attn_post", grid=(s_len // tr,),
        in_specs=[heads, heads, ANY],
        out_specs=[pl.BlockSpec((tr, 2 * D_MODEL), lambda i: (i, AL_Q // (2 * D_MODEL))), pl.BlockSpec((tr, LANES), lambda i: (i, 0))],
        out_shape=[jax.ShapeDtypeStruct(dproj.shape, dproj.dtype), jax.ShapeDtypeStruct((s_len, LANES), F32)],
        input_output_aliases={2: 0}, compiler_params=_params(("parallel",)),
    )(dqa, dka, dproj)


def _ln_stats(r):
    mu = _rowmean(r)
    xc = r - mu
    rstd = lax.rsqrt(_rowmean(xc * xc) + LN_EPS)
    return xc * rstd, rstd


def _ln_bwd(dxh, xh, rstd):
    return rstd * (dxh - _rowmean(dxh) - xh * _rowmean(dxh * xh))


def _rms_bwd(dgn, g, r):
    return r * dgn - (r * r * r) * g * _rowmean(dgn * g)


def _to_aligned(wt):
    out = jnp.zeros((AL_COLS, wt.shape[1]), wt.dtype)
    for dst, (lo, hi) in ((0, (0, 2048)), (AL_Q, (2576, 5648)), (AL_B, (2048, 2560)), (AL_DTF, (2560, 2576)), (AL_DTF + 16, (5648, 5664))):
        out = lax.dynamic_update_slice_in_dim(out, wt[lo:hi], dst, axis=0)
    return out


def _from_aligned(gt):
    out = jnp.zeros((IN_COLS, gt.shape[1]), gt.dtype)
    for dst, (lo, hi) in ((0, (0, AL_Q)), (2048, (AL_B, AL_DTF)), (2560, (AL_DTF, AL_DTF + 16)), (2576, (AL_Q, AL_B)),
                          (5648, (AL_DTF + 16, AL_DTF + 32))):
        out = lax.dynamic_update_slice_in_dim(out, gt[lo:hi], dst, axis=0)
    return out


def _lanes(v, at=0):
    return jnp.pad(v, ((0, 0), (at, LANES - at - v.shape[1])))


def _local_step(x, tgt, mod, w_alt, shards, sp):
    d = D_MODEL
    sh1, sc1, g1, sh2, sc2, g2 = [mod[:, i * d:(i + 1) * d] for i in range(6)]
    dt_bias_l, a_log_l, f_bias_l = _lanes(sp["dt_bias"]), _lanes(sp["a_log"]), _lanes(sp["f_bias"], F_LANE)
    d_exp = jnp.repeat(sp["d_skip"], HEAD_DIM, axis=1)

    (h1,), _ = _rowwise("mod1", lambda x, sc, sh: ([x * (1.0 + sc) + sh], []), [x], [sc1, sh1], [(d, BF16)], [])
    proj = _matmul("proj", h1, w_alt, dims=NT, tn=1152)
    xc_all, y_ssd, prevs, y_mix = _ssd_fwd(proj, sp["conv_w"], sp["conv_b"], dt_bias_l, a_log_l, d_exp, sp["ssm_norm_w"])
    qa, ka, vb = _attn_prep(proj, f_bias_l)
    o, lse, (g_out, g_fi, g_fo) = _attn_fwd(qa, ka, vb, shards)
    w_out = g_out.reshape(2 * d, d)
    w_fi = g_fi.transpose(1, 0, 2).reshape(d, D_FF)
    w_fo = g_fo.reshape(D_FF, d)
    (y_mix,), _ = _rowwise("attn_norm", lambda o, w: ([o * lax.rsqrt(_rowmean(o * o) + RMS_EPS) * w], []),
                           [o], [sp["attn_norm_w"]], [(d, BF16, (2 * d, 1, y_mix))], [])
    def ln1_fwd(y, x, g1, sc2, sh2, lg, lb):
        r1 = ALPHA * x + (1.0 + g1) * y
        xh, _ = _ln_stats(r1)
        x1 = xh * lg + lb
        h2 = x1 * (1.0 + sc2) + sh2
        return [y, r1, h2, h2.T], []

    def relu2(u):
        a = jnp.square(jnp.maximum(u, 0.0))
        return [a, a.T], []

    y, r1, h2, h2_t = _matmul("out_proj", y_mix, w_out, tm=512, tk=2048,
                              epi=(ln1_fwd, [x], [g1, sc2, sh2, sp["ln1_g"], sp["ln1_b"]], [F32, F32, BF16, ("T", BF16)], []))
    act, act_t = _matmul("ff_in", h2, w_fi, epi=(relu2, [], [], [BF16, ("T", BF16)], []))

    def head(ff, r1, tgt, g2, l1g, l1b, l2g, l2b):
        xh1, _ = _ln_stats(r1)
        x1 = xh1 * l1g + l1b
        xh2, rstd2 = _ln_stats(ALPHA * x1 + (1.0 + g2) * ff)
        err = xh2 * l2g + l2b - tgt
        loss = 0.5 * jnp.sum(_rowmean(err * err))
        dx2 = err * (1.0 / d)
        dr2 = _ln_bwd(dx2 * l2g, xh2, rstd2)
        return ([dr2, (1.0 + g2) * dr2],
                [_colsum(dx2 * xh2), _colsum(dx2), _colsum(dr2 * ff), jnp.full((1, LANES), loss, F32)])

    dr2, dff, d_ln2_g, d_ln2_b, d_g2, loss = _matmul(
        "ff_out", act, w_fo, tm=512, tk=D_FF,
        epi=(head, [r1, tgt], [g2, sp["ln1_g"], sp["ln1_b"], sp["ln2_g"], sp["ln2_b"]], [F32, BF16], [d, d, d, LANES]))
    du = _matmul("d_act", dff, w_fo, dims=NT, epi=(lambda da, act: ([da * (2.0 * jnp.sqrt(act.astype(F32)))], []), [act], [], [BF16], []))
    dw_fo = _matmul("dw_ff_out", act_t, dff, tk=SEQ_TK, out_dtype=BF16, by_chip="rows")
    dw_fi = _matmul("dw_ff_in", h2_t, du, tk=SEQ_TK, out_dtype=BF16, by_chip="cols")

    def ln1_bwd(dh2, r1, dr2, y, sc2, g1, lg, lb):
        xh, rstd = _ln_stats(r1)
        x1 = xh * lg + lb
        dx1 = ALPHA * dr2 + dh2 * (1.0 + sc2)
        dr1 = _ln_bwd(dx1 * lg, xh, rstd)
        return ([dr1, (1.0 + g1) * dr1],
                [_colsum(dh2 * x1), _colsum(dh2), _colsum(dx1 * xh), _colsum(dx1), _colsum(dr1 * y)])

    dr1, dy, d_sc2, d_sh2, d_ln1_g, d_ln1_b, d_g1 = _matmul(
        "dh2", du, w_fi, dims=NT, tm=512, tk=D_FF,
        epi=(ln1_bwd, [r1, dr2, y], [sc2, g1, sp["ln1_g"], sp["ln1_b"]], [F32, BF16], [d] * 5))
    dw_out = _matmul("dw_out", y_mix, dy, dims=TN, tk=SEQ_TK, out_dtype=BF16, by_chip="rows")

    def attn_norm_bwd(dyo, o, w):
        r = lax.rsqrt(_rowmean(o * o) + RMS_EPS)
        return [_rms_bwd(dyo * w, o, r)], [_colsum(dyo * o * r)]

    dymix = _matmul("dy_mix_ssm", dy, w_out[:d], dims=NT)
    do, d_attn_w = _matmul("dy_mix_att", dy, w_out[d:], dims=NT, tm=512, epi=(attn_norm_bwd, [o], [sp["attn_norm_w"]], [F32], [d]))

    dxc, ddt_tile, d_alog_l, d_dexp, d_dtb_l, dproj, d_ssm_w = _ssd_bwd(
        xc_all, proj, dt_bias_l, a_log_l, d_exp, prevs, y_ssd, dymix, sp["ssm_norm_w"])
    dqa, dka, dproj, landed = _attn_bwd(qa, ka, vb, o, lse, do, [dw_out, dw_fi, dw_fo], dproj)
    dproj, dcum = _attn_post(dqa, dka, dproj)
    dproj, d_fb_l = _fox_cum_bwd(dcum, proj, f_bias_l, ddt_tile, dproj)
    dproj, d_conv_w, d_conv_b = _conv_bwd(proj, sp["conv_w"], sp["conv_b"], dxc, dproj)
    dw_alt = _matmul("dw_in", dproj, h1, dims=TN, tm=1152, tk=SEQ_TK, out_dtype=BF16)
    part_in = _from_aligned(dw_alt).reshape(N_CHIPS, IN_COLS // N_CHIPS, d)

    def last(dh1, x, dr1, sc1):
        return [ALPHA * dr1 + dh1 * (1.0 + sc1)], [_colsum(dh1 * x), _colsum(dh1)]

    chip_in = _pair_sum(part_in, _pair_exchange(part_in), lax.axis_index("c"))
    dx, d_sc1, d_sh1, landed_in = _matmul("dh1", dproj, w_alt, tm=512, tk=AL_COLS, carry=[chip_in],
                                          epi=(last, [x, dr1], [sc1], [F32], [d, d]))

    small = {
        "mod": jnp.concatenate([d_sh1, d_sc1, d_g1, d_sh2, d_sc2, d_g2], axis=1),
        "conv_w": d_conv_w, "conv_b": d_conv_b,
        "dt_bias": d_dtb_l[:, :N_HEADS], "a_log": d_alog_l[:, :N_HEADS],
        "d_skip": jnp.sum(d_dexp.reshape(N_HEADS, HEAD_DIM), axis=1)[None, :],
        "ssm_norm_w": d_ssm_w, "f_bias": d_fb_l[:, F_LANE:F_LANE + N_HEADS], "attn_norm_w": d_attn_w,
        "ln1_g": d_ln1_g, "ln1_b": d_ln1_b, "ln2_g": d_ln2_g, "ln2_b": d_ln2_b, "loss": loss,
    }
    return dx, [landed_in, *landed], small


N_DEV = 8
N_CHIPS = 4
ANY = pl.BlockSpec(memory_space=pl.ANY)
VMEM_SPEC = pl.BlockSpec(memory_space=pltpu.VMEM)


def _place():
    x, y, c = lax.axis_index("x"), lax.axis_index("y"), lax.axis_index("c")
    return x, y, c


def _other_chips(x, y):
    return [(1 - x, y, 2 * (1 - x) + y), (x, 1 - y, 2 * x + 1 - y), (1 - x, 1 - y, 2 * (1 - x) + 1 - y)]


def _small_gather(v_ref, out_ref, send_sems, recv_sems, local_sem, after_start=None):
    x, y, c = _place()
    me = 4 * x + 2 * y + c
    mine = pltpu.make_async_copy(v_ref, out_ref.at[me], local_sem)
    mine.start()
    peers = _peers(x, y, c)

    def copy(rel, slot, to):
        return pltpu.make_async_remote_copy(src_ref=v_ref, dst_ref=out_ref.at[slot], send_sem=send_sems.at[rel],
                                            recv_sem=recv_sems.at[rel], device_id=to, device_id_type=MESH)

    sends = [copy(rel, me, peer) for rel, peer in enumerate(peers)]
    for cp in sends:
        cp.start()
    if after_start is not None:
        after_start()
    for rel, (px, py, pc) in enumerate(peers):
        copy(rel, 4 * px + 2 * py + pc, (x, y, c)).wait_recv()
    for cp in sends:
        cp.wait_send()
    mine.wait()


SMALL_GATHER_SEMS = [pltpu.SemaphoreType.DMA((N_DEV - 1,)), pltpu.SemaphoreType.DMA((N_DEV - 1,)), pltpu.SemaphoreType.DMA]


def _allgather_small(name, v):
    def body(v_ref, out_ref, *sems):
        _small_gather(v_ref, out_ref, *sems)

    return pl.pallas_call(
        body, name=name, out_shape=jax.ShapeDtypeStruct((N_DEV, *v.shape), v.dtype),
        in_specs=[VMEM_SPEC], out_specs=VMEM_SPEC, scratch_shapes=SMALL_GATHER_SEMS,
    )(v)


def _tail_exchange(vec, halves):
    n = len(halves)

    def body(v_ref, *rest):
        ins, (every_ref, total_ref), outs = rest[:n], rest[n:n + 2], rest[n + 2:2 * n + 2]
        gather_sems, (swap_send, swap_recv) = rest[2 * n + 2:2 * n + 5], rest[2 * n + 5:]
        x, y, c = _place()
        swaps = [pltpu.make_async_remote_copy(src_ref=ins[w], dst_ref=outs[w], send_sem=swap_send.at[w], recv_sem=swap_recv.at[w],
                                              device_id=(x, y, 1 - c), device_id_type=MESH) for w in range(n)]

        def start_swaps():
            for cp in swaps:
                cp.start()

        _small_gather(v_ref, every_ref, *gather_sems, after_start=start_swaps)
        acc = every_ref[0]
        for dev in range(1, N_DEV):
            acc = acc + every_ref[dev]
        total_ref[...] = acc
        for cp in swaps:
            cp.wait_recv()
        for cp in swaps:
            cp.wait_send()

    res = pl.pallas_call(
        body, name="tail_exchange",
        out_shape=[jax.ShapeDtypeStruct((N_DEV, *vec.shape), vec.dtype), jax.ShapeDtypeStruct(vec.shape, vec.dtype)]
        + [jax.ShapeDtypeStruct(h.shape, h.dtype) for h in halves],
        in_specs=[VMEM_SPEC] + [ANY] * n, out_specs=[VMEM_SPEC, VMEM_SPEC] + [ANY] * n,
        scratch_shapes=SMALL_GATHER_SEMS + [pltpu.SemaphoreType.DMA((n,)), pltpu.SemaphoreType.DMA((n,))],
    )(vec, *halves)
    return res[0], res[1], res[2:]


def _gather_shards(shard):
    def body(in_ref, out_ref, stage, send_sems, recv_sems, local_sems):
        start, finish = _shard_gather_plan(in_ref, out_ref, stage, send_sems, recv_sems, local_sems)
        start()
        finish()

    return pl.pallas_call(
        body, name="gather_w_in", out_shape=jax.ShapeDtypeStruct((N_CHIPS, *shard.shape), shard.dtype),
        in_specs=[ANY], out_specs=ANY,
        scratch_shapes=[pltpu.VMEM(shard.shape, shard.dtype), pltpu.SemaphoreType.DMA((6,)), pltpu.SemaphoreType.DMA((6,)),
                        pltpu.SemaphoreType.DMA((2,))],
        compiler_params=_params(),
    )(shard)


def _shard_gather_plan(in_ref, out_ref, stage, send_sems, recv_sems, local_sems):
    ch = in_ref.shape[1] // 2
    x, y, c = _place()
    k_me = 2 * x + y
    me, sibling = (x, y, c), (x, y, 1 - c)
    chips = _other_chips(x, y)

    def copy(idx, k, half, to, src=None):
        cols = out_ref.at[k, :, pl.ds(pl.multiple_of(half * ch, ch), ch)]
        return pltpu.make_async_remote_copy(src_ref=cols if src is None else src, dst_ref=cols, send_sem=send_sems.at[idx],
                                            recv_sem=recv_sems.at[idx], device_id=to, device_id_type=MESH)

    mine = in_ref.at[:, pl.ds(pl.multiple_of(c * ch, ch), ch)]
    sends = [copy(j, k_me, c, (cx, cy, c), src=mine) for j, (cx, cy, _) in enumerate(chips)]
    load = pltpu.make_async_copy(in_ref, stage, local_sems.at[0])
    store = pltpu.make_async_copy(stage, out_ref.at[k_me], local_sems.at[1])

    def start():
        for cp in sends:
            cp.start()
        load.start()

    def finish():
        load.wait()
        store.start()
        forwards = []
        for j, (_, _, kj) in enumerate(chips):
            copy(j, kj, c, me).wait_recv()
            forwards.append(copy(3 + j, kj, c, sibling))
            forwards[-1].start()
        for j, (_, _, kj) in enumerate(chips):
            copy(3 + j, kj, 1 - c, me).wait_recv()
        for cp in sends + forwards:
            cp.wait_send()
        store.wait()

    return start, finish


def _peers(x, y, c):
    return [((1 - x) if rel & 4 else x, (1 - y) if rel & 2 else y, (1 - c) if rel & 1 else c) for rel in range(1, N_DEV)]


def _exchange_sems(n):
    return [pltpu.SemaphoreType.DMA((n, N_DEV - 1)), pltpu.SemaphoreType.DMA((n, N_DEV - 1)), pltpu.SemaphoreType.DMA((n,))]


def _gather_plan(ins, outs, send_sems, recv_sems, local_sems):
    x, y, c = _place()
    k_me = 2 * x + y
    peers = [(rel, p) for rel, p in enumerate(_peers(x, y, c)) if (rel + 1) & 6]

    def copy(w, rel, k, half, to, src=None):
        rh = ins[w].shape[0] // 2
        rows = outs[w].at[k, pl.ds(pl.multiple_of(half * rh, rh), rh), :]
        return pltpu.make_async_remote_copy(src_ref=rows if src is None else src, dst_ref=rows, send_sem=send_sems.at[w, rel],
                                            recv_sem=recv_sems.at[w, rel], device_id=to, device_id_type=MESH)

    def mine(w):
        rh = ins[w].shape[0] // 2
        return ins[w].at[pl.ds(pl.multiple_of(c * rh, rh), rh), :]

    n = len(ins)
    local = [pltpu.make_async_copy(ins[w], outs[w].at[k_me], local_sems.at[w]) for w in range(n)]
    sends = [copy(w, rel, k_me, c, peer, src=mine(w)) for w in range(n) for rel, peer in peers]

    def start():
        for cp in local + sends:
            cp.start()

    def finish():
        for w in range(n):
            for rel, (px, py, pc) in peers:
                copy(w, rel, 2 * px + py, pc, (x, y, c)).wait_recv()
        for cp in sends:
            cp.wait_send()
        for cp in local:
            cp.wait()

    return start, finish


def _reduce_plan(ins, outs, send_sems, recv_sems, local_sems):
    x, y, c = _place()
    me = 4 * x + 2 * y + c
    peers = _peers(x, y, c)

    def block(w, k, half):
        rh = ins[w].shape[1] // 2
        return ins[w].at[k, pl.ds(pl.multiple_of(half * rh, rh), rh), :]

    def copy(w, rel, src, slot, to):
        return pltpu.make_async_remote_copy(src_ref=src, dst_ref=outs[w].at[slot], send_sem=send_sems.at[w, rel],
                                            recv_sem=recv_sems.at[w, rel], device_id=to, device_id_type=MESH)

    n = len(ins)
    local = [pltpu.make_async_copy(block(w, 2 * x + y, c), outs[w].at[me], local_sems.at[w]) for w in range(n)]
    sends = [copy(w, rel, block(w, 2 * px + py, pc), me, (px, py, pc)) for w in range(n) for rel, (px, py, pc) in enumerate(peers)]

    def start():
        for cp in local + sends:
            cp.start()

    def finish():
        for w in range(n):
            for rel, (px, py, pc) in enumerate(peers):
                copy(w, rel, block(w, 2 * x + y, c), 4 * px + 2 * py + pc, (x, y, c)).wait_recv()
        for cp in sends:
            cp.wait_send()
        for cp in local:
            cp.wait()

    return start, finish


def _scatter_plan(ins, outs, send_sems, recv_sems, local_sems):
    x, y, c = _place()
    k_me = 2 * x + y
    chips = _other_chips(x, y)

    def copy(w, j, src_k, dst_k, to):
        return pltpu.make_async_remote_copy(src_ref=ins[w].at[src_k], dst_ref=outs[w].at[dst_k], send_sem=send_sems.at[w, j],
                                            recv_sem=recv_sems.at[w, j], device_id=to, device_id_type=MESH)

    n = len(ins)
    local = [pltpu.make_async_copy(ins[w].at[k_me], outs[w].at[k_me], local_sems.at[w]) for w in range(n)]
    sends = [copy(w, j, kj, k_me, (cx, cy, c)) for w in range(n) for j, (cx, cy, kj) in enumerate(chips)]

    def start():
        for cp in local + sends:
            cp.start()

    def finish():
        for w in range(n):
            for j, (_, _, kj) in enumerate(chips):
                copy(w, j, k_me, kj, (x, y, c)).wait_recv()
        for cp in sends:
            cp.wait_send()
        for cp in local:
            cp.wait()

    return start, finish


def _row_tile(r, mult=2 * SUBLANES):
    if r % 256 == 0:
        return 256
    return max([t for t in range(mult, 513, mult) if r % t == 0], default=r)


def _pair_exchange(g):
    _, r, cdim = g.shape
    ch = cdim // 2

    def body(g_ref, got_ref, send_sem, recv_sem):
        x, y, c = _place()
        cp = pltpu.make_async_remote_copy(src_ref=g_ref.at[:, :, pl.ds(pl.multiple_of((1 - c) * ch, ch), ch)], dst_ref=got_ref,
                                          send_sem=send_sem, recv_sem=recv_sem, device_id=(x, y, 1 - c), device_id_type=MESH)
        cp.start()
        cp.wait_recv()
        cp.wait_send()

    return pl.pallas_call(
        body, name="pair_exchange", out_shape=jax.ShapeDtypeStruct((N_CHIPS, r, ch), g.dtype),
        in_specs=[ANY], out_specs=ANY, scratch_shapes=[pltpu.SemaphoreType.DMA, pltpu.SemaphoreType.DMA],
    )(g)


def _pair_sum(g, got, c):
    _, r, cdim = g.shape
    ch = cdim // 2
    tr = _row_tile(r)

    def body(c_ref, g_ref, got_ref, o_ref):
        o_ref[...] = (g_ref[...].astype(F32) + got_ref[...].astype(F32)).astype(o_ref.dtype)

    blk = pl.BlockSpec((1, tr, ch), lambda k, i, c_ref: (k, i, 0))
    return pl.pallas_call(
        body, name="pair_sum",
        grid_spec=pltpu.PrefetchScalarGridSpec(
            num_scalar_prefetch=1, grid=(N_CHIPS, r // tr),
            in_specs=[pl.BlockSpec((1, tr, ch), lambda k, i, c_ref: (k, i, c_ref[0])), blk], out_specs=blk),
        out_shape=jax.ShapeDtypeStruct((N_CHIPS, r, ch), BF16),
        compiler_params=_params(("parallel", "parallel")),
    )(jnp.reshape(c, (1,)).astype(jnp.int32), g, got)


def _sum_blocks(name, parts):
    k, r, cdim = parts.shape
    tr = _row_tile(r)

    def body(p_ref, o_ref):
        acc = p_ref[0].astype(F32)
        for i in range(1, k):
            acc = acc + p_ref[i].astype(F32)
        o_ref[...] = acc

    return pl.pallas_call(
        body, name=name, grid=(r // tr,),
        in_specs=[pl.BlockSpec((k, tr, cdim), lambda i: (0, i, 0))], out_specs=pl.BlockSpec((tr, cdim), lambda i: (i, 0)),
        out_shape=jax.ShapeDtypeStruct((r, cdim), F32), compiler_params=_params(("parallel",)),
    )(parts)


ADA_SHARD = 6 * D_MODEL // N_CHIPS


def _mod_part(c_all, w_shard, b_shard):
    tn = 512

    def body(c_ref, w_ref, b_ref, o_ref):
        o_ref[...] = _dot(_silu(c_ref[...]).astype(BF16), w_ref[...].astype(BF16)) + b_ref[...]

    return pl.pallas_call(
        body, name="mod_part", grid=(ADA_SHARD // tn,),
        in_specs=[pl.BlockSpec((N_DEV, D_MODEL), lambda j: (0, 0)), pl.BlockSpec((D_MODEL, tn), lambda j: (0, j)),
                  pl.BlockSpec((1, tn), lambda j: (0, j))],
        out_specs=pl.BlockSpec((N_DEV, tn), lambda j: (0, j)),
        out_shape=jax.ShapeDtypeStruct((N_DEV, ADA_SHARD), F32), compiler_params=_params(("parallel",)),
    )(c_all, w_shard, b_shard)


def _w_ada_grad(c_all_t, dmod_shard):
    tm = 256

    def body(ct_ref, dm_ref, o_ref):
        act = _silu(ct_ref[...])
        acc = act[:, 0:1] * dm_ref[0:1, :]
        for dev in range(1, N_DEV):
            acc = acc + act[:, dev:dev + 1] * dm_ref[dev:dev + 1, :]
        o_ref[...] = acc

    return pl.pallas_call(
        body, name="w_ada_grad", grid=(D_MODEL // tm,),
        in_specs=[pl.BlockSpec((tm, N_DEV), lambda i: (i, 0)), pl.BlockSpec((N_DEV, ADA_SHARD), lambda i: (0, 0))],
        out_specs=pl.BlockSpec((tm, ADA_SHARD), lambda i: (i, 0)),
        out_shape=jax.ShapeDtypeStruct((D_MODEL, ADA_SHARD), F32), compiler_params=_params(("parallel",)),
    )(c_all_t, dmod_shard)


def _adamw_math(w, g, m, v):
    nm = ADAM_B1 * m + (1.0 - ADAM_B1) * g
    nv = ADAM_B2 * v + (1.0 - ADAM_B2) * jnp.square(g)
    m_hat = nm / (1.0 - ADAM_B1 ** ADAM_STEP)
    v_hat = nv / (1.0 - ADAM_B2 ** ADAM_STEP)
    return -ADAM_LR * (m_hat / (jnp.sqrt(v_hat) + ADAM_EPS) + ADAM_WD * w), nm, nv


def _adamw(name, w, g, m, v):
    _, r, cdim = w.shape
    tr = 256 if r % 256 == 0 else r

    def body(w_ref, g_ref, m_ref, v_ref, go_ref, d_ref, nm_ref, nv_ref):
        go_ref[...] = g_ref[...]
        d_ref[...], nm_ref[...], nv_ref[...] = _adamw_math(w_ref[...], g_ref[...], m_ref[...], v_ref[...])

    blk = pl.BlockSpec((None, tr, cdim), lambda i: (0, i, 0))
    return pl.pallas_call(
        body, name=name, grid=(r // tr,), in_specs=[blk, pl.BlockSpec((tr, cdim), lambda i: (i, 0)), blk, blk], out_specs=[blk] * 4,
        out_shape=[jax.ShapeDtypeStruct((1, r, cdim), F32)] * 4, compiler_params=_params(("parallel",)),
    )(w, g, m, v)


def _adamw_pair(name, w, mine, other, m, v, c, by_cols=False):
    _, r, cdim = w.shape
    hr, hc = mine.shape
    tr = _row_tile(hr, SUBLANES)
    per = hr // tr

    def body(c_ref, w_ref, a_ref, b_ref, m_ref, v_ref, g_ref, d_ref, nm_ref, nv_ref):
        half = pl.program_id(1) if by_cols else pl.program_id(0) // per
        g = jnp.where(half == c_ref[0], a_ref[...], b_ref[...])
        g_ref[...] = g
        d_ref[...], nm_ref[...], nv_ref[...] = _adamw_math(w_ref[...], g, m_ref[...], v_ref[...])

    blk = pl.BlockSpec((None, tr, hc), lambda i, j, c_ref: (0, i, j))
    half = pl.BlockSpec((tr, hc), lambda i, j, c_ref: (i % per, 0))
    return pl.pallas_call(
        body, name=name,
        grid_spec=pltpu.PrefetchScalarGridSpec(num_scalar_prefetch=1, grid=(r // tr, cdim // hc),
                                               in_specs=[blk, half, half, blk, blk], out_specs=[blk] * 4),
        out_shape=[jax.ShapeDtypeStruct((1, r, cdim), F32)] * 4, compiler_params=_params(("parallel", "parallel")),
    )(jnp.reshape(c, (1,)).astype(jnp.int32), w, mine, other, m, v)


SMALL = ["b_ada", "conv_b", "dt_bias", "a_log", "d_skip", "ssm_norm_w", "f_bias", "attn_norm_w", "ln1_g", "ln1_b", "ln2_g", "ln2_b"]


def _pack(vs):
    pieces = []
    for v in vs:
        pieces.append(v)
        if v.shape[1] % LANES:
            pieces.append(jnp.zeros((1, -v.shape[1] % LANES), v.dtype))
    return jnp.concatenate(pieces, axis=1)


def _adamw_small(total, offs, ws, ms, vs):
    n = len(ws)

    def body(*refs):
        t_ref, outs = refs[0], refs[1 + 3 * n:]
        for i in range(n):
            g = t_ref[:, offs[i]:offs[i] + ws[i].shape[1]]
            dl, nm, nv = _adamw_math(refs[1 + i][...], g, refs[1 + n + i][...], refs[1 + 2 * n + i][...])
            outs[4 * i][...], outs[4 * i + 1][...], outs[4 * i + 2][...], outs[4 * i + 3][...] = g, dl, nm, nv

    res = pl.pallas_call(
        body, name="adamw_small", in_specs=[VMEM_SPEC] * (1 + 3 * n), out_specs=[VMEM_SPEC] * (4 * n),
        out_shape=[jax.ShapeDtypeStruct(w.shape, F32) for w in ws for _ in range(4)],
    )(total, *ws, *ms, *vs)
    return [res[4 * i:4 * i + 4] for i in range(n)]


def kernel(x, c, w_ada, b_ada, w_in, conv_w, conv_b, dt_bias, a_log, d_skip, ssm_norm_w, f_bias, attn_norm_w, w_out, ln1_g, ln1_b, w_ff_in, w_ff_out, ln2_g, ln2_b, loss_target, m_w_ada, m_b_ada, m_w_in, m_conv_w, m_conv_b, m_dt_bias, m_a_log, m_d_skip, m_ssm_norm_w, m_f_bias, m_attn_norm_w, m_w_out, m_ln1_g, m_ln1_b, m_w_ff_in, m_w_ff_out, m_ln2_g, m_ln2_b, v_w_ada, v_b_ada, v_w_in, v_conv_w, v_conv_b, v_dt_bias, v_a_log, v_d_skip, v_ssm_norm_w, v_f_bias, v_attn_norm_w, v_w_out, v_ln1_g, v_ln1_b, v_w_ff_in, v_w_ff_out, v_ln2_g, v_ln2_b):
    a = dict(b_ada=b_ada, conv_b=conv_b, dt_bias=dt_bias, a_log=a_log, d_skip=d_skip, ssm_norm_w=ssm_norm_w, f_bias=f_bias,
             attn_norm_w=attn_norm_w, ln1_g=ln1_g, ln1_b=ln1_b, ln2_g=ln2_g, ln2_b=ln2_b)
    ms = dict(b_ada=m_b_ada, conv_b=m_conv_b, dt_bias=m_dt_bias, a_log=m_a_log, d_skip=m_d_skip, ssm_norm_w=m_ssm_norm_w,
              f_bias=m_f_bias, attn_norm_w=m_attn_norm_w, ln1_g=m_ln1_g, ln1_b=m_ln1_b, ln2_g=m_ln2_g, ln2_b=m_ln2_b)
    vs = dict(b_ada=v_b_ada, conv_b=v_conv_b, dt_bias=v_dt_bias, a_log=v_a_log, d_skip=v_d_skip, ssm_norm_w=v_ssm_norm_w,
              f_bias=v_f_bias, attn_norm_w=v_attn_norm_w, ln1_g=v_ln1_g, ln1_b=v_ln1_b, ln2_g=v_ln2_g, ln2_b=v_ln2_b)
    xi, yi, ci = _place()
    chip = 2 * xi + yi
    me = 4 * xi + 2 * yi + ci
    d = D_MODEL
    conv_shard = CONV_DIM // N_CHIPS

    first = _allgather_small("gather_c", jnp.concatenate([c, conv_w[0].reshape(1, CONV_W * conv_shard)], axis=1))[:, 0]
    c_all = first[:, :d]
    conv_w_full = first[::2, d:].reshape(N_CHIPS, CONV_W, conv_shard).transpose(1, 0, 2).reshape(CONV_W, CONV_DIM)
    b_shard = lax.dynamic_slice_in_dim(b_ada, chip * ADA_SHARD, ADA_SHARD, axis=1)
    parts = _allgather_small("gather_mod", _mod_part(c_all, w_ada[0], b_shard))
    mod = lax.dynamic_index_in_dim(parts[::2], me, axis=1, keepdims=False).reshape(1, 6 * d)

    w_in_t, m_w_in_t, v_w_in_t = [jnp.transpose(t, (0, 2, 1)) for t in (w_in, m_w_in, v_w_in)]
    w_alt = _to_aligned(_gather_shards(w_in_t[0].astype(BF16)).reshape(IN_COLS, d))

    sp = {n: a[n] for n in SMALL[1:]}
    sp["conv_w"] = conv_w_full
    shards = [w_out[0].astype(BF16), w_ff_in[0].astype(BF16), w_ff_out[0].astype(BF16)]
    dx, landed, small = _local_step(x[0], loss_target[0], mod, w_alt, shards, sp)

    mine = [_sum_blocks("dev_sum_%d" % i, p) for i, p in enumerate(landed)]
    names = ["mod"] + SMALL[1:]
    vec = _pack([small[n] for n in names] + [small["conv_w"].reshape(1, CONV_W * CONV_DIM), small["loss"]])
    every, total, other = _tail_exchange(vec, mine)
    widths = [6 * d] + [a[n].shape[1] for n in SMALL[1:]]
    offs = [0]
    for w in widths:
        offs.append(offs[-1] + w + (-w % LANES))
    g_conv_w_full = total[:, offs[-1]:offs[-1] + CONV_W * CONV_DIM].reshape(CONV_W, CONV_DIM)
    loss = total[0, offs[-1] + CONV_W * CONV_DIM]
    dmod_shard = lax.dynamic_slice_in_dim(every[:, 0, :6 * d], chip * ADA_SHARD, ADA_SHARD, axis=1)
    g_w_ada = _w_ada_grad(c_all.T, dmod_shard)
    g_conv_w = lax.dynamic_slice_in_dim(g_conv_w_full, chip * conv_shard, conv_shard, axis=1)

    grads, deltas, new_m, new_v = {}, {}, {}, {}
    paired = dict(w_in=(w_in_t, m_w_in_t, v_w_in_t), w_out=(w_out, m_w_out, v_w_out), w_ff_in=(w_ff_in, m_w_ff_in, v_w_ff_in),
                  w_ff_out=(w_ff_out, m_w_ff_out, v_w_ff_out))
    for i, (n, (w, m, v)) in enumerate(paired.items()):
        res = _adamw_pair("adamw_" + n, w, mine[i], other[i], m, v, ci, by_cols=n == "w_in")
        grads[n], deltas[n], new_m[n], new_v[n] = [jnp.transpose(t, (0, 2, 1)) for t in res] if n == "w_in" else res
    for n, g, (w, m, v) in (("w_ada", g_w_ada, (w_ada, m_w_ada, v_w_ada)), ("conv_w", g_conv_w, (conv_w, m_conv_w, v_conv_w))):
        grads[n], deltas[n], new_m[n], new_v[n] = _adamw("adamw_" + n, w, g, m, v)
    for n, res in zip(SMALL, _adamw_small(total, offs, [a[n] for n in SMALL], [ms[n] for n in SMALL], [vs[n] for n in SMALL])):
        grads[n], deltas[n], new_m[n], new_v[n] = res

    order = ["w_ada", "b_ada", "w_in", "conv_w", "conv_b", "dt_bias", "a_log", "d_skip", "ssm_norm_w", "f_bias", "attn_norm_w", "w_out",
             "ln1_g", "ln1_b", "w_ff_in", "w_ff_out", "ln2_g", "ln2_b"]
    return (loss, dx[None], *[grads[n] for n in order], *[deltas[n] for n in order], *[new_m[n] for n in order], *[new_v[n] for n in order])
```
